```python
import math
import jax, jax.numpy as jnp
from jax import lax
import numpy as np

D_MODEL = 1024
BATCH = 8
SEQ = 4096
DEPTH = 2

CHUNK = 64
Q_BLOCK = 128
SB_HEADS = 16
SB_HEAD_DIM = D_MODEL // SB_HEADS
SB_WIDTH = SB_HEADS * SB_HEAD_DIM
CONV_WIDTH = D_MODEL
CONV_K = 3
N_BRANCH = 2
FFN_HIDDEN = -(-8 * D_MODEL // (3 * 256)) * 256
IN_WIDTH = 3 * SB_WIDTH + 3 * CONV_WIDTH + N_BRANCH * D_MODEL
EPS = 1e-6

kernel_name = "stickbreak_shortconv_griffin_adaln_block"


def rmsnorm(x, g):
    xf = x.astype(jnp.float32)
    y = xf * lax.rsqrt(jnp.mean(xf * xf, axis=-1, keepdims=True) + EPS)
    return (y * g.astype(jnp.float32)).astype(x.dtype)


def stick_breaking_attention(q, k, v):
    b, s_len, h, dh = q.shape
    qh = jnp.transpose(q, (0, 2, 1, 3)).astype(jnp.float32)
    kh = jnp.transpose(k, (0, 2, 1, 3)).astype(jnp.float32)
    vh = jnp.transpose(v, (0, 2, 1, 3)).astype(jnp.float32)
    inv_sqrt = 1.0 / math.sqrt(dh)
    outs = []
    for i in range(s_len // Q_BLOCK):
        t0 = i * Q_BLOCK
        n_keys = t0 + Q_BLOCK
        qb = qh[:, :, t0:t0 + Q_BLOCK]
        kp = kh[:, :, :n_keys]
        vp = vh[:, :, :n_keys]
        z = jnp.einsum('bhqd,bhkd->bhqk', qb, kp) * inv_sqrt
        t_idx = t0 + jnp.arange(Q_BLOCK)[:, None]
        s_idx = jnp.arange(n_keys)[None, :]
        mask = s_idx < t_idx
        log_not = jnp.where(mask, jax.nn.log_sigmoid(-z), 0.0)
        excl = lax.cumsum(log_not, axis=3, reverse=True) - log_not
        log_a = jax.nn.log_sigmoid(z) + excl
        a = jnp.where(mask, jnp.exp(log_a), 0.0)
        outs.append(jnp.einsum('bhqk,bhkd->bhqd', a, vp))
    o = jnp.concatenate(outs, axis=2)
    o = jnp.transpose(o, (0, 2, 1, 3)).reshape(b, s_len, h * dh)
    return o.astype(q.dtype)


def causal_dwconv(x, w):
    ch = x.shape[-1]
    return lax.conv_general_dilated(
        x, w[:, None, :].astype(x.dtype), window_strides=(1,),
        padding=[(CONV_K - 1, 0)], dimension_numbers=('NWC', 'WIO', 'NWC'),
        feature_group_count=ch)


def _fwd_setup_inputs(seed: int = 0) -> dict:
    key = jax.random.key(seed)
    ks = jax.random.split(key, 18)
    f32 = jnp.float32

    def nrm(k, shape, fan_in):
        return jax.random.normal(k, shape, f32) * (fan_in ** -0.5)

    def gain(k, shape):
        return 1.0 + 0.02 * jax.random.normal(k, shape, f32)

    return {
        "x": jax.random.normal(ks[0], (BATCH, SEQ, D_MODEL), f32),
        "c": jax.random.normal(ks[1], (BATCH, D_MODEL), f32),
        "ada_w": nrm(ks[2], (DEPTH, D_MODEL, 6 * D_MODEL), D_MODEL),
        "ada_b": 0.02 * jax.random.normal(ks[3], (DEPTH, 6 * D_MODEL), f32),
        "ln1_g": gain(ks[4], (DEPTH, D_MODEL)),
        "w_in": nrm(ks[5], (DEPTH, D_MODEL, IN_WIDTH), D_MODEL),
        "q_norm_g": gain(ks[6], (DEPTH, SB_HEAD_DIM)),
        "k_norm_g": gain(ks[7], (DEPTH, SB_HEAD_DIM)),
        "conv_w": nrm(ks[8], (DEPTH, CONV_K, CONV_WIDTH), CONV_K),
        "w_branch_a": nrm(ks[9], (DEPTH, SB_WIDTH, D_MODEL), SB_WIDTH),
        "w_branch_b": nrm(ks[10], (DEPTH, CONV_WIDTH, D_MODEL), CONV_WIDTH),
        "w_out": nrm(ks[11], (DEPTH, D_MODEL, D_MODEL), D_MODEL),
        "ln2_g": gain(ks[12], (DEPTH, D_MODEL)),
        "w_ffn_gate": nrm(ks[13], (DEPTH, D_MODEL, FFN_HIDDEN), D_MODEL),
        "w_ffn_up": nrm(ks[14], (DEPTH, D_MODEL, FFN_HIDDEN), D_MODEL),
        "w_ffn_down": nrm(ks[15], (DEPTH, FFN_HIDDEN, D_MODEL), FFN_HIDDEN),
    }


def _fwd_reference(x, c, ada_w, ada_b, ln1_g, w_in, q_norm_g, k_norm_g, conv_w,
              w_branch_a, w_branch_b, w_out, ln2_g, w_ffn_gate, w_ffn_up, w_ffn_down):
    b, s_len, d = x.shape
    split_at = np.cumsum([SB_WIDTH, SB_WIDTH, SB_WIDTH,
                          CONV_WIDTH, CONV_WIDTH, CONV_WIDTH, D_MODEL])
    c_act = jax.nn.silu(c)
    for l in range(DEPTH):
        mod = c_act @ ada_w[l] + ada_b[l]
        sh1, sc1, g1, sh2, sc2, g2 = [m[:, None, :] for m in jnp.split(mod, 6, axis=-1)]

        h = rmsnorm(x, ln1_g[l]) * (1.0 + sc1) + sh1
        p = h @ w_in[l]
        q, k, v, cb, cc, cx, ga, gb = jnp.split(p, split_at, axis=-1)
        q = rmsnorm(q.reshape(b, s_len, SB_HEADS, SB_HEAD_DIM), q_norm_g[l])
        k = rmsnorm(k.reshape(b, s_len, SB_HEADS, SB_HEAD_DIM), k_norm_g[l])
        v = v.reshape(b, s_len, SB_HEADS, SB_HEAD_DIM)
        y_a = stick_breaking_attention(q, k, v)
        y_b = cb * causal_dwconv(cc * cx, conv_w[l])
        merged = (jax.nn.sigmoid(ga) * (y_a @ w_branch_a[l])
                  + jax.nn.sigmoid(gb) * (y_b @ w_branch_b[l]))
        x = x + g1 * (merged @ w_out[l])

        h = rmsnorm(x, ln2_g[l]) * (1.0 + sc2) + sh2
        f = (jax.nn.silu(h @ w_ffn_gate[l]) * (h @ w_ffn_up[l])) @ w_ffn_down[l]
        x = x + g2 * f
    return x


import jax as _jax
import jax.numpy as _jnp

TWIN_FORMAT = 'train_step'
FWD_PARAMS = ['x', 'c', 'ada_w', 'ada_b', 'ln1_g', 'w_in', 'q_norm_g', 'k_norm_g', 'conv_w', 'w_branch_a', 'w_branch_b', 'w_out', 'ln2_g', 'w_ffn_gate', 'w_ffn_up', 'w_ffn_down']
TWIN_WEIGHTS = ['ada_w', 'ada_b', 'ln1_g', 'w_in', 'q_norm_g', 'k_norm_g', 'conv_w', 'w_branch_a', 'w_branch_b', 'w_out', 'ln2_g', 'w_ffn_gate', 'w_ffn_up', 'w_ffn_down']
TWIN_DIFF_INPUT = 'x'
TWIN_INPUTS = ['x', 'c', 'ada_w', 'ada_b', 'ln1_g', 'w_in', 'q_norm_g', 'k_norm_g', 'conv_w', 'w_branch_a', 'w_branch_b', 'w_out', 'ln2_g', 'w_ffn_gate', 'w_ffn_up', 'w_ffn_down', 'loss_target', 'm_ada_w', 'm_ada_b', 'm_ln1_g', 'm_w_in', 'm_q_norm_g', 'm_k_norm_g', 'm_conv_w', 'm_w_branch_a', 'm_w_branch_b', 'm_w_out', 'm_ln2_g', 'm_w_ffn_gate', 'm_w_ffn_up', 'm_w_ffn_down', 'v_ada_w', 'v_ada_b', 'v_ln1_g', 'v_w_in', 'v_q_norm_g', 'v_k_norm_g', 'v_conv_w', 'v_w_branch_a', 'v_w_branch_b', 'v_w_out', 'v_ln2_g', 'v_w_ffn_gate', 'v_w_ffn_up', 'v_w_ffn_down']
TWIN_OUTPUTS = ['loss', 'grad_x', 'grad_ada_w', 'grad_ada_b', 'grad_ln1_g', 'grad_w_in', 'grad_q_norm_g', 'grad_k_norm_g', 'grad_conv_w', 'grad_w_branch_a', 'grad_w_branch_b', 'grad_w_out', 'grad_ln2_g', 'grad_w_ffn_gate', 'grad_w_ffn_up', 'grad_w_ffn_down', 'delta_ada_w', 'delta_ada_b', 'delta_ln1_g', 'delta_w_in', 'delta_q_norm_g', 'delta_k_norm_g', 'delta_conv_w', 'delta_w_branch_a', 'delta_w_branch_b', 'delta_w_out', 'delta_ln2_g', 'delta_w_ffn_gate', 'delta_w_ffn_up', 'delta_w_ffn_down', 'new_m_ada_w', 'new_m_ada_b', 'new_m_ln1_g', 'new_m_w_in', 'new_m_q_norm_g', 'new_m_k_norm_g', 'new_m_conv_w', 'new_m_w_branch_a', 'new_m_w_branch_b', 'new_m_w_out', 'new_m_ln2_g', 'new_m_w_ffn_gate', 'new_m_w_ffn_up', 'new_m_w_ffn_down', 'new_v_ada_w', 'new_v_ada_b', 'new_v_ln1_g', 'new_v_w_in', 'new_v_q_norm_g', 'new_v_k_norm_g', 'new_v_conv_w', 'new_v_w_branch_a', 'new_v_w_branch_b', 'new_v_w_out', 'new_v_ln2_g', 'new_v_w_ffn_gate', 'new_v_w_ffn_up', 'new_v_w_ffn_down']
TWIN_LEAF_KINDS = {'loss': 'loss', 'grad_x': 'grad_x', 'grad_ada_w': 'grad_w', 'grad_ada_b': 'grad_w', 'grad_ln1_g': 'grad_w', 'grad_w_in': 'grad_w', 'grad_q_norm_g': 'grad_w', 'grad_k_norm_g': 'grad_w', 'grad_conv_w': 'grad_w', 'grad_w_branch_a': 'grad_w', 'grad_w_branch_b': 'grad_w', 'grad_w_out': 'grad_w', 'grad_ln2_g': 'grad_w', 'grad_w_ffn_gate': 'grad_w', 'grad_w_ffn_up': 'grad_w', 'grad_w_ffn_down': 'grad_w', 'delta_ada_w': 'delta_w', 'delta_ada_b': 'delta_w', 'delta_ln1_g': 'delta_w', 'delta_w_in': 'delta_w', 'delta_q_norm_g': 'delta_w', 'delta_k_norm_g': 'delta_w', 'delta_conv_w': 'delta_w', 'delta_w_branch_a': 'delta_w', 'delta_w_branch_b': 'delta_w', 'delta_w_out': 'delta_w', 'delta_ln2_g': 'delta_w', 'delta_w_ffn_gate': 'delta_w', 'delta_w_ffn_up': 'delta_w', 'delta_w_ffn_down': 'delta_w', 'new_m_ada_w': 'new_m', 'new_m_ada_b': 'new_m', 'new_m_ln1_g': 'new_m', 'new_m_w_in': 'new_m', 'new_m_q_norm_g': 'new_m', 'new_m_k_norm_g': 'new_m', 'new_m_conv_w': 'new_m', 'new_m_w_branch_a': 'new_m', 'new_m_w_branch_b': 'new_m', 'new_m_w_out': 'new_m', 'new_m_ln2_g': 'new_m', 'new_m_w_ffn_gate': 'new_m', 'new_m_w_ffn_up': 'new_m', 'new_m_w_ffn_down': 'new_m', 'new_v_ada_w': 'new_v', 'new_v_ada_b': 'new_v', 'new_v_ln1_g': 'new_v', 'new_v_w_in': 'new_v', 'new_v_q_norm_g': 'new_v', 'new_v_k_norm_g': 'new_v', 'new_v_conv_w': 'new_v', 'new_v_w_branch_a': 'new_v', 'new_v_w_branch_b': 'new_v', 'new_v_w_out': 'new_v', 'new_v_ln2_g': 'new_v', 'new_v_w_ffn_gate': 'new_v', 'new_v_w_ffn_up': 'new_v', 'new_v_w_ffn_down': 'new_v'}


def _forward(args):
    return _fwd_reference(*[args[k] for k in FWD_PARAMS])


def _output_shape():
    def fwd():
        inp = _fwd_setup_inputs(0)
        return _fwd_reference(*[inp[k] for k in FWD_PARAMS])
    out = _jax.eval_shape(fwd)
    return out.shape, out.dtype

N_MICROBATCH = 1
ADAM_LR = 0.001
ADAM_B1 = 0.9
ADAM_B2 = 0.999
ADAM_EPS = 1e-08
ADAM_WD = 0.01
ADAM_STEP = 10
PER_EXAMPLE_BATCH_AXIS = {'x': 0, 'c': 0, 'loss_target': 0}
SHARED_INPUTS = []
_WEIGHT_DTYPES = {'ada_w': _jnp.float32, 'ada_b': _jnp.float32, 'ln1_g': _jnp.float32, 'w_in': _jnp.float32, 'q_norm_g': _jnp.float32, 'k_norm_g': _jnp.float32, 'conv_w': _jnp.float32, 'w_branch_a': _jnp.float32, 'w_branch_b': _jnp.float32, 'w_out': _jnp.float32, 'ln2_g': _jnp.float32, 'w_ffn_gate': _jnp.float32, 'w_ffn_up': _jnp.float32, 'w_ffn_down': _jnp.float32}
MOMENT_SCALE = {'ada_w': 8.199266e+00, 'ada_b': 2.122385e+01, 'ln1_g': 5.606662e+01, 'w_in': 1.710225e+00, 'q_norm_g': 4.215016e+00, 'k_norm_g': 4.223366e+00, 'conv_w': 1.159177e+01, 'w_branch_a': 1.984926e+00, 'w_branch_b': 1.247714e+00, 'w_out': 2.188212e+00, 'ln2_g': 2.719811e+01, 'w_ffn_gate': 1.333621e+00, 'w_ffn_up': 1.128941e+00, 'w_ffn_down': 1.608764e+00}


def _to_microbatches(a, axis):
    t = _jnp.moveaxis(a, axis, 0)
    t = t.reshape((N_MICROBATCH, t.shape[0] // N_MICROBATCH) + t.shape[1:])
    return _jnp.moveaxis(t, 1, axis + 1)


def setup_inputs(seed: int = 0) -> dict:
    inp = _fwd_setup_inputs(seed)
    key = _jax.random.fold_in(_jax.random.key(seed), 7919)
    shape, _ = _output_shape()
    out = dict(inp)
    out["loss_target"] = _jax.random.normal(_jax.random.fold_in(key, 0), shape, _jnp.float32)
    for i, name in enumerate(TWIN_WEIGHTS):
        w = inp[name].astype(_jnp.float32)
        if MOMENT_SCALE is None:
            s = _jnp.sqrt(_jnp.mean(_jnp.square(w)) + 1e-30)
        else:
            s = MOMENT_SCALE[name]
        km, kv = _jax.random.split(_jax.random.fold_in(key, i + 1))
        out[name] = w
        out["m_" + name] = s * _jax.random.normal(km, w.shape, _jnp.float32)
        out["v_" + name] = (s * s) * _jax.random.uniform(kv, w.shape, _jnp.float32, 0.5, 1.5)
    if N_MICROBATCH > 1:
        for name, axis in PER_EXAMPLE_BATCH_AXIS.items():
            out[name] = _to_microbatches(out[name], axis)
    return {'x': out['x'], 'c': out['c'], 'ada_w': out['ada_w'], 'ada_b': out['ada_b'], 'ln1_g': out['ln1_g'], 'w_in': out['w_in'], 'q_norm_g': out['q_norm_g'], 'k_norm_g': out['k_norm_g'], 'conv_w': out['conv_w'], 'w_branch_a': out['w_branch_a'], 'w_branch_b': out['w_branch_b'], 'w_out': out['w_out'], 'ln2_g': out['ln2_g'], 'w_ffn_gate': out['w_ffn_gate'], 'w_ffn_up': out['w_ffn_up'], 'w_ffn_down': out['w_ffn_down'], 'loss_target': out['loss_target'], 'm_ada_w': out['m_ada_w'], 'm_ada_b': out['m_ada_b'], 'm_ln1_g': out['m_ln1_g'], 'm_w_in': out['m_w_in'], 'm_q_norm_g': out['m_q_norm_g'], 'm_k_norm_g': out['m_k_norm_g'], 'm_conv_w': out['m_conv_w'], 'm_w_branch_a': out['m_w_branch_a'], 'm_w_branch_b': out['m_w_branch_b'], 'm_w_out': out['m_w_out'], 'm_ln2_g': out['m_ln2_g'], 'm_w_ffn_gate': out['m_w_ffn_gate'], 'm_w_ffn_up': out['m_w_ffn_up'], 'm_w_ffn_down': out['m_w_ffn_down'], 'v_ada_w': out['v_ada_w'], 'v_ada_b': out['v_ada_b'], 'v_ln1_g': out['v_ln1_g'], 'v_w_in': out['v_w_in'], 'v_q_norm_g': out['v_q_norm_g'], 'v_k_norm_g': out['v_k_norm_g'], 'v_conv_w': out['v_conv_w'], 'v_w_branch_a': out['v_w_branch_a'], 'v_w_branch_b': out['v_w_branch_b'], 'v_w_out': out['v_w_out'], 'v_ln2_g': out['v_ln2_g'], 'v_w_ffn_gate': out['v_w_ffn_gate'], 'v_w_ffn_up': out['v_w_ffn_up'], 'v_w_ffn_down': out['v_w_ffn_down']}


def _loss(weights, diff, rest, loss_target):
    with _jax.named_scope("forward"):
        args = {**rest, TWIN_DIFF_INPUT: diff, **{k: w.astype(_WEIGHT_DTYPES[k]) for k, w in weights.items()}}
        y = _forward(args)
    with _jax.named_scope("loss_head"):
        err = _jnp.square(y.astype(_jnp.float32) - loss_target)
        return 0.5 * _jnp.sum(_jnp.mean(err, axis=-1)) if err.ndim else 0.5 * err


def _adamw(w, g, m, v):
    m = ADAM_B1 * m + (1.0 - ADAM_B1) * g
    v = ADAM_B2 * v + (1.0 - ADAM_B2) * _jnp.square(g)
    m_hat = m / (1.0 - ADAM_B1 ** ADAM_STEP)
    v_hat = v / (1.0 - ADAM_B2 ** ADAM_STEP)
    delta = -ADAM_LR * (m_hat / (_jnp.sqrt(v_hat) + ADAM_EPS) + ADAM_WD * w)
    return delta, m, v


def reference(x, c, ada_w, ada_b, ln1_g, w_in, q_norm_g, k_norm_g, conv_w, w_branch_a, w_branch_b, w_out, ln2_g, w_ffn_gate, w_ffn_up, w_ffn_down, loss_target, m_ada_w, m_ada_b, m_ln1_g, m_w_in, m_q_norm_g, m_k_norm_g, m_conv_w, m_w_branch_a, m_w_branch_b, m_w_out, m_ln2_g, m_w_ffn_gate, m_w_ffn_up, m_w_ffn_down, v_ada_w, v_ada_b, v_ln1_g, v_w_in, v_q_norm_g, v_k_norm_g, v_conv_w, v_w_branch_a, v_w_branch_b, v_w_out, v_ln2_g, v_w_ffn_gate, v_w_ffn_up, v_w_ffn_down):
    given = dict(x=x, c=c, ada_w=ada_w, ada_b=ada_b, ln1_g=ln1_g, w_in=w_in, q_norm_g=q_norm_g, k_norm_g=k_norm_g, conv_w=conv_w, w_branch_a=w_branch_a, w_branch_b=w_branch_b, w_out=w_out, ln2_g=ln2_g, w_ffn_gate=w_ffn_gate, w_ffn_up=w_ffn_up, w_ffn_down=w_ffn_down, loss_target=loss_target, m_ada_w=m_ada_w, m_ada_b=m_ada_b, m_ln1_g=m_ln1_g, m_w_in=m_w_in, m_q_norm_g=m_q_norm_g, m_k_norm_g=m_k_norm_g, m_conv_w=m_conv_w, m_w_branch_a=m_w_branch_a, m_w_branch_b=m_w_branch_b, m_w_out=m_w_out, m_ln2_g=m_ln2_g, m_w_ffn_gate=m_w_ffn_gate, m_w_ffn_up=m_w_ffn_up, m_w_ffn_down=m_w_ffn_down, v_ada_w=v_ada_w, v_ada_b=v_ada_b, v_ln1_g=v_ln1_g, v_w_in=v_w_in, v_q_norm_g=v_q_norm_g, v_k_norm_g=v_k_norm_g, v_conv_w=v_conv_w, v_w_branch_a=v_w_branch_a, v_w_branch_b=v_w_branch_b, v_w_out=v_w_out, v_ln2_g=v_ln2_g, v_w_ffn_gate=v_w_ffn_gate, v_w_ffn_up=v_w_ffn_up, v_w_ffn_down=v_w_ffn_down)
    weights = {n: given[n] for n in TWIN_WEIGHTS}
    shared = {n: given[n] for n in SHARED_INPUTS}
    per_example = {n: given[n] for n in ['x', 'c']}
    grad_fn = _jax.value_and_grad(_loss, argnums=(0, 1))

    def one_microbatch(ex, loss_target):
        ex = dict(ex)
        diff = ex.pop(TWIN_DIFF_INPUT)
        return grad_fn(weights, diff, {**shared, **ex}, loss_target)

    if N_MICROBATCH == 1:
        loss, (grad_w, grad_x) = one_microbatch(per_example, given["loss_target"])
    else:
        def body(carry, xs):
            loss_sum, grad_sum = carry
            l_k, (gw_k, gx_k) = one_microbatch(xs[0], xs[1])
            with _jax.named_scope("update"):
                return (loss_sum + l_k, _jax.tree.map(_jnp.add, grad_sum, gw_k)), gx_k

        init = (_jnp.zeros((), _jnp.float32), _jax.tree.map(_jnp.zeros_like, weights))
        (loss, grad_w), grad_x = _jax.lax.scan(body, init, (per_example, given["loss_target"]))
    with _jax.named_scope("update"):
        delta_w, new_m, new_v = {}, {}, {}
        for n in TWIN_WEIGHTS:
            delta_w[n], new_m[n], new_v[n] = _adamw(weights[n], grad_w[n], given["m_" + n], given["v_" + n])
    return (loss, grad_x, *[grad_w[n] for n in TWIN_WEIGHTS], *[delta_w[n] for n in TWIN_WEIGHTS],
            *[new_m[n] for n in TWIN_WEIGHTS], *[new_v[n] for n in TWIN_WEIGHTS])
```

```python
import functools
import math

import jax
import jax.numpy as jnp
from jax import lax
from jax.experimental import pallas as pl
from jax.experimental.pallas import tpu as pltpu

F32 = jnp.float32
BF16 = jnp.bfloat16
MESH_ID = pl.DeviceIdType.MESH

EPS = 1e-6
HEAD_DIM = 64
Q_BLOCK = 128
LANES = 128
N_DEV = 8
N_CHIP = 4
VMEM_LIMIT_BYTES = 56 * 1024 * 1024

ADAM_LR = 0.001
ADAM_B1 = 0.9
ADAM_B2 = 0.999
ADAM_EPS = 1e-08
ADAM_WD = 0.01
ADAM_STEP = 10


def _params(*sem):
    return pltpu.CompilerParams(dimension_semantics=tuple(sem), vmem_limit_bytes=VMEM_LIMIT_BYTES)


def _tile(n, pref):
    return pref if n % pref == 0 else n


def _dot(a, b):
    return jnp.dot(a, b, preferred_element_type=F32)


def _dot_nt(a, b):
    return lax.dot_general(a, b, (((1,), (1,)), ((), ())), preferred_element_type=F32)


def _dot_tn(a, b):
    return lax.dot_general(a, b, (((0,), (0,)), ((), ())), preferred_element_type=F32)


def _softplus(z):
    return jnp.maximum(z, 0.0) + jnp.log(1.0 + jnp.exp(-jnp.abs(z)))


def _cumdot(v, tri):
    hi = v.astype(BF16)
    lo = (v - hi.astype(F32)).astype(BF16)
    return _dot(hi, tri) + _dot(lo, tri)


def _adamw(w, g, m, v):
    m = ADAM_B1 * m + (1.0 - ADAM_B1) * g
    v = ADAM_B2 * v + (1.0 - ADAM_B2) * (g * g)
    m_hat = m / (1.0 - ADAM_B1 ** ADAM_STEP)
    v_hat = v / (1.0 - ADAM_B2 ** ADAM_STEP)
    delta = -ADAM_LR * (m_hat / (jnp.sqrt(v_hat) + ADAM_EPS) + ADAM_WD * w)
    return delta, m, v


def _peer(x, y, c, k):
    return (1 - x if k & 4 else x, 1 - y if k & 2 else y, 1 - c if k & 1 else c)


def _gather8(v):
    rows_per, m = v.shape

    def body(v_ref, out_ref, send_sems, recv_sems, local_sem):
        x, y, c = lax.axis_index("x"), lax.axis_index("y"), lax.axis_index("c")

        def rows(p):
            return out_ref.at[pl.ds((4 * p[0] + 2 * p[1] + p[2]) * rows_per, rows_per), :]

        me = (x, y, c)
        mine = pltpu.make_async_copy(v_ref, rows(me), local_sem)
        mine.start()
        sends = []
        for k in range(1, N_DEV):
            cp = pltpu.make_async_remote_copy(
                src_ref=v_ref, dst_ref=rows(me), send_sem=send_sems.at[k - 1], recv_sem=recv_sems.at[k - 1],
                device_id=_peer(x, y, c, k), device_id_type=MESH_ID)
            cp.start()
            sends.append(cp)
        for k in range(1, N_DEV):
            pltpu.make_async_remote_copy(
                src_ref=v_ref, dst_ref=rows(_peer(x, y, c, k)), send_sem=send_sems.at[k - 1],
                recv_sem=recv_sems.at[k - 1], device_id=_peer(x, y, c, k), device_id_type=MESH_ID).wait_recv()
        for cp in sends:
            cp.wait_send()
        mine.wait()

    return pl.pallas_call(
        body, name="gather8",
        out_shape=jax.ShapeDtypeStruct((N_DEV * rows_per, m), v.dtype),
        in_specs=[pl.BlockSpec(memory_space=pltpu.VMEM)],
        out_specs=pl.BlockSpec(memory_space=pltpu.VMEM),
        scratch_shapes=[pltpu.SemaphoreType.DMA((N_DEV - 1,)), pltpu.SemaphoreType.DMA((N_DEV - 1,)),
                        pltpu.SemaphoreType.DMA],
    )(v)


def _gather_weights(shards):
    n = len(shards)

    def body(*refs):
        ins, outs = refs[:n], refs[n:2 * n]
        send_sems, recv_sems, local_sems = refs[2 * n:]
        x, y, c = lax.axis_index("x"), lax.axis_index("y"), lax.axis_index("c")
        chips = [(1 - x, y), (x, 1 - y), (1 - x, 1 - y)]
        mine = 2 * x + y
        local, sends = [], []
        for a in range(n):
            cp = pltpu.make_async_copy(ins[a], outs[a].at[mine], local_sems.at[a])
            cp.start()
            local.append(cp)
            for j, (px, py) in enumerate(chips):
                rc = pltpu.make_async_remote_copy(
                    src_ref=ins[a], dst_ref=outs[a].at[mine], send_sem=send_sems.at[a, j],
                    recv_sem=recv_sems.at[a, j], device_id=(px, py, c), device_id_type=MESH_ID)
                rc.start()
                sends.append(rc)
        for a in range(n):
            for j, (px, py) in enumerate(chips):
                pltpu.make_async_remote_copy(
                    src_ref=ins[a], dst_ref=outs[a].at[2 * px + py], send_sem=send_sems.at[a, j],
                    recv_sem=recv_sems.at[a, j], device_id=(px, py, c), device_id_type=MESH_ID).wait_recv()
        for rc in sends:
            rc.wait_send()
        for cp in local:
            cp.wait()

    any_spec = pl.BlockSpec(memory_space=pl.ANY)
    return pl.pallas_call(
        body, name="gather_weights",
        out_shape=[jax.ShapeDtypeStruct((N_CHIP,) + s.shape, s.dtype) for s in shards],
        in_specs=[any_spec] * n, out_specs=[any_spec] * n,
        scratch_shapes=[pltpu.SemaphoreType.DMA((n, 3)), pltpu.SemaphoreType.DMA((n, 3)),
                        pltpu.SemaphoreType.DMA((n,))],
    )(*shards)


def _exchange_grads(grads, n_layers):
    flat = [g for per_w in grads for g in per_w]
    n_w, n = len(grads), len(flat)

    def body(*refs):
        ins, outs = refs[:n], refs[n:n + n_w]
        send_sems, recv_sems, local_sems = refs[n + n_w:]
        x, y, c = lax.axis_index("x"), lax.axis_index("y"), lax.axis_index("c")
        chips = [(1 - x, y), (x, 1 - y), (1 - x, 1 - y)]
        mine = 2 * x + y
        sibling = (x, y, 1 - c)

        def slot(px, py, pc):
            return 4 * px + 2 * py + pc

        def copy(a, k, src, dst, to):
            return pltpu.make_async_remote_copy(src_ref=src, dst_ref=dst, send_sem=send_sems.at[a, k],
                                                recv_sem=recv_sems.at[a, k], device_id=to, device_id_type=MESH_ID)

        sends, locals_ = [], []
        for a in range(n):
            w, l = divmod(a, n_layers)
            out = outs[w]
            lc = pltpu.make_async_copy(ins[a].at[mine], out.at[l, slot(x, y, c)], local_sems.at[a])
            lc.start()
            locals_.append(lc)
            cp = copy(a, 0, ins[a].at[mine], out.at[l, slot(x, y, c)], sibling)
            cp.start()
            sends.append(cp)
            for j, (px, py) in enumerate(chips):
                cp = copy(a, 1 + j, ins[a].at[2 * px + py], out.at[l, slot(x, y, c)], (px, py, c))
                cp.start()
                sends.append(cp)
        for a in range(n):
            w, l = divmod(a, n_layers)
            out = outs[w]
            for j, (px, py) in enumerate(chips):
                landed = out.at[l, slot(px, py, c)]
                copy(a, 1 + j, landed, landed, (px, py, c)).wait_recv()
                cp = copy(a, 4 + j, landed, landed, sibling)
                cp.start()
                sends.append(cp)
        for a in range(n):
            w, l = divmod(a, n_layers)
            out = outs[w]
            theirs = out.at[l, slot(x, y, 1 - c)]
            copy(a, 0, theirs, theirs, sibling).wait_recv()
            for j, (px, py) in enumerate(chips):
                passed = out.at[l, slot(px, py, 1 - c)]
                copy(a, 4 + j, passed, passed, sibling).wait_recv()
        for cp in sends:
            cp.wait_send()
        for lc in locals_:
            lc.wait()

    any_spec = pl.BlockSpec(memory_space=pl.ANY)
    return pl.pallas_call(
        body, name="exchange_grads",
        out_shape=[jax.ShapeDtypeStruct((n_layers, N_DEV) + per_w[0].shape[1:], per_w[0].dtype) for per_w in grads],
        in_specs=[any_spec] * n, out_specs=[any_spec] * n_w,
        scratch_shapes=[pltpu.SemaphoreType.DMA((n, 7)), pltpu.SemaphoreType.DMA((n, 7)),
                        pltpu.SemaphoreType.DMA((n,))],
    )(*flat)


def _ada_mod(c_all, ada_w, ada_b_cols):
    n_l, d, a4 = ada_w.shape
    tn = _tile(a4, 512)

    def body(c_ref, w_ref, b_ref, o_ref):
        cv = c_ref[...]
        ca = (cv * jax.nn.sigmoid(cv)).astype(BF16)
        o_ref[...] = _dot(ca, w_ref[...].astype(BF16)) + b_ref[...]

    return pl.pallas_call(
        body, name="ada_mod", grid=(n_l, a4 // tn),
        in_specs=[pl.BlockSpec((N_DEV, d), lambda l, j: (0, 0)),
                  pl.BlockSpec((None, d, tn), lambda l, j: (l, 0, j)),
                  pl.BlockSpec((None, 1, tn), lambda l, j: (l, 0, j))],
        out_specs=pl.BlockSpec((None, N_DEV, tn), lambda l, j: (l, 0, j)),
        out_shape=jax.ShapeDtypeStruct((n_l, N_DEV, a4), F32),
        compiler_params=_params("parallel", "parallel"),
    )(c_all, ada_w, ada_b_cols)


def _ada_grad_adam(c_all_t, dmod_cols, w, m, v):
    n_l, d, a4 = w.shape
    tn = _tile(a4, 512)

    def body(ct_ref, dm_ref, w_ref, m_ref, v_ref, g_ref, dl_ref, nm_ref, nv_ref):
        ct = ct_ref[...]
        ca = ct * jax.nn.sigmoid(ct)
        dm = dm_ref[...]
        g = ca[:, 0:1] * dm[0:1, :]
        for dev in range(1, N_DEV):
            g = g + ca[:, dev:dev + 1] * dm[dev:dev + 1, :]
        g_ref[...] = g
        delta, nm, nv = _adamw(w_ref[...], g, m_ref[...], v_ref[...])
        dl_ref[...] = delta
        nm_ref[...] = nm
        nv_ref[...] = nv

    wspec = pl.BlockSpec((None, d, tn), lambda l, j: (l, 0, j))
    shp = jax.ShapeDtypeStruct(w.shape, F32)
    return pl.pallas_call(
        body, name="ada_grad_adam", grid=(n_l, a4 // tn),
        in_specs=[pl.BlockSpec((d, N_DEV), lambda l, j: (0, 0)),
                  pl.BlockSpec((None, N_DEV, tn), lambda l, j: (l, 0, j)), wspec, wspec, wspec],
        out_specs=[wspec] * 4, out_shape=[shp] * 4,
        compiler_params=_params("parallel", "parallel"),
    )(c_all_t, dmod_cols, w, m, v)


def _lnmod(x, g, sc, sh):
    s, d = x.shape
    tm = _tile(s, 512)

    def body(x_ref, g_ref, sc_ref, sh_ref, h_ref):
        xv = x_ref[...]
        r = lax.rsqrt(jnp.mean(xv * xv, axis=-1, keepdims=True) + EPS)
        h_ref[...] = ((xv * r * g_ref[...]) * (1.0 + sc_ref[...]) + sh_ref[...]).astype(BF16)

    vec = pl.BlockSpec((1, d), lambda i: (0, 0))
    row = pl.BlockSpec((tm, d), lambda i: (i, 0))
    return pl.pallas_call(
        body, name="lnmod", grid=(s // tm,), in_specs=[row, vec, vec, vec], out_specs=row,
        out_shape=jax.ShapeDtypeStruct((s, d), BF16), compiler_params=_params("parallel"),
    )(x, g, sc, sh)


def _mm_in(h, w_g, layer):
    s, d = h.shape
    n4 = w_g.shape[-1]
    tm = _tile(s, 512)

    def body(a_ref, b_ref, o_ref):
        o_ref[...] = _dot(a_ref[...], b_ref[...])

    return pl.pallas_call(
        body, name="mm_in", grid=(N_CHIP, s // tm),
        in_specs=[pl.BlockSpec((tm, d), lambda j, i: (i, 0)),
                  pl.BlockSpec((None, None, d, n4), lambda j, i: (j, layer, 0, 0))],
        out_specs=pl.BlockSpec((tm, n4), lambda j, i: (i, j)),
        out_shape=jax.ShapeDtypeStruct((s, N_CHIP * n4), F32),
        compiler_params=_params("parallel", "parallel"),
    )(h, w_g)


def _head_norm(x, gain, scale):
    r = lax.rsqrt(jnp.mean(x * x, axis=-1, keepdims=True) + EPS)
    return x * r * gain * scale


def _attn_fwd(p, qg, kg, d):
    s = p.shape[0]
    n_pairs = d // LANES
    n_qb = s // Q_BLOCK
    chunk = _tile(s, 512)
    inv_sqrt = 1.0 / math.sqrt(HEAD_DIM)

    def body(q_ref, k_ref, v_ref, qg_ref, kg_ref, o_ref, lt_ref, qs, ks, vs):
        row = lax.broadcasted_iota(jnp.int32, (Q_BLOCK, Q_BLOCK), 0)
        col = lax.broadcasted_iota(jnp.int32, (Q_BLOCK, Q_BLOCK), 1)
        later = (row > col).astype(BF16)
        causal = col < row

        for hh in range(2):
            lanes = slice(hh * HEAD_DIM, (hh + 1) * HEAD_DIM)

            def prep(r, _):
                rows = pl.ds(pl.multiple_of(r * chunk, chunk), chunk)
                qs[rows, :] = _head_norm(q_ref[rows, lanes], qg_ref[...], inv_sqrt).astype(BF16)
                ks[rows, :] = _head_norm(k_ref[rows, lanes], kg_ref[...], 1.0).astype(BF16)
                vs[rows, :] = v_ref[rows, lanes].astype(BF16)
                return 0

            lax.fori_loop(0, s // chunk, prep, 0)

            def q_block(i, _):
                rows_i = pl.ds(pl.multiple_of(i * Q_BLOCK, Q_BLOCK), Q_BLOCK)
                qi = qs[rows_i, :]

                def step(j, carry, masked):
                    run, acc = carry
                    rows_j = pl.ds(pl.multiple_of(j * Q_BLOCK, Q_BLOCK), Q_BLOCK)
                    z = _dot_nt(qi, ks[rows_j, :])
                    log_not = -_softplus(z)
                    if masked:
                        log_not = jnp.where(causal, log_not, 0.0)
                    excl = _cumdot(log_not, later)
                    log_a = z + log_not + excl + run
                    if masked:
                        log_a = jnp.where(causal, log_a, -1e30)
                    a = jnp.exp(log_a)
                    acc = acc + _dot(a.astype(BF16), vs[rows_j, :])
                    run = run + excl[:, 0:1] + log_not[:, 0:1]
                    return run, acc

                carry = (jnp.zeros((Q_BLOCK, 1), F32), jnp.zeros((Q_BLOCK, HEAD_DIM), F32))
                carry = step(i, carry, True)
                run, acc = lax.fori_loop(0, i, lambda n, cr: step(i - 1 - n, cr, False), carry)
                o_ref[rows_i, lanes] = acc.astype(BF16)
                lt_ref[rows_i, lanes] = jnp.broadcast_to(run, (Q_BLOCK, HEAD_DIM))
                return 0

            lax.fori_loop(0, n_qb, q_block, 0)

    def seg(k):
        return pl.BlockSpec((s, LANES), lambda h, k=k: (0, k * n_pairs + h))

    vec = pl.BlockSpec((1, HEAD_DIM), lambda h: (0, 0))
    out = pl.BlockSpec((s, LANES), lambda h: (0, h))
    return pl.pallas_call(
        body, name="attn_fwd", grid=(n_pairs,),
        in_specs=[seg(0), seg(1), seg(2), vec, vec], out_specs=[out, out],
        out_shape=[jax.ShapeDtypeStruct((s, d), BF16), jax.ShapeDtypeStruct((s, d), F32)],
        scratch_shapes=[pltpu.VMEM((s, HEAD_DIM), BF16)] * 3,
        compiler_params=_params("parallel"),
    )(p, p, p, qg, kg)


def _conv_rows(s):
    return _tile(s, 512)


def _conv_fwd(p, conv_w, d):
    s = p.shape[0]
    nb = d // LANES
    rows_n = _conv_rows(s)

    def body(cb_ref, cc_ref, cx_ref, w_ref, y_ref, us):
        us[pl.ds(0, 8), :] = jnp.zeros((8, LANES), F32)

        def fill(r, _):
            rows = pl.ds(pl.multiple_of(r * rows_n, rows_n), rows_n)
            us[pl.ds(pl.multiple_of(r * rows_n + 8, 8), rows_n), :] = cc_ref[rows, :] * cx_ref[rows, :]
            return 0

        lax.fori_loop(0, s // rows_n, fill, 0)
        w = w_ref[...]

        def out(r, _):
            rows = pl.ds(pl.multiple_of(r * rows_n, rows_n), rows_n)
            ext = us[pl.ds(pl.multiple_of(r * rows_n, 8), rows_n + 8), :]
            cv = (w[0:1, :] * pltpu.roll(ext, 2, 0)[8:, :] + w[1:2, :] * pltpu.roll(ext, 1, 0)[8:, :]
                  + w[2:3, :] * ext[8:, :])
            y_ref[rows, :] = (cb_ref[rows, :] * cv).astype(BF16)
            return 0

        lax.fori_loop(0, s // rows_n, out, 0)

    def seg(k):
        return pl.BlockSpec((s, LANES), lambda b, k=k: (0, k * nb + b))

    return pl.pallas_call(
        body, name="conv_fwd", grid=(nb,),
        in_specs=[seg(3), seg(4), seg(5), pl.BlockSpec((3, LANES), lambda b: (0, b))],
        out_specs=pl.BlockSpec((s, LANES), lambda b: (0, b)),
        out_shape=jax.ShapeDtypeStruct((s, d), BF16),
        scratch_shapes=[pltpu.VMEM((s + 8, LANES), F32)],
        compiler_params=_params("parallel"),
    )(p, p, p, conv_w)


def _branch(ya, yb, p, wa, wb, d):
    s = ya.shape[0]
    tm = _tile(s, 512)

    def body(ya_ref, yb_ref, ga_ref, gb_ref, wa_ref, wb_ref, m_ref, a_ref, b_ref):
        pa = _dot(ya_ref[...], wa_ref[...])
        pb = _dot(yb_ref[...], wb_ref[...])
        m_ref[...] = (jax.nn.sigmoid(ga_ref[...]) * pa + jax.nn.sigmoid(gb_ref[...]) * pb).astype(BF16)
        a_ref[...] = pa.astype(BF16)
        b_ref[...] = pb.astype(BF16)

    row = pl.BlockSpec((tm, d), lambda i: (i, 0))
    wsp = pl.BlockSpec((d, d), lambda i: (0, 0))
    shp = jax.ShapeDtypeStruct((s, d), BF16)
    return pl.pallas_call(
        body, name="branch", grid=(s // tm,),
        in_specs=[row, row, pl.BlockSpec((tm, d), lambda i: (i, 6)), pl.BlockSpec((tm, d), lambda i: (i, 7)), wsp, wsp],
        out_specs=[row, row, row], out_shape=[shp, shp, shp], compiler_params=_params("parallel"),
    )(ya, yb, p, p, wa, wb)


def _out_proj(merged, wout, x0, g1):
    s, d = x0.shape
    tm = _tile(s, 512)

    def body(m_ref, w_ref, x_ref, g_ref, x1_ref, mo_ref):
        mo = _dot(m_ref[...], w_ref[...])
        mo_ref[...] = mo
        x1_ref[...] = x_ref[...] + g_ref[...] * mo

    row = pl.BlockSpec((tm, d), lambda i: (i, 0))
    shp = jax.ShapeDtypeStruct((s, d), F32)
    return pl.pallas_call(
        body, name="out_proj", grid=(s // tm,),
        in_specs=[row, pl.BlockSpec((d, d), lambda i: (0, 0)), row, pl.BlockSpec((1, d), lambda i: (0, 0))],
        out_specs=[row, row], out_shape=[shp, shp], compiler_params=_params("parallel"),
    )(merged, wout, x0, g1)


def _ffn_up(h, wg_g, wu_g, layer):
    s, d = h.shape
    f4 = wg_g.shape[-1]
    tm = _tile(s, 512)

    def body(h_ref, wg_ref, wu_ref, gate_ref, up_ref, act_ref):
        hv = h_ref[...]
        gt = _dot(hv, wg_ref[...])
        up = _dot(hv, wu_ref[...])
        gate_ref[...] = gt.astype(BF16)
        up_ref[...] = up.astype(BF16)
        act_ref[...] = (gt * jax.nn.sigmoid(gt) * up).astype(BF16)

    wsp = pl.BlockSpec((None, None, d, f4), lambda j, i: (j, layer, 0, 0))
    osp = pl.BlockSpec((None, tm, f4), lambda j, i: (j, i, 0))
    shp = jax.ShapeDtypeStruct((N_CHIP, s, f4), BF16)
    return pl.pallas_call(
        body, name="ffn_up", grid=(N_CHIP, s // tm),
        in_specs=[pl.BlockSpec((tm, d), lambda j, i: (i, 0)), wsp, wsp],
        out_specs=[osp, osp, osp], out_shape=[shp, shp, shp], compiler_params=_params("parallel", "parallel"),
    )(h, wg_g, wu_g)


def _ffn_down(act, wd_g, x1, g2, layer):
    s, d = x1.shape
    f4 = act.shape[-1]
    tm = _tile(s, 512)

    def body(a_ref, w_ref, x_ref, g_ref, x2_ref, f_ref, acc):
        j = pl.program_id(1)

        @pl.when(j == 0)
        def _():
            acc[...] = jnp.zeros_like(acc)

        acc[...] += _dot(a_ref[...], w_ref[...])

        @pl.when(j == N_CHIP - 1)
        def _():
            f = acc[...]
            f_ref[...] = f
            x2_ref[...] = x_ref[...] + g_ref[...] * f

    row = pl.BlockSpec((tm, d), lambda i, j: (i, 0))
    shp = jax.ShapeDtypeStruct((s, d), F32)
    return pl.pallas_call(
        body, name="ffn_down", grid=(s // tm, N_CHIP),
        in_specs=[pl.BlockSpec((None, tm, f4), lambda i, j: (j, i, 0)),
                  pl.BlockSpec((None, None, f4, d), lambda i, j: (j, layer, 0, 0)),
                  row, pl.BlockSpec((1, d), lambda i, j: (0, 0))],
        out_specs=[row, row], out_shape=[shp, shp],
        scratch_shapes=[pltpu.VMEM((tm, d), F32)], compiler_params=_params("parallel", "arbitrary"),
    )(act, wd_g, x1, g2)


def _loss_head(y, target):
    s, d = y.shape
    tm = _tile(s, 512)

    def body(y_ref, t_ref, dy_ref, l_ref, acc):
        i = pl.program_id(0)

        @pl.when(i == 0)
        def _():
            acc[...] = jnp.zeros_like(acc)

        err = y_ref[...] - t_ref[...]
        dy_ref[...] = err / d
        acc[...] += jnp.sum(err * err, axis=0, keepdims=True)

        @pl.when(i == pl.num_programs(0) - 1)
        def _():
            l_ref[...] = jnp.broadcast_to(jnp.sum(acc[...], axis=1, keepdims=True), (8, LANES))

    row = pl.BlockSpec((tm, d), lambda i: (i, 0))
    return pl.pallas_call(
        body, name="loss_head", grid=(s // tm,), in_specs=[row, row],
        out_specs=[row, pl.BlockSpec((8, LANES), lambda i: (0, 0))],
        out_shape=[jax.ShapeDtypeStruct((s, d), F32), jax.ShapeDtypeStruct((8, LANES), F32)],
        scratch_shapes=[pltpu.VMEM((1, d), F32)], compiler_params=_params("arbitrary"),
    )(y, target)


def _mm_tn(a, b, a_spec, b_spec, out_rc, name):
    r, c = out_rc
    s = a.shape[-2]
    tk = _tile(s, 512)
    nk = s // tk

    def body(a_ref, b_ref, o_ref, acc):
        k = pl.program_id(1)

        @pl.when(k == 0)
        def _():
            acc[...] = jnp.zeros_like(acc)

        acc[...] += _dot_tn(a_ref[...], b_ref[...])

        @pl.when(k == nk - 1)
        def _():
            o_ref[...] = acc[...].astype(BF16)

    return pl.pallas_call(
        body, name=name, grid=(N_CHIP, nk),
        in_specs=[pl.BlockSpec(*a_spec(tk)), pl.BlockSpec(*b_spec(tk))],
        out_specs=pl.BlockSpec((None, r, c), lambda j, k: (j, 0, 0)),
        out_shape=jax.ShapeDtypeStruct((N_CHIP, r, c), BF16),
        scratch_shapes=[pltpu.VMEM((r, c), F32)], compiler_params=_params("parallel", "arbitrary"),
    )(a, b)


def _ffn_bwd1(dx2, f, g2, wd_g, gate, up, layer):
    s, d = dx2.shape
    f4 = gate.shape[-1]
    tm = _tile(s, 512)

    def body(dx_ref, f_ref, g_ref, w_ref, gate_ref, up_ref, dgate_ref, dup_ref, df_ref, dg_ref):
        i, j = pl.program_id(0), pl.program_id(1)

        @pl.when((i == 0) & (j == 0))
        def _():
            dg_ref[...] = jnp.zeros_like(dg_ref)

        dxv = dx_ref[...]
        df = (g_ref[...] * dxv).astype(BF16)

        @pl.when(j == 0)
        def _():
            df_ref[...] = df
            dg_ref[0:1, :] += jnp.sum(dxv * f_ref[...], axis=0, keepdims=True)

        da = _dot_nt(df, w_ref[...])
        gt = gate_ref[...].astype(F32)
        sg = jax.nn.sigmoid(gt)
        dup_ref[...] = (da * gt * sg).astype(BF16)
        dgate_ref[...] = (da * up_ref[...].astype(F32) * (sg * (1.0 + gt * (1.0 - sg)))).astype(BF16)

    row = pl.BlockSpec((tm, d), lambda i, j: (i, 0))
    hsp = pl.BlockSpec((None, tm, f4), lambda i, j: (j, i, 0))
    hshp = jax.ShapeDtypeStruct((N_CHIP, s, f4), BF16)
    return pl.pallas_call(
        body, name="ffn_bwd1", grid=(s // tm, N_CHIP),
        in_specs=[row, row, pl.BlockSpec((1, d), lambda i, j: (0, 0)),
                  pl.BlockSpec((None, None, f4, d), lambda i, j: (j, layer, 0, 0)), hsp, hsp],
        out_specs=[hsp, hsp, row, pl.BlockSpec((8, d), lambda i, j: (0, 0))],
        out_shape=[hshp, hshp, jax.ShapeDtypeStruct((s, d), BF16), jax.ShapeDtypeStruct((8, d), F32)],
        compiler_params=_params("arbitrary", "arbitrary"),
    )(dx2, f, g2, wd_g, gate, up)


def _ffn_bwd2(dgate, dup, wg_g, wu_g, layer):
    _, s, f4 = dgate.shape
    d = wg_g.shape[-2]
    tm = _tile(s, 512)

    def body(dg_ref, du_ref, wg_ref, wu_ref, o_ref, acc):
        j = pl.program_id(1)

        @pl.when(j == 0)
        def _():
            acc[...] = jnp.zeros_like(acc)

        acc[...] += _dot_nt(dg_ref[...], wg_ref[...]) + _dot_nt(du_ref[...], wu_ref[...])

        @pl.when(j == N_CHIP - 1)
        def _():
            o_ref[...] = acc[...]

    hsp = pl.BlockSpec((None, tm, f4), lambda i, j: (j, i, 0))
    wsp = pl.BlockSpec((None, None, d, f4), lambda i, j: (j, layer, 0, 0))
    return pl.pallas_call(
        body, name="ffn_bwd2", grid=(s // tm, N_CHIP), in_specs=[hsp, hsp, wsp, wsp],
        out_specs=pl.BlockSpec((tm, d), lambda i, j: (i, 0)), out_shape=jax.ShapeDtypeStruct((s, d), F32),
        scratch_shapes=[pltpu.VMEM((tm, d), F32)], compiler_params=_params("parallel", "arbitrary"),
    )(dgate, dup, wg_g, wu_g)


def _lnmod_bwd(x, g, sc, sh, dh, dres):
    s, d = x.shape
    tm = _tile(s, 512)

    def body(x_ref, g_ref, sc_ref, dh_ref, dr_ref, dx_ref, sums_ref):
        @pl.when(pl.program_id(0) == 0)
        def _():
            sums_ref[...] = jnp.zeros_like(sums_ref)

        xv, dhv, gv = x_ref[...], dh_ref[...], g_ref[...]
        r = lax.rsqrt(jnp.mean(xv * xv, axis=-1, keepdims=True) + EPS)
        n = xv * r
        one_sc = 1.0 + sc_ref[...]
        dt = dhv * one_sc
        sums_ref[0:1, :] += jnp.sum(dhv, axis=0, keepdims=True)
        sums_ref[1:2, :] += jnp.sum(dhv * (n * gv), axis=0, keepdims=True)
        sums_ref[2:3, :] += jnp.sum(dt * n, axis=0, keepdims=True)
        dn = dt * gv
        dx_ref[...] = dr_ref[...] + r * (dn - n * jnp.mean(dn * n, axis=-1, keepdims=True))

    del sh
    vec = pl.BlockSpec((1, d), lambda i: (0, 0))
    row = pl.BlockSpec((tm, d), lambda i: (i, 0))
    return pl.pallas_call(
        body, name="lnmod_bwd", grid=(s // tm,), in_specs=[row, vec, vec, row, row],
        out_specs=[row, pl.BlockSpec((8, d), lambda i: (0, 0))],
        out_shape=[jax.ShapeDtypeStruct((s, d), F32), jax.ShapeDtypeStruct((8, d), F32)],
        compiler_params=_params("arbitrary"),
    )(x, g, sc, dh, dres)


def _out_bwd(dx1, mo, g1, wout, pa, pb, p, wa, wb, d):
    s = dx1.shape[0]
    tm = _tile(s, 256)

    def body(dx_ref, mo_ref, g_ref, wo_ref, pa_ref, pb_ref, ga_ref, gb_ref, wa_ref, wb_ref,
             dmo_ref, da_ref, db_ref, dya_ref, dyb_ref, dga_ref, dgb_ref, dg_ref):
        @pl.when(pl.program_id(0) == 0)
        def _():
            dg_ref[...] = jnp.zeros_like(dg_ref)

        dxv = dx_ref[...]
        dg_ref[0:1, :] += jnp.sum(dxv * mo_ref[...], axis=0, keepdims=True)
        dmo = (g_ref[...] * dxv).astype(BF16)
        dmo_ref[...] = dmo
        dm = _dot_nt(dmo, wo_ref[...])
        sa, sb = jax.nn.sigmoid(ga_ref[...]), jax.nn.sigmoid(gb_ref[...])
        da = (dm * sa).astype(BF16)
        db = (dm * sb).astype(BF16)
        da_ref[...] = da
        db_ref[...] = db
        dga_ref[...] = (dm * pa_ref[...].astype(F32) * (sa * (1.0 - sa))).astype(BF16)
        dgb_ref[...] = (dm * pb_ref[...].astype(F32) * (sb * (1.0 - sb))).astype(BF16)
        dya_ref[...] = _dot_nt(da, wa_ref[...]).astype(BF16)
        dyb_ref[...] = _dot_nt(db, wb_ref[...]).astype(BF16)

    row = pl.BlockSpec((tm, d), lambda i: (i, 0))
    wsp = pl.BlockSpec((d, d), lambda i: (0, 0))
    shp = jax.ShapeDtypeStruct((s, d), BF16)
    return pl.pallas_call(
        body, name="out_bwd", grid=(s // tm,),
        in_specs=[row, row, pl.BlockSpec((1, d), lambda i: (0, 0)), wsp, row, row,
                  pl.BlockSpec((tm, d), lambda i: (i, 6)), pl.BlockSpec((tm, d), lambda i: (i, 7)), wsp, wsp],
        out_specs=[row] * 7 + [pl.BlockSpec((8, d), lambda i: (0, 0))],
        out_shape=[shp] * 7 + [jax.ShapeDtypeStruct((8, d), F32)],
        compiler_params=_params("arbitrary"),
    )(dx1, mo, g1, wout, pa, pb, p, p, wa, wb)


def _conv_bwd(p, conv_w, dyb, d):
    s = p.shape[0]
    nb = d // LANES
    rows_n = _conv_rows(s)

    def body(cb_ref, cc_ref, cx_ref, w_ref, dy_ref, dcb_ref, dcc_ref, dcx_ref, dw_ref, us, ds):
        us[pl.ds(0, 8), :] = jnp.zeros((8, LANES), F32)
        ds[pl.ds(s, 8), :] = jnp.zeros((8, LANES), F32)

        def fill(r, _):
            rows = pl.ds(pl.multiple_of(r * rows_n, rows_n), rows_n)
            us[pl.ds(pl.multiple_of(r * rows_n + 8, 8), rows_n), :] = cc_ref[rows, :] * cx_ref[rows, :]
            ds[rows, :] = dy_ref[rows, :].astype(F32) * cb_ref[rows, :]
            return 0

        lax.fori_loop(0, s // rows_n, fill, 0)
        w = w_ref[...]

        def out(r, carry):
            dw0, dw1, dw2 = carry
            rows = pl.ds(pl.multiple_of(r * rows_n, rows_n), rows_n)
            ext = us[pl.ds(pl.multiple_of(r * rows_n, 8), rows_n + 8), :]
            u0, u1, u2 = ext[8:, :], pltpu.roll(ext, 1, 0)[8:, :], pltpu.roll(ext, 2, 0)[8:, :]
            cv = w[0:1, :] * u2 + w[1:2, :] * u1 + w[2:3, :] * u0
            dcb_ref[rows, :] = (dy_ref[rows, :].astype(F32) * cv).astype(BF16)
            nxt = ds[pl.ds(pl.multiple_of(r * rows_n, 8), rows_n + 8), :]
            e0 = nxt[:rows_n, :]
            e1 = pltpu.roll(nxt, rows_n + 7, 0)[:rows_n, :]
            e2 = pltpu.roll(nxt, rows_n + 6, 0)[:rows_n, :]
            du = w[2:3, :] * e0 + w[1:2, :] * e1 + w[0:1, :] * e2
            dcc_ref[rows, :] = (du * cx_ref[rows, :]).astype(BF16)
            dcx_ref[rows, :] = (du * cc_ref[rows, :]).astype(BF16)
            return (dw0 + jnp.sum(e0 * u2, axis=0, keepdims=True), dw1 + jnp.sum(e0 * u1, axis=0, keepdims=True),
                    dw2 + jnp.sum(e0 * u0, axis=0, keepdims=True))

        zero = jnp.zeros((1, LANES), F32)
        dw0, dw1, dw2 = lax.fori_loop(0, s // rows_n, out, (zero, zero, zero))
        dw_ref[...] = jnp.zeros_like(dw_ref)
        dw_ref[0:1, :] = dw0
        dw_ref[1:2, :] = dw1
        dw_ref[2:3, :] = dw2

    def seg(k):
        return pl.BlockSpec((s, LANES), lambda b, k=k: (0, k * nb + b))

    col = pl.BlockSpec((s, LANES), lambda b: (0, b))
    shp = jax.ShapeDtypeStruct((s, d), BF16)
    return pl.pallas_call(
        body, name="conv_bwd", grid=(nb,),
        in_specs=[seg(3), seg(4), seg(5), pl.BlockSpec((3, LANES), lambda b: (0, b)), col],
        out_specs=[col, col, col, pl.BlockSpec((8, LANES), lambda b: (0, b))],
        out_shape=[shp, shp, shp, jax.ShapeDtypeStruct((8, d), F32)],
        scratch_shapes=[pltpu.VMEM((s + 8, LANES), F32), pltpu.VMEM((s + 8, LANES), F32)],
        compiler_params=_params("parallel"),
    )(p, p, p, conv_w, dyb)


def _attn_bwd(p, qg, kg, dy, lt, d):
    s = p.shape[0]
    n_pairs = d // LANES
    n_qb = s // Q_BLOCK
    chunk = _tile(s, 512)
    inv_sqrt = 1.0 / math.sqrt(HEAD_DIM)

    def body(q_ref, k_ref, v_ref, qg_ref, kg_ref, dy_ref, lt_ref, dq_ref, dk_ref, dv_ref, dgain_ref,
             qs, ks, vs, dys, dks, dvs):
        row = lax.broadcasted_iota(jnp.int32, (Q_BLOCK, Q_BLOCK), 0)
        col = lax.broadcasted_iota(jnp.int32, (Q_BLOCK, Q_BLOCK), 1)
        upto = (row <= col).astype(BF16)
        causal = col < row
        dgain_ref[...] = jnp.zeros_like(dgain_ref)

        for hh in range(2):
            lanes = slice(hh * HEAD_DIM, (hh + 1) * HEAD_DIM)

            def prep(r, _):
                rows = pl.ds(pl.multiple_of(r * chunk, chunk), chunk)
                qs[rows, :] = _head_norm(q_ref[rows, lanes], qg_ref[...], inv_sqrt).astype(BF16)
                ks[rows, :] = _head_norm(k_ref[rows, lanes], kg_ref[...], 1.0).astype(BF16)
                vs[rows, :] = v_ref[rows, lanes].astype(BF16)
                dys[rows, :] = dy_ref[rows, lanes]
                dks[rows, :] = jnp.zeros((chunk, HEAD_DIM), F32)
                dvs[rows, :] = jnp.zeros((chunk, HEAD_DIM), F32)
                return 0

            lax.fori_loop(0, s // chunk, prep, 0)

            def q_block(i, dqg):
                rows_i = pl.ds(pl.multiple_of(i * Q_BLOCK, Q_BLOCK), Q_BLOCK)
                qi = qs[rows_i, :]
                dyi = dys[rows_i, :]
                total = lt_ref[rows_i, lanes][:, 0:1]

                def step(j, carry, masked):
                    before, g_before, dq = carry
                    rows_j = pl.ds(pl.multiple_of(j * Q_BLOCK, Q_BLOCK), Q_BLOCK)
                    kj = ks[rows_j, :]
                    z = _dot_nt(qi, kj)
                    log_not = -_softplus(z)
                    if masked:
                        log_not = jnp.where(causal, log_not, 0.0)
                    incl = _cumdot(log_not, upto)
                    log_a = z + log_not + (total - (before + incl))
                    if masked:
                        log_a = jnp.where(causal, log_a, -1e30)
                    a = jnp.exp(log_a)
                    g = a * _dot_nt(dyi, vs[rows_j, :])
                    g_incl = _cumdot(g, upto)
                    dz = g - jnp.exp(z + log_not) * (g_before + g_incl)
                    if masked:
                        dz = jnp.where(causal, dz, 0.0)
                    dzb = dz.astype(BF16)
                    dvs[rows_j, :] += _dot(a.T.astype(BF16), dyi)
                    dks[rows_j, :] += _dot(dz.T.astype(BF16), qi)
                    dq = dq + _dot(dzb, kj)
                    return (before + incl[:, Q_BLOCK - 1:Q_BLOCK], g_before + g_incl[:, Q_BLOCK - 1:Q_BLOCK], dq)

                zero = jnp.zeros((Q_BLOCK, 1), F32)
                carry = (zero, zero, jnp.zeros((Q_BLOCK, HEAD_DIM), F32))
                carry = lax.fori_loop(0, i, lambda j, cr: step(j, cr, False), carry)
                _, _, dqn = step(i, carry, True)
                qraw = q_ref[rows_i, lanes]
                r = lax.rsqrt(jnp.mean(qraw * qraw, axis=-1, keepdims=True) + EPS)
                qhat = qraw * r
                dqhat = dqn * (qg_ref[...] * inv_sqrt)
                dq_ref[rows_i, lanes] = (r * (dqhat - qhat * jnp.mean(dqhat * qhat, axis=-1, keepdims=True))
                                         ).astype(BF16)
                return dqg + jnp.sum(dqn * qhat, axis=0, keepdims=True) * inv_sqrt

            dqg = lax.fori_loop(0, n_qb, q_block, jnp.zeros((1, HEAD_DIM), F32))

            def finish(r, dkg):
                rows = pl.ds(pl.multiple_of(r * chunk, chunk), chunk)
                kraw = k_ref[rows, lanes]
                rk = lax.rsqrt(jnp.mean(kraw * kraw, axis=-1, keepdims=True) + EPS)
                khat = kraw * rk
                dkn = dks[rows, :]
                dkhat = dkn * kg_ref[...]
                dk_ref[rows, lanes] = (rk * (dkhat - khat * jnp.mean(dkhat * khat, axis=-1, keepdims=True))
                                       ).astype(BF16)
                dv_ref[rows, lanes] = dvs[rows, :].astype(BF16)
                return dkg + jnp.sum(dkn * khat, axis=0, keepdims=True)

            dkg = lax.fori_loop(0, s // chunk, finish, jnp.zeros((1, HEAD_DIM), F32))
            dgain_ref[0:1, lanes] = dqg
            dgain_ref[1:2, lanes] = dkg

    def seg(k):
        return pl.BlockSpec((s, LANES), lambda h, k=k: (0, k * n_pairs + h))

    vec = pl.BlockSpec((1, HEAD_DIM), lambda h: (0, 0))
    col = pl.BlockSpec((s, LANES), lambda h: (0, h))
    shp = jax.ShapeDtypeStruct((s, d), BF16)
    return pl.pallas_call(
        body, name="attn_bwd", grid=(n_pairs,),
        in_specs=[seg(0), seg(1), seg(2), vec, vec, col, col],
        out_specs=[col, col, col, pl.BlockSpec((None, 8, LANES), lambda h: (h, 0, 0))],
        out_shape=[shp, shp, shp, jax.ShapeDtypeStruct((n_pairs, 8, LANES), F32)],
        scratch_shapes=[pltpu.VMEM((s, HEAD_DIM), BF16)] * 4 + [pltpu.VMEM((s, HEAD_DIM), F32)] * 2,
        compiler_params=_params("parallel"),
    )(p, p, p, qg, kg, dy, lt)


def _mm_in_bwd(dp, w_g, layer):
    s = dp.shape[0]
    d, n4 = w_g.shape[-2:]
    tm = _tile(s, 512)

    def body(a_ref, w_ref, o_ref, acc):
        j = pl.program_id(1)

        @pl.when(j == 0)
        def _():
            acc[...] = jnp.zeros_like(acc)

        acc[...] += _dot_nt(a_ref[...], w_ref[...])

        @pl.when(j == N_CHIP - 1)
        def _():
            o_ref[...] = acc[...]

    return pl.pallas_call(
        body, name="mm_in_bwd", grid=(s // tm, N_CHIP),
        in_specs=[pl.BlockSpec((tm, n4), lambda i, j: (i, j)),
                  pl.BlockSpec((None, None, d, n4), lambda i, j: (j, layer, 0, 0))],
        out_specs=pl.BlockSpec((tm, d), lambda i, j: (i, 0)), out_shape=jax.ShapeDtypeStruct((s, d), F32),
        scratch_shapes=[pltpu.VMEM((tm, d), F32)], compiler_params=_params("parallel", "arbitrary"),
    )(dp, w_g)


def _sum_adam(parts, w, m, v):
    n_l, r, c = w.shape
    tr = next((t for t in (256, 176, 128, 64, 32, 16) if r % t == 0 and t * c <= 256 * 1024), r)

    def body(p_ref, w_ref, m_ref, v_ref, g_ref, dl_ref, nm_ref, nv_ref):
        g = p_ref[0].astype(F32)
        for dev in range(1, N_DEV):
            g = g + p_ref[dev].astype(F32)
        g_ref[...] = g
        delta, nm, nv = _adamw(w_ref[...], g, m_ref[...], v_ref[...])
        dl_ref[...] = delta
        nm_ref[...] = nm
        nv_ref[...] = nv

    wsp = pl.BlockSpec((None, tr, c), lambda l, i: (l, i, 0))
    shp = jax.ShapeDtypeStruct(w.shape, F32)
    return pl.pallas_call(
        body, name="sum_adam", grid=(n_l, r // tr),
        in_specs=[pl.BlockSpec((None, N_DEV, tr, c), lambda l, i: (l, 0, i, 0)), wsp, wsp, wsp],
        out_specs=[wsp] * 4, out_shape=[shp] * 4, compiler_params=_params("parallel", "parallel"),
    )(parts, w, m, v)


def _small_adam(parts, w, m, v):
    def body(p_ref, w_ref, m_ref, v_ref, g_ref, dl_ref, nm_ref, nv_ref):
        g = p_ref[0]
        for dev in range(1, N_DEV):
            g = g + p_ref[dev]
        g_ref[...] = g
        delta, nm, nv = _adamw(w_ref[...], g, m_ref[...], v_ref[...])
        dl_ref[...] = delta
        nm_ref[...] = nm
        nv_ref[...] = nv

    shp = jax.ShapeDtypeStruct(w.shape, F32)
    vm = pl.BlockSpec(memory_space=pltpu.VMEM)
    return pl.pallas_call(body, name="small_adam", in_specs=[vm] * 4, out_specs=[vm] * 4, out_shape=[shp] * 4,
                          compiler_params=pltpu.CompilerParams(vmem_limit_bytes=VMEM_LIMIT_BYTES))(parts, w, m, v)


def _pack(vecs, mult=8 * LANES):
    flat = jnp.concatenate([a.reshape(-1).astype(F32) for a in vecs])
    pad = (-flat.shape[0]) % mult
    if pad:
        flat = jnp.concatenate([flat, jnp.zeros((pad,), F32)])
    return flat.reshape(8, -1)


def _unpack(flat, shapes):
    flat = flat.reshape(-1)
    out, off = [], 0
    for shp in shapes:
        n = math.prod(shp)
        out.append(flat[off:off + n].reshape(shp))
        off += n
    return out


def _local_step(x, target, mods, ln1_g, ln2_g, qg, kg, conv_w, win_g, wa_g, wb_g, wo_g, wg_g, wu_g, wd_g):
    s, d = x.shape
    n_l = mods.shape[0]
    saved = []
    h_in = x
    for l in range(n_l):
        sh1, sc1, g1, sh2, sc2, g2 = [mods[l, k * d:(k + 1) * d].reshape(1, d) for k in range(6)]
        wa = wa_g[:, l].reshape(d, d)
        wb = wb_g[:, l].reshape(d, d)
        wo = wo_g[:, l].reshape(d, d)
        h1 = _lnmod(h_in, ln1_g[l:l + 1], sc1, sh1)
        p = _mm_in(h1, win_g, l)
        ya, lt = _attn_fwd(p, qg[l:l + 1], kg[l:l + 1], d)
        yb = _conv_fwd(p, conv_w[l], d)
        merged, pa, pb = _branch(ya, yb, p, wa, wb, d)
        x1, mo = _out_proj(merged, wo, h_in, g1)
        h2 = _lnmod(x1, ln2_g[l:l + 1], sc2, sh2)
        gate, up, act = _ffn_up(h2, wg_g, wu_g, l)
        x2, f = _ffn_down(act, wd_g, x1, g2, l)
        saved.append(dict(x0=h_in, h1=h1, p=p, ya=ya, lt=lt, yb=yb, merged=merged, pa=pa, pb=pb, x1=x1, mo=mo,
                          h2=h2, gate=gate, up=up, act=act, f=f, wa=wa, wb=wb, wo=wo,
                          mod=(sh1, sc1, g1, sh2, sc2, g2)))
        h_in = x2

    dx, loss_tile = _loss_head(h_in, target)

    f4 = wg_g.shape[-1]
    n4 = win_g.shape[-1]
    r4 = d // N_CHIP
    small, big = [None] * n_l, [None] * n_l
    for l in reversed(range(n_l)):
        sv = saved[l]
        sh1, sc1, g1, sh2, sc2, g2 = sv["mod"]
        dgate, dup, df, dg2 = _ffn_bwd1(dx, sv["f"], g2, wd_g, sv["gate"], sv["up"], l)
        g_wd = _mm_tn(sv["act"], df, lambda tk: ((None, tk, f4), lambda j, k: (j, k, 0)),
                      lambda tk: ((tk, d), lambda j, k: (k, 0)), (f4, d), "grad_wd")
        hsp = lambda tk: ((tk, d), lambda j, k: (k, 0))
        fsp = lambda tk: ((None, tk, f4), lambda j, k: (j, k, 0))
        g_wg = _mm_tn(sv["h2"], dgate, hsp, fsp, (d, f4), "grad_wg")
        g_wu = _mm_tn(sv["h2"], dup, hsp, fsp, (d, f4), "grad_wu")
        dh2 = _ffn_bwd2(dgate, dup, wg_g, wu_g, l)
        dx1, sums2 = _lnmod_bwd(sv["x1"], ln2_g[l:l + 1], sc2, sh2, dh2, dx)
        dmo, da, db, dya, dyb, dga, dgb, dg1 = _out_bwd(dx1, sv["mo"], g1, sv["wo"], sv["pa"], sv["pb"], sv["p"],
                                                        sv["wa"], sv["wb"], d)
        csp = lambda tk: ((tk, r4), lambda j, k: (k, j))
        g_wo = _mm_tn(sv["merged"], dmo, csp, hsp, (r4, d), "grad_wo")
        g_wa = _mm_tn(sv["ya"], da, csp, hsp, (r4, d), "grad_wa")
        g_wb = _mm_tn(sv["yb"], db, csp, hsp, (r4, d), "grad_wb")
        dcb, dcc, dcx, dconv = _conv_bwd(sv["p"], conv_w[l], dyb, d)
        dq, dk, dv, dgain = _attn_bwd(sv["p"], qg[l:l + 1], kg[l:l + 1], dya, sv["lt"], d)
        dp = jnp.concatenate([dq, dk, dv, dcb, dcc, dcx, dga, dgb], axis=1)
        g_win = _mm_tn(sv["h1"], dp, hsp, lambda tk: ((tk, n4), lambda j, k: (k, j)), (d, n4), "grad_win")
        dh1 = _mm_in_bwd(dp, win_g, l)
        dx, sums1 = _lnmod_bwd(sv["x0"], ln1_g[l:l + 1], sc1, sh1, dh1, dx1)
        dgain = jnp.sum(dgain[:, 0:2, :], axis=0)
        dgain = dgain[:, :HEAD_DIM] + dgain[:, HEAD_DIM:]
        dmod = jnp.concatenate([sums1[0], sums1[1], dg1[0], sums2[0], sums2[1], dg2[0]])
        small[l] = dict(dmod=dmod, ln1=sums1[2], ln2=sums2[2], qg=dgain[0], kg=dgain[1], conv=dconv[0:3])
        big[l] = dict(win=g_win, wa=g_wa, wb=g_wb, wo=g_wo, wg=g_wg, wu=g_wu, wd=g_wd)
    return loss_tile, dx, small, big


BIG = ("win", "wa", "wb", "wo", "wg", "wu", "wd")


def kernel(x, c, ada_w, ada_b, ln1_g, w_in, q_norm_g, k_norm_g, conv_w, w_branch_a, w_branch_b, w_out, ln2_g, w_ffn_gate, w_ffn_up, w_ffn_down, loss_target, m_ada_w, m_ada_b, m_ln1_g, m_w_in, m_q_norm_g, m_k_norm_g, m_conv_w, m_w_branch_a, m_w_branch_b, m_w_out, m_ln2_g, m_w_ffn_gate, m_w_ffn_up, m_w_ffn_down, v_ada_w, v_ada_b, v_ln1_g, v_w_in, v_q_norm_g, v_k_norm_g, v_conv_w, v_w_branch_a, v_w_branch_b, v_w_out, v_ln2_g, v_w_ffn_gate, v_w_ffn_up, v_w_ffn_down):
    n_l, d, a4 = ada_w.shape
    s = x.shape[1]
    cw4 = conv_w.shape[-1]
    ix, iy, ic = lax.axis_index("x"), lax.axis_index("y"), lax.axis_index("c")
    chip = 2 * ix + iy
    me = 2 * chip + ic

    got = _gather8(_pack([c, conv_w])).reshape(N_DEV, -1)
    c_all = got[:, :d]
    conv_all = got[:, d:d + n_l * 3 * cw4].reshape(N_CHIP, 2, n_l, 3, cw4)[:, 0]
    conv_full = jnp.transpose(conv_all, (1, 2, 0, 3)).reshape(n_l, 3, N_CHIP * cw4)
    b_cols = lax.dynamic_slice_in_dim(ada_b, chip * a4, a4, axis=1).reshape(n_l, 1, a4)
    mod_cols = _ada_mod(c_all, ada_w, b_cols)
    mod_all = _gather8(_pack([mod_cols])).reshape(N_DEV, -1)[:, :n_l * N_DEV * a4]
    mod_all = mod_all.reshape(N_CHIP, 2, n_l, N_DEV, a4)[:, 0]
    mods = lax.dynamic_index_in_dim(mod_all, me, axis=2, keepdims=False)
    mods = jnp.transpose(mods, (1, 0, 2)).reshape(n_l, N_CHIP * a4)

    big_w = dict(win=w_in, wa=w_branch_a, wb=w_branch_b, wo=w_out, wg=w_ffn_gate, wu=w_ffn_up, wd=w_ffn_down)
    big_m = dict(win=m_w_in, wa=m_w_branch_a, wb=m_w_branch_b, wo=m_w_out, wg=m_w_ffn_gate, wu=m_w_ffn_up,
                 wd=m_w_ffn_down)
    big_v = dict(win=v_w_in, wa=v_w_branch_a, wb=v_w_branch_b, wo=v_w_out, wg=v_w_ffn_gate, wu=v_w_ffn_up,
                 wd=v_w_ffn_down)
    gathered = dict(zip(BIG, _gather_weights([big_w[k].astype(BF16) for k in BIG])))

    loss_tile, grad_x, small, big = _local_step(
        x[0], loss_target[0], mods, ln1_g, ln2_g, q_norm_g, k_norm_g, conv_full,
        *[gathered[k] for k in BIG])

    parts = _exchange_grads([[big[l][k] for l in range(n_l)] for k in BIG], n_l)
    big_out = {k: _sum_adam(pt, big_w[k], big_m[k], big_v[k]) for k, pt in zip(BIG, parts)}

    sm_shapes = [(n_l, 6 * d), (n_l, d), (n_l, d), (n_l, HEAD_DIM), (n_l, HEAD_DIM), (n_l, 3, d), (1,)]
    vec = _pack([jnp.stack([small[l][k] for l in range(n_l)]) for k in ("dmod", "ln1", "ln2", "qg", "kg", "conv")]
                + [loss_tile[0, 0:1]])
    n_vec = vec.shape[1] * 8
    all_vec = _gather8(vec).reshape(N_DEV, n_vec)
    per_dev = [_unpack(all_vec[dev], sm_shapes) for dev in range(N_DEV)]
    dmod_all = jnp.stack([pd[0] for pd in per_dev])
    dmod_cols = jnp.transpose(lax.dynamic_slice_in_dim(dmod_all, chip * a4, a4, axis=2), (1, 0, 2))
    ada_out = _ada_grad_adam(jnp.transpose(c_all), dmod_cols, ada_w, m_ada_w, v_ada_w)

    def small_pack(parts_of):
        return _pack([parts_of[0], parts_of[1], parts_of[2], parts_of[3], parts_of[4], parts_of[5], parts_of[6]])

    dev_parts = jnp.stack([
        small_pack([pd[0], pd[1], pd[2], pd[3], pd[4], lax.dynamic_slice_in_dim(pd[5], chip * cw4, cw4, axis=2),
                    pd[6]]) for pd in per_dev])
    zero1 = jnp.zeros((1,), F32)
    sw = small_pack([ada_b, ln1_g, ln2_g, q_norm_g, k_norm_g, conv_w, zero1])
    sm = small_pack([m_ada_b, m_ln1_g, m_ln2_g, m_q_norm_g, m_k_norm_g, m_conv_w, zero1])
    sv = small_pack([v_ada_b, v_ln1_g, v_ln2_g, v_q_norm_g, v_k_norm_g, v_conv_w, zero1 + 1.0])
    out_shapes = [(n_l, 6 * d), (n_l, d), (n_l, d), (n_l, HEAD_DIM), (n_l, HEAD_DIM), (n_l, 3, cw4), (1,)]
    sm_out = [_unpack(o, out_shapes) for o in _small_adam(dev_parts, sw, sm, sv)]
    loss = 0.5 * sm_out[0][6][0] / d

    outs = [loss, grad_x[None]]
    for kind in range(4):
        sm_k = sm_out[kind]
        outs += [ada_out[kind], sm_k[0], sm_k[1], big_out["win"][kind], sm_k[3], sm_k[4], sm_k[5],
                 big_out["wa"][kind], big_out["wb"][kind], big_out["wo"][kind], sm_k[2],
                 big_out["wg"][kind], big_out["wu"][kind], big_out["wd"][kind]]
    return tuple(outs)
```

```python
import functools
import math

import jax
import jax.numpy as jnp
from jax import lax
from jax.experimental import pallas as pl
from jax.experimental.pallas import tpu as pltpu

F32 = jnp.float32
BF16 = jnp.bfloat16
MESH_ID = pl.DeviceIdType.MESH

EPS = 1e-6
HEAD_DIM = 64
Q_BLOCK = 128
Q_SUPER = 512
LANES = 128
N_DEV = 8
N_CHIP = 4
VMEM_LIMIT_BYTES = 56 * 1024 * 1024

ADAM_LR = 0.001
ADAM_B1 = 0.9
ADAM_B2 = 0.999
ADAM_EPS = 1e-08
ADAM_WD = 0.01
ADAM_STEP = 10


def _params(*sem):
    return pltpu.CompilerParams(dimension_semantics=tuple(sem), vmem_limit_bytes=VMEM_LIMIT_BYTES)


def _tile(n, pref):
    return pref if n % pref == 0 else n


def _dot(a, b):
    return jnp.dot(a, b, preferred_element_type=F32)


def _dot_nt(a, b):
    return lax.dot_general(a, b, (((1,), (1,)), ((), ())), preferred_element_type=F32)


def _dot_tn(a, b):
    return lax.dot_general(a, b, (((0,), (0,)), ((), ())), preferred_element_type=F32)


def _softplus(z):
    return jnp.maximum(z, 0.0) + jnp.log(1.0 + jnp.exp(-jnp.abs(z)))


def _cumdot(v, tri):
    hi = v.astype(BF16)
    lo = (v - hi.astype(F32)).astype(BF16)
    return _dot(hi, tri) + _dot(lo, tri)


def _adamw(w, g, m, v):
    m = ADAM_B1 * m + (1.0 - ADAM_B1) * g
    v = ADAM_B2 * v + (1.0 - ADAM_B2) * (g * g)
    m_hat = m / (1.0 - ADAM_B1 ** ADAM_STEP)
    v_hat = v / (1.0 - ADAM_B2 ** ADAM_STEP)
    delta = -ADAM_LR * (m_hat / (jnp.sqrt(v_hat) + ADAM_EPS) + ADAM_WD * w)
    return delta, m, v


def _peer(x, y, c, k):
    return (1 - x if k & 4 else x, 1 - y if k & 2 else y, 1 - c if k & 1 else c)


def _gather8(v):
    rows_per, m = v.shape

    def body(v_ref, out_ref, send_sems, recv_sems, local_sem):
        x, y, c = lax.axis_index("x"), lax.axis_index("y"), lax.axis_index("c")

        def rows(p):
            return out_ref.at[pl.ds((4 * p[0] + 2 * p[1] + p[2]) * rows_per, rows_per), :]

        me = (x, y, c)
        mine = pltpu.make_async_copy(v_ref, rows(me), local_sem)
        mine.start()
        sends = []
        for k in range(1, N_DEV):
            cp = pltpu.make_async_remote_copy(
                src_ref=v_ref, dst_ref=rows(me), send_sem=send_sems.at[k - 1], recv_sem=recv_sems.at[k - 1],
                device_id=_peer(x, y, c, k), device_id_type=MESH_ID)
            cp.start()
            sends.append(cp)
        for k in range(1, N_DEV):
            pltpu.make_async_remote_copy(
                src_ref=v_ref, dst_ref=rows(_peer(x, y, c, k)), send_sem=send_sems.at[k - 1],
                recv_sem=recv_sems.at[k - 1], device_id=_peer(x, y, c, k), device_id_type=MESH_ID).wait_recv()
        for cp in sends:
            cp.wait_send()
        mine.wait()

    return pl.pallas_call(
        body, name="gather8",
        out_shape=jax.ShapeDtypeStruct((N_DEV * rows_per, m), v.dtype),
        in_specs=[pl.BlockSpec(memory_space=pltpu.VMEM)],
        out_specs=pl.BlockSpec(memory_space=pltpu.VMEM),
        scratch_shapes=[pltpu.SemaphoreType.DMA((N_DEV - 1,)), pltpu.SemaphoreType.DMA((N_DEV - 1,)),
                        pltpu.SemaphoreType.DMA],
    )(v)


def _gather_weights(shards):
    n = len(shards)

    def body(*refs):
        ins, outs = refs[:n], refs[n:2 * n]
        send_sems, recv_sems, local_sems = refs[2 * n:]
        x, y, c = lax.axis_index("x"), lax.axis_index("y"), lax.axis_index("c")
        chips = [(1 - x, y), (x, 1 - y), (1 - x, 1 - y)]
        mine = 2 * x + y
        local, sends = [], []
        for a in range(n):
            cp = pltpu.make_async_copy(ins[a], outs[a].at[mine], local_sems.at[a])
            cp.start()
            local.append(cp)
            for j, (px, py) in enumerate(chips):
                rc = pltpu.make_async_remote_copy(
                    src_ref=ins[a], dst_ref=outs[a].at[mine], send_sem=send_sems.at[a, j],
                    recv_sem=recv_sems.at[a, j], device_id=(px, py, c), device_id_type=MESH_ID)
                rc.start()
                sends.append(rc)
        for a in range(n):
            for j, (px, py) in enumerate(chips):
                pltpu.make_async_remote_copy(
                    src_ref=ins[a], dst_ref=outs[a].at[2 * px + py], send_sem=send_sems.at[a, j],
                    recv_sem=recv_sems.at[a, j], device_id=(px, py, c), device_id_type=MESH_ID).wait_recv()
        for rc in sends:
            rc.wait_send()
        for cp in local:
            cp.wait()

    any_spec = pl.BlockSpec(memory_space=pl.ANY)
    return pl.pallas_call(
        body, name="gather_weights",
        out_shape=[jax.ShapeDtypeStruct((N_CHIP,) + s.shape, s.dtype) for s in shards],
        in_specs=[any_spec] * n, out_specs=[any_spec] * n,
        scratch_shapes=[pltpu.SemaphoreType.DMA((n, 3)), pltpu.SemaphoreType.DMA((n, 3)),
                        pltpu.SemaphoreType.DMA((n,))],
    )(*shards)


def _exchange_grads(grads, n_layers):
    flat = [g for per_w in grads for g in per_w]
    n_w, n = len(grads), len(flat)

    def body(*refs):
        ins, outs = refs[:n], refs[n:n + n_w]
        send_sems, recv_sems, local_sems = refs[n + n_w:]
        x, y, c = lax.axis_index("x"), lax.axis_index("y"), lax.axis_index("c")
        chips = [(1 - x, y), (x, 1 - y), (1 - x, 1 - y)]
        mine = 2 * x + y
        sibling = (x, y, 1 - c)

        def slot(px, py, pc):
            return 4 * px + 2 * py + pc

        def copy(a, k, src, dst, to):
            return pltpu.make_async_remote_copy(src_ref=src, dst_ref=dst, send_sem=send_sems.at[a, k],
                                                recv_sem=recv_sems.at[a, k], device_id=to, device_id_type=MESH_ID)

        sends, locals_ = [], []
        for a in range(n):
            w, l = divmod(a, n_layers)
            out = outs[w]
            lc = pltpu.make_async_copy(ins[a].at[mine], out.at[l, slot(x, y, c)], local_sems.at[a])
            lc.start()
            locals_.append(lc)
            cp = copy(a, 0, ins[a].at[mine], out.at[l, slot(x, y, c)], sibling)
            cp.start()
            sends.append(cp)
            for j, (px, py) in enumerate(chips):
                cp = copy(a, 1 + j, ins[a].at[2 * px + py], out.at[l, slot(x, y, c)], (px, py, c))
                cp.start()
                sends.append(cp)
        for a in range(n):
            w, l = divmod(a, n_layers)
            out = outs[w]
            for j, (px, py) in enumerate(chips):
                landed = out.at[l, slot(px, py, c)]
                copy(a, 1 + j, landed, landed, (px, py, c)).wait_recv()
                cp = copy(a, 4 + j, landed, landed, sibling)
                cp.start()
                sends.append(cp)
        for a in range(n):
            w, l = divmod(a, n_layers)
            out = outs[w]
            theirs = out.at[l, slot(x, y, 1 - c)]
            copy(a, 0, theirs, theirs, sibling).wait_recv()
            for j, (px, py) in enumerate(chips):
                passed = out.at[l, slot(px, py, 1 - c)]
                copy(a, 4 + j, passed, passed, sibling).wait_recv()
        for cp in sends:
            cp.wait_send()
        for lc in locals_:
            lc.wait()

    any_spec = pl.BlockSpec(memory_space=pl.ANY)
    return pl.pallas_call(
        body, name="exchange_grads",
        out_shape=[jax.ShapeDtypeStruct((n_layers, N_DEV) + per_w[0].shape[1:], per_w[0].dtype) for per_w in grads],
        in_specs=[any_spec] * n, out_specs=[any_spec] * n_w,
        scratch_shapes=[pltpu.SemaphoreType.DMA((n, 7)), pltpu.SemaphoreType.DMA((n, 7)),
                        pltpu.SemaphoreType.DMA((n,))],
    )(*flat)


def _ada_mod(c_all, ada_w, ada_b_cols):
    n_l, d, a4 = ada_w.shape
    tn = _tile(a4, 512)

    def body(c_ref, w_ref, b_ref, o_ref):
        cv = c_ref[...]
        ca = (cv * jax.nn.sigmoid(cv)).astype(BF16)
        o_ref[...] = _dot(ca, w_ref[...].astype(BF16)) + b_ref[...]

    return pl.pallas_call(
        body, name="ada_mod", grid=(n_l, a4 // tn),
        in_specs=[pl.BlockSpec((N_DEV, d), lambda l, j: (0, 0)),
                  pl.BlockSpec((None, d, tn), lambda l, j: (l, 0, j)),
                  pl.BlockSpec((None, 1, tn), lambda l, j: (l, 0, j))],
        out_specs=pl.BlockSpec((None, N_DEV, tn), lambda l, j: (l, 0, j)),
        out_shape=jax.ShapeDtypeStruct((n_l, N_DEV, a4), F32),
        compiler_params=_params("parallel", "parallel"),
    )(c_all, ada_w, ada_b_cols)


def _ada_grad_adam(c_all_t, dmod_cols, w, m, v):
    n_l, d, a4 = w.shape
    tn = _tile(a4, 512)

    def body(ct_ref, dm_ref, w_ref, m_ref, v_ref, g_ref, dl_ref, nm_ref, nv_ref):
        ct = ct_ref[...]
        ca = ct * jax.nn.sigmoid(ct)
        dm = dm_ref[...]
        g = ca[:, 0:1] * dm[0:1, :]
        for dev in range(1, N_DEV):
            g = g + ca[:, dev:dev + 1] * dm[dev:dev + 1, :]
        g_ref[...] = g
        delta, nm, nv = _adamw(w_ref[...], g, m_ref[...], v_ref[...])
        dl_ref[...] = delta
        nm_ref[...] = nm
        nv_ref[...] = nv

    wspec = pl.BlockSpec((None, d, tn), lambda l, j: (l, 0, j))
    shp = jax.ShapeDtypeStruct(w.shape, F32)
    return pl.pallas_call(
        body, name="ada_grad_adam", grid=(n_l, a4 // tn),
        in_specs=[pl.BlockSpec((d, N_DEV), lambda l, j: (0, 0)),
                  pl.BlockSpec((None, N_DEV, tn), lambda l, j: (l, 0, j)), wspec, wspec, wspec],
        out_specs=[wspec] * 4, out_shape=[shp] * 4,
        compiler_params=_params("parallel", "parallel"),
    )(c_all_t, dmod_cols, w, m, v)


def _lnmod(x, g, sc, sh):
    s, d = x.shape
    tm = _tile(s, 512)

    def body(x_ref, g_ref, sc_ref, sh_ref, h_ref):
        xv = x_ref[...]
        r = lax.rsqrt(jnp.mean(xv * xv, axis=-1, keepdims=True) + EPS)
        h_ref[...] = ((xv * r * g_ref[...]) * (1.0 + sc_ref[...]) + sh_ref[...]).astype(BF16)

    vec = pl.BlockSpec((1, d), lambda i: (0, 0))
    row = pl.BlockSpec((tm, d), lambda i: (i, 0))
    return pl.pallas_call(
        body, name="lnmod", grid=(s // tm,), in_specs=[row, vec, vec, vec], out_specs=row,
        out_shape=jax.ShapeDtypeStruct((s, d), BF16), compiler_params=_params("parallel"),
    )(x, g, sc, sh)


def _mm_in(h, w_g, layer):
    s, d = h.shape
    n4 = w_g.shape[-1]
    tm = _tile(s, 512)

    def body(a_ref, b_ref, o_ref):
        o_ref[...] = _dot(a_ref[...], b_ref[...])

    return pl.pallas_call(
        body, name="mm_in", grid=(N_CHIP, s // tm),
        in_specs=[pl.BlockSpec((tm, d), lambda j, i: (i, 0)),
                  pl.BlockSpec((None, None, d, n4), lambda j, i: (j, layer, 0, 0))],
        out_specs=pl.BlockSpec((tm, n4), lambda j, i: (i, j)),
        out_shape=jax.ShapeDtypeStruct((s, N_CHIP * n4), F32),
        compiler_params=_params("parallel", "parallel"),
    )(h, w_g)


def _head_norm(x, gain, scale):
    r = lax.rsqrt(jnp.mean(x * x, axis=-1, keepdims=True) + EPS)
    return x * r * gain * scale


def _attn_fwd(p, qg, kg, d):
    s = p.shape[0]
    n_pairs = d // LANES
    n_qb = s // Q_BLOCK
    chunk = _tile(s, 512)
    inv_sqrt = 1.0 / math.sqrt(HEAD_DIM)

    def body(q_ref, k_ref, v_ref, qg_ref, kg_ref, o_ref, lt_ref, qs, ks, vs):
        row = lax.broadcasted_iota(jnp.int32, (Q_BLOCK, Q_BLOCK), 0)
        col = lax.broadcasted_iota(jnp.int32, (Q_BLOCK, Q_BLOCK), 1)
        later = (row > col).astype(BF16)
        causal = col < row

        for hh in range(2):
            lanes = slice(hh * HEAD_DIM, (hh + 1) * HEAD_DIM)

            def prep(r, _):
                rows = pl.ds(pl.multiple_of(r * chunk, chunk), chunk)
                qs[rows, :] = _head_norm(q_ref[rows, lanes], qg_ref[...], inv_sqrt).astype(BF16)
                ks[rows, :] = _head_norm(k_ref[rows, lanes], kg_ref[...], 1.0).astype(BF16)
                vs[rows, :] = v_ref[rows, lanes].astype(BF16)
                return 0

            lax.fori_loop(0, s // chunk, prep, 0)

            def q_block(i, _):
                rows_i = pl.ds(pl.multiple_of(i * Q_BLOCK, Q_BLOCK), Q_BLOCK)
                qi = qs[rows_i, :]

                def step(j, carry, masked):
                    run, acc = carry
                    rows_j = pl.ds(pl.multiple_of(j * Q_BLOCK, Q_BLOCK), Q_BLOCK)
                    z = _dot_nt(qi, ks[rows_j, :])
                    log_not = -_softplus(z)
                    if masked:
                        log_not = jnp.where(causal, log_not, 0.0)
                    excl = _cumdot(log_not, later)
                    log_a = z + log_not + excl + run
                    if masked:
                        log_a = jnp.where(causal, log_a, -1e30)
                    a = jnp.exp(log_a)
                    acc = acc + _dot(a.astype(BF16), vs[rows_j, :])
                    run = run + excl[:, 0:1] + log_not[:, 0:1]
                    return run, acc

                carry = (jnp.zeros((Q_BLOCK, 1), F32), jnp.zeros((Q_BLOCK, HEAD_DIM), F32))
                carry = step(i, carry, True)
                run, acc = lax.fori_loop(0, i, lambda n, cr: step(i - 1 - n, cr, False), carry)
                o_ref[rows_i, lanes] = acc.astype(BF16)
                lt_ref[rows_i, lanes] = jnp.broadcast_to(run, (Q_BLOCK, HEAD_DIM))
                return 0

            lax.fori_loop(0, n_qb, q_block, 0)

    def seg(k):
        return pl.BlockSpec((s, LANES), lambda h, k=k: (0, k * n_pairs + h))

    vec = pl.BlockSpec((1, HEAD_DIM), lambda h: (0, 0))
    out = pl.BlockSpec((s, LANES), lambda h: (0, h))
    return pl.pallas_call(
        body, name="attn_fwd", grid=(n_pairs,),
        in_specs=[seg(0), seg(1), seg(2), vec, vec], out_specs=[out, out],
        out_shape=[jax.ShapeDtypeStruct((s, d), BF16), jax.ShapeDtypeStruct((s, d), F32)],
        scratch_shapes=[pltpu.VMEM((s, HEAD_DIM), BF16)] * 3,
        compiler_params=_params("parallel"),
    )(p, p, p, qg, kg)


def _pair_mean(x, low):
    lo = jnp.sum(jnp.where(low, x, 0.0), axis=-1, keepdims=True)
    hi = jnp.sum(jnp.where(low, 0.0, x), axis=-1, keepdims=True)
    return jnp.where(low, lo, hi) * (1.0 / HEAD_DIM)


def _pair_norm(x, low):
    r = lax.rsqrt(_pair_mean(x * x, low) + EPS)
    return x * r, r


def _log_not(z):
    return jnp.minimum(-z, 0.0) - jnp.log(1.0 + jnp.exp(-jnp.abs(z)))


def _split_bf16(v):
    hi = v.astype(BF16)
    return hi, (v - hi.astype(F32)).astype(BF16)


def _attn_consts(inclusive):
    low = lax.broadcasted_iota(jnp.int32, (1, LANES), 1) < HEAD_DIM
    row = lax.broadcasted_iota(jnp.int32, (Q_BLOCK, Q_BLOCK), 0)
    col = lax.broadcasted_iota(jnp.int32, (Q_BLOCK, Q_BLOCK), 1)
    tri = (row <= col) if inclusive else (row > col)
    w2 = jnp.concatenate([tri.astype(BF16), jnp.ones((Q_BLOCK, Q_BLOCK), BF16)], axis=1)
    return low, col - row, w2


def _attn2_fwd(p, qg2, kg2, d):
    s = p.shape[0]
    n_pairs = d // LANES
    qsb = _tile(s, Q_SUPER)
    n_sub, n_sb = qsb // Q_BLOCK, s // qsb
    chunk = _tile(s, 512)
    inv_sqrt = 1.0 / math.sqrt(HEAD_DIM)

    def body(q_ref, k_ref, v_ref, qg_ref, kg_ref, o_ref, lt_ref, q0s, q1s, ks, vs, run, acc):
        low, diff, w2 = _attn_consts(False)
        qsh = (q0s, q1s)

        def prep(r, _):
            rows = pl.ds(pl.multiple_of(r * chunk, chunk), chunk)
            qn = _pair_norm(q_ref[rows, :], low)[0] * (qg_ref[...] * inv_sqrt)
            q0s[rows, :] = jnp.where(low, qn, 0.0).astype(BF16)
            q1s[rows, :] = jnp.where(low, 0.0, qn).astype(BF16)
            ks[rows, :] = (_pair_norm(k_ref[rows, :], low)[0] * kg_ref[...]).astype(BF16)
            vs[rows, :] = v_ref[rows, :].astype(BF16)
            return 0

        lax.fori_loop(0, s // chunk, prep, 0)

        def step(sb, j, masked):
            rows_sb = pl.ds(pl.multiple_of(sb * qsb, qsb), qsb)
            rows_j = pl.ds(pl.multiple_of(j * Q_BLOCK, Q_BLOCK), Q_BLOCK)
            kj, vj = ks[rows_j, :], vs[rows_j, :]
            res = []
            for h in range(2):
                z_all = _dot_nt(qsh[h][rows_sb, :], kj)
                zls, his, los, keeps = [], [], [], []
                for t in range(n_sub):
                    z = z_all[t * Q_BLOCK:(t + 1) * Q_BLOCK, :]
                    ln = _log_not(z)
                    if masked:
                        keeps.append(diff < (sb * qsb + t * Q_BLOCK - j * Q_BLOCK))
                        ln = jnp.where(keeps[t], ln, 0.0)
                    hi, lo = _split_bf16(ln)
                    zls.append(z + ln)
                    his.append(hi)
                    los.append(lo)
                c2 = _dot(jnp.concatenate(his, axis=0), w2) + _dot(jnp.concatenate(los, axis=0), w2)
                a_parts = []
                for t in range(n_sub):
                    sub = slice(t * Q_BLOCK, (t + 1) * Q_BLOCK)
                    later = run[h, sub, :]
                    log_a = zls[t] + c2[sub, :LANES] + later
                    if masked:
                        log_a = jnp.where(keeps[t], log_a, -1e30)
                    a_parts.append(jnp.exp(log_a).astype(BF16))
                    run[h, sub, :] = later + c2[sub, LANES:]
                res.append(_dot(jnp.concatenate(a_parts, axis=0), vj))
            acc[...] += jnp.where(low, res[0], res[1])

        def super_block(sb, _):
            run[...] = jnp.zeros_like(run)
            acc[...] = jnp.zeros_like(acc)

            def diag(n, _):
                step(sb, sb * n_sub + n_sub - 1 - n, True)
                return 0

            def below(n, _):
                step(sb, sb * n_sub - 1 - n, False)
                return 0

            lax.fori_loop(0, n_sub, diag, 0)
            lax.fori_loop(0, sb * n_sub, below, 0)
            rows_sb = pl.ds(pl.multiple_of(sb * qsb, qsb), qsb)
            o_ref[rows_sb, :] = acc[...].astype(BF16)
            lt_ref[rows_sb, :] = jnp.where(low, run[0], run[1])
            return 0

        lax.fori_loop(0, n_sb, super_block, 0)

    def seg(k):
        return pl.BlockSpec((s, LANES), lambda h, k=k: (0, k * n_pairs + h))

    vec = pl.BlockSpec((1, LANES), lambda h: (0, 0))
    out = pl.BlockSpec((s, LANES), lambda h: (0, h))
    return pl.pallas_call(
        body, name="attn_fwd", grid=(n_pairs,),
        in_specs=[seg(0), seg(1), seg(2), vec, vec], out_specs=[out, out],
        out_shape=[jax.ShapeDtypeStruct((s, d), BF16), jax.ShapeDtypeStruct((s, d), F32)],
        scratch_shapes=[pltpu.VMEM((s, LANES), BF16)] * 4
        + [pltpu.VMEM((2, qsb, LANES), F32), pltpu.VMEM((qsb, LANES), F32)],
        compiler_params=_params("parallel"),
    )(p, p, p, qg2, kg2)


def _attn2_bwd(p, qg2, kg2, dy, lt, d):
    s = p.shape[0]
    n_pairs = d // LANES
    qsb = _tile(s, Q_SUPER)
    n_sub, n_sb, n_kb = qsb // Q_BLOCK, s // qsb, s // Q_BLOCK
    chunk = _tile(s, 512)
    inv_sqrt = 1.0 / math.sqrt(HEAD_DIM)

    def body(q_ref, k_ref, v_ref, qg_ref, kg_ref, dy_ref, lt_ref, dq_ref, dk_ref, dv_ref, dgain_ref,
             q0s, q1s, ks, vs, dy0s, dy1s, dkt, dvt, qt, dyt, rem, gbef, dqa):
        low, diff, w2 = _attn_consts(True)
        qsh, dysh = (q0s, q1s), (dy0s, dy1s)

        def prep(r, _):
            rows = pl.ds(pl.multiple_of(r * chunk, chunk), chunk)
            qn = _pair_norm(q_ref[rows, :], low)[0] * (qg_ref[...] * inv_sqrt)
            q0s[rows, :] = jnp.where(low, qn, 0.0).astype(BF16)
            q1s[rows, :] = jnp.where(low, 0.0, qn).astype(BF16)
            ks[rows, :] = (_pair_norm(k_ref[rows, :], low)[0] * kg_ref[...]).astype(BF16)
            vs[rows, :] = v_ref[rows, :].astype(BF16)
            dyv = dy_ref[rows, :]
            dy0s[rows, :] = jnp.where(low, dyv, jnp.zeros_like(dyv))
            dy1s[rows, :] = jnp.where(low, jnp.zeros_like(dyv), dyv)
            return 0

        lax.fori_loop(0, s // chunk, prep, 0)

        def clear(b, _):
            dkt[b] = jnp.zeros((LANES, Q_BLOCK), F32)
            dvt[b] = jnp.zeros((LANES, Q_BLOCK), F32)
            return 0

        lax.fori_loop(0, n_kb, clear, 0)

        def step(sb, j, masked):
            rows_sb = pl.ds(pl.multiple_of(sb * qsb, qsb), qsb)
            rows_j = pl.ds(pl.multiple_of(j * Q_BLOCK, Q_BLOCK), Q_BLOCK)
            kj, vj = ks[rows_j, :], vs[rows_j, :]
            dq_res = []
            for h in range(2):
                z_all = _dot_nt(qsh[h][rows_sb, :], kj)
                da_all = _dot_nt(dysh[h][rows_sb, :], vj)
                zls, his, los, keeps = [], [], [], []
                for t in range(n_sub):
                    z = z_all[t * Q_BLOCK:(t + 1) * Q_BLOCK, :]
                    ln = _log_not(z)
                    if masked:
                        keeps.append(diff < (sb * qsb + t * Q_BLOCK - j * Q_BLOCK))
                        ln = jnp.where(keeps[t], ln, 0.0)
                    hi, lo = _split_bf16(ln)
                    zls.append(z + ln)
                    his.append(hi)
                    los.append(lo)
                c2 = _dot(jnp.concatenate(his, axis=0), w2) + _dot(jnp.concatenate(los, axis=0), w2)
                a_parts, gs, his, los = [], [], [], []
                for t in range(n_sub):
                    sub = slice(t * Q_BLOCK, (t + 1) * Q_BLOCK)
                    left = rem[h, sub, :]
                    log_a = zls[t] + (left - c2[sub, :LANES])
                    if masked:
                        log_a = jnp.where(keeps[t], log_a, -1e30)
                    a = jnp.exp(log_a)
                    rem[h, sub, :] = left - c2[sub, LANES:]
                    g = a * da_all[sub, :]
                    hi, lo = _split_bf16(g)
                    a_parts.append(a.astype(BF16))
                    gs.append(g)
                    his.append(hi)
                    los.append(lo)
                c2g = _dot(jnp.concatenate(his, axis=0), w2) + _dot(jnp.concatenate(los, axis=0), w2)
                dz_parts = []
                for t in range(n_sub):
                    sub = slice(t * Q_BLOCK, (t + 1) * Q_BLOCK)
                    before = gbef[h, sub, :]
                    dz = gs[t] - jnp.exp(zls[t]) * (before + c2g[sub, :LANES])
                    if masked:
                        dz = jnp.where(keeps[t], dz, 0.0)
                    gbef[h, sub, :] = before + c2g[sub, LANES:]
                    dz_parts.append(dz.astype(BF16))
                a_all = jnp.concatenate(a_parts, axis=0)
                dz_all = jnp.concatenate(dz_parts, axis=0)
                dvt[j] += _dot(dyt[h], a_all)
                dkt[j] += _dot(qt[h], dz_all)
                dq_res.append(_dot(dz_all, kj))
            dqa[...] += jnp.where(low, dq_res[0], dq_res[1])

        def super_block(sb, dqg):
            rows_sb = pl.ds(pl.multiple_of(sb * qsb, qsb), qsb)
            total = lt_ref[rows_sb, :]
            other = pltpu.roll(total, HEAD_DIM, 1)
            rem[0] = jnp.where(low, total, other)
            rem[1] = jnp.where(low, other, total)
            gbef[...] = jnp.zeros_like(gbef)
            dqa[...] = jnp.zeros_like(dqa)
            for h in range(2):
                qt[h] = qsh[h][rows_sb, :].astype(F32).T.astype(BF16)
                dyt[h] = dysh[h][rows_sb, :].astype(F32).T.astype(BF16)

            def below(j, _):
                step(sb, j, False)
                return 0

            def diag(n, _):
                step(sb, sb * n_sub + n, True)
                return 0

            lax.fori_loop(0, sb * n_sub, below, 0)
            lax.fori_loop(0, n_sub, diag, 0)
            qhat, r = _pair_norm(q_ref[rows_sb, :], low)
            dqn = dqa[...]
            dqhat = dqn * (qg_ref[...] * inv_sqrt)
            dq_ref[rows_sb, :] = (r * (dqhat - qhat * _pair_mean(dqhat * qhat, low))).astype(BF16)
            return dqg + jnp.sum(dqn * qhat, axis=0, keepdims=True) * inv_sqrt

        dqg = lax.fori_loop(0, n_sb, super_block, jnp.zeros((1, LANES), F32))

        def finish(b, dkg):
            rows = pl.ds(pl.multiple_of(b * Q_BLOCK, Q_BLOCK), Q_BLOCK)
            khat, rk = _pair_norm(k_ref[rows, :], low)
            dkn = dkt[b].T
            dkhat = dkn * kg_ref[...]
            dk_ref[rows, :] = (rk * (dkhat - khat * _pair_mean(dkhat * khat, low))).astype(BF16)
            dv_ref[rows, :] = dvt[b].T.astype(BF16)
            return dkg + jnp.sum(dkn * khat, axis=0, keepdims=True)

        dkg = lax.fori_loop(0, n_kb, finish, jnp.zeros((1, LANES), F32))
        dgain_ref[...] = jnp.zeros_like(dgain_ref)
        dgain_ref[0:1, :] = dqg
        dgain_ref[1:2, :] = dkg

    def seg(k):
        return pl.BlockSpec((s, LANES), lambda h, k=k: (0, k * n_pairs + h))

    vec = pl.BlockSpec((1, LANES), lambda h: (0, 0))
    col = pl.BlockSpec((s, LANES), lambda h: (0, h))
    shp = jax.ShapeDtypeStruct((s, d), BF16)
    return pl.pallas_call(
        body, name="attn_bwd", grid=(n_pairs,),
        in_specs=[seg(0), seg(1), seg(2), vec, vec, col, col],
        out_specs=[col, col, col, pl.BlockSpec((None, 8, LANES), lambda h: (h, 0, 0))],
        out_shape=[shp, shp, shp, jax.ShapeDtypeStruct((n_pairs, 8, LANES), F32)],
        scratch_shapes=[pltpu.VMEM((s, LANES), BF16)] * 6
        + [pltpu.VMEM((n_kb, LANES, Q_BLOCK), F32)] * 2
        + [pltpu.VMEM((2, LANES, qsb), BF16)] * 2
        + [pltpu.VMEM((2, qsb, LANES), F32)] * 2 + [pltpu.VMEM((qsb, LANES), F32)],
        compiler_params=_params("parallel"),
    )(p, p, p, qg2, kg2, dy, lt)


def _attn3_consts(inclusive):
    low = lax.broadcasted_iota(jnp.int32, (1, LANES), 1) < HEAD_DIM
    row = lax.broadcasted_iota(jnp.int32, (Q_BLOCK, Q_BLOCK), 0)
    col = lax.broadcasted_iota(jnp.int32, (Q_BLOCK, Q_BLOCK), 1)
    tri = (row <= col) if inclusive else (row > col)
    w2 = jnp.concatenate([tri.astype(BF16), jnp.ones((Q_BLOCK, Q_BLOCK), BF16)], axis=1)
    return low, col - row, jnp.concatenate([w2, w2], axis=0)


def _split_cat(v):
    hi = v.astype(BF16)
    return jnp.concatenate([hi, (v - hi.astype(F32)).astype(BF16)], axis=1)


def _fill_pair_blocks(dst, src_fn, low, n_kb):
    def fill(b, _):
        v = src_fn(pl.ds(pl.multiple_of(b * Q_BLOCK, Q_BLOCK), Q_BLOCK))
        dst[b, 0:Q_BLOCK, :] = jnp.where(low, v, 0.0).astype(BF16)
        dst[b, Q_BLOCK:2 * Q_BLOCK, :] = jnp.where(low, 0.0, v).astype(BF16)
        return 0

    lax.fori_loop(0, n_kb, fill, 0)


def _attn3_fwd(p, qg2, kg2, d):
    s = p.shape[0]
    n_pairs = d // LANES
    qsb = _tile(s, Q_SUPER)
    n_sub, n_sb, n_kb = qsb // Q_BLOCK, s // qsb, s // Q_BLOCK
    chunk = _tile(s, 512)
    inv_sqrt = 1.0 / math.sqrt(HEAD_DIM)

    def body(q_ref, k_ref, v_ref, qg_ref, kg_ref, o_ref, lt_ref, qs, k2, v2, run, acc):
        low, diff, w4 = _attn3_consts(False)

        def prep(r, _):
            rows = pl.ds(pl.multiple_of(r * chunk, chunk), chunk)
            qs[rows, :] = (_pair_norm(q_ref[rows, :], low)[0] * (qg_ref[...] * inv_sqrt)).astype(BF16)
            return 0

        lax.fori_loop(0, s // chunk, prep, 0)
        _fill_pair_blocks(k2, lambda rows: _pair_norm(k_ref[rows, :], low)[0] * kg_ref[...], low, n_kb)
        _fill_pair_blocks(v2, lambda rows: v_ref[rows, :], low, n_kb)

        def step(sb, j, masked):
            rows_sb = pl.ds(pl.multiple_of(sb * qsb, qsb), qsb)
            z_both = _dot_nt(qs[rows_sb, :], k2[j])
            zls, cats, keeps = [], [], []
            for t in range(n_sub):
                sub = slice(t * Q_BLOCK, (t + 1) * Q_BLOCK)
                keep = (diff < (sb * qsb + t * Q_BLOCK - j * Q_BLOCK)) if masked else None
                for h in range(2):
                    z = z_both[sub, h * LANES:(h + 1) * LANES]
                    ln = _log_not(z)
                    if masked:
                        ln = jnp.where(keep, ln, 0.0)
                    zls.append(z + ln)
                    cats.append(_split_cat(ln))
                    keeps.append(keep)
            c2 = _dot(jnp.concatenate(cats, axis=0), w4)
            a_rows = []
            for t in range(n_sub):
                sub = slice(t * Q_BLOCK, (t + 1) * Q_BLOCK)
                a_pair = []
                for h in range(2):
                    i = 2 * t + h
                    tile = slice(i * Q_BLOCK, (i + 1) * Q_BLOCK)
                    later = run[h, sub, :]
                    log_a = zls[i] + c2[tile, :LANES] + later
                    if masked:
                        log_a = jnp.where(keeps[i], log_a, -1e30)
                    a_pair.append(jnp.exp(log_a).astype(BF16))
                    run[h, sub, :] = later + c2[tile, LANES:]
                a_rows.append(jnp.concatenate(a_pair, axis=1))
            acc[...] += _dot(jnp.concatenate(a_rows, axis=0), v2[j])

        def super_block(sb, _):
            run[...] = jnp.zeros_like(run)
            acc[...] = jnp.zeros_like(acc)

            def diag(n, _):
                step(sb, sb * n_sub + n_sub - 1 - n, True)
                return 0

            def below(n, _):
                step(sb, sb * n_sub - 1 - n, False)
                return 0

            lax.fori_loop(0, n_sub, diag, 0)
            lax.fori_loop(0, sb * n_sub, below, 0)
            rows_sb = pl.ds(pl.multiple_of(sb * qsb, qsb), qsb)
            o_ref[rows_sb, :] = acc[...].astype(BF16)
            lt_ref[rows_sb, :] = jnp.where(low, run[0], run[1])
            return 0

        lax.fori_loop(0, n_sb, super_block, 0)

    def seg(k):
        return pl.BlockSpec((s, LANES), lambda h, k=k: (0, k * n_pairs + h))

    vec = pl.BlockSpec((1, LANES), lambda h: (0, 0))
    out = pl.BlockSpec((s, LANES), lambda h: (0, h))
    return pl.pallas_call(
        body, name="attn_fwd", grid=(n_pairs,),
        in_specs=[seg(0), seg(1), seg(2), vec, vec], out_specs=[out, out],
        out_shape=[jax.ShapeDtypeStruct((s, d), BF16), jax.ShapeDtypeStruct((s, d), F32)],
        scratch_shapes=[pltpu.VMEM((s, LANES), BF16)] + [pltpu.VMEM((n_kb, 2 * Q_BLOCK, LANES), BF16)] * 2
        + [pltpu.VMEM((2, qsb, LANES), F32), pltpu.VMEM((qsb, LANES), F32)],
        compiler_params=_params("parallel"),
    )(p, p, p, qg2, kg2)


def _attn3_bwd(p, qg2, kg2, dy, lt, d):
    s = p.shape[0]
    n_pairs = d // LANES
    qsb = _tile(s, Q_SUPER)
    n_sub, n_sb, n_kb = qsb // Q_BLOCK, s // qsb, s // Q_BLOCK
    chunk = _tile(s, 512)
    inv_sqrt = 1.0 / math.sqrt(HEAD_DIM)

    def body(q_ref, k_ref, v_ref, qg_ref, kg_ref, dy_ref, lt_ref, dq_ref, dk_ref, dv_ref, dgain_ref,
             qs, k2, v2, dkt, dvt, qt, dyt, rem, gbef, dqa):
        low, diff, w4 = _attn3_consts(True)

        def prep(r, _):
            rows = pl.ds(pl.multiple_of(r * chunk, chunk), chunk)
            qs[rows, :] = (_pair_norm(q_ref[rows, :], low)[0] * (qg_ref[...] * inv_sqrt)).astype(BF16)
            return 0

        lax.fori_loop(0, s // chunk, prep, 0)
        _fill_pair_blocks(k2, lambda rows: _pair_norm(k_ref[rows, :], low)[0] * kg_ref[...], low, n_kb)
        _fill_pair_blocks(v2, lambda rows: v_ref[rows, :], low, n_kb)

        def clear(b, _):
            dkt[b] = jnp.zeros((LANES, Q_BLOCK), F32)
            dvt[b] = jnp.zeros((LANES, Q_BLOCK), F32)
            return 0

        lax.fori_loop(0, n_kb, clear, 0)

        def step(sb, j, masked):
            rows_sb = pl.ds(pl.multiple_of(sb * qsb, qsb), qsb)
            kj2, vj2 = k2[j], v2[j]
            z_both = _dot_nt(qs[rows_sb, :], kj2)
            da_both = _dot_nt(dy_ref[rows_sb, :], vj2)
            zls, cats, keeps = [], [], []
            for t in range(n_sub):
                sub = slice(t * Q_BLOCK, (t + 1) * Q_BLOCK)
                keep = (diff < (sb * qsb + t * Q_BLOCK - j * Q_BLOCK)) if masked else None
                for h in range(2):
                    z = z_both[sub, h * LANES:(h + 1) * LANES]
                    ln = _log_not(z)
                    if masked:
                        ln = jnp.where(keep, ln, 0.0)
                    zls.append(z + ln)
                    cats.append(_split_cat(ln))
                    keeps.append(keep)
            c2 = _dot(jnp.concatenate(cats, axis=0), w4)
            a_rows, gs, cats = [], [], []
            for t in range(n_sub):
                sub = slice(t * Q_BLOCK, (t + 1) * Q_BLOCK)
                a_pair = []
                for h in range(2):
                    i = 2 * t + h
                    tile = slice(i * Q_BLOCK, (i + 1) * Q_BLOCK)
                    left = rem[h, sub, :]
                    log_a = zls[i] + (left - c2[tile, :LANES])
                    if masked:
                        log_a = jnp.where(keeps[i], log_a, -1e30)
                    a = jnp.exp(log_a)
                    rem[h, sub, :] = left - c2[tile, LANES:]
                    g = a * da_both[sub, h * LANES:(h + 1) * LANES]
                    a_pair.append(a.astype(BF16))
                    gs.append(g)
                    cats.append(_split_cat(g))
                a_rows.append(jnp.concatenate(a_pair, axis=1))
            c2g = _dot(jnp.concatenate(cats, axis=0), w4)
            dz_rows = []
            for t in range(n_sub):
                sub = slice(t * Q_BLOCK, (t + 1) * Q_BLOCK)
                dz_pair = []
                for h in range(2):
                    i = 2 * t + h
                    tile = slice(i * Q_BLOCK, (i + 1) * Q_BLOCK)
                    before = gbef[h, sub, :]
                    dz = gs[i] - jnp.exp(zls[i]) * (before + c2g[tile, :LANES])
                    if masked:
                        dz = jnp.where(keeps[i], dz, 0.0)
                    gbef[h, sub, :] = before + c2g[tile, LANES:]
                    dz_pair.append(dz.astype(BF16))
                dz_rows.append(jnp.concatenate(dz_pair, axis=1))
            a_both = jnp.concatenate(a_rows, axis=0)
            dz_both = jnp.concatenate(dz_rows, axis=0)
            dvt[j] += _dot(dyt[0], a_both[:, :LANES]) + _dot(dyt[1], a_both[:, LANES:])
            dkt[j] += _dot(qt[0], dz_both[:, :LANES]) + _dot(qt[1], dz_both[:, LANES:])
            dqa[...] += _dot(dz_both, kj2)

        def super_block(sb, dqg):
            rows_sb = pl.ds(pl.multiple_of(sb * qsb, qsb), qsb)
            total = lt_ref[rows_sb, :]
            other = pltpu.roll(total, HEAD_DIM, 1)
            rem[0] = jnp.where(low, total, other)
            rem[1] = jnp.where(low, other, total)
            gbef[...] = jnp.zeros_like(gbef)
            dqa[...] = jnp.zeros_like(dqa)
            qv = qs[rows_sb, :].astype(F32)
            dyv = dy_ref[rows_sb, :].astype(F32)
            qt[0] = jnp.where(low, qv, 0.0).T.astype(BF16)
            qt[1] = jnp.where(low, 0.0, qv).T.astype(BF16)
            dyt[0] = jnp.where(low, dyv, 0.0).T.astype(BF16)
            dyt[1] = jnp.where(low, 0.0, dyv).T.astype(BF16)

            def below(j, _):
                step(sb, j, False)
                return 0

            def diag(n, _):
                step(sb, sb * n_sub + n, True)
                return 0

            lax.fori_loop(0, sb * n_sub, below, 0)
            lax.fori_loop(0, n_sub, diag, 0)
            qhat, r = _pair_norm(q_ref[rows_sb, :], low)
            dqn = dqa[...]
            dqhat = dqn * (qg_ref[...] * inv_sqrt)
            dq_ref[rows_sb, :] = (r * (dqhat - qhat * _pair_mean(dqhat * qhat, low))).astype(BF16)
            return dqg + jnp.sum(dqn * qhat, axis=0, keepdims=True) * inv_sqrt

        dqg = lax.fori_loop(0, n_sb, super_block, jnp.zeros((1, LANES), F32))

        def finish(b, dkg):
            rows = pl.ds(pl.multiple_of(b * Q_BLOCK, Q_BLOCK), Q_BLOCK)
            khat, rk = _pair_norm(k_ref[rows, :], low)
            dkn = dkt[b].T
            dkhat = dkn * kg_ref[...]
            dk_ref[rows, :] = (rk * (dkhat - khat * _pair_mean(dkhat * khat, low))).astype(BF16)
            dv_ref[rows, :] = dvt[b].T.astype(BF16)
            return dkg + jnp.sum(dkn * khat, axis=0, keepdims=True)

        dkg = lax.fori_loop(0, n_kb, finish, jnp.zeros((1, LANES), F32))
        dgain_ref[...] = jnp.zeros_like(dgain_ref)
        dgain_ref[0:1, :] = dqg
        dgain_ref[1:2, :] = dkg

    def seg(k):
        return pl.BlockSpec((s, LANES), lambda h, k=k: (0, k * n_pairs + h))

    vec = pl.BlockSpec((1, LANES), lambda h: (0, 0))
    col = pl.BlockSpec((s, LANES), lambda h: (0, h))
    shp = jax.ShapeDtypeStruct((s, d), BF16)
    return pl.pallas_call(
        body, name="attn_bwd", grid=(n_pairs,),
        in_specs=[seg(0), seg(1), seg(2), vec, vec, col, col],
        out_specs=[col, col, col, pl.BlockSpec((None, 8, LANES), lambda h: (h, 0, 0))],
        out_shape=[shp, shp, shp, jax.ShapeDtypeStruct((n_pairs, 8, LANES), F32)],
        scratch_shapes=[pltpu.VMEM((s, LANES), BF16)] + [pltpu.VMEM((n_kb, 2 * Q_BLOCK, LANES), BF16)] * 2
        + [pltpu.VMEM((n_kb, LANES, Q_BLOCK), F32)] * 2
        + [pltpu.VMEM((2, LANES, qsb), BF16)] * 2
        + [pltpu.VMEM((2, qsb, LANES), F32)] * 2 + [pltpu.VMEM((qsb, LANES), F32)],
        compiler_params=_params("parallel"),
    )(p, p, p, qg2, kg2, dy, lt)


def _conv_rows(s):
    return _tile(s, 512)


def _conv_fwd(p, conv_w, d):
    s = p.shape[0]
    nb = d // LANES
    rows_n = _conv_rows(s)

    def body(cb_ref, cc_ref, cx_ref, w_ref, y_ref, us):
        us[pl.ds(0, 8), :] = jnp.zeros((8, LANES), F32)

        def fill(r, _):
            rows = pl.ds(pl.multiple_of(r * rows_n, rows_n), rows_n)
            us[pl.ds(pl.multiple_of(r * rows_n + 8, 8), rows_n), :] = cc_ref[rows, :] * cx_ref[rows, :]
            return 0

        lax.fori_loop(0, s // rows_n, fill, 0)
        w = w_ref[...]

        def out(r, _):
            rows = pl.ds(pl.multiple_of(r * rows_n, rows_n), rows_n)
            ext = us[pl.ds(pl.multiple_of(r * rows_n, 8), rows_n + 8), :]
            cv = (w[0:1, :] * pltpu.roll(ext, 2, 0)[8:, :] + w[1:2, :] * pltpu.roll(ext, 1, 0)[8:, :]
                  + w[2:3, :] * ext[8:, :])
            y_ref[rows, :] = (cb_ref[rows, :] * cv).astype(BF16)
            return 0

        lax.fori_loop(0, s // rows_n, out, 0)

    def seg(k):
        return pl.BlockSpec((s, LANES), lambda b, k=k: (0, k * nb + b))

    return pl.pallas_call(
        body, name="conv_fwd", grid=(nb,),
        in_specs=[seg(3), seg(4), seg(5), pl.BlockSpec((3, LANES), lambda b: (0, b))],
        out_specs=pl.BlockSpec((s, LANES), lambda b: (0, b)),
        out_shape=jax.ShapeDtypeStruct((s, d), BF16),
        scratch_shapes=[pltpu.VMEM((s + 8, LANES), F32)],
        compiler_params=_params("parallel"),
    )(p, p, p, conv_w)


def _branch(ya, yb, p, wa, wb, d):
    s = ya.shape[0]
    tm = _tile(s, 512)

    def body(ya_ref, yb_ref, ga_ref, gb_ref, wa_ref, wb_ref, m_ref, a_ref, b_ref):
        pa = _dot(ya_ref[...], wa_ref[...])
        pb = _dot(yb_ref[...], wb_ref[...])
        m_ref[...] = (jax.nn.sigmoid(ga_ref[...]) * pa + jax.nn.sigmoid(gb_ref[...]) * pb).astype(BF16)
        a_ref[...] = pa.astype(BF16)
        b_ref[...] = pb.astype(BF16)

    row = pl.BlockSpec((tm, d), lambda i: (i, 0))
    wsp = pl.BlockSpec((d, d), lambda i: (0, 0))
    shp = jax.ShapeDtypeStruct((s, d), BF16)
    return pl.pallas_call(
        body, name="branch", grid=(s // tm,),
        in_specs=[row, row, pl.BlockSpec((tm, d), lambda i: (i, 6)), pl.BlockSpec((tm, d), lambda i: (i, 7)), wsp, wsp],
        out_specs=[row, row, row], out_shape=[shp, shp, shp], compiler_params=_params("parallel"),
    )(ya, yb, p, p, wa, wb)


def _out_proj(merged, wout, x0, g1):
    s, d = x0.shape
    tm = _tile(s, 512)

    def body(m_ref, w_ref, x_ref, g_ref, x1_ref, mo_ref):
        mo = _dot(m_ref[...], w_ref[...])
        mo_ref[...] = mo
        x1_ref[...] = x_ref[...] + g_ref[...] * mo

    row = pl.BlockSpec((tm, d), lambda i: (i, 0))
    shp = jax.ShapeDtypeStruct((s, d), F32)
    return pl.pallas_call(
        body, name="out_proj", grid=(s // tm,),
        in_specs=[row, pl.BlockSpec((d, d), lambda i: (0, 0)), row, pl.BlockSpec((1, d), lambda i: (0, 0))],
        out_specs=[row, row], out_shape=[shp, shp], compiler_params=_params("parallel"),
    )(merged, wout, x0, g1)


def _ffn_up(h, wg_g, wu_g, layer):
    s, d = h.shape
    f4 = wg_g.shape[-1]
    tm = _tile(s, 512)

    def body(h_ref, wg_ref, wu_ref, gate_ref, up_ref, act_ref):
        hv = h_ref[...]
        gt = _dot(hv, wg_ref[...])
        up = _dot(hv, wu_ref[...])
        gate_ref[...] = gt.astype(BF16)
        up_ref[...] = up.astype(BF16)
        act_ref[...] = (gt * jax.nn.sigmoid(gt) * up).astype(BF16)

    wsp = pl.BlockSpec((None, None, d, f4), lambda j, i: (j, layer, 0, 0))
    osp = pl.BlockSpec((None, tm, f4), lambda j, i: (j, i, 0))
    shp = jax.ShapeDtypeStruct((N_CHIP, s, f4), BF16)
    return pl.pallas_call(
        body, name="ffn_up", grid=(N_CHIP, s // tm),
        in_specs=[pl.BlockSpec((tm, d), lambda j, i: (i, 0)), wsp, wsp],
        out_specs=[osp, osp, osp], out_shape=[shp, shp, shp], compiler_params=_params("parallel", "parallel"),
    )(h, wg_g, wu_g)


def _ffn_down(act, wd_g, x1, g2, layer):
    s, d = x1.shape
    f4 = act.shape[-1]
    tm = _tile(s, 512)

    def body(a_ref, w_ref, x_ref, g_ref, x2_ref, f_ref, acc):
        j = pl.program_id(1)

        @pl.when(j == 0)
        def _():
            acc[...] = jnp.zeros_like(acc)

        acc[...] += _dot(a_ref[...], w_ref[...])

        @pl.when(j == N_CHIP - 1)
        def _():
            f = acc[...]
            f_ref[...] = f
            x2_ref[...] = x_ref[...] + g_ref[...] * f

    row = pl.BlockSpec((tm, d), lambda i, j: (i, 0))
    shp = jax.ShapeDtypeStruct((s, d), F32)
    return pl.pallas_call(
        body, name="ffn_down", grid=(s // tm, N_CHIP),
        in_specs=[pl.BlockSpec((None, tm, f4), lambda i, j: (j, i, 0)),
                  pl.BlockSpec((None, None, f4, d), lambda i, j: (j, layer, 0, 0)),
                  row, pl.BlockSpec((1, d), lambda i, j: (0, 0))],
        out_specs=[row, row], out_shape=[shp, shp],
        scratch_shapes=[pltpu.VMEM((tm, d), F32)], compiler_params=_params("parallel", "arbitrary"),
    )(act, wd_g, x1, g2)


def _loss_head(y, target):
    s, d = y.shape
    tm = _tile(s, 512)

    def body(y_ref, t_ref, dy_ref, l_ref, acc):
        i = pl.program_id(0)

        @pl.when(i == 0)
        def _():
            acc[...] = jnp.zeros_like(acc)

        err = y_ref[...] - t_ref[...]
        dy_ref[...] = err / d
        acc[...] += jnp.sum(err * err, axis=0, keepdims=True)

        @pl.when(i == pl.num_programs(0) - 1)
        def _():
            l_ref[...] = jnp.broadcast_to(jnp.sum(acc[...], axis=1, keepdims=True), (8, LANES))

    row = pl.BlockSpec((tm, d), lambda i: (i, 0))
    return pl.pallas_call(
        body, name="loss_head", grid=(s // tm,), in_specs=[row, row],
        out_specs=[row, pl.BlockSpec((8, LANES), lambda i: (0, 0))],
        out_shape=[jax.ShapeDtypeStruct((s, d), F32), jax.ShapeDtypeStruct((8, LANES), F32)],
        scratch_shapes=[pltpu.VMEM((1, d), F32)], compiler_params=_params("arbitrary"),
    )(y, target)


def _mm_tn(a, b, a_spec, b_spec, out_rc, name):
    r, c = out_rc
    s = a.shape[-2]
    tk = _tile(s, 512)
    nk = s // tk

    def body(a_ref, b_ref, o_ref, acc):
        k = pl.program_id(1)

        @pl.when(k == 0)
        def _():
            acc[...] = jnp.zeros_like(acc)

        acc[...] += _dot_tn(a_ref[...], b_ref[...])

        @pl.when(k == nk - 1)
        def _():
            o_ref[...] = acc[...].astype(BF16)

    return pl.pallas_call(
        body, name=name, grid=(N_CHIP, nk),
        in_specs=[pl.BlockSpec(*a_spec(tk)), pl.BlockSpec(*b_spec(tk))],
        out_specs=pl.BlockSpec((None, r, c), lambda j, k: (j, 0, 0)),
        out_shape=jax.ShapeDtypeStruct((N_CHIP, r, c), BF16),
        scratch_shapes=[pltpu.VMEM((r, c), F32)], compiler_params=_params("parallel", "arbitrary"),
    )(a, b)


def _ffn_bwd1(dx2, f, g2, wd_g, gate, up, layer):
    s, d = dx2.shape
    f4 = gate.shape[-1]
    tm = _tile(s, 512)

    def body(dx_ref, f_ref, g_ref, w_ref, gate_ref, up_ref, dgate_ref, dup_ref, df_ref, dg_ref):
        i, j = pl.program_id(0), pl.program_id(1)

        @pl.when((i == 0) & (j == 0))
        def _():
            dg_ref[...] = jnp.zeros_like(dg_ref)

        dxv = dx_ref[...]
        df = (g_ref[...] * dxv).astype(BF16)

        @pl.when(j == 0)
        def _():
            df_ref[...] = df
            dg_ref[0:1, :] += jnp.sum(dxv * f_ref[...], axis=0, keepdims=True)

        da = _dot_nt(df, w_ref[...])
        gt = gate_ref[...].astype(F32)
        sg = jax.nn.sigmoid(gt)
        dup_ref[...] = (da * gt * sg).astype(BF16)
        dgate_ref[...] = (da * up_ref[...].astype(F32) * (sg * (1.0 + gt * (1.0 - sg)))).astype(BF16)

    row = pl.BlockSpec((tm, d), lambda i, j: (i, 0))
    hsp = pl.BlockSpec((None, tm, f4), lambda i, j: (j, i, 0))
    hshp = jax.ShapeDtypeStruct((N_CHIP, s, f4), BF16)
    return pl.pallas_call(
        body, name="ffn_bwd1", grid=(s // tm, N_CHIP),
        in_specs=[row, row, pl.BlockSpec((1, d), lambda i, j: (0, 0)),
                  pl.BlockSpec((None, None, f4, d), lambda i, j: (j, layer, 0, 0)), hsp, hsp],
        out_specs=[hsp, hsp, row, pl.BlockSpec((8, d), lambda i, j: (0, 0))],
        out_shape=[hshp, hshp, jax.ShapeDtypeStruct((s, d), BF16), jax.ShapeDtypeStruct((8, d), F32)],
        compiler_params=_params("arbitrary", "arbitrary"),
    )(dx2, f, g2, wd_g, gate, up)


def _ffn_bwd2(dgate, dup, wg_g, wu_g, layer):
    _, s, f4 = dgate.shape
    d = wg_g.shape[-2]
    tm = _tile(s, 512)

    def body(dg_ref, du_ref, wg_ref, wu_ref, o_ref, acc):
        j = pl.program_id(1)

        @pl.when(j == 0)
        def _():
            acc[...] = jnp.zeros_like(acc)

        acc[...] += _dot_nt(dg_ref[...], wg_ref[...]) + _dot_nt(du_ref[...], wu_ref[...])

        @pl.when(j == N_CHIP - 1)
        def _():
            o_ref[...] = acc[...]

    hsp = pl.BlockSpec((None, tm, f4), lambda i, j: (j, i, 0))
    wsp = pl.BlockSpec((None, None, d, f4), lambda i, j: (j, layer, 0, 0))
    return pl.pallas_call(
        body, name="ffn_bwd2", grid=(s // tm, N_CHIP), in_specs=[hsp, hsp, wsp, wsp],
        out_specs=pl.BlockSpec((tm, d), lambda i, j: (i, 0)), out_shape=jax.ShapeDtypeStruct((s, d), F32),
        scratch_shapes=[pltpu.VMEM((tm, d), F32)], compiler_params=_params("parallel", "arbitrary"),
    )(dgate, dup, wg_g, wu_g)


def _lnmod_bwd(x, g, sc, sh, dh, dres):
    s, d = x.shape
    tm = _tile(s, 512)

    def body(x_ref, g_ref, sc_ref, dh_ref, dr_ref, dx_ref, sums_ref):
        @pl.when(pl.program_id(0) == 0)
        def _():
            sums_ref[...] = jnp.zeros_like(sums_ref)

        xv, dhv, gv = x_ref[...], dh_ref[...], g_ref[...]
        r = lax.rsqrt(jnp.mean(xv * xv, axis=-1, keepdims=True) + EPS)
        n = xv * r
        one_sc = 1.0 + sc_ref[...]
        dt = dhv * one_sc
        sums_ref[0:1, :] += jnp.sum(dhv, axis=0, keepdims=True)
        sums_ref[1:2, :] += jnp.sum(dhv * (n * gv), axis=0, keepdims=True)
        sums_ref[2:3, :] += jnp.sum(dt * n, axis=0, keepdims=True)
        dn = dt * gv
        dx_ref[...] = dr_ref[...] + r * (dn - n * jnp.mean(dn * n, axis=-1, keepdims=True))

    del sh
    vec = pl.BlockSpec((1, d), lambda i: (0, 0))
    row = pl.BlockSpec((tm, d), lambda i: (i, 0))
    return pl.pallas_call(
        body, name="lnmod_bwd", grid=(s // tm,), in_specs=[row, vec, vec, row, row],
        out_specs=[row, pl.BlockSpec((8, d), lambda i: (0, 0))],
        out_shape=[jax.ShapeDtypeStruct((s, d), F32), jax.ShapeDtypeStruct((8, d), F32)],
        compiler_params=_params("arbitrary"),
    )(x, g, sc, dh, dres)


def _out_bwd(dx1, mo, g1, wout, pa, pb, p, wa, wb, d):
    s = dx1.shape[0]
    tm = _tile(s, 256)

    def body(dx_ref, mo_ref, g_ref, wo_ref, pa_ref, pb_ref, ga_ref, gb_ref, wa_ref, wb_ref,
             dmo_ref, da_ref, db_ref, dya_ref, dyb_ref, dga_ref, dgb_ref, dg_ref):
        @pl.when(pl.program_id(0) == 0)
        def _():
            dg_ref[...] = jnp.zeros_like(dg_ref)

        dxv = dx_ref[...]
        dg_ref[0:1, :] += jnp.sum(dxv * mo_ref[...], axis=0, keepdims=True)
        dmo = (g_ref[...] * dxv).astype(BF16)
        dmo_ref[...] = dmo
        dm = _dot_nt(dmo, wo_ref[...])
        sa, sb = jax.nn.sigmoid(ga_ref[...]), jax.nn.sigmoid(gb_ref[...])
        da = (dm * sa).astype(BF16)
        db = (dm * sb).astype(BF16)
        da_ref[...] = da
        db_ref[...] = db
        dga_ref[...] = (dm * pa_ref[...].astype(F32) * (sa * (1.0 - sa))).astype(BF16)
        dgb_ref[...] = (dm * pb_ref[...].astype(F32) * (sb * (1.0 - sb))).astype(BF16)
        dya_ref[...] = _dot_nt(da, wa_ref[...]).astype(BF16)
        dyb_ref[...] = _dot_nt(db, wb_ref[...]).astype(BF16)

    row = pl.BlockSpec((tm, d), lambda i: (i, 0))
    wsp = pl.BlockSpec((d, d), lambda i: (0, 0))
    shp = jax.ShapeDtypeStruct((s, d), BF16)
    return pl.pallas_call(
        body, name="out_bwd", grid=(s // tm,),
        in_specs=[row, row, pl.BlockSpec((1, d), lambda i: (0, 0)), wsp, row, row,
                  pl.BlockSpec((tm, d), lambda i: (i, 6)), pl.BlockSpec((tm, d), lambda i: (i, 7)), wsp, wsp],
        out_specs=[row] * 7 + [pl.BlockSpec((8, d), lambda i: (0, 0))],
        out_shape=[shp] * 7 + [jax.ShapeDtypeStruct((8, d), F32)],
        compiler_params=_params("arbitrary"),
    )(dx1, mo, g1, wout, pa, pb, p, p, wa, wb)


def _conv_bwd(p, conv_w, dyb, d):
    s = p.shape[0]
    nb = d // LANES
    rows_n = _conv_rows(s)

    def body(cb_ref, cc_ref, cx_ref, w_ref, dy_ref, dcb_ref, dcc_ref, dcx_ref, dw_ref, us, ds):
        us[pl.ds(0, 8), :] = jnp.zeros((8, LANES), F32)
        ds[pl.ds(s, 8), :] = jnp.zeros((8, LANES), F32)

        def fill(r, _):
            rows = pl.ds(pl.multiple_of(r * rows_n, rows_n), rows_n)
            us[pl.ds(pl.multiple_of(r * rows_n + 8, 8), rows_n), :] = cc_ref[rows, :] * cx_ref[rows, :]
            ds[rows, :] = dy_ref[rows, :].astype(F32) * cb_ref[rows, :]
            return 0

        lax.fori_loop(0, s // rows_n, fill, 0)
        w = w_ref[...]

        def out(r, carry):
            dw0, dw1, dw2 = carry
            rows = pl.ds(pl.multiple_of(r * rows_n, rows_n), rows_n)
            ext = us[pl.ds(pl.multiple_of(r * rows_n, 8), rows_n + 8), :]
            u0, u1, u2 = ext[8:, :], pltpu.roll(ext, 1, 0)[8:, :], pltpu.roll(ext, 2, 0)[8:, :]
            cv = w[0:1, :] * u2 + w[1:2, :] * u1 + w[2:3, :] * u0
            dcb_ref[rows, :] = (dy_ref[rows, :].astype(F32) * cv).astype(BF16)
            nxt = ds[pl.ds(pl.multiple_of(r * rows_n, 8), rows_n + 8), :]
            e0 = nxt[:rows_n, :]
            e1 = pltpu.roll(nxt, rows_n + 7, 0)[:rows_n, :]
            e2 = pltpu.roll(nxt, rows_n + 6, 0)[:rows_n, :]
            du = w[2:3, :] * e0 + w[1:2, :] * e1 + w[0:1, :] * e2
            dcc_ref[rows, :] = (du * cx_ref[rows, :]).astype(BF16)
            dcx_ref[rows, :] = (du * cc_ref[rows, :]).astype(BF16)
            return (dw0 + jnp.sum(e0 * u2, axis=0, keepdims=True), dw1 + jnp.sum(e0 * u1, axis=0, keepdims=True),
                    dw2 + jnp.sum(e0 * u0, axis=0, keepdims=True))

        zero = jnp.zeros((1, LANES), F32)
        dw0, dw1, dw2 = lax.fori_loop(0, s // rows_n, out, (zero, zero, zero))
        dw_ref[...] = jnp.zeros_like(dw_ref)
        dw_ref[0:1, :] = dw0
        dw_ref[1:2, :] = dw1
        dw_ref[2:3, :] = dw2

    def seg(k):
        return pl.BlockSpec((s, LANES), lambda b, k=k: (0, k * nb + b))

    col = pl.BlockSpec((s, LANES), lambda b: (0, b))
    shp = jax.ShapeDtypeStruct((s, d), BF16)
    return pl.pallas_call(
        body, name="conv_bwd", grid=(nb,),
        in_specs=[seg(3), seg(4), seg(5), pl.BlockSpec((3, LANES), lambda b: (0, b)), col],
        out_specs=[col, col, col, pl.BlockSpec((8, LANES), lambda b: (0, b))],
        out_shape=[shp, shp, shp, jax.ShapeDtypeStruct((8, d), F32)],
        scratch_shapes=[pltpu.VMEM((s + 8, LANES), F32), pltpu.VMEM((s + 8, LANES), F32)],
        compiler_params=_params("parallel"),
    )(p, p, p, conv_w, dyb)


def _attn_bwd(p, qg, kg, dy, lt, d):
    s = p.shape[0]
    n_pairs = d // LANES
    n_qb = s // Q_BLOCK
    chunk = _tile(s, 512)
    inv_sqrt = 1.0 / math.sqrt(HEAD_DIM)

    def body(q_ref, k_ref, v_ref, qg_ref, kg_ref, dy_ref, lt_ref, dq_ref, dk_ref, dv_ref, dgain_ref,
             qs, ks, vs, dys, dks, dvs):
        row = lax.broadcasted_iota(jnp.int32, (Q_BLOCK, Q_BLOCK), 0)
        col = lax.broadcasted_iota(jnp.int32, (Q_BLOCK, Q_BLOCK), 1)
        upto = (row <= col).astype(BF16)
        causal = col < row
        dgain_ref[...] = jnp.zeros_like(dgain_ref)

        for hh in range(2):
            lanes = slice(hh * HEAD_DIM, (hh + 1) * HEAD_DIM)

            def prep(r, _):
                rows = pl.ds(pl.multiple_of(r * chunk, chunk), chunk)
                qs[rows, :] = _head_norm(q_ref[rows, lanes], qg_ref[...], inv_sqrt).astype(BF16)
                ks[rows, :] = _head_norm(k_ref[rows, lanes], kg_ref[...], 1.0).astype(BF16)
                vs[rows, :] = v_ref[rows, lanes].astype(BF16)
                dys[rows, :] = dy_ref[rows, lanes]
                dks[rows, :] = jnp.zeros((chunk, HEAD_DIM), F32)
                dvs[rows, :] = jnp.zeros((chunk, HEAD_DIM), F32)
                return 0

            lax.fori_loop(0, s // chunk, prep, 0)

            def q_block(i, dqg):
                rows_i = pl.ds(pl.multiple_of(i * Q_BLOCK, Q_BLOCK), Q_BLOCK)
                qi = qs[rows_i, :]
                dyi = dys[rows_i, :]
                total = lt_ref[rows_i, lanes][:, 0:1]

                def step(j, carry, masked):
                    before, g_before, dq = carry
                    rows_j = pl.ds(pl.multiple_of(j * Q_BLOCK, Q_BLOCK), Q_BLOCK)
                    kj = ks[rows_j, :]
                    z = _dot_nt(qi, kj)
                    log_not = -_softplus(z)
                    if masked:
                        log_not = jnp.where(causal, log_not, 0.0)
                    incl = _cumdot(log_not, upto)
                    log_a = z + log_not + (total - (before + incl))
                    if masked:
                        log_a = jnp.where(causal, log_a, -1e30)
                    a = jnp.exp(log_a)
                    g = a * _dot_nt(dyi, vs[rows_j, :])
                    g_incl = _cumdot(g, upto)
                    dz = g - jnp.exp(z + log_not) * (g_before + g_incl)
                    if masked:
                        dz = jnp.where(causal, dz, 0.0)
                    dzb = dz.astype(BF16)
                    dvs[rows_j, :] += _dot(a.T.astype(BF16), dyi)
                    dks[rows_j, :] += _dot(dz.T.astype(BF16), qi)
                    dq = dq + _dot(dzb, kj)
                    return (before + incl[:, Q_BLOCK - 1:Q_BLOCK], g_before + g_incl[:, Q_BLOCK - 1:Q_BLOCK], dq)

                zero = jnp.zeros((Q_BLOCK, 1), F32)
                carry = (zero, zero, jnp.zeros((Q_BLOCK, HEAD_DIM), F32))
                carry = lax.fori_loop(0, i, lambda j, cr: step(j, cr, False), carry)
                _, _, dqn = step(i, carry, True)
                qraw = q_ref[rows_i, lanes]
                r = lax.rsqrt(jnp.mean(qraw * qraw, axis=-1, keepdims=True) + EPS)
                qhat = qraw * r
                dqhat = dqn * (qg_ref[...] * inv_sqrt)
                dq_ref[rows_i, lanes] = (r * (dqhat - qhat * jnp.mean(dqhat * qhat, axis=-1, keepdims=True))
                                         ).astype(BF16)
                return dqg + jnp.sum(dqn * qhat, axis=0, keepdims=True) * inv_sqrt

            dqg = lax.fori_loop(0, n_qb, q_block, jnp.zeros((1, HEAD_DIM), F32))

            def finish(r, dkg):
                rows = pl.ds(pl.multiple_of(r * chunk, chunk), chunk)
                kraw = k_ref[rows, lanes]
                rk = lax.rsqrt(jnp.mean(kraw * kraw, axis=-1, keepdims=True) + EPS)
                khat = kraw * rk
                dkn = dks[rows, :]
                dkhat = dkn * kg_ref[...]
                dk_ref[rows, lanes] = (rk * (dkhat - khat * jnp.mean(dkhat * khat, axis=-1, keepdims=True))
                                       ).astype(BF16)
                dv_ref[rows, lanes] = dvs[rows, :].astype(BF16)
                return dkg + jnp.sum(dkn * khat, axis=0, keepdims=True)

            dkg = lax.fori_loop(0, s // chunk, finish, jnp.zeros((1, HEAD_DIM), F32))
            dgain_ref[0:1, lanes] = dqg
            dgain_ref[1:2, lanes] = dkg

    def seg(k):
        return pl.BlockSpec((s, LANES), lambda h, k=k: (0, k * n_pairs + h))

    vec = pl.BlockSpec((1, HEAD_DIM), lambda h: (0, 0))
    col = pl.BlockSpec((s, LANES), lambda h: (0, h))
    shp = jax.ShapeDtypeStruct((s, d), BF16)
    return pl.pallas_call(
        body, name="attn_bwd", grid=(n_pairs,),
        in_specs=[seg(0), seg(1), seg(2), vec, vec, col, col],
        out_specs=[col, col, col, pl.BlockSpec((None, 8, LANES), lambda h: (h, 0, 0))],
        out_shape=[shp, shp, shp, jax.ShapeDtypeStruct((n_pairs, 8, LANES), F32)],
        scratch_shapes=[pltpu.VMEM((s, HEAD_DIM), BF16)] * 4 + [pltpu.VMEM((s, HEAD_DIM), F32)] * 2,
        compiler_params=_params("parallel"),
    )(p, p, p, qg, kg, dy, lt)


def _mm_in_bwd(dp, w_g, layer):
    s = dp.shape[0]
    d, n4 = w_g.shape[-2:]
    tm = _tile(s, 512)

    def body(a_ref, w_ref, o_ref, acc):
        j = pl.program_id(1)

        @pl.when(j == 0)
        def _():
            acc[...] = jnp.zeros_like(acc)

        acc[...] += _dot_nt(a_ref[...], w_ref[...])

        @pl.when(j == N_CHIP - 1)
        def _():
            o_ref[...] = acc[...]

    return pl.pallas_call(
        body, name="mm_in_bwd", grid=(s // tm, N_CHIP),
        in_specs=[pl.BlockSpec((tm, n4), lambda i, j: (i, j)),
                  pl.BlockSpec((None, None, d, n4), lambda i, j: (j, layer, 0, 0))],
        out_specs=pl.BlockSpec((tm, d), lambda i, j: (i, 0)), out_shape=jax.ShapeDtypeStruct((s, d), F32),
        scratch_shapes=[pltpu.VMEM((tm, d), F32)], compiler_params=_params("parallel", "arbitrary"),
    )(dp, w_g)


def _sum_adam(parts, w, m, v):
    n_l, r, c = w.shape
    tr = next((t for t in (256, 176, 128, 64, 32, 16) if r % t == 0 and t * c <= 256 * 1024), r)

    def body(p_ref, w_ref, m_ref, v_ref, g_ref, dl_ref, nm_ref, nv_ref):
        g = p_ref[0].astype(F32)
        for dev in range(1, N_DEV):
            g = g + p_ref[dev].astype(F32)
        g_ref[...] = g
        delta, nm, nv = _adamw(w_ref[...], g, m_ref[...], v_ref[...])
        dl_ref[...] = delta
        nm_ref[...] = nm
        nv_ref[...] = nv

    wsp = pl.BlockSpec((None, tr, c), lambda l, i: (l, i, 0))
    shp = jax.ShapeDtypeStruct(w.shape, F32)
    return pl.pallas_call(
        body, name="sum_adam", grid=(n_l, r // tr),
        in_specs=[pl.BlockSpec((None, N_DEV, tr, c), lambda l, i: (l, 0, i, 0)), wsp, wsp, wsp],
        out_specs=[wsp] * 4, out_shape=[shp] * 4, compiler_params=_params("parallel", "parallel"),
    )(parts, w, m, v)


def _small_adam(parts, w, m, v):
    def body(p_ref, w_ref, m_ref, v_ref, g_ref, dl_ref, nm_ref, nv_ref):
        g = p_ref[0]
        for dev in range(1, N_DEV):
            g = g + p_ref[dev]
        g_ref[...] = g
        delta, nm, nv = _adamw(w_ref[...], g, m_ref[...], v_ref[...])
        dl_ref[...] = delta
        nm_ref[...] = nm
        nv_ref[...] = nv

    shp = jax.ShapeDtypeStruct(w.shape, F32)
    vm = pl.BlockSpec(memory_space=pltpu.VMEM)
    return pl.pallas_call(body, name="small_adam", in_specs=[vm] * 4, out_specs=[vm] * 4, out_shape=[shp] * 4,
                          compiler_params=pltpu.CompilerParams(vmem_limit_bytes=VMEM_LIMIT_BYTES))(parts, w, m, v)


def _pack(vecs, mult=8 * LANES):
    flat = jnp.concatenate([a.reshape(-1).astype(F32) for a in vecs])
    pad = (-flat.shape[0]) % mult
    if pad:
        flat = jnp.concatenate([flat, jnp.zeros((pad,), F32)])
    return flat.reshape(8, -1)


def _unpack(flat, shapes):
    flat = flat.reshape(-1)
    out, off = [], 0
    for shp in shapes:
        n = math.prod(shp)
        out.append(flat[off:off + n].reshape(shp))
        off += n
    return out


def _local_step(x, target, mods, ln1_g, ln2_g, qg, kg, conv_w, win_g, wa_g, wb_g, wo_g, wg_g, wu_g, wd_g):
    s, d = x.shape
    n_l = mods.shape[0]
    saved = []
    h_in = x
    for l in range(n_l):
        sh1, sc1, g1, sh2, sc2, g2 = [mods[l, k * d:(k + 1) * d].reshape(1, d) for k in range(6)]
        wa = wa_g[:, l].reshape(d, d)
        wb = wb_g[:, l].reshape(d, d)
        wo = wo_g[:, l].reshape(d, d)
        h1 = _lnmod(h_in, ln1_g[l:l + 1], sc1, sh1)
        p = _mm_in(h1, win_g, l)
        qg2, kg2 = jnp.tile(qg[l:l + 1], (1, 2)), jnp.tile(kg[l:l + 1], (1, 2))
        ya, lt = _attn3_fwd(p, qg2, kg2, d)
        yb = _conv_fwd(p, conv_w[l], d)
        merged, pa, pb = _branch(ya, yb, p, wa, wb, d)
        x1, mo = _out_proj(merged, wo, h_in, g1)
        h2 = _lnmod(x1, ln2_g[l:l + 1], sc2, sh2)
        gate, up, act = _ffn_up(h2, wg_g, wu_g, l)
        x2, f = _ffn_down(act, wd_g, x1, g2, l)
        saved.append(dict(x0=h_in, h1=h1, p=p, ya=ya, lt=lt, yb=yb, merged=merged, pa=pa, pb=pb, x1=x1, mo=mo,
                          h2=h2, gate=gate, up=up, act=act, f=f, wa=wa, wb=wb, wo=wo,
                          mod=(sh1, sc1, g1, sh2, sc2, g2)))
        h_in = x2

    dx, loss_tile = _loss_head(h_in, target)

    f4 = wg_g.shape[-1]
    n4 = win_g.shape[-1]
    r4 = d // N_CHIP
    small, big = [None] * n_l, [None] * n_l
    for l in reversed(range(n_l)):
        sv = saved[l]
        sh1, sc1, g1, sh2, sc2, g2 = sv["mod"]
        dgate, dup, df, dg2 = _ffn_bwd1(dx, sv["f"], g2, wd_g, sv["gate"], sv["up"], l)
        g_wd = _mm_tn(sv["act"], df, lambda tk: ((None, tk, f4), lambda j, k: (j, k, 0)),
                      lambda tk: ((tk, d), lambda j, k: (k, 0)), (f4, d), "grad_wd")
        hsp = lambda tk: ((tk, d), lambda j, k: (k, 0))
        fsp = lambda tk: ((None, tk, f4), lambda j, k: (j, k, 0))
        g_wg = _mm_tn(sv["h2"], dgate, hsp, fsp, (d, f4), "grad_wg")
        g_wu = _mm_tn(sv["h2"], dup, hsp, fsp, (d, f4), "grad_wu")
        dh2 = _ffn_bwd2(dgate, dup, wg_g, wu_g, l)
        dx1, sums2 = _lnmod_bwd(sv["x1"], ln2_g[l:l + 1], sc2, sh2, dh2, dx)
        dmo, da, db, dya, dyb, dga, dgb, dg1 = _out_bwd(dx1, sv["mo"], g1, sv["wo"], sv["pa"], sv["pb"], sv["p"],
                                                        sv["wa"], sv["wb"], d)
        csp = lambda tk: ((tk, r4), lambda j, k: (k, j))
        g_wo = _mm_tn(sv["merged"], dmo, csp, hsp, (r4, d), "grad_wo")
        g_wa = _mm_tn(sv["ya"], da, csp, hsp, (r4, d), "grad_wa")
        g_wb = _mm_tn(sv["yb"], db, csp, hsp, (r4, d), "grad_wb")
        dcb, dcc, dcx, dconv = _conv_bwd(sv["p"], conv_w[l], dyb, d)
        qg2, kg2 = jnp.tile(qg[l:l + 1], (1, 2)), jnp.tile(kg[l:l + 1], (1, 2))
        dq, dk, dv, dgain = _attn3_bwd(sv["p"], qg2, kg2, dya, sv["lt"], d)
        dp = jnp.concatenate([dq, dk, dv, dcb, dcc, dcx, dga, dgb], axis=1)
        g_win = _mm_tn(sv["h1"], dp, hsp, lambda tk: ((tk, n4), lambda j, k: (k, j)), (d, n4), "grad_win")
        dh1 = _mm_in_bwd(dp, win_g, l)
        dx, sums1 = _lnmod_bwd(sv["x0"], ln1_g[l:l + 1], sc1, sh1, dh1, dx1)
        dgain = jnp.sum(dgain[:, 0:2, :], axis=0)
        dgain = dgain[:, :HEAD_DIM] + dgain[:, HEAD_DIM:]
        dmod = jnp.concatenate([sums1[0], sums1[1], dg1[0], sums2[0], sums2[1], dg2[0]])
        small[l] = dict(dmod=dmod, ln1=sums1[2], ln2=sums2[2], qg=dgain[0], kg=dgain[1], conv=dconv[0:3])
        big[l] = dict(win=g_win, wa=g_wa, wb=g_wb, wo=g_wo, wg=g_wg, wu=g_wu, wd=g_wd)
    return loss_tile, dx, small, big


BIG = ("win", "wa", "wb", "wo", "wg", "wu", "wd")


def kernel(x, c, ada_w, ada_b, ln1_g, w_in, q_norm_g, k_norm_g, conv_w, w_branch_a, w_branch_b, w_out, ln2_g, w_ffn_gate, w_ffn_up, w_ffn_down, loss_target, m_ada_w, m_ada_b, m_ln1_g, m_w_in, m_q_norm_g, m_k_norm_g, m_conv_w, m_w_branch_a, m_w_branch_b, m_w_out, m_ln2_g, m_w_ffn_gate, m_w_ffn_up, m_w_ffn_down, v_ada_w, v_ada_b, v_ln1_g, v_w_in, v_q_norm_g, v_k_norm_g, v_conv_w, v_w_branch_a, v_w_branch_b, v_w_out, v_ln2_g, v_w_ffn_gate, v_w_ffn_up, v_w_ffn_down):
    n_l, d, a4 = ada_w.shape
    s = x.shape[1]
    cw4 = conv_w.shape[-1]
    ix, iy, ic = lax.axis_index("x"), lax.axis_index("y"), lax.axis_index("c")
    chip = 2 * ix + iy
    me = 2 * chip + ic

    got = _gather8(_pack([c, conv_w])).reshape(N_DEV, -1)
    c_all = got[:, :d]
    conv_all = got[:, d:d + n_l * 3 * cw4].reshape(N_CHIP, 2, n_l, 3, cw4)[:, 0]
    conv_full = jnp.transpose(conv_all, (1, 2, 0, 3)).reshape(n_l, 3, N_CHIP * cw4)
    b_cols = lax.dynamic_slice_in_dim(ada_b, chip * a4, a4, axis=1).reshape(n_l, 1, a4)
    mod_cols = _ada_mod(c_all, ada_w, b_cols)
    mod_all = _gather8(_pack([mod_cols])).reshape(N_DEV, -1)[:, :n_l * N_DEV * a4]
    mod_all = mod_all.reshape(N_CHIP, 2, n_l, N_DEV, a4)[:, 0]
    mods = lax.dynamic_index_in_dim(mod_all, me, axis=2, keepdims=False)
    mods = jnp.transpose(mods, (1, 0, 2)).reshape(n_l, N_CHIP * a4)

    big_w = dict(win=w_in, wa=w_branch_a, wb=w_branch_b, wo=w_out, wg=w_ffn_gate, wu=w_ffn_up, wd=w_ffn_down)
    big_m = dict(win=m_w_in, wa=m_w_branch_a, wb=m_w_branch_b, wo=m_w_out, wg=m_w_ffn_gate, wu=m_w_ffn_up,
                 wd=m_w_ffn_down)
    big_v = dict(win=v_w_in, wa=v_w_branch_a, wb=v_w_branch_b, wo=v_w_out, wg=v_w_ffn_gate, wu=v_w_ffn_up,
                 wd=v_w_ffn_down)
    gathered = dict(zip(BIG, _gather_weights([big_w[k].astype(BF16) for k in BIG])))

    loss_tile, grad_x, small, big = _local_step(
        x[0], loss_target[0], mods, ln1_g, ln2_g, q_norm_g, k_norm_g, conv_full,
        *[gathered[k] for k in BIG])

    parts = _exchange_grads([[big[l][k] for l in range(n_l)] for k in BIG], n_l)
    big_out = {k: _sum_adam(pt, big_w[k], big_m[k], big_v[k]) for k, pt in zip(BIG, parts)}

    sm_shapes = [(n_l, 6 * d), (n_l, d), (n_l, d), (n_l, HEAD_DIM), (n_l, HEAD_DIM), (n_l, 3, d), (1,)]
    vec = _pack([jnp.stack([small[l][k] for l in range(n_l)]) for k in ("dmod", "ln1", "ln2", "qg", "kg", "conv")]
                + [loss_tile[0, 0:1]])
    n_vec = vec.shape[1] * 8
    all_vec = _gather8(vec).reshape(N_DEV, n_vec)
    per_dev = [_unpack(all_vec[dev], sm_shapes) for dev in range(N_DEV)]
    dmod_all = jnp.stack([pd[0] for pd in per_dev])
    dmod_cols = jnp.transpose(lax.dynamic_slice_in_dim(dmod_all, chip * a4, a4, axis=2), (1, 0, 2))
    ada_out = _ada_grad_adam(jnp.transpose(c_all), dmod_cols, ada_w, m_ada_w, v_ada_w)

    def small_pack(parts_of):
        return _pack([parts_of[0], parts_of[1], parts_of[2], parts_of[3], parts_of[4], parts_of[5], parts_of[6]])

    dev_parts = jnp.stack([
        small_pack([pd[0], pd[1], pd[2], pd[3], pd[4], lax.dynamic_slice_in_dim(pd[5], chip * cw4, cw4, axis=2),
                    pd[6]]) for pd in per_dev])
    zero1 = jnp.zeros((1,), F32)
    sw = small_pack([ada_b, ln1_g, ln2_g, q_norm_g, k_norm_g, conv_w, zero1])
    sm = small_pack([m_ada_b, m_ln1_g, m_ln2_g, m_q_norm_g, m_k_norm_g, m_conv_w, zero1])
    sv = small_pack([v_ada_b, v_ln1_g, v_ln2_g, v_q_norm_g, v_k_norm_g, v_conv_w, zero1 + 1.0])
    out_shapes = [(n_l, 6 * d), (n_l, d), (n_l, d), (n_l, HEAD_DIM), (n_l, HEAD_DIM), (n_l, 3, cw4), (1,)]
    sm_out = [_unpack(o, out_shapes) for o in _small_adam(dev_parts, sw, sm, sv)]
    loss = 0.5 * sm_out[0][6][0] / d

    outs = [loss, grad_x[None]]
    for kind in range(4):
        sm_k = sm_out[kind]
        outs += [ada_out[kind], sm_k[0], sm_k[1], big_out["win"][kind], sm_k[3], sm_k[4], sm_k[5],
                 big_out["wa"][kind], big_out["wb"][kind], big_out["wo"][kind], sm_k[2],
                 big_out["wg"][kind], big_out["wu"][kind], big_out["wd"][kind]]
    return tuple(outs)
```

```python
import math

import jax
import jax.numpy as jnp
from jax import lax
from jax.experimental import pallas as pl
from jax.experimental.pallas import tpu as pltpu

F32 = jnp.float32
BF16 = jnp.bfloat16
MESH_ID = pl.DeviceIdType.MESH

EPS = 1e-6
HEAD_DIM = 64
Q_BLOCK = 128
Q_SUPER = 512
LANES = 128
N_DEV = 8
N_CHIP = 4
VMEM_LIMIT_BYTES = 56 * 1024 * 1024

ADAM_LR = 0.001
ADAM_B1 = 0.9
ADAM_B2 = 0.999
ADAM_EPS = 1e-08
ADAM_WD = 0.01
ADAM_STEP = 10

HBM_SPEC = pl.BlockSpec(memory_space=pltpu.HBM)
ANY_SPEC = pl.BlockSpec(memory_space=pl.ANY)
SEM_SPEC = pl.BlockSpec(memory_space=pltpu.SEMAPHORE)
VMEM_SPEC = pl.BlockSpec(memory_space=pltpu.VMEM)
SIDE_EFFECT = pltpu.SideEffectType.DATAFLOW_SIDE_EFFECTING


def _params(*sem):
    return pltpu.CompilerParams(dimension_semantics=tuple(sem), vmem_limit_bytes=VMEM_LIMIT_BYTES)


def _tile(n, pref):
    return pref if n % pref == 0 else n


def _dot(a, b):
    return jnp.dot(a, b, preferred_element_type=F32)


def _dot_nt(a, b):
    return lax.dot_general(a, b, (((1,), (1,)), ((), ())), preferred_element_type=F32)


def _dot_tn(a, b):
    return lax.dot_general(a, b, (((0,), (0,)), ((), ())), preferred_element_type=F32)


def _adamw(w, g, m, v):
    m = ADAM_B1 * m + (1.0 - ADAM_B1) * g
    v = ADAM_B2 * v + (1.0 - ADAM_B2) * (g * g)
    m_hat = m / (1.0 - ADAM_B1 ** ADAM_STEP)
    v_hat = v / (1.0 - ADAM_B2 ** ADAM_STEP)
    delta = -ADAM_LR * (m_hat / (jnp.sqrt(v_hat) + ADAM_EPS) + ADAM_WD * w)
    return delta, m, v


def _hbm(a):
    return pltpu.with_memory_space_constraint(a, pltpu.HBM)


def _peer(x, y, c, k):
    return (1 - x if k & 4 else x, 1 - y if k & 2 else y, 1 - c if k & 1 else c)


def _gather8(v):
    rows_per, m = v.shape

    def body(v_ref, out_ref, send_sems, recv_sems, local_sem):
        x, y, c = lax.axis_index("x"), lax.axis_index("y"), lax.axis_index("c")

        def rows(p):
            return out_ref.at[pl.ds((4 * p[0] + 2 * p[1] + p[2]) * rows_per, rows_per), :]

        me = (x, y, c)
        mine = pltpu.make_async_copy(v_ref, rows(me), local_sem)
        mine.start()
        sends = []
        for k in range(1, N_DEV):
            cp = pltpu.make_async_remote_copy(
                src_ref=v_ref, dst_ref=rows(me), send_sem=send_sems.at[k - 1], recv_sem=recv_sems.at[k - 1],
                device_id=_peer(x, y, c, k), device_id_type=MESH_ID)
            cp.start()
            sends.append(cp)
        for k in range(1, N_DEV):
            pltpu.make_async_remote_copy(
                src_ref=v_ref, dst_ref=rows(_peer(x, y, c, k)), send_sem=send_sems.at[k - 1],
                recv_sem=recv_sems.at[k - 1], device_id=_peer(x, y, c, k), device_id_type=MESH_ID).wait_recv()
        for cp in sends:
            cp.wait_send()
        mine.wait()

    return pl.pallas_call(
        body, name="gather8",
        out_shape=jax.ShapeDtypeStruct((N_DEV * rows_per, m), v.dtype),
        in_specs=[VMEM_SPEC], out_specs=VMEM_SPEC,
        scratch_shapes=[pltpu.SemaphoreType.DMA((N_DEV - 1,)), pltpu.SemaphoreType.DMA((N_DEV - 1,)),
                        pltpu.SemaphoreType.DMA],
    )(v)


def _weight_copies(srcs, lands, send_sems, recv_sems):
    x, y, c = lax.axis_index("x"), lax.axis_index("y"), lax.axis_index("c")
    chips = [(1 - x, y), (x, 1 - y), (1 - x, 1 - y)]
    sends, recvs = [], []
    for a, (src, land) in enumerate(zip(srcs, lands)):
        for j, (px, py) in enumerate(chips):
            def copy(dst_block, a=a, j=j, px=px, py=py, src=src, land=land):
                return pltpu.make_async_remote_copy(
                    src_ref=src, dst_ref=land.at[dst_block], send_sem=send_sems.at[3 * a + j],
                    recv_sem=recv_sems.at[3 * a + j], device_id=(px, py, c), device_id_type=MESH_ID)
            sends.append(copy(2 * x + y))
            recvs.append(copy(2 * px + py))
    return sends, recvs


def _weights_start(name, srcs, lands):
    n = len(srcs)

    def body(*refs):
        sends, _ = _weight_copies(refs[:n], refs[n:2 * n], refs[2 * n], refs[2 * n + 1])
        for cp in sends:
            cp.start()
        token = refs[-1]
        token[...] = jnp.zeros_like(token)

    outs = pl.pallas_call(
        body, name=name,
        out_shape=(pltpu.SemaphoreType.DMA((3 * n,)), pltpu.SemaphoreType.DMA((3 * n,)),
                   *[pltpu.HBM(a.shape, a.dtype) for a in list(srcs) + list(lands)],
                   jax.ShapeDtypeStruct((8, LANES), F32)),
        in_specs=[HBM_SPEC] * (2 * n), out_specs=(SEM_SPEC, SEM_SPEC, *[HBM_SPEC] * (2 * n), VMEM_SPEC),
        input_output_aliases={i: 2 + i for i in range(2 * n)},
        compiler_params=pltpu.CompilerParams(has_side_effects=SIDE_EFFECT),
    )(*[_hbm(a) for a in list(srcs) + list(lands)])
    return outs[0], outs[1], outs[2:2 + n], outs[2 + n:2 + 2 * n], outs[-1]


def _weights_wait(name, started, after):
    send_sems, recv_sems, srcs, lands, _ = started
    n = len(srcs)

    def body(*refs):
        sends, recvs = _weight_copies(refs[:n], refs[n:2 * n], refs[2 * n], refs[2 * n + 1])
        for cp in sends:
            cp.wait_send()
        for cp in recvs:
            cp.wait_recv()

    outs = pl.pallas_call(
        body, name=name,
        out_shape=tuple(pltpu.HBM(a.shape, a.dtype) for a in list(srcs) + list(lands)),
        in_specs=[HBM_SPEC] * (2 * n) + [SEM_SPEC, SEM_SPEC, ANY_SPEC], out_specs=tuple([HBM_SPEC] * (2 * n)),
        input_output_aliases={i: i for i in range(2 * n)},
        compiler_params=pltpu.CompilerParams(has_side_effects=SIDE_EFFECT),
    )(*srcs, *lands, send_sems, recv_sems, after)
    return outs[n:]


def _grad_copies(grads, parts, send_sems, recv_sems):
    x, y, c = lax.axis_index("x"), lax.axis_index("y"), lax.axis_index("c")
    chips = [(1 - x, y), (x, 1 - y), (1 - x, 1 - y)]
    my_slot = 4 * x + 2 * y + c
    sends, recvs = [], []
    for a, (grad, part) in enumerate(zip(grads, parts)):
        def copy(k, block, slot, to, a=a, grad=grad, part=part):
            return pltpu.make_async_remote_copy(
                src_ref=grad.at[block], dst_ref=part.at[slot], send_sem=send_sems.at[7 * a + k],
                recv_sem=recv_sems.at[7 * a + k], device_id=to, device_id_type=MESH_ID)
        sends.append(copy(0, 2 * x + y, my_slot, (x, y, 1 - c)))
        recvs.append(copy(0, 2 * x + y, 4 * x + 2 * y + (1 - c), (x, y, 1 - c)))
        for j, (px, py) in enumerate(chips):
            for other, pc in enumerate((c, 1 - c)):
                sends.append(copy(1 + 2 * j + other, 2 * px + py, my_slot, (px, py, pc)))
                recvs.append(copy(1 + 2 * j + other, 2 * x + y, 4 * px + 2 * py + pc, (px, py, pc)))
    return sends, recvs


def _grads_start(name, grads, parts):
    n = len(grads)

    def body(*refs):
        sends, _ = _grad_copies(refs[:n], refs[n:2 * n], refs[2 * n], refs[2 * n + 1])
        for cp in sends:
            cp.start()
        token = refs[-1]
        token[...] = jnp.zeros_like(token)

    outs = pl.pallas_call(
        body, name=name,
        out_shape=(pltpu.SemaphoreType.DMA((7 * n,)), pltpu.SemaphoreType.DMA((7 * n,)),
                   *[pltpu.HBM(a.shape, a.dtype) for a in list(grads) + list(parts)],
                   jax.ShapeDtypeStruct((8, LANES), F32)),
        in_specs=[HBM_SPEC] * (2 * n), out_specs=(SEM_SPEC, SEM_SPEC, *[HBM_SPEC] * (2 * n), VMEM_SPEC),
        input_output_aliases={i: 2 + i for i in range(2 * n)},
        compiler_params=pltpu.CompilerParams(has_side_effects=SIDE_EFFECT),
    )(*[_hbm(a) for a in list(grads) + list(parts)])
    return outs[0], outs[1], outs[2:2 + n], outs[2 + n:2 + 2 * n], outs[-1]


def _grads_wait(name, started, after):
    send_sems, recv_sems, grads, parts, _ = started
    n = len(grads)

    def body(*refs):
        sends, recvs = _grad_copies(refs[:n], refs[n:2 * n], refs[2 * n], refs[2 * n + 1])
        for cp in sends:
            cp.wait_send()
        for cp in recvs:
            cp.wait_recv()

    outs = pl.pallas_call(
        body, name=name,
        out_shape=tuple(pltpu.HBM(a.shape, a.dtype) for a in list(grads) + list(parts)),
        in_specs=[HBM_SPEC] * (2 * n) + [SEM_SPEC, SEM_SPEC, ANY_SPEC], out_specs=tuple([HBM_SPEC] * (2 * n)),
        input_output_aliases={i: i for i in range(2 * n)},
        compiler_params=pltpu.CompilerParams(has_side_effects=SIDE_EFFECT),
    )(*grads, *parts, send_sems, recv_sems, after)
    return outs[n:]


def _ada_mod(c_all, ada_w, ada_b_cols):
    n_l, d, a4 = ada_w.shape
    tn = _tile(a4, 512)

    def body(c_ref, w_ref, b_ref, o_ref):
        cv = c_ref[...]
        ca = (cv * jax.nn.sigmoid(cv)).astype(BF16)
        o_ref[...] = _dot(ca, w_ref[...].astype(BF16)) + b_ref[...]

    return pl.pallas_call(
        body, name="ada_mod", grid=(n_l, a4 // tn),
        in_specs=[pl.BlockSpec((N_DEV, d), lambda l, j: (0, 0)),
                  pl.BlockSpec((None, d, tn), lambda l, j: (l, 0, j)),
                  pl.BlockSpec((None, 1, tn), lambda l, j: (l, 0, j))],
        out_specs=pl.BlockSpec((None, N_DEV, tn), lambda l, j: (l, 0, j)),
        out_shape=jax.ShapeDtypeStruct((n_l, N_DEV, a4), F32),
        compiler_params=_params("parallel", "parallel"),
    )(c_all, ada_w, ada_b_cols)


def _ada_grad_adam(c_all_t, dmod_cols, w, m, v):
    n_l, d, a4 = w.shape
    tn = _tile(a4, 512)

    def body(ct_ref, dm_ref, w_ref, m_ref, v_ref, g_ref, dl_ref, nm_ref, nv_ref):
        ct = ct_ref[...]
        ca = ct * jax.nn.sigmoid(ct)
        dm = dm_ref[...]
        g = ca[:, 0:1] * dm[0:1, :]
        for dev in range(1, N_DEV):
            g = g + ca[:, dev:dev + 1] * dm[dev:dev + 1, :]
        g_ref[...] = g
        delta, nm, nv = _adamw(w_ref[...], g, m_ref[...], v_ref[...])
        dl_ref[...] = delta
        nm_ref[...] = nm
        nv_ref[...] = nv

    wspec = pl.BlockSpec((None, d, tn), lambda l, j: (l, 0, j))
    shp = jax.ShapeDtypeStruct(w.shape, F32)
    return pl.pallas_call(
        body, name="ada_grad_adam", grid=(n_l, a4 // tn),
        in_specs=[pl.BlockSpec((d, N_DEV), lambda l, j: (0, 0)),
                  pl.BlockSpec((None, N_DEV, tn), lambda l, j: (l, 0, j)), wspec, wspec, wspec],
        out_specs=[wspec] * 4, out_shape=[shp] * 4,
        compiler_params=_params("parallel", "parallel"),
    )(c_all_t, dmod_cols, w, m, v)


def _lnmod(x, g, sc, sh):
    s, d = x.shape
    tm = _tile(s, 512)

    def body(x_ref, g_ref, sc_ref, sh_ref, h_ref):
        xv = x_ref[...]
        r = lax.rsqrt(jnp.mean(xv * xv, axis=-1, keepdims=True) + EPS)
        h_ref[...] = ((xv * r * g_ref[...]) * (1.0 + sc_ref[...]) + sh_ref[...]).astype(BF16)

    vec = pl.BlockSpec((1, d), lambda i: (0, 0))
    row = pl.BlockSpec((tm, d), lambda i: (i, 0))
    return pl.pallas_call(
        body, name="lnmod", grid=(s // tm,), in_specs=[row, vec, vec, vec], out_specs=row,
        out_shape=jax.ShapeDtypeStruct((s, d), BF16), compiler_params=_params("parallel"),
    )(x, g, sc, sh)


def _mm_in(h, w_g):
    s, d = h.shape
    n4 = w_g.shape[-1]
    tm = _tile(s, 512)

    def body(a_ref, b_ref, o_ref):
        o_ref[...] = _dot(a_ref[...], b_ref[...])

    return pl.pallas_call(
        body, name="mm_in", grid=(N_CHIP, s // tm),
        in_specs=[pl.BlockSpec((tm, d), lambda j, i: (i, 0)),
                  pl.BlockSpec((None, d, n4), lambda j, i: (j, 0, 0))],
        out_specs=pl.BlockSpec((tm, n4), lambda j, i: (i, j)),
        out_shape=jax.ShapeDtypeStruct((s, N_CHIP * n4), F32),
        compiler_params=_params("parallel", "parallel"),
    )(h, w_g)


def _pair_mean(x, low):
    lo = jnp.sum(jnp.where(low, x, 0.0), axis=-1, keepdims=True)
    hi = jnp.sum(jnp.where(low, 0.0, x), axis=-1, keepdims=True)
    return jnp.where(low, lo, hi) * (1.0 / HEAD_DIM)


def _pair_norm(x, low):
    r = lax.rsqrt(_pair_mean(x * x, low) + EPS)
    return x * r, r


def _log_not(z):
    return jnp.minimum(-z, 0.0) - jnp.log(1.0 + jnp.exp(-jnp.abs(z)))


def _attn_consts(inclusive):
    low = lax.broadcasted_iota(jnp.int32, (1, LANES), 1) < HEAD_DIM
    row = lax.broadcasted_iota(jnp.int32, (Q_BLOCK, Q_BLOCK), 0)
    col = lax.broadcasted_iota(jnp.int32, (Q_BLOCK, Q_BLOCK), 1)
    tri = (row <= col) if inclusive else (row > col)
    w2 = jnp.concatenate([tri.astype(BF16), jnp.ones((Q_BLOCK, Q_BLOCK), BF16)], axis=1)
    return low, col - row, jnp.concatenate([w2, w2], axis=0)


def _split_cat(v):
    hi = v.astype(BF16)
    return jnp.concatenate([hi, (v - hi.astype(F32)).astype(BF16)], axis=1)


def _fill_pair_blocks(dst, src_fn, low, n_kb):
    def fill(b, _):
        v = src_fn(pl.ds(pl.multiple_of(b * Q_BLOCK, Q_BLOCK), Q_BLOCK))
        dst[b, 0:Q_BLOCK, :] = jnp.where(low, v, 0.0).astype(BF16)
        dst[b, Q_BLOCK:2 * Q_BLOCK, :] = jnp.where(low, 0.0, v).astype(BF16)
        return 0

    lax.fori_loop(0, n_kb, fill, 0)


def _attn_fwd(p, qg2, kg2, d):
    s = p.shape[0]
    n_pairs = d // LANES
    qsb = _tile(s, Q_SUPER)
    n_sub, n_sb, n_kb = qsb // Q_BLOCK, s // qsb, s // Q_BLOCK
    chunk = _tile(s, 512)
    inv_sqrt = 1.0 / math.sqrt(HEAD_DIM)

    def body(q_ref, k_ref, v_ref, qg_ref, kg_ref, o_ref, lt_ref, qs, k2, v2, run, acc):
        low, diff, w4 = _attn_consts(False)

        def prep(r, _):
            rows = pl.ds(pl.multiple_of(r * chunk, chunk), chunk)
            qs[rows, :] = (_pair_norm(q_ref[rows, :], low)[0] * (qg_ref[...] * inv_sqrt)).astype(BF16)
            return 0

        lax.fori_loop(0, s // chunk, prep, 0)
        _fill_pair_blocks(k2, lambda rows: _pair_norm(k_ref[rows, :], low)[0] * kg_ref[...], low, n_kb)
        _fill_pair_blocks(v2, lambda rows: v_ref[rows, :], low, n_kb)

        def step(sb, j, masked):
            rows_sb = pl.ds(pl.multiple_of(sb * qsb, qsb), qsb)
            z_both = _dot_nt(qs[rows_sb, :], k2[j])
            zls, cats, keeps = [], [], []
            for t in range(n_sub):
                sub = slice(t * Q_BLOCK, (t + 1) * Q_BLOCK)
                keep = (diff < (sb * qsb + t * Q_BLOCK - j * Q_BLOCK)) if masked else None
                for h in range(2):
                    z = z_both[sub, h * LANES:(h + 1) * LANES]
                    ln = _log_not(z)
                    if masked:
                        ln = jnp.where(keep, ln, 0.0)
                    zls.append(z + ln)
                    cats.append(_split_cat(ln))
                    keeps.append(keep)
            c2 = _dot(jnp.concatenate(cats, axis=0), w4)
            a_rows = []
            for t in range(n_sub):
                sub = slice(t * Q_BLOCK, (t + 1) * Q_BLOCK)
                a_pair = []
                for h in range(2):
                    i = 2 * t + h
                    tile = slice(i * Q_BLOCK, (i + 1) * Q_BLOCK)
                    later = run[h, sub, :]
                    log_a = zls[i] + c2[tile, :LANES] + later
                    if masked:
                        log_a = jnp.where(keeps[i], log_a, -1e30)
                    a_pair.append(jnp.exp(log_a).astype(BF16))
                    run[h, sub, :] = later + c2[tile, LANES:]
                a_rows.append(jnp.concatenate(a_pair, axis=1))
            acc[...] += _dot(jnp.concatenate(a_rows, axis=0), v2[j])

        def super_block(sb, _):
            run[...] = jnp.zeros_like(run)
            acc[...] = jnp.zeros_like(acc)

            def diag(n, _):
                step(sb, sb * n_sub + n_sub - 1 - n, True)
                return 0

            def below(n, _):
                step(sb, sb * n_sub - 1 - n, False)
                return 0

            lax.fori_loop(0, n_sub, diag, 0)
            lax.fori_loop(0, sb * n_sub, below, 0)
            rows_sb = pl.ds(pl.multiple_of(sb * qsb, qsb), qsb)
            o_ref[rows_sb, :] = acc[...].astype(BF16)
            lt_ref[rows_sb, :] = jnp.where(low, run[0], run[1])
            return 0

        lax.fori_loop(0, n_sb, super_block, 0)

    def seg(k):
        return pl.BlockSpec((s, LANES), lambda h, k=k: (0, k * n_pairs + h))

    vec = pl.BlockSpec((1, LANES), lambda h: (0, 0))
    out = pl.BlockSpec((s, LANES), lambda h: (0, h))
    return pl.pallas_call(
        body, name="attn_fwd", grid=(n_pairs,),
        in_specs=[seg(0), seg(1), seg(2), vec, vec], out_specs=[out, out],
        out_shape=[jax.ShapeDtypeStruct((s, d), BF16), jax.ShapeDtypeStruct((s, d), F32)],
        scratch_shapes=[pltpu.VMEM((s, LANES), BF16)] + [pltpu.VMEM((n_kb, 2 * Q_BLOCK, LANES), BF16)] * 2
        + [pltpu.VMEM((2, qsb, LANES), F32), pltpu.VMEM((qsb, LANES), F32)],
        compiler_params=_params("parallel"),
    )(p, p, p, qg2, kg2)


def _conv_rows(s):
    return _tile(s, 512)


def _conv_fwd(p, conv_w, d):
    s = p.shape[0]
    nb = d // LANES
    rows_n = _conv_rows(s)

    def body(cb_ref, cc_ref, cx_ref, w_ref, y_ref, us):
        us[pl.ds(0, 8), :] = jnp.zeros((8, LANES), F32)

        def fill(r, _):
            rows = pl.ds(pl.multiple_of(r * rows_n, rows_n), rows_n)
            us[pl.ds(pl.multiple_of(r * rows_n + 8, 8), rows_n), :] = cc_ref[rows, :] * cx_ref[rows, :]
            return 0

        lax.fori_loop(0, s // rows_n, fill, 0)
        w = w_ref[...]

        def out(r, _):
            rows = pl.ds(pl.multiple_of(r * rows_n, rows_n), rows_n)
            ext = us[pl.ds(pl.multiple_of(r * rows_n, 8), rows_n + 8), :]
            cv = (w[0:1, :] * pltpu.roll(ext, 2, 0)[8:, :] + w[1:2, :] * pltpu.roll(ext, 1, 0)[8:, :]
                  + w[2:3, :] * ext[8:, :])
            y_ref[rows, :] = (cb_ref[rows, :] * cv).astype(BF16)
            return 0

        lax.fori_loop(0, s // rows_n, out, 0)

    def seg(k):
        return pl.BlockSpec((s, LANES), lambda b, k=k: (0, k * nb + b))

    return pl.pallas_call(
        body, name="conv_fwd", grid=(nb,),
        in_specs=[seg(3), seg(4), seg(5), pl.BlockSpec((3, LANES), lambda b: (0, b))],
        out_specs=pl.BlockSpec((s, LANES), lambda b: (0, b)),
        out_shape=jax.ShapeDtypeStruct((s, d), BF16),
        scratch_shapes=[pltpu.VMEM((s + 8, LANES), F32)],
        compiler_params=_params("parallel"),
    )(p, p, p, conv_w)


def _branch(ya, yb, p, wa, wb, d):
    s = ya.shape[0]
    tm = _tile(s, 512)

    def body(ya_ref, yb_ref, ga_ref, gb_ref, wa_ref, wb_ref, m_ref, a_ref, b_ref):
        pa = _dot(ya_ref[...], wa_ref[...])
        pb = _dot(yb_ref[...], wb_ref[...])
        m_ref[...] = (jax.nn.sigmoid(ga_ref[...]) * pa + jax.nn.sigmoid(gb_ref[...]) * pb).astype(BF16)
        a_ref[...] = pa.astype(BF16)
        b_ref[...] = pb.astype(BF16)

    row = pl.BlockSpec((tm, d), lambda i: (i, 0))
    wsp = pl.BlockSpec((d, d), lambda i: (0, 0))
    shp = jax.ShapeDtypeStruct((s, d), BF16)
    return pl.pallas_call(
        body, name="branch", grid=(s // tm,),
        in_specs=[row, row, pl.BlockSpec((tm, d), lambda i: (i, 6)), pl.BlockSpec((tm, d), lambda i: (i, 7)), wsp, wsp],
        out_specs=[row, row, row], out_shape=[shp, shp, shp], compiler_params=_params("parallel"),
    )(ya, yb, p, p, wa, wb)


def _out_proj(merged, wout, x0, g1):
    s, d = x0.shape
    tm = _tile(s, 512)

    def body(m_ref, w_ref, x_ref, g_ref, x1_ref, mo_ref):
        mo = _dot(m_ref[...], w_ref[...])
        mo_ref[...] = mo
        x1_ref[...] = x_ref[...] + g_ref[...] * mo

    row = pl.BlockSpec((tm, d), lambda i: (i, 0))
    shp = jax.ShapeDtypeStruct((s, d), F32)
    return pl.pallas_call(
        body, name="out_proj", grid=(s // tm,),
        in_specs=[row, pl.BlockSpec((d, d), lambda i: (0, 0)), row, pl.BlockSpec((1, d), lambda i: (0, 0))],
        out_specs=[row, row], out_shape=[shp, shp], compiler_params=_params("parallel"),
    )(merged, wout, x0, g1)


def _ffn_up(h, wg_g, wu_g):
    s, d = h.shape
    f4 = wg_g.shape[-1]
    tm = _tile(s, 512)

    def body(h_ref, wg_ref, wu_ref, gate_ref, up_ref, act_ref):
        hv = h_ref[...]
        gt = _dot(hv, wg_ref[...])
        up = _dot(hv, wu_ref[...])
        gate_ref[...] = gt.astype(BF16)
        up_ref[...] = up.astype(BF16)
        act_ref[...] = (gt * jax.nn.sigmoid(gt) * up).astype(BF16)

    wsp = pl.BlockSpec((None, d, f4), lambda j, i: (j, 0, 0))
    osp = pl.BlockSpec((None, tm, f4), lambda j, i: (j, i, 0))
    shp = jax.ShapeDtypeStruct((N_CHIP, s, f4), BF16)
    return pl.pallas_call(
        body, name="ffn_up", grid=(N_CHIP, s // tm),
        in_specs=[pl.BlockSpec((tm, d), lambda j, i: (i, 0)), wsp, wsp],
        out_specs=[osp, osp, osp], out_shape=[shp, shp, shp], compiler_params=_params("parallel", "parallel"),
    )(h, wg_g, wu_g)


def _ffn_down(act, wd_g, x1, g2):
    s, d = x1.shape
    f4 = act.shape[-1]
    tm = _tile(s, 512)

    def body(a_ref, w_ref, x_ref, g_ref, x2_ref, f_ref, acc):
        j = pl.program_id(1)

        @pl.when(j == 0)
        def _():
            acc[...] = jnp.zeros_like(acc)

        acc[...] += _dot(a_ref[...], w_ref[...])

        @pl.when(j == N_CHIP - 1)
        def _():
            f = acc[...]
            f_ref[...] = f
            x2_ref[...] = x_ref[...] + g_ref[...] * f

    row = pl.BlockSpec((tm, d), lambda i, j: (i, 0))
    shp = jax.ShapeDtypeStruct((s, d), F32)
    return pl.pallas_call(
        body, name="ffn_down", grid=(s // tm, N_CHIP),
        in_specs=[pl.BlockSpec((None, tm, f4), lambda i, j: (j, i, 0)),
                  pl.BlockSpec((None, f4, d), lambda i, j: (j, 0, 0)),
                  row, pl.BlockSpec((1, d), lambda i, j: (0, 0))],
        out_specs=[row, row], out_shape=[shp, shp],
        scratch_shapes=[pltpu.VMEM((tm, d), F32)], compiler_params=_params("parallel", "arbitrary"),
    )(act, wd_g, x1, g2)


def _loss_head(y, target):
    s, d = y.shape
    tm = _tile(s, 512)
    n_steps = s // tm

    def body(y_ref, t_ref, dy_ref, l_ref, acc):
        i = pl.program_id(0)

        @pl.when(i == 0)
        def _():
            acc[...] = jnp.zeros_like(acc)

        err = y_ref[...] - t_ref[...]
        dy_ref[...] = err / d
        acc[...] += jnp.sum(err * err, axis=0, keepdims=True)

        @pl.when(i == n_steps - 1)
        def _():
            l_ref[...] = jnp.broadcast_to(jnp.sum(acc[...], axis=1, keepdims=True), (8, LANES))

    row = pl.BlockSpec((tm, d), lambda i: (i, 0))
    return pl.pallas_call(
        body, name="loss_head", grid=(n_steps,), in_specs=[row, row],
        out_specs=[row, pl.BlockSpec((8, LANES), lambda i: (0, 0))],
        out_shape=[jax.ShapeDtypeStruct((s, d), F32), jax.ShapeDtypeStruct((8, LANES), F32)],
        scratch_shapes=[pltpu.VMEM((1, d), F32)], compiler_params=_params("arbitrary"),
    )(y, target)


def _mm_tn(a, b, a_spec, b_spec, out_rc, name):
    r, c = out_rc
    s = a.shape[-2]
    tk = _tile(s, 512)
    nk = s // tk

    def body(a_ref, b_ref, o_ref, acc):
        k = pl.program_id(1)

        @pl.when(k == 0)
        def _():
            acc[...] = jnp.zeros_like(acc)

        acc[...] += _dot_tn(a_ref[...], b_ref[...])

        @pl.when(k == nk - 1)
        def _():
            o_ref[...] = acc[...].astype(BF16)

    return pl.pallas_call(
        body, name=name, grid=(N_CHIP, nk),
        in_specs=[pl.BlockSpec(*a_spec(tk)), pl.BlockSpec(*b_spec(tk))],
        out_specs=pl.BlockSpec((None, r, c), lambda j, k: (j, 0, 0)),
        out_shape=jax.ShapeDtypeStruct((N_CHIP, r, c), BF16),
        scratch_shapes=[pltpu.VMEM((r, c), F32)], compiler_params=_params("parallel", "arbitrary"),
    )(a, b)


def _ffn_bwd1(dx2, f, g2, wd_g, gate, up):
    s, d = dx2.shape
    f4 = gate.shape[-1]
    tm = _tile(s, 512)

    def body(dx_ref, f_ref, g_ref, w_ref, gate_ref, up_ref, dgate_ref, dup_ref, df_ref, dg_ref):
        i, j = pl.program_id(0), pl.program_id(1)

        @pl.when((i == 0) & (j == 0))
        def _():
            dg_ref[...] = jnp.zeros_like(dg_ref)

        dxv = dx_ref[...]
        df = (g_ref[...] * dxv).astype(BF16)

        @pl.when(j == 0)
        def _():
            df_ref[...] = df
            dg_ref[0:1, :] += jnp.sum(dxv * f_ref[...], axis=0, keepdims=True)

        da = _dot_nt(df, w_ref[...])
        gt = gate_ref[...].astype(F32)
        sg = jax.nn.sigmoid(gt)
        dup_ref[...] = (da * gt * sg).astype(BF16)
        dgate_ref[...] = (da * up_ref[...].astype(F32) * (sg * (1.0 + gt * (1.0 - sg)))).astype(BF16)

    row = pl.BlockSpec((tm, d), lambda i, j: (i, 0))
    hsp = pl.BlockSpec((None, tm, f4), lambda i, j: (j, i, 0))
    hshp = jax.ShapeDtypeStruct((N_CHIP, s, f4), BF16)
    return pl.pallas_call(
        body, name="ffn_bwd1", grid=(s // tm, N_CHIP),
        in_specs=[row, row, pl.BlockSpec((1, d), lambda i, j: (0, 0)),
                  pl.BlockSpec((None, f4, d), lambda i, j: (j, 0, 0)), hsp, hsp],
        out_specs=[hsp, hsp, row, pl.BlockSpec((8, d), lambda i, j: (0, 0))],
        out_shape=[hshp, hshp, jax.ShapeDtypeStruct((s, d), BF16), jax.ShapeDtypeStruct((8, d), F32)],
        compiler_params=_params("arbitrary", "arbitrary"),
    )(dx2, f, g2, wd_g, gate, up)


def _ffn_bwd2(dgate, dup, wg_g, wu_g):
    _, s, f4 = dgate.shape
    d = wg_g.shape[-2]
    tm = _tile(s, 512)

    def body(dg_ref, du_ref, wg_ref, wu_ref, o_ref, acc):
        j = pl.program_id(1)

        @pl.when(j == 0)
        def _():
            acc[...] = jnp.zeros_like(acc)

        acc[...] += _dot_nt(dg_ref[...], wg_ref[...]) + _dot_nt(du_ref[...], wu_ref[...])

        @pl.when(j == N_CHIP - 1)
        def _():
            o_ref[...] = acc[...]

    hsp = pl.BlockSpec((None, tm, f4), lambda i, j: (j, i, 0))
    wsp = pl.BlockSpec((None, d, f4), lambda i, j: (j, 0, 0))
    return pl.pallas_call(
        body, name="ffn_bwd2", grid=(s // tm, N_CHIP), in_specs=[hsp, hsp, wsp, wsp],
        out_specs=pl.BlockSpec((tm, d), lambda i, j: (i, 0)), out_shape=jax.ShapeDtypeStruct((s, d), F32),
        scratch_shapes=[pltpu.VMEM((tm, d), F32)], compiler_params=_params("parallel", "arbitrary"),
    )(dgate, dup, wg_g, wu_g)


def _lnmod_bwd(x, g, sc, dh, dres):
    s, d = x.shape
    tm = _tile(s, 512)

    def body(x_ref, g_ref, sc_ref, dh_ref, dr_ref, dx_ref, sums_ref):
        @pl.when(pl.program_id(0) == 0)
        def _():
            sums_ref[...] = jnp.zeros_like(sums_ref)

        xv, dhv, gv = x_ref[...], dh_ref[...], g_ref[...]
        r = lax.rsqrt(jnp.mean(xv * xv, axis=-1, keepdims=True) + EPS)
        n = xv * r
        one_sc = 1.0 + sc_ref[...]
        dt = dhv * one_sc
        sums_ref[0:1, :] += jnp.sum(dhv, axis=0, keepdims=True)
        sums_ref[1:2, :] += jnp.sum(dhv * (n * gv), axis=0, keepdims=True)
        sums_ref[2:3, :] += jnp.sum(dt * n, axis=0, keepdims=True)
        dn = dt * gv
        dx_ref[...] = dr_ref[...] + r * (dn - n * jnp.mean(dn * n, axis=-1, keepdims=True))

    vec = pl.BlockSpec((1, d), lambda i: (0, 0))
    row = pl.BlockSpec((tm, d), lambda i: (i, 0))
    return pl.pallas_call(
        body, name="lnmod_bwd", grid=(s // tm,), in_specs=[row, vec, vec, row, row],
        out_specs=[row, pl.BlockSpec((8, d), lambda i: (0, 0))],
        out_shape=[jax.ShapeDtypeStruct((s, d), F32), jax.ShapeDtypeStruct((8, d), F32)],
        compiler_params=_params("arbitrary"),
    )(x, g, sc, dh, dres)


def _out_bwd(dx1, mo, g1, wout, pa, pb, p, wa, wb, d):
    s = dx1.shape[0]
    tm = _tile(s, 256)

    def body(dx_ref, mo_ref, g_ref, wo_ref, pa_ref, pb_ref, ga_ref, gb_ref, wa_ref, wb_ref,
             dmo_ref, da_ref, db_ref, dya_ref, dyb_ref, dga_ref, dgb_ref, dg_ref):
        @pl.when(pl.program_id(0) == 0)
        def _():
            dg_ref[...] = jnp.zeros_like(dg_ref)

        dxv = dx_ref[...]
        dg_ref[0:1, :] += jnp.sum(dxv * mo_ref[...], axis=0, keepdims=True)
        dmo = (g_ref[...] * dxv).astype(BF16)
        dmo_ref[...] = dmo
        dm = _dot_nt(dmo, wo_ref[...])
        sa, sb = jax.nn.sigmoid(ga_ref[...]), jax.nn.sigmoid(gb_ref[...])
        da = (dm * sa).astype(BF16)
        db = (dm * sb).astype(BF16)
        da_ref[...] = da
        db_ref[...] = db
        dga_ref[...] = (dm * pa_ref[...].astype(F32) * (sa * (1.0 - sa))).astype(BF16)
        dgb_ref[...] = (dm * pb_ref[...].astype(F32) * (sb * (1.0 - sb))).astype(BF16)
        dya_ref[...] = _dot_nt(da, wa_ref[...]).astype(BF16)
        dyb_ref[...] = _dot_nt(db, wb_ref[...]).astype(BF16)

    row = pl.BlockSpec((tm, d), lambda i: (i, 0))
    wsp = pl.BlockSpec((d, d), lambda i: (0, 0))
    shp = jax.ShapeDtypeStruct((s, d), BF16)
    return pl.pallas_call(
        body, name="out_bwd", grid=(s // tm,),
        in_specs=[row, row, pl.BlockSpec((1, d), lambda i: (0, 0)), wsp, row, row,
                  pl.BlockSpec((tm, d), lambda i: (i, 6)), pl.BlockSpec((tm, d), lambda i: (i, 7)), wsp, wsp],
        out_specs=[row] * 7 + [pl.BlockSpec((8, d), lambda i: (0, 0))],
        out_shape=[shp] * 7 + [jax.ShapeDtypeStruct((8, d), F32)],
        compiler_params=_params("arbitrary"),
    )(dx1, mo, g1, wout, pa, pb, p, p, wa, wb)


def _conv_bwd(p, conv_w, dyb, d):
    s = p.shape[0]
    nb = d // LANES
    rows_n = _conv_rows(s)

    def body(cb_ref, cc_ref, cx_ref, w_ref, dy_ref, dcb_ref, dcc_ref, dcx_ref, dw_ref, us, ds):
        us[pl.ds(0, 8), :] = jnp.zeros((8, LANES), F32)
        ds[pl.ds(s, 8), :] = jnp.zeros((8, LANES), F32)

        def fill(r, _):
            rows = pl.ds(pl.multiple_of(r * rows_n, rows_n), rows_n)
            us[pl.ds(pl.multiple_of(r * rows_n + 8, 8), rows_n), :] = cc_ref[rows, :] * cx_ref[rows, :]
            ds[rows, :] = dy_ref[rows, :].astype(F32) * cb_ref[rows, :]
            return 0

        lax.fori_loop(0, s // rows_n, fill, 0)
        w = w_ref[...]

        def out(r, carry):
            dw0, dw1, dw2 = carry
            rows = pl.ds(pl.multiple_of(r * rows_n, rows_n), rows_n)
            ext = us[pl.ds(pl.multiple_of(r * rows_n, 8), rows_n + 8), :]
            u0, u1, u2 = ext[8:, :], pltpu.roll(ext, 1, 0)[8:, :], pltpu.roll(ext, 2, 0)[8:, :]
            cv = w[0:1, :] * u2 + w[1:2, :] * u1 + w[2:3, :] * u0
            dcb_ref[rows, :] = (dy_ref[rows, :].astype(F32) * cv).astype(BF16)
            nxt = ds[pl.ds(pl.multiple_of(r * rows_n, 8), rows_n + 8), :]
            e0 = nxt[:rows_n, :]
            e1 = pltpu.roll(nxt, rows_n + 7, 0)[:rows_n, :]
            e2 = pltpu.roll(nxt, rows_n + 6, 0)[:rows_n, :]
            du = w[2:3, :] * e0 + w[1:2, :] * e1 + w[0:1, :] * e2
            dcc_ref[rows, :] = (du * cx_ref[rows, :]).astype(BF16)
            dcx_ref[rows, :] = (du * cc_ref[rows, :]).astype(BF16)
            return (dw0 + jnp.sum(e0 * u2, axis=0, keepdims=True), dw1 + jnp.sum(e0 * u1, axis=0, keepdims=True),
                    dw2 + jnp.sum(e0 * u0, axis=0, keepdims=True))

        zero = jnp.zeros((1, LANES), F32)
        dw0, dw1, dw2 = lax.fori_loop(0, s // rows_n, out, (zero, zero, zero))
        dw_ref[...] = jnp.zeros_like(dw_ref)
        dw_ref[0:1, :] = dw0
        dw_ref[1:2, :] = dw1
        dw_ref[2:3, :] = dw2

    def seg(k):
        return pl.BlockSpec((s, LANES), lambda b, k=k: (0, k * nb + b))

    col = pl.BlockSpec((s, LANES), lambda b: (0, b))
    shp = jax.ShapeDtypeStruct((s, d), BF16)
    return pl.pallas_call(
        body, name="conv_bwd", grid=(nb,),
        in_specs=[seg(3), seg(4), seg(5), pl.BlockSpec((3, LANES), lambda b: (0, b)), col],
        out_specs=[col, col, col, pl.BlockSpec((8, LANES), lambda b: (0, b))],
        out_shape=[shp, shp, shp, jax.ShapeDtypeStruct((8, d), F32)],
        scratch_shapes=[pltpu.VMEM((s + 8, LANES), F32), pltpu.VMEM((s + 8, LANES), F32)],
        compiler_params=_params("parallel"),
    )(p, p, p, conv_w, dyb)


def _attn_bwd(p, qg2, kg2, dy, lt, d):
    s = p.shape[0]
    n_pairs = d // LANES
    qsb = _tile(s, Q_SUPER)
    n_sub, n_sb, n_kb = qsb // Q_BLOCK, s // qsb, s // Q_BLOCK
    chunk = _tile(s, 512)
    inv_sqrt = 1.0 / math.sqrt(HEAD_DIM)

    def body(q_ref, k_ref, v_ref, qg_ref, kg_ref, dy_ref, lt_ref, dq_ref, dk_ref, dv_ref, dgain_ref,
             qs, k2, v2, dkt, dvt, qt, dyt, rem, gbef, dqa):
        low, diff, w4 = _attn_consts(True)

        def prep(r, _):
            rows = pl.ds(pl.multiple_of(r * chunk, chunk), chunk)
            qs[rows, :] = (_pair_norm(q_ref[rows, :], low)[0] * (qg_ref[...] * inv_sqrt)).astype(BF16)
            return 0

        lax.fori_loop(0, s // chunk, prep, 0)
        _fill_pair_blocks(k2, lambda rows: _pair_norm(k_ref[rows, :], low)[0] * kg_ref[...], low, n_kb)
        _fill_pair_blocks(v2, lambda rows: v_ref[rows, :], low, n_kb)

        def clear(b, _):
            dkt[b] = jnp.zeros((LANES, Q_BLOCK), F32)
            dvt[b] = jnp.zeros((LANES, Q_BLOCK), F32)
            return 0

        lax.fori_loop(0, n_kb, clear, 0)

        def step(sb, j, masked):
            rows_sb = pl.ds(pl.multiple_of(sb * qsb, qsb), qsb)
            kj2, vj2 = k2[j], v2[j]
            z_both = _dot_nt(qs[rows_sb, :], kj2)
            da_both = _dot_nt(dy_ref[rows_sb, :], vj2)
            zls, cats, keeps = [], [], []
            for t in range(n_sub):
                sub = slice(t * Q_BLOCK, (t + 1) * Q_BLOCK)
                keep = (diff < (sb * qsb + t * Q_BLOCK - j * Q_BLOCK)) if masked else None
                for h in range(2):
                    z = z_both[sub, h * LANES:(h + 1) * LANES]
                    ln = _log_not(z)
                    if masked:
                        ln = jnp.where(keep, ln, 0.0)
                    zls.append(z + ln)
                    cats.append(_split_cat(ln))
                    keeps.append(keep)
            c2 = _dot(jnp.concatenate(cats, axis=0), w4)
            a_rows, gs, cats = [], [], []
            for t in range(n_sub):
                sub = slice(t * Q_BLOCK, (t + 1) * Q_BLOCK)
                a_pair = []
                for h in range(2):
                    i = 2 * t + h
                    tile = slice(i * Q_BLOCK, (i + 1) * Q_BLOCK)
                    left = rem[h, sub, :]
                    log_a = zls[i] + (left - c2[tile, :LANES])
                    if masked:
                        log_a = jnp.where(keeps[i], log_a, -1e30)
                    a = jnp.exp(log_a)
                    rem[h, sub, :] = left - c2[tile, LANES:]
                    g = a * da_both[sub, h * LANES:(h + 1) * LANES]
                    a_pair.append(a.astype(BF16))
                    gs.append(g)
                    cats.append(_split_cat(g))
                a_rows.append(jnp.concatenate(a_pair, axis=1))
            c2g = _dot(jnp.concatenate(cats, axis=0), w4)
            dz_rows = []
            for t in range(n_sub):
                sub = slice(t * Q_BLOCK, (t + 1) * Q_BLOCK)
                dz_pair = []
                for h in range(2):
                    i = 2 * t + h
                    tile = slice(i * Q_BLOCK, (i + 1) * Q_BLOCK)
                    before = gbef[h, sub, :]
                    dz = gs[i] - jnp.exp(zls[i]) * (before + c2g[tile, :LANES])
                    if masked:
                        dz = jnp.where(keeps[i], dz, 0.0)
                    gbef[h, sub, :] = before + c2g[tile, LANES:]
                    dz_pair.append(dz.astype(BF16))
                dz_rows.append(jnp.concatenate(dz_pair, axis=1))
            a_both = jnp.concatenate(a_rows, axis=0)
            dz_both = jnp.concatenate(dz_rows, axis=0)
            dvt[j] += _dot(dyt[0], a_both[:, :LANES]) + _dot(dyt[1], a_both[:, LANES:])
            dkt[j] += _dot(qt[0], dz_both[:, :LANES]) + _dot(qt[1], dz_both[:, LANES:])
            dqa[...] += _dot(dz_both, kj2)

        def super_block(sb, dqg):
            rows_sb = pl.ds(pl.multiple_of(sb * qsb, qsb), qsb)
            total = lt_ref[rows_sb, :]
            other = pltpu.roll(total, HEAD_DIM, 1)
            rem[0] = jnp.where(low, total, other)
            rem[1] = jnp.where(low, other, total)
            gbef[...] = jnp.zeros_like(gbef)
            dqa[...] = jnp.zeros_like(dqa)
            qv = qs[rows_sb, :].astype(F32)
            dyv = dy_ref[rows_sb, :].astype(F32)
            qt[0] = jnp.where(low, qv, 0.0).T.astype(BF16)
            qt[1] = jnp.where(low, 0.0, qv).T.astype(BF16)
            dyt[0] = jnp.where(low, dyv, 0.0).T.astype(BF16)
            dyt[1] = jnp.where(low, 0.0, dyv).T.astype(BF16)

            def below(j, _):
                step(sb, j, False)
                return 0

            def diag(n, _):
                step(sb, sb * n_sub + n, True)
                return 0

            lax.fori_loop(0, sb * n_sub, below, 0)
            lax.fori_loop(0, n_sub, diag, 0)
            qhat, r = _pair_norm(q_ref[rows_sb, :], low)
            dqn = dqa[...]
            dqhat = dqn * (qg_ref[...] * inv_sqrt)
            dq_ref[rows_sb, :] = (r * (dqhat - qhat * _pair_mean(dqhat * qhat, low))).astype(BF16)
            return dqg + jnp.sum(dqn * qhat, axis=0, keepdims=True) * inv_sqrt

        dqg = lax.fori_loop(0, n_sb, super_block, jnp.zeros((1, LANES), F32))

        def finish(b, dkg):
            rows = pl.ds(pl.multiple_of(b * Q_BLOCK, Q_BLOCK), Q_BLOCK)
            khat, rk = _pair_norm(k_ref[rows, :], low)
            dkn = dkt[b].T
            dkhat = dkn * kg_ref[...]
            dk_ref[rows, :] = (rk * (dkhat - khat * _pair_mean(dkhat * khat, low))).astype(BF16)
            dv_ref[rows, :] = dvt[b].T.astype(BF16)
            return dkg + jnp.sum(dkn * khat, axis=0, keepdims=True)

        dkg = lax.fori_loop(0, n_kb, finish, jnp.zeros((1, LANES), F32))
        dgain_ref[...] = jnp.zeros_like(dgain_ref)
        dgain_ref[0:1, :] = dqg
        dgain_ref[1:2, :] = dkg

    def seg(k):
        return pl.BlockSpec((s, LANES), lambda h, k=k: (0, k * n_pairs + h))

    vec = pl.BlockSpec((1, LANES), lambda h: (0, 0))
    col = pl.BlockSpec((s, LANES), lambda h: (0, h))
    shp = jax.ShapeDtypeStruct((s, d), BF16)
    return pl.pallas_call(
        body, name="attn_bwd", grid=(n_pairs,),
        in_specs=[seg(0), seg(1), seg(2), vec, vec, col, col],
        out_specs=[col, col, col, pl.BlockSpec((None, 8, LANES), lambda h: (h, 0, 0))],
        out_shape=[shp, shp, shp, jax.ShapeDtypeStruct((n_pairs, 8, LANES), F32)],
        scratch_shapes=[pltpu.VMEM((s, LANES), BF16)] + [pltpu.VMEM((n_kb, 2 * Q_BLOCK, LANES), BF16)] * 2
        + [pltpu.VMEM((n_kb, LANES, Q_BLOCK), F32)] * 2
        + [pltpu.VMEM((2, LANES, qsb), BF16)] * 2
        + [pltpu.VMEM((2, qsb, LANES), F32)] * 2 + [pltpu.VMEM((qsb, LANES), F32)],
        compiler_params=_params("parallel"),
    )(p, p, p, qg2, kg2, dy, lt)


def _mm_in_bwd(dp, w_g):
    s = dp.shape[0]
    d, n4 = w_g.shape[-2:]
    tm = _tile(s, 512)

    def body(a_ref, w_ref, o_ref, acc):
        j = pl.program_id(1)

        @pl.when(j == 0)
        def _():
            acc[...] = jnp.zeros_like(acc)

        acc[...] += _dot_nt(a_ref[...], w_ref[...])

        @pl.when(j == N_CHIP - 1)
        def _():
            o_ref[...] = acc[...]

    return pl.pallas_call(
        body, name="mm_in_bwd", grid=(s // tm, N_CHIP),
        in_specs=[pl.BlockSpec((tm, n4), lambda i, j: (i, j)),
                  pl.BlockSpec((None, d, n4), lambda i, j: (j, 0, 0))],
        out_specs=pl.BlockSpec((tm, d), lambda i, j: (i, 0)), out_shape=jax.ShapeDtypeStruct((s, d), F32),
        scratch_shapes=[pltpu.VMEM((tm, d), F32)], compiler_params=_params("parallel", "arbitrary"),
    )(dp, w_g)


def _sum_adam(parts, w, m, v, name):
    n_l, r, c = w.shape
    tr = next((t for t in (256, 176, 128, 64, 32, 16) if r % t == 0 and t * c <= 256 * 1024), r)
    n_blk = r // tr

    def body(*refs):
        p_refs = refs[:n_l]
        w_ref, m_ref, v_ref, g_ref, dl_ref, nm_ref, nv_ref = refs[n_l:]
        for l in range(n_l):
            @pl.when(pl.program_id(0) == l)
            def _(p_ref=p_refs[l]):
                g = p_ref[0].astype(F32)
                for dev in range(1, N_DEV):
                    g = g + p_ref[dev].astype(F32)
                g_ref[...] = g
                delta, nm, nv = _adamw(w_ref[...], g, m_ref[...], v_ref[...])
                dl_ref[...] = delta
                nm_ref[...] = nm
                nv_ref[...] = nv

    def part_spec(l):
        return pl.BlockSpec((N_DEV, tr, c), lambda ll, i, l=l: (0, jnp.where(ll == l, i, jnp.where(ll < l, 0, n_blk - 1)), 0))

    wsp = pl.BlockSpec((None, tr, c), lambda l, i: (l, i, 0))
    shp = jax.ShapeDtypeStruct(w.shape, F32)
    return pl.pallas_call(
        body, name=name, grid=(n_l, n_blk),
        in_specs=[part_spec(l) for l in range(n_l)] + [wsp, wsp, wsp],
        out_specs=[wsp] * 4, out_shape=[shp] * 4, compiler_params=_params("arbitrary", "arbitrary"),
    )(*parts, w, m, v)


def _small_adam(parts, w, m, v):
    def body(p_ref, w_ref, m_ref, v_ref, g_ref, dl_ref, nm_ref, nv_ref):
        g = p_ref[0]
        for dev in range(1, N_DEV):
            g = g + p_ref[dev]
        g_ref[...] = g
        delta, nm, nv = _adamw(w_ref[...], g, m_ref[...], v_ref[...])
        dl_ref[...] = delta
        nm_ref[...] = nm
        nv_ref[...] = nv

    shp = jax.ShapeDtypeStruct(w.shape, F32)
    return pl.pallas_call(body, name="small_adam", in_specs=[VMEM_SPEC] * 4, out_specs=[VMEM_SPEC] * 4,
                          out_shape=[shp] * 4,
                          compiler_params=pltpu.CompilerParams(vmem_limit_bytes=VMEM_LIMIT_BYTES))(parts, w, m, v)


def _pack(vecs, mult=8 * LANES):
    flat = jnp.concatenate([a.reshape(-1).astype(F32) for a in vecs])
    pad = (-flat.shape[0]) % mult
    if pad:
        flat = jnp.concatenate([flat, jnp.zeros((pad,), F32)])
    return flat.reshape(8, -1)


def _unpack(flat, shapes):
    flat = flat.reshape(-1)
    out, off = [], 0
    for shp in shapes:
        n = math.prod(shp)
        out.append(flat[off:off + n].reshape(shp))
        off += n
    return out


BIG = ("win", "wa", "wb", "wo", "wg", "wu", "wd")
GRAD_GROUPS = (("wd", "wg", "wu"), ("wo", "wa", "wb"), ("win",))


def _local_step(x, target, mods, ln1_g, ln2_g, qg, kg, conv_w, weights, send_grads):
    s, d = x.shape
    n_l = mods.shape[0]
    saved = []
    h_in = x
    for l in range(n_l):
        sh1, sc1, g1, sh2, sc2, g2 = [mods[l, k * d:(k + 1) * d].reshape(1, d) for k in range(6)]
        qg2, kg2 = jnp.tile(qg[l:l + 1], (1, 2)), jnp.tile(kg[l:l + 1], (1, 2))
        h1 = _lnmod(h_in, ln1_g[l:l + 1], sc1, sh1)
        win, = weights(l, ("win",), h1)
        p = _mm_in(h1, win)
        ya, lt = _attn_fwd(p, qg2, kg2, d)
        yb = _conv_fwd(p, conv_w[l], d)
        wa, wb, wo, wg, wu, wd = weights(l, ("wa", "wb", "wo", "wg", "wu", "wd"), yb)
        wa, wb, wo = wa.reshape(d, d), wb.reshape(d, d), wo.reshape(d, d)
        merged, pa, pb = _branch(ya, yb, p, wa, wb, d)
        x1, mo = _out_proj(merged, wo, h_in, g1)
        h2 = _lnmod(x1, ln2_g[l:l + 1], sc2, sh2)
        gate, up, act = _ffn_up(h2, wg, wu)
        x2, f = _ffn_down(act, wd, x1, g2)
        saved.append(dict(x0=h_in, h1=h1, p=p, ya=ya, lt=lt, yb=yb, merged=merged, pa=pa, pb=pb, x1=x1, mo=mo,
                          h2=h2, gate=gate, up=up, act=act, f=f, win=win, wa=wa, wb=wb, wo=wo, wg=wg, wu=wu, wd=wd,
                          mod=(sh1, sc1, g1, sh2, sc2, g2), qg2=qg2, kg2=kg2))
        h_in = x2

    dx, loss_tile = _loss_head(h_in, target)

    small = [None] * n_l
    for l in reversed(range(n_l)):
        sv = saved[l]
        sh1, sc1, g1, sh2, sc2, g2 = sv["mod"]
        f4, n4, r4 = sv["wg"].shape[-1], sv["win"].shape[-1], d // N_CHIP
        hsp = lambda tk: ((tk, d), lambda j, k: (k, 0))
        fsp = lambda tk: ((None, tk, f4), lambda j, k: (j, k, 0))
        csp = lambda tk: ((tk, r4), lambda j, k: (k, j))
        dgate, dup, df, dg2 = _ffn_bwd1(dx, sv["f"], g2, sv["wd"], sv["gate"], sv["up"])
        g_wd = _mm_tn(sv["act"], df, fsp, hsp, (f4, d), "grad_wd")
        g_wg = _mm_tn(sv["h2"], dgate, hsp, fsp, (d, f4), "grad_wg")
        g_wu = _mm_tn(sv["h2"], dup, hsp, fsp, (d, f4), "grad_wu")
        tie = send_grads(l, dict(wd=g_wd, wg=g_wg, wu=g_wu))
        dh2 = _ffn_bwd2(dgate, dup, sv["wg"], sv["wu"])
        dx1, sums2 = _lnmod_bwd(sv["x1"], ln2_g[l:l + 1], sc2 + tie, dh2, dx)
        dmo, da, db, dya, dyb, dga, dgb, dg1 = _out_bwd(dx1, sv["mo"], g1, sv["wo"], sv["pa"], sv["pb"], sv["p"],
                                                        sv["wa"], sv["wb"], d)
        g_wo = _mm_tn(sv["merged"], dmo, csp, hsp, (r4, d), "grad_wo")
        g_wa = _mm_tn(sv["ya"], da, csp, hsp, (r4, d), "grad_wa")
        g_wb = _mm_tn(sv["yb"], db, csp, hsp, (r4, d), "grad_wb")
        tie = send_grads(l, dict(wo=g_wo, wa=g_wa, wb=g_wb))
        dcb, dcc, dcx, dconv = _conv_bwd(sv["p"], conv_w[l] + tie, dyb, d)
        dq, dk, dv, dgain = _attn_bwd(sv["p"], sv["qg2"], sv["kg2"], dya, sv["lt"], d)
        dp = jnp.concatenate([dq, dk, dv, dcb, dcc, dcx, dga, dgb], axis=1)
        g_win = _mm_tn(sv["h1"], dp, hsp, lambda tk: ((tk, n4), lambda j, k: (k, j)), (d, n4), "grad_win")
        tie = send_grads(l, dict(win=g_win))
        dh1 = _mm_in_bwd(dp, sv["win"])
        dx, sums1 = _lnmod_bwd(sv["x0"], ln1_g[l:l + 1], sc1 + tie, dh1, dx1)
        dgain = jnp.sum(dgain[:, 0:2, :], axis=0)
        dgain = dgain[:, :HEAD_DIM] + dgain[:, HEAD_DIM:]
        dmod = jnp.concatenate([sums1[0], sums1[1], dg1[0], sums2[0], sums2[1], dg2[0]])
        small[l] = dict(dmod=dmod, ln1=sums1[2], ln2=sums2[2], qg=dgain[0], kg=dgain[1], conv=dconv[0:3])
    return loss_tile, dx, small


def kernel(x, c, ada_w, ada_b, ln1_g, w_in, q_norm_g, k_norm_g, conv_w, w_branch_a, w_branch_b, w_out, ln2_g, w_ffn_gate, w_ffn_up, w_ffn_down, loss_target, m_ada_w, m_ada_b, m_ln1_g, m_w_in, m_q_norm_g, m_k_norm_g, m_conv_w, m_w_branch_a, m_w_branch_b, m_w_out, m_ln2_g, m_w_ffn_gate, m_w_ffn_up, m_w_ffn_down, v_ada_w, v_ada_b, v_ln1_g, v_w_in, v_q_norm_g, v_k_norm_g, v_conv_w, v_w_branch_a, v_w_branch_b, v_w_out, v_ln2_g, v_w_ffn_gate, v_w_ffn_up, v_w_ffn_down):
    n_l, d, a4 = ada_w.shape
    cw4 = conv_w.shape[-1]
    ix, iy, ic = lax.axis_index("x"), lax.axis_index("y"), lax.axis_index("c")
    chip = 2 * ix + iy
    me = 2 * chip + ic

    big_w = dict(win=w_in, wa=w_branch_a, wb=w_branch_b, wo=w_out, wg=w_ffn_gate, wu=w_ffn_up, wd=w_ffn_down)
    big_m = dict(win=m_w_in, wa=m_w_branch_a, wb=m_w_branch_b, wo=m_w_out, wg=m_w_ffn_gate, wu=m_w_ffn_up,
                 wd=m_w_ffn_down)
    big_v = dict(win=v_w_in, wa=v_w_branch_a, wb=v_w_branch_b, wo=v_w_out, wg=v_w_ffn_gate, wu=v_w_ffn_up,
                 wd=v_w_ffn_down)

    def own_block_in_place(shard, n_slots, slot):
        return lax.dynamic_update_index_in_dim(lax.empty((n_slots,) + shard.shape, shard.dtype), shard, slot, 0)

    weight_groups = [(0, ("win",)), (0, ("wa", "wb", "wo", "wg", "wu", "wd"))] + [(l, BIG) for l in range(1, n_l)]
    started_w, tie = {}, jnp.zeros((), F32)
    for gi, (l, names) in enumerate(weight_groups):
        srcs = [big_w[k][l].astype(BF16) for k in names]
        lands = [own_block_in_place(sh, N_CHIP, chip) for sh in srcs]
        st = _weights_start("weights_start_%d" % gi, srcs, lands)
        tie = tie + st[4][0, 0]
        for k in names:
            started_w[(l, k)] = [gi, names, st, None]

    def weights(l, names, after):
        entry = started_w[(l, names[0])]
        if entry[3] is None:
            lands = _weights_wait("weights_wait_%d" % entry[0], entry[2], after)
            for k in entry[1]:
                started_w[(l, k)][3] = dict(zip(entry[1], lands))
        return [started_w[(l, k)][3][k] for k in names]

    got = _gather8(_pack([c + tie, conv_w])).reshape(N_DEV, -1)
    c_all = got[:, :d]
    conv_all = got[:, d:d + n_l * 3 * cw4].reshape(N_CHIP, 2, n_l, 3, cw4)[:, 0]
    conv_full = jnp.transpose(conv_all, (1, 2, 0, 3)).reshape(n_l, 3, N_CHIP * cw4)
    b_cols = lax.dynamic_slice_in_dim(ada_b, chip * a4, a4, axis=1).reshape(n_l, 1, a4)
    mod_cols = _ada_mod(c_all, ada_w, b_cols)
    mod_all = _gather8(_pack([mod_cols])).reshape(N_DEV, -1)[:, :n_l * N_DEV * a4]
    mod_all = mod_all.reshape(N_CHIP, 2, n_l, N_DEV, a4)[:, 0]
    mods = lax.dynamic_index_in_dim(mod_all, me, axis=2, keepdims=False)
    mods = jnp.transpose(mods, (1, 0, 2)).reshape(n_l, N_CHIP * a4)

    started_g = []

    def send_grads(l, grads):
        names = tuple(grads)
        parts = [own_block_in_place(lax.dynamic_index_in_dim(grads[k], chip, 0, keepdims=False), N_DEV, me)
                 for k in names]
        st = _grads_start("grads_start_%d" % len(started_g), [grads[k] for k in names], parts)
        started_g.append((l, names, st))
        return st[4][0, 0]

    loss_tile, grad_x, small = _local_step(
        x[0], loss_target[0], mods, ln1_g, ln2_g, q_norm_g, k_norm_g, conv_full, weights, send_grads)

    sm_shapes = [(n_l, 6 * d), (n_l, d), (n_l, d), (n_l, HEAD_DIM), (n_l, HEAD_DIM), (n_l, 3, d), (1,)]
    vec = _pack([jnp.stack([small[l][k] for l in range(n_l)]) for k in ("dmod", "ln1", "ln2", "qg", "kg", "conv")]
                + [loss_tile[0, 0:1]])
    n_vec = vec.shape[1] * 8
    all_vec = _gather8(vec).reshape(N_DEV, n_vec)
    per_dev = [_unpack(all_vec[dev], sm_shapes) for dev in range(N_DEV)]
    dmod_all = jnp.stack([pd[0] for pd in per_dev])
    dmod_cols = jnp.transpose(lax.dynamic_slice_in_dim(dmod_all, chip * a4, a4, axis=2), (1, 0, 2))
    ada_out = _ada_grad_adam(jnp.transpose(c_all), dmod_cols, ada_w, m_ada_w, v_ada_w)

    dev_parts = jnp.stack([
        _pack([pd[0], pd[1], pd[2], pd[3], pd[4], lax.dynamic_slice_in_dim(pd[5], chip * cw4, cw4, axis=2), pd[6]])
        for pd in per_dev])
    zero1 = jnp.zeros((1,), F32)
    sw = _pack([ada_b, ln1_g, ln2_g, q_norm_g, k_norm_g, conv_w, zero1])
    sm = _pack([m_ada_b, m_ln1_g, m_ln2_g, m_q_norm_g, m_k_norm_g, m_conv_w, zero1])
    sv = _pack([v_ada_b, v_ln1_g, v_ln2_g, v_q_norm_g, v_k_norm_g, v_conv_w, zero1 + 1.0])
    out_shapes = [(n_l, 6 * d), (n_l, d), (n_l, d), (n_l, HEAD_DIM), (n_l, HEAD_DIM), (n_l, 3, cw4), (1,)]
    sm_out = [_unpack(o, out_shapes) for o in _small_adam(dev_parts, sw, sm, sv)]
    loss = 0.5 * sm_out[0][6][0] / d

    big_out, after = {}, sm_out[0][0]
    for names in GRAD_GROUPS:
        got_parts = {}
        for gi, (l, sent, st) in enumerate(started_g):
            if sent == names:
                for k, part in zip(sent, _grads_wait("grads_wait_%d" % gi, st, after)):
                    got_parts[(l, k)] = part
        for k in names:
            big_out[k] = _sum_adam([got_parts[(l, k)] for l in range(n_l)], big_w[k], big_m[k], big_v[k],
                                   "sum_adam_" + k)
            after = big_out[k][0]

    outs = [loss, grad_x[None]]
    for kind in range(4):
        sm_k = sm_out[kind]
        outs += [ada_out[kind], sm_k[0], sm_k[1], big_out["win"][kind], sm_k[3], sm_k[4], sm_k[5],
                 big_out["wa"][kind], big_out["wb"][kind], big_out["wo"][kind], sm_k[2],
                 big_out["wg"][kind], big_out["wu"][kind], big_out["wd"][kind]]
    return tuple(outs)
```

```python
import math

import jax
import jax.numpy as jnp
from jax import lax
from jax.experimental import pallas as pl
from jax.experimental.pallas import tpu as pltpu

F32 = jnp.float32
BF16 = jnp.bfloat16
MESH_ID = pl.DeviceIdType.MESH

EPS = 1e-6
HEAD_DIM = 64
Q_BLOCK = 128
Q_SUPER = 1024
Q_SUPER_BWD = 512
LANES = 128
N_DEV = 8
N_CHIP = 4
VMEM_LIMIT_BYTES = 56 * 1024 * 1024

ADAM_LR = 0.001
ADAM_B1 = 0.9
ADAM_B2 = 0.999
ADAM_EPS = 1e-08
ADAM_WD = 0.01
ADAM_STEP = 10

HBM_SPEC = pl.BlockSpec(memory_space=pltpu.HBM)
ANY_SPEC = pl.BlockSpec(memory_space=pl.ANY)
SEM_SPEC = pl.BlockSpec(memory_space=pltpu.SEMAPHORE)
VMEM_SPEC = pl.BlockSpec(memory_space=pltpu.VMEM)
SIDE_EFFECT = pltpu.SideEffectType.DATAFLOW_SIDE_EFFECTING


def _params(*sem):
    return pltpu.CompilerParams(dimension_semantics=tuple(sem), vmem_limit_bytes=VMEM_LIMIT_BYTES)


def _tile(n, pref):
    return pref if n % pref == 0 else n


def _dot(a, b):
    return jnp.dot(a, b, preferred_element_type=F32)


def _dot_nt(a, b):
    return lax.dot_general(a, b, (((1,), (1,)), ((), ())), preferred_element_type=F32)


def _dot_tn(a, b):
    return lax.dot_general(a, b, (((0,), (0,)), ((), ())), preferred_element_type=F32)


def _adamw(w, g, m, v):
    m = ADAM_B1 * m + (1.0 - ADAM_B1) * g
    v = ADAM_B2 * v + (1.0 - ADAM_B2) * (g * g)
    m_hat = m / (1.0 - ADAM_B1 ** ADAM_STEP)
    v_hat = v / (1.0 - ADAM_B2 ** ADAM_STEP)
    delta = -ADAM_LR * (m_hat / (jnp.sqrt(v_hat) + ADAM_EPS) + ADAM_WD * w)
    return delta, m, v


def _hbm(a):
    return pltpu.with_memory_space_constraint(a, pltpu.HBM)


def _peer(x, y, c, k):
    return (1 - x if k & 4 else x, 1 - y if k & 2 else y, 1 - c if k & 1 else c)


def _gather8(v):
    rows_per, m = v.shape

    def body(v_ref, out_ref, send_sems, recv_sems, local_sem):
        x, y, c = lax.axis_index("x"), lax.axis_index("y"), lax.axis_index("c")

        def rows(p):
            return out_ref.at[pl.ds((4 * p[0] + 2 * p[1] + p[2]) * rows_per, rows_per), :]

        me = (x, y, c)
        mine = pltpu.make_async_copy(v_ref, rows(me), local_sem)
        mine.start()
        sends = []
        for k in range(1, N_DEV):
            cp = pltpu.make_async_remote_copy(
                src_ref=v_ref, dst_ref=rows(me), send_sem=send_sems.at[k - 1], recv_sem=recv_sems.at[k - 1],
                device_id=_peer(x, y, c, k), device_id_type=MESH_ID)
            cp.start()
            sends.append(cp)
        for k in range(1, N_DEV):
            pltpu.make_async_remote_copy(
                src_ref=v_ref, dst_ref=rows(_peer(x, y, c, k)), send_sem=send_sems.at[k - 1],
                recv_sem=recv_sems.at[k - 1], device_id=_peer(x, y, c, k), device_id_type=MESH_ID).wait_recv()
        for cp in sends:
            cp.wait_send()
        mine.wait()

    return pl.pallas_call(
        body, name="gather8",
        out_shape=jax.ShapeDtypeStruct((N_DEV * rows_per, m), v.dtype),
        in_specs=[VMEM_SPEC], out_specs=VMEM_SPEC,
        scratch_shapes=[pltpu.SemaphoreType.DMA((N_DEV - 1,)), pltpu.SemaphoreType.DMA((N_DEV - 1,)),
                        pltpu.SemaphoreType.DMA],
    )(v)


def _weight_copies(srcs, lands, send_sems, recv_sems):
    x, y, c = lax.axis_index("x"), lax.axis_index("y"), lax.axis_index("c")
    chips = [(1 - x, y), (x, 1 - y), (1 - x, 1 - y)]
    sends, recvs = [], []
    for a, (src, land) in enumerate(zip(srcs, lands)):
        for j, (px, py) in enumerate(chips):
            def copy(dst_block, a=a, j=j, px=px, py=py, src=src, land=land):
                return pltpu.make_async_remote_copy(
                    src_ref=src, dst_ref=land.at[dst_block], send_sem=send_sems.at[3 * a + j],
                    recv_sem=recv_sems.at[3 * a + j], device_id=(px, py, c), device_id_type=MESH_ID)
            sends.append(copy(2 * x + y))
            recvs.append(copy(2 * px + py))
    return sends, recvs


def _split_start(name, copies, srcs, land_shapes, sems_per_src):
    n = len(srcs)

    def body(*refs):
        sends, _ = copies(refs[:n], refs[n + 2:2 * n + 2], refs[n], refs[n + 1])
        for cp in sends:
            cp.start()
        token = refs[-1]
        token[...] = jnp.zeros_like(token)

    n_sems = sems_per_src * n
    outs = pl.pallas_call(
        body, name=name,
        out_shape=(pltpu.SemaphoreType.DMA((n_sems,)), pltpu.SemaphoreType.DMA((n_sems,)),
                   *[pltpu.HBM(shape, a.dtype) for a, shape in zip(srcs, land_shapes)],
                   jax.ShapeDtypeStruct((8, LANES), F32)),
        in_specs=[HBM_SPEC] * n, out_specs=(SEM_SPEC, SEM_SPEC, *[HBM_SPEC] * n, VMEM_SPEC),
        compiler_params=pltpu.CompilerParams(has_side_effects=SIDE_EFFECT),
    )(*[_hbm(a) for a in srcs])
    return outs[0], outs[1], list(srcs), list(outs[2:2 + n]), outs[-1]


def _split_wait(name, copies, started, after):
    send_sems, recv_sems, srcs, lands, _ = started
    n = len(srcs)

    def body(*refs):
        sends, recvs = copies(refs[:n], refs[n:2 * n], refs[2 * n], refs[2 * n + 1])
        for cp in sends:
            cp.wait_send()
        for cp in recvs:
            cp.wait_recv()

    return pl.pallas_call(
        body, name=name,
        out_shape=tuple(pltpu.HBM(a.shape, a.dtype) for a in lands),
        in_specs=[HBM_SPEC] * (2 * n) + [SEM_SPEC, SEM_SPEC, ANY_SPEC], out_specs=tuple([HBM_SPEC] * n),
        input_output_aliases={n + i: i for i in range(n)},
        compiler_params=pltpu.CompilerParams(has_side_effects=SIDE_EFFECT),
    )(*srcs, *lands, send_sems, recv_sems, after)


def _grad_copies(grads, parts, send_sems, recv_sems):
    x, y, c = lax.axis_index("x"), lax.axis_index("y"), lax.axis_index("c")
    chips = [(1 - x, y), (x, 1 - y), (1 - x, 1 - y)]
    my_slot = 4 * x + 2 * y + c
    sends, recvs = [], []
    for a, (grad, part) in enumerate(zip(grads, parts)):
        def copy(k, block, slot, to, a=a, grad=grad, part=part):
            return pltpu.make_async_remote_copy(
                src_ref=grad.at[block], dst_ref=part.at[slot], send_sem=send_sems.at[7 * a + k],
                recv_sem=recv_sems.at[7 * a + k], device_id=to, device_id_type=MESH_ID)
        sends.append(copy(0, 2 * x + y, my_slot, (x, y, 1 - c)))
        recvs.append(copy(0, 2 * x + y, 4 * x + 2 * y + (1 - c), (x, y, 1 - c)))
        for j, (px, py) in enumerate(chips):
            for other, pc in enumerate((c, 1 - c)):
                sends.append(copy(1 + 2 * j + other, 2 * px + py, my_slot, (px, py, pc)))
                recvs.append(copy(1 + 2 * j + other, 2 * x + y, 4 * px + 2 * py + pc, (px, py, pc)))
    return sends, recvs


def _ada_mod(c_all, ada_w, ada_b_cols):
    n_l, d, a4 = ada_w.shape
    tn = _tile(a4, 512)

    def body(c_ref, w_ref, b_ref, o_ref):
        cv = c_ref[...]
        ca = (cv * jax.nn.sigmoid(cv)).astype(BF16)
        o_ref[...] = _dot(ca, w_ref[...].astype(BF16)) + b_ref[...]

    return pl.pallas_call(
        body, name="ada_mod", grid=(n_l, a4 // tn),
        in_specs=[pl.BlockSpec((N_DEV, d), lambda l, j: (0, 0)),
                  pl.BlockSpec((None, d, tn), lambda l, j: (l, 0, j)),
                  pl.BlockSpec((None, 1, tn), lambda l, j: (l, 0, j))],
        out_specs=pl.BlockSpec((None, N_DEV, tn), lambda l, j: (l, 0, j)),
        out_shape=jax.ShapeDtypeStruct((n_l, N_DEV, a4), F32),
        compiler_params=_params("parallel", "parallel"),
    )(c_all, ada_w, ada_b_cols)


def _ada_grad_adam(c_all_t, dmod_cols, w, m, v):
    n_l, d, a4 = w.shape
    tn = _tile(a4, 512)

    def body(ct_ref, dm_ref, w_ref, m_ref, v_ref, g_ref, dl_ref, nm_ref, nv_ref):
        ct = ct_ref[...]
        ca = ct * jax.nn.sigmoid(ct)
        dm = dm_ref[...]
        g = ca[:, 0:1] * dm[0:1, :]
        for dev in range(1, N_DEV):
            g = g + ca[:, dev:dev + 1] * dm[dev:dev + 1, :]
        g_ref[...] = g
        delta, nm, nv = _adamw(w_ref[...], g, m_ref[...], v_ref[...])
        dl_ref[...] = delta
        nm_ref[...] = nm
        nv_ref[...] = nv

    wspec = pl.BlockSpec((None, d, tn), lambda l, j: (l, 0, j))
    shp = jax.ShapeDtypeStruct(w.shape, F32)
    return pl.pallas_call(
        body, name="ada_grad_adam", grid=(n_l, a4 // tn),
        in_specs=[pl.BlockSpec((d, N_DEV), lambda l, j: (0, 0)),
                  pl.BlockSpec((None, N_DEV, tn), lambda l, j: (l, 0, j)), wspec, wspec, wspec],
        out_specs=[wspec] * 4, out_shape=[shp] * 4,
        compiler_params=_params("parallel", "parallel"),
    )(c_all_t, dmod_cols, w, m, v)


def _lnmod(x, g, sc, sh):
    s, d = x.shape
    tm = _tile(s, 512)

    def body(x_ref, g_ref, sc_ref, sh_ref, h_ref):
        xv = x_ref[...]
        r = lax.rsqrt(jnp.mean(xv * xv, axis=-1, keepdims=True) + EPS)
        h_ref[...] = ((xv * r * g_ref[...]) * (1.0 + sc_ref[...]) + sh_ref[...]).astype(BF16)

    vec = pl.BlockSpec((1, d), lambda i: (0, 0))
    row = pl.BlockSpec((tm, d), lambda i: (i, 0))
    return pl.pallas_call(
        body, name="lnmod", grid=(s // tm,), in_specs=[row, vec, vec, vec], out_specs=row,
        out_shape=jax.ShapeDtypeStruct((s, d), BF16), compiler_params=_params("parallel"),
    )(x, g, sc, sh)


def _mm_in(h, w_g):
    s, d = h.shape
    n4 = w_g.shape[-1]
    tm = _tile(s, 512)

    def body(a_ref, b_ref, o_ref):
        o_ref[...] = _dot(a_ref[...], b_ref[...])

    return pl.pallas_call(
        body, name="mm_in", grid=(N_CHIP, s // tm),
        in_specs=[pl.BlockSpec((tm, d), lambda j, i: (i, 0)),
                  pl.BlockSpec((None, d, n4), lambda j, i: (j, 0, 0))],
        out_specs=pl.BlockSpec((tm, n4), lambda j, i: (i, j)),
        out_shape=jax.ShapeDtypeStruct((s, N_CHIP * n4), F32),
        compiler_params=_params("parallel", "parallel"),
    )(h, w_g)


def _pair_mean(x, low):
    lo = jnp.sum(jnp.where(low, x, 0.0), axis=-1, keepdims=True)
    hi = jnp.sum(jnp.where(low, 0.0, x), axis=-1, keepdims=True)
    return jnp.where(low, lo, hi) * (1.0 / HEAD_DIM)


def _pair_norm(x, low):
    r = lax.rsqrt(_pair_mean(x * x, low) + EPS)
    return x * r, r


def _log_not(z):
    return jnp.minimum(-z, 0.0) - jnp.log(1.0 + jnp.exp(-jnp.abs(z)))


def _attn_consts(inclusive):
    low = lax.broadcasted_iota(jnp.int32, (1, LANES), 1) < HEAD_DIM
    row = lax.broadcasted_iota(jnp.int32, (Q_BLOCK, Q_BLOCK), 0)
    col = lax.broadcasted_iota(jnp.int32, (Q_BLOCK, Q_BLOCK), 1)
    tri = (row <= col) if inclusive else (row > col)
    w2 = jnp.concatenate([tri.astype(BF16), jnp.ones((Q_BLOCK, Q_BLOCK), BF16)], axis=1)
    return low, col - row, jnp.concatenate([w2, w2], axis=0)


def _split_cat(v):
    hi = v.astype(BF16)
    return jnp.concatenate([hi, (v - hi.astype(F32)).astype(BF16)], axis=1)


def _fill_pair_blocks(dst, src_fn, low, n_kb):
    def fill(b, _):
        v = src_fn(pl.ds(pl.multiple_of(b * Q_BLOCK, Q_BLOCK), Q_BLOCK))
        dst[b, 0:Q_BLOCK, :] = jnp.where(low, v, 0.0).astype(BF16)
        dst[b, Q_BLOCK:2 * Q_BLOCK, :] = jnp.where(low, 0.0, v).astype(BF16)
        return 0

    lax.fori_loop(0, n_kb, fill, 0)


def _attn_fwd(p, qg2, kg2, d):
    s = p.shape[0]
    n_pairs = d // LANES
    qsb = _tile(s, Q_SUPER)
    n_sub, n_sb, n_kb = qsb // Q_BLOCK, s // qsb, s // Q_BLOCK
    chunk = _tile(s, 512)
    inv_sqrt = 1.0 / math.sqrt(HEAD_DIM)

    def body(q_ref, k_ref, v_ref, qg_ref, kg_ref, o_ref, lt_ref, qs, k2, v2, run, acc):
        low, diff, w4 = _attn_consts(False)

        def prep(r, _):
            rows = pl.ds(pl.multiple_of(r * chunk, chunk), chunk)
            qs[rows, :] = (_pair_norm(q_ref[rows, :], low)[0] * (qg_ref[...] * inv_sqrt)).astype(BF16)
            return 0

        lax.fori_loop(0, s // chunk, prep, 0)
        _fill_pair_blocks(k2, lambda rows: _pair_norm(k_ref[rows, :], low)[0] * kg_ref[...], low, n_kb)
        _fill_pair_blocks(v2, lambda rows: v_ref[rows, :], low, n_kb)

        def step(sb, j, masked):
            rows_sb = pl.ds(pl.multiple_of(sb * qsb, qsb), qsb)
            z_both = _dot_nt(qs[rows_sb, :], k2[j])
            zls, cats, keeps = [], [], []
            for t in range(n_sub):
                sub = slice(t * Q_BLOCK, (t + 1) * Q_BLOCK)
                keep = (diff < (sb * qsb + t * Q_BLOCK - j * Q_BLOCK)) if masked else None
                for h in range(2):
                    z = z_both[sub, h * LANES:(h + 1) * LANES]
                    ln = _log_not(z)
                    if masked:
                        ln = jnp.where(keep, ln, 0.0)
                    zls.append(z + ln)
                    cats.append(_split_cat(ln))
                    keeps.append(keep)
            c2 = _dot(jnp.concatenate(cats, axis=0), w4)
            a_rows = []
            for t in range(n_sub):
                sub = slice(t * Q_BLOCK, (t + 1) * Q_BLOCK)
                a_pair = []
                for h in range(2):
                    i = 2 * t + h
                    tile = slice(i * Q_BLOCK, (i + 1) * Q_BLOCK)
                    later = run[h, sub, :]
                    log_a = zls[i] + c2[tile, :LANES] + later
                    if masked:
                        log_a = jnp.where(keeps[i], log_a, -1e30)
                    a_pair.append(jnp.exp(log_a).astype(BF16))
                    run[h, sub, :] = later + c2[tile, LANES:]
                a_rows.append(jnp.concatenate(a_pair, axis=1))
            acc[...] += _dot(jnp.concatenate(a_rows, axis=0), v2[j])

        def super_block(sb, _):
            run[...] = jnp.zeros_like(run)
            acc[...] = jnp.zeros_like(acc)

            def diag(n, _):
                step(sb, sb * n_sub + n_sub - 1 - n, True)
                return 0

            def below(n, _):
                step(sb, sb * n_sub - 1 - n, False)
                return 0

            lax.fori_loop(0, n_sub, diag, 0)
            lax.fori_loop(0, sb * n_sub, below, 0)
            rows_sb = pl.ds(pl.multiple_of(sb * qsb, qsb), qsb)
            o_ref[rows_sb, :] = acc[...].astype(BF16)
            lt_ref[rows_sb, :] = jnp.where(low, run[0], run[1])
            return 0

        lax.fori_loop(0, n_sb, super_block, 0)

    def seg(k):
        return pl.BlockSpec((s, LANES), lambda h, k=k: (0, k * n_pairs + h))

    vec = pl.BlockSpec((1, LANES), lambda h: (0, 0))
    out = pl.BlockSpec((s, LANES), lambda h: (0, h))
    return pl.pallas_call(
        body, name="attn_fwd", grid=(n_pairs,),
        in_specs=[seg(0), seg(1), seg(2), vec, vec], out_specs=[out, out],
        out_shape=[jax.ShapeDtypeStruct((s, d), BF16), jax.ShapeDtypeStruct((s, d), F32)],
        scratch_shapes=[pltpu.VMEM((s, LANES), BF16)] + [pltpu.VMEM((n_kb, 2 * Q_BLOCK, LANES), BF16)] * 2
        + [pltpu.VMEM((2, qsb, LANES), F32), pltpu.VMEM((qsb, LANES), F32)],
        compiler_params=_params("parallel"),
    )(p, p, p, qg2, kg2)


def _conv_rows(s):
    return _tile(s, 512)


def _conv_fwd(p, conv_w, d):
    s = p.shape[0]
    nb = d // LANES
    rows_n = _conv_rows(s)

    def body(cb_ref, cc_ref, cx_ref, w_ref, y_ref, us):
        us[pl.ds(0, 8), :] = jnp.zeros((8, LANES), F32)

        def fill(r, _):
            rows = pl.ds(pl.multiple_of(r * rows_n, rows_n), rows_n)
            us[pl.ds(pl.multiple_of(r * rows_n + 8, 8), rows_n), :] = cc_ref[rows, :] * cx_ref[rows, :]
            return 0

        lax.fori_loop(0, s // rows_n, fill, 0)
        w = w_ref[...]

        def out(r, _):
            rows = pl.ds(pl.multiple_of(r * rows_n, rows_n), rows_n)
            ext = us[pl.ds(pl.multiple_of(r * rows_n, 8), rows_n + 8), :]
            cv = (w[0:1, :] * pltpu.roll(ext, 2, 0)[8:, :] + w[1:2, :] * pltpu.roll(ext, 1, 0)[8:, :]
                  + w[2:3, :] * ext[8:, :])
            y_ref[rows, :] = (cb_ref[rows, :] * cv).astype(BF16)
            return 0

        lax.fori_loop(0, s // rows_n, out, 0)

    def seg(k):
        return pl.BlockSpec((s, LANES), lambda b, k=k: (0, k * nb + b))

    return pl.pallas_call(
        body, name="conv_fwd", grid=(nb,),
        in_specs=[seg(3), seg(4), seg(5), pl.BlockSpec((3, LANES), lambda b: (0, b))],
        out_specs=pl.BlockSpec((s, LANES), lambda b: (0, b)),
        out_shape=jax.ShapeDtypeStruct((s, d), BF16),
        scratch_shapes=[pltpu.VMEM((s + 8, LANES), F32)],
        compiler_params=_params("parallel"),
    )(p, p, p, conv_w)


def _branch(ya, yb, p, wa, wb, d):
    s = ya.shape[0]
    tm = _tile(s, 512)

    def body(ya_ref, yb_ref, ga_ref, gb_ref, wa_ref, wb_ref, m_ref, a_ref, b_ref):
        pa = _dot(ya_ref[...], wa_ref[...])
        pb = _dot(yb_ref[...], wb_ref[...])
        m_ref[...] = (jax.nn.sigmoid(ga_ref[...]) * pa + jax.nn.sigmoid(gb_ref[...]) * pb).astype(BF16)
        a_ref[...] = pa.astype(BF16)
        b_ref[...] = pb.astype(BF16)

    row = pl.BlockSpec((tm, d), lambda i: (i, 0))
    wsp = pl.BlockSpec((d, d), lambda i: (0, 0))
    shp = jax.ShapeDtypeStruct((s, d), BF16)
    return pl.pallas_call(
        body, name="branch", grid=(s // tm,),
        in_specs=[row, row, pl.BlockSpec((tm, d), lambda i: (i, 6)), pl.BlockSpec((tm, d), lambda i: (i, 7)), wsp, wsp],
        out_specs=[row, row, row], out_shape=[shp, shp, shp], compiler_params=_params("parallel"),
    )(ya, yb, p, p, wa, wb)


def _out_proj(merged, wout, x0, g1):
    s, d = x0.shape
    tm = _tile(s, 512)

    def body(m_ref, w_ref, x_ref, g_ref, x1_ref, mo_ref):
        mo = _dot(m_ref[...], w_ref[...])
        mo_ref[...] = mo
        x1_ref[...] = x_ref[...] + g_ref[...] * mo

    row = pl.BlockSpec((tm, d), lambda i: (i, 0))
    shp = jax.ShapeDtypeStruct((s, d), F32)
    return pl.pallas_call(
        body, name="out_proj", grid=(s // tm,),
        in_specs=[row, pl.BlockSpec((d, d), lambda i: (0, 0)), row, pl.BlockSpec((1, d), lambda i: (0, 0))],
        out_specs=[row, row], out_shape=[shp, shp], compiler_params=_params("parallel"),
    )(merged, wout, x0, g1)


def _ffn_up(h, wg_g, wu_g):
    s, d = h.shape
    f4 = wg_g.shape[-1]
    tm = _tile(s, 512)

    def body(h_ref, wg_ref, wu_ref, gate_ref, up_ref, act_ref):
        hv = h_ref[...]
        gt = _dot(hv, wg_ref[...])
        up = _dot(hv, wu_ref[...])
        gate_ref[...] = gt.astype(BF16)
        up_ref[...] = up.astype(BF16)
        act_ref[...] = (gt * jax.nn.sigmoid(gt) * up).astype(BF16)

    wsp = pl.BlockSpec((None, d, f4), lambda j, i: (j, 0, 0))
    osp = pl.BlockSpec((None, tm, f4), lambda j, i: (j, i, 0))
    shp = jax.ShapeDtypeStruct((N_CHIP, s, f4), BF16)
    return pl.pallas_call(
        body, name="ffn_up", grid=(N_CHIP, s // tm),
        in_specs=[pl.BlockSpec((tm, d), lambda j, i: (i, 0)), wsp, wsp],
        out_specs=[osp, osp, osp], out_shape=[shp, shp, shp], compiler_params=_params("parallel", "parallel"),
    )(h, wg_g, wu_g)


def _ffn_down(act, wd_g, x1, g2):
    s, d = x1.shape
    f4 = act.shape[-1]
    tm = _tile(s, 512)

    def body(a_ref, w_ref, x_ref, g_ref, x2_ref, f_ref, acc):
        j = pl.program_id(1)

        @pl.when(j == 0)
        def _():
            acc[...] = jnp.zeros_like(acc)

        acc[...] += _dot(a_ref[...], w_ref[...])

        @pl.when(j == N_CHIP - 1)
        def _():
            f = acc[...]
            f_ref[...] = f
            x2_ref[...] = x_ref[...] + g_ref[...] * f

    row = pl.BlockSpec((tm, d), lambda i, j: (i, 0))
    shp = jax.ShapeDtypeStruct((s, d), F32)
    return pl.pallas_call(
        body, name="ffn_down", grid=(s // tm, N_CHIP),
        in_specs=[pl.BlockSpec((None, tm, f4), lambda i, j: (j, i, 0)),
                  pl.BlockSpec((None, f4, d), lambda i, j: (j, 0, 0)),
                  row, pl.BlockSpec((1, d), lambda i, j: (0, 0))],
        out_specs=[row, row], out_shape=[shp, shp],
        scratch_shapes=[pltpu.VMEM((tm, d), F32)], compiler_params=_params("parallel", "arbitrary"),
    )(act, wd_g, x1, g2)


def _loss_head(y, target):
    s, d = y.shape
    tm = _tile(s, 512)
    n_steps = s // tm

    def body(y_ref, t_ref, dy_ref, l_ref, acc):
        i = pl.program_id(0)

        @pl.when(i == 0)
        def _():
            acc[...] = jnp.zeros_like(acc)

        err = y_ref[...] - t_ref[...]
        dy_ref[...] = err / d
        acc[...] += jnp.sum(err * err, axis=0, keepdims=True)

        @pl.when(i == n_steps - 1)
        def _():
            l_ref[...] = jnp.broadcast_to(jnp.sum(acc[...], axis=1, keepdims=True), (8, LANES))

    row = pl.BlockSpec((tm, d), lambda i: (i, 0))
    return pl.pallas_call(
        body, name="loss_head", grid=(n_steps,), in_specs=[row, row],
        out_specs=[row, pl.BlockSpec((8, LANES), lambda i: (0, 0))],
        out_shape=[jax.ShapeDtypeStruct((s, d), F32), jax.ShapeDtypeStruct((8, LANES), F32)],
        scratch_shapes=[pltpu.VMEM((1, d), F32)], compiler_params=_params("arbitrary"),
    )(y, target)


def _mm_tn(a, b, a_spec, b_spec, out_rc, name):
    r, c = out_rc
    s = a.shape[-2]
    tk = _tile(s, 512)
    nk = s // tk

    def body(a_ref, b_ref, o_ref, acc):
        k = pl.program_id(1)

        @pl.when(k == 0)
        def _():
            acc[...] = jnp.zeros_like(acc)

        acc[...] += _dot_tn(a_ref[...], b_ref[...])

        @pl.when(k == nk - 1)
        def _():
            o_ref[...] = acc[...].astype(BF16)

    return pl.pallas_call(
        body, name=name, grid=(N_CHIP, nk),
        in_specs=[pl.BlockSpec(*a_spec(tk)), pl.BlockSpec(*b_spec(tk))],
        out_specs=pl.BlockSpec((None, r, c), lambda j, k: (j, 0, 0)),
        out_shape=jax.ShapeDtypeStruct((N_CHIP, r, c), BF16),
        scratch_shapes=[pltpu.VMEM((r, c), F32)], compiler_params=_params("parallel", "arbitrary"),
    )(a, b)


def _ffn_bwd1(dx2, f, g2, wd_g, gate, up):
    s, d = dx2.shape
    f4 = gate.shape[-1]
    tm = _tile(s, 512)

    def body(dx_ref, f_ref, g_ref, w_ref, gate_ref, up_ref, dgate_ref, dup_ref, df_ref, dg_ref):
        i, j = pl.program_id(0), pl.program_id(1)

        @pl.when((i == 0) & (j == 0))
        def _():
            dg_ref[...] = jnp.zeros_like(dg_ref)

        dxv = dx_ref[...]
        df = (g_ref[...] * dxv).astype(BF16)

        @pl.when(j == 0)
        def _():
            df_ref[...] = df
            dg_ref[0:1, :] += jnp.sum(dxv * f_ref[...], axis=0, keepdims=True)

        da = _dot_nt(df, w_ref[...])
        gt = gate_ref[...].astype(F32)
        sg = jax.nn.sigmoid(gt)
        dup_ref[...] = (da * gt * sg).astype(BF16)
        dgate_ref[...] = (da * up_ref[...].astype(F32) * (sg * (1.0 + gt * (1.0 - sg)))).astype(BF16)

    row = pl.BlockSpec((tm, d), lambda i, j: (i, 0))
    hsp = pl.BlockSpec((None, tm, f4), lambda i, j: (j, i, 0))
    hshp = jax.ShapeDtypeStruct((N_CHIP, s, f4), BF16)
    return pl.pallas_call(
        body, name="ffn_bwd1", grid=(s // tm, N_CHIP),
        in_specs=[row, row, pl.BlockSpec((1, d), lambda i, j: (0, 0)),
                  pl.BlockSpec((None, f4, d), lambda i, j: (j, 0, 0)), hsp, hsp],
        out_specs=[hsp, hsp, row, pl.BlockSpec((8, d), lambda i, j: (0, 0))],
        out_shape=[hshp, hshp, jax.ShapeDtypeStruct((s, d), BF16), jax.ShapeDtypeStruct((8, d), F32)],
        compiler_params=_params("arbitrary", "arbitrary"),
    )(dx2, f, g2, wd_g, gate, up)


def _ffn_bwd2(dgate, dup, wg_g, wu_g):
    _, s, f4 = dgate.shape
    d = wg_g.shape[-2]
    tm = _tile(s, 512)

    def body(dg_ref, du_ref, wg_ref, wu_ref, o_ref, acc):
        j = pl.program_id(1)

        @pl.when(j == 0)
        def _():
            acc[...] = jnp.zeros_like(acc)

        acc[...] += _dot_nt(dg_ref[...], wg_ref[...]) + _dot_nt(du_ref[...], wu_ref[...])

        @pl.when(j == N_CHIP - 1)
        def _():
            o_ref[...] = acc[...]

    hsp = pl.BlockSpec((None, tm, f4), lambda i, j: (j, i, 0))
    wsp = pl.BlockSpec((None, d, f4), lambda i, j: (j, 0, 0))
    return pl.pallas_call(
        body, name="ffn_bwd2", grid=(s // tm, N_CHIP), in_specs=[hsp, hsp, wsp, wsp],
        out_specs=pl.BlockSpec((tm, d), lambda i, j: (i, 0)), out_shape=jax.ShapeDtypeStruct((s, d), F32),
        scratch_shapes=[pltpu.VMEM((tm, d), F32)], compiler_params=_params("parallel", "arbitrary"),
    )(dgate, dup, wg_g, wu_g)


def _lnmod_bwd(x, g, sc, dh, dres):
    s, d = x.shape
    tm = _tile(s, 512)

    def body(x_ref, g_ref, sc_ref, dh_ref, dr_ref, dx_ref, sums_ref):
        @pl.when(pl.program_id(0) == 0)
        def _():
            sums_ref[...] = jnp.zeros_like(sums_ref)

        xv, dhv, gv = x_ref[...], dh_ref[...], g_ref[...]
        r = lax.rsqrt(jnp.mean(xv * xv, axis=-1, keepdims=True) + EPS)
        n = xv * r
        one_sc = 1.0 + sc_ref[...]
        dt = dhv * one_sc
        sums_ref[0:1, :] += jnp.sum(dhv, axis=0, keepdims=True)
        sums_ref[1:2, :] += jnp.sum(dhv * (n * gv), axis=0, keepdims=True)
        sums_ref[2:3, :] += jnp.sum(dt * n, axis=0, keepdims=True)
        dn = dt * gv
        dx_ref[...] = dr_ref[...] + r * (dn - n * jnp.mean(dn * n, axis=-1, keepdims=True))

    vec = pl.BlockSpec((1, d), lambda i: (0, 0))
    row = pl.BlockSpec((tm, d), lambda i: (i, 0))
    return pl.pallas_call(
        body, name="lnmod_bwd", grid=(s // tm,), in_specs=[row, vec, vec, row, row],
        out_specs=[row, pl.BlockSpec((8, d), lambda i: (0, 0))],
        out_shape=[jax.ShapeDtypeStruct((s, d), F32), jax.ShapeDtypeStruct((8, d), F32)],
        compiler_params=_params("arbitrary"),
    )(x, g, sc, dh, dres)


def _out_bwd(dx1, mo, g1, wout, pa, pb, p, wa, wb, d):
    s = dx1.shape[0]
    tm = _tile(s, 256)

    def body(dx_ref, mo_ref, g_ref, wo_ref, pa_ref, pb_ref, ga_ref, gb_ref, wa_ref, wb_ref,
             dmo_ref, da_ref, db_ref, dya_ref, dyb_ref, dga_ref, dgb_ref, dg_ref):
        @pl.when(pl.program_id(0) == 0)
        def _():
            dg_ref[...] = jnp.zeros_like(dg_ref)

        dxv = dx_ref[...]
        dg_ref[0:1, :] += jnp.sum(dxv * mo_ref[...], axis=0, keepdims=True)
        dmo = (g_ref[...] * dxv).astype(BF16)
        dmo_ref[...] = dmo
        dm = _dot_nt(dmo, wo_ref[...])
        sa, sb = jax.nn.sigmoid(ga_ref[...]), jax.nn.sigmoid(gb_ref[...])
        da = (dm * sa).astype(BF16)
        db = (dm * sb).astype(BF16)
        da_ref[...] = da
        db_ref[...] = db
        dga_ref[...] = (dm * pa_ref[...].astype(F32) * (sa * (1.0 - sa))).astype(BF16)
        dgb_ref[...] = (dm * pb_ref[...].astype(F32) * (sb * (1.0 - sb))).astype(BF16)
        dya_ref[...] = _dot_nt(da, wa_ref[...]).astype(BF16)
        dyb_ref[...] = _dot_nt(db, wb_ref[...]).astype(BF16)

    row = pl.BlockSpec((tm, d), lambda i: (i, 0))
    wsp = pl.BlockSpec((d, d), lambda i: (0, 0))
    shp = jax.ShapeDtypeStruct((s, d), BF16)
    return pl.pallas_call(
        body, name="out_bwd", grid=(s // tm,),
        in_specs=[row, row, pl.BlockSpec((1, d), lambda i: (0, 0)), wsp, row, row,
                  pl.BlockSpec((tm, d), lambda i: (i, 6)), pl.BlockSpec((tm, d), lambda i: (i, 7)), wsp, wsp],
        out_specs=[row] * 7 + [pl.BlockSpec((8, d), lambda i: (0, 0))],
        out_shape=[shp] * 7 + [jax.ShapeDtypeStruct((8, d), F32)],
        compiler_params=_params("arbitrary"),
    )(dx1, mo, g1, wout, pa, pb, p, p, wa, wb)


def _conv_bwd(p, conv_w, dyb, d):
    s = p.shape[0]
    nb = d // LANES
    rows_n = _conv_rows(s)

    def body(cb_ref, cc_ref, cx_ref, w_ref, dy_ref, dcb_ref, dcc_ref, dcx_ref, dw_ref, us, ds):
        us[pl.ds(0, 8), :] = jnp.zeros((8, LANES), F32)
        ds[pl.ds(s, 8), :] = jnp.zeros((8, LANES), F32)

        def fill(r, _):
            rows = pl.ds(pl.multiple_of(r * rows_n, rows_n), rows_n)
            us[pl.ds(pl.multiple_of(r * rows_n + 8, 8), rows_n), :] = cc_ref[rows, :] * cx_ref[rows, :]
            ds[rows, :] = dy_ref[rows, :].astype(F32) * cb_ref[rows, :]
            return 0

        lax.fori_loop(0, s // rows_n, fill, 0)
        w = w_ref[...]

        def out(r, carry):
            dw0, dw1, dw2 = carry
            rows = pl.ds(pl.multiple_of(r * rows_n, rows_n), rows_n)
            ext = us[pl.ds(pl.multiple_of(r * rows_n, 8), rows_n + 8), :]
            u0, u1, u2 = ext[8:, :], pltpu.roll(ext, 1, 0)[8:, :], pltpu.roll(ext, 2, 0)[8:, :]
            cv = w[0:1, :] * u2 + w[1:2, :] * u1 + w[2:3, :] * u0
            dcb_ref[rows, :] = (dy_ref[rows, :].astype(F32) * cv).astype(BF16)
            nxt = ds[pl.ds(pl.multiple_of(r * rows_n, 8), rows_n + 8), :]
            e0 = nxt[:rows_n, :]
            e1 = pltpu.roll(nxt, rows_n + 7, 0)[:rows_n, :]
            e2 = pltpu.roll(nxt, rows_n + 6, 0)[:rows_n, :]
            du = w[2:3, :] * e0 + w[1:2, :] * e1 + w[0:1, :] * e2
            dcc_ref[rows, :] = (du * cx_ref[rows, :]).astype(BF16)
            dcx_ref[rows, :] = (du * cc_ref[rows, :]).astype(BF16)
            return (dw0 + jnp.sum(e0 * u2, axis=0, keepdims=True), dw1 + jnp.sum(e0 * u1, axis=0, keepdims=True),
                    dw2 + jnp.sum(e0 * u0, axis=0, keepdims=True))

        zero = jnp.zeros((1, LANES), F32)
        dw0, dw1, dw2 = lax.fori_loop(0, s // rows_n, out, (zero, zero, zero))
        dw_ref[...] = jnp.zeros_like(dw_ref)
        dw_ref[0:1, :] = dw0
        dw_ref[1:2, :] = dw1
        dw_ref[2:3, :] = dw2

    def seg(k):
        return pl.BlockSpec((s, LANES), lambda b, k=k: (0, k * nb + b))

    col = pl.BlockSpec((s, LANES), lambda b: (0, b))
    shp = jax.ShapeDtypeStruct((s, d), BF16)
    return pl.pallas_call(
        body, name="conv_bwd", grid=(nb,),
        in_specs=[seg(3), seg(4), seg(5), pl.BlockSpec((3, LANES), lambda b: (0, b)), col],
        out_specs=[col, col, col, pl.BlockSpec((8, LANES), lambda b: (0, b))],
        out_shape=[shp, shp, shp, jax.ShapeDtypeStruct((8, d), F32)],
        scratch_shapes=[pltpu.VMEM((s + 8, LANES), F32), pltpu.VMEM((s + 8, LANES), F32)],
        compiler_params=_params("parallel"),
    )(p, p, p, conv_w, dyb)


def _attn_bwd(p, qg2, kg2, dy, lt, d):
    s = p.shape[0]
    n_pairs = d // LANES
    qsb = _tile(s, Q_SUPER_BWD)
    n_sub, n_sb, n_kb = qsb // Q_BLOCK, s // qsb, s // Q_BLOCK
    chunk = _tile(s, 512)
    inv_sqrt = 1.0 / math.sqrt(HEAD_DIM)

    def body(q_ref, k_ref, v_ref, qg_ref, kg_ref, dy_ref, lt_ref, dq_ref, dk_ref, dv_ref, dgain_ref,
             qs, k2, v2, dkt, dvt, qt, dyt, rem, gbef, dqa):
        low, diff, w4 = _attn_consts(True)

        def prep(r, _):
            rows = pl.ds(pl.multiple_of(r * chunk, chunk), chunk)
            qs[rows, :] = (_pair_norm(q_ref[rows, :], low)[0] * (qg_ref[...] * inv_sqrt)).astype(BF16)
            return 0

        lax.fori_loop(0, s // chunk, prep, 0)
        _fill_pair_blocks(k2, lambda rows: _pair_norm(k_ref[rows, :], low)[0] * kg_ref[...], low, n_kb)
        _fill_pair_blocks(v2, lambda rows: v_ref[rows, :], low, n_kb)

        def clear(b, _):
            dkt[b] = jnp.zeros((LANES, Q_BLOCK), F32)
            dvt[b] = jnp.zeros((LANES, Q_BLOCK), F32)
            return 0

        lax.fori_loop(0, n_kb, clear, 0)

        def step(sb, j, masked):
            rows_sb = pl.ds(pl.multiple_of(sb * qsb, qsb), qsb)
            kj2, vj2 = k2[j], v2[j]
            z_both = _dot_nt(qs[rows_sb, :], kj2)
            da_both = _dot_nt(dy_ref[rows_sb, :], vj2)
            zls, cats, keeps = [], [], []
            for t in range(n_sub):
                sub = slice(t * Q_BLOCK, (t + 1) * Q_BLOCK)
                keep = (diff < (sb * qsb + t * Q_BLOCK - j * Q_BLOCK)) if masked else None
                for h in range(2):
                    z = z_both[sub, h * LANES:(h + 1) * LANES]
                    ln = _log_not(z)
                    if masked:
                        ln = jnp.where(keep, ln, 0.0)
                    zls.append(z + ln)
                    cats.append(_split_cat(ln))
                    keeps.append(keep)
            c2 = _dot(jnp.concatenate(cats, axis=0), w4)
            a_rows, gs, cats = [], [], []
            for t in range(n_sub):
                sub = slice(t * Q_BLOCK, (t + 1) * Q_BLOCK)
                a_pair = []
                for h in range(2):
                    i = 2 * t + h
                    tile = slice(i * Q_BLOCK, (i + 1) * Q_BLOCK)
                    left = rem[h, sub, :]
                    log_a = zls[i] + (left - c2[tile, :LANES])
                    if masked:
                        log_a = jnp.where(keeps[i], log_a, -1e30)
                    a = jnp.exp(log_a)
                    rem[h, sub, :] = left - c2[tile, LANES:]
                    g = a * da_both[sub, h * LANES:(h + 1) * LANES]
                    a_pair.append(a.astype(BF16))
                    gs.append(g)
                    cats.append(_split_cat(g))
                a_rows.append(jnp.concatenate(a_pair, axis=1))
            c2g = _dot(jnp.concatenate(cats, axis=0), w4)
            dz_rows = []
            for t in range(n_sub):
                sub = slice(t * Q_BLOCK, (t + 1) * Q_BLOCK)
                dz_pair = []
                for h in range(2):
                    i = 2 * t + h
                    tile = slice(i * Q_BLOCK, (i + 1) * Q_BLOCK)
                    before = gbef[h, sub, :]
                    dz = gs[i] - jnp.exp(zls[i]) * (before + c2g[tile, :LANES])
                    if masked:
                        dz = jnp.where(keeps[i], dz, 0.0)
                    gbef[h, sub, :] = before + c2g[tile, LANES:]
                    dz_pair.append(dz.astype(BF16))
                dz_rows.append(jnp.concatenate(dz_pair, axis=1))
            a_both = jnp.concatenate(a_rows, axis=0)
            dz_both = jnp.concatenate(dz_rows, axis=0)
            dvt[j] += _dot(dyt[0], a_both[:, :LANES]) + _dot(dyt[1], a_both[:, LANES:])
            dkt[j] += _dot(qt[0], dz_both[:, :LANES]) + _dot(qt[1], dz_both[:, LANES:])
            dqa[...] += _dot(dz_both, kj2)

        def super_block(sb, dqg):
            rows_sb = pl.ds(pl.multiple_of(sb * qsb, qsb), qsb)
            total = lt_ref[rows_sb, :]
            other = pltpu.roll(total, HEAD_DIM, 1)
            rem[0] = jnp.where(low, total, other)
            rem[1] = jnp.where(low, other, total)
            gbef[...] = jnp.zeros_like(gbef)
            dqa[...] = jnp.zeros_like(dqa)
            qv = qs[rows_sb, :].astype(F32)
            dyv = dy_ref[rows_sb, :].astype(F32)
            qt[0] = jnp.where(low, qv, 0.0).T.astype(BF16)
            qt[1] = jnp.where(low, 0.0, qv).T.astype(BF16)
            dyt[0] = jnp.where(low, dyv, 0.0).T.astype(BF16)
            dyt[1] = jnp.where(low, 0.0, dyv).T.astype(BF16)

            def below(j, _):
                step(sb, j, False)
                return 0

            def diag(n, _):
                step(sb, sb * n_sub + n, True)
                return 0

            lax.fori_loop(0, sb * n_sub, below, 0)
            lax.fori_loop(0, n_sub, diag, 0)
            qhat, r = _pair_norm(q_ref[rows_sb, :], low)
            dqn = dqa[...]
            dqhat = dqn * (qg_ref[...] * inv_sqrt)
            dq_ref[rows_sb, :] = (r * (dqhat - qhat * _pair_mean(dqhat * qhat, low))).astype(BF16)
            return dqg + jnp.sum(dqn * qhat, axis=0, keepdims=True) * inv_sqrt

        dqg = lax.fori_loop(0, n_sb, super_block, jnp.zeros((1, LANES), F32))

        def finish(b, dkg):
            rows = pl.ds(pl.multiple_of(b * Q_BLOCK, Q_BLOCK), Q_BLOCK)
            khat, rk = _pair_norm(k_ref[rows, :], low)
            dkn = dkt[b].T
            dkhat = dkn * kg_ref[...]
            dk_ref[rows, :] = (rk * (dkhat - khat * _pair_mean(dkhat * khat, low))).astype(BF16)
            dv_ref[rows, :] = dvt[b].T.astype(BF16)
            return dkg + jnp.sum(dkn * khat, axis=0, keepdims=True)

        dkg = lax.fori_loop(0, n_kb, finish, jnp.zeros((1, LANES), F32))
        dgain_ref[...] = jnp.zeros_like(dgain_ref)
        dgain_ref[0:1, :] = dqg
        dgain_ref[1:2, :] = dkg

    def seg(k):
        return pl.BlockSpec((s, LANES), lambda h, k=k: (0, k * n_pairs + h))

    vec = pl.BlockSpec((1, LANES), lambda h: (0, 0))
    col = pl.BlockSpec((s, LANES), lambda h: (0, h))
    shp = jax.ShapeDtypeStruct((s, d), BF16)
    return pl.pallas_call(
        body, name="attn_bwd", grid=(n_pairs,),
        in_specs=[seg(0), seg(1), seg(2), vec, vec, col, col],
        out_specs=[col, col, col, pl.BlockSpec((None, 8, LANES), lambda h: (h, 0, 0))],
        out_shape=[shp, shp, shp, jax.ShapeDtypeStruct((n_pairs, 8, LANES), F32)],
        scratch_shapes=[pltpu.VMEM((s, LANES), BF16)] + [pltpu.VMEM((n_kb, 2 * Q_BLOCK, LANES), BF16)] * 2
        + [pltpu.VMEM((n_kb, LANES, Q_BLOCK), F32)] * 2
        + [pltpu.VMEM((2, LANES, qsb), BF16)] * 2
        + [pltpu.VMEM((2, qsb, LANES), F32)] * 2 + [pltpu.VMEM((qsb, LANES), F32)],
        compiler_params=_params("parallel"),
    )(p, p, p, qg2, kg2, dy, lt)


def _mm_in_bwd(dp, w_g):
    s = dp.shape[0]
    d, n4 = w_g.shape[-2:]
    tm = _tile(s, 512)

    def body(a_ref, w_ref, o_ref, acc):
        j = pl.program_id(1)

        @pl.when(j == 0)
        def _():
            acc[...] = jnp.zeros_like(acc)

        acc[...] += _dot_nt(a_ref[...], w_ref[...])

        @pl.when(j == N_CHIP - 1)
        def _():
            o_ref[...] = acc[...]

    return pl.pallas_call(
        body, name="mm_in_bwd", grid=(s // tm, N_CHIP),
        in_specs=[pl.BlockSpec((tm, n4), lambda i, j: (i, j)),
                  pl.BlockSpec((None, d, n4), lambda i, j: (j, 0, 0))],
        out_specs=pl.BlockSpec((tm, d), lambda i, j: (i, 0)), out_shape=jax.ShapeDtypeStruct((s, d), F32),
        scratch_shapes=[pltpu.VMEM((tm, d), F32)], compiler_params=_params("parallel", "arbitrary"),
    )(dp, w_g)


def _sum_adam(parts, w, m, v, name):
    n_l, r, c = w.shape
    tr = next((t for t in (256, 176, 128, 64, 32, 16) if r % t == 0 and t * c <= 256 * 1024), r)
    n_blk = r // tr

    def body(*refs):
        p_refs = refs[:n_l]
        w_ref, m_ref, v_ref, g_ref, dl_ref, nm_ref, nv_ref = refs[n_l:]
        for l in range(n_l):
            @pl.when(pl.program_id(0) == l)
            def _(p_ref=p_refs[l]):
                g = p_ref[0].astype(F32)
                for dev in range(1, N_DEV):
                    g = g + p_ref[dev].astype(F32)
                g_ref[...] = g
                delta, nm, nv = _adamw(w_ref[...], g, m_ref[...], v_ref[...])
                dl_ref[...] = delta
                nm_ref[...] = nm
                nv_ref[...] = nv

    def part_spec(l):
        return pl.BlockSpec((N_DEV, tr, c), lambda ll, i, l=l: (0, jnp.where(ll == l, i, jnp.where(ll < l, 0, n_blk - 1)), 0))

    wsp = pl.BlockSpec((None, tr, c), lambda l, i: (l, i, 0))
    shp = jax.ShapeDtypeStruct(w.shape, F32)
    return pl.pallas_call(
        body, name=name, grid=(n_l, n_blk),
        in_specs=[part_spec(l) for l in range(n_l)] + [wsp, wsp, wsp],
        out_specs=[wsp] * 4, out_shape=[shp] * 4, compiler_params=_params("arbitrary", "arbitrary"),
    )(*parts, w, m, v)


def _small_adam(parts, w, m, v):
    def body(p_ref, w_ref, m_ref, v_ref, g_ref, dl_ref, nm_ref, nv_ref):
        g = p_ref[0]
        for dev in range(1, N_DEV):
            g = g + p_ref[dev]
        g_ref[...] = g
        delta, nm, nv = _adamw(w_ref[...], g, m_ref[...], v_ref[...])
        dl_ref[...] = delta
        nm_ref[...] = nm
        nv_ref[...] = nv

    shp = jax.ShapeDtypeStruct(w.shape, F32)
    return pl.pallas_call(body, name="small_adam", in_specs=[VMEM_SPEC] * 4, out_specs=[VMEM_SPEC] * 4,
                          out_shape=[shp] * 4,
                          compiler_params=pltpu.CompilerParams(vmem_limit_bytes=VMEM_LIMIT_BYTES))(parts, w, m, v)


def _pack(vecs, mult=8 * LANES):
    flat = jnp.concatenate([a.reshape(-1).astype(F32) for a in vecs])
    pad = (-flat.shape[0]) % mult
    if pad:
        flat = jnp.concatenate([flat, jnp.zeros((pad,), F32)])
    return flat.reshape(8, -1)


def _unpack(flat, shapes):
    flat = flat.reshape(-1)
    out, off = [], 0
    for shp in shapes:
        n = math.prod(shp)
        out.append(flat[off:off + n].reshape(shp))
        off += n
    return out


BIG = ("win", "wa", "wb", "wo", "wg", "wu", "wd")
GRAD_GROUPS = (("wd", "wg", "wu"), ("wo", "wa", "wb"), ("win",))


def _local_step(x, target, mods, ln1_g, ln2_g, qg, kg, conv_w, weights, send_grads):
    s, d = x.shape
    n_l = mods.shape[0]
    saved = []
    h_in = x
    for l in range(n_l):
        sh1, sc1, g1, sh2, sc2, g2 = [mods[l, k * d:(k + 1) * d].reshape(1, d) for k in range(6)]
        qg2, kg2 = jnp.tile(qg[l:l + 1], (1, 2)), jnp.tile(kg[l:l + 1], (1, 2))
        h1 = _lnmod(h_in, ln1_g[l:l + 1], sc1, sh1)
        win, = weights(l, ("win",), h1)
        p = _mm_in(h1, win)
        ya, lt = _attn_fwd(p, qg2, kg2, d)
        yb = _conv_fwd(p, conv_w[l], d)
        wa, wb, wo, wg, wu, wd = weights(l, ("wa", "wb", "wo", "wg", "wu", "wd"), yb)
        wa, wb, wo = wa.reshape(d, d), wb.reshape(d, d), wo.reshape(d, d)
        merged, pa, pb = _branch(ya, yb, p, wa, wb, d)
        x1, mo = _out_proj(merged, wo, h_in, g1)
        h2 = _lnmod(x1, ln2_g[l:l + 1], sc2, sh2)
        gate, up, act = _ffn_up(h2, wg, wu)
        x2, f = _ffn_down(act, wd, x1, g2)
        saved.append(dict(x0=h_in, h1=h1, p=p, ya=ya, lt=lt, yb=yb, merged=merged, pa=pa, pb=pb, x1=x1, mo=mo,
                          h2=h2, gate=gate, up=up, act=act, f=f, win=win, wa=wa, wb=wb, wo=wo, wg=wg, wu=wu, wd=wd,
                          mod=(sh1, sc1, g1, sh2, sc2, g2), qg2=qg2, kg2=kg2))
        h_in = x2

    dx, loss_tile = _loss_head(h_in, target)

    small = [None] * n_l
    for l in reversed(range(n_l)):
        sv = saved[l]
        sh1, sc1, g1, sh2, sc2, g2 = sv["mod"]
        f4, n4, r4 = sv["wg"].shape[-1], sv["win"].shape[-1], d // N_CHIP
        hsp = lambda tk: ((tk, d), lambda j, k: (k, 0))
        fsp = lambda tk: ((None, tk, f4), lambda j, k: (j, k, 0))
        csp = lambda tk: ((tk, r4), lambda j, k: (k, j))
        dgate, dup, df, dg2 = _ffn_bwd1(dx, sv["f"], g2, sv["wd"], sv["gate"], sv["up"])
        g_wd = _mm_tn(sv["act"], df, fsp, hsp, (f4, d), "grad_wd")
        g_wg = _mm_tn(sv["h2"], dgate, hsp, fsp, (d, f4), "grad_wg")
        g_wu = _mm_tn(sv["h2"], dup, hsp, fsp, (d, f4), "grad_wu")
        tie = send_grads(l, dict(wd=g_wd, wg=g_wg, wu=g_wu))
        dh2 = _ffn_bwd2(dgate, dup, sv["wg"], sv["wu"])
        dx1, sums2 = _lnmod_bwd(sv["x1"], ln2_g[l:l + 1], sc2 + tie, dh2, dx)
        dmo, da, db, dya, dyb, dga, dgb, dg1 = _out_bwd(dx1, sv["mo"], g1, sv["wo"], sv["pa"], sv["pb"], sv["p"],
                                                        sv["wa"], sv["wb"], d)
        g_wo = _mm_tn(sv["merged"], dmo, csp, hsp, (r4, d), "grad_wo")
        g_wa = _mm_tn(sv["ya"], da, csp, hsp, (r4, d), "grad_wa")
        g_wb = _mm_tn(sv["yb"], db, csp, hsp, (r4, d), "grad_wb")
        tie = send_grads(l, dict(wo=g_wo, wa=g_wa, wb=g_wb))
        dcb, dcc, dcx, dconv = _conv_bwd(sv["p"], conv_w[l] + tie, dyb, d)
        dq, dk, dv, dgain = _attn_bwd(sv["p"], sv["qg2"], sv["kg2"], dya, sv["lt"], d)
        dp = jnp.concatenate([dq, dk, dv, dcb, dcc, dcx, dga, dgb], axis=1)
        g_win = _mm_tn(sv["h1"], dp, hsp, lambda tk: ((tk, n4), lambda j, k: (k, j)), (d, n4), "grad_win")
        tie = send_grads(l, dict(win=g_win))
        dh1 = _mm_in_bwd(dp, sv["win"])
        dx, sums1 = _lnmod_bwd(sv["x0"], ln1_g[l:l + 1], sc1 + tie, dh1, dx1)
        dgain = jnp.sum(dgain[:, 0:2, :], axis=0)
        dgain = dgain[:, :HEAD_DIM] + dgain[:, HEAD_DIM:]
        dmod = jnp.concatenate([sums1[0], sums1[1], dg1[0], sums2[0], sums2[1], dg2[0]])
        small[l] = dict(dmod=dmod, ln1=sums1[2], ln2=sums2[2], qg=dgain[0], kg=dgain[1], conv=dconv[0:3])
    return loss_tile, dx, small


def kernel(x, c, ada_w, ada_b, ln1_g, w_in, q_norm_g, k_norm_g, conv_w, w_branch_a, w_branch_b, w_out, ln2_g, w_ffn_gate, w_ffn_up, w_ffn_down, loss_target, m_ada_w, m_ada_b, m_ln1_g, m_w_in, m_q_norm_g, m_k_norm_g, m_conv_w, m_w_branch_a, m_w_branch_b, m_w_out, m_ln2_g, m_w_ffn_gate, m_w_ffn_up, m_w_ffn_down, v_ada_w, v_ada_b, v_ln1_g, v_w_in, v_q_norm_g, v_k_norm_g, v_conv_w, v_w_branch_a, v_w_branch_b, v_w_out, v_ln2_g, v_w_ffn_gate, v_w_ffn_up, v_w_ffn_down):
    n_l, d, a4 = ada_w.shape
    cw4 = conv_w.shape[-1]
    ix, iy, ic = lax.axis_index("x"), lax.axis_index("y"), lax.axis_index("c")
    chip = 2 * ix + iy
    me = 2 * chip + ic

    big_w = dict(win=w_in, wa=w_branch_a, wb=w_branch_b, wo=w_out, wg=w_ffn_gate, wu=w_ffn_up, wd=w_ffn_down)
    big_m = dict(win=m_w_in, wa=m_w_branch_a, wb=m_w_branch_b, wo=m_w_out, wg=m_w_ffn_gate, wu=m_w_ffn_up,
                 wd=m_w_ffn_down)
    big_v = dict(win=v_w_in, wa=v_w_branch_a, wb=v_w_branch_b, wo=v_w_out, wg=v_w_ffn_gate, wu=v_w_ffn_up,
                 wd=v_w_ffn_down)

    got = _gather8(_pack([c, conv_w])).reshape(N_DEV, -1)
    c_all = got[:, :d]
    conv_all = got[:, d:d + n_l * 3 * cw4].reshape(N_CHIP, 2, n_l, 3, cw4)[:, 0]
    conv_full = jnp.transpose(conv_all, (1, 2, 0, 3)).reshape(n_l, 3, N_CHIP * cw4)
    b_cols = lax.dynamic_slice_in_dim(ada_b, chip * a4, a4, axis=1).reshape(n_l, 1, a4)
    mod_cols = _ada_mod(c_all, ada_w, b_cols)
    mod_all = _gather8(_pack([mod_cols])).reshape(N_DEV, -1)[:, :n_l * N_DEV * a4]
    mod_all = mod_all.reshape(N_CHIP, 2, n_l, N_DEV, a4)[:, 0]
    mods = lax.dynamic_index_in_dim(mod_all, me, axis=2, keepdims=False)
    mods = jnp.transpose(mods, (1, 0, 2)).reshape(n_l, N_CHIP * a4)

    weight_groups = [(0, ("win",)), (0, ("wa", "wb", "wo", "wg", "wu", "wd"))] + [(l, BIG) for l in range(1, n_l)]
    group_srcs = [[big_w[k][l].astype(BF16) for k in names] for l, names in weight_groups]
    mods, group_srcs = lax.optimization_barrier((mods, group_srcs))
    started_w, tie = {}, jnp.zeros((), F32)
    for gi, (l, names) in enumerate(weight_groups):
        st = _split_start("weights_start_%d" % gi, _weight_copies, group_srcs[gi],
                          [(N_CHIP,) + sh.shape for sh in group_srcs[gi]], 3)
        tie = tie + st[4][0, 0]
        for k in names:
            started_w[(l, k)] = [gi, names, st, None]
    mods = mods + tie

    def weights(l, names, after):
        entry = started_w[(l, names[0])]
        if entry[3] is None:
            lands = _split_wait("weights_wait_%d" % entry[0], _weight_copies, entry[2], after)
            lands = [lax.dynamic_update_index_in_dim(land, own, chip, 0) for land, own in zip(lands, entry[2][2])]
            for k in entry[1]:
                started_w[(l, k)][3] = dict(zip(entry[1], lands))
        return [started_w[(l, k)][3][k] for k in names]

    started_g, held_back = [], []

    def start_grads(l, grads):
        names = tuple(grads)
        st = _split_start("grads_start_%d" % len(started_g), _grad_copies, [grads[k] for k in names],
                          [(N_DEV,) + grads[k].shape[1:] for k in names], 7)
        started_g.append((l, names, st))
        return st[4][0, 0]

    def send_grads(l, grads):
        if l == 0 and tuple(grads) == GRAD_GROUPS[-1]:
            held_back.append(grads)
            return jnp.zeros((), F32)
        return start_grads(l, grads)

    loss_tile, grad_x, small = _local_step(
        x[0], loss_target[0], mods, ln1_g, ln2_g, q_norm_g, k_norm_g, conv_full, weights, send_grads)

    sm_shapes = [(n_l, 6 * d), (n_l, d), (n_l, d), (n_l, HEAD_DIM), (n_l, HEAD_DIM), (n_l, 3, d), (1,)]
    vec = _pack([jnp.stack([small[l][k] for l in range(n_l)]) for k in ("dmod", "ln1", "ln2", "qg", "kg", "conv")]
                + [loss_tile[0, 0:1]])
    n_vec = vec.shape[1] * 8
    all_vec = _gather8(vec).reshape(N_DEV, n_vec)
    all_vec, held_back = lax.optimization_barrier((all_vec, held_back))
    tie = sum([start_grads(0, grads) for grads in held_back], jnp.zeros((), F32))
    per_dev = [_unpack(all_vec[dev], sm_shapes) for dev in range(N_DEV)]
    dmod_all = jnp.stack([pd[0] for pd in per_dev])
    dmod_cols = jnp.transpose(lax.dynamic_slice_in_dim(dmod_all, chip * a4, a4, axis=2), (1, 0, 2))
    ada_out = _ada_grad_adam(jnp.transpose(c_all) + tie, dmod_cols, ada_w, m_ada_w, v_ada_w)

    dev_parts = jnp.stack([
        _pack([pd[0], pd[1], pd[2], pd[3], pd[4], lax.dynamic_slice_in_dim(pd[5], chip * cw4, cw4, axis=2), pd[6]])
        for pd in per_dev])
    zero1 = jnp.zeros((1,), F32)
    sw = _pack([ada_b, ln1_g, ln2_g, q_norm_g, k_norm_g, conv_w, zero1])
    sm = _pack([m_ada_b, m_ln1_g, m_ln2_g, m_q_norm_g, m_k_norm_g, m_conv_w, zero1])
    sv = _pack([v_ada_b, v_ln1_g, v_ln2_g, v_q_norm_g, v_k_norm_g, v_conv_w, zero1 + 1.0])
    out_shapes = [(n_l, 6 * d), (n_l, d), (n_l, d), (n_l, HEAD_DIM), (n_l, HEAD_DIM), (n_l, 3, cw4), (1,)]
    sm_out = [_unpack(o, out_shapes) for o in _small_adam(dev_parts, sw, sm, sv)]
    loss = 0.5 * sm_out[0][6][0] / d

    big_out, after = {}, sm_out[0][0]
    for names in GRAD_GROUPS:
        got_parts = {}
        for gi, (l, sent, st) in enumerate(started_g):
            if sent == names:
                for k, part, grad in zip(sent, _split_wait("grads_wait_%d" % gi, _grad_copies, st, after), st[2]):
                    own = lax.dynamic_index_in_dim(grad, chip, 0, keepdims=False)
                    got_parts[(l, k)] = lax.dynamic_update_index_in_dim(part, own, me, 0)
        for k in names:
            big_out[k] = _sum_adam([got_parts[(l, k)] for l in range(n_l)], big_w[k], big_m[k], big_v[k],
                                   "sum_adam_" + k)
            after = big_out[k][0]

    outs = [loss, grad_x[None]]
    for kind in range(4):
        sm_k = sm_out[kind]
        outs += [ada_out[kind], sm_k[0], sm_k[1], big_out["win"][kind], sm_k[3], sm_k[4], sm_k[5],
                 big_out["wa"][kind], big_out["wb"][kind], big_out["wo"][kind], sm_k[2],
                 big_out["wg"][kind], big_out["wu"][kind], big_out["wd"][kind]]
    return tuple(outs)
```

```python
import math

import jax
import jax.numpy as jnp
from jax import lax
from jax.experimental import pallas as pl
from jax.experimental.pallas import tpu as pltpu

F32 = jnp.float32
BF16 = jnp.bfloat16
MESH_ID = pl.DeviceIdType.MESH

EPS = 1e-6
HEAD_DIM = 64
Q_BLOCK = 128
Q_SUPER = 1024
Q_SUPER_BWD = 1024
KEY_UNROLL = 4
LANES = 128
N_DEV = 8
N_CHIP = 4
VMEM_LIMIT_BYTES = 56 * 1024 * 1024

ADAM_LR = 0.001
ADAM_B1 = 0.9
ADAM_B2 = 0.999
ADAM_EPS = 1e-08
ADAM_WD = 0.01
ADAM_STEP = 10

HBM_SPEC = pl.BlockSpec(memory_space=pltpu.HBM)
ANY_SPEC = pl.BlockSpec(memory_space=pl.ANY)
SEM_SPEC = pl.BlockSpec(memory_space=pltpu.SEMAPHORE)
VMEM_SPEC = pl.BlockSpec(memory_space=pltpu.VMEM)
SIDE_EFFECT = pltpu.SideEffectType.DATAFLOW_SIDE_EFFECTING


def _params(*sem):
    return pltpu.CompilerParams(dimension_semantics=tuple(sem), vmem_limit_bytes=VMEM_LIMIT_BYTES)


def _tile(n, pref):
    return pref if n % pref == 0 else n


def _dot(a, b):
    return jnp.dot(a, b, preferred_element_type=F32)


def _dot_nt(a, b):
    return lax.dot_general(a, b, (((1,), (1,)), ((), ())), preferred_element_type=F32)


def _dot_tn(a, b):
    return lax.dot_general(a, b, (((0,), (0,)), ((), ())), preferred_element_type=F32)


def _adamw(w, g, m, v):
    m = ADAM_B1 * m + (1.0 - ADAM_B1) * g
    v = ADAM_B2 * v + (1.0 - ADAM_B2) * (g * g)
    m_hat = m / (1.0 - ADAM_B1 ** ADAM_STEP)
    v_hat = v / (1.0 - ADAM_B2 ** ADAM_STEP)
    delta = -ADAM_LR * (m_hat / (jnp.sqrt(v_hat) + ADAM_EPS) + ADAM_WD * w)
    return delta, m, v


def _hbm(a):
    return pltpu.with_memory_space_constraint(a, pltpu.HBM)


def _peer(x, y, c, k):
    return (1 - x if k & 4 else x, 1 - y if k & 2 else y, 1 - c if k & 1 else c)


def _gather8(v):
    rows_per, m = v.shape

    def body(v_ref, out_ref, send_sems, recv_sems, local_sem):
        x, y, c = lax.axis_index("x"), lax.axis_index("y"), lax.axis_index("c")

        def rows(p):
            return out_ref.at[pl.ds((4 * p[0] + 2 * p[1] + p[2]) * rows_per, rows_per), :]

        me = (x, y, c)
        mine = pltpu.make_async_copy(v_ref, rows(me), local_sem)
        mine.start()
        sends = []
        for k in range(1, N_DEV):
            cp = pltpu.make_async_remote_copy(
                src_ref=v_ref, dst_ref=rows(me), send_sem=send_sems.at[k - 1], recv_sem=recv_sems.at[k - 1],
                device_id=_peer(x, y, c, k), device_id_type=MESH_ID)
            cp.start()
            sends.append(cp)
        for k in range(1, N_DEV):
            pltpu.make_async_remote_copy(
                src_ref=v_ref, dst_ref=rows(_peer(x, y, c, k)), send_sem=send_sems.at[k - 1],
                recv_sem=recv_sems.at[k - 1], device_id=_peer(x, y, c, k), device_id_type=MESH_ID).wait_recv()
        for cp in sends:
            cp.wait_send()
        mine.wait()

    return pl.pallas_call(
        body, name="gather8",
        out_shape=jax.ShapeDtypeStruct((N_DEV * rows_per, m), v.dtype),
        in_specs=[VMEM_SPEC], out_specs=VMEM_SPEC,
        scratch_shapes=[pltpu.SemaphoreType.DMA((N_DEV - 1,)), pltpu.SemaphoreType.DMA((N_DEV - 1,)),
                        pltpu.SemaphoreType.DMA],
    )(v)


def _weight_copies(srcs, lands, send_sems, recv_sems):
    x, y, c = lax.axis_index("x"), lax.axis_index("y"), lax.axis_index("c")
    chips = [(1 - x, y), (x, 1 - y), (1 - x, 1 - y)]
    sends, recvs = [], []
    for a, (src, land) in enumerate(zip(srcs, lands)):
        for j, (px, py) in enumerate(chips):
            def copy(dst_block, a=a, j=j, px=px, py=py, src=src, land=land):
                return pltpu.make_async_remote_copy(
                    src_ref=src, dst_ref=land.at[dst_block], send_sem=send_sems.at[3 * a + j],
                    recv_sem=recv_sems.at[3 * a + j], device_id=(px, py, c), device_id_type=MESH_ID)
            sends.append(copy(2 * x + y))
            recvs.append(copy(2 * px + py))
    return sends, recvs


def _split_start(name, copies, srcs, land_shapes, sems_per_src):
    n = len(srcs)

    def body(*refs):
        sends, _ = copies(refs[:n], refs[n + 2:2 * n + 2], refs[n], refs[n + 1])
        for cp in sends:
            cp.start()
        token = refs[-1]
        token[...] = jnp.zeros_like(token)

    n_sems = sems_per_src * n
    outs = pl.pallas_call(
        body, name=name,
        out_shape=(pltpu.SemaphoreType.DMA((n_sems,)), pltpu.SemaphoreType.DMA((n_sems,)),
                   *[pltpu.HBM(shape, a.dtype) for a, shape in zip(srcs, land_shapes)],
                   jax.ShapeDtypeStruct((8, LANES), F32)),
        in_specs=[HBM_SPEC] * n, out_specs=(SEM_SPEC, SEM_SPEC, *[HBM_SPEC] * n, VMEM_SPEC),
        compiler_params=pltpu.CompilerParams(has_side_effects=SIDE_EFFECT),
    )(*[_hbm(a) for a in srcs])
    return outs[0], outs[1], list(srcs), list(outs[2:2 + n]), outs[-1]


def _split_wait(name, copies, started, after):
    send_sems, recv_sems, srcs, lands, _ = started
    n = len(srcs)

    def body(*refs):
        sends, recvs = copies(refs[:n], refs[n:2 * n], refs[2 * n], refs[2 * n + 1])
        for cp in sends:
            cp.wait_send()
        for cp in recvs:
            cp.wait_recv()

    return pl.pallas_call(
        body, name=name,
        out_shape=tuple(pltpu.HBM(a.shape, a.dtype) for a in lands),
        in_specs=[HBM_SPEC] * (2 * n) + [SEM_SPEC, SEM_SPEC, ANY_SPEC], out_specs=tuple([HBM_SPEC] * n),
        input_output_aliases={n + i: i for i in range(n)},
        compiler_params=pltpu.CompilerParams(has_side_effects=SIDE_EFFECT),
    )(*srcs, *lands, send_sems, recv_sems, after)


def _split_start_in_place(name, copies, bufs, sems_per_buf):
    n = len(bufs)

    def body(*refs):
        sends, _ = copies(refs[:n], refs[:n], refs[n], refs[n + 1])
        for cp in sends:
            cp.start()
        token = refs[-1]
        token[...] = jnp.zeros_like(token)

    n_sems = sems_per_buf * n
    outs = pl.pallas_call(
        body, name=name,
        out_shape=(pltpu.SemaphoreType.DMA((n_sems,)), pltpu.SemaphoreType.DMA((n_sems,)),
                   *[pltpu.HBM(a.shape, a.dtype) for a in bufs], jax.ShapeDtypeStruct((8, LANES), F32)),
        in_specs=[HBM_SPEC] * n, out_specs=(SEM_SPEC, SEM_SPEC, *[HBM_SPEC] * n, VMEM_SPEC),
        input_output_aliases={i: 2 + i for i in range(n)},
        compiler_params=pltpu.CompilerParams(has_side_effects=SIDE_EFFECT),
    )(*[_hbm(a) for a in bufs])
    return outs[0], outs[1], list(outs[2:2 + n]), outs[-1]


def _split_wait_in_place(name, copies, started, after):
    send_sems, recv_sems, bufs, _ = started
    n = len(bufs)

    def body(*refs):
        sends, recvs = copies(refs[:n], refs[:n], refs[n], refs[n + 1])
        for cp in sends:
            cp.wait_send()
        for cp in recvs:
            cp.wait_recv()

    return pl.pallas_call(
        body, name=name,
        out_shape=tuple(pltpu.HBM(a.shape, a.dtype) for a in bufs),
        in_specs=[HBM_SPEC] * n + [SEM_SPEC, SEM_SPEC, ANY_SPEC], out_specs=tuple([HBM_SPEC] * n),
        input_output_aliases={i: i for i in range(n)},
        compiler_params=pltpu.CompilerParams(has_side_effects=SIDE_EFFECT),
    )(*bufs, send_sems, recv_sems, after)


def _grad_copies(grads, parts, send_sems, recv_sems):
    x, y, c = lax.axis_index("x"), lax.axis_index("y"), lax.axis_index("c")
    chips = [(1 - x, y), (x, 1 - y), (1 - x, 1 - y)]
    my_slot = 4 * x + 2 * y + c
    sends, recvs = [], []
    for a, (grad, part) in enumerate(zip(grads, parts)):
        def copy(k, block, slot, to, a=a, grad=grad, part=part):
            return pltpu.make_async_remote_copy(
                src_ref=grad.at[block], dst_ref=part.at[slot], send_sem=send_sems.at[7 * a + k],
                recv_sem=recv_sems.at[7 * a + k], device_id=to, device_id_type=MESH_ID)
        sends.append(copy(0, 2 * x + y, my_slot, (x, y, 1 - c)))
        recvs.append(copy(0, 2 * x + y, 4 * x + 2 * y + (1 - c), (x, y, 1 - c)))
        for j, (px, py) in enumerate(chips):
            for other, pc in enumerate((c, 1 - c)):
                sends.append(copy(1 + 2 * j + other, 2 * px + py, my_slot, (px, py, pc)))
                recvs.append(copy(1 + 2 * j + other, 2 * x + y, 4 * px + 2 * py + pc, (px, py, pc)))
    return sends, recvs


def _grad_copies_same_core(grads, parts, send_sems, recv_sems):
    x, y, c = lax.axis_index("x"), lax.axis_index("y"), lax.axis_index("c")
    chips = [(1 - x, y), (x, 1 - y), (1 - x, 1 - y)]
    my_slot = 4 * x + 2 * y + c
    sends, recvs = [], []
    for a, (grad, part) in enumerate(zip(grads, parts)):
        def copy(k, block, slot, to, a=a, grad=grad, part=part):
            return pltpu.make_async_remote_copy(
                src_ref=grad.at[block], dst_ref=part.at[slot], send_sem=send_sems.at[4 * a + k],
                recv_sem=recv_sems.at[4 * a + k], device_id=to, device_id_type=MESH_ID)
        sends.append(copy(0, 2 * x + y, my_slot, (x, y, 1 - c)))
        recvs.append(copy(0, 2 * x + y, 4 * x + 2 * y + (1 - c), (x, y, 1 - c)))
        for j, (px, py) in enumerate(chips):
            sends.append(copy(1 + j, 2 * px + py, my_slot, (px, py, c)))
            recvs.append(copy(1 + j, 2 * x + y, 4 * px + 2 * py + c, (px, py, c)))
    return sends, recvs


def _grad_pass_copies(parts, same_parts, send_sems, recv_sems):
    del same_parts
    x, y, c = lax.axis_index("x"), lax.axis_index("y"), lax.axis_index("c")
    chips = [(1 - x, y), (x, 1 - y), (1 - x, 1 - y)]
    sends, recvs = [], []
    for a, part in enumerate(parts):
        for j, (px, py) in enumerate(chips):
            def copy(pc, a=a, j=j, px=px, py=py, part=part):
                slot = part.at[4 * px + 2 * py + pc]
                return pltpu.make_async_remote_copy(
                    src_ref=slot, dst_ref=slot, send_sem=send_sems.at[3 * a + j], recv_sem=recv_sems.at[3 * a + j],
                    device_id=(x, y, 1 - c), device_id_type=MESH_ID)
            sends.append(copy(c))
            recvs.append(copy(1 - c))
    return sends, recvs


def _ada_mod(c_all, ada_w, ada_b_cols):
    n_l, d, a4 = ada_w.shape
    tn = _tile(a4, 512)

    def body(c_ref, w_ref, b_ref, o_ref):
        cv = c_ref[...]
        ca = (cv * jax.nn.sigmoid(cv)).astype(BF16)
        o_ref[...] = _dot(ca, w_ref[...].astype(BF16)) + b_ref[...]

    return pl.pallas_call(
        body, name="ada_mod", grid=(n_l, a4 // tn),
        in_specs=[pl.BlockSpec((N_DEV, d), lambda l, j: (0, 0)),
                  pl.BlockSpec((None, d, tn), lambda l, j: (l, 0, j)),
                  pl.BlockSpec((None, 1, tn), lambda l, j: (l, 0, j))],
        out_specs=pl.BlockSpec((None, N_DEV, tn), lambda l, j: (l, 0, j)),
        out_shape=jax.ShapeDtypeStruct((n_l, N_DEV, a4), F32),
        compiler_params=_params("parallel", "parallel"),
    )(c_all, ada_w, ada_b_cols)


def _ada_grad_adam(c_all_t, dmod_cols, w, m, v):
    n_l, d, a4 = w.shape
    tn = _tile(a4, 512)

    def body(ct_ref, dm_ref, w_ref, m_ref, v_ref, g_ref, dl_ref, nm_ref, nv_ref):
        ct = ct_ref[...]
        ca = ct * jax.nn.sigmoid(ct)
        dm = dm_ref[...]
        g = ca[:, 0:1] * dm[0:1, :]
        for dev in range(1, N_DEV):
            g = g + ca[:, dev:dev + 1] * dm[dev:dev + 1, :]
        g_ref[...] = g
        delta, nm, nv = _adamw(w_ref[...], g, m_ref[...], v_ref[...])
        dl_ref[...] = delta
        nm_ref[...] = nm
        nv_ref[...] = nv

    wspec = pl.BlockSpec((None, d, tn), lambda l, j: (l, 0, j))
    shp = jax.ShapeDtypeStruct(w.shape, F32)
    return pl.pallas_call(
        body, name="ada_grad_adam", grid=(n_l, a4 // tn),
        in_specs=[pl.BlockSpec((d, N_DEV), lambda l, j: (0, 0)),
                  pl.BlockSpec((None, N_DEV, tn), lambda l, j: (l, 0, j)), wspec, wspec, wspec],
        out_specs=[wspec] * 4, out_shape=[shp] * 4,
        compiler_params=_params("parallel", "parallel"),
    )(c_all_t, dmod_cols, w, m, v)


def _lnmod(x, g, sc, sh):
    s, d = x.shape
    tm = _tile(s, 512)

    def body(x_ref, g_ref, sc_ref, sh_ref, h_ref):
        xv = x_ref[...]
        r = lax.rsqrt(jnp.mean(xv * xv, axis=-1, keepdims=True) + EPS)
        h_ref[...] = ((xv * r * g_ref[...]) * (1.0 + sc_ref[...]) + sh_ref[...]).astype(BF16)

    vec = pl.BlockSpec((1, d), lambda i: (0, 0))
    row = pl.BlockSpec((tm, d), lambda i: (i, 0))
    return pl.pallas_call(
        body, name="lnmod", grid=(s // tm,), in_specs=[row, vec, vec, vec], out_specs=row,
        out_shape=jax.ShapeDtypeStruct((s, d), BF16), compiler_params=_params("parallel"),
    )(x, g, sc, sh)


def _mm_in(h, w_g):
    s, d = h.shape
    n4 = w_g.shape[-1]
    tm = _tile(s, 512)

    def body(a_ref, b_ref, o_ref):
        o_ref[...] = _dot(a_ref[...], b_ref[...])

    return pl.pallas_call(
        body, name="mm_in", grid=(N_CHIP, s // tm),
        in_specs=[pl.BlockSpec((tm, d), lambda j, i: (i, 0)),
                  pl.BlockSpec((None, d, n4), lambda j, i: (j, 0, 0))],
        out_specs=pl.BlockSpec((tm, n4), lambda j, i: (i, j)),
        out_shape=jax.ShapeDtypeStruct((s, N_CHIP * n4), F32),
        compiler_params=_params("parallel", "parallel"),
    )(h, w_g)


def _pair_mean(x, low):
    lo = jnp.sum(jnp.where(low, x, 0.0), axis=-1, keepdims=True)
    hi = jnp.sum(jnp.where(low, 0.0, x), axis=-1, keepdims=True)
    return jnp.where(low, lo, hi) * (1.0 / HEAD_DIM)


def _pair_norm(x, low):
    r = lax.rsqrt(_pair_mean(x * x, low) + EPS)
    return x * r, r


def _log_not(z):
    return jnp.minimum(-z, 0.0) - jnp.log(1.0 + jnp.exp(-jnp.abs(z)))


def _attn_consts(inclusive):
    low = lax.broadcasted_iota(jnp.int32, (1, LANES), 1) < HEAD_DIM
    row = lax.broadcasted_iota(jnp.int32, (Q_BLOCK, Q_BLOCK), 0)
    col = lax.broadcasted_iota(jnp.int32, (Q_BLOCK, Q_BLOCK), 1)
    tri = (row <= col) if inclusive else (row > col)
    w2 = jnp.concatenate([tri.astype(BF16), jnp.ones((Q_BLOCK, Q_BLOCK), BF16)], axis=1)
    return low, col < row, jnp.concatenate([w2, w2], axis=0)


def _split_cat(v):
    hi = v.astype(BF16)
    return jnp.concatenate([hi, (v - hi.astype(F32)).astype(BF16)], axis=1)


def _fill_pair_blocks(dst, src_fn, low, n_kb):
    def fill(b, _):
        v = src_fn(pl.ds(pl.multiple_of(b * Q_BLOCK, Q_BLOCK), Q_BLOCK))
        dst[b, 0:Q_BLOCK, :] = jnp.where(low, v, 0.0).astype(BF16)
        dst[b, Q_BLOCK:2 * Q_BLOCK, :] = jnp.where(low, 0.0, v).astype(BF16)
        return 0

    lax.fori_loop(0, n_kb, fill, 0)


def _attn_fwd(p, qg2, kg2, d):
    s = p.shape[0]
    n_pairs = d // LANES
    qsb = _tile(s, Q_SUPER)
    n_sub, n_sb, n_kb = qsb // Q_BLOCK, s // qsb, s // Q_BLOCK
    unroll = math.gcd(KEY_UNROLL, n_sub)
    chunk = _tile(s, 512)
    inv_sqrt = 1.0 / math.sqrt(HEAD_DIM)

    def body(q_ref, k_ref, v_ref, qg_ref, kg_ref, o_ref, lt_ref, qs, k2, v2, run, acc):
        low, causal, w4 = _attn_consts(False)

        def prep(r, _):
            rows = pl.ds(pl.multiple_of(r * chunk, chunk), chunk)
            qs[rows, :] = (_pair_norm(q_ref[rows, :], low)[0] * (qg_ref[...] * inv_sqrt)).astype(BF16)
            return 0

        lax.fori_loop(0, s // chunk, prep, 0)
        _fill_pair_blocks(k2, lambda rows: _pair_norm(k_ref[rows, :], low)[0] * kg_ref[...], low, n_kb)
        _fill_pair_blocks(v2, lambda rows: v_ref[rows, :], low, n_kb)

        def step(sb, j, t0=0, diag_t=None):
            rows = pl.ds(pl.multiple_of(sb * qsb + t0 * Q_BLOCK, Q_BLOCK), (n_sub - t0) * Q_BLOCK)
            z_both = _dot_nt(qs[rows, :], k2[j])
            zls, cats = [], []
            for t in range(t0, n_sub):
                sub = slice((t - t0) * Q_BLOCK, (t - t0 + 1) * Q_BLOCK)
                for h in range(2):
                    z = z_both[sub, h * LANES:(h + 1) * LANES]
                    ln = _log_not(z)
                    if t == diag_t:
                        ln = jnp.where(causal, ln, 0.0)
                    zls.append(z + ln)
                    cats.append(_split_cat(ln))
            c2 = _dot(jnp.concatenate(cats, axis=0), w4)
            a_rows = []
            for t in range(t0, n_sub):
                sub = slice(t * Q_BLOCK, (t + 1) * Q_BLOCK)
                a_pair = []
                for h in range(2):
                    i = 2 * (t - t0) + h
                    tile = slice(i * Q_BLOCK, (i + 1) * Q_BLOCK)
                    later = run[h, sub, :]
                    log_a = zls[i] + c2[tile, :LANES] + later
                    if t == diag_t:
                        log_a = jnp.where(causal, log_a, -1e30)
                    a_pair.append(jnp.exp(log_a).astype(BF16))
                    run[h, sub, :] = later + c2[tile, LANES:]
                a_rows.append(jnp.concatenate(a_pair, axis=1))
            acc[t0 * Q_BLOCK:, :] += _dot(jnp.concatenate(a_rows, axis=0), v2[j])

        def super_block(sb, _):
            run[...] = jnp.zeros_like(run)
            acc[...] = jnp.zeros_like(acc)
            for t in reversed(range(n_sub)):
                step(sb, sb * n_sub + t, t0=t, diag_t=t)

            def below(n, _):
                for u in range(unroll):
                    step(sb, sb * n_sub - 1 - (unroll * n + u))
                return 0

            lax.fori_loop(0, sb * (n_sub // unroll), below, 0)
            rows_sb = pl.ds(pl.multiple_of(sb * qsb, qsb), qsb)
            o_ref[rows_sb, :] = acc[...].astype(BF16)
            lt_ref[rows_sb, :] = jnp.where(low, run[0], run[1])
            return 0

        lax.fori_loop(0, n_sb, super_block, 0)

    def seg(k):
        return pl.BlockSpec((s, LANES), lambda h, k=k: (0, k * n_pairs + h))

    vec = pl.BlockSpec((1, LANES), lambda h: (0, 0))
    out = pl.BlockSpec((s, LANES), lambda h: (0, h))
    return pl.pallas_call(
        body, name="attn_fwd", grid=(n_pairs,),
        in_specs=[seg(0), seg(1), seg(2), vec, vec], out_specs=[out, out],
        out_shape=[jax.ShapeDtypeStruct((s, d), BF16), jax.ShapeDtypeStruct((s, d), F32)],
        scratch_shapes=[pltpu.VMEM((s, LANES), BF16)] + [pltpu.VMEM((n_kb, 2 * Q_BLOCK, LANES), BF16)] * 2
        + [pltpu.VMEM((2, qsb, LANES), F32), pltpu.VMEM((qsb, LANES), F32)],
        compiler_params=_params("parallel"),
    )(p, p, p, qg2, kg2)


def _conv_rows(s):
    return _tile(s, 512)


def _conv_fwd(p, conv_w, d):
    s = p.shape[0]
    nb = d // LANES
    rows_n = _conv_rows(s)

    def body(cb_ref, cc_ref, cx_ref, w_ref, y_ref, us):
        us[pl.ds(0, 8), :] = jnp.zeros((8, LANES), F32)

        def fill(r, _):
            rows = pl.ds(pl.multiple_of(r * rows_n, rows_n), rows_n)
            us[pl.ds(pl.multiple_of(r * rows_n + 8, 8), rows_n), :] = cc_ref[rows, :] * cx_ref[rows, :]
            return 0

        lax.fori_loop(0, s // rows_n, fill, 0)
        w = w_ref[...]

        def out(r, _):
            rows = pl.ds(pl.multiple_of(r * rows_n, rows_n), rows_n)
            ext = us[pl.ds(pl.multiple_of(r * rows_n, 8), rows_n + 8), :]
            cv = (w[0:1, :] * pltpu.roll(ext, 2, 0)[8:, :] + w[1:2, :] * pltpu.roll(ext, 1, 0)[8:, :]
                  + w[2:3, :] * ext[8:, :])
            y_ref[rows, :] = (cb_ref[rows, :] * cv).astype(BF16)
            return 0

        lax.fori_loop(0, s // rows_n, out, 0)

    def seg(k):
        return pl.BlockSpec((s, LANES), lambda b, k=k: (0, k * nb + b))

    return pl.pallas_call(
        body, name="conv_fwd", grid=(nb,),
        in_specs=[seg(3), seg(4), seg(5), pl.BlockSpec((3, LANES), lambda b: (0, b))],
        out_specs=pl.BlockSpec((s, LANES), lambda b: (0, b)),
        out_shape=jax.ShapeDtypeStruct((s, d), BF16),
        scratch_shapes=[pltpu.VMEM((s + 8, LANES), F32)],
        compiler_params=_params("parallel"),
    )(p, p, p, conv_w)


def _branch(ya, yb, p, wa, wb, d):
    s = ya.shape[0]
    tm = _tile(s, 512)

    def body(ya_ref, yb_ref, ga_ref, gb_ref, wa_ref, wb_ref, m_ref, a_ref, b_ref):
        pa = _dot(ya_ref[...], wa_ref[...])
        pb = _dot(yb_ref[...], wb_ref[...])
        m_ref[...] = (jax.nn.sigmoid(ga_ref[...]) * pa + jax.nn.sigmoid(gb_ref[...]) * pb).astype(BF16)
        a_ref[...] = pa.astype(BF16)
        b_ref[...] = pb.astype(BF16)

    row = pl.BlockSpec((tm, d), lambda i: (i, 0))
    wsp = pl.BlockSpec((d, d), lambda i: (0, 0))
    shp = jax.ShapeDtypeStruct((s, d), BF16)
    return pl.pallas_call(
        body, name="branch", grid=(s // tm,),
        in_specs=[row, row, pl.BlockSpec((tm, d), lambda i: (i, 6)), pl.BlockSpec((tm, d), lambda i: (i, 7)), wsp, wsp],
        out_specs=[row, row, row], out_shape=[shp, shp, shp], compiler_params=_params("parallel"),
    )(ya, yb, p, p, wa, wb)


def _out_proj(merged, wout, x0, g1):
    s, d = x0.shape
    tm = _tile(s, 512)

    def body(m_ref, w_ref, x_ref, g_ref, x1_ref, mo_ref):
        mo = _dot(m_ref[...], w_ref[...])
        mo_ref[...] = mo
        x1_ref[...] = x_ref[...] + g_ref[...] * mo

    row = pl.BlockSpec((tm, d), lambda i: (i, 0))
    shp = jax.ShapeDtypeStruct((s, d), F32)
    return pl.pallas_call(
        body, name="out_proj", grid=(s // tm,),
        in_specs=[row, pl.BlockSpec((d, d), lambda i: (0, 0)), row, pl.BlockSpec((1, d), lambda i: (0, 0))],
        out_specs=[row, row], out_shape=[shp, shp], compiler_params=_params("parallel"),
    )(merged, wout, x0, g1)


def _ffn_up(h, wg_g, wu_g):
    s, d = h.shape
    f4 = wg_g.shape[-1]
    tm = _tile(s, 512)

    def body(h_ref, wg_ref, wu_ref, gate_ref, up_ref, act_ref):
        hv = h_ref[...]
        gt = _dot(hv, wg_ref[...])
        up = _dot(hv, wu_ref[...])
        gate_ref[...] = gt.astype(BF16)
        up_ref[...] = up.astype(BF16)
        act_ref[...] = (gt * jax.nn.sigmoid(gt) * up).astype(BF16)

    wsp = pl.BlockSpec((None, d, f4), lambda j, i: (j, 0, 0))
    osp = pl.BlockSpec((None, tm, f4), lambda j, i: (j, i, 0))
    shp = jax.ShapeDtypeStruct((N_CHIP, s, f4), BF16)
    return pl.pallas_call(
        body, name="ffn_up", grid=(N_CHIP, s // tm),
        in_specs=[pl.BlockSpec((tm, d), lambda j, i: (i, 0)), wsp, wsp],
        out_specs=[osp, osp, osp], out_shape=[shp, shp, shp], compiler_params=_params("parallel", "parallel"),
    )(h, wg_g, wu_g)


def _ffn_down(act, wd_g, x1, g2):
    s, d = x1.shape
    f4 = act.shape[-1]
    tm = _tile(s, 512)

    def body(a_ref, w_ref, x_ref, g_ref, x2_ref, f_ref, acc):
        j = pl.program_id(1)

        @pl.when(j == 0)
        def _():
            acc[...] = jnp.zeros_like(acc)

        acc[...] += _dot(a_ref[...], w_ref[...])

        @pl.when(j == N_CHIP - 1)
        def _():
            f = acc[...]
            f_ref[...] = f
            x2_ref[...] = x_ref[...] + g_ref[...] * f

    row = pl.BlockSpec((tm, d), lambda i, j: (i, 0))
    shp = jax.ShapeDtypeStruct((s, d), F32)
    return pl.pallas_call(
        body, name="ffn_down", grid=(s // tm, N_CHIP),
        in_specs=[pl.BlockSpec((None, tm, f4), lambda i, j: (j, i, 0)),
                  pl.BlockSpec((None, f4, d), lambda i, j: (j, 0, 0)),
                  row, pl.BlockSpec((1, d), lambda i, j: (0, 0))],
        out_specs=[row, row], out_shape=[shp, shp],
        scratch_shapes=[pltpu.VMEM((tm, d), F32)], compiler_params=_params("parallel", "arbitrary"),
    )(act, wd_g, x1, g2)


def _loss_head(y, target):
    s, d = y.shape
    tm = _tile(s, 512)
    n_steps = s // tm

    def body(y_ref, t_ref, dy_ref, l_ref, acc):
        i = pl.program_id(0)

        @pl.when(i == 0)
        def _():
            acc[...] = jnp.zeros_like(acc)

        err = y_ref[...] - t_ref[...]
        dy_ref[...] = err / d
        acc[...] += jnp.sum(err * err, axis=0, keepdims=True)

        @pl.when(i == n_steps - 1)
        def _():
            l_ref[...] = jnp.broadcast_to(jnp.sum(acc[...], axis=1, keepdims=True), (8, LANES))

    row = pl.BlockSpec((tm, d), lambda i: (i, 0))
    return pl.pallas_call(
        body, name="loss_head", grid=(n_steps,), in_specs=[row, row],
        out_specs=[row, pl.BlockSpec((8, LANES), lambda i: (0, 0))],
        out_shape=[jax.ShapeDtypeStruct((s, d), F32), jax.ShapeDtypeStruct((8, LANES), F32)],
        scratch_shapes=[pltpu.VMEM((1, d), F32)], compiler_params=_params("arbitrary"),
    )(y, target)


def _mm_tn(a, b, a_spec, b_spec, out_rc, name):
    r, c = out_rc
    s = a.shape[-2]
    tk = _tile(s, 512)
    nk = s // tk

    def body(a_ref, b_ref, o_ref, acc):
        k = pl.program_id(1)

        @pl.when(k == 0)
        def _():
            acc[...] = jnp.zeros_like(acc)

        acc[...] += _dot_tn(a_ref[...], b_ref[...])

        @pl.when(k == nk - 1)
        def _():
            o_ref[...] = acc[...].astype(BF16)

    return pl.pallas_call(
        body, name=name, grid=(N_CHIP, nk),
        in_specs=[pl.BlockSpec(*a_spec(tk)), pl.BlockSpec(*b_spec(tk))],
        out_specs=pl.BlockSpec((None, r, c), lambda j, k: (j, 0, 0)),
        out_shape=jax.ShapeDtypeStruct((N_CHIP, r, c), BF16),
        scratch_shapes=[pltpu.VMEM((r, c), F32)], compiler_params=_params("parallel", "arbitrary"),
    )(a, b)


def _ffn_bwd1(dx2, f, g2, wd_g, gate, up):
    s, d = dx2.shape
    f4 = gate.shape[-1]
    tm = _tile(s, 512)

    def body(dx_ref, f_ref, g_ref, w_ref, gate_ref, up_ref, dgate_ref, dup_ref, df_ref, dg_ref):
        i, j = pl.program_id(0), pl.program_id(1)

        @pl.when((i == 0) & (j == 0))
        def _():
            dg_ref[...] = jnp.zeros_like(dg_ref)

        dxv = dx_ref[...]
        df = (g_ref[...] * dxv).astype(BF16)

        @pl.when(j == 0)
        def _():
            df_ref[...] = df
            dg_ref[0:1, :] += jnp.sum(dxv * f_ref[...], axis=0, keepdims=True)

        da = _dot_nt(df, w_ref[...])
        gt = gate_ref[...].astype(F32)
        sg = jax.nn.sigmoid(gt)
        dup_ref[...] = (da * gt * sg).astype(BF16)
        dgate_ref[...] = (da * up_ref[...].astype(F32) * (sg * (1.0 + gt * (1.0 - sg)))).astype(BF16)

    row = pl.BlockSpec((tm, d), lambda i, j: (i, 0))
    hsp = pl.BlockSpec((None, tm, f4), lambda i, j: (j, i, 0))
    hshp = jax.ShapeDtypeStruct((N_CHIP, s, f4), BF16)
    return pl.pallas_call(
        body, name="ffn_bwd1", grid=(s // tm, N_CHIP),
        in_specs=[row, row, pl.BlockSpec((1, d), lambda i, j: (0, 0)),
                  pl.BlockSpec((None, f4, d), lambda i, j: (j, 0, 0)), hsp, hsp],
        out_specs=[hsp, hsp, row, pl.BlockSpec((8, d), lambda i, j: (0, 0))],
        out_shape=[hshp, hshp, jax.ShapeDtypeStruct((s, d), BF16), jax.ShapeDtypeStruct((8, d), F32)],
        compiler_params=_params("arbitrary", "arbitrary"),
    )(dx2, f, g2, wd_g, gate, up)


def _ffn_bwd2(dgate, dup, wg_g, wu_g):
    _, s, f4 = dgate.shape
    d = wg_g.shape[-2]
    tm = _tile(s, 512)

    def body(dg_ref, du_ref, wg_ref, wu_ref, o_ref, acc):
        j = pl.program_id(1)

        @pl.when(j == 0)
        def _():
            acc[...] = jnp.zeros_like(acc)

        acc[...] += _dot_nt(dg_ref[...], wg_ref[...]) + _dot_nt(du_ref[...], wu_ref[...])

        @pl.when(j == N_CHIP - 1)
        def _():
            o_ref[...] = acc[...]

    hsp = pl.BlockSpec((None, tm, f4), lambda i, j: (j, i, 0))
    wsp = pl.BlockSpec((None, d, f4), lambda i, j: (j, 0, 0))
    return pl.pallas_call(
        body, name="ffn_bwd2", grid=(s // tm, N_CHIP), in_specs=[hsp, hsp, wsp, wsp],
        out_specs=pl.BlockSpec((tm, d), lambda i, j: (i, 0)), out_shape=jax.ShapeDtypeStruct((s, d), F32),
        scratch_shapes=[pltpu.VMEM((tm, d), F32)], compiler_params=_params("parallel", "arbitrary"),
    )(dgate, dup, wg_g, wu_g)


def _lnmod_bwd(x, g, sc, dh, dres):
    s, d = x.shape
    tm = _tile(s, 512)

    def body(x_ref, g_ref, sc_ref, dh_ref, dr_ref, dx_ref, sums_ref):
        @pl.when(pl.program_id(0) == 0)
        def _():
            sums_ref[...] = jnp.zeros_like(sums_ref)

        xv, dhv, gv = x_ref[...], dh_ref[...], g_ref[...]
        r = lax.rsqrt(jnp.mean(xv * xv, axis=-1, keepdims=True) + EPS)
        n = xv * r
        one_sc = 1.0 + sc_ref[...]
        dt = dhv * one_sc
        sums_ref[0:1, :] += jnp.sum(dhv, axis=0, keepdims=True)
        sums_ref[1:2, :] += jnp.sum(dhv * (n * gv), axis=0, keepdims=True)
        sums_ref[2:3, :] += jnp.sum(dt * n, axis=0, keepdims=True)
        dn = dt * gv
        dx_ref[...] = dr_ref[...] + r * (dn - n * jnp.mean(dn * n, axis=-1, keepdims=True))

    vec = pl.BlockSpec((1, d), lambda i: (0, 0))
    row = pl.BlockSpec((tm, d), lambda i: (i, 0))
    return pl.pallas_call(
        body, name="lnmod_bwd", grid=(s // tm,), in_specs=[row, vec, vec, row, row],
        out_specs=[row, pl.BlockSpec((8, d), lambda i: (0, 0))],
        out_shape=[jax.ShapeDtypeStruct((s, d), F32), jax.ShapeDtypeStruct((8, d), F32)],
        compiler_params=_params("arbitrary"),
    )(x, g, sc, dh, dres)


def _out_bwd(dx1, mo, g1, wout, pa, pb, p, wa, wb, d):
    s = dx1.shape[0]
    tm = _tile(s, 256)

    def body(dx_ref, mo_ref, g_ref, wo_ref, pa_ref, pb_ref, ga_ref, gb_ref, wa_ref, wb_ref,
             dmo_ref, da_ref, db_ref, dya_ref, dyb_ref, dga_ref, dgb_ref, dg_ref):
        @pl.when(pl.program_id(0) == 0)
        def _():
            dg_ref[...] = jnp.zeros_like(dg_ref)

        dxv = dx_ref[...]
        dg_ref[0:1, :] += jnp.sum(dxv * mo_ref[...], axis=0, keepdims=True)
        dmo = (g_ref[...] * dxv).astype(BF16)
        dmo_ref[...] = dmo
        dm = _dot_nt(dmo, wo_ref[...])
        sa, sb = jax.nn.sigmoid(ga_ref[...]), jax.nn.sigmoid(gb_ref[...])
        da = (dm * sa).astype(BF16)
        db = (dm * sb).astype(BF16)
        da_ref[...] = da
        db_ref[...] = db
        dga_ref[...] = (dm * pa_ref[...].astype(F32) * (sa * (1.0 - sa))).astype(BF16)
        dgb_ref[...] = (dm * pb_ref[...].astype(F32) * (sb * (1.0 - sb))).astype(BF16)
        dya_ref[...] = _dot_nt(da, wa_ref[...]).astype(BF16)
        dyb_ref[...] = _dot_nt(db, wb_ref[...]).astype(BF16)

    row = pl.BlockSpec((tm, d), lambda i: (i, 0))
    wsp = pl.BlockSpec((d, d), lambda i: (0, 0))
    shp = jax.ShapeDtypeStruct((s, d), BF16)
    return pl.pallas_call(
        body, name="out_bwd", grid=(s // tm,),
        in_specs=[row, row, pl.BlockSpec((1, d), lambda i: (0, 0)), wsp, row, row,
                  pl.BlockSpec((tm, d), lambda i: (i, 6)), pl.BlockSpec((tm, d), lambda i: (i, 7)), wsp, wsp],
        out_specs=[row] * 7 + [pl.BlockSpec((8, d), lambda i: (0, 0))],
        out_shape=[shp] * 7 + [jax.ShapeDtypeStruct((8, d), F32)],
        compiler_params=_params("arbitrary"),
    )(dx1, mo, g1, wout, pa, pb, p, p, wa, wb)


def _conv_bwd(p, conv_w, dyb, d):
    s = p.shape[0]
    nb = d // LANES
    rows_n = _conv_rows(s)

    def body(cb_ref, cc_ref, cx_ref, w_ref, dy_ref, dcb_ref, dcc_ref, dcx_ref, dw_ref, us, ds):
        us[pl.ds(0, 8), :] = jnp.zeros((8, LANES), F32)
        ds[pl.ds(s, 8), :] = jnp.zeros((8, LANES), F32)

        def fill(r, _):
            rows = pl.ds(pl.multiple_of(r * rows_n, rows_n), rows_n)
            us[pl.ds(pl.multiple_of(r * rows_n + 8, 8), rows_n), :] = cc_ref[rows, :] * cx_ref[rows, :]
            ds[rows, :] = dy_ref[rows, :].astype(F32) * cb_ref[rows, :]
            return 0

        lax.fori_loop(0, s // rows_n, fill, 0)
        w = w_ref[...]

        def out(r, carry):
            dw0, dw1, dw2 = carry
            rows = pl.ds(pl.multiple_of(r * rows_n, rows_n), rows_n)
            ext = us[pl.ds(pl.multiple_of(r * rows_n, 8), rows_n + 8), :]
            u0, u1, u2 = ext[8:, :], pltpu.roll(ext, 1, 0)[8:, :], pltpu.roll(ext, 2, 0)[8:, :]
            cv = w[0:1, :] * u2 + w[1:2, :] * u1 + w[2:3, :] * u0
            dcb_ref[rows, :] = (dy_ref[rows, :].astype(F32) * cv).astype(BF16)
            nxt = ds[pl.ds(pl.multiple_of(r * rows_n, 8), rows_n + 8), :]
            e0 = nxt[:rows_n, :]
            e1 = pltpu.roll(nxt, rows_n + 7, 0)[:rows_n, :]
            e2 = pltpu.roll(nxt, rows_n + 6, 0)[:rows_n, :]
            du = w[2:3, :] * e0 + w[1:2, :] * e1 + w[0:1, :] * e2
            dcc_ref[rows, :] = (du * cx_ref[rows, :]).astype(BF16)
            dcx_ref[rows, :] = (du * cc_ref[rows, :]).astype(BF16)
            return (dw0 + jnp.sum(e0 * u2, axis=0, keepdims=True), dw1 + jnp.sum(e0 * u1, axis=0, keepdims=True),
                    dw2 + jnp.sum(e0 * u0, axis=0, keepdims=True))

        zero = jnp.zeros((1, LANES), F32)
        dw0, dw1, dw2 = lax.fori_loop(0, s // rows_n, out, (zero, zero, zero))
        dw_ref[...] = jnp.zeros_like(dw_ref)
        dw_ref[0:1, :] = dw0
        dw_ref[1:2, :] = dw1
        dw_ref[2:3, :] = dw2

    def seg(k):
        return pl.BlockSpec((s, LANES), lambda b, k=k: (0, k * nb + b))

    col = pl.BlockSpec((s, LANES), lambda b: (0, b))
    shp = jax.ShapeDtypeStruct((s, d), BF16)
    return pl.pallas_call(
        body, name="conv_bwd", grid=(nb,),
        in_specs=[seg(3), seg(4), seg(5), pl.BlockSpec((3, LANES), lambda b: (0, b)), col],
        out_specs=[col, col, col, pl.BlockSpec((8, LANES), lambda b: (0, b))],
        out_shape=[shp, shp, shp, jax.ShapeDtypeStruct((8, d), F32)],
        scratch_shapes=[pltpu.VMEM((s + 8, LANES), F32), pltpu.VMEM((s + 8, LANES), F32)],
        compiler_params=_params("parallel"),
    )(p, p, p, conv_w, dyb)


def _attn_bwd(p, qg2, kg2, dy, lt, d):
    s = p.shape[0]
    n_pairs = d // LANES
    qsb = _tile(s, Q_SUPER_BWD)
    n_sub, n_sb, n_kb = qsb // Q_BLOCK, s // qsb, s // Q_BLOCK
    unroll = math.gcd(KEY_UNROLL, n_sub)
    chunk = _tile(s, 512)
    inv_sqrt = 1.0 / math.sqrt(HEAD_DIM)

    def body(q_ref, k_ref, v_ref, qg_ref, kg_ref, dy_ref, lt_ref, dq_ref, dk_ref, dv_ref, dgain_ref,
             qs, k2, v2, dkt, dvt, qt, dyt, rem, gbef, dqa):
        low, causal, w4 = _attn_consts(True)

        def prep(r, _):
            rows = pl.ds(pl.multiple_of(r * chunk, chunk), chunk)
            qs[rows, :] = (_pair_norm(q_ref[rows, :], low)[0] * (qg_ref[...] * inv_sqrt)).astype(BF16)
            return 0

        lax.fori_loop(0, s // chunk, prep, 0)
        _fill_pair_blocks(k2, lambda rows: _pair_norm(k_ref[rows, :], low)[0] * kg_ref[...], low, n_kb)
        _fill_pair_blocks(v2, lambda rows: v_ref[rows, :], low, n_kb)

        def clear(b, _):
            dkt[b] = jnp.zeros((LANES, Q_BLOCK), F32)
            dvt[b] = jnp.zeros((LANES, Q_BLOCK), F32)
            return 0

        lax.fori_loop(0, n_kb, clear, 0)

        def step(sb, j, t0=0, diag_t=None):
            rows = pl.ds(pl.multiple_of(sb * qsb + t0 * Q_BLOCK, Q_BLOCK), (n_sub - t0) * Q_BLOCK)
            kj2, vj2 = k2[j], v2[j]
            z_both = _dot_nt(qs[rows, :], kj2)
            da_both = _dot_nt(dy_ref[rows, :], vj2)
            zls, cats = [], []
            for t in range(t0, n_sub):
                sub = slice((t - t0) * Q_BLOCK, (t - t0 + 1) * Q_BLOCK)
                for h in range(2):
                    z = z_both[sub, h * LANES:(h + 1) * LANES]
                    ln = _log_not(z)
                    if t == diag_t:
                        ln = jnp.where(causal, ln, 0.0)
                    zls.append(z + ln)
                    cats.append(_split_cat(ln))
            c2 = _dot(jnp.concatenate(cats, axis=0), w4)
            a_rows, gs, cats = [], [], []
            for t in range(t0, n_sub):
                sub = slice(t * Q_BLOCK, (t + 1) * Q_BLOCK)
                a_pair = []
                for h in range(2):
                    i = 2 * (t - t0) + h
                    tile = slice(i * Q_BLOCK, (i + 1) * Q_BLOCK)
                    left = rem[h, sub, :]
                    log_a = zls[i] + (left - c2[tile, :LANES])
                    if t == diag_t:
                        log_a = jnp.where(causal, log_a, -1e30)
                    a = jnp.exp(log_a)
                    rem[h, sub, :] = left - c2[tile, LANES:]
                    g = a * da_both[(t - t0) * Q_BLOCK:(t - t0 + 1) * Q_BLOCK, h * LANES:(h + 1) * LANES]
                    a_pair.append(a.astype(BF16))
                    gs.append(g)
                    cats.append(_split_cat(g))
                a_rows.append(jnp.concatenate(a_pair, axis=1))
            c2g = _dot(jnp.concatenate(cats, axis=0), w4)
            dz_rows = []
            for t in range(t0, n_sub):
                sub = slice(t * Q_BLOCK, (t + 1) * Q_BLOCK)
                dz_pair = []
                for h in range(2):
                    i = 2 * (t - t0) + h
                    tile = slice(i * Q_BLOCK, (i + 1) * Q_BLOCK)
                    before = gbef[h, sub, :]
                    dz = gs[i] - jnp.exp(zls[i]) * (before + c2g[tile, :LANES])
                    if t == diag_t:
                        dz = jnp.where(causal, dz, 0.0)
                    gbef[h, sub, :] = before + c2g[tile, LANES:]
                    dz_pair.append(dz.astype(BF16))
                dz_rows.append(jnp.concatenate(dz_pair, axis=1))
            a_both = jnp.concatenate(a_rows, axis=0)
            dz_both = jnp.concatenate(dz_rows, axis=0)
            used = slice(t0 * Q_BLOCK, qsb)
            dvt[j] += _dot(dyt[0, :, used], a_both[:, :LANES]) + _dot(dyt[1, :, used], a_both[:, LANES:])
            dkt[j] += _dot(qt[0, :, used], dz_both[:, :LANES]) + _dot(qt[1, :, used], dz_both[:, LANES:])
            dqa[used, :] += _dot(dz_both, kj2)

        def super_block(sb, dqg):
            rows_sb = pl.ds(pl.multiple_of(sb * qsb, qsb), qsb)
            total = lt_ref[rows_sb, :]
            other = pltpu.roll(total, HEAD_DIM, 1)
            rem[0] = jnp.where(low, total, other)
            rem[1] = jnp.where(low, other, total)
            gbef[...] = jnp.zeros_like(gbef)
            dqa[...] = jnp.zeros_like(dqa)
            qv = qs[rows_sb, :].astype(F32)
            dyv = dy_ref[rows_sb, :].astype(F32)
            qt[0] = jnp.where(low, qv, 0.0).T.astype(BF16)
            qt[1] = jnp.where(low, 0.0, qv).T.astype(BF16)
            dyt[0] = jnp.where(low, dyv, 0.0).T.astype(BF16)
            dyt[1] = jnp.where(low, 0.0, dyv).T.astype(BF16)

            def below(n, _):
                for u in range(unroll):
                    step(sb, unroll * n + u)
                return 0

            lax.fori_loop(0, sb * (n_sub // unroll), below, 0)
            for t in range(n_sub):
                step(sb, sb * n_sub + t, t0=t, diag_t=t)
            qhat, r = _pair_norm(q_ref[rows_sb, :], low)
            dqn = dqa[...]
            dqhat = dqn * (qg_ref[...] * inv_sqrt)
            dq_ref[rows_sb, :] = (r * (dqhat - qhat * _pair_mean(dqhat * qhat, low))).astype(BF16)
            return dqg + jnp.sum(dqn * qhat, axis=0, keepdims=True) * inv_sqrt

        dqg = lax.fori_loop(0, n_sb, super_block, jnp.zeros((1, LANES), F32))

        def finish(b, dkg):
            rows = pl.ds(pl.multiple_of(b * Q_BLOCK, Q_BLOCK), Q_BLOCK)
            khat, rk = _pair_norm(k_ref[rows, :], low)
            dkn = dkt[b].T
            dkhat = dkn * kg_ref[...]
            dk_ref[rows, :] = (rk * (dkhat - khat * _pair_mean(dkhat * khat, low))).astype(BF16)
            dv_ref[rows, :] = dvt[b].T.astype(BF16)
            return dkg + jnp.sum(dkn * khat, axis=0, keepdims=True)

        dkg = lax.fori_loop(0, n_kb, finish, jnp.zeros((1, LANES), F32))
        dgain_ref[...] = jnp.zeros_like(dgain_ref)
        dgain_ref[0:1, :] = dqg
        dgain_ref[1:2, :] = dkg

    def seg(k):
        return pl.BlockSpec((s, LANES), lambda h, k=k: (0, k * n_pairs + h))

    vec = pl.BlockSpec((1, LANES), lambda h: (0, 0))
    col = pl.BlockSpec((s, LANES), lambda h: (0, h))
    shp = jax.ShapeDtypeStruct((s, d), BF16)
    return pl.pallas_call(
        body, name="attn_bwd", grid=(n_pairs,),
        in_specs=[seg(0), seg(1), seg(2), vec, vec, col, col],
        out_specs=[col, col, col, pl.BlockSpec((None, 8, LANES), lambda h: (h, 0, 0))],
        out_shape=[shp, shp, shp, jax.ShapeDtypeStruct((n_pairs, 8, LANES), F32)],
        scratch_shapes=[pltpu.VMEM((s, LANES), BF16)] + [pltpu.VMEM((n_kb, 2 * Q_BLOCK, LANES), BF16)] * 2
        + [pltpu.VMEM((n_kb, LANES, Q_BLOCK), F32)] * 2
        + [pltpu.VMEM((2, LANES, qsb), BF16)] * 2
        + [pltpu.VMEM((2, qsb, LANES), F32)] * 2 + [pltpu.VMEM((qsb, LANES), F32)],
        compiler_params=_params("parallel"),
    )(p, p, p, qg2, kg2, dy, lt)


def _mm_in_bwd(dp, w_g):
    s = dp.shape[0]
    d, n4 = w_g.shape[-2:]
    tm = _tile(s, 512)

    def body(a_ref, w_ref, o_ref, acc):
        j = pl.program_id(1)

        @pl.when(j == 0)
        def _():
            acc[...] = jnp.zeros_like(acc)

        acc[...] += _dot_nt(a_ref[...], w_ref[...])

        @pl.when(j == N_CHIP - 1)
        def _():
            o_ref[...] = acc[...]

    return pl.pallas_call(
        body, name="mm_in_bwd", grid=(s // tm, N_CHIP),
        in_specs=[pl.BlockSpec((tm, n4), lambda i, j: (i, j)),
                  pl.BlockSpec((None, d, n4), lambda i, j: (j, 0, 0))],
        out_specs=pl.BlockSpec((tm, d), lambda i, j: (i, 0)), out_shape=jax.ShapeDtypeStruct((s, d), F32),
        scratch_shapes=[pltpu.VMEM((tm, d), F32)], compiler_params=_params("parallel", "arbitrary"),
    )(dp, w_g)


def _sum_adam(parts, w, m, v, name):
    n_l, r, c = w.shape
    tr = next((t for t in (256, 176, 128, 64, 32, 16) if r % t == 0 and t * c <= 256 * 1024), r)
    n_blk = r // tr

    def body(*refs):
        p_refs = refs[:n_l]
        w_ref, m_ref, v_ref, g_ref, dl_ref, nm_ref, nv_ref = refs[n_l:]
        for l in range(n_l):
            @pl.when(pl.program_id(0) == l)
            def _(p_ref=p_refs[l]):
                g = p_ref[0].astype(F32)
                for dev in range(1, N_DEV):
                    g = g + p_ref[dev].astype(F32)
                g_ref[...] = g
                delta, nm, nv = _adamw(w_ref[...], g, m_ref[...], v_ref[...])
                dl_ref[...] = delta
                nm_ref[...] = nm
                nv_ref[...] = nv

    def part_spec(l):
        return pl.BlockSpec((N_DEV, tr, c), lambda ll, i, l=l: (0, jnp.where(ll == l, i, jnp.where(ll < l, 0, n_blk - 1)), 0))

    wsp = pl.BlockSpec((None, tr, c), lambda l, i: (l, i, 0))
    shp = jax.ShapeDtypeStruct(w.shape, F32)
    return pl.pallas_call(
        body, name=name, grid=(n_l, n_blk),
        in_specs=[part_spec(l) for l in range(n_l)] + [wsp, wsp, wsp],
        out_specs=[wsp] * 4, out_shape=[shp] * 4, compiler_params=_params("arbitrary", "arbitrary"),
    )(*parts, w, m, v)


def _small_adam(parts, w, m, v):
    def body(p_ref, w_ref, m_ref, v_ref, g_ref, dl_ref, nm_ref, nv_ref):
        g = p_ref[0]
        for dev in range(1, N_DEV):
            g = g + p_ref[dev]
        g_ref[...] = g
        delta, nm, nv = _adamw(w_ref[...], g, m_ref[...], v_ref[...])
        dl_ref[...] = delta
        nm_ref[...] = nm
        nv_ref[...] = nv

    shp = jax.ShapeDtypeStruct(w.shape, F32)
    return pl.pallas_call(body, name="small_adam", in_specs=[VMEM_SPEC] * 4, out_specs=[VMEM_SPEC] * 4,
                          out_shape=[shp] * 4,
                          compiler_params=pltpu.CompilerParams(vmem_limit_bytes=VMEM_LIMIT_BYTES))(parts, w, m, v)


def _pack(vecs, mult=8 * LANES):
    flat = jnp.concatenate([a.reshape(-1).astype(F32) for a in vecs])
    pad = (-flat.shape[0]) % mult
    if pad:
        flat = jnp.concatenate([flat, jnp.zeros((pad,), F32)])
    return flat.reshape(8, -1)


def _unpack(flat, shapes):
    flat = flat.reshape(-1)
    out, off = [], 0
    for shp in shapes:
        n = math.prod(shp)
        out.append(flat[off:off + n].reshape(shp))
        off += n
    return out


BIG = ("win", "wa", "wb", "wo", "wg", "wu", "wd")
GRAD_GROUPS = (("wd", "wg", "wu"), ("wo", "wa", "wb"), ("win",))


def _local_step(x, target, mods, ln1_g, ln2_g, qg, kg, conv_w, weights, send_grads):
    s, d = x.shape
    n_l = mods.shape[0]
    saved = []
    h_in = x
    for l in range(n_l):
        sh1, sc1, g1, sh2, sc2, g2 = [mods[l, k * d:(k + 1) * d].reshape(1, d) for k in range(6)]
        qg2, kg2 = jnp.tile(qg[l:l + 1], (1, 2)), jnp.tile(kg[l:l + 1], (1, 2))
        h1 = _lnmod(h_in, ln1_g[l:l + 1], sc1, sh1)
        (win,), h1 = weights(l, ("win",), h1)
        p = _mm_in(h1, win)
        ya, lt = _attn_fwd(p, qg2, kg2, d)
        yb = _conv_fwd(p, conv_w[l], d)
        (wa, wb, wo, wg, wu, wd), yb = weights(l, ("wa", "wb", "wo", "wg", "wu", "wd"), yb)
        wa, wb, wo = wa.reshape(d, d), wb.reshape(d, d), wo.reshape(d, d)
        merged, pa, pb = _branch(ya, yb, p, wa, wb, d)
        x1, mo = _out_proj(merged, wo, h_in, g1)
        h2 = _lnmod(x1, ln2_g[l:l + 1], sc2, sh2)
        gate, up, act = _ffn_up(h2, wg, wu)
        x2, f = _ffn_down(act, wd, x1, g2)
        saved.append(dict(x0=h_in, h1=h1, p=p, ya=ya, lt=lt, yb=yb, merged=merged, pa=pa, pb=pb, x1=x1, mo=mo,
                          h2=h2, gate=gate, up=up, act=act, f=f, win=win, wa=wa, wb=wb, wo=wo, wg=wg, wu=wu, wd=wd,
                          mod=(sh1, sc1, g1, sh2, sc2, g2), qg2=qg2, kg2=kg2))
        h_in = x2

    dx, loss_tile = _loss_head(h_in, target)

    small = [None] * n_l
    for l in reversed(range(n_l)):
        sv = saved[l]
        sh1, sc1, g1, sh2, sc2, g2 = sv["mod"]
        f4, n4, r4 = sv["wg"].shape[-1], sv["win"].shape[-1], d // N_CHIP
        hsp = lambda tk: ((tk, d), lambda j, k: (k, 0))
        fsp = lambda tk: ((None, tk, f4), lambda j, k: (j, k, 0))
        csp = lambda tk: ((tk, r4), lambda j, k: (k, j))
        dgate, dup, df, dg2 = _ffn_bwd1(dx, sv["f"], g2, sv["wd"], sv["gate"], sv["up"])
        g_wd = _mm_tn(sv["act"], df, fsp, hsp, (f4, d), "grad_wd")
        g_wg = _mm_tn(sv["h2"], dgate, hsp, fsp, (d, f4), "grad_wg")
        g_wu = _mm_tn(sv["h2"], dup, hsp, fsp, (d, f4), "grad_wu")
        tie = send_grads(l, dict(wd=g_wd, wg=g_wg, wu=g_wu))
        dh2 = _ffn_bwd2(dgate, dup, sv["wg"], sv["wu"])
        dx1, sums2 = _lnmod_bwd(sv["x1"], ln2_g[l:l + 1], sc2 + tie, dh2, dx)
        dmo, da, db, dya, dyb, dga, dgb, dg1 = _out_bwd(dx1, sv["mo"], g1, sv["wo"], sv["pa"], sv["pb"], sv["p"],
                                                        sv["wa"], sv["wb"], d)
        g_wo = _mm_tn(sv["merged"], dmo, csp, hsp, (r4, d), "grad_wo")
        g_wa = _mm_tn(sv["ya"], da, csp, hsp, (r4, d), "grad_wa")
        g_wb = _mm_tn(sv["yb"], db, csp, hsp, (r4, d), "grad_wb")
        tie = send_grads(l, dict(wo=g_wo, wa=g_wa, wb=g_wb))
        dcb, dcc, dcx, dconv = _conv_bwd(sv["p"], conv_w[l] + tie, dyb, d)
        dq, dk, dv, dgain = _attn_bwd(sv["p"], sv["qg2"], sv["kg2"], dya, sv["lt"], d)
        dp = jnp.concatenate([dq, dk, dv, dcb, dcc, dcx, dga, dgb], axis=1)
        g_win = _mm_tn(sv["h1"], dp, hsp, lambda tk: ((tk, n4), lambda j, k: (k, j)), (d, n4), "grad_win")
        tie = send_grads(l, dict(win=g_win))
        dh1 = _mm_in_bwd(dp, sv["win"])
        dx, sums1 = _lnmod_bwd(sv["x0"], ln1_g[l:l + 1], sc1 + tie, dh1, dx1)
        dgain = jnp.sum(dgain[:, 0:2, :], axis=0)
        dgain = dgain[:, :HEAD_DIM] + dgain[:, HEAD_DIM:]
        dmod = jnp.concatenate([sums1[0], sums1[1], dg1[0], sums2[0], sums2[1], dg2[0]])
        small[l] = dict(dmod=dmod, ln1=sums1[2], ln2=sums2[2], qg=dgain[0], kg=dgain[1], conv=dconv[0:3])
    return loss_tile, dx, small


def kernel(x, c, ada_w, ada_b, ln1_g, w_in, q_norm_g, k_norm_g, conv_w, w_branch_a, w_branch_b, w_out, ln2_g, w_ffn_gate, w_ffn_up, w_ffn_down, loss_target, m_ada_w, m_ada_b, m_ln1_g, m_w_in, m_q_norm_g, m_k_norm_g, m_conv_w, m_w_branch_a, m_w_branch_b, m_w_out, m_ln2_g, m_w_ffn_gate, m_w_ffn_up, m_w_ffn_down, v_ada_w, v_ada_b, v_ln1_g, v_w_in, v_q_norm_g, v_k_norm_g, v_conv_w, v_w_branch_a, v_w_branch_b, v_w_out, v_ln2_g, v_w_ffn_gate, v_w_ffn_up, v_w_ffn_down):
    n_l, d, a4 = ada_w.shape
    cw4 = conv_w.shape[-1]
    ix, iy, ic = lax.axis_index("x"), lax.axis_index("y"), lax.axis_index("c")
    chip = 2 * ix + iy
    me = 2 * chip + ic

    big_w = dict(win=w_in, wa=w_branch_a, wb=w_branch_b, wo=w_out, wg=w_ffn_gate, wu=w_ffn_up, wd=w_ffn_down)
    big_m = dict(win=m_w_in, wa=m_w_branch_a, wb=m_w_branch_b, wo=m_w_out, wg=m_w_ffn_gate, wu=m_w_ffn_up,
                 wd=m_w_ffn_down)
    big_v = dict(win=v_w_in, wa=v_w_branch_a, wb=v_w_branch_b, wo=v_w_out, wg=v_w_ffn_gate, wu=v_w_ffn_up,
                 wd=v_w_ffn_down)

    got = _gather8(_pack([c, conv_w])).reshape(N_DEV, -1)
    c_all = got[:, :d]
    conv_all = got[:, d:d + n_l * 3 * cw4].reshape(N_CHIP, 2, n_l, 3, cw4)[:, 0]
    conv_full = jnp.transpose(conv_all, (1, 2, 0, 3)).reshape(n_l, 3, N_CHIP * cw4)
    b_cols = lax.dynamic_slice_in_dim(ada_b, chip * a4, a4, axis=1).reshape(n_l, 1, a4)
    mod_cols = _ada_mod(c_all, ada_w, b_cols)
    mod_all = _gather8(_pack([mod_cols])).reshape(N_DEV, -1)[:, :n_l * N_DEV * a4]
    mod_all = mod_all.reshape(N_CHIP, 2, n_l, N_DEV, a4)[:, 0]
    mods = lax.dynamic_index_in_dim(mod_all, me, axis=2, keepdims=False)
    mods = jnp.transpose(mods, (1, 0, 2)).reshape(n_l, N_CHIP * a4)

    weight_groups = [(0, ("win",)), (0, ("wa", "wb", "wo", "wg", "wu", "wd"))] + [(l, BIG) for l in range(1, n_l)]
    group_srcs = [[big_w[k][l].astype(BF16) for k in names] for l, names in weight_groups]
    mods, group_srcs = lax.optimization_barrier((mods, group_srcs))
    started_w = {}

    def start_weights(gi):
        l, names = weight_groups[gi]
        st = _split_start("weights_start_%d" % gi, _weight_copies, group_srcs[gi],
                          [(N_CHIP,) + sh.shape for sh in group_srcs[gi]], 3)
        for k in names:
            started_w[(l, k)] = [gi, names, st, None]
        return st[4]

    mods, _ = lax.optimization_barrier((mods, start_weights(0)))

    def weights(l, names, after):
        entry = started_w[(l, names[0])]
        if entry[3] is None:
            lands = _split_wait("weights_wait_%d" % entry[0], _weight_copies, entry[2], after)
            if entry[0] == 0:
                rest = list(range(1, len(weight_groups)))
                lands, rest_srcs = lax.optimization_barrier((lands, [group_srcs[gi] for gi in rest]))
                for gi, srcs in zip(rest, rest_srcs):
                    group_srcs[gi] = srcs
                after, _ = lax.optimization_barrier((after, [start_weights(gi) for gi in rest]))
            lands = [lax.dynamic_update_index_in_dim(land, own, chip, 0) for land, own in zip(lands, entry[2][2])]
            for k in entry[1]:
                started_w[(l, k)][3] = dict(zip(entry[1], lands))
        return [started_w[(l, k)][3][k] for k in names], after

    started_g, held_back = [], []

    def start_grads(l, grads, copies=_grad_copies, sems_per=7):
        names = tuple(grads)
        st = _split_start("grads_start_%d" % len(started_g), copies, [grads[k] for k in names],
                          [(N_DEV,) + grads[k].shape[1:] for k in names], sems_per)
        started_g.append((l, names, st, copies))
        return st[4][0, 0]

    def send_grads(l, grads):
        if l == 0 and tuple(grads) == GRAD_GROUPS[-1]:
            held_back.append(grads)
            return jnp.zeros((), F32)
        return start_grads(l, grads)

    loss_tile, grad_x, small = _local_step(
        x[0], loss_target[0], mods, ln1_g, ln2_g, q_norm_g, k_norm_g, conv_full, weights, send_grads)

    sm_shapes = [(n_l, 6 * d), (n_l, d), (n_l, d), (n_l, HEAD_DIM), (n_l, HEAD_DIM), (n_l, 3, d), (1,)]
    vec = _pack([jnp.stack([small[l][k] for l in range(n_l)]) for k in ("dmod", "ln1", "ln2", "qg", "kg", "conv")]
                + [loss_tile[0, 0:1]])
    n_vec = vec.shape[1] * 8
    all_vec = _gather8(vec).reshape(N_DEV, n_vec)
    all_vec, held_back = lax.optimization_barrier((all_vec, held_back))
    tie = sum([start_grads(0, grads, _grad_copies_same_core, 4) for grads in held_back], jnp.zeros((), F32))
    per_dev = [_unpack(all_vec[dev], sm_shapes) for dev in range(N_DEV)]
    dmod_all = jnp.stack([pd[0] for pd in per_dev])
    dmod_cols = jnp.transpose(lax.dynamic_slice_in_dim(dmod_all, chip * a4, a4, axis=2), (1, 0, 2))
    ada_out = _ada_grad_adam(jnp.transpose(c_all) + tie, dmod_cols, ada_w, m_ada_w, v_ada_w)

    dev_parts = jnp.stack([
        _pack([pd[0], pd[1], pd[2], pd[3], pd[4], lax.dynamic_slice_in_dim(pd[5], chip * cw4, cw4, axis=2), pd[6]])
        for pd in per_dev])
    zero1 = jnp.zeros((1,), F32)
    sw = _pack([ada_b, ln1_g, ln2_g, q_norm_g, k_norm_g, conv_w, zero1])
    sm = _pack([m_ada_b, m_ln1_g, m_ln2_g, m_q_norm_g, m_k_norm_g, m_conv_w, zero1])
    sv = _pack([v_ada_b, v_ln1_g, v_ln2_g, v_q_norm_g, v_k_norm_g, v_conv_w, zero1 + 1.0])
    out_shapes = [(n_l, 6 * d), (n_l, d), (n_l, d), (n_l, HEAD_DIM), (n_l, HEAD_DIM), (n_l, 3, cw4), (1,)]
    sm_out = [_unpack(o, out_shapes) for o in _small_adam(dev_parts, sw, sm, sv)]
    loss = 0.5 * sm_out[0][6][0] / d

    big_out, after = {}, sm_out[0][0]
    for names in GRAD_GROUPS:
        got_parts = {}
        for gi, (l, sent, st, copies) in enumerate(started_g):
            if sent == names:
                parts = _split_wait("grads_wait_%d" % gi, copies, st, after)
                if copies is _grad_copies_same_core:
                    passed = _split_start_in_place("grads_pass_start_%d" % gi, _grad_pass_copies, parts, 3)
                    parts = _split_wait_in_place("grads_pass_wait_%d" % gi, _grad_pass_copies, passed, passed[3])
                for k, part, grad in zip(sent, parts, st[2]):
                    own = lax.dynamic_index_in_dim(grad, chip, 0, keepdims=False)
                    got_parts[(l, k)] = lax.dynamic_update_index_in_dim(part, own, me, 0)
        for k in names:
            big_out[k] = _sum_adam([got_parts[(l, k)] for l in range(n_l)], big_w[k], big_m[k], big_v[k],
                                   "sum_adam_" + k)
            after = big_out[k][0]

    outs = [loss, grad_x[None]]
    for kind in range(4):
        sm_k = sm_out[kind]
        outs += [ada_out[kind], sm_k[0], sm_k[1], big_out["win"][kind], sm_k[3], sm_k[4], sm_k[5],
                 big_out["wa"][kind], big_out["wb"][kind], big_out["wo"][kind], sm_k[2],
                 big_out["wg"][kind], big_out["wu"][kind], big_out["wd"][kind]]
    return tuple(outs)
```

```python
import math

import jax
import jax.numpy as jnp
from jax import lax
from jax.experimental import pallas as pl
from jax.experimental.pallas import tpu as pltpu

F32 = jnp.float32
BF16 = jnp.bfloat16
MESH_ID = pl.DeviceIdType.MESH

EPS = 1e-6
HEAD_DIM = 64
Q_BLOCK = 128
Q_SUPER = 1024
Q_SUPER_BWD = 1024
KEY_UNROLL = 4
LANES = 128
N_DEV = 8
N_CHIP = 4
VMEM_LIMIT_BYTES = 56 * 1024 * 1024

ADAM_LR = 0.001
ADAM_B1 = 0.9
ADAM_B2 = 0.999
ADAM_EPS = 1e-08
ADAM_WD = 0.01
ADAM_STEP = 10

HBM_SPEC = pl.BlockSpec(memory_space=pltpu.HBM)
ANY_SPEC = pl.BlockSpec(memory_space=pl.ANY)
SEM_SPEC = pl.BlockSpec(memory_space=pltpu.SEMAPHORE)
VMEM_SPEC = pl.BlockSpec(memory_space=pltpu.VMEM)
SIDE_EFFECT = pltpu.SideEffectType.DATAFLOW_SIDE_EFFECTING


def _params(*sem):
    return pltpu.CompilerParams(dimension_semantics=tuple(sem), vmem_limit_bytes=VMEM_LIMIT_BYTES)


def _tile(n, pref):
    return pref if n % pref == 0 else n


def _dot(a, b):
    return jnp.dot(a, b, preferred_element_type=F32)


def _dot_nt(a, b):
    return lax.dot_general(a, b, (((1,), (1,)), ((), ())), preferred_element_type=F32)


def _dot_tn(a, b):
    return lax.dot_general(a, b, (((0,), (0,)), ((), ())), preferred_element_type=F32)


def _adamw(w, g, m, v):
    m = ADAM_B1 * m + (1.0 - ADAM_B1) * g
    v = ADAM_B2 * v + (1.0 - ADAM_B2) * (g * g)
    m_hat = m / (1.0 - ADAM_B1 ** ADAM_STEP)
    v_hat = v / (1.0 - ADAM_B2 ** ADAM_STEP)
    delta = -ADAM_LR * (m_hat / (jnp.sqrt(v_hat) + ADAM_EPS) + ADAM_WD * w)
    return delta, m, v


def _hbm(a):
    return pltpu.with_memory_space_constraint(a, pltpu.HBM)


def _peer(x, y, c, k):
    return (1 - x if k & 4 else x, 1 - y if k & 2 else y, 1 - c if k & 1 else c)


def _gather8(v):
    rows_per, m = v.shape

    def body(v_ref, out_ref, send_sems, recv_sems, local_sem):
        x, y, c = lax.axis_index("x"), lax.axis_index("y"), lax.axis_index("c")

        def rows(p):
            return out_ref.at[pl.ds((4 * p[0] + 2 * p[1] + p[2]) * rows_per, rows_per), :]

        me = (x, y, c)
        mine = pltpu.make_async_copy(v_ref, rows(me), local_sem)
        mine.start()
        sends = []
        for k in range(1, N_DEV):
            cp = pltpu.make_async_remote_copy(
                src_ref=v_ref, dst_ref=rows(me), send_sem=send_sems.at[k - 1], recv_sem=recv_sems.at[k - 1],
                device_id=_peer(x, y, c, k), device_id_type=MESH_ID)
            cp.start()
            sends.append(cp)
        for k in range(1, N_DEV):
            pltpu.make_async_remote_copy(
                src_ref=v_ref, dst_ref=rows(_peer(x, y, c, k)), send_sem=send_sems.at[k - 1],
                recv_sem=recv_sems.at[k - 1], device_id=_peer(x, y, c, k), device_id_type=MESH_ID).wait_recv()
        for cp in sends:
            cp.wait_send()
        mine.wait()

    return pl.pallas_call(
        body, name="gather8",
        out_shape=jax.ShapeDtypeStruct((N_DEV * rows_per, m), v.dtype),
        in_specs=[VMEM_SPEC], out_specs=VMEM_SPEC,
        scratch_shapes=[pltpu.SemaphoreType.DMA((N_DEV - 1,)), pltpu.SemaphoreType.DMA((N_DEV - 1,)),
                        pltpu.SemaphoreType.DMA],
    )(v)


def _weight_copies(srcs, lands, send_sems, recv_sems):
    x, y, c = lax.axis_index("x"), lax.axis_index("y"), lax.axis_index("c")
    chips = [(1 - x, y), (x, 1 - y), (1 - x, 1 - y)]
    sends, recvs = [], []
    for a, (src, land) in enumerate(zip(srcs, lands)):
        for j, (px, py) in enumerate(chips):
            def copy(dst_block, a=a, j=j, px=px, py=py, src=src, land=land):
                return pltpu.make_async_remote_copy(
                    src_ref=src, dst_ref=land.at[dst_block], send_sem=send_sems.at[3 * a + j],
                    recv_sem=recv_sems.at[3 * a + j], device_id=(px, py, c), device_id_type=MESH_ID)
            sends.append(copy(2 * x + y))
            recvs.append(copy(2 * px + py))
    return sends, recvs


def _split_start(name, copies, srcs, land_shapes, sems_per_src):
    n = len(srcs)

    def body(*refs):
        sends, _ = copies(refs[:n], refs[n + 2:2 * n + 2], refs[n], refs[n + 1])
        for cp in sends:
            cp.start()
        token = refs[-1]
        token[...] = jnp.zeros_like(token)

    n_sems = sems_per_src * n
    outs = pl.pallas_call(
        body, name=name,
        out_shape=(pltpu.SemaphoreType.DMA((n_sems,)), pltpu.SemaphoreType.DMA((n_sems,)),
                   *[pltpu.HBM(shape, a.dtype) for a, shape in zip(srcs, land_shapes)],
                   jax.ShapeDtypeStruct((8, LANES), F32)),
        in_specs=[HBM_SPEC] * n, out_specs=(SEM_SPEC, SEM_SPEC, *[HBM_SPEC] * n, VMEM_SPEC),
        compiler_params=pltpu.CompilerParams(has_side_effects=SIDE_EFFECT),
    )(*[_hbm(a) for a in srcs])
    return outs[0], outs[1], list(srcs), list(outs[2:2 + n]), outs[-1]


def _split_wait(name, copies, started, after):
    send_sems, recv_sems, srcs, lands, _ = started
    n = len(srcs)

    def body(*refs):
        sends, recvs = copies(refs[:n], refs[n:2 * n], refs[2 * n], refs[2 * n + 1])
        for cp in sends:
            cp.wait_send()
        for cp in recvs:
            cp.wait_recv()

    return pl.pallas_call(
        body, name=name,
        out_shape=tuple(pltpu.HBM(a.shape, a.dtype) for a in lands),
        in_specs=[HBM_SPEC] * (2 * n) + [SEM_SPEC, SEM_SPEC, ANY_SPEC], out_specs=tuple([HBM_SPEC] * n),
        input_output_aliases={n + i: i for i in range(n)},
        compiler_params=pltpu.CompilerParams(has_side_effects=SIDE_EFFECT),
    )(*srcs, *lands, send_sems, recv_sems, after)


def _split_start_in_place(name, copies, bufs, sems_per_buf):
    n = len(bufs)

    def body(*refs):
        sends, _ = copies(refs[:n], refs[:n], refs[n], refs[n + 1])
        for cp in sends:
            cp.start()
        token = refs[-1]
        token[...] = jnp.zeros_like(token)

    n_sems = sems_per_buf * n
    outs = pl.pallas_call(
        body, name=name,
        out_shape=(pltpu.SemaphoreType.DMA((n_sems,)), pltpu.SemaphoreType.DMA((n_sems,)),
                   *[pltpu.HBM(a.shape, a.dtype) for a in bufs], jax.ShapeDtypeStruct((8, LANES), F32)),
        in_specs=[HBM_SPEC] * n, out_specs=(SEM_SPEC, SEM_SPEC, *[HBM_SPEC] * n, VMEM_SPEC),
        input_output_aliases={i: 2 + i for i in range(n)},
        compiler_params=pltpu.CompilerParams(has_side_effects=SIDE_EFFECT),
    )(*[_hbm(a) for a in bufs])
    return outs[0], outs[1], list(outs[2:2 + n]), outs[-1]


def _split_wait_in_place(name, copies, started, after):
    send_sems, recv_sems, bufs, _ = started
    n = len(bufs)

    def body(*refs):
        sends, recvs = copies(refs[:n], refs[:n], refs[n], refs[n + 1])
        for cp in sends:
            cp.wait_send()
        for cp in recvs:
            cp.wait_recv()

    return pl.pallas_call(
        body, name=name,
        out_shape=tuple(pltpu.HBM(a.shape, a.dtype) for a in bufs),
        in_specs=[HBM_SPEC] * n + [SEM_SPEC, SEM_SPEC, ANY_SPEC], out_specs=tuple([HBM_SPEC] * n),
        input_output_aliases={i: i for i in range(n)},
        compiler_params=pltpu.CompilerParams(has_side_effects=SIDE_EFFECT),
    )(*bufs, send_sems, recv_sems, after)


def _grad_copies(grads, parts, send_sems, recv_sems):
    x, y, c = lax.axis_index("x"), lax.axis_index("y"), lax.axis_index("c")
    chips = [(1 - x, y), (x, 1 - y), (1 - x, 1 - y)]
    my_slot = 4 * x + 2 * y + c
    sends, recvs = [], []
    for a, (grad, part) in enumerate(zip(grads, parts)):
        def copy(k, block, slot, to, a=a, grad=grad, part=part):
            return pltpu.make_async_remote_copy(
                src_ref=grad.at[block], dst_ref=part.at[slot], send_sem=send_sems.at[7 * a + k],
                recv_sem=recv_sems.at[7 * a + k], device_id=to, device_id_type=MESH_ID)
        sends.append(copy(0, 2 * x + y, my_slot, (x, y, 1 - c)))
        recvs.append(copy(0, 2 * x + y, 4 * x + 2 * y + (1 - c), (x, y, 1 - c)))
        for j, (px, py) in enumerate(chips):
            for other, pc in enumerate((c, 1 - c)):
                sends.append(copy(1 + 2 * j + other, 2 * px + py, my_slot, (px, py, pc)))
                recvs.append(copy(1 + 2 * j + other, 2 * x + y, 4 * px + 2 * py + pc, (px, py, pc)))
    return sends, recvs


def _grad_copies_same_core(grads, parts, send_sems, recv_sems):
    x, y, c = lax.axis_index("x"), lax.axis_index("y"), lax.axis_index("c")
    chips = [(1 - x, y), (x, 1 - y), (1 - x, 1 - y)]
    my_slot = 4 * x + 2 * y + c
    sends, recvs = [], []
    for a, (grad, part) in enumerate(zip(grads, parts)):
        def copy(k, block, slot, to, a=a, grad=grad, part=part):
            return pltpu.make_async_remote_copy(
                src_ref=grad.at[block], dst_ref=part.at[slot], send_sem=send_sems.at[4 * a + k],
                recv_sem=recv_sems.at[4 * a + k], device_id=to, device_id_type=MESH_ID)
        sends.append(copy(0, 2 * x + y, my_slot, (x, y, 1 - c)))
        recvs.append(copy(0, 2 * x + y, 4 * x + 2 * y + (1 - c), (x, y, 1 - c)))
        for j, (px, py) in enumerate(chips):
            sends.append(copy(1 + j, 2 * px + py, my_slot, (px, py, c)))
            recvs.append(copy(1 + j, 2 * x + y, 4 * px + 2 * py + c, (px, py, c)))
    return sends, recvs


def _grad_pass_copies(parts, same_parts, send_sems, recv_sems):
    del same_parts
    x, y, c = lax.axis_index("x"), lax.axis_index("y"), lax.axis_index("c")
    chips = [(1 - x, y), (x, 1 - y), (1 - x, 1 - y)]
    sends, recvs = [], []
    for a, part in enumerate(parts):
        for j, (px, py) in enumerate(chips):
            def copy(pc, a=a, j=j, px=px, py=py, part=part):
                slot = part.at[4 * px + 2 * py + pc]
                return pltpu.make_async_remote_copy(
                    src_ref=slot, dst_ref=slot, send_sem=send_sems.at[3 * a + j], recv_sem=recv_sems.at[3 * a + j],
                    device_id=(x, y, 1 - c), device_id_type=MESH_ID)
            sends.append(copy(c))
            recvs.append(copy(1 - c))
    return sends, recvs


def _ada_mod(c_all, ada_w, ada_b_cols):
    n_l, d, a4 = ada_w.shape
    tn = _tile(a4, 512)

    def body(c_ref, w_ref, b_ref, o_ref):
        cv = c_ref[...]
        ca = (cv * jax.nn.sigmoid(cv)).astype(BF16)
        o_ref[...] = _dot(ca, w_ref[...].astype(BF16)) + b_ref[...]

    return pl.pallas_call(
        body, name="ada_mod", grid=(n_l, a4 // tn),
        in_specs=[pl.BlockSpec((N_DEV, d), lambda l, j: (0, 0)),
                  pl.BlockSpec((None, d, tn), lambda l, j: (l, 0, j)),
                  pl.BlockSpec((None, 1, tn), lambda l, j: (l, 0, j))],
        out_specs=pl.BlockSpec((None, N_DEV, tn), lambda l, j: (l, 0, j)),
        out_shape=jax.ShapeDtypeStruct((n_l, N_DEV, a4), F32),
        compiler_params=_params("parallel", "parallel"),
    )(c_all, ada_w, ada_b_cols)


def _ada_grad_adam(c_all_t, dmod_cols, w, m, v):
    n_l, d, a4 = w.shape
    tn = _tile(a4, 512)

    def body(ct_ref, dm_ref, w_ref, m_ref, v_ref, g_ref, dl_ref, nm_ref, nv_ref):
        ct = ct_ref[...]
        ca = ct * jax.nn.sigmoid(ct)
        dm = dm_ref[...]
        g = ca[:, 0:1] * dm[0:1, :]
        for dev in range(1, N_DEV):
            g = g + ca[:, dev:dev + 1] * dm[dev:dev + 1, :]
        g_ref[...] = g
        delta, nm, nv = _adamw(w_ref[...], g, m_ref[...], v_ref[...])
        dl_ref[...] = delta
        nm_ref[...] = nm
        nv_ref[...] = nv

    wspec = pl.BlockSpec((None, d, tn), lambda l, j: (l, 0, j))
    shp = jax.ShapeDtypeStruct(w.shape, F32)
    return pl.pallas_call(
        body, name="ada_grad_adam", grid=(n_l, a4 // tn),
        in_specs=[pl.BlockSpec((d, N_DEV), lambda l, j: (0, 0)),
                  pl.BlockSpec((None, N_DEV, tn), lambda l, j: (l, 0, j)), wspec, wspec, wspec],
        out_specs=[wspec] * 4, out_shape=[shp] * 4,
        compiler_params=_params("parallel", "parallel"),
    )(c_all_t, dmod_cols, w, m, v)


def _lnmod(x, g, sc, sh):
    s, d = x.shape
    tm = _tile(s, 512)

    def body(x_ref, g_ref, sc_ref, sh_ref, h_ref):
        xv = x_ref[...]
        r = lax.rsqrt(jnp.mean(xv * xv, axis=-1, keepdims=True) + EPS)
        h_ref[...] = ((xv * r * g_ref[...]) * (1.0 + sc_ref[...]) + sh_ref[...]).astype(BF16)

    vec = pl.BlockSpec((1, d), lambda i: (0, 0))
    row = pl.BlockSpec((tm, d), lambda i: (i, 0))
    return pl.pallas_call(
        body, name="lnmod", grid=(s // tm,), in_specs=[row, vec, vec, vec], out_specs=row,
        out_shape=jax.ShapeDtypeStruct((s, d), BF16), compiler_params=_params("parallel"),
    )(x, g, sc, sh)


def _mm_in(h, w_g):
    s, d = h.shape
    n4 = w_g.shape[-1]
    tm = _tile(s, 512)

    def body(a_ref, b_ref, o_ref):
        o_ref[...] = _dot(a_ref[...], b_ref[...])

    return pl.pallas_call(
        body, name="mm_in", grid=(N_CHIP, s // tm),
        in_specs=[pl.BlockSpec((tm, d), lambda j, i: (i, 0)),
                  pl.BlockSpec((None, d, n4), lambda j, i: (j, 0, 0))],
        out_specs=pl.BlockSpec((tm, n4), lambda j, i: (i, j)),
        out_shape=jax.ShapeDtypeStruct((s, N_CHIP * n4), F32),
        compiler_params=_params("parallel", "parallel"),
    )(h, w_g)


def _pair_mean(x, low):
    lo = jnp.sum(jnp.where(low, x, 0.0), axis=-1, keepdims=True)
    hi = jnp.sum(jnp.where(low, 0.0, x), axis=-1, keepdims=True)
    return jnp.where(low, lo, hi) * (1.0 / HEAD_DIM)


def _pair_norm(x, low):
    r = lax.rsqrt(_pair_mean(x * x, low) + EPS)
    return x * r, r


def _log_not(z):
    return jnp.minimum(-z, 0.0) - jnp.log(1.0 + jnp.exp(-jnp.abs(z)))


def _attn_consts(inclusive):
    low = lax.broadcasted_iota(jnp.int32, (1, LANES), 1) < HEAD_DIM
    row = lax.broadcasted_iota(jnp.int32, (Q_BLOCK, Q_BLOCK), 0)
    col = lax.broadcasted_iota(jnp.int32, (Q_BLOCK, Q_BLOCK), 1)
    tri = (row <= col) if inclusive else (row > col)
    w2 = jnp.concatenate([tri.astype(BF16), jnp.ones((Q_BLOCK, Q_BLOCK), BF16)], axis=1)
    return low, col < row, jnp.concatenate([w2, w2], axis=0)


def _split_cat(v):
    hi = v.astype(BF16)
    return jnp.concatenate([hi, (v - hi.astype(F32)).astype(BF16)], axis=1)


def _fill_pair_blocks(dst, src_fn, low, n_kb):
    def fill(b, _):
        v = src_fn(pl.ds(pl.multiple_of(b * Q_BLOCK, Q_BLOCK), Q_BLOCK))
        dst[b, 0:Q_BLOCK, :] = jnp.where(low, v, 0.0).astype(BF16)
        dst[b, Q_BLOCK:2 * Q_BLOCK, :] = jnp.where(low, 0.0, v).astype(BF16)
        return 0

    lax.fori_loop(0, n_kb, fill, 0)


def _attn_fwd(p, qg2, kg2, d):
    s = p.shape[0]
    n_pairs = d // LANES
    qsb = _tile(s, Q_SUPER)
    n_sub, n_sb, n_kb = qsb // Q_BLOCK, s // qsb, s // Q_BLOCK
    unroll = math.gcd(KEY_UNROLL, n_sub)
    chunk = _tile(s, 512)
    inv_sqrt = 1.0 / math.sqrt(HEAD_DIM)

    def body(q_ref, k_ref, v_ref, qg_ref, kg_ref, o_ref, lt_ref, qs, k2, v2, run, acc):
        low, causal, w4 = _attn_consts(False)

        def prep(r, _):
            rows = pl.ds(pl.multiple_of(r * chunk, chunk), chunk)
            qs[rows, :] = (_pair_norm(q_ref[rows, :], low)[0] * (qg_ref[...] * inv_sqrt)).astype(BF16)
            return 0

        lax.fori_loop(0, s // chunk, prep, 0)
        _fill_pair_blocks(k2, lambda rows: _pair_norm(k_ref[rows, :], low)[0] * kg_ref[...], low, n_kb)
        _fill_pair_blocks(v2, lambda rows: v_ref[rows, :], low, n_kb)

        def step(sb, j, t0=0, diag_t=None):
            rows = pl.ds(pl.multiple_of(sb * qsb + t0 * Q_BLOCK, Q_BLOCK), (n_sub - t0) * Q_BLOCK)
            z_both = _dot_nt(qs[rows, :], k2[j])
            zls, cats = [], []
            for t in range(t0, n_sub):
                sub = slice((t - t0) * Q_BLOCK, (t - t0 + 1) * Q_BLOCK)
                for h in range(2):
                    z = z_both[sub, h * LANES:(h + 1) * LANES]
                    ln = _log_not(z)
                    if t == diag_t:
                        ln = jnp.where(causal, ln, 0.0)
                    zls.append(z + ln)
                    cats.append(_split_cat(ln))
            c2 = _dot(jnp.concatenate(cats, axis=0), w4)
            a_rows = []
            for t in range(t0, n_sub):
                sub = slice(t * Q_BLOCK, (t + 1) * Q_BLOCK)
                a_pair = []
                for h in range(2):
                    i = 2 * (t - t0) + h
                    tile = slice(i * Q_BLOCK, (i + 1) * Q_BLOCK)
                    later = run[h, sub, :]
                    log_a = zls[i] + c2[tile, :LANES] + later
                    if t == diag_t:
                        log_a = jnp.where(causal, log_a, -1e30)
                    a_pair.append(jnp.exp(log_a).astype(BF16))
                    run[h, sub, :] = later + c2[tile, LANES:]
                a_rows.append(jnp.concatenate(a_pair, axis=1))
            acc[t0 * Q_BLOCK:, :] += _dot(jnp.concatenate(a_rows, axis=0), v2[j])

        def super_block(sb, _):
            run[...] = jnp.zeros_like(run)
            acc[...] = jnp.zeros_like(acc)
            for t in reversed(range(n_sub)):
                step(sb, sb * n_sub + t, t0=t, diag_t=t)

            def below(n, _):
                for u in range(unroll):
                    step(sb, sb * n_sub - 1 - (unroll * n + u))
                return 0

            lax.fori_loop(0, sb * (n_sub // unroll), below, 0)
            rows_sb = pl.ds(pl.multiple_of(sb * qsb, qsb), qsb)
            o_ref[rows_sb, :] = acc[...].astype(BF16)
            lt_ref[rows_sb, :] = jnp.where(low, run[0], run[1])
            return 0

        lax.fori_loop(0, n_sb, super_block, 0)

    def seg(k):
        return pl.BlockSpec((s, LANES), lambda h, k=k: (0, k * n_pairs + h))

    vec = pl.BlockSpec((1, LANES), lambda h: (0, 0))
    out = pl.BlockSpec((s, LANES), lambda h: (0, h))
    return pl.pallas_call(
        body, name="attn_fwd", grid=(n_pairs,),
        in_specs=[seg(0), seg(1), seg(2), vec, vec], out_specs=[out, out],
        out_shape=[jax.ShapeDtypeStruct((s, d), BF16), jax.ShapeDtypeStruct((s, d), F32)],
        scratch_shapes=[pltpu.VMEM((s, LANES), BF16)] + [pltpu.VMEM((n_kb, 2 * Q_BLOCK, LANES), BF16)] * 2
        + [pltpu.VMEM((2, qsb, LANES), F32), pltpu.VMEM((qsb, LANES), F32)],
        compiler_params=_params("parallel"),
    )(p, p, p, qg2, kg2)


def _conv_rows(s):
    return _tile(s, 512)


def _conv_fwd(p, conv_w, d):
    s = p.shape[0]
    nb = d // LANES
    rows_n = _conv_rows(s)

    def body(cb_ref, cc_ref, cx_ref, w_ref, y_ref, us):
        us[pl.ds(0, 8), :] = jnp.zeros((8, LANES), F32)

        def fill(r, _):
            rows = pl.ds(pl.multiple_of(r * rows_n, rows_n), rows_n)
            us[pl.ds(pl.multiple_of(r * rows_n + 8, 8), rows_n), :] = cc_ref[rows, :] * cx_ref[rows, :]
            return 0

        lax.fori_loop(0, s // rows_n, fill, 0)
        w = w_ref[...]

        def out(r, _):
            rows = pl.ds(pl.multiple_of(r * rows_n, rows_n), rows_n)
            ext = us[pl.ds(pl.multiple_of(r * rows_n, 8), rows_n + 8), :]
            cv = (w[0:1, :] * pltpu.roll(ext, 2, 0)[8:, :] + w[1:2, :] * pltpu.roll(ext, 1, 0)[8:, :]
                  + w[2:3, :] * ext[8:, :])
            y_ref[rows, :] = (cb_ref[rows, :] * cv).astype(BF16)
            return 0

        lax.fori_loop(0, s // rows_n, out, 0)

    def seg(k):
        return pl.BlockSpec((s, LANES), lambda b, k=k: (0, k * nb + b))

    return pl.pallas_call(
        body, name="conv_fwd", grid=(nb,),
        in_specs=[seg(3), seg(4), seg(5), pl.BlockSpec((3, LANES), lambda b: (0, b))],
        out_specs=pl.BlockSpec((s, LANES), lambda b: (0, b)),
        out_shape=jax.ShapeDtypeStruct((s, d), BF16),
        scratch_shapes=[pltpu.VMEM((s + 8, LANES), F32)],
        compiler_params=_params("parallel"),
    )(p, p, p, conv_w)


def _branch(ya, yb, p, wa, wb, d):
    s = ya.shape[0]
    tm = _tile(s, 512)

    def body(ya_ref, yb_ref, ga_ref, gb_ref, wa_ref, wb_ref, m_ref, a_ref, b_ref):
        pa = _dot(ya_ref[...], wa_ref[...])
        pb = _dot(yb_ref[...], wb_ref[...])
        m_ref[...] = (jax.nn.sigmoid(ga_ref[...]) * pa + jax.nn.sigmoid(gb_ref[...]) * pb).astype(BF16)
        a_ref[...] = pa.astype(BF16)
        b_ref[...] = pb.astype(BF16)

    row = pl.BlockSpec((tm, d), lambda i: (i, 0))
    wsp = pl.BlockSpec((d, d), lambda i: (0, 0))
    shp = jax.ShapeDtypeStruct((s, d), BF16)
    return pl.pallas_call(
        body, name="branch", grid=(s // tm,),
        in_specs=[row, row, pl.BlockSpec((tm, d), lambda i: (i, 6)), pl.BlockSpec((tm, d), lambda i: (i, 7)), wsp, wsp],
        out_specs=[row, row, row], out_shape=[shp, shp, shp], compiler_params=_params("parallel"),
    )(ya, yb, p, p, wa, wb)


def _out_proj(merged, wout, x0, g1):
    s, d = x0.shape
    tm = _tile(s, 512)

    def body(m_ref, w_ref, x_ref, g_ref, x1_ref, mo_ref):
        mo = _dot(m_ref[...], w_ref[...])
        mo_ref[...] = mo
        x1_ref[...] = x_ref[...] + g_ref[...] * mo

    row = pl.BlockSpec((tm, d), lambda i: (i, 0))
    shp = jax.ShapeDtypeStruct((s, d), F32)
    return pl.pallas_call(
        body, name="out_proj", grid=(s // tm,),
        in_specs=[row, pl.BlockSpec((d, d), lambda i: (0, 0)), row, pl.BlockSpec((1, d), lambda i: (0, 0))],
        out_specs=[row, row], out_shape=[shp, shp], compiler_params=_params("parallel"),
    )(merged, wout, x0, g1)


def _ffn_up(h, wg_g, wu_g):
    s, d = h.shape
    f4 = wg_g.shape[-1]
    tm = _tile(s, 512)

    def body(h_ref, wg_ref, wu_ref, gate_ref, up_ref, act_ref):
        hv = h_ref[...]
        gt = _dot(hv, wg_ref[...])
        up = _dot(hv, wu_ref[...])
        gate_ref[...] = gt.astype(BF16)
        up_ref[...] = up.astype(BF16)
        act_ref[...] = (gt * jax.nn.sigmoid(gt) * up).astype(BF16)

    wsp = pl.BlockSpec((None, d, f4), lambda j, i: (j, 0, 0))
    osp = pl.BlockSpec((None, tm, f4), lambda j, i: (j, i, 0))
    shp = jax.ShapeDtypeStruct((N_CHIP, s, f4), BF16)
    return pl.pallas_call(
        body, name="ffn_up", grid=(N_CHIP, s // tm),
        in_specs=[pl.BlockSpec((tm, d), lambda j, i: (i, 0)), wsp, wsp],
        out_specs=[osp, osp, osp], out_shape=[shp, shp, shp], compiler_params=_params("parallel", "parallel"),
    )(h, wg_g, wu_g)


def _ffn_down(act, wd_g, x1, g2):
    s, d = x1.shape
    f4 = act.shape[-1]
    tm = _tile(s, 512)

    def body(a_ref, w_ref, x_ref, g_ref, x2_ref, f_ref, acc):
        j = pl.program_id(1)

        @pl.when(j == 0)
        def _():
            acc[...] = jnp.zeros_like(acc)

        acc[...] += _dot(a_ref[...], w_ref[...])

        @pl.when(j == N_CHIP - 1)
        def _():
            f = acc[...]
            f_ref[...] = f
            x2_ref[...] = x_ref[...] + g_ref[...] * f

    row = pl.BlockSpec((tm, d), lambda i, j: (i, 0))
    shp = jax.ShapeDtypeStruct((s, d), F32)
    return pl.pallas_call(
        body, name="ffn_down", grid=(s // tm, N_CHIP),
        in_specs=[pl.BlockSpec((None, tm, f4), lambda i, j: (j, i, 0)),
                  pl.BlockSpec((None, f4, d), lambda i, j: (j, 0, 0)),
                  row, pl.BlockSpec((1, d), lambda i, j: (0, 0))],
        out_specs=[row, row], out_shape=[shp, shp],
        scratch_shapes=[pltpu.VMEM((tm, d), F32)], compiler_params=_params("parallel", "arbitrary"),
    )(act, wd_g, x1, g2)


def _loss_head(y, target):
    s, d = y.shape
    tm = _tile(s, 512)
    n_steps = s // tm

    def body(y_ref, t_ref, dy_ref, l_ref, acc):
        i = pl.program_id(0)

        @pl.when(i == 0)
        def _():
            acc[...] = jnp.zeros_like(acc)

        err = y_ref[...] - t_ref[...]
        dy_ref[...] = err / d
        acc[...] += jnp.sum(err * err, axis=0, keepdims=True)

        @pl.when(i == n_steps - 1)
        def _():
            l_ref[...] = jnp.broadcast_to(jnp.sum(acc[...], axis=1, keepdims=True), (8, LANES))

    row = pl.BlockSpec((tm, d), lambda i: (i, 0))
    return pl.pallas_call(
        body, name="loss_head", grid=(n_steps,), in_specs=[row, row],
        out_specs=[row, pl.BlockSpec((8, LANES), lambda i: (0, 0))],
        out_shape=[jax.ShapeDtypeStruct((s, d), F32), jax.ShapeDtypeStruct((8, LANES), F32)],
        scratch_shapes=[pltpu.VMEM((1, d), F32)], compiler_params=_params("arbitrary"),
    )(y, target)


def _mm_tn(a, b, a_spec, b_spec, out_rc, name):
    r, c = out_rc
    s = a.shape[-2]
    tk = _tile(s, 512)
    nk = s // tk

    def body(a_ref, b_ref, o_ref, acc):
        k = pl.program_id(1)

        @pl.when(k == 0)
        def _():
            acc[...] = jnp.zeros_like(acc)

        acc[...] += _dot_tn(a_ref[...], b_ref[...])

        @pl.when(k == nk - 1)
        def _():
            o_ref[...] = acc[...].astype(BF16)

    return pl.pallas_call(
        body, name=name, grid=(N_CHIP, nk),
        in_specs=[pl.BlockSpec(*a_spec(tk)), pl.BlockSpec(*b_spec(tk))],
        out_specs=pl.BlockSpec((None, r, c), lambda j, k: (j, 0, 0)),
        out_shape=jax.ShapeDtypeStruct((N_CHIP, r, c), BF16),
        scratch_shapes=[pltpu.VMEM((r, c), F32)], compiler_params=_params("parallel", "arbitrary"),
    )(a, b)


def _ffn_bwd1(dx2, f, g2, wd_g, gate, up):
    s, d = dx2.shape
    f4 = gate.shape[-1]
    tm = _tile(s, 512)

    def body(dx_ref, f_ref, g_ref, w_ref, gate_ref, up_ref, dgate_ref, dup_ref, df_ref, dg_ref):
        i, j = pl.program_id(0), pl.program_id(1)

        @pl.when((i == 0) & (j == 0))
        def _():
            dg_ref[...] = jnp.zeros_like(dg_ref)

        dxv = dx_ref[...]
        df = (g_ref[...] * dxv).astype(BF16)

        @pl.when(j == 0)
        def _():
            df_ref[...] = df
            dg_ref[0:1, :] += jnp.sum(dxv * f_ref[...], axis=0, keepdims=True)

        da = _dot_nt(df, w_ref[...])
        gt = gate_ref[...].astype(F32)
        sg = jax.nn.sigmoid(gt)
        dup_ref[...] = (da * gt * sg).astype(BF16)
        dgate_ref[...] = (da * up_ref[...].astype(F32) * (sg * (1.0 + gt * (1.0 - sg)))).astype(BF16)

    row = pl.BlockSpec((tm, d), lambda i, j: (i, 0))
    hsp = pl.BlockSpec((None, tm, f4), lambda i, j: (j, i, 0))
    hshp = jax.ShapeDtypeStruct((N_CHIP, s, f4), BF16)
    return pl.pallas_call(
        body, name="ffn_bwd1", grid=(s // tm, N_CHIP),
        in_specs=[row, row, pl.BlockSpec((1, d), lambda i, j: (0, 0)),
                  pl.BlockSpec((None, f4, d), lambda i, j: (j, 0, 0)), hsp, hsp],
        out_specs=[hsp, hsp, row, pl.BlockSpec((8, d), lambda i, j: (0, 0))],
        out_shape=[hshp, hshp, jax.ShapeDtypeStruct((s, d), BF16), jax.ShapeDtypeStruct((8, d), F32)],
        compiler_params=_params("arbitrary", "arbitrary"),
    )(dx2, f, g2, wd_g, gate, up)


def _ffn_bwd2(dgate, dup, wg_g, wu_g):
    _, s, f4 = dgate.shape
    d = wg_g.shape[-2]
    tm = _tile(s, 512)

    def body(dg_ref, du_ref, wg_ref, wu_ref, o_ref, acc):
        j = pl.program_id(1)

        @pl.when(j == 0)
        def _():
            acc[...] = jnp.zeros_like(acc)

        acc[...] += _dot_nt(dg_ref[...], wg_ref[...]) + _dot_nt(du_ref[...], wu_ref[...])

        @pl.when(j == N_CHIP - 1)
        def _():
            o_ref[...] = acc[...]

    hsp = pl.BlockSpec((None, tm, f4), lambda i, j: (j, i, 0))
    wsp = pl.BlockSpec((None, d, f4), lambda i, j: (j, 0, 0))
    return pl.pallas_call(
        body, name="ffn_bwd2", grid=(s // tm, N_CHIP), in_specs=[hsp, hsp, wsp, wsp],
        out_specs=pl.BlockSpec((tm, d), lambda i, j: (i, 0)), out_shape=jax.ShapeDtypeStruct((s, d), F32),
        scratch_shapes=[pltpu.VMEM((tm, d), F32)], compiler_params=_params("parallel", "arbitrary"),
    )(dgate, dup, wg_g, wu_g)


def _lnmod_bwd(x, g, sc, dh, dres):
    s, d = x.shape
    tm = _tile(s, 512)

    def body(x_ref, g_ref, sc_ref, dh_ref, dr_ref, dx_ref, sums_ref):
        @pl.when(pl.program_id(0) == 0)
        def _():
            sums_ref[...] = jnp.zeros_like(sums_ref)

        xv, dhv, gv = x_ref[...], dh_ref[...], g_ref[...]
        r = lax.rsqrt(jnp.mean(xv * xv, axis=-1, keepdims=True) + EPS)
        n = xv * r
        one_sc = 1.0 + sc_ref[...]
        dt = dhv * one_sc
        sums_ref[0:1, :] += jnp.sum(dhv, axis=0, keepdims=True)
        sums_ref[1:2, :] += jnp.sum(dhv * (n * gv), axis=0, keepdims=True)
        sums_ref[2:3, :] += jnp.sum(dt * n, axis=0, keepdims=True)
        dn = dt * gv
        dx_ref[...] = dr_ref[...] + r * (dn - n * jnp.mean(dn * n, axis=-1, keepdims=True))

    vec = pl.BlockSpec((1, d), lambda i: (0, 0))
    row = pl.BlockSpec((tm, d), lambda i: (i, 0))
    return pl.pallas_call(
        body, name="lnmod_bwd", grid=(s // tm,), in_specs=[row, vec, vec, row, row],
        out_specs=[row, pl.BlockSpec((8, d), lambda i: (0, 0))],
        out_shape=[jax.ShapeDtypeStruct((s, d), F32), jax.ShapeDtypeStruct((8, d), F32)],
        compiler_params=_params("arbitrary"),
    )(x, g, sc, dh, dres)


def _out_bwd(dx1, mo, g1, wout, pa, pb, p, wa, wb, d):
    s = dx1.shape[0]
    tm = _tile(s, 256)

    def body(dx_ref, mo_ref, g_ref, wo_ref, pa_ref, pb_ref, ga_ref, gb_ref, wa_ref, wb_ref,
             dmo_ref, da_ref, db_ref, dya_ref, dyb_ref, dp_ref, dg_ref):
        @pl.when(pl.program_id(0) == 0)
        def _():
            dg_ref[...] = jnp.zeros_like(dg_ref)

        dxv = dx_ref[...]
        dg_ref[0:1, :] += jnp.sum(dxv * mo_ref[...], axis=0, keepdims=True)
        dmo = (g_ref[...] * dxv).astype(BF16)
        dmo_ref[...] = dmo
        dm = _dot_nt(dmo, wo_ref[...])
        sa, sb = jax.nn.sigmoid(ga_ref[...]), jax.nn.sigmoid(gb_ref[...])
        da = (dm * sa).astype(BF16)
        db = (dm * sb).astype(BF16)
        da_ref[...] = da
        db_ref[...] = db
        dp_ref[:, :d] = (dm * pa_ref[...].astype(F32) * (sa * (1.0 - sa))).astype(BF16)
        dp_ref[:, d:] = (dm * pb_ref[...].astype(F32) * (sb * (1.0 - sb))).astype(BF16)
        dya_ref[...] = _dot_nt(da, wa_ref[...]).astype(BF16)
        dyb_ref[...] = _dot_nt(db, wb_ref[...]).astype(BF16)

    row = pl.BlockSpec((tm, d), lambda i: (i, 0))
    wsp = pl.BlockSpec((d, d), lambda i: (0, 0))
    shp = jax.ShapeDtypeStruct((s, d), BF16)
    return pl.pallas_call(
        body, name="out_bwd", grid=(s // tm,),
        in_specs=[row, row, pl.BlockSpec((1, d), lambda i: (0, 0)), wsp, row, row,
                  pl.BlockSpec((tm, d), lambda i: (i, 6)), pl.BlockSpec((tm, d), lambda i: (i, 7)), wsp, wsp],
        out_specs=[row] * 5 + [pl.BlockSpec((tm, 2 * d), lambda i: (i, 3)), pl.BlockSpec((8, d), lambda i: (0, 0))],
        out_shape=[shp] * 5 + [jax.ShapeDtypeStruct((s, 8 * d), BF16), jax.ShapeDtypeStruct((8, d), F32)],
        compiler_params=_params("arbitrary"),
    )(dx1, mo, g1, wout, pa, pb, p, p, wa, wb)


def _conv_bwd(p, conv_w, dyb, dp, d):
    s = p.shape[0]
    nb = d // LANES
    rows_n = _conv_rows(s)

    def compute(cb_ref, cc_ref, cx_ref, w_ref, dy_ref, dcb_ref, dcc_ref, dcx_ref, dw_ref, us, ds):
        us[pl.ds(0, 8), :] = jnp.zeros((8, LANES), F32)
        ds[pl.ds(s, 8), :] = jnp.zeros((8, LANES), F32)

        def fill(r, _):
            rows = pl.ds(pl.multiple_of(r * rows_n, rows_n), rows_n)
            us[pl.ds(pl.multiple_of(r * rows_n + 8, 8), rows_n), :] = cc_ref[rows, :] * cx_ref[rows, :]
            ds[rows, :] = dy_ref[rows, :].astype(F32) * cb_ref[rows, :]
            return 0

        lax.fori_loop(0, s // rows_n, fill, 0)
        w = w_ref[...]

        def out(r, carry):
            dw0, dw1, dw2 = carry
            rows = pl.ds(pl.multiple_of(r * rows_n, rows_n), rows_n)
            ext = us[pl.ds(pl.multiple_of(r * rows_n, 8), rows_n + 8), :]
            u0, u1, u2 = ext[8:, :], pltpu.roll(ext, 1, 0)[8:, :], pltpu.roll(ext, 2, 0)[8:, :]
            cv = w[0:1, :] * u2 + w[1:2, :] * u1 + w[2:3, :] * u0
            dcb_ref[rows, :] = (dy_ref[rows, :].astype(F32) * cv).astype(BF16)
            nxt = ds[pl.ds(pl.multiple_of(r * rows_n, 8), rows_n + 8), :]
            e0 = nxt[:rows_n, :]
            e1 = pltpu.roll(nxt, rows_n + 7, 0)[:rows_n, :]
            e2 = pltpu.roll(nxt, rows_n + 6, 0)[:rows_n, :]
            du = w[2:3, :] * e0 + w[1:2, :] * e1 + w[0:1, :] * e2
            dcc_ref[rows, :] = (du * cx_ref[rows, :]).astype(BF16)
            dcx_ref[rows, :] = (du * cc_ref[rows, :]).astype(BF16)
            return (dw0 + jnp.sum(e0 * u2, axis=0, keepdims=True), dw1 + jnp.sum(e0 * u1, axis=0, keepdims=True),
                    dw2 + jnp.sum(e0 * u0, axis=0, keepdims=True))

        zero = jnp.zeros((1, LANES), F32)
        dw0, dw1, dw2 = lax.fori_loop(0, s // rows_n, out, (zero, zero, zero))
        dw_ref[...] = jnp.zeros_like(dw_ref)
        dw_ref[0:1, :] = dw0
        dw_ref[1:2, :] = dw1
        dw_ref[2:3, :] = dw2

    def body(cb_ref, cc_ref, cx_ref, w_ref, dy_ref, dp_in, out_ref, dw_ref, us, ds, dcc_s, dcx_s):
        del dp_in
        g = pl.program_id(1)

        @pl.when(g == 0)
        def _():
            compute(cb_ref, cc_ref, cx_ref, w_ref, dy_ref, out_ref, dcc_s, dcx_s, dw_ref, us, ds)

        @pl.when(g == 1)
        def _():
            out_ref[...] = dcc_s[...]

        @pl.when(g == 2)
        def _():
            out_ref[...] = dcx_s[...]

    def seg(k):
        return pl.BlockSpec((s, LANES), lambda b, g, k=k: (0, k * nb + b))

    return pl.pallas_call(
        body, name="conv_bwd", grid=(nb, 3),
        in_specs=[seg(3), seg(4), seg(5), pl.BlockSpec((3, LANES), lambda b, g: (0, b)),
                  pl.BlockSpec((s, LANES), lambda b, g: (0, b)), ANY_SPEC],
        out_specs=[pl.BlockSpec((s, LANES), lambda b, g: (0, (3 + g) * nb + b)),
                   pl.BlockSpec((8, LANES), lambda b, g: (0, b))],
        out_shape=[jax.ShapeDtypeStruct(dp.shape, BF16), jax.ShapeDtypeStruct((8, d), F32)],
        input_output_aliases={5: 0},
        scratch_shapes=[pltpu.VMEM((s + 8, LANES), F32), pltpu.VMEM((s + 8, LANES), F32),
                        pltpu.VMEM((s, LANES), BF16), pltpu.VMEM((s, LANES), BF16)],
        compiler_params=_params("parallel", "arbitrary"),
    )(p, p, p, conv_w, dyb, dp)


def _attn_bwd(p, qg2, kg2, dy, lt, dp, d):
    s = p.shape[0]
    n_pairs = d // LANES
    qsb = _tile(s, Q_SUPER_BWD)
    n_sub, n_sb, n_kb = qsb // Q_BLOCK, s // qsb, s // Q_BLOCK
    unroll = math.gcd(KEY_UNROLL, n_sub)
    chunk = _tile(s, 512)
    inv_sqrt = 1.0 / math.sqrt(HEAD_DIM)

    def compute(q_ref, k_ref, v_ref, qg_ref, kg_ref, dy_ref, lt_ref, dq_ref, dk_ref, dv_ref, dgain_ref,
                qs, k2, v2, dkt, dvt, qt, dyt, rem, gbef, dqa):
        low, causal, w4 = _attn_consts(True)

        def prep(r, _):
            rows = pl.ds(pl.multiple_of(r * chunk, chunk), chunk)
            qs[rows, :] = (_pair_norm(q_ref[rows, :], low)[0] * (qg_ref[...] * inv_sqrt)).astype(BF16)
            return 0

        lax.fori_loop(0, s // chunk, prep, 0)
        _fill_pair_blocks(k2, lambda rows: _pair_norm(k_ref[rows, :], low)[0] * kg_ref[...], low, n_kb)
        _fill_pair_blocks(v2, lambda rows: v_ref[rows, :], low, n_kb)

        def clear(b, _):
            dkt[b] = jnp.zeros((LANES, Q_BLOCK), F32)
            dvt[b] = jnp.zeros((LANES, Q_BLOCK), F32)
            return 0

        lax.fori_loop(0, n_kb, clear, 0)

        def step(sb, j, t0=0, diag_t=None):
            rows = pl.ds(pl.multiple_of(sb * qsb + t0 * Q_BLOCK, Q_BLOCK), (n_sub - t0) * Q_BLOCK)
            kj2, vj2 = k2[j], v2[j]
            z_both = _dot_nt(qs[rows, :], kj2)
            da_both = _dot_nt(dy_ref[rows, :], vj2)
            zls, cats = [], []
            for t in range(t0, n_sub):
                sub = slice((t - t0) * Q_BLOCK, (t - t0 + 1) * Q_BLOCK)
                for h in range(2):
                    z = z_both[sub, h * LANES:(h + 1) * LANES]
                    ln = _log_not(z)
                    if t == diag_t:
                        ln = jnp.where(causal, ln, 0.0)
                    zls.append(z + ln)
                    cats.append(_split_cat(ln))
            c2 = _dot(jnp.concatenate(cats, axis=0), w4)
            a_rows, gs, cats = [], [], []
            for t in range(t0, n_sub):
                sub = slice(t * Q_BLOCK, (t + 1) * Q_BLOCK)
                a_pair = []
                for h in range(2):
                    i = 2 * (t - t0) + h
                    tile = slice(i * Q_BLOCK, (i + 1) * Q_BLOCK)
                    left = rem[h, sub, :]
                    log_a = zls[i] + (left - c2[tile, :LANES])
                    if t == diag_t:
                        log_a = jnp.where(causal, log_a, -1e30)
                    a = jnp.exp(log_a)
                    rem[h, sub, :] = left - c2[tile, LANES:]
                    g = a * da_both[(t - t0) * Q_BLOCK:(t - t0 + 1) * Q_BLOCK, h * LANES:(h + 1) * LANES]
                    a_pair.append(a.astype(BF16))
                    gs.append(g)
                    cats.append(_split_cat(g))
                a_rows.append(jnp.concatenate(a_pair, axis=1))
            c2g = _dot(jnp.concatenate(cats, axis=0), w4)
            dz_rows = []
            for t in range(t0, n_sub):
                sub = slice(t * Q_BLOCK, (t + 1) * Q_BLOCK)
                dz_pair = []
                for h in range(2):
                    i = 2 * (t - t0) + h
                    tile = slice(i * Q_BLOCK, (i + 1) * Q_BLOCK)
                    before = gbef[h, sub, :]
                    dz = gs[i] - jnp.exp(zls[i]) * (before + c2g[tile, :LANES])
                    if t == diag_t:
                        dz = jnp.where(causal, dz, 0.0)
                    gbef[h, sub, :] = before + c2g[tile, LANES:]
                    dz_pair.append(dz.astype(BF16))
                dz_rows.append(jnp.concatenate(dz_pair, axis=1))
            a_both = jnp.concatenate(a_rows, axis=0)
            dz_both = jnp.concatenate(dz_rows, axis=0)
            used = slice(t0 * Q_BLOCK, qsb)
            dvt[j] += _dot(dyt[0, :, used], a_both[:, :LANES]) + _dot(dyt[1, :, used], a_both[:, LANES:])
            dkt[j] += _dot(qt[0, :, used], dz_both[:, :LANES]) + _dot(qt[1, :, used], dz_both[:, LANES:])
            dqa[used, :] += _dot(dz_both, kj2)

        def super_block(sb, dqg):
            rows_sb = pl.ds(pl.multiple_of(sb * qsb, qsb), qsb)
            total = lt_ref[rows_sb, :]
            other = pltpu.roll(total, HEAD_DIM, 1)
            rem[0] = jnp.where(low, total, other)
            rem[1] = jnp.where(low, other, total)
            gbef[...] = jnp.zeros_like(gbef)
            dqa[...] = jnp.zeros_like(dqa)
            qv = qs[rows_sb, :].astype(F32)
            dyv = dy_ref[rows_sb, :].astype(F32)
            qt[0] = jnp.where(low, qv, 0.0).T.astype(BF16)
            qt[1] = jnp.where(low, 0.0, qv).T.astype(BF16)
            dyt[0] = jnp.where(low, dyv, 0.0).T.astype(BF16)
            dyt[1] = jnp.where(low, 0.0, dyv).T.astype(BF16)

            def below(n, _):
                for u in range(unroll):
                    step(sb, unroll * n + u)
                return 0

            lax.fori_loop(0, sb * (n_sub // unroll), below, 0)
            for t in range(n_sub):
                step(sb, sb * n_sub + t, t0=t, diag_t=t)
            qhat, r = _pair_norm(q_ref[rows_sb, :], low)
            dqn = dqa[...]
            dqhat = dqn * (qg_ref[...] * inv_sqrt)
            dq_ref[rows_sb, :] = (r * (dqhat - qhat * _pair_mean(dqhat * qhat, low))).astype(BF16)
            return dqg + jnp.sum(dqn * qhat, axis=0, keepdims=True) * inv_sqrt

        dqg = lax.fori_loop(0, n_sb, super_block, jnp.zeros((1, LANES), F32))

        def finish(b, dkg):
            rows = pl.ds(pl.multiple_of(b * Q_BLOCK, Q_BLOCK), Q_BLOCK)
            khat, rk = _pair_norm(k_ref[rows, :], low)
            dkn = dkt[b].T
            dkhat = dkn * kg_ref[...]
            dk_ref[rows, :] = (rk * (dkhat - khat * _pair_mean(dkhat * khat, low))).astype(BF16)
            dv_ref[rows, :] = dvt[b].T.astype(BF16)
            return dkg + jnp.sum(dkn * khat, axis=0, keepdims=True)

        dkg = lax.fori_loop(0, n_kb, finish, jnp.zeros((1, LANES), F32))
        dgain_ref[...] = jnp.zeros_like(dgain_ref)
        dgain_ref[0:1, :] = dqg
        dgain_ref[1:2, :] = dkg

    def body(q_ref, k_ref, v_ref, qg_ref, kg_ref, dy_ref, lt_ref, dp_in, out_ref, dgain_ref, dk_s, dv_s, *scratch):
        del dp_in
        g = pl.program_id(1)

        @pl.when(g == 0)
        def _():
            compute(q_ref, k_ref, v_ref, qg_ref, kg_ref, dy_ref, lt_ref, out_ref, dk_s, dv_s, dgain_ref, *scratch)

        @pl.when(g == 1)
        def _():
            out_ref[...] = dk_s[...]

        @pl.when(g == 2)
        def _():
            out_ref[...] = dv_s[...]

    def seg(k):
        return pl.BlockSpec((s, LANES), lambda h, g, k=k: (0, k * n_pairs + h))

    vec = pl.BlockSpec((1, LANES), lambda h, g: (0, 0))
    col = pl.BlockSpec((s, LANES), lambda h, g: (0, h))
    return pl.pallas_call(
        body, name="attn_bwd", grid=(n_pairs, 3),
        in_specs=[seg(0), seg(1), seg(2), vec, vec, col, col, ANY_SPEC],
        out_specs=[pl.BlockSpec((s, LANES), lambda h, g: (0, g * n_pairs + h)),
                   pl.BlockSpec((None, 8, LANES), lambda h, g: (h, 0, 0))],
        out_shape=[jax.ShapeDtypeStruct(dp.shape, BF16), jax.ShapeDtypeStruct((n_pairs, 8, LANES), F32)],
        input_output_aliases={7: 0},
        scratch_shapes=[pltpu.VMEM((s, LANES), BF16)] * 3 + [pltpu.VMEM((n_kb, 2 * Q_BLOCK, LANES), BF16)] * 2
        + [pltpu.VMEM((n_kb, LANES, Q_BLOCK), F32)] * 2
        + [pltpu.VMEM((2, LANES, qsb), BF16)] * 2
        + [pltpu.VMEM((2, qsb, LANES), F32)] * 2 + [pltpu.VMEM((qsb, LANES), F32)],
        compiler_params=_params("parallel", "arbitrary"),
    )(p, p, p, qg2, kg2, dy, lt, dp)


def _mm_in_bwd(dp, w_g):
    s = dp.shape[0]
    d, n4 = w_g.shape[-2:]
    tm = _tile(s, 512)

    def body(a_ref, w_ref, o_ref, acc):
        j = pl.program_id(1)

        @pl.when(j == 0)
        def _():
            acc[...] = jnp.zeros_like(acc)

        acc[...] += _dot_nt(a_ref[...], w_ref[...])

        @pl.when(j == N_CHIP - 1)
        def _():
            o_ref[...] = acc[...]

    return pl.pallas_call(
        body, name="mm_in_bwd", grid=(s // tm, N_CHIP),
        in_specs=[pl.BlockSpec((tm, n4), lambda i, j: (i, j)),
                  pl.BlockSpec((None, d, n4), lambda i, j: (j, 0, 0))],
        out_specs=pl.BlockSpec((tm, d), lambda i, j: (i, 0)), out_shape=jax.ShapeDtypeStruct((s, d), F32),
        scratch_shapes=[pltpu.VMEM((tm, d), F32)], compiler_params=_params("parallel", "arbitrary"),
    )(dp, w_g)


def _sum_adam(parts, w, m, v, name):
    n_l, r, c = w.shape
    tr = next((t for t in (256, 176, 128, 64, 32, 16) if r % t == 0 and t * c <= 256 * 1024), r)
    n_blk = r // tr

    def body(*refs):
        p_refs = refs[:n_l]
        w_ref, m_ref, v_ref, g_ref, dl_ref, nm_ref, nv_ref = refs[n_l:]
        for l in range(n_l):
            @pl.when(pl.program_id(0) == l)
            def _(p_ref=p_refs[l]):
                g = p_ref[0].astype(F32)
                for dev in range(1, N_DEV):
                    g = g + p_ref[dev].astype(F32)
                g_ref[...] = g
                delta, nm, nv = _adamw(w_ref[...], g, m_ref[...], v_ref[...])
                dl_ref[...] = delta
                nm_ref[...] = nm
                nv_ref[...] = nv

    def part_spec(l):
        return pl.BlockSpec((N_DEV, tr, c), lambda ll, i, l=l: (0, jnp.where(ll == l, i, jnp.where(ll < l, 0, n_blk - 1)), 0))

    wsp = pl.BlockSpec((None, tr, c), lambda l, i: (l, i, 0))
    shp = jax.ShapeDtypeStruct(w.shape, F32)
    return pl.pallas_call(
        body, name=name, grid=(n_l, n_blk),
        in_specs=[part_spec(l) for l in range(n_l)] + [wsp, wsp, wsp],
        out_specs=[wsp] * 4, out_shape=[shp] * 4, compiler_params=_params("arbitrary", "arbitrary"),
    )(*parts, w, m, v)


def _small_adam(parts, w, m, v):
    def body(p_ref, w_ref, m_ref, v_ref, g_ref, dl_ref, nm_ref, nv_ref):
        g = p_ref[0]
        for dev in range(1, N_DEV):
            g = g + p_ref[dev]
        g_ref[...] = g
        delta, nm, nv = _adamw(w_ref[...], g, m_ref[...], v_ref[...])
        dl_ref[...] = delta
        nm_ref[...] = nm
        nv_ref[...] = nv

    shp = jax.ShapeDtypeStruct(w.shape, F32)
    return pl.pallas_call(body, name="small_adam", in_specs=[VMEM_SPEC] * 4, out_specs=[VMEM_SPEC] * 4,
                          out_shape=[shp] * 4,
                          compiler_params=pltpu.CompilerParams(vmem_limit_bytes=VMEM_LIMIT_BYTES))(parts, w, m, v)


def _pack(vecs, mult=8 * LANES):
    flat = jnp.concatenate([a.reshape(-1).astype(F32) for a in vecs])
    pad = (-flat.shape[0]) % mult
    if pad:
        flat = jnp.concatenate([flat, jnp.zeros((pad,), F32)])
    return flat.reshape(8, -1)


def _unpack(flat, shapes):
    flat = flat.reshape(-1)
    out, off = [], 0
    for shp in shapes:
        n = math.prod(shp)
        out.append(flat[off:off + n].reshape(shp))
        off += n
    return out


BIG = ("win", "wa", "wb", "wo", "wg", "wu", "wd")
GRAD_GROUPS = (("wd", "wg", "wu"), ("wo", "wa", "wb"), ("win",))


def _local_step(x, target, mods, ln1_g, ln2_g, qg, kg, conv_w, weights, send_grads):
    s, d = x.shape
    n_l = mods.shape[0]
    saved = []
    h_in = x
    for l in range(n_l):
        sh1, sc1, g1, sh2, sc2, g2 = [mods[l, k * d:(k + 1) * d].reshape(1, d) for k in range(6)]
        qg2, kg2 = jnp.tile(qg[l:l + 1], (1, 2)), jnp.tile(kg[l:l + 1], (1, 2))
        h1 = _lnmod(h_in, ln1_g[l:l + 1], sc1, sh1)
        (win,), h1 = weights(l, ("win",), h1)
        p = _mm_in(h1, win)
        ya, lt = _attn_fwd(p, qg2, kg2, d)
        yb = _conv_fwd(p, conv_w[l], d)
        (wa, wb, wo, wg, wu, wd), yb = weights(l, ("wa", "wb", "wo", "wg", "wu", "wd"), yb)
        wa, wb, wo = wa.reshape(d, d), wb.reshape(d, d), wo.reshape(d, d)
        merged, pa, pb = _branch(ya, yb, p, wa, wb, d)
        x1, mo = _out_proj(merged, wo, h_in, g1)
        h2 = _lnmod(x1, ln2_g[l:l + 1], sc2, sh2)
        gate, up, act = _ffn_up(h2, wg, wu)
        x2, f = _ffn_down(act, wd, x1, g2)
        saved.append(dict(x0=h_in, h1=h1, p=p, ya=ya, lt=lt, yb=yb, merged=merged, pa=pa, pb=pb, x1=x1, mo=mo,
                          h2=h2, gate=gate, up=up, act=act, f=f, win=win, wa=wa, wb=wb, wo=wo, wg=wg, wu=wu, wd=wd,
                          mod=(sh1, sc1, g1, sh2, sc2, g2), qg2=qg2, kg2=kg2))
        h_in = x2

    dx, loss_tile = _loss_head(h_in, target)

    small = [None] * n_l
    for l in reversed(range(n_l)):
        sv = saved[l]
        sh1, sc1, g1, sh2, sc2, g2 = sv["mod"]
        f4, n4, r4 = sv["wg"].shape[-1], sv["win"].shape[-1], d // N_CHIP
        hsp = lambda tk: ((tk, d), lambda j, k: (k, 0))
        fsp = lambda tk: ((None, tk, f4), lambda j, k: (j, k, 0))
        csp = lambda tk: ((tk, r4), lambda j, k: (k, j))
        dgate, dup, df, dg2 = _ffn_bwd1(dx, sv["f"], g2, sv["wd"], sv["gate"], sv["up"])
        g_wd = _mm_tn(sv["act"], df, fsp, hsp, (f4, d), "grad_wd")
        g_wg = _mm_tn(sv["h2"], dgate, hsp, fsp, (d, f4), "grad_wg")
        g_wu = _mm_tn(sv["h2"], dup, hsp, fsp, (d, f4), "grad_wu")
        tie = send_grads(l, dict(wd=g_wd, wg=g_wg, wu=g_wu))
        dh2 = _ffn_bwd2(dgate, dup, sv["wg"], sv["wu"])
        dx1, sums2 = _lnmod_bwd(sv["x1"], ln2_g[l:l + 1], sc2 + tie, dh2, dx)
        dmo, da, db, dya, dyb, dp, dg1 = _out_bwd(dx1, sv["mo"], g1, sv["wo"], sv["pa"], sv["pb"], sv["p"],
                                                        sv["wa"], sv["wb"], d)
        g_wo = _mm_tn(sv["merged"], dmo, csp, hsp, (r4, d), "grad_wo")
        g_wa = _mm_tn(sv["ya"], da, csp, hsp, (r4, d), "grad_wa")
        g_wb = _mm_tn(sv["yb"], db, csp, hsp, (r4, d), "grad_wb")
        tie = send_grads(l, dict(wo=g_wo, wa=g_wa, wb=g_wb))
        dp, dconv = _conv_bwd(sv["p"], conv_w[l] + tie, dyb, dp, d)
        dp, dgain = _attn_bwd(sv["p"], sv["qg2"], sv["kg2"], dya, sv["lt"], dp, d)
        g_win = _mm_tn(sv["h1"], dp, hsp, lambda tk: ((tk, n4), lambda j, k: (k, j)), (d, n4), "grad_win")
        tie = send_grads(l, dict(win=g_win))
        dh1 = _mm_in_bwd(dp, sv["win"])
        dx, sums1 = _lnmod_bwd(sv["x0"], ln1_g[l:l + 1], sc1 + tie, dh1, dx1)
        dgain = jnp.sum(dgain[:, 0:2, :], axis=0)
        dgain = dgain[:, :HEAD_DIM] + dgain[:, HEAD_DIM:]
        dmod = jnp.concatenate([sums1[0], sums1[1], dg1[0], sums2[0], sums2[1], dg2[0]])
        small[l] = dict(dmod=dmod, ln1=sums1[2], ln2=sums2[2], qg=dgain[0], kg=dgain[1], conv=dconv[0:3])
    return loss_tile, dx, small


def kernel(x, c, ada_w, ada_b, ln1_g, w_in, q_norm_g, k_norm_g, conv_w, w_branch_a, w_branch_b, w_out, ln2_g, w_ffn_gate, w_ffn_up, w_ffn_down, loss_target, m_ada_w, m_ada_b, m_ln1_g, m_w_in, m_q_norm_g, m_k_norm_g, m_conv_w, m_w_branch_a, m_w_branch_b, m_w_out, m_ln2_g, m_w_ffn_gate, m_w_ffn_up, m_w_ffn_down, v_ada_w, v_ada_b, v_ln1_g, v_w_in, v_q_norm_g, v_k_norm_g, v_conv_w, v_w_branch_a, v_w_branch_b, v_w_out, v_ln2_g, v_w_ffn_gate, v_w_ffn_up, v_w_ffn_down):
    n_l, d, a4 = ada_w.shape
    cw4 = conv_w.shape[-1]
    ix, iy, ic = lax.axis_index("x"), lax.axis_index("y"), lax.axis_index("c")
    chip = 2 * ix + iy
    me = 2 * chip + ic

    big_w = dict(win=w_in, wa=w_branch_a, wb=w_branch_b, wo=w_out, wg=w_ffn_gate, wu=w_ffn_up, wd=w_ffn_down)
    big_m = dict(win=m_w_in, wa=m_w_branch_a, wb=m_w_branch_b, wo=m_w_out, wg=m_w_ffn_gate, wu=m_w_ffn_up,
                 wd=m_w_ffn_down)
    big_v = dict(win=v_w_in, wa=v_w_branch_a, wb=v_w_branch_b, wo=v_w_out, wg=v_w_ffn_gate, wu=v_w_ffn_up,
                 wd=v_w_ffn_down)

    got = _gather8(_pack([c, conv_w])).reshape(N_DEV, -1)
    c_all = got[:, :d]
    conv_all = got[:, d:d + n_l * 3 * cw4].reshape(N_CHIP, 2, n_l, 3, cw4)[:, 0]
    conv_full = jnp.transpose(conv_all, (1, 2, 0, 3)).reshape(n_l, 3, N_CHIP * cw4)
    b_cols = lax.dynamic_slice_in_dim(ada_b, chip * a4, a4, axis=1).reshape(n_l, 1, a4)
    mod_cols = _ada_mod(c_all, ada_w, b_cols)
    mod_all = _gather8(_pack([mod_cols])).reshape(N_DEV, -1)[:, :n_l * N_DEV * a4]
    mod_all = mod_all.reshape(N_CHIP, 2, n_l, N_DEV, a4)[:, 0]
    mods = lax.dynamic_index_in_dim(mod_all, me, axis=2, keepdims=False)
    mods = jnp.transpose(mods, (1, 0, 2)).reshape(n_l, N_CHIP * a4)

    weight_groups = [(l, names) for l in range(n_l) for names in (("win",), ("wa", "wb", "wo", "wg", "wu", "wd"))]
    group_srcs = [[big_w[k][l].astype(BF16) for k in names] for l, names in weight_groups]
    mods, group_srcs = lax.optimization_barrier((mods, group_srcs))
    started_w = {}

    def start_weights(gi):
        l, names = weight_groups[gi]
        st = _split_start("weights_start_%d" % gi, _weight_copies, group_srcs[gi],
                          [(N_CHIP,) + sh.shape for sh in group_srcs[gi]], 3)
        for k in names:
            started_w[(l, k)] = [gi, names, st, None]
        return st[4]

    mods, _ = lax.optimization_barrier((mods, start_weights(0)))

    def weights(l, names, after):
        entry = started_w[(l, names[0])]
        if entry[3] is None:
            lands = _split_wait("weights_wait_%d" % entry[0], _weight_copies, entry[2], after)
            nxt = entry[0] + 1
            if nxt < len(weight_groups):
                lands, group_srcs[nxt] = lax.optimization_barrier((lands, group_srcs[nxt]))
                after, _ = lax.optimization_barrier((after, start_weights(nxt)))
            lands = [lax.dynamic_update_index_in_dim(land, own, chip, 0) for land, own in zip(lands, entry[2][2])]
            for k in entry[1]:
                started_w[(l, k)][3] = dict(zip(entry[1], lands))
        return [started_w[(l, k)][3][k] for k in names], after

    started_g, held_back = [], []

    def start_grads(l, grads, copies=_grad_copies, sems_per=7):
        names = tuple(grads)
        st = _split_start("grads_start_%d" % len(started_g), copies, [grads[k] for k in names],
                          [(N_DEV,) + grads[k].shape[1:] for k in names], sems_per)
        started_g.append((l, names, st, copies))
        return st[4][0, 0]

    def send_grads(l, grads):
        if l == 0 and tuple(grads) == GRAD_GROUPS[-1]:
            held_back.append(grads)
            return jnp.zeros((), F32)
        return start_grads(l, grads)

    loss_tile, grad_x, small = _local_step(
        x[0], loss_target[0], mods, ln1_g, ln2_g, q_norm_g, k_norm_g, conv_full, weights, send_grads)

    sm_shapes = [(n_l, 6 * d), (n_l, d), (n_l, d), (n_l, HEAD_DIM), (n_l, HEAD_DIM), (n_l, 3, d), (1,)]
    vec = _pack([jnp.stack([small[l][k] for l in range(n_l)]) for k in ("dmod", "ln1", "ln2", "qg", "kg", "conv")]
                + [loss_tile[0, 0:1]])
    n_vec = vec.shape[1] * 8
    all_vec = _gather8(vec).reshape(N_DEV, n_vec)
    all_vec, held_back = lax.optimization_barrier((all_vec, held_back))
    tie = sum([start_grads(0, grads, _grad_copies_same_core, 4) for grads in held_back], jnp.zeros((), F32))
    per_dev = [_unpack(all_vec[dev], sm_shapes) for dev in range(N_DEV)]
    dmod_all = jnp.stack([pd[0] for pd in per_dev])
    dmod_cols = jnp.transpose(lax.dynamic_slice_in_dim(dmod_all, chip * a4, a4, axis=2), (1, 0, 2))
    ada_out = _ada_grad_adam(jnp.transpose(c_all) + tie, dmod_cols, ada_w, m_ada_w, v_ada_w)

    dev_parts = jnp.stack([
        _pack([pd[0], pd[1], pd[2], pd[3], pd[4], lax.dynamic_slice_in_dim(pd[5], chip * cw4, cw4, axis=2), pd[6]])
        for pd in per_dev])
    zero1 = jnp.zeros((1,), F32)
    sw = _pack([ada_b, ln1_g, ln2_g, q_norm_g, k_norm_g, conv_w, zero1])
    sm = _pack([m_ada_b, m_ln1_g, m_ln2_g, m_q_norm_g, m_k_norm_g, m_conv_w, zero1])
    sv = _pack([v_ada_b, v_ln1_g, v_ln2_g, v_q_norm_g, v_k_norm_g, v_conv_w, zero1 + 1.0])
    out_shapes = [(n_l, 6 * d), (n_l, d), (n_l, d), (n_l, HEAD_DIM), (n_l, HEAD_DIM), (n_l, 3, cw4), (1,)]
    sm_out = [_unpack(o, out_shapes) for o in _small_adam(dev_parts, sw, sm, sv)]
    loss = 0.5 * sm_out[0][6][0] / d

    big_out, after = {}, sm_out[0][0]
    for names in GRAD_GROUPS:
        got_parts = {}
        for gi, (l, sent, st, copies) in enumerate(started_g):
            if sent == names:
                parts = _split_wait("grads_wait_%d" % gi, copies, st, after)
                if copies is _grad_copies_same_core:
                    passed = _split_start_in_place("grads_pass_start_%d" % gi, _grad_pass_copies, parts, 3)
                    parts = _split_wait_in_place("grads_pass_wait_%d" % gi, _grad_pass_copies, passed, passed[3])
                for k, part, grad in zip(sent, parts, st[2]):
                    own = lax.dynamic_index_in_dim(grad, chip, 0, keepdims=False)
                    got_parts[(l, k)] = lax.dynamic_update_index_in_dim(part, own, me, 0)
        for k in names:
            big_out[k] = _sum_adam([got_parts[(l, k)] for l in range(n_l)], big_w[k], big_m[k], big_v[k],
                                   "sum_adam_" + k)
            after = big_out[k][0]

    outs = [loss, grad_x[None]]
    for kind in range(4):
        sm_k = sm_out[kind]
        outs += [ada_out[kind], sm_k[0], sm_k[1], big_out["win"][kind], sm_k[3], sm_k[4], sm_k[5],
                 big_out["wa"][kind], big_out["wb"][kind], big_out["wo"][kind], sm_k[2],
                 big_out["wg"][kind], big_out["wu"][kind], big_out["wd"][kind]]
    return tuple(outs)
```

```python
import math

import jax
import jax.numpy as jnp
from jax import lax
from jax.experimental import pallas as pl
from jax.experimental.pallas import tpu as pltpu

F32 = jnp.float32
BF16 = jnp.bfloat16
MESH_ID = pl.DeviceIdType.MESH

EPS = 1e-6
HEAD_DIM = 64
Q_BLOCK = 128
Q_SUPER = 1024
Q_SUPER_BWD = 1024
KEY_UNROLL = 4
LANES = 128
N_DEV = 8
N_CHIP = 4
VMEM_LIMIT_BYTES = 56 * 1024 * 1024

ADAM_LR = 0.001
ADAM_B1 = 0.9
ADAM_B2 = 0.999
ADAM_EPS = 1e-08
ADAM_WD = 0.01
ADAM_STEP = 10

HBM_SPEC = pl.BlockSpec(memory_space=pltpu.HBM)
ANY_SPEC = pl.BlockSpec(memory_space=pl.ANY)
SEM_SPEC = pl.BlockSpec(memory_space=pltpu.SEMAPHORE)
VMEM_SPEC = pl.BlockSpec(memory_space=pltpu.VMEM)
SIDE_EFFECT = pltpu.SideEffectType.DATAFLOW_SIDE_EFFECTING


def _params(*sem):
    return pltpu.CompilerParams(dimension_semantics=tuple(sem), vmem_limit_bytes=VMEM_LIMIT_BYTES)


def _tile(n, pref):
    return pref if n % pref == 0 else n


def _dot(a, b):
    return jnp.dot(a, b, preferred_element_type=F32)


def _dot_nt(a, b):
    return lax.dot_general(a, b, (((1,), (1,)), ((), ())), preferred_element_type=F32)


def _dot_tn(a, b):
    return lax.dot_general(a, b, (((0,), (0,)), ((), ())), preferred_element_type=F32)


def _adamw(w, g, m, v):
    m = ADAM_B1 * m + (1.0 - ADAM_B1) * g
    v = ADAM_B2 * v + (1.0 - ADAM_B2) * (g * g)
    m_hat = m / (1.0 - ADAM_B1 ** ADAM_STEP)
    v_hat = v / (1.0 - ADAM_B2 ** ADAM_STEP)
    delta = -ADAM_LR * (m_hat / (jnp.sqrt(v_hat) + ADAM_EPS) + ADAM_WD * w)
    return delta, m, v


def _hbm(a):
    return pltpu.with_memory_space_constraint(a, pltpu.HBM)


def _peer(x, y, c, k):
    return (1 - x if k & 4 else x, 1 - y if k & 2 else y, 1 - c if k & 1 else c)


def _gather8(v):
    rows_per, m = v.shape

    def body(v_ref, out_ref, send_sems, recv_sems, local_sem):
        x, y, c = lax.axis_index("x"), lax.axis_index("y"), lax.axis_index("c")

        def rows(p):
            return out_ref.at[pl.ds((4 * p[0] + 2 * p[1] + p[2]) * rows_per, rows_per), :]

        me = (x, y, c)
        mine = pltpu.make_async_copy(v_ref, rows(me), local_sem)
        mine.start()
        sends = []
        for k in range(1, N_DEV):
            cp = pltpu.make_async_remote_copy(
                src_ref=v_ref, dst_ref=rows(me), send_sem=send_sems.at[k - 1], recv_sem=recv_sems.at[k - 1],
                device_id=_peer(x, y, c, k), device_id_type=MESH_ID)
            cp.start()
            sends.append(cp)
        for k in range(1, N_DEV):
            pltpu.make_async_remote_copy(
                src_ref=v_ref, dst_ref=rows(_peer(x, y, c, k)), send_sem=send_sems.at[k - 1],
                recv_sem=recv_sems.at[k - 1], device_id=_peer(x, y, c, k), device_id_type=MESH_ID).wait_recv()
        for cp in sends:
            cp.wait_send()
        mine.wait()

    return pl.pallas_call(
        body, name="gather8",
        out_shape=jax.ShapeDtypeStruct((N_DEV * rows_per, m), v.dtype),
        in_specs=[VMEM_SPEC], out_specs=VMEM_SPEC,
        scratch_shapes=[pltpu.SemaphoreType.DMA((N_DEV - 1,)), pltpu.SemaphoreType.DMA((N_DEV - 1,)),
                        pltpu.SemaphoreType.DMA],
    )(v)


def _weight_copies(srcs, lands, send_sems, recv_sems):
    x, y, c = lax.axis_index("x"), lax.axis_index("y"), lax.axis_index("c")
    chips = [(1 - x, y), (x, 1 - y), (1 - x, 1 - y)]
    sends, recvs = [], []
    for a, (src, land) in enumerate(zip(srcs, lands)):
        for j, (px, py) in enumerate(chips):
            def copy(dst_block, a=a, j=j, px=px, py=py, src=src, land=land):
                return pltpu.make_async_remote_copy(
                    src_ref=src, dst_ref=land.at[dst_block], send_sem=send_sems.at[3 * a + j],
                    recv_sem=recv_sems.at[3 * a + j], device_id=(px, py, c), device_id_type=MESH_ID)
            sends.append(copy(2 * x + y))
            recvs.append(copy(2 * px + py))
    return sends, recvs


def _split_start(name, copies, srcs, land_shapes, sems_per_src):
    n = len(srcs)

    def body(*refs):
        sends, _ = copies(refs[:n], refs[n + 2:2 * n + 2], refs[n], refs[n + 1])
        for cp in sends:
            cp.start()
        token = refs[-1]
        token[...] = jnp.zeros_like(token)

    n_sems = sems_per_src * n
    outs = pl.pallas_call(
        body, name=name,
        out_shape=(pltpu.SemaphoreType.DMA((n_sems,)), pltpu.SemaphoreType.DMA((n_sems,)),
                   *[pltpu.HBM(shape, a.dtype) for a, shape in zip(srcs, land_shapes)],
                   jax.ShapeDtypeStruct((8, LANES), F32)),
        in_specs=[HBM_SPEC] * n, out_specs=(SEM_SPEC, SEM_SPEC, *[HBM_SPEC] * n, VMEM_SPEC),
        compiler_params=pltpu.CompilerParams(has_side_effects=SIDE_EFFECT),
    )(*[_hbm(a) for a in srcs])
    return outs[0], outs[1], list(srcs), list(outs[2:2 + n]), outs[-1]


def _split_wait(name, copies, started, after):
    send_sems, recv_sems, srcs, lands, _ = started
    n = len(srcs)

    def body(*refs):
        sends, recvs = copies(refs[:n], refs[n:2 * n], refs[2 * n], refs[2 * n + 1])
        for cp in sends:
            cp.wait_send()
        for cp in recvs:
            cp.wait_recv()

    return pl.pallas_call(
        body, name=name,
        out_shape=tuple(pltpu.HBM(a.shape, a.dtype) for a in lands),
        in_specs=[HBM_SPEC] * (2 * n) + [SEM_SPEC, SEM_SPEC, ANY_SPEC], out_specs=tuple([HBM_SPEC] * n),
        input_output_aliases={n + i: i for i in range(n)},
        compiler_params=pltpu.CompilerParams(has_side_effects=SIDE_EFFECT),
    )(*srcs, *lands, send_sems, recv_sems, after)


def _split_start_in_place(name, copies, bufs, sems_per_buf):
    n = len(bufs)

    def body(*refs):
        sends, _ = copies(refs[:n], refs[:n], refs[n], refs[n + 1])
        for cp in sends:
            cp.start()
        token = refs[-1]
        token[...] = jnp.zeros_like(token)

    n_sems = sems_per_buf * n
    outs = pl.pallas_call(
        body, name=name,
        out_shape=(pltpu.SemaphoreType.DMA((n_sems,)), pltpu.SemaphoreType.DMA((n_sems,)),
                   *[pltpu.HBM(a.shape, a.dtype) for a in bufs], jax.ShapeDtypeStruct((8, LANES), F32)),
        in_specs=[HBM_SPEC] * n, out_specs=(SEM_SPEC, SEM_SPEC, *[HBM_SPEC] * n, VMEM_SPEC),
        input_output_aliases={i: 2 + i for i in range(n)},
        compiler_params=pltpu.CompilerParams(has_side_effects=SIDE_EFFECT),
    )(*[_hbm(a) for a in bufs])
    return outs[0], outs[1], list(outs[2:2 + n]), outs[-1]


def _split_wait_in_place(name, copies, started, after):
    send_sems, recv_sems, bufs, _ = started
    n = len(bufs)

    def body(*refs):
        sends, recvs = copies(refs[:n], refs[:n], refs[n], refs[n + 1])
        for cp in sends:
            cp.wait_send()
        for cp in recvs:
            cp.wait_recv()

    return pl.pallas_call(
        body, name=name,
        out_shape=tuple(pltpu.HBM(a.shape, a.dtype) for a in bufs),
        in_specs=[HBM_SPEC] * n + [SEM_SPEC, SEM_SPEC, ANY_SPEC], out_specs=tuple([HBM_SPEC] * n),
        input_output_aliases={i: i for i in range(n)},
        compiler_params=pltpu.CompilerParams(has_side_effects=SIDE_EFFECT),
    )(*bufs, send_sems, recv_sems, after)


def _grad_copies(grads, parts, send_sems, recv_sems):
    x, y, c = lax.axis_index("x"), lax.axis_index("y"), lax.axis_index("c")
    chips = [(1 - x, y), (x, 1 - y), (1 - x, 1 - y)]
    my_slot = 4 * x + 2 * y + c
    sends, recvs = [], []
    for a, (grad, part) in enumerate(zip(grads, parts)):
        def copy(k, block, slot, to, a=a, grad=grad, part=part):
            return pltpu.make_async_remote_copy(
                src_ref=grad.at[block], dst_ref=part.at[slot], send_sem=send_sems.at[7 * a + k],
                recv_sem=recv_sems.at[7 * a + k], device_id=to, device_id_type=MESH_ID)
        sends.append(copy(0, 2 * x + y, my_slot, (x, y, 1 - c)))
        recvs.append(copy(0, 2 * x + y, 4 * x + 2 * y + (1 - c), (x, y, 1 - c)))
        for j, (px, py) in enumerate(chips):
            for other, pc in enumerate((c, 1 - c)):
                sends.append(copy(1 + 2 * j + other, 2 * px + py, my_slot, (px, py, pc)))
                recvs.append(copy(1 + 2 * j + other, 2 * x + y, 4 * px + 2 * py + pc, (px, py, pc)))
    return sends, recvs


def _grad_copies_same_core(grads, parts, send_sems, recv_sems):
    x, y, c = lax.axis_index("x"), lax.axis_index("y"), lax.axis_index("c")
    chips = [(1 - x, y), (x, 1 - y), (1 - x, 1 - y)]
    my_slot = 4 * x + 2 * y + c
    sends, recvs = [], []
    for a, (grad, part) in enumerate(zip(grads, parts)):
        def copy(k, block, slot, to, a=a, grad=grad, part=part):
            return pltpu.make_async_remote_copy(
                src_ref=grad.at[block], dst_ref=part.at[slot], send_sem=send_sems.at[4 * a + k],
                recv_sem=recv_sems.at[4 * a + k], device_id=to, device_id_type=MESH_ID)
        sends.append(copy(0, 2 * x + y, my_slot, (x, y, 1 - c)))
        recvs.append(copy(0, 2 * x + y, 4 * x + 2 * y + (1 - c), (x, y, 1 - c)))
        for j, (px, py) in enumerate(chips):
            sends.append(copy(1 + j, 2 * px + py, my_slot, (px, py, c)))
            recvs.append(copy(1 + j, 2 * x + y, 4 * px + 2 * py + c, (px, py, c)))
    return sends, recvs


def _grad_pass_copies(parts, same_parts, send_sems, recv_sems):
    del same_parts
    x, y, c = lax.axis_index("x"), lax.axis_index("y"), lax.axis_index("c")
    chips = [(1 - x, y), (x, 1 - y), (1 - x, 1 - y)]
    sends, recvs = [], []
    for a, part in enumerate(parts):
        for j, (px, py) in enumerate(chips):
            def copy(pc, a=a, j=j, px=px, py=py, part=part):
                slot = part.at[4 * px + 2 * py + pc]
                return pltpu.make_async_remote_copy(
                    src_ref=slot, dst_ref=slot, send_sem=send_sems.at[3 * a + j], recv_sem=recv_sems.at[3 * a + j],
                    device_id=(x, y, 1 - c), device_id_type=MESH_ID)
            sends.append(copy(c))
            recvs.append(copy(1 - c))
    return sends, recvs


def _ada_mod(c_all, ada_w, ada_b_cols):
    n_l, d, a4 = ada_w.shape
    tn = _tile(a4, 512)

    def body(c_ref, w_ref, b_ref, o_ref):
        cv = c_ref[...]
        ca = (cv * jax.nn.sigmoid(cv)).astype(BF16)
        o_ref[...] = _dot(ca, w_ref[...].astype(BF16)) + b_ref[...]

    return pl.pallas_call(
        body, name="ada_mod", grid=(n_l, a4 // tn),
        in_specs=[pl.BlockSpec((N_DEV, d), lambda l, j: (0, 0)),
                  pl.BlockSpec((None, d, tn), lambda l, j: (l, 0, j)),
                  pl.BlockSpec((None, 1, tn), lambda l, j: (l, 0, j))],
        out_specs=pl.BlockSpec((None, N_DEV, tn), lambda l, j: (l, 0, j)),
        out_shape=jax.ShapeDtypeStruct((n_l, N_DEV, a4), F32),
        compiler_params=_params("parallel", "parallel"),
    )(c_all, ada_w, ada_b_cols)


def _ada_grad_adam(c_all_t, dmod_cols, w, m, v):
    n_l, d, a4 = w.shape
    tn = _tile(a4, 512)

    def body(ct_ref, dm_ref, w_ref, m_ref, v_ref, g_ref, dl_ref, nm_ref, nv_ref):
        ct = ct_ref[...]
        ca = ct * jax.nn.sigmoid(ct)
        dm = dm_ref[...]
        g = ca[:, 0:1] * dm[0:1, :]
        for dev in range(1, N_DEV):
            g = g + ca[:, dev:dev + 1] * dm[dev:dev + 1, :]
        g_ref[...] = g
        delta, nm, nv = _adamw(w_ref[...], g, m_ref[...], v_ref[...])
        dl_ref[...] = delta
        nm_ref[...] = nm
        nv_ref[...] = nv

    wspec = pl.BlockSpec((None, d, tn), lambda l, j: (l, 0, j))
    shp = jax.ShapeDtypeStruct(w.shape, F32)
    return pl.pallas_call(
        body, name="ada_grad_adam", grid=(n_l, a4 // tn),
        in_specs=[pl.BlockSpec((d, N_DEV), lambda l, j: (0, 0)),
                  pl.BlockSpec((None, N_DEV, tn), lambda l, j: (l, 0, j)), wspec, wspec, wspec],
        out_specs=[wspec] * 4, out_shape=[shp] * 4,
        compiler_params=_params("parallel", "parallel"),
    )(c_all_t, dmod_cols, w, m, v)


def _lnmod(x, g, sc, sh):
    s, d = x.shape
    tm = _tile(s, 512)

    def body(x_ref, g_ref, sc_ref, sh_ref, h_ref):
        xv = x_ref[...]
        r = lax.rsqrt(jnp.mean(xv * xv, axis=-1, keepdims=True) + EPS)
        h_ref[...] = ((xv * r * g_ref[...]) * (1.0 + sc_ref[...]) + sh_ref[...]).astype(BF16)

    vec = pl.BlockSpec((1, d), lambda i: (0, 0))
    row = pl.BlockSpec((tm, d), lambda i: (i, 0))
    return pl.pallas_call(
        body, name="lnmod", grid=(s // tm,), in_specs=[row, vec, vec, vec], out_specs=row,
        out_shape=jax.ShapeDtypeStruct((s, d), BF16), compiler_params=_params("parallel"),
    )(x, g, sc, sh)


def _mm_in(h, w_g):
    s, d = h.shape
    n4 = w_g.shape[-1]
    tm = _tile(s, 512)

    def body(a_ref, b_ref, o_ref):
        o_ref[...] = _dot(a_ref[...], b_ref[...])

    return pl.pallas_call(
        body, name="mm_in", grid=(N_CHIP, s // tm),
        in_specs=[pl.BlockSpec((tm, d), lambda j, i: (i, 0)),
                  pl.BlockSpec((None, d, n4), lambda j, i: (j, 0, 0))],
        out_specs=pl.BlockSpec((tm, n4), lambda j, i: (i, j)),
        out_shape=jax.ShapeDtypeStruct((s, N_CHIP * n4), F32),
        compiler_params=_params("parallel", "parallel"),
    )(h, w_g)


def _pair_mean(x, low):
    lo = jnp.sum(jnp.where(low, x, 0.0), axis=-1, keepdims=True)
    hi = jnp.sum(jnp.where(low, 0.0, x), axis=-1, keepdims=True)
    return jnp.where(low, lo, hi) * (1.0 / HEAD_DIM)


def _pair_norm(x, low):
    r = lax.rsqrt(_pair_mean(x * x, low) + EPS)
    return x * r, r


def _log_not(z):
    return jnp.minimum(-z, 0.0) - jnp.log(1.0 + jnp.exp(-jnp.abs(z)))


def _attn_consts(inclusive):
    low = lax.broadcasted_iota(jnp.int32, (1, LANES), 1) < HEAD_DIM
    row = lax.broadcasted_iota(jnp.int32, (Q_BLOCK, Q_BLOCK), 0)
    col = lax.broadcasted_iota(jnp.int32, (Q_BLOCK, Q_BLOCK), 1)
    tri = (row <= col) if inclusive else (row > col)
    w2 = jnp.concatenate([tri.astype(BF16), jnp.ones((Q_BLOCK, Q_BLOCK), BF16)], axis=1)
    return low, col < row, jnp.concatenate([w2, w2], axis=0)


def _split_cat(v):
    hi = v.astype(BF16)
    return jnp.concatenate([hi, (v - hi.astype(F32)).astype(BF16)], axis=1)


def _fill_pair_blocks(dst, src_fn, low, n_kb):
    def fill(b, _):
        v = src_fn(pl.ds(pl.multiple_of(b * Q_BLOCK, Q_BLOCK), Q_BLOCK))
        dst[b, 0:Q_BLOCK, :] = jnp.where(low, v, 0.0).astype(BF16)
        dst[b, Q_BLOCK:2 * Q_BLOCK, :] = jnp.where(low, 0.0, v).astype(BF16)
        return 0

    lax.fori_loop(0, n_kb, fill, 0)


def _attn_fwd(p, qg2, kg2, d):
    s = p.shape[0]
    n_pairs = d // LANES
    qsb = _tile(s, Q_SUPER)
    n_sub, n_sb, n_kb = qsb // Q_BLOCK, s // qsb, s // Q_BLOCK
    unroll = math.gcd(KEY_UNROLL, n_sub)
    chunk = _tile(s, 512)
    inv_sqrt = 1.0 / math.sqrt(HEAD_DIM)

    def body(q_ref, k_ref, v_ref, qg_ref, kg_ref, o_ref, lt_ref, qs, k2, v2, run, acc):
        low, causal, w4 = _attn_consts(False)

        def prep(r, _):
            rows = pl.ds(pl.multiple_of(r * chunk, chunk), chunk)
            qs[rows, :] = (_pair_norm(q_ref[rows, :], low)[0] * (qg_ref[...] * inv_sqrt)).astype(BF16)
            return 0

        lax.fori_loop(0, s // chunk, prep, 0)
        _fill_pair_blocks(k2, lambda rows: _pair_norm(k_ref[rows, :], low)[0] * kg_ref[...], low, n_kb)
        _fill_pair_blocks(v2, lambda rows: v_ref[rows, :], low, n_kb)

        def step(sb, j, t0=0, diag_t=None):
            rows = pl.ds(pl.multiple_of(sb * qsb + t0 * Q_BLOCK, Q_BLOCK), (n_sub - t0) * Q_BLOCK)
            z_both = _dot_nt(qs[rows, :], k2[j])
            zls, cats = [], []
            for t in range(t0, n_sub):
                sub = slice((t - t0) * Q_BLOCK, (t - t0 + 1) * Q_BLOCK)
                for h in range(2):
                    z = z_both[sub, h * LANES:(h + 1) * LANES]
                    ln = _log_not(z)
                    if t == diag_t:
                        ln = jnp.where(causal, ln, 0.0)
                    zls.append(z + ln)
                    cats.append(_split_cat(ln))
            c2 = _dot(jnp.concatenate(cats, axis=0), w4)
            a_rows = []
            for t in range(t0, n_sub):
                sub = slice(t * Q_BLOCK, (t + 1) * Q_BLOCK)
                a_pair = []
                for h in range(2):
                    i = 2 * (t - t0) + h
                    tile = slice(i * Q_BLOCK, (i + 1) * Q_BLOCK)
                    later = run[h, sub, :]
                    log_a = zls[i] + c2[tile, :LANES] + later
                    if t == diag_t:
                        log_a = jnp.where(causal, log_a, -1e30)
                    a_pair.append(jnp.exp(log_a).astype(BF16))
                    run[h, sub, :] = later + c2[tile, LANES:]
                a_rows.append(jnp.concatenate(a_pair, axis=1))
            acc[t0 * Q_BLOCK:, :] += _dot(jnp.concatenate(a_rows, axis=0), v2[j])

        def super_block(sb, _):
            run[...] = jnp.zeros_like(run)
            acc[...] = jnp.zeros_like(acc)
            for t in reversed(range(n_sub)):
                step(sb, sb * n_sub + t, t0=t, diag_t=t)

            def below(n, _):
                for u in range(unroll):
                    step(sb, sb * n_sub - 1 - (unroll * n + u))
                return 0

            lax.fori_loop(0, sb * (n_sub // unroll), below, 0)
            rows_sb = pl.ds(pl.multiple_of(sb * qsb, qsb), qsb)
            o_ref[rows_sb, :] = acc[...].astype(BF16)
            lt_ref[rows_sb, :] = jnp.where(low, run[0], run[1])
            return 0

        lax.fori_loop(0, n_sb, super_block, 0)

    def seg(k):
        return pl.BlockSpec((s, LANES), lambda h, k=k: (0, k * n_pairs + h))

    vec = pl.BlockSpec((1, LANES), lambda h: (0, 0))
    out = pl.BlockSpec((s, LANES), lambda h: (0, h))
    return pl.pallas_call(
        body, name="attn_fwd", grid=(n_pairs,),
        in_specs=[seg(0), seg(1), seg(2), vec, vec], out_specs=[out, out],
        out_shape=[jax.ShapeDtypeStruct((s, d), BF16), jax.ShapeDtypeStruct((s, d), F32)],
        scratch_shapes=[pltpu.VMEM((s, LANES), BF16)] + [pltpu.VMEM((n_kb, 2 * Q_BLOCK, LANES), BF16)] * 2
        + [pltpu.VMEM((2, qsb, LANES), F32), pltpu.VMEM((qsb, LANES), F32)],
        compiler_params=_params("parallel"),
    )(p, p, p, qg2, kg2)


def _conv_rows(s):
    return _tile(s, 512)


def _conv_fwd(p, conv_w, d):
    s = p.shape[0]
    nb = d // LANES
    rows_n = _conv_rows(s)

    def body(cb_ref, cc_ref, cx_ref, w_ref, y_ref, us):
        us[pl.ds(0, 8), :] = jnp.zeros((8, LANES), F32)

        def fill(r, _):
            rows = pl.ds(pl.multiple_of(r * rows_n, rows_n), rows_n)
            us[pl.ds(pl.multiple_of(r * rows_n + 8, 8), rows_n), :] = cc_ref[rows, :] * cx_ref[rows, :]
            return 0

        lax.fori_loop(0, s // rows_n, fill, 0)
        w = w_ref[...]

        def out(r, _):
            rows = pl.ds(pl.multiple_of(r * rows_n, rows_n), rows_n)
            ext = us[pl.ds(pl.multiple_of(r * rows_n, 8), rows_n + 8), :]
            cv = (w[0:1, :] * pltpu.roll(ext, 2, 0)[8:, :] + w[1:2, :] * pltpu.roll(ext, 1, 0)[8:, :]
                  + w[2:3, :] * ext[8:, :])
            y_ref[rows, :] = (cb_ref[rows, :] * cv).astype(BF16)
            return 0

        lax.fori_loop(0, s // rows_n, out, 0)

    def seg(k):
        return pl.BlockSpec((s, LANES), lambda b, k=k: (0, k * nb + b))

    return pl.pallas_call(
        body, name="conv_fwd", grid=(nb,),
        in_specs=[seg(3), seg(4), seg(5), pl.BlockSpec((3, LANES), lambda b: (0, b))],
        out_specs=pl.BlockSpec((s, LANES), lambda b: (0, b)),
        out_shape=jax.ShapeDtypeStruct((s, d), BF16),
        scratch_shapes=[pltpu.VMEM((s + 8, LANES), F32)],
        compiler_params=_params("parallel"),
    )(p, p, p, conv_w)


def _branch(ya, yb, p, wa, wb, d):
    s = ya.shape[0]
    tm = _tile(s, 512)

    def body(ya_ref, yb_ref, ga_ref, gb_ref, wa_ref, wb_ref, m_ref, a_ref, b_ref):
        pa = _dot(ya_ref[...], wa_ref[...])
        pb = _dot(yb_ref[...], wb_ref[...])
        m_ref[...] = (jax.nn.sigmoid(ga_ref[...]) * pa + jax.nn.sigmoid(gb_ref[...]) * pb).astype(BF16)
        a_ref[...] = pa.astype(BF16)
        b_ref[...] = pb.astype(BF16)

    row = pl.BlockSpec((tm, d), lambda i: (i, 0))
    wsp = pl.BlockSpec((d, d), lambda i: (0, 0))
    shp = jax.ShapeDtypeStruct((s, d), BF16)
    return pl.pallas_call(
        body, name="branch", grid=(s // tm,),
        in_specs=[row, row, pl.BlockSpec((tm, d), lambda i: (i, 6)), pl.BlockSpec((tm, d), lambda i: (i, 7)), wsp, wsp],
        out_specs=[row, row, row], out_shape=[shp, shp, shp], compiler_params=_params("parallel"),
    )(ya, yb, p, p, wa, wb)


def _out_proj(merged, wout, x0, g1):
    s, d = x0.shape
    tm = _tile(s, 512)

    def body(m_ref, w_ref, x_ref, g_ref, x1_ref, mo_ref):
        mo = _dot(m_ref[...], w_ref[...])
        mo_ref[...] = mo
        x1_ref[...] = x_ref[...] + g_ref[...] * mo

    row = pl.BlockSpec((tm, d), lambda i: (i, 0))
    shp = jax.ShapeDtypeStruct((s, d), F32)
    return pl.pallas_call(
        body, name="out_proj", grid=(s // tm,),
        in_specs=[row, pl.BlockSpec((d, d), lambda i: (0, 0)), row, pl.BlockSpec((1, d), lambda i: (0, 0))],
        out_specs=[row, row], out_shape=[shp, shp], compiler_params=_params("parallel"),
    )(merged, wout, x0, g1)


def _ffn_up(h, wg_g, wu_g):
    s, d = h.shape
    f4 = wg_g.shape[-1]
    tm = _tile(s, 512)

    def body(h_ref, wg_ref, wu_ref, gate_ref, up_ref, act_ref):
        hv = h_ref[...]
        gt = _dot(hv, wg_ref[...])
        up = _dot(hv, wu_ref[...])
        gate_ref[...] = gt.astype(BF16)
        up_ref[...] = up.astype(BF16)
        act_ref[...] = (gt * jax.nn.sigmoid(gt) * up).astype(BF16)

    wsp = pl.BlockSpec((None, d, f4), lambda j, i: (j, 0, 0))
    osp = pl.BlockSpec((None, tm, f4), lambda j, i: (j, i, 0))
    shp = jax.ShapeDtypeStruct((N_CHIP, s, f4), BF16)
    return pl.pallas_call(
        body, name="ffn_up", grid=(N_CHIP, s // tm),
        in_specs=[pl.BlockSpec((tm, d), lambda j, i: (i, 0)), wsp, wsp],
        out_specs=[osp, osp, osp], out_shape=[shp, shp, shp], compiler_params=_params("parallel", "parallel"),
    )(h, wg_g, wu_g)


def _ffn_down(act, wd_g, x1, g2):
    s, d = x1.shape
    f4 = act.shape[-1]
    tm = _tile(s, 512)

    def body(a_ref, w_ref, x_ref, g_ref, x2_ref, f_ref, acc):
        j = pl.program_id(1)

        @pl.when(j == 0)
        def _():
            acc[...] = jnp.zeros_like(acc)

        acc[...] += _dot(a_ref[...], w_ref[...])

        @pl.when(j == N_CHIP - 1)
        def _():
            f = acc[...]
            f_ref[...] = f
            x2_ref[...] = x_ref[...] + g_ref[...] * f

    row = pl.BlockSpec((tm, d), lambda i, j: (i, 0))
    shp = jax.ShapeDtypeStruct((s, d), F32)
    return pl.pallas_call(
        body, name="ffn_down", grid=(s // tm, N_CHIP),
        in_specs=[pl.BlockSpec((None, tm, f4), lambda i, j: (j, i, 0)),
                  pl.BlockSpec((None, f4, d), lambda i, j: (j, 0, 0)),
                  row, pl.BlockSpec((1, d), lambda i, j: (0, 0))],
        out_specs=[row, row], out_shape=[shp, shp],
        scratch_shapes=[pltpu.VMEM((tm, d), F32)], compiler_params=_params("parallel", "arbitrary"),
    )(act, wd_g, x1, g2)


def _loss_head(y, target):
    s, d = y.shape
    tm = _tile(s, 512)
    n_steps = s // tm

    def body(y_ref, t_ref, dy_ref, l_ref, acc):
        i = pl.program_id(0)

        @pl.when(i == 0)
        def _():
            acc[...] = jnp.zeros_like(acc)

        err = y_ref[...] - t_ref[...]
        dy_ref[...] = err / d
        acc[...] += jnp.sum(err * err, axis=0, keepdims=True)

        @pl.when(i == n_steps - 1)
        def _():
            l_ref[...] = jnp.broadcast_to(jnp.sum(acc[...], axis=1, keepdims=True), (8, LANES))

    row = pl.BlockSpec((tm, d), lambda i: (i, 0))
    return pl.pallas_call(
        body, name="loss_head", grid=(n_steps,), in_specs=[row, row],
        out_specs=[row, pl.BlockSpec((8, LANES), lambda i: (0, 0))],
        out_shape=[jax.ShapeDtypeStruct((s, d), F32), jax.ShapeDtypeStruct((8, LANES), F32)],
        scratch_shapes=[pltpu.VMEM((1, d), F32)], compiler_params=_params("arbitrary"),
    )(y, target)


def _mm_tn(a, b, a_spec, b_spec, out_rc, name):
    r, c = out_rc
    s = a.shape[-2]
    tk = _tile(s, 512)
    nk = s // tk

    def body(a_ref, b_ref, o_ref, acc):
        k = pl.program_id(1)

        @pl.when(k == 0)
        def _():
            acc[...] = jnp.zeros_like(acc)

        acc[...] += _dot_tn(a_ref[...], b_ref[...])

        @pl.when(k == nk - 1)
        def _():
            o_ref[...] = acc[...].astype(BF16)

    return pl.pallas_call(
        body, name=name, grid=(N_CHIP, nk),
        in_specs=[pl.BlockSpec(*a_spec(tk)), pl.BlockSpec(*b_spec(tk))],
        out_specs=pl.BlockSpec((None, r, c), lambda j, k: (j, 0, 0)),
        out_shape=jax.ShapeDtypeStruct((N_CHIP, r, c), BF16),
        scratch_shapes=[pltpu.VMEM((r, c), F32)], compiler_params=_params("parallel", "arbitrary"),
    )(a, b)


def _ffn_bwd1(dx2, f, g2, wd_g, gate, up):
    s, d = dx2.shape
    f4 = gate.shape[-1]
    tm = _tile(s, 512)

    def body(dx_ref, f_ref, g_ref, w_ref, gate_ref, up_ref, dgate_ref, dup_ref, df_ref, dg_ref):
        i, j = pl.program_id(0), pl.program_id(1)

        @pl.when((i == 0) & (j == 0))
        def _():
            dg_ref[...] = jnp.zeros_like(dg_ref)

        dxv = dx_ref[...]
        df = (g_ref[...] * dxv).astype(BF16)

        @pl.when(j == 0)
        def _():
            df_ref[...] = df
            dg_ref[0:1, :] += jnp.sum(dxv * f_ref[...], axis=0, keepdims=True)

        da = _dot_nt(df, w_ref[...])
        gt = gate_ref[...].astype(F32)
        sg = jax.nn.sigmoid(gt)
        dup_ref[...] = (da * gt * sg).astype(BF16)
        dgate_ref[...] = (da * up_ref[...].astype(F32) * (sg * (1.0 + gt * (1.0 - sg)))).astype(BF16)

    row = pl.BlockSpec((tm, d), lambda i, j: (i, 0))
    hsp = pl.BlockSpec((None, tm, f4), lambda i, j: (j, i, 0))
    hshp = jax.ShapeDtypeStruct((N_CHIP, s, f4), BF16)
    return pl.pallas_call(
        body, name="ffn_bwd1", grid=(s // tm, N_CHIP),
        in_specs=[row, row, pl.BlockSpec((1, d), lambda i, j: (0, 0)),
                  pl.BlockSpec((None, f4, d), lambda i, j: (j, 0, 0)), hsp, hsp],
        out_specs=[hsp, hsp, row, pl.BlockSpec((8, d), lambda i, j: (0, 0))],
        out_shape=[hshp, hshp, jax.ShapeDtypeStruct((s, d), BF16), jax.ShapeDtypeStruct((8, d), F32)],
        compiler_params=_params("arbitrary", "arbitrary"),
    )(dx2, f, g2, wd_g, gate, up)


def _ffn_bwd2(dgate, dup, wg_g, wu_g):
    _, s, f4 = dgate.shape
    d = wg_g.shape[-2]
    tm = _tile(s, 512)

    def body(dg_ref, du_ref, wg_ref, wu_ref, o_ref, acc):
        j = pl.program_id(1)

        @pl.when(j == 0)
        def _():
            acc[...] = jnp.zeros_like(acc)

        acc[...] += _dot_nt(dg_ref[...], wg_ref[...]) + _dot_nt(du_ref[...], wu_ref[...])

        @pl.when(j == N_CHIP - 1)
        def _():
            o_ref[...] = acc[...]

    hsp = pl.BlockSpec((None, tm, f4), lambda i, j: (j, i, 0))
    wsp = pl.BlockSpec((None, d, f4), lambda i, j: (j, 0, 0))
    return pl.pallas_call(
        body, name="ffn_bwd2", grid=(s // tm, N_CHIP), in_specs=[hsp, hsp, wsp, wsp],
        out_specs=pl.BlockSpec((tm, d), lambda i, j: (i, 0)), out_shape=jax.ShapeDtypeStruct((s, d), F32),
        scratch_shapes=[pltpu.VMEM((tm, d), F32)], compiler_params=_params("parallel", "arbitrary"),
    )(dgate, dup, wg_g, wu_g)


def _lnmod_bwd(x, g, sc, dh, dres):
    s, d = x.shape
    tm = _tile(s, 512)

    def body(x_ref, g_ref, sc_ref, dh_ref, dr_ref, dx_ref, sums_ref):
        @pl.when(pl.program_id(0) == 0)
        def _():
            sums_ref[...] = jnp.zeros_like(sums_ref)

        xv, dhv, gv = x_ref[...], dh_ref[...], g_ref[...]
        r = lax.rsqrt(jnp.mean(xv * xv, axis=-1, keepdims=True) + EPS)
        n = xv * r
        one_sc = 1.0 + sc_ref[...]
        dt = dhv * one_sc
        sums_ref[0:1, :] += jnp.sum(dhv, axis=0, keepdims=True)
        sums_ref[1:2, :] += jnp.sum(dhv * (n * gv), axis=0, keepdims=True)
        sums_ref[2:3, :] += jnp.sum(dt * n, axis=0, keepdims=True)
        dn = dt * gv
        dx_ref[...] = dr_ref[...] + r * (dn - n * jnp.mean(dn * n, axis=-1, keepdims=True))

    vec = pl.BlockSpec((1, d), lambda i: (0, 0))
    row = pl.BlockSpec((tm, d), lambda i: (i, 0))
    return pl.pallas_call(
        body, name="lnmod_bwd", grid=(s // tm,), in_specs=[row, vec, vec, row, row],
        out_specs=[row, pl.BlockSpec((8, d), lambda i: (0, 0))],
        out_shape=[jax.ShapeDtypeStruct((s, d), F32), jax.ShapeDtypeStruct((8, d), F32)],
        compiler_params=_params("arbitrary"),
    )(x, g, sc, dh, dres)


def _out_bwd(dx1, mo, g1, wout, pa, pb, p, wa, wb, d):
    s = dx1.shape[0]
    tm = _tile(s, 256)

    def body(dx_ref, mo_ref, g_ref, wo_ref, pa_ref, pb_ref, ga_ref, gb_ref, wa_ref, wb_ref,
             dmo_ref, da_ref, db_ref, dya_ref, dyb_ref, dp_ref, dg_ref):
        @pl.when(pl.program_id(0) == 0)
        def _():
            dg_ref[...] = jnp.zeros_like(dg_ref)

        dxv = dx_ref[...]
        dg_ref[0:1, :] += jnp.sum(dxv * mo_ref[...], axis=0, keepdims=True)
        dmo = (g_ref[...] * dxv).astype(BF16)
        dmo_ref[...] = dmo
        dm = _dot_nt(dmo, wo_ref[...])
        sa, sb = jax.nn.sigmoid(ga_ref[...]), jax.nn.sigmoid(gb_ref[...])
        da = (dm * sa).astype(BF16)
        db = (dm * sb).astype(BF16)
        da_ref[...] = da
        db_ref[...] = db
        dp_ref[:, :d] = (dm * pa_ref[...].astype(F32) * (sa * (1.0 - sa))).astype(BF16)
        dp_ref[:, d:] = (dm * pb_ref[...].astype(F32) * (sb * (1.0 - sb))).astype(BF16)
        dya_ref[...] = _dot_nt(da, wa_ref[...]).astype(BF16)
        dyb_ref[...] = _dot_nt(db, wb_ref[...]).astype(BF16)

    row = pl.BlockSpec((tm, d), lambda i: (i, 0))
    wsp = pl.BlockSpec((d, d), lambda i: (0, 0))
    shp = jax.ShapeDtypeStruct((s, d), BF16)
    return pl.pallas_call(
        body, name="out_bwd", grid=(s // tm,),
        in_specs=[row, row, pl.BlockSpec((1, d), lambda i: (0, 0)), wsp, row, row,
                  pl.BlockSpec((tm, d), lambda i: (i, 6)), pl.BlockSpec((tm, d), lambda i: (i, 7)), wsp, wsp],
        out_specs=[row] * 5 + [pl.BlockSpec((tm, 2 * d), lambda i: (i, 3)), pl.BlockSpec((8, d), lambda i: (0, 0))],
        out_shape=[shp] * 5 + [jax.ShapeDtypeStruct((s, 8 * d), BF16), jax.ShapeDtypeStruct((8, d), F32)],
        compiler_params=_params("arbitrary"),
    )(dx1, mo, g1, wout, pa, pb, p, p, wa, wb)


def _store_segments(outs, dp_out, sems, col_blocks):
    copies = [pltpu.make_async_copy(outs.at[k], dp_out.at[:, pl.ds(pl.multiple_of(cb * LANES, LANES), LANES)],
                                    sems.at[k]) for k, cb in enumerate(col_blocks)]
    for cp in copies:
        cp.start()
    for cp in copies:
        cp.wait()


def _conv_bwd(p, conv_w, dyb, dp, d):
    s = p.shape[0]
    nb = d // LANES
    rows_n = _conv_rows(s)

    def compute(cb_ref, cc_ref, cx_ref, w_ref, dy_ref, dcb_ref, dcc_ref, dcx_ref, dw_ref, us, ds):
        us[pl.ds(0, 8), :] = jnp.zeros((8, LANES), F32)
        ds[pl.ds(s, 8), :] = jnp.zeros((8, LANES), F32)

        def fill(r, _):
            rows = pl.ds(pl.multiple_of(r * rows_n, rows_n), rows_n)
            us[pl.ds(pl.multiple_of(r * rows_n + 8, 8), rows_n), :] = cc_ref[rows, :] * cx_ref[rows, :]
            ds[rows, :] = dy_ref[rows, :].astype(F32) * cb_ref[rows, :]
            return 0

        lax.fori_loop(0, s // rows_n, fill, 0)
        w = w_ref[...]

        def out(r, carry):
            dw0, dw1, dw2 = carry
            rows = pl.ds(pl.multiple_of(r * rows_n, rows_n), rows_n)
            ext = us[pl.ds(pl.multiple_of(r * rows_n, 8), rows_n + 8), :]
            u0, u1, u2 = ext[8:, :], pltpu.roll(ext, 1, 0)[8:, :], pltpu.roll(ext, 2, 0)[8:, :]
            cv = w[0:1, :] * u2 + w[1:2, :] * u1 + w[2:3, :] * u0
            dcb_ref[rows, :] = (dy_ref[rows, :].astype(F32) * cv).astype(BF16)
            nxt = ds[pl.ds(pl.multiple_of(r * rows_n, 8), rows_n + 8), :]
            e0 = nxt[:rows_n, :]
            e1 = pltpu.roll(nxt, rows_n + 7, 0)[:rows_n, :]
            e2 = pltpu.roll(nxt, rows_n + 6, 0)[:rows_n, :]
            du = w[2:3, :] * e0 + w[1:2, :] * e1 + w[0:1, :] * e2
            dcc_ref[rows, :] = (du * cx_ref[rows, :]).astype(BF16)
            dcx_ref[rows, :] = (du * cc_ref[rows, :]).astype(BF16)
            return (dw0 + jnp.sum(e0 * u2, axis=0, keepdims=True), dw1 + jnp.sum(e0 * u1, axis=0, keepdims=True),
                    dw2 + jnp.sum(e0 * u0, axis=0, keepdims=True))

        zero = jnp.zeros((1, LANES), F32)
        dw0, dw1, dw2 = lax.fori_loop(0, s // rows_n, out, (zero, zero, zero))
        dw_ref[...] = jnp.zeros_like(dw_ref)
        dw_ref[0:1, :] = dw0
        dw_ref[1:2, :] = dw1
        dw_ref[2:3, :] = dw2

    def body(cb_ref, cc_ref, cx_ref, w_ref, dy_ref, dp_in, dp_out, dw_ref, us, ds, outs, sems):
        del dp_in
        compute(cb_ref, cc_ref, cx_ref, w_ref, dy_ref, outs.at[0], outs.at[1], outs.at[2], dw_ref, us, ds)
        _store_segments(outs, dp_out, sems, [(3 + k) * nb + pl.program_id(0) for k in range(3)])

    def seg(k):
        return pl.BlockSpec((s, LANES), lambda b, k=k: (0, k * nb + b))

    return pl.pallas_call(
        body, name="conv_bwd", grid=(nb,),
        in_specs=[seg(3), seg(4), seg(5), pl.BlockSpec((3, LANES), lambda b: (0, b)),
                  pl.BlockSpec((s, LANES), lambda b: (0, b)), ANY_SPEC],
        out_specs=[ANY_SPEC, pl.BlockSpec((8, LANES), lambda b: (0, b))],
        out_shape=[jax.ShapeDtypeStruct(dp.shape, BF16), jax.ShapeDtypeStruct((8, d), F32)],
        input_output_aliases={5: 0},
        scratch_shapes=[pltpu.VMEM((s + 8, LANES), F32), pltpu.VMEM((s + 8, LANES), F32),
                        pltpu.VMEM((3, s, LANES), BF16), pltpu.SemaphoreType.DMA((3,))],
        compiler_params=_params("arbitrary"),
    )(p, p, p, conv_w, dyb, dp)


def _attn_bwd(p, qg2, kg2, dy, lt, dp, d):
    s = p.shape[0]
    n_pairs = d // LANES
    qsb = _tile(s, Q_SUPER_BWD)
    n_sub, n_sb, n_kb = qsb // Q_BLOCK, s // qsb, s // Q_BLOCK
    unroll = math.gcd(KEY_UNROLL, n_sub)
    chunk = _tile(s, 512)
    inv_sqrt = 1.0 / math.sqrt(HEAD_DIM)

    def compute(q_ref, k_ref, v_ref, qg_ref, kg_ref, dy_ref, lt_ref, dq_ref, dk_ref, dv_ref, dgain_ref,
                qs, k2, v2, dkt, dvt, qt, dyt, rem, gbef, dqa):
        low, causal, w4 = _attn_consts(True)

        def prep(r, _):
            rows = pl.ds(pl.multiple_of(r * chunk, chunk), chunk)
            qs[rows, :] = (_pair_norm(q_ref[rows, :], low)[0] * (qg_ref[...] * inv_sqrt)).astype(BF16)
            return 0

        lax.fori_loop(0, s // chunk, prep, 0)
        _fill_pair_blocks(k2, lambda rows: _pair_norm(k_ref[rows, :], low)[0] * kg_ref[...], low, n_kb)
        _fill_pair_blocks(v2, lambda rows: v_ref[rows, :], low, n_kb)

        def clear(b, _):
            dkt[b] = jnp.zeros((LANES, Q_BLOCK), F32)
            dvt[b] = jnp.zeros((LANES, Q_BLOCK), F32)
            return 0

        lax.fori_loop(0, n_kb, clear, 0)

        def step(sb, j, t0=0, diag_t=None):
            rows = pl.ds(pl.multiple_of(sb * qsb + t0 * Q_BLOCK, Q_BLOCK), (n_sub - t0) * Q_BLOCK)
            kj2, vj2 = k2[j], v2[j]
            z_both = _dot_nt(qs[rows, :], kj2)
            da_both = _dot_nt(dy_ref[rows, :], vj2)
            zls, cats = [], []
            for t in range(t0, n_sub):
                sub = slice((t - t0) * Q_BLOCK, (t - t0 + 1) * Q_BLOCK)
                for h in range(2):
                    z = z_both[sub, h * LANES:(h + 1) * LANES]
                    ln = _log_not(z)
                    if t == diag_t:
                        ln = jnp.where(causal, ln, 0.0)
                    zls.append(z + ln)
                    cats.append(_split_cat(ln))
            c2 = _dot(jnp.concatenate(cats, axis=0), w4)
            a_rows, gs, cats = [], [], []
            for t in range(t0, n_sub):
                sub = slice(t * Q_BLOCK, (t + 1) * Q_BLOCK)
                a_pair = []
                for h in range(2):
                    i = 2 * (t - t0) + h
                    tile = slice(i * Q_BLOCK, (i + 1) * Q_BLOCK)
                    left = rem[h, sub, :]
                    log_a = zls[i] + (left - c2[tile, :LANES])
                    if t == diag_t:
                        log_a = jnp.where(causal, log_a, -1e30)
                    a = jnp.exp(log_a)
                    rem[h, sub, :] = left - c2[tile, LANES:]
                    g = a * da_both[(t - t0) * Q_BLOCK:(t - t0 + 1) * Q_BLOCK, h * LANES:(h + 1) * LANES]
                    a_pair.append(a.astype(BF16))
                    gs.append(g)
                    cats.append(_split_cat(g))
                a_rows.append(jnp.concatenate(a_pair, axis=1))
            c2g = _dot(jnp.concatenate(cats, axis=0), w4)
            dz_rows = []
            for t in range(t0, n_sub):
                sub = slice(t * Q_BLOCK, (t + 1) * Q_BLOCK)
                dz_pair = []
                for h in range(2):
                    i = 2 * (t - t0) + h
                    tile = slice(i * Q_BLOCK, (i + 1) * Q_BLOCK)
                    before = gbef[h, sub, :]
                    dz = gs[i] - jnp.exp(zls[i]) * (before + c2g[tile, :LANES])
                    if t == diag_t:
                        dz = jnp.where(causal, dz, 0.0)
                    gbef[h, sub, :] = before + c2g[tile, LANES:]
                    dz_pair.append(dz.astype(BF16))
                dz_rows.append(jnp.concatenate(dz_pair, axis=1))
            a_both = jnp.concatenate(a_rows, axis=0)
            dz_both = jnp.concatenate(dz_rows, axis=0)
            used = slice(t0 * Q_BLOCK, qsb)
            dvt[j] += _dot(dyt[0, :, used], a_both[:, :LANES]) + _dot(dyt[1, :, used], a_both[:, LANES:])
            dkt[j] += _dot(qt[0, :, used], dz_both[:, :LANES]) + _dot(qt[1, :, used], dz_both[:, LANES:])
            dqa[used, :] += _dot(dz_both, kj2)

        def super_block(sb, dqg):
            rows_sb = pl.ds(pl.multiple_of(sb * qsb, qsb), qsb)
            total = lt_ref[rows_sb, :]
            other = pltpu.roll(total, HEAD_DIM, 1)
            rem[0] = jnp.where(low, total, other)
            rem[1] = jnp.where(low, other, total)
            gbef[...] = jnp.zeros_like(gbef)
            dqa[...] = jnp.zeros_like(dqa)
            qv = qs[rows_sb, :].astype(F32)
            dyv = dy_ref[rows_sb, :].astype(F32)
            qt[0] = jnp.where(low, qv, 0.0).T.astype(BF16)
            qt[1] = jnp.where(low, 0.0, qv).T.astype(BF16)
            dyt[0] = jnp.where(low, dyv, 0.0).T.astype(BF16)
            dyt[1] = jnp.where(low, 0.0, dyv).T.astype(BF16)

            def below(n, _):
                for u in range(unroll):
                    step(sb, unroll * n + u)
                return 0

            lax.fori_loop(0, sb * (n_sub // unroll), below, 0)
            for t in range(n_sub):
                step(sb, sb * n_sub + t, t0=t, diag_t=t)
            qhat, r = _pair_norm(q_ref[rows_sb, :], low)
            dqn = dqa[...]
            dqhat = dqn * (qg_ref[...] * inv_sqrt)
            dq_ref[rows_sb, :] = (r * (dqhat - qhat * _pair_mean(dqhat * qhat, low))).astype(BF16)
            return dqg + jnp.sum(dqn * qhat, axis=0, keepdims=True) * inv_sqrt

        dqg = lax.fori_loop(0, n_sb, super_block, jnp.zeros((1, LANES), F32))

        def finish(b, dkg):
            rows = pl.ds(pl.multiple_of(b * Q_BLOCK, Q_BLOCK), Q_BLOCK)
            khat, rk = _pair_norm(k_ref[rows, :], low)
            dkn = dkt[b].T
            dkhat = dkn * kg_ref[...]
            dk_ref[rows, :] = (rk * (dkhat - khat * _pair_mean(dkhat * khat, low))).astype(BF16)
            dv_ref[rows, :] = dvt[b].T.astype(BF16)
            return dkg + jnp.sum(dkn * khat, axis=0, keepdims=True)

        dkg = lax.fori_loop(0, n_kb, finish, jnp.zeros((1, LANES), F32))
        dgain_ref[...] = jnp.zeros_like(dgain_ref)
        dgain_ref[0:1, :] = dqg
        dgain_ref[1:2, :] = dkg

    def body(q_ref, k_ref, v_ref, qg_ref, kg_ref, dy_ref, lt_ref, dp_in, dp_out, dgain_ref, outs, sems, *scratch):
        del dp_in
        compute(q_ref, k_ref, v_ref, qg_ref, kg_ref, dy_ref, lt_ref, outs.at[0], outs.at[1], outs.at[2], dgain_ref,
                *scratch)
        _store_segments(outs, dp_out, sems, [k * n_pairs + pl.program_id(0) for k in range(3)])

    def seg(k):
        return pl.BlockSpec((s, LANES), lambda h, k=k: (0, k * n_pairs + h))

    vec = pl.BlockSpec((1, LANES), lambda h: (0, 0))
    col = pl.BlockSpec((s, LANES), lambda h: (0, h))
    return pl.pallas_call(
        body, name="attn_bwd", grid=(n_pairs,),
        in_specs=[seg(0), seg(1), seg(2), vec, vec, col, col, ANY_SPEC],
        out_specs=[ANY_SPEC, pl.BlockSpec((None, 8, LANES), lambda h: (h, 0, 0))],
        out_shape=[jax.ShapeDtypeStruct(dp.shape, BF16), jax.ShapeDtypeStruct((n_pairs, 8, LANES), F32)],
        input_output_aliases={7: 0},
        scratch_shapes=[pltpu.VMEM((3, s, LANES), BF16), pltpu.SemaphoreType.DMA((3,)), pltpu.VMEM((s, LANES), BF16)]
        + [pltpu.VMEM((n_kb, 2 * Q_BLOCK, LANES), BF16)] * 2
        + [pltpu.VMEM((n_kb, LANES, Q_BLOCK), F32)] * 2
        + [pltpu.VMEM((2, LANES, qsb), BF16)] * 2
        + [pltpu.VMEM((2, qsb, LANES), F32)] * 2 + [pltpu.VMEM((qsb, LANES), F32)],
        compiler_params=_params("arbitrary"),
    )(p, p, p, qg2, kg2, dy, lt, dp)


def _mm_in_bwd(dp, w_g):
    s = dp.shape[0]
    d, n4 = w_g.shape[-2:]
    tm = _tile(s, 512)

    def body(a_ref, w_ref, o_ref, acc):
        j = pl.program_id(1)

        @pl.when(j == 0)
        def _():
            acc[...] = jnp.zeros_like(acc)

        acc[...] += _dot_nt(a_ref[...], w_ref[...])

        @pl.when(j == N_CHIP - 1)
        def _():
            o_ref[...] = acc[...]

    return pl.pallas_call(
        body, name="mm_in_bwd", grid=(s // tm, N_CHIP),
        in_specs=[pl.BlockSpec((tm, n4), lambda i, j: (i, j)),
                  pl.BlockSpec((None, d, n4), lambda i, j: (j, 0, 0))],
        out_specs=pl.BlockSpec((tm, d), lambda i, j: (i, 0)), out_shape=jax.ShapeDtypeStruct((s, d), F32),
        scratch_shapes=[pltpu.VMEM((tm, d), F32)], compiler_params=_params("parallel", "arbitrary"),
    )(dp, w_g)


def _sum_adam(parts, w, m, v, name):
    n_l, r, c = w.shape
    tr = next((t for t in (256, 176, 128, 64, 32, 16) if r % t == 0 and t * c <= 256 * 1024), r)
    n_blk = r // tr

    def body(*refs):
        p_refs = refs[:n_l]
        w_ref, m_ref, v_ref, g_ref, dl_ref, nm_ref, nv_ref = refs[n_l:]
        for l in range(n_l):
            @pl.when(pl.program_id(0) == l)
            def _(p_ref=p_refs[l]):
                g = p_ref[0].astype(F32)
                for dev in range(1, N_DEV):
                    g = g + p_ref[dev].astype(F32)
                g_ref[...] = g
                delta, nm, nv = _adamw(w_ref[...], g, m_ref[...], v_ref[...])
                dl_ref[...] = delta
                nm_ref[...] = nm
                nv_ref[...] = nv

    def part_spec(l):
        return pl.BlockSpec((N_DEV, tr, c), lambda ll, i, l=l: (0, jnp.where(ll == l, i, jnp.where(ll < l, 0, n_blk - 1)), 0))

    wsp = pl.BlockSpec((None, tr, c), lambda l, i: (l, i, 0))
    shp = jax.ShapeDtypeStruct(w.shape, F32)
    return pl.pallas_call(
        body, name=name, grid=(n_l, n_blk),
        in_specs=[part_spec(l) for l in range(n_l)] + [wsp, wsp, wsp],
        out_specs=[wsp] * 4, out_shape=[shp] * 4, compiler_params=_params("arbitrary", "arbitrary"),
    )(*parts, w, m, v)


def _small_adam(parts, w, m, v):
    def body(p_ref, w_ref, m_ref, v_ref, g_ref, dl_ref, nm_ref, nv_ref):
        g = p_ref[0]
        for dev in range(1, N_DEV):
            g = g + p_ref[dev]
        g_ref[...] = g
        delta, nm, nv = _adamw(w_ref[...], g, m_ref[...], v_ref[...])
        dl_ref[...] = delta
        nm_ref[...] = nm
        nv_ref[...] = nv

    shp = jax.ShapeDtypeStruct(w.shape, F32)
    return pl.pallas_call(body, name="small_adam", in_specs=[VMEM_SPEC] * 4, out_specs=[VMEM_SPEC] * 4,
                          out_shape=[shp] * 4,
                          compiler_params=pltpu.CompilerParams(vmem_limit_bytes=VMEM_LIMIT_BYTES))(parts, w, m, v)


def _pack(vecs, mult=8 * LANES):
    flat = jnp.concatenate([a.reshape(-1).astype(F32) for a in vecs])
    pad = (-flat.shape[0]) % mult
    if pad:
        flat = jnp.concatenate([flat, jnp.zeros((pad,), F32)])
    return flat.reshape(8, -1)


def _unpack(flat, shapes):
    flat = flat.reshape(-1)
    out, off = [], 0
    for shp in shapes:
        n = math.prod(shp)
        out.append(flat[off:off + n].reshape(shp))
        off += n
    return out


BIG = ("win", "wa", "wb", "wo", "wg", "wu", "wd")
GRAD_GROUPS = (("wd", "wg", "wu"), ("wo", "wa", "wb"), ("win",))


def _local_step(x, target, mods, ln1_g, ln2_g, qg, kg, conv_w, weights, send_grads):
    s, d = x.shape
    n_l = mods.shape[0]
    saved = []
    h_in = x
    for l in range(n_l):
        sh1, sc1, g1, sh2, sc2, g2 = [mods[l, k * d:(k + 1) * d].reshape(1, d) for k in range(6)]
        qg2, kg2 = jnp.tile(qg[l:l + 1], (1, 2)), jnp.tile(kg[l:l + 1], (1, 2))
        h1 = _lnmod(h_in, ln1_g[l:l + 1], sc1, sh1)
        (win,), tie = weights(l, ("win",), h1)
        p = _mm_in(h1, win)
        ya, lt = _attn_fwd(p, qg2 + tie, kg2, d)
        yb = _conv_fwd(p, conv_w[l], d)
        (wa, wb, wo, wg, wu, wd), tie = weights(l, ("wa", "wb", "wo", "wg", "wu", "wd"), ya)
        wa, wb, wo = wa.reshape(d, d), wb.reshape(d, d), wo.reshape(d, d)
        merged, pa, pb = _branch(ya, yb, p, wa, wb, d)
        x1, mo = _out_proj(merged, wo, h_in, g1 + tie)
        h2 = _lnmod(x1, ln2_g[l:l + 1], sc2, sh2)
        gate, up, act = _ffn_up(h2, wg, wu)
        x2, f = _ffn_down(act, wd, x1, g2)
        saved.append(dict(x0=h_in, h1=h1, p=p, ya=ya, lt=lt, yb=yb, merged=merged, pa=pa, pb=pb, x1=x1, mo=mo,
                          h2=h2, gate=gate, up=up, act=act, f=f, win=win, wa=wa, wb=wb, wo=wo, wg=wg, wu=wu, wd=wd,
                          mod=(sh1, sc1, g1, sh2, sc2, g2), qg2=qg2, kg2=kg2))
        h_in = x2

    dx, loss_tile = _loss_head(h_in, target)

    small = [None] * n_l
    for l in reversed(range(n_l)):
        sv = saved[l]
        sh1, sc1, g1, sh2, sc2, g2 = sv["mod"]
        f4, n4, r4 = sv["wg"].shape[-1], sv["win"].shape[-1], d // N_CHIP
        hsp = lambda tk: ((tk, d), lambda j, k: (k, 0))
        fsp = lambda tk: ((None, tk, f4), lambda j, k: (j, k, 0))
        csp = lambda tk: ((tk, r4), lambda j, k: (k, j))
        dgate, dup, df, dg2 = _ffn_bwd1(dx, sv["f"], g2, sv["wd"], sv["gate"], sv["up"])
        g_wd = _mm_tn(sv["act"], df, fsp, hsp, (f4, d), "grad_wd")
        g_wg = _mm_tn(sv["h2"], dgate, hsp, fsp, (d, f4), "grad_wg")
        g_wu = _mm_tn(sv["h2"], dup, hsp, fsp, (d, f4), "grad_wu")
        tie = send_grads(l, dict(wd=g_wd, wg=g_wg, wu=g_wu))
        dh2 = _ffn_bwd2(dgate, dup, sv["wg"], sv["wu"])
        dx1, sums2 = _lnmod_bwd(sv["x1"], ln2_g[l:l + 1], sc2 + tie, dh2, dx)
        dmo, da, db, dya, dyb, dp, dg1 = _out_bwd(dx1, sv["mo"], g1, sv["wo"], sv["pa"], sv["pb"], sv["p"],
                                                        sv["wa"], sv["wb"], d)
        g_wo = _mm_tn(sv["merged"], dmo, csp, hsp, (r4, d), "grad_wo")
        g_wa = _mm_tn(sv["ya"], da, csp, hsp, (r4, d), "grad_wa")
        g_wb = _mm_tn(sv["yb"], db, csp, hsp, (r4, d), "grad_wb")
        tie = send_grads(l, dict(wo=g_wo, wa=g_wa, wb=g_wb))
        dp, dconv = _conv_bwd(sv["p"], conv_w[l] + tie, dyb, dp, d)
        dp, dgain = _attn_bwd(sv["p"], sv["qg2"], sv["kg2"], dya, sv["lt"], dp, d)
        g_win = _mm_tn(sv["h1"], dp, hsp, lambda tk: ((tk, n4), lambda j, k: (k, j)), (d, n4), "grad_win")
        tie = send_grads(l, dict(win=g_win))
        dh1 = _mm_in_bwd(dp, sv["win"])
        dx, sums1 = _lnmod_bwd(sv["x0"], ln1_g[l:l + 1], sc1 + tie, dh1, dx1)
        dgain = jnp.sum(dgain[:, 0:2, :], axis=0)
        dgain = dgain[:, :HEAD_DIM] + dgain[:, HEAD_DIM:]
        dmod = jnp.concatenate([sums1[0], sums1[1], dg1[0], sums2[0], sums2[1], dg2[0]])
        small[l] = dict(dmod=dmod, ln1=sums1[2], ln2=sums2[2], qg=dgain[0], kg=dgain[1], conv=dconv[0:3])
    return loss_tile, dx, small


def kernel(x, c, ada_w, ada_b, ln1_g, w_in, q_norm_g, k_norm_g, conv_w, w_branch_a, w_branch_b, w_out, ln2_g, w_ffn_gate, w_ffn_up, w_ffn_down, loss_target, m_ada_w, m_ada_b, m_ln1_g, m_w_in, m_q_norm_g, m_k_norm_g, m_conv_w, m_w_branch_a, m_w_branch_b, m_w_out, m_ln2_g, m_w_ffn_gate, m_w_ffn_up, m_w_ffn_down, v_ada_w, v_ada_b, v_ln1_g, v_w_in, v_q_norm_g, v_k_norm_g, v_conv_w, v_w_branch_a, v_w_branch_b, v_w_out, v_ln2_g, v_w_ffn_gate, v_w_ffn_up, v_w_ffn_down):
    n_l, d, a4 = ada_w.shape
    cw4 = conv_w.shape[-1]
    ix, iy, ic = lax.axis_index("x"), lax.axis_index("y"), lax.axis_index("c")
    chip = 2 * ix + iy
    me = 2 * chip + ic

    big_w = dict(win=w_in, wa=w_branch_a, wb=w_branch_b, wo=w_out, wg=w_ffn_gate, wu=w_ffn_up, wd=w_ffn_down)
    big_m = dict(win=m_w_in, wa=m_w_branch_a, wb=m_w_branch_b, wo=m_w_out, wg=m_w_ffn_gate, wu=m_w_ffn_up,
                 wd=m_w_ffn_down)
    big_v = dict(win=v_w_in, wa=v_w_branch_a, wb=v_w_branch_b, wo=v_w_out, wg=v_w_ffn_gate, wu=v_w_ffn_up,
                 wd=v_w_ffn_down)

    got = _gather8(_pack([c, conv_w])).reshape(N_DEV, -1)
    c_all = got[:, :d]
    conv_all = got[:, d:d + n_l * 3 * cw4].reshape(N_CHIP, 2, n_l, 3, cw4)[:, 0]
    conv_full = jnp.transpose(conv_all, (1, 2, 0, 3)).reshape(n_l, 3, N_CHIP * cw4)
    b_cols = lax.dynamic_slice_in_dim(ada_b, chip * a4, a4, axis=1).reshape(n_l, 1, a4)
    mod_cols = _ada_mod(c_all, ada_w, b_cols)
    mod_all = _gather8(_pack([mod_cols])).reshape(N_DEV, -1)[:, :n_l * N_DEV * a4]
    mod_all = mod_all.reshape(N_CHIP, 2, n_l, N_DEV, a4)[:, 0]
    mods = lax.dynamic_index_in_dim(mod_all, me, axis=2, keepdims=False)
    mods = jnp.transpose(mods, (1, 0, 2)).reshape(n_l, N_CHIP * a4)

    weight_groups = [(l, names) for l in range(n_l) for names in (("win",), ("wa", "wb", "wo", "wg", "wu", "wd"))]
    group_srcs = [[big_w[k][l].astype(BF16) for k in names] for l, names in weight_groups]
    mods, group_srcs = lax.optimization_barrier((mods, group_srcs))
    started_w = {}

    def start_weights(gi):
        l, names = weight_groups[gi]
        st = _split_start("weights_start_%d" % gi, _weight_copies, group_srcs[gi],
                          [(N_CHIP,) + sh.shape for sh in group_srcs[gi]], 3)
        for k in names:
            started_w[(l, k)] = [gi, names, st, None]
        return st[4]

    mods = mods + start_weights(0)[0, 0]

    def weights(l, names, after):
        entry, tie = started_w[(l, names[0])], jnp.zeros((), F32)
        if entry[3] is None:
            lands = _split_wait("weights_wait_%d" % entry[0], _weight_copies, entry[2], after)
            nxt = entry[0] + 1
            if nxt < len(weight_groups):
                lands, group_srcs[nxt] = lax.optimization_barrier((lands, group_srcs[nxt]))
                tie = start_weights(nxt)[0, 0]
            lands = [lax.dynamic_update_index_in_dim(land, own, chip, 0) for land, own in zip(lands, entry[2][2])]
            for k in entry[1]:
                started_w[(l, k)][3] = dict(zip(entry[1], lands))
        return [started_w[(l, k)][3][k] for k in names], tie

    started_g, held_back = [], []

    def start_grads(l, grads, copies=_grad_copies, sems_per=7):
        names = tuple(grads)
        st = _split_start("grads_start_%d" % len(started_g), copies, [grads[k] for k in names],
                          [(N_DEV,) + grads[k].shape[1:] for k in names], sems_per)
        started_g.append((l, names, st, copies))
        return st[4][0, 0]

    def send_grads(l, grads):
        if l == 0 and tuple(grads) == GRAD_GROUPS[-1]:
            held_back.append(grads)
            return jnp.zeros((), F32)
        return start_grads(l, grads)

    loss_tile, grad_x, small = _local_step(
        x[0], loss_target[0], mods, ln1_g, ln2_g, q_norm_g, k_norm_g, conv_full, weights, send_grads)

    sm_shapes = [(n_l, 6 * d), (n_l, d), (n_l, d), (n_l, HEAD_DIM), (n_l, HEAD_DIM), (n_l, 3, d), (1,)]
    vec = _pack([jnp.stack([small[l][k] for l in range(n_l)]) for k in ("dmod", "ln1", "ln2", "qg", "kg", "conv")]
                + [loss_tile[0, 0:1]])
    n_vec = vec.shape[1] * 8
    all_vec = _gather8(vec).reshape(N_DEV, n_vec)
    all_vec, held_back = lax.optimization_barrier((all_vec, held_back))
    tie = sum([start_grads(0, grads, _grad_copies_same_core, 4) for grads in held_back], jnp.zeros((), F32))
    per_dev = [_unpack(all_vec[dev], sm_shapes) for dev in range(N_DEV)]
    dmod_all = jnp.stack([pd[0] for pd in per_dev])
    dmod_cols = jnp.transpose(lax.dynamic_slice_in_dim(dmod_all, chip * a4, a4, axis=2), (1, 0, 2))
    ada_out = _ada_grad_adam(jnp.transpose(c_all) + tie, dmod_cols, ada_w, m_ada_w, v_ada_w)

    dev_parts = jnp.stack([
        _pack([pd[0], pd[1], pd[2], pd[3], pd[4], lax.dynamic_slice_in_dim(pd[5], chip * cw4, cw4, axis=2), pd[6]])
        for pd in per_dev])
    zero1 = jnp.zeros((1,), F32)
    sw = _pack([ada_b, ln1_g, ln2_g, q_norm_g, k_norm_g, conv_w, zero1])
    sm = _pack([m_ada_b, m_ln1_g, m_ln2_g, m_q_norm_g, m_k_norm_g, m_conv_w, zero1])
    sv = _pack([v_ada_b, v_ln1_g, v_ln2_g, v_q_norm_g, v_k_norm_g, v_conv_w, zero1 + 1.0])
    out_shapes = [(n_l, 6 * d), (n_l, d), (n_l, d), (n_l, HEAD_DIM), (n_l, HEAD_DIM), (n_l, 3, cw4), (1,)]
    sm_out = [_unpack(o, out_shapes) for o in _small_adam(dev_parts, sw, sm, sv)]
    loss = 0.5 * sm_out[0][6][0] / d

    big_out, after = {}, sm_out[0][0]
    for names in GRAD_GROUPS:
        got_parts = {}
        for gi, (l, sent, st, copies) in enumerate(started_g):
            if sent == names:
                parts = _split_wait("grads_wait_%d" % gi, copies, st, after)
                if copies is _grad_copies_same_core:
                    passed = _split_start_in_place("grads_pass_start_%d" % gi, _grad_pass_copies, parts, 3)
                    parts = _split_wait_in_place("grads_pass_wait_%d" % gi, _grad_pass_copies, passed, passed[3])
                for k, part, grad in zip(sent, parts, st[2]):
                    own = lax.dynamic_index_in_dim(grad, chip, 0, keepdims=False)
                    got_parts[(l, k)] = lax.dynamic_update_index_in_dim(part, own, me, 0)
        for k in names:
            big_out[k] = _sum_adam([got_parts[(l, k)] for l in range(n_l)], big_w[k], big_m[k], big_v[k],
                                   "sum_adam_" + k)
            after = big_out[k][0]

    outs = [loss, grad_x[None]]
    for kind in range(4):
        sm_k = sm_out[kind]
        outs += [ada_out[kind], sm_k[0], sm_k[1], big_out["win"][kind], sm_k[3], sm_k[4], sm_k[5],
                 big_out["wa"][kind], big_out["wb"][kind], big_out["wo"][kind], sm_k[2],
                 big_out["wg"][kind], big_out["wu"][kind], big_out["wd"][kind]]
    return tuple(outs)
```

```python
import math

import jax
import jax.numpy as jnp
from jax import lax
from jax.experimental import pallas as pl
from jax.experimental.pallas import tpu as pltpu

F32 = jnp.float32
BF16 = jnp.bfloat16
MESH_ID = pl.DeviceIdType.MESH

EPS = 1e-6
HEAD_DIM = 64
Q_BLOCK = 128
Q_SUPER = 1024
Q_SUPER_BWD = 1024
KEY_UNROLL = 4
LANES = 128
N_DEV = 8
N_CHIP = 4
VMEM_LIMIT_BYTES = 56 * 1024 * 1024

ADAM_LR = 0.001
ADAM_B1 = 0.9
ADAM_B2 = 0.999
ADAM_EPS = 1e-08
ADAM_WD = 0.01
ADAM_STEP = 10

HBM_SPEC = pl.BlockSpec(memory_space=pltpu.HBM)
ANY_SPEC = pl.BlockSpec(memory_space=pl.ANY)
SEM_SPEC = pl.BlockSpec(memory_space=pltpu.SEMAPHORE)
VMEM_SPEC = pl.BlockSpec(memory_space=pltpu.VMEM)
SIDE_EFFECT = pltpu.SideEffectType.DATAFLOW_SIDE_EFFECTING


def _params(*sem):
    return pltpu.CompilerParams(dimension_semantics=tuple(sem), vmem_limit_bytes=VMEM_LIMIT_BYTES)


def _tile(n, pref):
    return pref if n % pref == 0 else n


def _dot(a, b):
    return jnp.dot(a, b, preferred_element_type=F32)


def _dot_nt(a, b):
    return lax.dot_general(a, b, (((1,), (1,)), ((), ())), preferred_element_type=F32)


def _dot_tn(a, b):
    return lax.dot_general(a, b, (((0,), (0,)), ((), ())), preferred_element_type=F32)


def _adamw(w, g, m, v):
    m = ADAM_B1 * m + (1.0 - ADAM_B1) * g
    v = ADAM_B2 * v + (1.0 - ADAM_B2) * (g * g)
    m_hat = m / (1.0 - ADAM_B1 ** ADAM_STEP)
    v_hat = v / (1.0 - ADAM_B2 ** ADAM_STEP)
    delta = -ADAM_LR * (m_hat / (jnp.sqrt(v_hat) + ADAM_EPS) + ADAM_WD * w)
    return delta, m, v


def _hbm(a):
    return pltpu.with_memory_space_constraint(a, pltpu.HBM)


def _peer(x, y, c, k):
    return (1 - x if k & 4 else x, 1 - y if k & 2 else y, 1 - c if k & 1 else c)


def _gather8(v):
    rows_per, m = v.shape

    def body(v_ref, out_ref, send_sems, recv_sems, local_sem):
        x, y, c = lax.axis_index("x"), lax.axis_index("y"), lax.axis_index("c")

        def rows(p):
            return out_ref.at[pl.ds((4 * p[0] + 2 * p[1] + p[2]) * rows_per, rows_per), :]

        me = (x, y, c)
        mine = pltpu.make_async_copy(v_ref, rows(me), local_sem)
        mine.start()
        sends = []
        for k in range(1, N_DEV):
            cp = pltpu.make_async_remote_copy(
                src_ref=v_ref, dst_ref=rows(me), send_sem=send_sems.at[k - 1], recv_sem=recv_sems.at[k - 1],
                device_id=_peer(x, y, c, k), device_id_type=MESH_ID)
            cp.start()
            sends.append(cp)
        for k in range(1, N_DEV):
            pltpu.make_async_remote_copy(
                src_ref=v_ref, dst_ref=rows(_peer(x, y, c, k)), send_sem=send_sems.at[k - 1],
                recv_sem=recv_sems.at[k - 1], device_id=_peer(x, y, c, k), device_id_type=MESH_ID).wait_recv()
        for cp in sends:
            cp.wait_send()
        mine.wait()

    return pl.pallas_call(
        body, name="gather8",
        out_shape=jax.ShapeDtypeStruct((N_DEV * rows_per, m), v.dtype),
        in_specs=[VMEM_SPEC], out_specs=VMEM_SPEC,
        scratch_shapes=[pltpu.SemaphoreType.DMA((N_DEV - 1,)), pltpu.SemaphoreType.DMA((N_DEV - 1,)),
                        pltpu.SemaphoreType.DMA],
    )(v)


def _weight_copies(srcs, lands, send_sems, recv_sems):
    x, y, c = lax.axis_index("x"), lax.axis_index("y"), lax.axis_index("c")
    chips = [(1 - x, y), (x, 1 - y), (1 - x, 1 - y)]
    sends, recvs = [], []
    for a, (src, land) in enumerate(zip(srcs, lands)):
        for j, (px, py) in enumerate(chips):
            def copy(dst_block, a=a, j=j, px=px, py=py, src=src, land=land):
                return pltpu.make_async_remote_copy(
                    src_ref=src, dst_ref=land.at[dst_block], send_sem=send_sems.at[3 * a + j],
                    recv_sem=recv_sems.at[3 * a + j], device_id=(px, py, c), device_id_type=MESH_ID)
            sends.append(copy(2 * x + y))
            recvs.append(copy(2 * px + py))
    return sends, recvs


def _split_start(name, copies, srcs, land_shapes, sems_per_src):
    n = len(srcs)

    def body(*refs):
        sends, _ = copies(refs[:n], refs[n + 2:2 * n + 2], refs[n], refs[n + 1])
        for cp in sends:
            cp.start()
        token = refs[-1]
        token[...] = jnp.zeros_like(token)

    n_sems = sems_per_src * n
    outs = pl.pallas_call(
        body, name=name,
        out_shape=(pltpu.SemaphoreType.DMA((n_sems,)), pltpu.SemaphoreType.DMA((n_sems,)),
                   *[pltpu.HBM(shape, a.dtype) for a, shape in zip(srcs, land_shapes)],
                   jax.ShapeDtypeStruct((8, LANES), F32)),
        in_specs=[HBM_SPEC] * n, out_specs=(SEM_SPEC, SEM_SPEC, *[HBM_SPEC] * n, VMEM_SPEC),
        compiler_params=pltpu.CompilerParams(has_side_effects=SIDE_EFFECT),
    )(*[_hbm(a) for a in srcs])
    return outs[0], outs[1], list(srcs), list(outs[2:2 + n]), outs[-1]


def _split_wait(name, copies, started, after):
    send_sems, recv_sems, srcs, lands, _ = started
    n = len(srcs)

    def body(*refs):
        sends, recvs = copies(refs[:n], refs[n:2 * n], refs[2 * n], refs[2 * n + 1])
        for cp in sends:
            cp.wait_send()
        for cp in recvs:
            cp.wait_recv()

    return pl.pallas_call(
        body, name=name,
        out_shape=tuple(pltpu.HBM(a.shape, a.dtype) for a in lands),
        in_specs=[HBM_SPEC] * (2 * n) + [SEM_SPEC, SEM_SPEC, ANY_SPEC], out_specs=tuple([HBM_SPEC] * n),
        input_output_aliases={n + i: i for i in range(n)},
        compiler_params=pltpu.CompilerParams(has_side_effects=SIDE_EFFECT),
    )(*srcs, *lands, send_sems, recv_sems, after)


def _split_start_in_place(name, copies, bufs, sems_per_buf):
    n = len(bufs)

    def body(*refs):
        sends, _ = copies(refs[:n], refs[:n], refs[n], refs[n + 1])
        for cp in sends:
            cp.start()
        token = refs[-1]
        token[...] = jnp.zeros_like(token)

    n_sems = sems_per_buf * n
    outs = pl.pallas_call(
        body, name=name,
        out_shape=(pltpu.SemaphoreType.DMA((n_sems,)), pltpu.SemaphoreType.DMA((n_sems,)),
                   *[pltpu.HBM(a.shape, a.dtype) for a in bufs], jax.ShapeDtypeStruct((8, LANES), F32)),
        in_specs=[HBM_SPEC] * n, out_specs=(SEM_SPEC, SEM_SPEC, *[HBM_SPEC] * n, VMEM_SPEC),
        input_output_aliases={i: 2 + i for i in range(n)},
        compiler_params=pltpu.CompilerParams(has_side_effects=SIDE_EFFECT),
    )(*[_hbm(a) for a in bufs])
    return outs[0], outs[1], list(outs[2:2 + n]), outs[-1]


def _split_wait_in_place(name, copies, started, after):
    send_sems, recv_sems, bufs, _ = started
    n = len(bufs)

    def body(*refs):
        sends, recvs = copies(refs[:n], refs[:n], refs[n], refs[n + 1])
        for cp in sends:
            cp.wait_send()
        for cp in recvs:
            cp.wait_recv()

    return pl.pallas_call(
        body, name=name,
        out_shape=tuple(pltpu.HBM(a.shape, a.dtype) for a in bufs),
        in_specs=[HBM_SPEC] * n + [SEM_SPEC, SEM_SPEC, ANY_SPEC], out_specs=tuple([HBM_SPEC] * n),
        input_output_aliases={i: i for i in range(n)},
        compiler_params=pltpu.CompilerParams(has_side_effects=SIDE_EFFECT),
    )(*bufs, send_sems, recv_sems, after)


def _grad_copies(grads, parts, send_sems, recv_sems):
    x, y, c = lax.axis_index("x"), lax.axis_index("y"), lax.axis_index("c")
    chips = [(1 - x, y), (x, 1 - y), (1 - x, 1 - y)]
    my_slot = 4 * x + 2 * y + c
    sends, recvs = [], []
    for a, (grad, part) in enumerate(zip(grads, parts)):
        def copy(k, block, slot, to, a=a, grad=grad, part=part):
            return pltpu.make_async_remote_copy(
                src_ref=grad.at[block], dst_ref=part.at[slot], send_sem=send_sems.at[7 * a + k],
                recv_sem=recv_sems.at[7 * a + k], device_id=to, device_id_type=MESH_ID)
        sends.append(copy(0, 2 * x + y, my_slot, (x, y, 1 - c)))
        recvs.append(copy(0, 2 * x + y, 4 * x + 2 * y + (1 - c), (x, y, 1 - c)))
        for j, (px, py) in enumerate(chips):
            for other, pc in enumerate((c, 1 - c)):
                sends.append(copy(1 + 2 * j + other, 2 * px + py, my_slot, (px, py, pc)))
                recvs.append(copy(1 + 2 * j + other, 2 * x + y, 4 * px + 2 * py + pc, (px, py, pc)))
    return sends, recvs


def _grad_copies_same_core(grads, parts, send_sems, recv_sems):
    x, y, c = lax.axis_index("x"), lax.axis_index("y"), lax.axis_index("c")
    chips = [(1 - x, y), (x, 1 - y), (1 - x, 1 - y)]
    my_slot = 4 * x + 2 * y + c
    sends, recvs = [], []
    for a, (grad, part) in enumerate(zip(grads, parts)):
        def copy(k, block, slot, to, a=a, grad=grad, part=part):
            return pltpu.make_async_remote_copy(
                src_ref=grad.at[block], dst_ref=part.at[slot], send_sem=send_sems.at[4 * a + k],
                recv_sem=recv_sems.at[4 * a + k], device_id=to, device_id_type=MESH_ID)
        sends.append(copy(0, 2 * x + y, my_slot, (x, y, 1 - c)))
        recvs.append(copy(0, 2 * x + y, 4 * x + 2 * y + (1 - c), (x, y, 1 - c)))
        for j, (px, py) in enumerate(chips):
            sends.append(copy(1 + j, 2 * px + py, my_slot, (px, py, c)))
            recvs.append(copy(1 + j, 2 * x + y, 4 * px + 2 * py + c, (px, py, c)))
    return sends, recvs


def _grad_pass_copies(parts, same_parts, send_sems, recv_sems):
    del same_parts
    x, y, c = lax.axis_index("x"), lax.axis_index("y"), lax.axis_index("c")
    chips = [(1 - x, y), (x, 1 - y), (1 - x, 1 - y)]
    sends, recvs = [], []
    for a, part in enumerate(parts):
        for j, (px, py) in enumerate(chips):
            def copy(pc, a=a, j=j, px=px, py=py, part=part):
                slot = part.at[4 * px + 2 * py + pc]
                return pltpu.make_async_remote_copy(
                    src_ref=slot, dst_ref=slot, send_sem=send_sems.at[3 * a + j], recv_sem=recv_sems.at[3 * a + j],
                    device_id=(x, y, 1 - c), device_id_type=MESH_ID)
            sends.append(copy(c))
            recvs.append(copy(1 - c))
    return sends, recvs


def _ada_mod(c_all, ada_w, ada_b_cols):
    n_l, d, a4 = ada_w.shape
    tn = _tile(a4, 512)

    def body(c_ref, w_ref, b_ref, o_ref):
        cv = c_ref[...]
        ca = (cv * jax.nn.sigmoid(cv)).astype(BF16)
        o_ref[...] = _dot(ca, w_ref[...].astype(BF16)) + b_ref[...]

    return pl.pallas_call(
        body, name="ada_mod", grid=(n_l, a4 // tn),
        in_specs=[pl.BlockSpec((N_DEV, d), lambda l, j: (0, 0)),
                  pl.BlockSpec((None, d, tn), lambda l, j: (l, 0, j)),
                  pl.BlockSpec((None, 1, tn), lambda l, j: (l, 0, j))],
        out_specs=pl.BlockSpec((None, N_DEV, tn), lambda l, j: (l, 0, j)),
        out_shape=jax.ShapeDtypeStruct((n_l, N_DEV, a4), F32),
        compiler_params=_params("parallel", "parallel"),
    )(c_all, ada_w, ada_b_cols)


def _ada_grad_adam(c_all_t, dmod_cols, w, m, v):
    n_l, d, a4 = w.shape
    tn = _tile(a4, 512)

    def body(ct_ref, dm_ref, w_ref, m_ref, v_ref, g_ref, dl_ref, nm_ref, nv_ref):
        ct = ct_ref[...]
        ca = ct * jax.nn.sigmoid(ct)
        dm = dm_ref[...]
        g = ca[:, 0:1] * dm[0:1, :]
        for dev in range(1, N_DEV):
            g = g + ca[:, dev:dev + 1] * dm[dev:dev + 1, :]
        g_ref[...] = g
        delta, nm, nv = _adamw(w_ref[...], g, m_ref[...], v_ref[...])
        dl_ref[...] = delta
        nm_ref[...] = nm
        nv_ref[...] = nv

    wspec = pl.BlockSpec((None, d, tn), lambda l, j: (l, 0, j))
    shp = jax.ShapeDtypeStruct(w.shape, F32)
    return pl.pallas_call(
        body, name="ada_grad_adam", grid=(n_l, a4 // tn),
        in_specs=[pl.BlockSpec((d, N_DEV), lambda l, j: (0, 0)),
                  pl.BlockSpec((None, N_DEV, tn), lambda l, j: (l, 0, j)), wspec, wspec, wspec],
        out_specs=[wspec] * 4, out_shape=[shp] * 4,
        compiler_params=_params("parallel", "parallel"),
    )(c_all_t, dmod_cols, w, m, v)


def _lnmod(x, g, sc, sh):
    s, d = x.shape
    tm = _tile(s, 512)

    def body(x_ref, g_ref, sc_ref, sh_ref, h_ref):
        xv = x_ref[...]
        r = lax.rsqrt(jnp.mean(xv * xv, axis=-1, keepdims=True) + EPS)
        h_ref[...] = ((xv * r * g_ref[...]) * (1.0 + sc_ref[...]) + sh_ref[...]).astype(BF16)

    vec = pl.BlockSpec((1, d), lambda i: (0, 0))
    row = pl.BlockSpec((tm, d), lambda i: (i, 0))
    return pl.pallas_call(
        body, name="lnmod", grid=(s // tm,), in_specs=[row, vec, vec, vec], out_specs=row,
        out_shape=jax.ShapeDtypeStruct((s, d), BF16), compiler_params=_params("parallel"),
    )(x, g, sc, sh)


def _mm_in(h, w_g):
    s, d = h.shape
    n4 = w_g.shape[-1]
    tm = _tile(s, 512)

    def body(a_ref, b_ref, o_ref):
        o_ref[...] = _dot(a_ref[...], b_ref[...])

    return pl.pallas_call(
        body, name="mm_in", grid=(N_CHIP, s // tm),
        in_specs=[pl.BlockSpec((tm, d), lambda j, i: (i, 0)),
                  pl.BlockSpec((None, d, n4), lambda j, i: (j, 0, 0))],
        out_specs=pl.BlockSpec((tm, n4), lambda j, i: (i, j)),
        out_shape=jax.ShapeDtypeStruct((s, N_CHIP * n4), F32),
        compiler_params=_params("parallel", "parallel"),
    )(h, w_g)


def _pair_mean(x, low):
    lo = jnp.sum(jnp.where(low, x, 0.0), axis=-1, keepdims=True)
    hi = jnp.sum(jnp.where(low, 0.0, x), axis=-1, keepdims=True)
    return jnp.where(low, lo, hi) * (1.0 / HEAD_DIM)


def _pair_norm(x, low):
    r = lax.rsqrt(_pair_mean(x * x, low) + EPS)
    return x * r, r


def _log_not(z):
    return jnp.minimum(-z, 0.0) - jnp.log(1.0 + jnp.exp(-jnp.abs(z)))


def _attn_consts(inclusive):
    low = lax.broadcasted_iota(jnp.int32, (1, LANES), 1) < HEAD_DIM
    row = lax.broadcasted_iota(jnp.int32, (Q_BLOCK, Q_BLOCK), 0)
    col = lax.broadcasted_iota(jnp.int32, (Q_BLOCK, Q_BLOCK), 1)
    tri = (row <= col) if inclusive else (row > col)
    w2 = jnp.concatenate([tri.astype(BF16), jnp.ones((Q_BLOCK, Q_BLOCK), BF16)], axis=1)
    return low, col < row, jnp.concatenate([w2, w2], axis=0)


def _split_cat(v):
    hi = v.astype(BF16)
    return jnp.concatenate([hi, (v - hi.astype(F32)).astype(BF16)], axis=1)


def _fill_pair_blocks(dst, src_fn, low, n_kb):
    def fill(b, _):
        v = src_fn(pl.ds(pl.multiple_of(b * Q_BLOCK, Q_BLOCK), Q_BLOCK))
        dst[b, 0:Q_BLOCK, :] = jnp.where(low, v, 0.0).astype(BF16)
        dst[b, Q_BLOCK:2 * Q_BLOCK, :] = jnp.where(low, 0.0, v).astype(BF16)
        return 0

    lax.fori_loop(0, n_kb, fill, 0)


def _attn_fwd(p, qg2, kg2, d):
    s = p.shape[0]
    n_pairs = d // LANES
    qsb = _tile(s, Q_SUPER)
    n_sub, n_sb, n_kb = qsb // Q_BLOCK, s // qsb, s // Q_BLOCK
    unroll = math.gcd(KEY_UNROLL, n_sub)
    chunk = _tile(s, 512)
    inv_sqrt = 1.0 / math.sqrt(HEAD_DIM)

    def body(q_ref, k_ref, v_ref, qg_ref, kg_ref, o_ref, lt_ref, qs, k2, v2, run, acc):
        low, causal, w4 = _attn_consts(False)

        def prep(r, _):
            rows = pl.ds(pl.multiple_of(r * chunk, chunk), chunk)
            qs[rows, :] = (_pair_norm(q_ref[rows, :], low)[0] * (qg_ref[...] * inv_sqrt)).astype(BF16)
            return 0

        lax.fori_loop(0, s // chunk, prep, 0)
        _fill_pair_blocks(k2, lambda rows: _pair_norm(k_ref[rows, :], low)[0] * kg_ref[...], low, n_kb)
        _fill_pair_blocks(v2, lambda rows: v_ref[rows, :], low, n_kb)

        def step(sb, j, t0=0, diag_t=None):
            rows = pl.ds(pl.multiple_of(sb * qsb + t0 * Q_BLOCK, Q_BLOCK), (n_sub - t0) * Q_BLOCK)
            z_both = _dot_nt(qs[rows, :], k2[j])
            zls, cats = [], []
            for t in range(t0, n_sub):
                sub = slice((t - t0) * Q_BLOCK, (t - t0 + 1) * Q_BLOCK)
                for h in range(2):
                    z = z_both[sub, h * LANES:(h + 1) * LANES]
                    ln = _log_not(z)
                    if t == diag_t:
                        ln = jnp.where(causal, ln, 0.0)
                    zls.append(z + ln)
                    cats.append(_split_cat(ln))
            c2 = _dot(jnp.concatenate(cats, axis=0), w4)
            a_rows = []
            for t in range(t0, n_sub):
                sub = slice(t * Q_BLOCK, (t + 1) * Q_BLOCK)
                a_pair = []
                for h in range(2):
                    i = 2 * (t - t0) + h
                    tile = slice(i * Q_BLOCK, (i + 1) * Q_BLOCK)
                    later = run[h, sub, :]
                    log_a = zls[i] + c2[tile, :LANES] + later
                    if t == diag_t:
                        log_a = jnp.where(causal, log_a, -1e30)
                    a_pair.append(jnp.exp(log_a).astype(BF16))
                    run[h, sub, :] = later + c2[tile, LANES:]
                a_rows.append(jnp.concatenate(a_pair, axis=1))
            acc[t0 * Q_BLOCK:, :] += _dot(jnp.concatenate(a_rows, axis=0), v2[j])

        def super_block(sb, _):
            run[...] = jnp.zeros_like(run)
            acc[...] = jnp.zeros_like(acc)
            for t in reversed(range(n_sub)):
                step(sb, sb * n_sub + t, t0=t, diag_t=t)

            def below(n, _):
                for u in range(unroll):
                    step(sb, sb * n_sub - 1 - (unroll * n + u))
                return 0

            lax.fori_loop(0, sb * (n_sub // unroll), below, 0)
            rows_sb = pl.ds(pl.multiple_of(sb * qsb, qsb), qsb)
            o_ref[rows_sb, :] = acc[...].astype(BF16)
            lt_ref[rows_sb, :] = jnp.where(low, run[0], run[1])
            return 0

        lax.fori_loop(0, n_sb, super_block, 0)

    def seg(k):
        return pl.BlockSpec((s, LANES), lambda h, k=k: (0, k * n_pairs + h))

    vec = pl.BlockSpec((1, LANES), lambda h: (0, 0))
    out = pl.BlockSpec((s, LANES), lambda h: (0, h))
    return pl.pallas_call(
        body, name="attn_fwd", grid=(n_pairs,),
        in_specs=[seg(0), seg(1), seg(2), vec, vec], out_specs=[out, out],
        out_shape=[jax.ShapeDtypeStruct((s, d), BF16), jax.ShapeDtypeStruct((s, d), F32)],
        scratch_shapes=[pltpu.VMEM((s, LANES), BF16)] + [pltpu.VMEM((n_kb, 2 * Q_BLOCK, LANES), BF16)] * 2
        + [pltpu.VMEM((2, qsb, LANES), F32), pltpu.VMEM((qsb, LANES), F32)],
        compiler_params=_params("parallel"),
    )(p, p, p, qg2, kg2)


def _conv_rows(s):
    return _tile(s, 512)


def _conv_fwd(p, conv_w, d):
    s = p.shape[0]
    nb = d // LANES
    rows_n = _conv_rows(s)

    def body(cb_ref, cc_ref, cx_ref, w_ref, y_ref, us):
        us[pl.ds(0, 8), :] = jnp.zeros((8, LANES), F32)

        def fill(r, _):
            rows = pl.ds(pl.multiple_of(r * rows_n, rows_n), rows_n)
            us[pl.ds(pl.multiple_of(r * rows_n + 8, 8), rows_n), :] = cc_ref[rows, :] * cx_ref[rows, :]
            return 0

        lax.fori_loop(0, s // rows_n, fill, 0)
        w = w_ref[...]

        def out(r, _):
            rows = pl.ds(pl.multiple_of(r * rows_n, rows_n), rows_n)
            ext = us[pl.ds(pl.multiple_of(r * rows_n, 8), rows_n + 8), :]
            cv = (w[0:1, :] * pltpu.roll(ext, 2, 0)[8:, :] + w[1:2, :] * pltpu.roll(ext, 1, 0)[8:, :]
                  + w[2:3, :] * ext[8:, :])
            y_ref[rows, :] = (cb_ref[rows, :] * cv).astype(BF16)
            return 0

        lax.fori_loop(0, s // rows_n, out, 0)

    def seg(k):
        return pl.BlockSpec((s, LANES), lambda b, k=k: (0, k * nb + b))

    return pl.pallas_call(
        body, name="conv_fwd", grid=(nb,),
        in_specs=[seg(3), seg(4), seg(5), pl.BlockSpec((3, LANES), lambda b: (0, b))],
        out_specs=pl.BlockSpec((s, LANES), lambda b: (0, b)),
        out_shape=jax.ShapeDtypeStruct((s, d), BF16),
        scratch_shapes=[pltpu.VMEM((s + 8, LANES), F32)],
        compiler_params=_params("parallel"),
    )(p, p, p, conv_w)


def _branch(ya, yb, p, wa, wb, d):
    s = ya.shape[0]
    tm = _tile(s, 512)

    def body(ya_ref, yb_ref, ga_ref, gb_ref, wa_ref, wb_ref, m_ref, a_ref, b_ref):
        pa = _dot(ya_ref[...], wa_ref[...])
        pb = _dot(yb_ref[...], wb_ref[...])
        m_ref[...] = (jax.nn.sigmoid(ga_ref[...]) * pa + jax.nn.sigmoid(gb_ref[...]) * pb).astype(BF16)
        a_ref[...] = pa.astype(BF16)
        b_ref[...] = pb.astype(BF16)

    row = pl.BlockSpec((tm, d), lambda i: (i, 0))
    wsp = pl.BlockSpec((d, d), lambda i: (0, 0))
    shp = jax.ShapeDtypeStruct((s, d), BF16)
    return pl.pallas_call(
        body, name="branch", grid=(s // tm,),
        in_specs=[row, row, pl.BlockSpec((tm, d), lambda i: (i, 6)), pl.BlockSpec((tm, d), lambda i: (i, 7)), wsp, wsp],
        out_specs=[row, row, row], out_shape=[shp, shp, shp], compiler_params=_params("parallel"),
    )(ya, yb, p, p, wa, wb)


def _out_proj(merged, wout, x0, g1):
    s, d = x0.shape
    tm = _tile(s, 512)

    def body(m_ref, w_ref, x_ref, g_ref, x1_ref, mo_ref):
        mo = _dot(m_ref[...], w_ref[...])
        mo_ref[...] = mo
        x1_ref[...] = x_ref[...] + g_ref[...] * mo

    row = pl.BlockSpec((tm, d), lambda i: (i, 0))
    shp = jax.ShapeDtypeStruct((s, d), F32)
    return pl.pallas_call(
        body, name="out_proj", grid=(s // tm,),
        in_specs=[row, pl.BlockSpec((d, d), lambda i: (0, 0)), row, pl.BlockSpec((1, d), lambda i: (0, 0))],
        out_specs=[row, row], out_shape=[shp, shp], compiler_params=_params("parallel"),
    )(merged, wout, x0, g1)


def _ffn_up(h, wg_g, wu_g):
    s, d = h.shape
    f4 = wg_g.shape[-1]
    tm = _tile(s, 512)

    def body(h_ref, wg_ref, wu_ref, gate_ref, up_ref, act_ref):
        hv = h_ref[...]
        gt = _dot(hv, wg_ref[...])
        up = _dot(hv, wu_ref[...])
        gate_ref[...] = gt.astype(BF16)
        up_ref[...] = up.astype(BF16)
        act_ref[...] = (gt * jax.nn.sigmoid(gt) * up).astype(BF16)

    wsp = pl.BlockSpec((None, d, f4), lambda j, i: (j, 0, 0))
    osp = pl.BlockSpec((None, tm, f4), lambda j, i: (j, i, 0))
    shp = jax.ShapeDtypeStruct((N_CHIP, s, f4), BF16)
    return pl.pallas_call(
        body, name="ffn_up", grid=(N_CHIP, s // tm),
        in_specs=[pl.BlockSpec((tm, d), lambda j, i: (i, 0)), wsp, wsp],
        out_specs=[osp, osp, osp], out_shape=[shp, shp, shp], compiler_params=_params("parallel", "parallel"),
    )(h, wg_g, wu_g)


def _ffn_down(act, wd_g, x1, g2):
    s, d = x1.shape
    f4 = act.shape[-1]
    tm = _tile(s, 512)

    def body(a_ref, w_ref, x_ref, g_ref, x2_ref, f_ref, acc):
        j = pl.program_id(1)

        @pl.when(j == 0)
        def _():
            acc[...] = jnp.zeros_like(acc)

        acc[...] += _dot(a_ref[...], w_ref[...])

        @pl.when(j == N_CHIP - 1)
        def _():
            f = acc[...]
            f_ref[...] = f
            x2_ref[...] = x_ref[...] + g_ref[...] * f

    row = pl.BlockSpec((tm, d), lambda i, j: (i, 0))
    shp = jax.ShapeDtypeStruct((s, d), F32)
    return pl.pallas_call(
        body, name="ffn_down", grid=(s // tm, N_CHIP),
        in_specs=[pl.BlockSpec((None, tm, f4), lambda i, j: (j, i, 0)),
                  pl.BlockSpec((None, f4, d), lambda i, j: (j, 0, 0)),
                  row, pl.BlockSpec((1, d), lambda i, j: (0, 0))],
        out_specs=[row, row], out_shape=[shp, shp],
        scratch_shapes=[pltpu.VMEM((tm, d), F32)], compiler_params=_params("parallel", "arbitrary"),
    )(act, wd_g, x1, g2)


def _loss_head(y, target):
    s, d = y.shape
    tm = _tile(s, 512)
    n_steps = s // tm

    def body(y_ref, t_ref, dy_ref, l_ref, acc):
        i = pl.program_id(0)

        @pl.when(i == 0)
        def _():
            acc[...] = jnp.zeros_like(acc)

        err = y_ref[...] - t_ref[...]
        dy_ref[...] = err / d
        acc[...] += jnp.sum(err * err, axis=0, keepdims=True)

        @pl.when(i == n_steps - 1)
        def _():
            l_ref[...] = jnp.broadcast_to(jnp.sum(acc[...], axis=1, keepdims=True), (8, LANES))

    row = pl.BlockSpec((tm, d), lambda i: (i, 0))
    return pl.pallas_call(
        body, name="loss_head", grid=(n_steps,), in_specs=[row, row],
        out_specs=[row, pl.BlockSpec((8, LANES), lambda i: (0, 0))],
        out_shape=[jax.ShapeDtypeStruct((s, d), F32), jax.ShapeDtypeStruct((8, LANES), F32)],
        scratch_shapes=[pltpu.VMEM((1, d), F32)], compiler_params=_params("arbitrary"),
    )(y, target)


def _mm_tn(a, b, a_spec, b_spec, out_rc, name):
    r, c = out_rc
    s = a.shape[-2]
    tk = _tile(s, 512)
    nk = s // tk

    def body(a_ref, b_ref, o_ref, acc):
        k = pl.program_id(1)

        @pl.when(k == 0)
        def _():
            acc[...] = jnp.zeros_like(acc)

        acc[...] += _dot_tn(a_ref[...], b_ref[...])

        @pl.when(k == nk - 1)
        def _():
            o_ref[...] = acc[...].astype(BF16)

    return pl.pallas_call(
        body, name=name, grid=(N_CHIP, nk),
        in_specs=[pl.BlockSpec(*a_spec(tk)), pl.BlockSpec(*b_spec(tk))],
        out_specs=pl.BlockSpec((None, r, c), lambda j, k: (j, 0, 0)),
        out_shape=jax.ShapeDtypeStruct((N_CHIP, r, c), BF16),
        scratch_shapes=[pltpu.VMEM((r, c), F32)], compiler_params=_params("parallel", "arbitrary"),
    )(a, b)


def _ffn_bwd1(dx2, f, g2, wd_g, gate, up):
    s, d = dx2.shape
    f4 = gate.shape[-1]
    tm = _tile(s, 512)

    def body(dx_ref, f_ref, g_ref, w_ref, gate_ref, up_ref, dgate_ref, dup_ref, df_ref, dg_ref):
        i, j = pl.program_id(0), pl.program_id(1)

        @pl.when((i == 0) & (j == 0))
        def _():
            dg_ref[...] = jnp.zeros_like(dg_ref)

        dxv = dx_ref[...]
        df = (g_ref[...] * dxv).astype(BF16)

        @pl.when(j == 0)
        def _():
            df_ref[...] = df
            dg_ref[0:1, :] += jnp.sum(dxv * f_ref[...], axis=0, keepdims=True)

        da = _dot_nt(df, w_ref[...])
        gt = gate_ref[...].astype(F32)
        sg = jax.nn.sigmoid(gt)
        dup_ref[...] = (da * gt * sg).astype(BF16)
        dgate_ref[...] = (da * up_ref[...].astype(F32) * (sg * (1.0 + gt * (1.0 - sg)))).astype(BF16)

    row = pl.BlockSpec((tm, d), lambda i, j: (i, 0))
    hsp = pl.BlockSpec((None, tm, f4), lambda i, j: (j, i, 0))
    hshp = jax.ShapeDtypeStruct((N_CHIP, s, f4), BF16)
    return pl.pallas_call(
        body, name="ffn_bwd1", grid=(s // tm, N_CHIP),
        in_specs=[row, row, pl.BlockSpec((1, d), lambda i, j: (0, 0)),
                  pl.BlockSpec((None, f4, d), lambda i, j: (j, 0, 0)), hsp, hsp],
        out_specs=[hsp, hsp, row, pl.BlockSpec((8, d), lambda i, j: (0, 0))],
        out_shape=[hshp, hshp, jax.ShapeDtypeStruct((s, d), BF16), jax.ShapeDtypeStruct((8, d), F32)],
        compiler_params=_params("arbitrary", "arbitrary"),
    )(dx2, f, g2, wd_g, gate, up)


def _ffn_bwd2(dgate, dup, wg_g, wu_g):
    _, s, f4 = dgate.shape
    d = wg_g.shape[-2]
    tm = _tile(s, 512)

    def body(dg_ref, du_ref, wg_ref, wu_ref, o_ref, acc):
        j = pl.program_id(1)

        @pl.when(j == 0)
        def _():
            acc[...] = jnp.zeros_like(acc)

        acc[...] += _dot_nt(dg_ref[...], wg_ref[...]) + _dot_nt(du_ref[...], wu_ref[...])

        @pl.when(j == N_CHIP - 1)
        def _():
            o_ref[...] = acc[...]

    hsp = pl.BlockSpec((None, tm, f4), lambda i, j: (j, i, 0))
    wsp = pl.BlockSpec((None, d, f4), lambda i, j: (j, 0, 0))
    return pl.pallas_call(
        body, name="ffn_bwd2", grid=(s // tm, N_CHIP), in_specs=[hsp, hsp, wsp, wsp],
        out_specs=pl.BlockSpec((tm, d), lambda i, j: (i, 0)), out_shape=jax.ShapeDtypeStruct((s, d), F32),
        scratch_shapes=[pltpu.VMEM((tm, d), F32)], compiler_params=_params("parallel", "arbitrary"),
    )(dgate, dup, wg_g, wu_g)


def _lnmod_bwd(x, g, sc, dh, dres):
    s, d = x.shape
    tm = _tile(s, 512)

    def body(x_ref, g_ref, sc_ref, dh_ref, dr_ref, dx_ref, sums_ref):
        @pl.when(pl.program_id(0) == 0)
        def _():
            sums_ref[...] = jnp.zeros_like(sums_ref)

        xv, dhv, gv = x_ref[...], dh_ref[...], g_ref[...]
        r = lax.rsqrt(jnp.mean(xv * xv, axis=-1, keepdims=True) + EPS)
        n = xv * r
        one_sc = 1.0 + sc_ref[...]
        dt = dhv * one_sc
        sums_ref[0:1, :] += jnp.sum(dhv, axis=0, keepdims=True)
        sums_ref[1:2, :] += jnp.sum(dhv * (n * gv), axis=0, keepdims=True)
        sums_ref[2:3, :] += jnp.sum(dt * n, axis=0, keepdims=True)
        dn = dt * gv
        dx_ref[...] = dr_ref[...] + r * (dn - n * jnp.mean(dn * n, axis=-1, keepdims=True))

    vec = pl.BlockSpec((1, d), lambda i: (0, 0))
    row = pl.BlockSpec((tm, d), lambda i: (i, 0))
    return pl.pallas_call(
        body, name="lnmod_bwd", grid=(s // tm,), in_specs=[row, vec, vec, row, row],
        out_specs=[row, pl.BlockSpec((8, d), lambda i: (0, 0))],
        out_shape=[jax.ShapeDtypeStruct((s, d), F32), jax.ShapeDtypeStruct((8, d), F32)],
        compiler_params=_params("arbitrary"),
    )(x, g, sc, dh, dres)


def _out_bwd(dx1, mo, g1, wout, pa, pb, p, wa, wb, d):
    s = dx1.shape[0]
    tm = _tile(s, 256)

    def body(dx_ref, mo_ref, g_ref, wo_ref, pa_ref, pb_ref, ga_ref, gb_ref, wa_ref, wb_ref,
             dmo_ref, da_ref, db_ref, dya_ref, dyb_ref, dp_ref, dg_ref):
        @pl.when(pl.program_id(0) == 0)
        def _():
            dg_ref[...] = jnp.zeros_like(dg_ref)

        dxv = dx_ref[...]
        dg_ref[0:1, :] += jnp.sum(dxv * mo_ref[...], axis=0, keepdims=True)
        dmo = (g_ref[...] * dxv).astype(BF16)
        dmo_ref[...] = dmo
        dm = _dot_nt(dmo, wo_ref[...])
        sa, sb = jax.nn.sigmoid(ga_ref[...]), jax.nn.sigmoid(gb_ref[...])
        da = (dm * sa).astype(BF16)
        db = (dm * sb).astype(BF16)
        da_ref[...] = da
        db_ref[...] = db
        dp_ref[:, :d] = (dm * pa_ref[...].astype(F32) * (sa * (1.0 - sa))).astype(BF16)
        dp_ref[:, d:] = (dm * pb_ref[...].astype(F32) * (sb * (1.0 - sb))).astype(BF16)
        dya_ref[...] = _dot_nt(da, wa_ref[...]).astype(BF16)
        dyb_ref[...] = _dot_nt(db, wb_ref[...]).astype(BF16)

    row = pl.BlockSpec((tm, d), lambda i: (i, 0))
    wsp = pl.BlockSpec((d, d), lambda i: (0, 0))
    shp = jax.ShapeDtypeStruct((s, d), BF16)
    return pl.pallas_call(
        body, name="out_bwd", grid=(s // tm,),
        in_specs=[row, row, pl.BlockSpec((1, d), lambda i: (0, 0)), wsp, row, row,
                  pl.BlockSpec((tm, d), lambda i: (i, 6)), pl.BlockSpec((tm, d), lambda i: (i, 7)), wsp, wsp],
        out_specs=[row] * 5 + [pl.BlockSpec((tm, 2 * d), lambda i: (i, 3)), pl.BlockSpec((8, d), lambda i: (0, 0))],
        out_shape=[shp] * 5 + [jax.ShapeDtypeStruct((s, 8 * d), BF16), jax.ShapeDtypeStruct((8, d), F32)],
        compiler_params=_params("arbitrary"),
    )(dx1, mo, g1, wout, pa, pb, p, p, wa, wb)


def _store_segments(outs, dp_out, sems, col_blocks):
    copies = [pltpu.make_async_copy(outs.at[k], dp_out.at[:, pl.ds(pl.multiple_of(cb * LANES, LANES), LANES)],
                                    sems.at[k]) for k, cb in enumerate(col_blocks)]
    for cp in copies:
        cp.start()
    for cp in copies:
        cp.wait()


def _conv_bwd(p, conv_w, dyb, dp, d):
    s = p.shape[0]
    nb = d // LANES
    rows_n = _conv_rows(s)

    def compute(cb_ref, cc_ref, cx_ref, w_ref, dy_ref, dcb_ref, dcc_ref, dcx_ref, dw_ref, us, ds):
        us[pl.ds(0, 8), :] = jnp.zeros((8, LANES), F32)
        ds[pl.ds(s, 8), :] = jnp.zeros((8, LANES), F32)

        def fill(r, _):
            rows = pl.ds(pl.multiple_of(r * rows_n, rows_n), rows_n)
            us[pl.ds(pl.multiple_of(r * rows_n + 8, 8), rows_n), :] = cc_ref[rows, :] * cx_ref[rows, :]
            ds[rows, :] = dy_ref[rows, :].astype(F32) * cb_ref[rows, :]
            return 0

        lax.fori_loop(0, s // rows_n, fill, 0)
        w = w_ref[...]

        def out(r, carry):
            dw0, dw1, dw2 = carry
            rows = pl.ds(pl.multiple_of(r * rows_n, rows_n), rows_n)
            ext = us[pl.ds(pl.multiple_of(r * rows_n, 8), rows_n + 8), :]
            u0, u1, u2 = ext[8:, :], pltpu.roll(ext, 1, 0)[8:, :], pltpu.roll(ext, 2, 0)[8:, :]
            cv = w[0:1, :] * u2 + w[1:2, :] * u1 + w[2:3, :] * u0
            dcb_ref[rows, :] = (dy_ref[rows, :].astype(F32) * cv).astype(BF16)
            nxt = ds[pl.ds(pl.multiple_of(r * rows_n, 8), rows_n + 8), :]
            e0 = nxt[:rows_n, :]
            e1 = pltpu.roll(nxt, rows_n + 7, 0)[:rows_n, :]
            e2 = pltpu.roll(nxt, rows_n + 6, 0)[:rows_n, :]
            du = w[2:3, :] * e0 + w[1:2, :] * e1 + w[0:1, :] * e2
            dcc_ref[rows, :] = (du * cx_ref[rows, :]).astype(BF16)
            dcx_ref[rows, :] = (du * cc_ref[rows, :]).astype(BF16)
            return (dw0 + jnp.sum(e0 * u2, axis=0, keepdims=True), dw1 + jnp.sum(e0 * u1, axis=0, keepdims=True),
                    dw2 + jnp.sum(e0 * u0, axis=0, keepdims=True))

        zero = jnp.zeros((1, LANES), F32)
        dw0, dw1, dw2 = lax.fori_loop(0, s // rows_n, out, (zero, zero, zero))
        dw_ref[...] = jnp.zeros_like(dw_ref)
        dw_ref[0:1, :] = dw0
        dw_ref[1:2, :] = dw1
        dw_ref[2:3, :] = dw2

    def body(cb_ref, cc_ref, cx_ref, w_ref, dy_ref, dp_in, dp_out, dw_ref, us, ds, outs, sems):
        del dp_in
        compute(cb_ref, cc_ref, cx_ref, w_ref, dy_ref, outs.at[0], outs.at[1], outs.at[2], dw_ref, us, ds)
        _store_segments(outs, dp_out, sems, [(3 + k) * nb + pl.program_id(0) for k in range(3)])

    def seg(k):
        return pl.BlockSpec((s, LANES), lambda b, k=k: (0, k * nb + b))

    return pl.pallas_call(
        body, name="conv_bwd", grid=(nb,),
        in_specs=[seg(3), seg(4), seg(5), pl.BlockSpec((3, LANES), lambda b: (0, b)),
                  pl.BlockSpec((s, LANES), lambda b: (0, b)), ANY_SPEC],
        out_specs=[ANY_SPEC, pl.BlockSpec((8, LANES), lambda b: (0, b))],
        out_shape=[jax.ShapeDtypeStruct(dp.shape, BF16), jax.ShapeDtypeStruct((8, d), F32)],
        input_output_aliases={5: 0},
        scratch_shapes=[pltpu.VMEM((s + 8, LANES), F32), pltpu.VMEM((s + 8, LANES), F32),
                        pltpu.VMEM((3, s, LANES), BF16), pltpu.SemaphoreType.DMA((3,))],
        compiler_params=_params("arbitrary"),
    )(p, p, p, conv_w, dyb, dp)


def _attn_bwd(p, qg2, kg2, dy, lt, dp, d):
    s = p.shape[0]
    n_pairs = d // LANES
    qsb = _tile(s, Q_SUPER_BWD)
    n_sub, n_sb, n_kb = qsb // Q_BLOCK, s // qsb, s // Q_BLOCK
    unroll = math.gcd(KEY_UNROLL, n_sub)
    chunk = _tile(s, 512)
    inv_sqrt = 1.0 / math.sqrt(HEAD_DIM)

    def compute(q_ref, k_ref, v_ref, qg_ref, kg_ref, dy_ref, lt_ref, dq_ref, dk_ref, dv_ref, dgain_ref,
                qs, k2, v2, dkt, dvt, qt, dyt, rem, gbef, dqa):
        low, causal, w4 = _attn_consts(True)

        def prep(r, _):
            rows = pl.ds(pl.multiple_of(r * chunk, chunk), chunk)
            qs[rows, :] = (_pair_norm(q_ref[rows, :], low)[0] * (qg_ref[...] * inv_sqrt)).astype(BF16)
            return 0

        lax.fori_loop(0, s // chunk, prep, 0)
        _fill_pair_blocks(k2, lambda rows: _pair_norm(k_ref[rows, :], low)[0] * kg_ref[...], low, n_kb)
        _fill_pair_blocks(v2, lambda rows: v_ref[rows, :], low, n_kb)

        def clear(b, _):
            dkt[b] = jnp.zeros((LANES, Q_BLOCK), F32)
            dvt[b] = jnp.zeros((LANES, Q_BLOCK), F32)
            return 0

        lax.fori_loop(0, n_kb, clear, 0)

        def step(sb, j, t0=0, diag_t=None):
            rows = pl.ds(pl.multiple_of(sb * qsb + t0 * Q_BLOCK, Q_BLOCK), (n_sub - t0) * Q_BLOCK)
            kj2, vj2 = k2[j], v2[j]
            z_both = _dot_nt(qs[rows, :], kj2)
            da_both = _dot_nt(dy_ref[rows, :], vj2)
            zls, cats = [], []
            for t in range(t0, n_sub):
                sub = slice((t - t0) * Q_BLOCK, (t - t0 + 1) * Q_BLOCK)
                for h in range(2):
                    z = z_both[sub, h * LANES:(h + 1) * LANES]
                    ln = _log_not(z)
                    if t == diag_t:
                        ln = jnp.where(causal, ln, 0.0)
                    zls.append(z + ln)
                    cats.append(_split_cat(ln))
            c2 = _dot(jnp.concatenate(cats, axis=0), w4)
            a_rows, gs, cats = [], [], []
            for t in range(t0, n_sub):
                sub = slice(t * Q_BLOCK, (t + 1) * Q_BLOCK)
                a_pair = []
                for h in range(2):
                    i = 2 * (t - t0) + h
                    tile = slice(i * Q_BLOCK, (i + 1) * Q_BLOCK)
                    left = rem[h, sub, :]
                    log_a = zls[i] + (left - c2[tile, :LANES])
                    if t == diag_t:
                        log_a = jnp.where(causal, log_a, -1e30)
                    a = jnp.exp(log_a)
                    rem[h, sub, :] = left - c2[tile, LANES:]
                    g = a * da_both[(t - t0) * Q_BLOCK:(t - t0 + 1) * Q_BLOCK, h * LANES:(h + 1) * LANES]
                    a_pair.append(a.astype(BF16))
                    gs.append(g)
                    cats.append(_split_cat(g))
                a_rows.append(jnp.concatenate(a_pair, axis=1))
            c2g = _dot(jnp.concatenate(cats, axis=0), w4)
            dz_rows = []
            for t in range(t0, n_sub):
                sub = slice(t * Q_BLOCK, (t + 1) * Q_BLOCK)
                dz_pair = []
                for h in range(2):
                    i = 2 * (t - t0) + h
                    tile = slice(i * Q_BLOCK, (i + 1) * Q_BLOCK)
                    before = gbef[h, sub, :]
                    dz = gs[i] - jnp.exp(zls[i]) * (before + c2g[tile, :LANES])
                    if t == diag_t:
                        dz = jnp.where(causal, dz, 0.0)
                    gbef[h, sub, :] = before + c2g[tile, LANES:]
                    dz_pair.append(dz.astype(BF16))
                dz_rows.append(jnp.concatenate(dz_pair, axis=1))
            a_both = jnp.concatenate(a_rows, axis=0)
            dz_both = jnp.concatenate(dz_rows, axis=0)
            used = slice(t0 * Q_BLOCK, qsb)
            dvt[j] += _dot(dyt[0, :, used], a_both[:, :LANES]) + _dot(dyt[1, :, used], a_both[:, LANES:])
            dkt[j] += _dot(qt[0, :, used], dz_both[:, :LANES]) + _dot(qt[1, :, used], dz_both[:, LANES:])
            dqa[used, :] += _dot(dz_both, kj2)

        def super_block(sb, dqg):
            rows_sb = pl.ds(pl.multiple_of(sb * qsb, qsb), qsb)
            total = lt_ref[rows_sb, :]
            other = pltpu.roll(total, HEAD_DIM, 1)
            rem[0] = jnp.where(low, total, other)
            rem[1] = jnp.where(low, other, total)
            gbef[...] = jnp.zeros_like(gbef)
            dqa[...] = jnp.zeros_like(dqa)
            qv = qs[rows_sb, :].astype(F32)
            dyv = dy_ref[rows_sb, :].astype(F32)
            qt[0] = jnp.where(low, qv, 0.0).T.astype(BF16)
            qt[1] = jnp.where(low, 0.0, qv).T.astype(BF16)
            dyt[0] = jnp.where(low, dyv, 0.0).T.astype(BF16)
            dyt[1] = jnp.where(low, 0.0, dyv).T.astype(BF16)

            def below(n, _):
                for u in range(unroll):
                    step(sb, unroll * n + u)
                return 0

            lax.fori_loop(0, sb * (n_sub // unroll), below, 0)
            for t in range(n_sub):
                step(sb, sb * n_sub + t, t0=t, diag_t=t)
            qhat, r = _pair_norm(q_ref[rows_sb, :], low)
            dqn = dqa[...]
            dqhat = dqn * (qg_ref[...] * inv_sqrt)
            dq_ref[rows_sb, :] = (r * (dqhat - qhat * _pair_mean(dqhat * qhat, low))).astype(BF16)
            return dqg + jnp.sum(dqn * qhat, axis=0, keepdims=True) * inv_sqrt

        dqg = lax.fori_loop(0, n_sb, super_block, jnp.zeros((1, LANES), F32))

        def finish(b, dkg):
            rows = pl.ds(pl.multiple_of(b * Q_BLOCK, Q_BLOCK), Q_BLOCK)
            khat, rk = _pair_norm(k_ref[rows, :], low)
            dkn = dkt[b].T
            dkhat = dkn * kg_ref[...]
            dk_ref[rows, :] = (rk * (dkhat - khat * _pair_mean(dkhat * khat, low))).astype(BF16)
            dv_ref[rows, :] = dvt[b].T.astype(BF16)
            return dkg + jnp.sum(dkn * khat, axis=0, keepdims=True)

        dkg = lax.fori_loop(0, n_kb, finish, jnp.zeros((1, LANES), F32))
        dgain_ref[...] = jnp.zeros_like(dgain_ref)
        dgain_ref[0:1, :] = dqg
        dgain_ref[1:2, :] = dkg

    def body(q_ref, k_ref, v_ref, qg_ref, kg_ref, dy_ref, lt_ref, dp_in, dp_out, dgain_ref, outs, sems, *scratch):
        del dp_in
        compute(q_ref, k_ref, v_ref, qg_ref, kg_ref, dy_ref, lt_ref, outs.at[0], outs.at[1], outs.at[2], dgain_ref,
                *scratch)
        _store_segments(outs, dp_out, sems, [k * n_pairs + pl.program_id(0) for k in range(3)])

    def seg(k):
        return pl.BlockSpec((s, LANES), lambda h, k=k: (0, k * n_pairs + h))

    vec = pl.BlockSpec((1, LANES), lambda h: (0, 0))
    col = pl.BlockSpec((s, LANES), lambda h: (0, h))
    return pl.pallas_call(
        body, name="attn_bwd", grid=(n_pairs,),
        in_specs=[seg(0), seg(1), seg(2), vec, vec, col, col, ANY_SPEC],
        out_specs=[ANY_SPEC, pl.BlockSpec((None, 8, LANES), lambda h: (h, 0, 0))],
        out_shape=[jax.ShapeDtypeStruct(dp.shape, BF16), jax.ShapeDtypeStruct((n_pairs, 8, LANES), F32)],
        input_output_aliases={7: 0},
        scratch_shapes=[pltpu.VMEM((3, s, LANES), BF16), pltpu.SemaphoreType.DMA((3,)), pltpu.VMEM((s, LANES), BF16)]
        + [pltpu.VMEM((n_kb, 2 * Q_BLOCK, LANES), BF16)] * 2
        + [pltpu.VMEM((n_kb, LANES, Q_BLOCK), F32)] * 2
        + [pltpu.VMEM((2, LANES, qsb), BF16)] * 2
        + [pltpu.VMEM((2, qsb, LANES), F32)] * 2 + [pltpu.VMEM((qsb, LANES), F32)],
        compiler_params=_params("arbitrary"),
    )(p, p, p, qg2, kg2, dy, lt, dp)


def _mm_in_bwd(dp, w_g):
    s = dp.shape[0]
    d, n4 = w_g.shape[-2:]
    tm = _tile(s, 512)

    def body(a_ref, w_ref, o_ref, acc):
        j = pl.program_id(1)

        @pl.when(j == 0)
        def _():
            acc[...] = jnp.zeros_like(acc)

        acc[...] += _dot_nt(a_ref[...], w_ref[...])

        @pl.when(j == N_CHIP - 1)
        def _():
            o_ref[...] = acc[...]

    return pl.pallas_call(
        body, name="mm_in_bwd", grid=(s // tm, N_CHIP),
        in_specs=[pl.BlockSpec((tm, n4), lambda i, j: (i, j)),
                  pl.BlockSpec((None, d, n4), lambda i, j: (j, 0, 0))],
        out_specs=pl.BlockSpec((tm, d), lambda i, j: (i, 0)), out_shape=jax.ShapeDtypeStruct((s, d), F32),
        scratch_shapes=[pltpu.VMEM((tm, d), F32)], compiler_params=_params("parallel", "arbitrary"),
    )(dp, w_g)


def _sum_adam(parts, w, m, v, name):
    n_l, r, c = w.shape
    tr = next((t for t in (256, 176, 128, 64, 32, 16) if r % t == 0 and t * c <= 256 * 1024), r)
    n_blk = r // tr

    def body(*refs):
        p_refs = refs[:n_l]
        w_ref, m_ref, v_ref, g_ref, dl_ref, nm_ref, nv_ref = refs[n_l:]
        for l in range(n_l):
            @pl.when(pl.program_id(0) == l)
            def _(p_ref=p_refs[l]):
                g = p_ref[0].astype(F32)
                for dev in range(1, N_DEV):
                    g = g + p_ref[dev].astype(F32)
                g_ref[...] = g
                delta, nm, nv = _adamw(w_ref[...], g, m_ref[...], v_ref[...])
                dl_ref[...] = delta
                nm_ref[...] = nm
                nv_ref[...] = nv

    def part_spec(l):
        return pl.BlockSpec((N_DEV, tr, c), lambda ll, i, l=l: (0, jnp.where(ll == l, i, jnp.where(ll < l, 0, n_blk - 1)), 0))

    wsp = pl.BlockSpec((None, tr, c), lambda l, i: (l, i, 0))
    shp = jax.ShapeDtypeStruct(w.shape, F32)
    return pl.pallas_call(
        body, name=name, grid=(n_l, n_blk),
        in_specs=[part_spec(l) for l in range(n_l)] + [wsp, wsp, wsp],
        out_specs=[wsp] * 4, out_shape=[shp] * 4, compiler_params=_params("arbitrary", "arbitrary"),
    )(*parts, w, m, v)


def _small_adam(parts, w, m, v):
    def body(p_ref, w_ref, m_ref, v_ref, g_ref, dl_ref, nm_ref, nv_ref):
        g = p_ref[0]
        for dev in range(1, N_DEV):
            g = g + p_ref[dev]
        g_ref[...] = g
        delta, nm, nv = _adamw(w_ref[...], g, m_ref[...], v_ref[...])
        dl_ref[...] = delta
        nm_ref[...] = nm
        nv_ref[...] = nv

    shp = jax.ShapeDtypeStruct(w.shape, F32)
    return pl.pallas_call(body, name="small_adam", in_specs=[VMEM_SPEC] * 4, out_specs=[VMEM_SPEC] * 4,
                          out_shape=[shp] * 4,
                          compiler_params=pltpu.CompilerParams(vmem_limit_bytes=VMEM_LIMIT_BYTES))(parts, w, m, v)


def _pack(vecs, mult=8 * LANES):
    flat = jnp.concatenate([a.reshape(-1).astype(F32) for a in vecs])
    pad = (-flat.shape[0]) % mult
    if pad:
        flat = jnp.concatenate([flat, jnp.zeros((pad,), F32)])
    return flat.reshape(8, -1)


def _unpack(flat, shapes):
    flat = flat.reshape(-1)
    out, off = [], 0
    for shp in shapes:
        n = math.prod(shp)
        out.append(flat[off:off + n].reshape(shp))
        off += n
    return out


BIG = ("win", "wa", "wb", "wo", "wg", "wu", "wd")
GRAD_GROUPS = (("wd", "wg", "wu"), ("wo", "wa", "wb"), ("win",))


def _local_step(x, target, mods, ln1_g, ln2_g, qg, kg, conv_w, weights, send_grads):
    s, d = x.shape
    n_l = mods.shape[0]
    saved = []
    h_in = x
    for l in range(n_l):
        sh1, sc1, g1, sh2, sc2, g2 = [mods[l, k * d:(k + 1) * d].reshape(1, d) for k in range(6)]
        qg2, kg2 = jnp.tile(qg[l:l + 1], (1, 2)), jnp.tile(kg[l:l + 1], (1, 2))
        h1 = _lnmod(h_in, ln1_g[l:l + 1], sc1, sh1)
        (win,), tie = weights(l, ("win",), h1)
        p = _mm_in(h1, win)
        ya, lt = _attn_fwd(p, qg2 + tie, kg2, d)
        yb = _conv_fwd(p, conv_w[l], d)
        (wa, wb, wo, wg, wu, wd), tie = weights(l, ("wa", "wb", "wo", "wg", "wu", "wd"), ya)
        wa, wb, wo = wa.reshape(d, d), wb.reshape(d, d), wo.reshape(d, d)
        merged, pa, pb = _branch(ya, yb, p, wa, wb, d)
        x1, mo = _out_proj(merged, wo, h_in, g1 + tie)
        h2 = _lnmod(x1, ln2_g[l:l + 1], sc2, sh2)
        gate, up, act = _ffn_up(h2, wg, wu)
        x2, f = _ffn_down(act, wd, x1, g2)
        saved.append(dict(x0=h_in, h1=h1, p=p, ya=ya, lt=lt, yb=yb, merged=merged, pa=pa, pb=pb, x1=x1, mo=mo,
                          h2=h2, gate=gate, up=up, act=act, f=f, win=win, wa=wa, wb=wb, wo=wo, wg=wg, wu=wu, wd=wd,
                          mod=(sh1, sc1, g1, sh2, sc2, g2), qg2=qg2, kg2=kg2))
        h_in = x2

    dx, loss_tile = _loss_head(h_in, target)

    small = [None] * n_l
    for l in reversed(range(n_l)):
        sv = saved[l]
        sh1, sc1, g1, sh2, sc2, g2 = sv["mod"]
        f4, n4, r4 = sv["wg"].shape[-1], sv["win"].shape[-1], d // N_CHIP
        hsp = lambda tk: ((tk, d), lambda j, k: (k, 0))
        fsp = lambda tk: ((None, tk, f4), lambda j, k: (j, k, 0))
        csp = lambda tk: ((tk, r4), lambda j, k: (k, j))
        dgate, dup, df, dg2 = _ffn_bwd1(dx, sv["f"], g2, sv["wd"], sv["gate"], sv["up"])
        g_wd = _mm_tn(sv["act"], df, fsp, hsp, (f4, d), "grad_wd")
        g_wg = _mm_tn(dgate, sv["h2"], fsp, hsp, (f4, d), "grad_wg")
        g_wu = _mm_tn(dup, sv["h2"], fsp, hsp, (f4, d), "grad_wu")
        tie = send_grads(l, dict(wd=g_wd, wg=g_wg, wu=g_wu))
        dh2 = _ffn_bwd2(dgate, dup, sv["wg"], sv["wu"])
        dx1, sums2 = _lnmod_bwd(sv["x1"], ln2_g[l:l + 1], sc2 + tie, dh2, dx)
        dmo, da, db, dya, dyb, dp, dg1 = _out_bwd(dx1, sv["mo"], g1, sv["wo"], sv["pa"], sv["pb"], sv["p"],
                                                        sv["wa"], sv["wb"], d)
        g_wo = _mm_tn(sv["merged"], dmo, csp, hsp, (r4, d), "grad_wo")
        g_wa = _mm_tn(sv["ya"], da, csp, hsp, (r4, d), "grad_wa")
        g_wb = _mm_tn(sv["yb"], db, csp, hsp, (r4, d), "grad_wb")
        tie = send_grads(l, dict(wo=g_wo, wa=g_wa, wb=g_wb))
        dp, dconv = _conv_bwd(sv["p"], conv_w[l] + tie, dyb, dp, d)
        dp, dgain = _attn_bwd(sv["p"], sv["qg2"], sv["kg2"], dya, sv["lt"], dp, d)
        g_win = _mm_tn(sv["h1"], dp, hsp, lambda tk: ((tk, n4), lambda j, k: (k, j)), (d, n4), "grad_win")
        tie = send_grads(l, dict(win=g_win))
        dh1 = _mm_in_bwd(dp, sv["win"])
        dx, sums1 = _lnmod_bwd(sv["x0"], ln1_g[l:l + 1], sc1 + tie, dh1, dx1)
        dgain = jnp.sum(dgain[:, 0:2, :], axis=0)
        dgain = dgain[:, :HEAD_DIM] + dgain[:, HEAD_DIM:]
        dmod = jnp.concatenate([sums1[0], sums1[1], dg1[0], sums2[0], sums2[1], dg2[0]])
        small[l] = dict(dmod=dmod, ln1=sums1[2], ln2=sums2[2], qg=dgain[0], kg=dgain[1], conv=dconv[0:3])
    return loss_tile, dx, small


def kernel(x, c, ada_w, ada_b, ln1_g, w_in, q_norm_g, k_norm_g, conv_w, w_branch_a, w_branch_b, w_out, ln2_g, w_ffn_gate, w_ffn_up, w_ffn_down, loss_target, m_ada_w, m_ada_b, m_ln1_g, m_w_in, m_q_norm_g, m_k_norm_g, m_conv_w, m_w_branch_a, m_w_branch_b, m_w_out, m_ln2_g, m_w_ffn_gate, m_w_ffn_up, m_w_ffn_down, v_ada_w, v_ada_b, v_ln1_g, v_w_in, v_q_norm_g, v_k_norm_g, v_conv_w, v_w_branch_a, v_w_branch_b, v_w_out, v_ln2_g, v_w_ffn_gate, v_w_ffn_up, v_w_ffn_down):
    n_l, d, a4 = ada_w.shape
    cw4 = conv_w.shape[-1]
    ix, iy, ic = lax.axis_index("x"), lax.axis_index("y"), lax.axis_index("c")
    chip = 2 * ix + iy
    me = 2 * chip + ic

    big_w = dict(win=w_in, wa=w_branch_a, wb=w_branch_b, wo=w_out, wg=w_ffn_gate, wu=w_ffn_up, wd=w_ffn_down)
    big_m = dict(win=m_w_in, wa=m_w_branch_a, wb=m_w_branch_b, wo=m_w_out, wg=m_w_ffn_gate, wu=m_w_ffn_up,
                 wd=m_w_ffn_down)
    big_v = dict(win=v_w_in, wa=v_w_branch_a, wb=v_w_branch_b, wo=v_w_out, wg=v_w_ffn_gate, wu=v_w_ffn_up,
                 wd=v_w_ffn_down)

    def adam_view(a, k):
        return jnp.swapaxes(a, 1, 2) if k in ("wg", "wu") else a

    got = _gather8(_pack([c, conv_w])).reshape(N_DEV, -1)

    weight_groups = [(l, names) for l in range(n_l) for names in (("win",), ("wa", "wb", "wo", "wg", "wu", "wd"))]
    group_srcs = [[big_w[k][l].astype(BF16) for k in names] for l, names in weight_groups]
    started_w = {}

    def start_weights(gi):
        l, names = weight_groups[gi]
        st = _split_start("weights_start_%d" % gi, _weight_copies, group_srcs[gi],
                          [(N_CHIP,) + sh.shape for sh in group_srcs[gi]], 3)
        for k in names:
            started_w[(l, k)] = [gi, names, st, None]
        return st[4]

    got, group_srcs[0] = lax.optimization_barrier((got, group_srcs[0]))
    tie = start_weights(0)[0, 0]
    c_all = got[:, :d]
    conv_all = got[:, d:d + n_l * 3 * cw4].reshape(N_CHIP, 2, n_l, 3, cw4)[:, 0]
    conv_full = jnp.transpose(conv_all, (1, 2, 0, 3)).reshape(n_l, 3, N_CHIP * cw4)
    b_cols = lax.dynamic_slice_in_dim(ada_b, chip * a4, a4, axis=1).reshape(n_l, 1, a4)
    mod_cols = _ada_mod(c_all, ada_w, b_cols + tie)
    mod_all = _gather8(_pack([mod_cols])).reshape(N_DEV, -1)[:, :n_l * N_DEV * a4]
    mod_all = mod_all.reshape(N_CHIP, 2, n_l, N_DEV, a4)[:, 0]
    mods = lax.dynamic_index_in_dim(mod_all, me, axis=2, keepdims=False)
    mods = jnp.transpose(mods, (1, 0, 2)).reshape(n_l, N_CHIP * a4)

    def weights(l, names, after):
        entry, tie = started_w[(l, names[0])], jnp.zeros((), F32)
        if entry[3] is None:
            lands = _split_wait("weights_wait_%d" % entry[0], _weight_copies, entry[2], after)
            nxt = entry[0] + 1
            if nxt < len(weight_groups):
                lands, group_srcs[nxt] = lax.optimization_barrier((lands, group_srcs[nxt]))
                tie = start_weights(nxt)[0, 0]
            lands = [lax.dynamic_update_index_in_dim(land, own, chip, 0) for land, own in zip(lands, entry[2][2])]
            for k in entry[1]:
                started_w[(l, k)][3] = dict(zip(entry[1], lands))
        return [started_w[(l, k)][3][k] for k in names], tie

    started_g, held_back = [], []

    def start_grads(l, grads, copies=_grad_copies, sems_per=7):
        names = tuple(grads)
        st = _split_start("grads_start_%d" % len(started_g), copies, [grads[k] for k in names],
                          [(N_DEV,) + grads[k].shape[1:] for k in names], sems_per)
        started_g.append((l, names, st, copies))
        return st[4][0, 0]

    def send_grads(l, grads):
        if l == 0 and tuple(grads) == GRAD_GROUPS[-1]:
            held_back.append(grads)
            return jnp.zeros((), F32)
        return start_grads(l, grads)

    loss_tile, grad_x, small = _local_step(
        x[0], loss_target[0], mods, ln1_g, ln2_g, q_norm_g, k_norm_g, conv_full, weights, send_grads)

    sm_shapes = [(n_l, 6 * d), (n_l, d), (n_l, d), (n_l, HEAD_DIM), (n_l, HEAD_DIM), (n_l, 3, d), (1,)]
    vec = _pack([jnp.stack([small[l][k] for l in range(n_l)]) for k in ("dmod", "ln1", "ln2", "qg", "kg", "conv")]
                + [loss_tile[0, 0:1]])
    n_vec = vec.shape[1] * 8
    all_vec = _gather8(vec).reshape(N_DEV, n_vec)
    all_vec, held_back = lax.optimization_barrier((all_vec, held_back))
    tie = sum([start_grads(0, grads, _grad_copies_same_core, 4) for grads in held_back], jnp.zeros((), F32))
    per_dev = [_unpack(all_vec[dev], sm_shapes) for dev in range(N_DEV)]
    dmod_all = jnp.stack([pd[0] for pd in per_dev])
    dmod_cols = jnp.transpose(lax.dynamic_slice_in_dim(dmod_all, chip * a4, a4, axis=2), (1, 0, 2))
    ada_out = _ada_grad_adam(jnp.transpose(c_all) + tie, dmod_cols, ada_w, m_ada_w, v_ada_w)

    dev_parts = jnp.stack([
        _pack([pd[0], pd[1], pd[2], pd[3], pd[4], lax.dynamic_slice_in_dim(pd[5], chip * cw4, cw4, axis=2), pd[6]])
        for pd in per_dev])
    zero1 = jnp.zeros((1,), F32)
    sw = _pack([ada_b, ln1_g, ln2_g, q_norm_g, k_norm_g, conv_w, zero1])
    sm = _pack([m_ada_b, m_ln1_g, m_ln2_g, m_q_norm_g, m_k_norm_g, m_conv_w, zero1])
    sv = _pack([v_ada_b, v_ln1_g, v_ln2_g, v_q_norm_g, v_k_norm_g, v_conv_w, zero1 + 1.0])
    out_shapes = [(n_l, 6 * d), (n_l, d), (n_l, d), (n_l, HEAD_DIM), (n_l, HEAD_DIM), (n_l, 3, cw4), (1,)]
    sm_out = [_unpack(o, out_shapes) for o in _small_adam(dev_parts, sw, sm, sv)]
    loss = 0.5 * sm_out[0][6][0] / d

    after = jnp.full((8, LANES), tie + sm_out[0][0][0, 0] + ada_out[0][0, 0, 0])
    big_out = {}
    for names in GRAD_GROUPS:
        got_parts = {}
        for gi, (l, sent, st, copies) in enumerate(started_g):
            if sent == names:
                parts = _split_wait("grads_wait_%d" % gi, copies, st, after)
                if copies is _grad_copies_same_core:
                    passed = _split_start_in_place("grads_pass_start_%d" % gi, _grad_pass_copies, parts, 3)
                    parts = _split_wait_in_place("grads_pass_wait_%d" % gi, _grad_pass_copies, passed, passed[3])
                for k, part, grad in zip(sent, parts, st[2]):
                    own = lax.dynamic_index_in_dim(grad, chip, 0, keepdims=False)
                    got_parts[(l, k)] = lax.dynamic_update_index_in_dim(part, own, me, 0)
        for k in names:
            res = _sum_adam([got_parts[(l, k)] for l in range(n_l)], adam_view(big_w[k], k), adam_view(big_m[k], k),
                            adam_view(big_v[k], k), "sum_adam_" + k)
            after = res[0]
            big_out[k] = [adam_view(r, k) for r in res]

    outs = [loss, grad_x[None]]
    for kind in range(4):
        sm_k = sm_out[kind]
        outs += [ada_out[kind], sm_k[0], sm_k[1], big_out["win"][kind], sm_k[3], sm_k[4], sm_k[5],
                 big_out["wa"][kind], big_out["wb"][kind], big_out["wo"][kind], sm_k[2],
                 big_out["wg"][kind], big_out["wu"][kind], big_out["wd"][kind]]
    return tuple(outs)
```

```python
import math

import jax
import jax.numpy as jnp
from jax import lax
from jax.experimental import pallas as pl
from jax.experimental.pallas import tpu as pltpu

F32 = jnp.float32
BF16 = jnp.bfloat16
MESH_ID = pl.DeviceIdType.MESH

EPS = 1e-6
HEAD_DIM = 64
Q_BLOCK = 128
Q_SUPER = 1024
Q_SUPER_BWD = 1024
KEY_UNROLL = 4
LANES = 128
N_DEV = 8
N_CHIP = 4
VMEM_LIMIT_BYTES = 56 * 1024 * 1024

ADAM_LR = 0.001
ADAM_B1 = 0.9
ADAM_B2 = 0.999
ADAM_EPS = 1e-08
ADAM_WD = 0.01
ADAM_STEP = 10

HBM_SPEC = pl.BlockSpec(memory_space=pltpu.HBM)
ANY_SPEC = pl.BlockSpec(memory_space=pl.ANY)
SEM_SPEC = pl.BlockSpec(memory_space=pltpu.SEMAPHORE)
VMEM_SPEC = pl.BlockSpec(memory_space=pltpu.VMEM)
SIDE_EFFECT = pltpu.SideEffectType.DATAFLOW_SIDE_EFFECTING


def _params(*sem):
    return pltpu.CompilerParams(dimension_semantics=tuple(sem), vmem_limit_bytes=VMEM_LIMIT_BYTES)


def _tile(n, pref):
    return pref if n % pref == 0 else n


def _dot(a, b):
    return jnp.dot(a, b, preferred_element_type=F32)


def _dot_nt(a, b):
    return lax.dot_general(a, b, (((1,), (1,)), ((), ())), preferred_element_type=F32)


def _dot_tn(a, b):
    return lax.dot_general(a, b, (((0,), (0,)), ((), ())), preferred_element_type=F32)


def _adamw(w, g, m, v):
    m = ADAM_B1 * m + (1.0 - ADAM_B1) * g
    v = ADAM_B2 * v + (1.0 - ADAM_B2) * (g * g)
    m_hat = m / (1.0 - ADAM_B1 ** ADAM_STEP)
    v_hat = v / (1.0 - ADAM_B2 ** ADAM_STEP)
    delta = -ADAM_LR * (m_hat / (jnp.sqrt(v_hat) + ADAM_EPS) + ADAM_WD * w)
    return delta, m, v


def _hbm(a):
    return pltpu.with_memory_space_constraint(a, pltpu.HBM)


def _peer(x, y, c, k):
    return (1 - x if k & 4 else x, 1 - y if k & 2 else y, 1 - c if k & 1 else c)


def _gather8(v):
    rows_per, m = v.shape

    def body(v_ref, out_ref, send_sems, recv_sems, local_sem):
        x, y, c = lax.axis_index("x"), lax.axis_index("y"), lax.axis_index("c")

        def rows(p):
            return out_ref.at[pl.ds((4 * p[0] + 2 * p[1] + p[2]) * rows_per, rows_per), :]

        me = (x, y, c)
        mine = pltpu.make_async_copy(v_ref, rows(me), local_sem)
        mine.start()
        sends = []
        for k in range(1, N_DEV):
            cp = pltpu.make_async_remote_copy(
                src_ref=v_ref, dst_ref=rows(me), send_sem=send_sems.at[k - 1], recv_sem=recv_sems.at[k - 1],
                device_id=_peer(x, y, c, k), device_id_type=MESH_ID)
            cp.start()
            sends.append(cp)
        for k in range(1, N_DEV):
            pltpu.make_async_remote_copy(
                src_ref=v_ref, dst_ref=rows(_peer(x, y, c, k)), send_sem=send_sems.at[k - 1],
                recv_sem=recv_sems.at[k - 1], device_id=_peer(x, y, c, k), device_id_type=MESH_ID).wait_recv()
        for cp in sends:
            cp.wait_send()
        mine.wait()

    return pl.pallas_call(
        body, name="gather8",
        out_shape=jax.ShapeDtypeStruct((N_DEV * rows_per, m), v.dtype),
        in_specs=[VMEM_SPEC], out_specs=VMEM_SPEC,
        scratch_shapes=[pltpu.SemaphoreType.DMA((N_DEV - 1,)), pltpu.SemaphoreType.DMA((N_DEV - 1,)),
                        pltpu.SemaphoreType.DMA],
    )(v)


def _weight_copies(srcs, lands, send_sems, recv_sems):
    x, y, c = lax.axis_index("x"), lax.axis_index("y"), lax.axis_index("c")
    chips = [(1 - x, y), (x, 1 - y), (1 - x, 1 - y)]
    sends, recvs = [], []
    for a, (src, land) in enumerate(zip(srcs, lands)):
        for j, (px, py) in enumerate(chips):
            def copy(dst_block, a=a, j=j, px=px, py=py, src=src, land=land):
                return pltpu.make_async_remote_copy(
                    src_ref=src, dst_ref=land.at[dst_block], send_sem=send_sems.at[3 * a + j],
                    recv_sem=recv_sems.at[3 * a + j], device_id=(px, py, c), device_id_type=MESH_ID)
            sends.append(copy(2 * x + y))
            recvs.append(copy(2 * px + py))
    return sends, recvs


def _weight_half_copies(srcs, lands, send_sems, recv_sems):
    x, y, c = lax.axis_index("x"), lax.axis_index("y"), lax.axis_index("c")
    chips = [(1 - x, y), (x, 1 - y), (1 - x, 1 - y)]
    sends, recvs = [], []
    for a, (src, land) in enumerate(zip(srcs, lands)):
        half = src.shape[0] // 2
        rows = pl.ds(c * half, half)
        for j, (px, py) in enumerate(chips):
            def copy(dst_block, a=a, j=j, px=px, py=py, src=src, land=land, rows=rows):
                return pltpu.make_async_remote_copy(
                    src_ref=src.at[rows], dst_ref=land.at[dst_block, rows], send_sem=send_sems.at[3 * a + j],
                    recv_sem=recv_sems.at[3 * a + j], device_id=(px, py, c), device_id_type=MESH_ID)
            sends.append(copy(2 * x + y))
            recvs.append(copy(2 * px + py))
    return sends, recvs


def _weight_half_pass(lands, same_lands, send_sems, recv_sems):
    del same_lands
    x, y, c = lax.axis_index("x"), lax.axis_index("y"), lax.axis_index("c")
    chips = [(1 - x, y), (x, 1 - y), (1 - x, 1 - y)]
    sends, recvs = [], []
    for a, land in enumerate(lands):
        half = land.shape[1] // 2
        for j, (px, py) in enumerate(chips):
            def copy(pc, a=a, j=j, px=px, py=py, land=land, half=half):
                part = land.at[2 * px + py, pl.ds(pc * half, half)]
                return pltpu.make_async_remote_copy(
                    src_ref=part, dst_ref=part, send_sem=send_sems.at[3 * a + j], recv_sem=recv_sems.at[3 * a + j],
                    device_id=(x, y, 1 - c), device_id_type=MESH_ID)
            sends.append(copy(c))
            recvs.append(copy(1 - c))
    return sends, recvs


def _split_start(name, copies, srcs, land_shapes, sems_per_src):
    n = len(srcs)

    def body(*refs):
        sends, _ = copies(refs[:n], refs[n + 2:2 * n + 2], refs[n], refs[n + 1])
        for cp in sends:
            cp.start()
        token = refs[-1]
        token[...] = jnp.zeros_like(token)

    n_sems = sems_per_src * n
    outs = pl.pallas_call(
        body, name=name,
        out_shape=(pltpu.SemaphoreType.DMA((n_sems,)), pltpu.SemaphoreType.DMA((n_sems,)),
                   *[pltpu.HBM(shape, a.dtype) for a, shape in zip(srcs, land_shapes)],
                   jax.ShapeDtypeStruct((8, LANES), F32)),
        in_specs=[HBM_SPEC] * n, out_specs=(SEM_SPEC, SEM_SPEC, *[HBM_SPEC] * n, VMEM_SPEC),
        compiler_params=pltpu.CompilerParams(has_side_effects=SIDE_EFFECT),
    )(*[_hbm(a) for a in srcs])
    return outs[0], outs[1], list(srcs), list(outs[2:2 + n]), outs[-1]


def _split_wait(name, copies, started, after):
    send_sems, recv_sems, srcs, lands, _ = started
    n = len(srcs)

    def body(*refs):
        sends, recvs = copies(refs[:n], refs[n:2 * n], refs[2 * n], refs[2 * n + 1])
        for cp in sends:
            cp.wait_send()
        for cp in recvs:
            cp.wait_recv()

    return pl.pallas_call(
        body, name=name,
        out_shape=tuple(pltpu.HBM(a.shape, a.dtype) for a in lands),
        in_specs=[HBM_SPEC] * (2 * n) + [SEM_SPEC, SEM_SPEC, ANY_SPEC], out_specs=tuple([HBM_SPEC] * n),
        input_output_aliases={n + i: i for i in range(n)},
        compiler_params=pltpu.CompilerParams(has_side_effects=SIDE_EFFECT),
    )(*srcs, *lands, send_sems, recv_sems, after)


def _split_start_in_place(name, copies, bufs, sems_per_buf):
    n = len(bufs)

    def body(*refs):
        sends, _ = copies(refs[:n], refs[:n], refs[n], refs[n + 1])
        for cp in sends:
            cp.start()
        token = refs[-1]
        token[...] = jnp.zeros_like(token)

    n_sems = sems_per_buf * n
    outs = pl.pallas_call(
        body, name=name,
        out_shape=(pltpu.SemaphoreType.DMA((n_sems,)), pltpu.SemaphoreType.DMA((n_sems,)),
                   *[pltpu.HBM(a.shape, a.dtype) for a in bufs], jax.ShapeDtypeStruct((8, LANES), F32)),
        in_specs=[HBM_SPEC] * n, out_specs=(SEM_SPEC, SEM_SPEC, *[HBM_SPEC] * n, VMEM_SPEC),
        input_output_aliases={i: 2 + i for i in range(n)},
        compiler_params=pltpu.CompilerParams(has_side_effects=SIDE_EFFECT),
    )(*[_hbm(a) for a in bufs])
    return outs[0], outs[1], list(outs[2:2 + n]), outs[-1]


def _split_wait_in_place(name, copies, started, after):
    send_sems, recv_sems, bufs, _ = started
    n = len(bufs)

    def body(*refs):
        sends, recvs = copies(refs[:n], refs[:n], refs[n], refs[n + 1])
        for cp in sends:
            cp.wait_send()
        for cp in recvs:
            cp.wait_recv()

    return pl.pallas_call(
        body, name=name,
        out_shape=tuple(pltpu.HBM(a.shape, a.dtype) for a in bufs),
        in_specs=[HBM_SPEC] * n + [SEM_SPEC, SEM_SPEC, ANY_SPEC], out_specs=tuple([HBM_SPEC] * n),
        input_output_aliases={i: i for i in range(n)},
        compiler_params=pltpu.CompilerParams(has_side_effects=SIDE_EFFECT),
    )(*bufs, send_sems, recv_sems, after)


def _grad_copies(grads, parts, send_sems, recv_sems):
    x, y, c = lax.axis_index("x"), lax.axis_index("y"), lax.axis_index("c")
    chips = [(1 - x, y), (x, 1 - y), (1 - x, 1 - y)]
    my_slot = 4 * x + 2 * y + c
    sends, recvs = [], []
    for a, (grad, part) in enumerate(zip(grads, parts)):
        def copy(k, block, slot, to, a=a, grad=grad, part=part):
            return pltpu.make_async_remote_copy(
                src_ref=grad.at[block], dst_ref=part.at[slot], send_sem=send_sems.at[7 * a + k],
                recv_sem=recv_sems.at[7 * a + k], device_id=to, device_id_type=MESH_ID)
        sends.append(copy(0, 2 * x + y, my_slot, (x, y, 1 - c)))
        recvs.append(copy(0, 2 * x + y, 4 * x + 2 * y + (1 - c), (x, y, 1 - c)))
        for j, (px, py) in enumerate(chips):
            for other, pc in enumerate((c, 1 - c)):
                sends.append(copy(1 + 2 * j + other, 2 * px + py, my_slot, (px, py, pc)))
                recvs.append(copy(1 + 2 * j + other, 2 * x + y, 4 * px + 2 * py + pc, (px, py, pc)))
    return sends, recvs


def _grad_copies_same_core(grads, parts, send_sems, recv_sems):
    x, y, c = lax.axis_index("x"), lax.axis_index("y"), lax.axis_index("c")
    chips = [(1 - x, y), (x, 1 - y), (1 - x, 1 - y)]
    my_slot = 4 * x + 2 * y + c
    sends, recvs = [], []
    for a, (grad, part) in enumerate(zip(grads, parts)):
        def copy(k, block, slot, to, a=a, grad=grad, part=part):
            return pltpu.make_async_remote_copy(
                src_ref=grad.at[block], dst_ref=part.at[slot], send_sem=send_sems.at[4 * a + k],
                recv_sem=recv_sems.at[4 * a + k], device_id=to, device_id_type=MESH_ID)
        sends.append(copy(0, 2 * x + y, my_slot, (x, y, 1 - c)))
        recvs.append(copy(0, 2 * x + y, 4 * x + 2 * y + (1 - c), (x, y, 1 - c)))
        for j, (px, py) in enumerate(chips):
            sends.append(copy(1 + j, 2 * px + py, my_slot, (px, py, c)))
            recvs.append(copy(1 + j, 2 * x + y, 4 * px + 2 * py + c, (px, py, c)))
    return sends, recvs


def _grad_pass_copies(parts, same_parts, send_sems, recv_sems):
    del same_parts
    x, y, c = lax.axis_index("x"), lax.axis_index("y"), lax.axis_index("c")
    chips = [(1 - x, y), (x, 1 - y), (1 - x, 1 - y)]
    sends, recvs = [], []
    for a, part in enumerate(parts):
        for j, (px, py) in enumerate(chips):
            def copy(pc, a=a, j=j, px=px, py=py, part=part):
                slot = part.at[4 * px + 2 * py + pc]
                return pltpu.make_async_remote_copy(
                    src_ref=slot, dst_ref=slot, send_sem=send_sems.at[3 * a + j], recv_sem=recv_sems.at[3 * a + j],
                    device_id=(x, y, 1 - c), device_id_type=MESH_ID)
            sends.append(copy(c))
            recvs.append(copy(1 - c))
    return sends, recvs


def _ada_mod(c_all, ada_w, ada_b_cols):
    n_l, d, a4 = ada_w.shape
    tn = _tile(a4, 512)

    def body(c_ref, w_ref, b_ref, o_ref):
        cv = c_ref[...]
        ca = (cv * jax.nn.sigmoid(cv)).astype(BF16)
        o_ref[...] = _dot(ca, w_ref[...].astype(BF16)) + b_ref[...]

    return pl.pallas_call(
        body, name="ada_mod", grid=(n_l, a4 // tn),
        in_specs=[pl.BlockSpec((N_DEV, d), lambda l, j: (0, 0)),
                  pl.BlockSpec((None, d, tn), lambda l, j: (l, 0, j)),
                  pl.BlockSpec((None, 1, tn), lambda l, j: (l, 0, j))],
        out_specs=pl.BlockSpec((None, N_DEV, tn), lambda l, j: (l, 0, j)),
        out_shape=jax.ShapeDtypeStruct((n_l, N_DEV, a4), F32),
        compiler_params=_params("parallel", "parallel"),
    )(c_all, ada_w, ada_b_cols)


def _ada_grad_adam(c_all_t, dmod_cols, w, m, v):
    n_l, d, a4 = w.shape
    tn = _tile(a4, 512)

    def body(ct_ref, dm_ref, w_ref, m_ref, v_ref, g_ref, dl_ref, nm_ref, nv_ref):
        ct = ct_ref[...]
        ca = ct * jax.nn.sigmoid(ct)
        dm = dm_ref[...]
        g = ca[:, 0:1] * dm[0:1, :]
        for dev in range(1, N_DEV):
            g = g + ca[:, dev:dev + 1] * dm[dev:dev + 1, :]
        g_ref[...] = g
        delta, nm, nv = _adamw(w_ref[...], g, m_ref[...], v_ref[...])
        dl_ref[...] = delta
        nm_ref[...] = nm
        nv_ref[...] = nv

    wspec = pl.BlockSpec((None, d, tn), lambda l, j: (l, 0, j))
    shp = jax.ShapeDtypeStruct(w.shape, F32)
    return pl.pallas_call(
        body, name="ada_grad_adam", grid=(n_l, a4 // tn),
        in_specs=[pl.BlockSpec((d, N_DEV), lambda l, j: (0, 0)),
                  pl.BlockSpec((None, N_DEV, tn), lambda l, j: (l, 0, j)), wspec, wspec, wspec],
        out_specs=[wspec] * 4, out_shape=[shp] * 4,
        compiler_params=_params("parallel", "parallel"),
    )(c_all_t, dmod_cols, w, m, v)


def _lnmod(x, g, sc, sh):
    s, d = x.shape
    tm = _tile(s, 512)

    def body(x_ref, g_ref, sc_ref, sh_ref, h_ref):
        xv = x_ref[...]
        r = lax.rsqrt(jnp.mean(xv * xv, axis=-1, keepdims=True) + EPS)
        h_ref[...] = ((xv * r * g_ref[...]) * (1.0 + sc_ref[...]) + sh_ref[...]).astype(BF16)

    vec = pl.BlockSpec((1, d), lambda i: (0, 0))
    row = pl.BlockSpec((tm, d), lambda i: (i, 0))
    return pl.pallas_call(
        body, name="lnmod", grid=(s // tm,), in_specs=[row, vec, vec, vec], out_specs=row,
        out_shape=jax.ShapeDtypeStruct((s, d), BF16), compiler_params=_params("parallel"),
    )(x, g, sc, sh)


def _mm_in(h, w_g):
    s, d = h.shape
    n4 = w_g.shape[-1]
    tm = _tile(s, 512)

    def body(a_ref, b_ref, o_ref):
        o_ref[...] = _dot(a_ref[...], b_ref[...])

    return pl.pallas_call(
        body, name="mm_in", grid=(N_CHIP, s // tm),
        in_specs=[pl.BlockSpec((tm, d), lambda j, i: (i, 0)),
                  pl.BlockSpec((None, d, n4), lambda j, i: (j, 0, 0))],
        out_specs=pl.BlockSpec((tm, n4), lambda j, i: (i, j)),
        out_shape=jax.ShapeDtypeStruct((s, N_CHIP * n4), F32),
        compiler_params=_params("parallel", "parallel"),
    )(h, w_g)


def _pair_mean(x, low):
    lo = jnp.sum(jnp.where(low, x, 0.0), axis=-1, keepdims=True)
    hi = jnp.sum(jnp.where(low, 0.0, x), axis=-1, keepdims=True)
    return jnp.where(low, lo, hi) * (1.0 / HEAD_DIM)


def _pair_norm(x, low):
    r = lax.rsqrt(_pair_mean(x * x, low) + EPS)
    return x * r, r


def _log_not(z):
    nz = -z
    return jnp.minimum(nz, 0.0) - jnp.log(1.0 + jnp.exp(jnp.minimum(z, nz)))


def _attn_consts(inclusive):
    low = lax.broadcasted_iota(jnp.int32, (1, LANES), 1) < HEAD_DIM
    row = lax.broadcasted_iota(jnp.int32, (Q_BLOCK, Q_BLOCK), 0)
    col = lax.broadcasted_iota(jnp.int32, (Q_BLOCK, Q_BLOCK), 1)
    tri = (row <= col) if inclusive else (row > col)
    w2 = jnp.concatenate([tri.astype(BF16), jnp.ones((Q_BLOCK, Q_BLOCK), BF16)], axis=1)
    return low, col < row, jnp.concatenate([w2, w2], axis=0)


def _split_cat(v):
    hi = v.astype(BF16)
    return jnp.concatenate([hi, (v - hi.astype(F32)).astype(BF16)], axis=1)


def _fill_pair_blocks(dst, src_fn, low, n_kb):
    def fill(b, _):
        v = src_fn(pl.ds(pl.multiple_of(b * Q_BLOCK, Q_BLOCK), Q_BLOCK))
        dst[b, 0:Q_BLOCK, :] = jnp.where(low, v, 0.0).astype(BF16)
        dst[b, Q_BLOCK:2 * Q_BLOCK, :] = jnp.where(low, 0.0, v).astype(BF16)
        return 0

    lax.fori_loop(0, n_kb, fill, 0)


def _attn_fwd(p, qg2, kg2, d):
    s = p.shape[0]
    n_pairs = d // LANES
    qsb = _tile(s, Q_SUPER)
    n_sub, n_sb, n_kb = qsb // Q_BLOCK, s // qsb, s // Q_BLOCK
    unroll = math.gcd(KEY_UNROLL, n_sub)
    chunk = _tile(s, 512)
    inv_sqrt = 1.0 / math.sqrt(HEAD_DIM)

    def body(q_ref, k_ref, v_ref, qg_ref, kg_ref, o_ref, lt_ref, qs, k2, v2, run, acc):
        low, causal, w4 = _attn_consts(False)

        def prep(r, _):
            rows = pl.ds(pl.multiple_of(r * chunk, chunk), chunk)
            qs[rows, :] = (_pair_norm(q_ref[rows, :], low)[0] * (qg_ref[...] * inv_sqrt)).astype(BF16)
            return 0

        lax.fori_loop(0, s // chunk, prep, 0)
        _fill_pair_blocks(k2, lambda rows: _pair_norm(k_ref[rows, :], low)[0] * kg_ref[...], low, n_kb)
        _fill_pair_blocks(v2, lambda rows: v_ref[rows, :], low, n_kb)

        def step(sb, j, t0=0, diag_t=None):
            rows = pl.ds(pl.multiple_of(sb * qsb + t0 * Q_BLOCK, Q_BLOCK), (n_sub - t0) * Q_BLOCK)
            z_both = _dot_nt(qs[rows, :], k2[j])
            zls, cats = [], []
            for t in range(t0, n_sub):
                sub = slice((t - t0) * Q_BLOCK, (t - t0 + 1) * Q_BLOCK)
                for h in range(2):
                    z = z_both[sub, h * LANES:(h + 1) * LANES]
                    ln = _log_not(z)
                    if t == diag_t:
                        ln = jnp.where(causal, ln, 0.0)
                    zls.append(z + ln)
                    cats.append(_split_cat(ln))
            c2 = _dot(jnp.concatenate(cats, axis=0), w4)
            a_rows = []
            for t in range(t0, n_sub):
                sub = slice(t * Q_BLOCK, (t + 1) * Q_BLOCK)
                a_pair = []
                for h in range(2):
                    i = 2 * (t - t0) + h
                    tile = slice(i * Q_BLOCK, (i + 1) * Q_BLOCK)
                    later = run[h, sub, :]
                    log_a = zls[i] + c2[tile, :LANES] + later
                    if t == diag_t:
                        log_a = jnp.where(causal, log_a, -1e30)
                    a_pair.append(jnp.exp(log_a).astype(BF16))
                    run[h, sub, :] = later + c2[tile, LANES:]
                a_rows.append(jnp.concatenate(a_pair, axis=1))
            acc[t0 * Q_BLOCK:, :] += _dot(jnp.concatenate(a_rows, axis=0), v2[j])

        def super_block(sb, _):
            run[...] = jnp.zeros_like(run)
            acc[...] = jnp.zeros_like(acc)
            for t in reversed(range(n_sub)):
                step(sb, sb * n_sub + t, t0=t, diag_t=t)

            def below(n, _):
                for u in range(unroll):
                    step(sb, sb * n_sub - 1 - (unroll * n + u))
                return 0

            lax.fori_loop(0, sb * (n_sub // unroll), below, 0)
            rows_sb = pl.ds(pl.multiple_of(sb * qsb, qsb), qsb)
            o_ref[rows_sb, :] = acc[...].astype(BF16)
            lt_ref[rows_sb, :] = jnp.where(low, run[0], run[1])
            return 0

        lax.fori_loop(0, n_sb, super_block, 0)

    def seg(k):
        return pl.BlockSpec((s, LANES), lambda h, k=k: (0, k * n_pairs + h))

    vec = pl.BlockSpec((1, LANES), lambda h: (0, 0))
    out = pl.BlockSpec((s, LANES), lambda h: (0, h))
    return pl.pallas_call(
        body, name="attn_fwd", grid=(n_pairs,),
        in_specs=[seg(0), seg(1), seg(2), vec, vec], out_specs=[out, out],
        out_shape=[jax.ShapeDtypeStruct((s, d), BF16), jax.ShapeDtypeStruct((s, d), F32)],
        scratch_shapes=[pltpu.VMEM((s, LANES), BF16)] + [pltpu.VMEM((n_kb, 2 * Q_BLOCK, LANES), BF16)] * 2
        + [pltpu.VMEM((2, qsb, LANES), F32), pltpu.VMEM((qsb, LANES), F32)],
        compiler_params=_params("parallel"),
    )(p, p, p, qg2, kg2)


def _conv_rows(s):
    return _tile(s, 512)


def _conv_fwd(p, conv_w, d):
    s = p.shape[0]
    nb = d // LANES
    rows_n = _conv_rows(s)

    def body(cb_ref, cc_ref, cx_ref, w_ref, y_ref, us):
        us[pl.ds(0, 8), :] = jnp.zeros((8, LANES), F32)

        def fill(r, _):
            rows = pl.ds(pl.multiple_of(r * rows_n, rows_n), rows_n)
            us[pl.ds(pl.multiple_of(r * rows_n + 8, 8), rows_n), :] = cc_ref[rows, :] * cx_ref[rows, :]
            return 0

        lax.fori_loop(0, s // rows_n, fill, 0)
        w = w_ref[...]

        def out(r, _):
            rows = pl.ds(pl.multiple_of(r * rows_n, rows_n), rows_n)
            ext = us[pl.ds(pl.multiple_of(r * rows_n, 8), rows_n + 8), :]
            cv = (w[0:1, :] * pltpu.roll(ext, 2, 0)[8:, :] + w[1:2, :] * pltpu.roll(ext, 1, 0)[8:, :]
                  + w[2:3, :] * ext[8:, :])
            y_ref[rows, :] = (cb_ref[rows, :] * cv).astype(BF16)
            return 0

        lax.fori_loop(0, s // rows_n, out, 0)

    def seg(k):
        return pl.BlockSpec((s, LANES), lambda b, k=k: (0, k * nb + b))

    return pl.pallas_call(
        body, name="conv_fwd", grid=(nb,),
        in_specs=[seg(3), seg(4), seg(5), pl.BlockSpec((3, LANES), lambda b: (0, b))],
        out_specs=pl.BlockSpec((s, LANES), lambda b: (0, b)),
        out_shape=jax.ShapeDtypeStruct((s, d), BF16),
        scratch_shapes=[pltpu.VMEM((s + 8, LANES), F32)],
        compiler_params=_params("parallel"),
    )(p, p, p, conv_w)


def _branch(ya, yb, p, wa, wb, d):
    s = ya.shape[0]
    tm = _tile(s, 512)

    def body(ya_ref, yb_ref, ga_ref, gb_ref, wa_ref, wb_ref, m_ref, a_ref, b_ref):
        pa = _dot(ya_ref[...], wa_ref[...])
        pb = _dot(yb_ref[...], wb_ref[...])
        m_ref[...] = (jax.nn.sigmoid(ga_ref[...]) * pa + jax.nn.sigmoid(gb_ref[...]) * pb).astype(BF16)
        a_ref[...] = pa.astype(BF16)
        b_ref[...] = pb.astype(BF16)

    row = pl.BlockSpec((tm, d), lambda i: (i, 0))
    wsp = pl.BlockSpec((d, d), lambda i: (0, 0))
    shp = jax.ShapeDtypeStruct((s, d), BF16)
    return pl.pallas_call(
        body, name="branch", grid=(s // tm,),
        in_specs=[row, row, pl.BlockSpec((tm, d), lambda i: (i, 6)), pl.BlockSpec((tm, d), lambda i: (i, 7)), wsp, wsp],
        out_specs=[row, row, row], out_shape=[shp, shp, shp], compiler_params=_params("parallel"),
    )(ya, yb, p, p, wa, wb)


def _out_proj(merged, wout, x0, g1):
    s, d = x0.shape
    tm = _tile(s, 512)

    def body(m_ref, w_ref, x_ref, g_ref, x1_ref, mo_ref):
        mo = _dot(m_ref[...], w_ref[...])
        mo_ref[...] = mo
        x1_ref[...] = x_ref[...] + g_ref[...] * mo

    row = pl.BlockSpec((tm, d), lambda i: (i, 0))
    shp = jax.ShapeDtypeStruct((s, d), F32)
    return pl.pallas_call(
        body, name="out_proj", grid=(s // tm,),
        in_specs=[row, pl.BlockSpec((d, d), lambda i: (0, 0)), row, pl.BlockSpec((1, d), lambda i: (0, 0))],
        out_specs=[row, row], out_shape=[shp, shp], compiler_params=_params("parallel"),
    )(merged, wout, x0, g1)


def _ffn_up(h, wg_g, wu_g):
    s, d = h.shape
    f4 = wg_g.shape[-1]
    tm = _tile(s, 512)

    def body(h_ref, wg_ref, wu_ref, gate_ref, up_ref, act_ref):
        hv = h_ref[...]
        gt = _dot(hv, wg_ref[...])
        up = _dot(hv, wu_ref[...])
        gate_ref[...] = gt.astype(BF16)
        up_ref[...] = up.astype(BF16)
        act_ref[...] = (gt * jax.nn.sigmoid(gt) * up).astype(BF16)

    wsp = pl.BlockSpec((None, d, f4), lambda j, i: (j, 0, 0))
    osp = pl.BlockSpec((None, tm, f4), lambda j, i: (j, i, 0))
    shp = jax.ShapeDtypeStruct((N_CHIP, s, f4), BF16)
    return pl.pallas_call(
        body, name="ffn_up", grid=(N_CHIP, s // tm),
        in_specs=[pl.BlockSpec((tm, d), lambda j, i: (i, 0)), wsp, wsp],
        out_specs=[osp, osp, osp], out_shape=[shp, shp, shp], compiler_params=_params("parallel", "parallel"),
    )(h, wg_g, wu_g)


def _ffn_down(act, wd_g, x1, g2):
    s, d = x1.shape
    f4 = act.shape[-1]
    tm = _tile(s, 512)

    def body(a_ref, w_ref, x_ref, g_ref, x2_ref, f_ref, acc):
        j = pl.program_id(1)

        @pl.when(j == 0)
        def _():
            acc[...] = jnp.zeros_like(acc)

        acc[...] += _dot(a_ref[...], w_ref[...])

        @pl.when(j == N_CHIP - 1)
        def _():
            f = acc[...]
            f_ref[...] = f
            x2_ref[...] = x_ref[...] + g_ref[...] * f

    row = pl.BlockSpec((tm, d), lambda i, j: (i, 0))
    shp = jax.ShapeDtypeStruct((s, d), F32)
    return pl.pallas_call(
        body, name="ffn_down", grid=(s // tm, N_CHIP),
        in_specs=[pl.BlockSpec((None, tm, f4), lambda i, j: (j, i, 0)),
                  pl.BlockSpec((None, f4, d), lambda i, j: (j, 0, 0)),
                  row, pl.BlockSpec((1, d), lambda i, j: (0, 0))],
        out_specs=[row, row], out_shape=[shp, shp],
        scratch_shapes=[pltpu.VMEM((tm, d), F32)], compiler_params=_params("parallel", "arbitrary"),
    )(act, wd_g, x1, g2)


def _loss_head(y, target):
    s, d = y.shape
    tm = _tile(s, 512)
    n_steps = s // tm

    def body(y_ref, t_ref, dy_ref, l_ref, acc):
        i = pl.program_id(0)

        @pl.when(i == 0)
        def _():
            acc[...] = jnp.zeros_like(acc)

        err = y_ref[...] - t_ref[...]
        dy_ref[...] = err / d
        acc[...] += jnp.sum(err * err, axis=0, keepdims=True)

        @pl.when(i == n_steps - 1)
        def _():
            l_ref[...] = jnp.broadcast_to(jnp.sum(acc[...], axis=1, keepdims=True), (8, LANES))

    row = pl.BlockSpec((tm, d), lambda i: (i, 0))
    return pl.pallas_call(
        body, name="loss_head", grid=(n_steps,), in_specs=[row, row],
        out_specs=[row, pl.BlockSpec((8, LANES), lambda i: (0, 0))],
        out_shape=[jax.ShapeDtypeStruct((s, d), F32), jax.ShapeDtypeStruct((8, LANES), F32)],
        scratch_shapes=[pltpu.VMEM((1, d), F32)], compiler_params=_params("arbitrary"),
    )(y, target)


def _mm_tn(a, b, a_spec, b_spec, out_rc, name):
    r, c = out_rc
    s = a.shape[-2]
    tk = _tile(s, 512)
    nk = s // tk

    def body(a_ref, b_ref, o_ref, acc):
        k = pl.program_id(1)

        @pl.when(k == 0)
        def _():
            acc[...] = jnp.zeros_like(acc)

        acc[...] += _dot_tn(a_ref[...], b_ref[...])

        @pl.when(k == nk - 1)
        def _():
            o_ref[...] = acc[...].astype(BF16)

    return pl.pallas_call(
        body, name=name, grid=(N_CHIP, nk),
        in_specs=[pl.BlockSpec(*a_spec(tk)), pl.BlockSpec(*b_spec(tk))],
        out_specs=pl.BlockSpec((None, r, c), lambda j, k: (j, 0, 0)),
        out_shape=jax.ShapeDtypeStruct((N_CHIP, r, c), BF16),
        scratch_shapes=[pltpu.VMEM((r, c), F32)], compiler_params=_params("parallel", "arbitrary"),
    )(a, b)


def _mm_tn_square(a, b, name):
    s, d = a.shape
    r4 = d // N_CHIP
    tk = _tile(s, 512)
    nk = s // tk

    def body(a_ref, b_ref, o_ref, acc):
        k = pl.program_id(0)

        @pl.when(k == 0)
        def _():
            acc[...] = jnp.zeros_like(acc)

        acc[...] += _dot_tn(a_ref[...], b_ref[...])

        @pl.when(k == nk - 1)
        def _():
            for j in range(N_CHIP):
                o_ref[j] = acc[j * r4:(j + 1) * r4, :].astype(BF16)

    blk = pl.BlockSpec((tk, d), lambda k: (k, 0))
    return pl.pallas_call(
        body, name=name, grid=(nk,), in_specs=[blk, blk],
        out_specs=pl.BlockSpec((N_CHIP, r4, d), lambda k: (0, 0, 0)),
        out_shape=jax.ShapeDtypeStruct((N_CHIP, r4, d), BF16),
        scratch_shapes=[pltpu.VMEM((d, d), F32)], compiler_params=_params("arbitrary"),
    )(a, b)


def _ffn_bwd1(dx2, f, g2, wd_g, gate, up):
    s, d = dx2.shape
    f4 = gate.shape[-1]
    tm = _tile(s, 512)

    def body(dx_ref, f_ref, g_ref, w_ref, gate_ref, up_ref, dgate_ref, dup_ref, df_ref, dg_ref):
        i, j = pl.program_id(0), pl.program_id(1)

        @pl.when((i == 0) & (j == 0))
        def _():
            dg_ref[...] = jnp.zeros_like(dg_ref)

        dxv = dx_ref[...]
        df = (g_ref[...] * dxv).astype(BF16)

        @pl.when(j == 0)
        def _():
            df_ref[...] = df
            dg_ref[0:1, :] += jnp.sum(dxv * f_ref[...], axis=0, keepdims=True)

        da = _dot_nt(df, w_ref[...])
        gt = gate_ref[...].astype(F32)
        sg = jax.nn.sigmoid(gt)
        dup_ref[...] = (da * gt * sg).astype(BF16)
        dgate_ref[...] = (da * up_ref[...].astype(F32) * (sg * (1.0 + gt * (1.0 - sg)))).astype(BF16)

    row = pl.BlockSpec((tm, d), lambda i, j: (i, 0))
    hsp = pl.BlockSpec((None, tm, f4), lambda i, j: (j, i, 0))
    hshp = jax.ShapeDtypeStruct((N_CHIP, s, f4), BF16)
    return pl.pallas_call(
        body, name="ffn_bwd1", grid=(s // tm, N_CHIP),
        in_specs=[row, row, pl.BlockSpec((1, d), lambda i, j: (0, 0)),
                  pl.BlockSpec((None, f4, d), lambda i, j: (j, 0, 0)), hsp, hsp],
        out_specs=[hsp, hsp, row, pl.BlockSpec((8, d), lambda i, j: (0, 0))],
        out_shape=[hshp, hshp, jax.ShapeDtypeStruct((s, d), BF16), jax.ShapeDtypeStruct((8, d), F32)],
        compiler_params=_params("arbitrary", "arbitrary"),
    )(dx2, f, g2, wd_g, gate, up)


def _ffn_bwd2(dgate, dup, wg_g, wu_g):
    _, s, f4 = dgate.shape
    d = wg_g.shape[-2]
    tm = _tile(s, 512)

    def body(dg_ref, du_ref, wg_ref, wu_ref, o_ref, acc):
        j = pl.program_id(1)

        @pl.when(j == 0)
        def _():
            acc[...] = jnp.zeros_like(acc)

        acc[...] += _dot_nt(dg_ref[...], wg_ref[...]) + _dot_nt(du_ref[...], wu_ref[...])

        @pl.when(j == N_CHIP - 1)
        def _():
            o_ref[...] = acc[...]

    hsp = pl.BlockSpec((None, tm, f4), lambda i, j: (j, i, 0))
    wsp = pl.BlockSpec((None, d, f4), lambda i, j: (j, 0, 0))
    return pl.pallas_call(
        body, name="ffn_bwd2", grid=(s // tm, N_CHIP), in_specs=[hsp, hsp, wsp, wsp],
        out_specs=pl.BlockSpec((tm, d), lambda i, j: (i, 0)), out_shape=jax.ShapeDtypeStruct((s, d), F32),
        scratch_shapes=[pltpu.VMEM((tm, d), F32)], compiler_params=_params("parallel", "arbitrary"),
    )(dgate, dup, wg_g, wu_g)


def _lnmod_bwd(x, g, sc, dh, dres):
    s, d = x.shape
    tm = _tile(s, 512)

    def body(x_ref, g_ref, sc_ref, dh_ref, dr_ref, dx_ref, sums_ref):
        @pl.when(pl.program_id(0) == 0)
        def _():
            sums_ref[...] = jnp.zeros_like(sums_ref)

        xv, dhv, gv = x_ref[...], dh_ref[...], g_ref[...]
        r = lax.rsqrt(jnp.mean(xv * xv, axis=-1, keepdims=True) + EPS)
        n = xv * r
        one_sc = 1.0 + sc_ref[...]
        dt = dhv * one_sc
        sums_ref[0:1, :] += jnp.sum(dhv, axis=0, keepdims=True)
        sums_ref[1:2, :] += jnp.sum(dhv * (n * gv), axis=0, keepdims=True)
        sums_ref[2:3, :] += jnp.sum(dt * n, axis=0, keepdims=True)
        dn = dt * gv
        dx_ref[...] = dr_ref[...] + r * (dn - n * jnp.mean(dn * n, axis=-1, keepdims=True))

    vec = pl.BlockSpec((1, d), lambda i: (0, 0))
    row = pl.BlockSpec((tm, d), lambda i: (i, 0))
    return pl.pallas_call(
        body, name="lnmod_bwd", grid=(s // tm,), in_specs=[row, vec, vec, row, row],
        out_specs=[row, pl.BlockSpec((8, d), lambda i: (0, 0))],
        out_shape=[jax.ShapeDtypeStruct((s, d), F32), jax.ShapeDtypeStruct((8, d), F32)],
        compiler_params=_params("arbitrary"),
    )(x, g, sc, dh, dres)


def _out_bwd(dx1, mo, g1, wout, pa, pb, p, wa, wb, d):
    s = dx1.shape[0]
    tm = _tile(s, 256)

    def body(dx_ref, mo_ref, g_ref, wo_ref, pa_ref, pb_ref, ga_ref, gb_ref, wa_ref, wb_ref,
             dmo_ref, da_ref, db_ref, dya_ref, dyb_ref, dp_ref, dg_ref):
        @pl.when(pl.program_id(0) == 0)
        def _():
            dg_ref[...] = jnp.zeros_like(dg_ref)

        dxv = dx_ref[...]
        dg_ref[0:1, :] += jnp.sum(dxv * mo_ref[...], axis=0, keepdims=True)
        dmo = (g_ref[...] * dxv).astype(BF16)
        dmo_ref[...] = dmo
        dm = _dot_nt(dmo, wo_ref[...])
        sa, sb = jax.nn.sigmoid(ga_ref[...]), jax.nn.sigmoid(gb_ref[...])
        da = (dm * sa).astype(BF16)
        db = (dm * sb).astype(BF16)
        da_ref[...] = da
        db_ref[...] = db
        dp_ref[:, :d] = (dm * pa_ref[...].astype(F32) * (sa * (1.0 - sa))).astype(BF16)
        dp_ref[:, d:] = (dm * pb_ref[...].astype(F32) * (sb * (1.0 - sb))).astype(BF16)
        dya_ref[...] = _dot_nt(da, wa_ref[...]).astype(BF16)
        dyb_ref[...] = _dot_nt(db, wb_ref[...]).astype(BF16)

    row = pl.BlockSpec((tm, d), lambda i: (i, 0))
    wsp = pl.BlockSpec((d, d), lambda i: (0, 0))
    shp = jax.ShapeDtypeStruct((s, d), BF16)
    return pl.pallas_call(
        body, name="out_bwd", grid=(s // tm,),
        in_specs=[row, row, pl.BlockSpec((1, d), lambda i: (0, 0)), wsp, row, row,
                  pl.BlockSpec((tm, d), lambda i: (i, 6)), pl.BlockSpec((tm, d), lambda i: (i, 7)), wsp, wsp],
        out_specs=[row] * 5 + [pl.BlockSpec((tm, 2 * d), lambda i: (i, 3)), pl.BlockSpec((8, d), lambda i: (0, 0))],
        out_shape=[shp] * 5 + [jax.ShapeDtypeStruct((s, 8 * d), BF16), jax.ShapeDtypeStruct((8, d), F32)],
        compiler_params=_params("arbitrary"),
    )(dx1, mo, g1, wout, pa, pb, p, p, wa, wb)


def _store_segments(outs, dp_out, sems, col_blocks):
    copies = [pltpu.make_async_copy(outs.at[k], dp_out.at[:, pl.ds(pl.multiple_of(cb * LANES, LANES), LANES)],
                                    sems.at[k]) for k, cb in enumerate(col_blocks)]
    for cp in copies:
        cp.start()
    for cp in copies:
        cp.wait()


def _conv_bwd(p, conv_w, dyb, dp, d):
    s = p.shape[0]
    nb = d // LANES
    rows_n = _conv_rows(s)

    def compute(cb_ref, cc_ref, cx_ref, w_ref, dy_ref, dcb_ref, dcc_ref, dcx_ref, dw_ref, us, ds):
        us[pl.ds(0, 8), :] = jnp.zeros((8, LANES), F32)
        ds[pl.ds(s, 8), :] = jnp.zeros((8, LANES), F32)

        def fill(r, _):
            rows = pl.ds(pl.multiple_of(r * rows_n, rows_n), rows_n)
            us[pl.ds(pl.multiple_of(r * rows_n + 8, 8), rows_n), :] = cc_ref[rows, :] * cx_ref[rows, :]
            ds[rows, :] = dy_ref[rows, :].astype(F32) * cb_ref[rows, :]
            return 0

        lax.fori_loop(0, s // rows_n, fill, 0)
        w = w_ref[...]

        def out(r, carry):
            dw0, dw1, dw2 = carry
            rows = pl.ds(pl.multiple_of(r * rows_n, rows_n), rows_n)
            ext = us[pl.ds(pl.multiple_of(r * rows_n, 8), rows_n + 8), :]
            u0, u1, u2 = ext[8:, :], pltpu.roll(ext, 1, 0)[8:, :], pltpu.roll(ext, 2, 0)[8:, :]
            cv = w[0:1, :] * u2 + w[1:2, :] * u1 + w[2:3, :] * u0
            dcb_ref[rows, :] = (dy_ref[rows, :].astype(F32) * cv).astype(BF16)
            nxt = ds[pl.ds(pl.multiple_of(r * rows_n, 8), rows_n + 8), :]
            e0 = nxt[:rows_n, :]
            e1 = pltpu.roll(nxt, rows_n + 7, 0)[:rows_n, :]
            e2 = pltpu.roll(nxt, rows_n + 6, 0)[:rows_n, :]
            du = w[2:3, :] * e0 + w[1:2, :] * e1 + w[0:1, :] * e2
            dcc_ref[rows, :] = (du * cx_ref[rows, :]).astype(BF16)
            dcx_ref[rows, :] = (du * cc_ref[rows, :]).astype(BF16)
            return (dw0 + jnp.sum(e0 * u2, axis=0, keepdims=True), dw1 + jnp.sum(e0 * u1, axis=0, keepdims=True),
                    dw2 + jnp.sum(e0 * u0, axis=0, keepdims=True))

        zero = jnp.zeros((1, LANES), F32)
        dw0, dw1, dw2 = lax.fori_loop(0, s // rows_n, out, (zero, zero, zero))
        dw_ref[...] = jnp.zeros_like(dw_ref)
        dw_ref[0:1, :] = dw0
        dw_ref[1:2, :] = dw1
        dw_ref[2:3, :] = dw2

    def body(cb_ref, cc_ref, cx_ref, w_ref, dy_ref, dp_in, dp_out, dw_ref, us, ds, outs, sems):
        del dp_in
        compute(cb_ref, cc_ref, cx_ref, w_ref, dy_ref, outs.at[0], outs.at[1], outs.at[2], dw_ref, us, ds)
        _store_segments(outs, dp_out, sems, [(3 + k) * nb + pl.program_id(0) for k in range(3)])

    def seg(k):
        return pl.BlockSpec((s, LANES), lambda b, k=k: (0, k * nb + b))

    return pl.pallas_call(
        body, name="conv_bwd", grid=(nb,),
        in_specs=[seg(3), seg(4), seg(5), pl.BlockSpec((3, LANES), lambda b: (0, b)),
                  pl.BlockSpec((s, LANES), lambda b: (0, b)), ANY_SPEC],
        out_specs=[ANY_SPEC, pl.BlockSpec((8, LANES), lambda b: (0, b))],
        out_shape=[jax.ShapeDtypeStruct(dp.shape, BF16), jax.ShapeDtypeStruct((8, d), F32)],
        input_output_aliases={5: 0},
        scratch_shapes=[pltpu.VMEM((s + 8, LANES), F32), pltpu.VMEM((s + 8, LANES), F32),
                        pltpu.VMEM((3, s, LANES), BF16), pltpu.SemaphoreType.DMA((3,))],
        compiler_params=_params("arbitrary"),
    )(p, p, p, conv_w, dyb, dp)


def _attn_bwd(p, qg2, kg2, dy, lt, dp, d):
    s = p.shape[0]
    n_pairs = d // LANES
    qsb = _tile(s, Q_SUPER_BWD)
    n_sub, n_sb, n_kb = qsb // Q_BLOCK, s // qsb, s // Q_BLOCK
    unroll = math.gcd(KEY_UNROLL, n_sub)
    chunk = _tile(s, 512)
    inv_sqrt = 1.0 / math.sqrt(HEAD_DIM)

    def compute(q_ref, k_ref, v_ref, qg_ref, kg_ref, dy_ref, lt_ref, dq_ref, dk_ref, dv_ref, dgain_ref,
                qs, k2, v2, dkt, dvt, qt, dyt, rem, gbef, dqa):
        low, causal, w4 = _attn_consts(True)

        def prep(r, _):
            rows = pl.ds(pl.multiple_of(r * chunk, chunk), chunk)
            qs[rows, :] = (_pair_norm(q_ref[rows, :], low)[0] * (qg_ref[...] * inv_sqrt)).astype(BF16)
            return 0

        lax.fori_loop(0, s // chunk, prep, 0)
        _fill_pair_blocks(k2, lambda rows: _pair_norm(k_ref[rows, :], low)[0] * kg_ref[...], low, n_kb)
        _fill_pair_blocks(v2, lambda rows: v_ref[rows, :], low, n_kb)

        def clear(b, _):
            dkt[b] = jnp.zeros((LANES, Q_BLOCK), F32)
            dvt[b] = jnp.zeros((LANES, Q_BLOCK), F32)
            return 0

        lax.fori_loop(0, n_kb, clear, 0)

        def step(sb, j, t0=0, diag_t=None):
            rows = pl.ds(pl.multiple_of(sb * qsb + t0 * Q_BLOCK, Q_BLOCK), (n_sub - t0) * Q_BLOCK)
            kj2, vj2 = k2[j], v2[j]
            z_both = _dot_nt(qs[rows, :], kj2)
            da_both = _dot_nt(dy_ref[rows, :], vj2)
            zls, cats = [], []
            for t in range(t0, n_sub):
                sub = slice((t - t0) * Q_BLOCK, (t - t0 + 1) * Q_BLOCK)
                for h in range(2):
                    z = z_both[sub, h * LANES:(h + 1) * LANES]
                    ln = _log_not(z)
                    if t == diag_t:
                        ln = jnp.where(causal, ln, 0.0)
                    zls.append(z + ln)
                    cats.append(_split_cat(ln))
            c2 = _dot(jnp.concatenate(cats, axis=0), w4)
            a_rows, gs, cats = [], [], []
            for t in range(t0, n_sub):
                sub = slice(t * Q_BLOCK, (t + 1) * Q_BLOCK)
                a_pair = []
                for h in range(2):
                    i = 2 * (t - t0) + h
                    tile = slice(i * Q_BLOCK, (i + 1) * Q_BLOCK)
                    left = rem[h, sub, :]
                    log_a = zls[i] + (left - c2[tile, :LANES])
                    if t == diag_t:
                        log_a = jnp.where(causal, log_a, -1e30)
                    a = jnp.exp(log_a)
                    rem[h, sub, :] = left - c2[tile, LANES:]
                    g = a * da_both[(t - t0) * Q_BLOCK:(t - t0 + 1) * Q_BLOCK, h * LANES:(h + 1) * LANES]
                    a_pair.append(a.astype(BF16))
                    gs.append(g)
                    cats.append(_split_cat(g))
                a_rows.append(jnp.concatenate(a_pair, axis=1))
            c2g = _dot(jnp.concatenate(cats, axis=0), w4)
            dz_rows = []
            for t in range(t0, n_sub):
                sub = slice(t * Q_BLOCK, (t + 1) * Q_BLOCK)
                dz_pair = []
                for h in range(2):
                    i = 2 * (t - t0) + h
                    tile = slice(i * Q_BLOCK, (i + 1) * Q_BLOCK)
                    before = gbef[h, sub, :]
                    dz = gs[i] - jnp.exp(zls[i]) * (before + c2g[tile, :LANES])
                    if t == diag_t:
                        dz = jnp.where(causal, dz, 0.0)
                    gbef[h, sub, :] = before + c2g[tile, LANES:]
                    dz_pair.append(dz.astype(BF16))
                dz_rows.append(jnp.concatenate(dz_pair, axis=1))
            a_both = jnp.concatenate(a_rows, axis=0)
            dz_both = jnp.concatenate(dz_rows, axis=0)
            used = slice(t0 * Q_BLOCK, qsb)
            dvt[j] += _dot(dyt[0, :, used], a_both[:, :LANES]) + _dot(dyt[1, :, used], a_both[:, LANES:])
            dkt[j] += _dot(qt[0, :, used], dz_both[:, :LANES]) + _dot(qt[1, :, used], dz_both[:, LANES:])
            dqa[used, :] += _dot(dz_both, kj2)

        def super_block(sb, dqg):
            rows_sb = pl.ds(pl.multiple_of(sb * qsb, qsb), qsb)
            total = lt_ref[rows_sb, :]
            other = pltpu.roll(total, HEAD_DIM, 1)
            rem[0] = jnp.where(low, total, other)
            rem[1] = jnp.where(low, other, total)
            gbef[...] = jnp.zeros_like(gbef)
            dqa[...] = jnp.zeros_like(dqa)
            qv = qs[rows_sb, :].astype(F32)
            dyv = dy_ref[rows_sb, :].astype(F32)
            qt[0] = jnp.where(low, qv, 0.0).T.astype(BF16)
            qt[1] = jnp.where(low, 0.0, qv).T.astype(BF16)
            dyt[0] = jnp.where(low, dyv, 0.0).T.astype(BF16)
            dyt[1] = jnp.where(low, 0.0, dyv).T.astype(BF16)

            def below(n, _):
                for u in range(unroll):
                    step(sb, unroll * n + u)
                return 0

            lax.fori_loop(0, sb * (n_sub // unroll), below, 0)
            for t in range(n_sub):
                step(sb, sb * n_sub + t, t0=t, diag_t=t)
            qhat, r = _pair_norm(q_ref[rows_sb, :], low)
            dqn = dqa[...]
            dqhat = dqn * (qg_ref[...] * inv_sqrt)
            dq_ref[rows_sb, :] = (r * (dqhat - qhat * _pair_mean(dqhat * qhat, low))).astype(BF16)
            return dqg + jnp.sum(dqn * qhat, axis=0, keepdims=True) * inv_sqrt

        dqg = lax.fori_loop(0, n_sb, super_block, jnp.zeros((1, LANES), F32))

        def finish(b, dkg):
            rows = pl.ds(pl.multiple_of(b * Q_BLOCK, Q_BLOCK), Q_BLOCK)
            khat, rk = _pair_norm(k_ref[rows, :], low)
            dkn = dkt[b].T
            dkhat = dkn * kg_ref[...]
            dk_ref[rows, :] = (rk * (dkhat - khat * _pair_mean(dkhat * khat, low))).astype(BF16)
            dv_ref[rows, :] = dvt[b].T.astype(BF16)
            return dkg + jnp.sum(dkn * khat, axis=0, keepdims=True)

        dkg = lax.fori_loop(0, n_kb, finish, jnp.zeros((1, LANES), F32))
        dgain_ref[...] = jnp.zeros_like(dgain_ref)
        dgain_ref[0:1, :] = dqg
        dgain_ref[1:2, :] = dkg

    def body(q_ref, k_ref, v_ref, qg_ref, kg_ref, dy_ref, lt_ref, dp_in, dp_out, dgain_ref, outs, sems, *scratch):
        del dp_in
        compute(q_ref, k_ref, v_ref, qg_ref, kg_ref, dy_ref, lt_ref, outs.at[0], outs.at[1], outs.at[2], dgain_ref,
                *scratch)
        _store_segments(outs, dp_out, sems, [k * n_pairs + pl.program_id(0) for k in range(3)])

    def seg(k):
        return pl.BlockSpec((s, LANES), lambda h, k=k: (0, k * n_pairs + h))

    vec = pl.BlockSpec((1, LANES), lambda h: (0, 0))
    col = pl.BlockSpec((s, LANES), lambda h: (0, h))
    return pl.pallas_call(
        body, name="attn_bwd", grid=(n_pairs,),
        in_specs=[seg(0), seg(1), seg(2), vec, vec, col, col, ANY_SPEC],
        out_specs=[ANY_SPEC, pl.BlockSpec((None, 8, LANES), lambda h: (h, 0, 0))],
        out_shape=[jax.ShapeDtypeStruct(dp.shape, BF16), jax.ShapeDtypeStruct((n_pairs, 8, LANES), F32)],
        input_output_aliases={7: 0},
        scratch_shapes=[pltpu.VMEM((3, s, LANES), BF16), pltpu.SemaphoreType.DMA((3,)), pltpu.VMEM((s, LANES), BF16)]
        + [pltpu.VMEM((n_kb, 2 * Q_BLOCK, LANES), BF16)] * 2
        + [pltpu.VMEM((n_kb, LANES, Q_BLOCK), F32)] * 2
        + [pltpu.VMEM((2, LANES, qsb), BF16)] * 2
        + [pltpu.VMEM((2, qsb, LANES), F32)] * 2 + [pltpu.VMEM((qsb, LANES), F32)],
        compiler_params=_params("arbitrary"),
    )(p, p, p, qg2, kg2, dy, lt, dp)


def _mm_in_bwd(dp, w_g):
    s = dp.shape[0]
    d, n4 = w_g.shape[-2:]
    tm = _tile(s, 512)

    def body(a_ref, w_ref, o_ref, acc):
        j = pl.program_id(1)

        @pl.when(j == 0)
        def _():
            acc[...] = jnp.zeros_like(acc)

        acc[...] += _dot_nt(a_ref[...], w_ref[...])

        @pl.when(j == N_CHIP - 1)
        def _():
            o_ref[...] = acc[...]

    return pl.pallas_call(
        body, name="mm_in_bwd", grid=(s // tm, N_CHIP),
        in_specs=[pl.BlockSpec((tm, n4), lambda i, j: (i, j)),
                  pl.BlockSpec((None, d, n4), lambda i, j: (j, 0, 0))],
        out_specs=pl.BlockSpec((tm, d), lambda i, j: (i, 0)), out_shape=jax.ShapeDtypeStruct((s, d), F32),
        scratch_shapes=[pltpu.VMEM((tm, d), F32)], compiler_params=_params("parallel", "arbitrary"),
    )(dp, w_g)


def _sum_adam(parts, w, m, v, name):
    n_l, r, c = w.shape
    tr = next((t for t in (256, 176, 128, 64, 32, 16) if r % t == 0 and t * c <= 256 * 1024), r)
    n_blk = r // tr

    def body(*refs):
        p_refs = refs[:n_l]
        w_ref, m_ref, v_ref, g_ref, dl_ref, nm_ref, nv_ref = refs[n_l:]
        for l in range(n_l):
            @pl.when(pl.program_id(0) == l)
            def _(p_ref=p_refs[l]):
                g = p_ref[0].astype(F32)
                for dev in range(1, N_DEV):
                    g = g + p_ref[dev].astype(F32)
                g_ref[...] = g
                delta, nm, nv = _adamw(w_ref[...], g, m_ref[...], v_ref[...])
                dl_ref[...] = delta
                nm_ref[...] = nm
                nv_ref[...] = nv

    def part_spec(l):
        return pl.BlockSpec((N_DEV, tr, c), lambda ll, i, l=l: (0, jnp.where(ll == l, i, jnp.where(ll < l, 0, n_blk - 1)), 0))

    wsp = pl.BlockSpec((None, tr, c), lambda l, i: (l, i, 0))
    shp = jax.ShapeDtypeStruct(w.shape, F32)
    return pl.pallas_call(
        body, name=name, grid=(n_l, n_blk),
        in_specs=[part_spec(l) for l in range(n_l)] + [wsp, wsp, wsp],
        out_specs=[wsp] * 4, out_shape=[shp] * 4, compiler_params=_params("arbitrary", "arbitrary"),
    )(*parts, w, m, v)


def _small_adam(parts, w, m, v):
    def body(p_ref, w_ref, m_ref, v_ref, g_ref, dl_ref, nm_ref, nv_ref):
        g = p_ref[0]
        for dev in range(1, N_DEV):
            g = g + p_ref[dev]
        g_ref[...] = g
        delta, nm, nv = _adamw(w_ref[...], g, m_ref[...], v_ref[...])
        dl_ref[...] = delta
        nm_ref[...] = nm
        nv_ref[...] = nv

    shp = jax.ShapeDtypeStruct(w.shape, F32)
    return pl.pallas_call(body, name="small_adam", in_specs=[VMEM_SPEC] * 4, out_specs=[VMEM_SPEC] * 4,
                          out_shape=[shp] * 4,
                          compiler_params=pltpu.CompilerParams(vmem_limit_bytes=VMEM_LIMIT_BYTES))(parts, w, m, v)


def _pack(vecs, mult=8 * LANES):
    flat = jnp.concatenate([a.reshape(-1).astype(F32) for a in vecs])
    pad = (-flat.shape[0]) % mult
    if pad:
        flat = jnp.concatenate([flat, jnp.zeros((pad,), F32)])
    return flat.reshape(8, -1)


def _unpack(flat, shapes):
    flat = flat.reshape(-1)
    out, off = [], 0
    for shp in shapes:
        n = math.prod(shp)
        out.append(flat[off:off + n].reshape(shp))
        off += n
    return out


BIG = ("win", "wa", "wb", "wo", "wg", "wu", "wd")
GRAD_GROUPS = (("wd", "wg", "wu"), ("wo", "wa", "wb"), ("win",))


def _local_step(x, target, mods, ln1_g, ln2_g, qg, kg, conv_w, weights, send_grads):
    s, d = x.shape
    n_l = mods.shape[0]
    saved = []
    h_in = x
    for l in range(n_l):
        sh1, sc1, g1, sh2, sc2, g2 = [mods[l, k * d:(k + 1) * d].reshape(1, d) for k in range(6)]
        qg2, kg2 = jnp.tile(qg[l:l + 1], (1, 2)), jnp.tile(kg[l:l + 1], (1, 2))
        h1 = _lnmod(h_in, ln1_g[l:l + 1], sc1, sh1)
        (win,), tie = weights(l, ("win",), h1)
        p = _mm_in(h1, win)
        ya, lt = _attn_fwd(p, qg2 + tie, kg2, d)
        yb = _conv_fwd(p, conv_w[l], d)
        (wa, wb, wo, wg, wu, wd), tie = weights(l, ("wa", "wb", "wo", "wg", "wu", "wd"), ya)
        wa, wb, wo = wa.reshape(d, d), wb.reshape(d, d), wo.reshape(d, d)
        merged, pa, pb = _branch(ya, yb, p, wa, wb, d)
        x1, mo = _out_proj(merged, wo, h_in, g1 + tie)
        h2 = _lnmod(x1, ln2_g[l:l + 1], sc2, sh2)
        gate, up, act = _ffn_up(h2, wg, wu)
        x2, f = _ffn_down(act, wd, x1, g2)
        saved.append(dict(x0=h_in, h1=h1, p=p, ya=ya, lt=lt, yb=yb, merged=merged, pa=pa, pb=pb, x1=x1, mo=mo,
                          h2=h2, gate=gate, up=up, act=act, f=f, win=win, wa=wa, wb=wb, wo=wo, wg=wg, wu=wu, wd=wd,
                          mod=(sh1, sc1, g1, sh2, sc2, g2), qg2=qg2, kg2=kg2))
        h_in = x2

    dx, loss_tile = _loss_head(h_in, target)

    small = [None] * n_l
    for l in reversed(range(n_l)):
        sv = saved[l]
        sh1, sc1, g1, sh2, sc2, g2 = sv["mod"]
        f4, n4 = sv["wg"].shape[-1], sv["win"].shape[-1]
        hsp = lambda tk: ((tk, d), lambda j, k: (k, 0))
        fsp = lambda tk: ((None, tk, f4), lambda j, k: (j, k, 0))
        dgate, dup, df, dg2 = _ffn_bwd1(dx, sv["f"], g2, sv["wd"], sv["gate"], sv["up"])
        g_wd = _mm_tn(sv["act"], df, fsp, hsp, (f4, d), "grad_wd")
        g_wg = _mm_tn(dgate, sv["h2"], fsp, hsp, (f4, d), "grad_wg")
        g_wu = _mm_tn(dup, sv["h2"], fsp, hsp, (f4, d), "grad_wu")
        tie = send_grads(l, dict(wd=g_wd, wg=g_wg, wu=g_wu))
        dh2 = _ffn_bwd2(dgate, dup, sv["wg"], sv["wu"])
        dx1, sums2 = _lnmod_bwd(sv["x1"], ln2_g[l:l + 1], sc2 + tie, dh2, dx)
        dmo, da, db, dya, dyb, dp, dg1 = _out_bwd(dx1, sv["mo"], g1, sv["wo"], sv["pa"], sv["pb"], sv["p"],
                                                        sv["wa"], sv["wb"], d)
        g_wo = _mm_tn_square(sv["merged"], dmo, "grad_wo")
        g_wa = _mm_tn_square(sv["ya"], da, "grad_wa")
        g_wb = _mm_tn_square(sv["yb"], db, "grad_wb")
        tie = send_grads(l, dict(wo=g_wo, wa=g_wa, wb=g_wb))
        dp, dconv = _conv_bwd(sv["p"], conv_w[l] + tie, dyb, dp, d)
        dp, dgain = _attn_bwd(sv["p"], sv["qg2"], sv["kg2"], dya, sv["lt"], dp, d)
        g_win = _mm_tn(sv["h1"], dp, hsp, lambda tk: ((tk, n4), lambda j, k: (k, j)), (d, n4), "grad_win")
        tie = send_grads(l, dict(win=g_win))
        dh1 = _mm_in_bwd(dp, sv["win"])
        dx, sums1 = _lnmod_bwd(sv["x0"], ln1_g[l:l + 1], sc1 + tie, dh1, dx1)
        dgain = jnp.sum(dgain[:, 0:2, :], axis=0)
        dgain = dgain[:, :HEAD_DIM] + dgain[:, HEAD_DIM:]
        dmod = jnp.concatenate([sums1[0], sums1[1], dg1[0], sums2[0], sums2[1], dg2[0]])
        small[l] = dict(dmod=dmod, ln1=sums1[2], ln2=sums2[2], qg=dgain[0], kg=dgain[1], conv=dconv[0:3])
    return loss_tile, dx, small


def kernel(x, c, ada_w, ada_b, ln1_g, w_in, q_norm_g, k_norm_g, conv_w, w_branch_a, w_branch_b, w_out, ln2_g, w_ffn_gate, w_ffn_up, w_ffn_down, loss_target, m_ada_w, m_ada_b, m_ln1_g, m_w_in, m_q_norm_g, m_k_norm_g, m_conv_w, m_w_branch_a, m_w_branch_b, m_w_out, m_ln2_g, m_w_ffn_gate, m_w_ffn_up, m_w_ffn_down, v_ada_w, v_ada_b, v_ln1_g, v_w_in, v_q_norm_g, v_k_norm_g, v_conv_w, v_w_branch_a, v_w_branch_b, v_w_out, v_ln2_g, v_w_ffn_gate, v_w_ffn_up, v_w_ffn_down):
    n_l, d, a4 = ada_w.shape
    cw4 = conv_w.shape[-1]
    ix, iy, ic = lax.axis_index("x"), lax.axis_index("y"), lax.axis_index("c")
    chip = 2 * ix + iy
    me = 2 * chip + ic

    big_w = dict(win=w_in, wa=w_branch_a, wb=w_branch_b, wo=w_out, wg=w_ffn_gate, wu=w_ffn_up, wd=w_ffn_down)
    big_m = dict(win=m_w_in, wa=m_w_branch_a, wb=m_w_branch_b, wo=m_w_out, wg=m_w_ffn_gate, wu=m_w_ffn_up,
                 wd=m_w_ffn_down)
    big_v = dict(win=v_w_in, wa=v_w_branch_a, wb=v_w_branch_b, wo=v_w_out, wg=v_w_ffn_gate, wu=v_w_ffn_up,
                 wd=v_w_ffn_down)

    def adam_view(a, k):
        return jnp.swapaxes(a, 1, 2) if k in ("wg", "wu") else a

    got = _gather8(_pack([c, conv_w])).reshape(N_DEV, -1)

    weight_groups = [(l, names) for l in range(n_l) for names in (("win",), ("wa", "wb", "wo", "wg", "wu", "wd"))]
    group_srcs = [[big_w[k][l].astype(BF16) for k in names] for l, names in weight_groups]
    started_w = {}

    def start_weights(gi):
        l, names = weight_groups[gi]
        copies = _weight_half_copies if gi == 0 else _weight_copies
        st = _split_start("weights_start_%d" % gi, copies, group_srcs[gi],
                          [(N_CHIP,) + sh.shape for sh in group_srcs[gi]], 3)
        for k in names:
            started_w[(l, k)] = [gi, names, st, None, copies]
        return st[4]

    got, group_srcs[0] = lax.optimization_barrier((got, group_srcs[0]))
    tie = start_weights(0)[0, 0]
    c_all = got[:, :d]
    conv_all = got[:, d:d + n_l * 3 * cw4].reshape(N_CHIP, 2, n_l, 3, cw4)[:, 0]
    conv_full = jnp.transpose(conv_all, (1, 2, 0, 3)).reshape(n_l, 3, N_CHIP * cw4)
    b_cols = lax.dynamic_slice_in_dim(ada_b, chip * a4, a4, axis=1).reshape(n_l, 1, a4)
    b_cols, group_srcs[1:] = lax.optimization_barrier((b_cols + tie, group_srcs[1:]))
    mod_cols = _ada_mod(c_all, ada_w, b_cols)
    mod_all = _gather8(_pack([mod_cols])).reshape(N_DEV, -1)[:, :n_l * N_DEV * a4]
    mod_all = mod_all.reshape(N_CHIP, 2, n_l, N_DEV, a4)[:, 0]
    mods = lax.dynamic_index_in_dim(mod_all, me, axis=2, keepdims=False)
    mods = jnp.transpose(mods, (1, 0, 2)).reshape(n_l, N_CHIP * a4)

    def weights(l, names, after):
        entry, tie = started_w[(l, names[0])], jnp.zeros((), F32)
        if entry[3] is None:
            lands = _split_wait("weights_wait_%d" % entry[0], entry[4], entry[2], after)
            if entry[4] is _weight_half_copies:
                passed = _split_start_in_place("weights_pass_start_%d" % entry[0], _weight_half_pass, lands, 3)
                lands = _split_wait_in_place("weights_pass_wait_%d" % entry[0], _weight_half_pass, passed, passed[3])
            nxt = entry[0] + 1
            if nxt < len(weight_groups):
                lands, group_srcs[nxt] = lax.optimization_barrier((lands, group_srcs[nxt]))
                tie = start_weights(nxt)[0, 0]
            lands = [lax.dynamic_update_index_in_dim(land, own, chip, 0) for land, own in zip(lands, entry[2][2])]
            for k in entry[1]:
                started_w[(l, k)][3] = dict(zip(entry[1], lands))
        return [started_w[(l, k)][3][k] for k in names], tie

    started_g, held_back = [], []

    def start_grads(l, grads, copies=_grad_copies, sems_per=7):
        names = tuple(grads)
        st = _split_start("grads_start_%d" % len(started_g), copies, [grads[k] for k in names],
                          [(N_DEV,) + grads[k].shape[1:] for k in names], sems_per)
        started_g.append((l, names, st, copies))
        return st[4][0, 0]

    def send_grads(l, grads):
        if l == 0 and tuple(grads) == GRAD_GROUPS[-1]:
            held_back.append(grads)
            return jnp.zeros((), F32)
        return start_grads(l, grads)

    loss_tile, grad_x, small = _local_step(
        x[0], loss_target[0], mods, ln1_g, ln2_g, q_norm_g, k_norm_g, conv_full, weights, send_grads)

    sm_shapes = [(n_l, 6 * d), (n_l, d), (n_l, d), (n_l, HEAD_DIM), (n_l, HEAD_DIM), (n_l, 3, d), (1,)]
    vec = _pack([jnp.stack([small[l][k] for l in range(n_l)]) for k in ("dmod", "ln1", "ln2", "qg", "kg", "conv")]
                + [loss_tile[0, 0:1]])
    n_vec = vec.shape[1] * 8
    all_vec = _gather8(vec).reshape(N_DEV, n_vec)
    all_vec, held_back = lax.optimization_barrier((all_vec, held_back))
    tie = sum([start_grads(0, grads, _grad_copies_same_core, 4) for grads in held_back], jnp.zeros((), F32))
    per_dev = [_unpack(all_vec[dev], sm_shapes) for dev in range(N_DEV)]
    dmod_all = jnp.stack([pd[0] for pd in per_dev])
    dmod_cols = jnp.transpose(lax.dynamic_slice_in_dim(dmod_all, chip * a4, a4, axis=2), (1, 0, 2))
    ada_out = _ada_grad_adam(jnp.transpose(c_all) + tie, dmod_cols, ada_w, m_ada_w, v_ada_w)

    dev_parts = jnp.stack([
        _pack([pd[0], pd[1], pd[2], pd[3], pd[4], lax.dynamic_slice_in_dim(pd[5], chip * cw4, cw4, axis=2), pd[6]])
        for pd in per_dev])
    zero1 = jnp.zeros((1,), F32)
    sw = _pack([ada_b, ln1_g, ln2_g, q_norm_g, k_norm_g, conv_w, zero1])
    sm = _pack([m_ada_b, m_ln1_g, m_ln2_g, m_q_norm_g, m_k_norm_g, m_conv_w, zero1])
    sv = _pack([v_ada_b, v_ln1_g, v_ln2_g, v_q_norm_g, v_k_norm_g, v_conv_w, zero1 + 1.0])
    out_shapes = [(n_l, 6 * d), (n_l, d), (n_l, d), (n_l, HEAD_DIM), (n_l, HEAD_DIM), (n_l, 3, cw4), (1,)]
    sm_out = [_unpack(o, out_shapes) for o in _small_adam(dev_parts, sw, sm, sv)]
    loss = 0.5 * sm_out[0][6][0] / d

    after = jnp.full((8, LANES), tie + sm_out[0][0][0, 0] + ada_out[0][0, 0, 0])
    big_out = {}
    for names in GRAD_GROUPS:
        got_parts = {}
        for gi, (l, sent, st, copies) in enumerate(started_g):
            if sent == names:
                parts = _split_wait("grads_wait_%d" % gi, copies, st, after)
                if copies is _grad_copies_same_core:
                    passed = _split_start_in_place("grads_pass_start_%d" % gi, _grad_pass_copies, parts, 3)
                    parts = _split_wait_in_place("grads_pass_wait_%d" % gi, _grad_pass_copies, passed, passed[3])
                for k, part, grad in zip(sent, parts, st[2]):
                    own = lax.dynamic_index_in_dim(grad, chip, 0, keepdims=False)
                    got_parts[(l, k)] = lax.dynamic_update_index_in_dim(part, own, me, 0)
        for k in names:
            res = _sum_adam([got_parts[(l, k)] for l in range(n_l)], adam_view(big_w[k], k), adam_view(big_m[k], k),
                            adam_view(big_v[k], k), "sum_adam_" + k)
            after = res[0]
            big_out[k] = [adam_view(r, k) for r in res]

    outs = [loss, grad_x[None]]
    for kind in range(4):
        sm_k = sm_out[kind]
        outs += [ada_out[kind], sm_k[0], sm_k[1], big_out["win"][kind], sm_k[3], sm_k[4], sm_k[5],
                 big_out["wa"][kind], big_out["wb"][kind], big_out["wo"][kind], sm_k[2],
                 big_out["wg"][kind], big_out["wu"][kind], big_out["wd"][kind]]
    return tuple(outs)
```

```python
import math

import jax
import jax.numpy as jnp
from jax import lax
from jax.experimental import pallas as pl
from jax.experimental.pallas import tpu as pltpu

F32 = jnp.float32
BF16 = jnp.bfloat16
MESH_ID = pl.DeviceIdType.MESH

EPS = 1e-6
HEAD_DIM = 64
Q_BLOCK = 128
Q_SUPER = 1024
Q_SUPER_BWD = 1024
KEY_UNROLL = 4
LANES = 128
N_DEV = 8
N_CHIP = 4
VMEM_LIMIT_BYTES = 56 * 1024 * 1024

ADAM_LR = 0.001
ADAM_B1 = 0.9
ADAM_B2 = 0.999
ADAM_EPS = 1e-08
ADAM_WD = 0.01
ADAM_STEP = 10

HBM_SPEC = pl.BlockSpec(memory_space=pltpu.HBM)
ANY_SPEC = pl.BlockSpec(memory_space=pl.ANY)
SEM_SPEC = pl.BlockSpec(memory_space=pltpu.SEMAPHORE)
VMEM_SPEC = pl.BlockSpec(memory_space=pltpu.VMEM)
SIDE_EFFECT = pltpu.SideEffectType.DATAFLOW_SIDE_EFFECTING


def _params(*sem):
    return pltpu.CompilerParams(dimension_semantics=tuple(sem), vmem_limit_bytes=VMEM_LIMIT_BYTES)


def _tile(n, pref):
    return pref if n % pref == 0 else n


def _dot(a, b):
    return jnp.dot(a, b, preferred_element_type=F32)


def _dot_nt(a, b):
    return lax.dot_general(a, b, (((1,), (1,)), ((), ())), preferred_element_type=F32)


def _dot_tn(a, b):
    return lax.dot_general(a, b, (((0,), (0,)), ((), ())), preferred_element_type=F32)


def _adamw(w, g, m, v):
    m = ADAM_B1 * m + (1.0 - ADAM_B1) * g
    v = ADAM_B2 * v + (1.0 - ADAM_B2) * (g * g)
    m_hat = m / (1.0 - ADAM_B1 ** ADAM_STEP)
    v_hat = v / (1.0 - ADAM_B2 ** ADAM_STEP)
    delta = -ADAM_LR * (m_hat / (jnp.sqrt(v_hat) + ADAM_EPS) + ADAM_WD * w)
    return delta, m, v


def _hbm(a):
    return pltpu.with_memory_space_constraint(a, pltpu.HBM)


def _peer(x, y, c, k):
    return (1 - x if k & 4 else x, 1 - y if k & 2 else y, 1 - c if k & 1 else c)


def _gather8(v):
    rows_per, m = v.shape

    def body(v_ref, out_ref, send_sems, recv_sems, local_sem):
        x, y, c = lax.axis_index("x"), lax.axis_index("y"), lax.axis_index("c")

        def rows(p):
            return out_ref.at[pl.ds((4 * p[0] + 2 * p[1] + p[2]) * rows_per, rows_per), :]

        me = (x, y, c)
        mine = pltpu.make_async_copy(v_ref, rows(me), local_sem)
        mine.start()
        sends = []
        for k in range(1, N_DEV):
            cp = pltpu.make_async_remote_copy(
                src_ref=v_ref, dst_ref=rows(me), send_sem=send_sems.at[k - 1], recv_sem=recv_sems.at[k - 1],
                device_id=_peer(x, y, c, k), device_id_type=MESH_ID)
            cp.start()
            sends.append(cp)
        for k in range(1, N_DEV):
            pltpu.make_async_remote_copy(
                src_ref=v_ref, dst_ref=rows(_peer(x, y, c, k)), send_sem=send_sems.at[k - 1],
                recv_sem=recv_sems.at[k - 1], device_id=_peer(x, y, c, k), device_id_type=MESH_ID).wait_recv()
        for cp in sends:
            cp.wait_send()
        mine.wait()

    return pl.pallas_call(
        body, name="gather8",
        out_shape=jax.ShapeDtypeStruct((N_DEV * rows_per, m), v.dtype),
        in_specs=[VMEM_SPEC], out_specs=VMEM_SPEC,
        scratch_shapes=[pltpu.SemaphoreType.DMA((N_DEV - 1,)), pltpu.SemaphoreType.DMA((N_DEV - 1,)),
                        pltpu.SemaphoreType.DMA],
    )(v)


def _weight_copies(srcs, lands, send_sems, recv_sems):
    x, y, c = lax.axis_index("x"), lax.axis_index("y"), lax.axis_index("c")
    chips = [(1 - x, y), (x, 1 - y), (1 - x, 1 - y)]
    sends, recvs = [], []
    for a, (src, land) in enumerate(zip(srcs, lands)):
        for j, (px, py) in enumerate(chips):
            def copy(dst_block, a=a, j=j, px=px, py=py, src=src, land=land):
                return pltpu.make_async_remote_copy(
                    src_ref=src, dst_ref=land.at[dst_block], send_sem=send_sems.at[3 * a + j],
                    recv_sem=recv_sems.at[3 * a + j], device_id=(px, py, c), device_id_type=MESH_ID)
            sends.append(copy(2 * x + y))
            recvs.append(copy(2 * px + py))
    return sends, recvs


def _weight_half_copies(srcs, lands, send_sems, recv_sems):
    x, y, c = lax.axis_index("x"), lax.axis_index("y"), lax.axis_index("c")
    chips = [(1 - x, y), (x, 1 - y), (1 - x, 1 - y)]
    sends, recvs = [], []
    for a, (src, land) in enumerate(zip(srcs, lands)):
        half = src.shape[0] // 2
        rows = pl.ds(c * half, half)
        for j, (px, py) in enumerate(chips):
            def copy(dst_block, a=a, j=j, px=px, py=py, src=src, land=land, rows=rows):
                return pltpu.make_async_remote_copy(
                    src_ref=src.at[rows], dst_ref=land.at[dst_block, rows], send_sem=send_sems.at[3 * a + j],
                    recv_sem=recv_sems.at[3 * a + j], device_id=(px, py, c), device_id_type=MESH_ID)
            sends.append(copy(2 * x + y))
            recvs.append(copy(2 * px + py))
    return sends, recvs


def _weight_half_pass(lands, same_lands, send_sems, recv_sems):
    del same_lands
    x, y, c = lax.axis_index("x"), lax.axis_index("y"), lax.axis_index("c")
    chips = [(1 - x, y), (x, 1 - y), (1 - x, 1 - y)]
    sends, recvs = [], []
    for a, land in enumerate(lands):
        half = land.shape[1] // 2
        for j, (px, py) in enumerate(chips):
            def copy(pc, a=a, j=j, px=px, py=py, land=land, half=half):
                part = land.at[2 * px + py, pl.ds(pc * half, half)]
                return pltpu.make_async_remote_copy(
                    src_ref=part, dst_ref=part, send_sem=send_sems.at[3 * a + j], recv_sem=recv_sems.at[3 * a + j],
                    device_id=(x, y, 1 - c), device_id_type=MESH_ID)
            sends.append(copy(c))
            recvs.append(copy(1 - c))
    return sends, recvs


def _split_start(name, copies, srcs, land_shapes, sems_per_src):
    n = len(srcs)

    def body(*refs):
        sends, _ = copies(refs[:n], refs[n + 2:2 * n + 2], refs[n], refs[n + 1])
        for cp in sends:
            cp.start()
        token = refs[-1]
        token[...] = jnp.zeros_like(token)

    n_sems = sems_per_src * n
    outs = pl.pallas_call(
        body, name=name,
        out_shape=(pltpu.SemaphoreType.DMA((n_sems,)), pltpu.SemaphoreType.DMA((n_sems,)),
                   *[pltpu.HBM(shape, a.dtype) for a, shape in zip(srcs, land_shapes)],
                   jax.ShapeDtypeStruct((8, LANES), F32)),
        in_specs=[HBM_SPEC] * n, out_specs=(SEM_SPEC, SEM_SPEC, *[HBM_SPEC] * n, VMEM_SPEC),
        compiler_params=pltpu.CompilerParams(has_side_effects=SIDE_EFFECT),
    )(*[_hbm(a) for a in srcs])
    return outs[0], outs[1], list(srcs), list(outs[2:2 + n]), outs[-1]


def _split_wait(name, copies, started, after):
    send_sems, recv_sems, srcs, lands, _ = started
    n = len(srcs)

    def body(*refs):
        sends, recvs = copies(refs[:n], refs[n:2 * n], refs[2 * n], refs[2 * n + 1])
        for cp in sends:
            cp.wait_send()
        for cp in recvs:
            cp.wait_recv()

    return pl.pallas_call(
        body, name=name,
        out_shape=tuple(pltpu.HBM(a.shape, a.dtype) for a in lands),
        in_specs=[HBM_SPEC] * (2 * n) + [SEM_SPEC, SEM_SPEC, ANY_SPEC], out_specs=tuple([HBM_SPEC] * n),
        input_output_aliases={n + i: i for i in range(n)},
        compiler_params=pltpu.CompilerParams(has_side_effects=SIDE_EFFECT),
    )(*srcs, *lands, send_sems, recv_sems, after)


def _split_start_in_place(name, copies, bufs, sems_per_buf):
    n = len(bufs)

    def body(*refs):
        sends, _ = copies(refs[:n], refs[:n], refs[n], refs[n + 1])
        for cp in sends:
            cp.start()
        token = refs[-1]
        token[...] = jnp.zeros_like(token)

    n_sems = sems_per_buf * n
    outs = pl.pallas_call(
        body, name=name,
        out_shape=(pltpu.SemaphoreType.DMA((n_sems,)), pltpu.SemaphoreType.DMA((n_sems,)),
                   *[pltpu.HBM(a.shape, a.dtype) for a in bufs], jax.ShapeDtypeStruct((8, LANES), F32)),
        in_specs=[HBM_SPEC] * n, out_specs=(SEM_SPEC, SEM_SPEC, *[HBM_SPEC] * n, VMEM_SPEC),
        input_output_aliases={i: 2 + i for i in range(n)},
        compiler_params=pltpu.CompilerParams(has_side_effects=SIDE_EFFECT),
    )(*[_hbm(a) for a in bufs])
    return outs[0], outs[1], list(outs[2:2 + n]), outs[-1]


def _split_wait_in_place(name, copies, started, after):
    send_sems, recv_sems, bufs, _ = started
    n = len(bufs)

    def body(*refs):
        sends, recvs = copies(refs[:n], refs[:n], refs[n], refs[n + 1])
        for cp in sends:
            cp.wait_send()
        for cp in recvs:
            cp.wait_recv()

    return pl.pallas_call(
        body, name=name,
        out_shape=tuple(pltpu.HBM(a.shape, a.dtype) for a in bufs),
        in_specs=[HBM_SPEC] * n + [SEM_SPEC, SEM_SPEC, ANY_SPEC], out_specs=tuple([HBM_SPEC] * n),
        input_output_aliases={i: i for i in range(n)},
        compiler_params=pltpu.CompilerParams(has_side_effects=SIDE_EFFECT),
    )(*bufs, send_sems, recv_sems, after)


def _grad_copies(grads, parts, send_sems, recv_sems):
    x, y, c = lax.axis_index("x"), lax.axis_index("y"), lax.axis_index("c")
    chips = [(1 - x, y), (x, 1 - y), (1 - x, 1 - y)]
    my_slot = 4 * x + 2 * y + c
    sends, recvs = [], []
    for a, (grad, part) in enumerate(zip(grads, parts)):
        def copy(k, block, slot, to, a=a, grad=grad, part=part):
            return pltpu.make_async_remote_copy(
                src_ref=grad.at[block], dst_ref=part.at[slot], send_sem=send_sems.at[7 * a + k],
                recv_sem=recv_sems.at[7 * a + k], device_id=to, device_id_type=MESH_ID)
        sends.append(copy(0, 2 * x + y, my_slot, (x, y, 1 - c)))
        recvs.append(copy(0, 2 * x + y, 4 * x + 2 * y + (1 - c), (x, y, 1 - c)))
        for j, (px, py) in enumerate(chips):
            for other, pc in enumerate((c, 1 - c)):
                sends.append(copy(1 + 2 * j + other, 2 * px + py, my_slot, (px, py, pc)))
                recvs.append(copy(1 + 2 * j + other, 2 * x + y, 4 * px + 2 * py + pc, (px, py, pc)))
    return sends, recvs


def _grad_copies_same_core(grads, parts, send_sems, recv_sems):
    x, y, c = lax.axis_index("x"), lax.axis_index("y"), lax.axis_index("c")
    chips = [(1 - x, y), (x, 1 - y), (1 - x, 1 - y)]
    my_slot = 4 * x + 2 * y + c
    sends, recvs = [], []
    for a, (grad, part) in enumerate(zip(grads, parts)):
        def copy(k, block, slot, to, a=a, grad=grad, part=part):
            return pltpu.make_async_remote_copy(
                src_ref=grad.at[block], dst_ref=part.at[slot], send_sem=send_sems.at[4 * a + k],
                recv_sem=recv_sems.at[4 * a + k], device_id=to, device_id_type=MESH_ID)
        sends.append(copy(0, 2 * x + y, my_slot, (x, y, 1 - c)))
        recvs.append(copy(0, 2 * x + y, 4 * x + 2 * y + (1 - c), (x, y, 1 - c)))
        for j, (px, py) in enumerate(chips):
            sends.append(copy(1 + j, 2 * px + py, my_slot, (px, py, c)))
            recvs.append(copy(1 + j, 2 * x + y, 4 * px + 2 * py + c, (px, py, c)))
    return sends, recvs


def _grad_pass_copies(parts, same_parts, send_sems, recv_sems):
    del same_parts
    x, y, c = lax.axis_index("x"), lax.axis_index("y"), lax.axis_index("c")
    chips = [(1 - x, y), (x, 1 - y), (1 - x, 1 - y)]
    sends, recvs = [], []
    for a, part in enumerate(parts):
        for j, (px, py) in enumerate(chips):
            def copy(pc, a=a, j=j, px=px, py=py, part=part):
                slot = part.at[4 * px + 2 * py + pc]
                return pltpu.make_async_remote_copy(
                    src_ref=slot, dst_ref=slot, send_sem=send_sems.at[3 * a + j], recv_sem=recv_sems.at[3 * a + j],
                    device_id=(x, y, 1 - c), device_id_type=MESH_ID)
            sends.append(copy(c))
            recvs.append(copy(1 - c))
    return sends, recvs


def _ada_mod(c_all, ada_w, ada_b_cols):
    n_l, d, a4 = ada_w.shape
    tn = _tile(a4, 512)

    def body(c_ref, w_ref, b_ref, o_ref):
        cv = c_ref[...]
        ca = (cv * jax.nn.sigmoid(cv)).astype(BF16)
        o_ref[...] = _dot(ca, w_ref[...].astype(BF16)) + b_ref[...]

    return pl.pallas_call(
        body, name="ada_mod", grid=(n_l, a4 // tn),
        in_specs=[pl.BlockSpec((N_DEV, d), lambda l, j: (0, 0)),
                  pl.BlockSpec((None, d, tn), lambda l, j: (l, 0, j)),
                  pl.BlockSpec((None, 1, tn), lambda l, j: (l, 0, j))],
        out_specs=pl.BlockSpec((None, N_DEV, tn), lambda l, j: (l, 0, j)),
        out_shape=jax.ShapeDtypeStruct((n_l, N_DEV, a4), F32),
        compiler_params=_params("parallel", "parallel"),
    )(c_all, ada_w, ada_b_cols)


def _ada_grad_adam(c_all_t, dmod_cols, w, m, v):
    n_l, d, a4 = w.shape
    tn = _tile(a4, 512)

    def body(ct_ref, dm_ref, w_ref, m_ref, v_ref, g_ref, dl_ref, nm_ref, nv_ref):
        ct = ct_ref[...]
        ca = ct * jax.nn.sigmoid(ct)
        dm = dm_ref[...]
        g = ca[:, 0:1] * dm[0:1, :]
        for dev in range(1, N_DEV):
            g = g + ca[:, dev:dev + 1] * dm[dev:dev + 1, :]
        g_ref[...] = g
        delta, nm, nv = _adamw(w_ref[...], g, m_ref[...], v_ref[...])
        dl_ref[...] = delta
        nm_ref[...] = nm
        nv_ref[...] = nv

    wspec = pl.BlockSpec((None, d, tn), lambda l, j: (l, 0, j))
    shp = jax.ShapeDtypeStruct(w.shape, F32)
    return pl.pallas_call(
        body, name="ada_grad_adam", grid=(n_l, a4 // tn),
        in_specs=[pl.BlockSpec((d, N_DEV), lambda l, j: (0, 0)),
                  pl.BlockSpec((None, N_DEV, tn), lambda l, j: (l, 0, j)), wspec, wspec, wspec],
        out_specs=[wspec] * 4, out_shape=[shp] * 4,
        compiler_params=_params("parallel", "parallel"),
    )(c_all_t, dmod_cols, w, m, v)


def _lnmod(x, g, sc, sh):
    s, d = x.shape
    tm = _tile(s, 512)

    def body(x_ref, g_ref, sc_ref, sh_ref, h_ref):
        xv = x_ref[...]
        r = lax.rsqrt(jnp.mean(xv * xv, axis=-1, keepdims=True) + EPS)
        h_ref[...] = ((xv * r * g_ref[...]) * (1.0 + sc_ref[...]) + sh_ref[...]).astype(BF16)

    vec = pl.BlockSpec((1, d), lambda i: (0, 0))
    row = pl.BlockSpec((tm, d), lambda i: (i, 0))
    return pl.pallas_call(
        body, name="lnmod", grid=(s // tm,), in_specs=[row, vec, vec, vec], out_specs=row,
        out_shape=jax.ShapeDtypeStruct((s, d), BF16), compiler_params=_params("parallel"),
    )(x, g, sc, sh)


def _mm_in(h, w_g):
    s, d = h.shape
    n4 = w_g.shape[-1]
    tm = _tile(s, 512)

    def body(a_ref, b_ref, o_ref):
        o_ref[...] = _dot(a_ref[...], b_ref[...])

    return pl.pallas_call(
        body, name="mm_in", grid=(N_CHIP, s // tm),
        in_specs=[pl.BlockSpec((tm, d), lambda j, i: (i, 0)),
                  pl.BlockSpec((None, d, n4), lambda j, i: (j, 0, 0))],
        out_specs=pl.BlockSpec((tm, n4), lambda j, i: (i, j)),
        out_shape=jax.ShapeDtypeStruct((s, N_CHIP * n4), F32),
        compiler_params=_params("parallel", "parallel"),
    )(h, w_g)


def _pair_mean(x, low):
    lo = jnp.sum(jnp.where(low, x, 0.0), axis=-1, keepdims=True)
    hi = jnp.sum(jnp.where(low, 0.0, x), axis=-1, keepdims=True)
    return jnp.where(low, lo, hi) * (1.0 / HEAD_DIM)


def _pair_norm(x, low):
    r = lax.rsqrt(_pair_mean(x * x, low) + EPS)
    return x * r, r


def _log_not(z):
    nz = -z
    return jnp.minimum(nz, 0.0) - jnp.log(1.0 + jnp.exp(jnp.minimum(z, nz)))


def _attn_consts(inclusive):
    low = lax.broadcasted_iota(jnp.int32, (1, LANES), 1) < HEAD_DIM
    row = lax.broadcasted_iota(jnp.int32, (Q_BLOCK, Q_BLOCK), 0)
    col = lax.broadcasted_iota(jnp.int32, (Q_BLOCK, Q_BLOCK), 1)
    tri = (row <= col) if inclusive else (row > col)
    w2 = jnp.concatenate([tri.astype(BF16), jnp.ones((Q_BLOCK, Q_BLOCK), BF16)], axis=1)
    return low, col < row, jnp.concatenate([w2, w2], axis=0)


def _split_cat(v):
    hi = v.astype(BF16)
    return jnp.concatenate([hi, (v - hi.astype(F32)).astype(BF16)], axis=1)


def _fill_pair_blocks(dst, src_fn, low, n_kb):
    def fill(b, _):
        v = src_fn(pl.ds(pl.multiple_of(b * Q_BLOCK, Q_BLOCK), Q_BLOCK))
        dst[b, 0:Q_BLOCK, :] = jnp.where(low, v, 0.0).astype(BF16)
        dst[b, Q_BLOCK:2 * Q_BLOCK, :] = jnp.where(low, 0.0, v).astype(BF16)
        return 0

    lax.fori_loop(0, n_kb, fill, 0)


def _attn_fwd(p, qg2, kg2, d):
    s = p.shape[0]
    n_pairs = d // LANES
    qsb = _tile(s, Q_SUPER)
    n_sub, n_sb, n_kb = qsb // Q_BLOCK, s // qsb, s // Q_BLOCK
    unroll = math.gcd(KEY_UNROLL, n_sub)
    chunk = _tile(s, 512)
    inv_sqrt = 1.0 / math.sqrt(HEAD_DIM)

    def body(q_ref, k_ref, v_ref, qg_ref, kg_ref, o_ref, lt_ref, qs, k2, v2, run, acc):
        low, causal, w4 = _attn_consts(False)

        def prep(r, _):
            rows = pl.ds(pl.multiple_of(r * chunk, chunk), chunk)
            qs[rows, :] = (_pair_norm(q_ref[rows, :], low)[0] * (qg_ref[...] * inv_sqrt)).astype(BF16)
            return 0

        lax.fori_loop(0, s // chunk, prep, 0)
        _fill_pair_blocks(k2, lambda rows: _pair_norm(k_ref[rows, :], low)[0] * kg_ref[...], low, n_kb)
        _fill_pair_blocks(v2, lambda rows: v_ref[rows, :], low, n_kb)

        def step(sb, j, t0=0, diag_t=None):
            rows = pl.ds(pl.multiple_of(sb * qsb + t0 * Q_BLOCK, Q_BLOCK), (n_sub - t0) * Q_BLOCK)
            z_both = _dot_nt(qs[rows, :], k2[j])
            zls, cats = [], []
            for t in range(t0, n_sub):
                sub = slice((t - t0) * Q_BLOCK, (t - t0 + 1) * Q_BLOCK)
                for h in range(2):
                    z = z_both[sub, h * LANES:(h + 1) * LANES]
                    ln = _log_not(z)
                    if t == diag_t:
                        ln = jnp.where(causal, ln, 0.0)
                    zls.append(z + ln)
                    cats.append(_split_cat(ln))
            c2 = _dot(jnp.concatenate(cats, axis=0), w4)
            a_rows = []
            for t in range(t0, n_sub):
                sub = slice(t * Q_BLOCK, (t + 1) * Q_BLOCK)
                a_pair = []
                for h in range(2):
                    i = 2 * (t - t0) + h
                    tile = slice(i * Q_BLOCK, (i + 1) * Q_BLOCK)
                    later = run[h, sub, :]
                    log_a = zls[i] + c2[tile, :LANES] + later
                    if t == diag_t:
                        log_a = jnp.where(causal, log_a, -1e30)
                    a_pair.append(jnp.exp(log_a).astype(BF16))
                    run[h, sub, :] = later + c2[tile, LANES:]
                a_rows.append(jnp.concatenate(a_pair, axis=1))
            acc[t0 * Q_BLOCK:, :] += _dot(jnp.concatenate(a_rows, axis=0), v2[j])

        def super_block(sb, _):
            run[...] = jnp.zeros_like(run)
            acc[...] = jnp.zeros_like(acc)
            for t in reversed(range(n_sub)):
                step(sb, sb * n_sub + t, t0=t, diag_t=t)

            def below(n, _):
                for u in range(unroll):
                    step(sb, sb * n_sub - 1 - (unroll * n + u))
                return 0

            lax.fori_loop(0, sb * (n_sub // unroll), below, 0)
            rows_sb = pl.ds(pl.multiple_of(sb * qsb, qsb), qsb)
            o_ref[rows_sb, :] = acc[...].astype(BF16)
            lt_ref[rows_sb, :] = jnp.where(low, run[0], run[1])
            return 0

        lax.fori_loop(0, n_sb, super_block, 0)

    def seg(k):
        return pl.BlockSpec((s, LANES), lambda h, k=k: (0, k * n_pairs + h))

    vec = pl.BlockSpec((1, LANES), lambda h: (0, 0))
    out = pl.BlockSpec((s, LANES), lambda h: (0, h))
    return pl.pallas_call(
        body, name="attn_fwd", grid=(n_pairs,),
        in_specs=[seg(0), seg(1), seg(2), vec, vec], out_specs=[out, out],
        out_shape=[jax.ShapeDtypeStruct((s, d), BF16), jax.ShapeDtypeStruct((s, d), F32)],
        scratch_shapes=[pltpu.VMEM((s, LANES), BF16)] + [pltpu.VMEM((n_kb, 2 * Q_BLOCK, LANES), BF16)] * 2
        + [pltpu.VMEM((2, qsb, LANES), F32), pltpu.VMEM((qsb, LANES), F32)],
        compiler_params=_params("parallel"),
    )(p, p, p, qg2, kg2)


def _conv_rows(s):
    return _tile(s, 512)


def _conv_fwd(p, conv_w, d):
    s = p.shape[0]
    nb = d // LANES
    rows_n = _conv_rows(s)

    def body(cb_ref, cc_ref, cx_ref, w_ref, y_ref, us):
        us[pl.ds(0, 8), :] = jnp.zeros((8, LANES), F32)

        def fill(r, _):
            rows = pl.ds(pl.multiple_of(r * rows_n, rows_n), rows_n)
            us[pl.ds(pl.multiple_of(r * rows_n + 8, 8), rows_n), :] = cc_ref[rows, :] * cx_ref[rows, :]
            return 0

        lax.fori_loop(0, s // rows_n, fill, 0)
        w = w_ref[...]

        def out(r, _):
            rows = pl.ds(pl.multiple_of(r * rows_n, rows_n), rows_n)
            ext = us[pl.ds(pl.multiple_of(r * rows_n, 8), rows_n + 8), :]
            cv = (w[0:1, :] * pltpu.roll(ext, 2, 0)[8:, :] + w[1:2, :] * pltpu.roll(ext, 1, 0)[8:, :]
                  + w[2:3, :] * ext[8:, :])
            y_ref[rows, :] = (cb_ref[rows, :] * cv).astype(BF16)
            return 0

        lax.fori_loop(0, s // rows_n, out, 0)

    def seg(k):
        return pl.BlockSpec((s, LANES), lambda b, k=k: (0, k * nb + b))

    return pl.pallas_call(
        body, name="conv_fwd", grid=(nb,),
        in_specs=[seg(3), seg(4), seg(5), pl.BlockSpec((3, LANES), lambda b: (0, b))],
        out_specs=pl.BlockSpec((s, LANES), lambda b: (0, b)),
        out_shape=jax.ShapeDtypeStruct((s, d), BF16),
        scratch_shapes=[pltpu.VMEM((s + 8, LANES), F32)],
        compiler_params=_params("parallel"),
    )(p, p, p, conv_w)


def _branch(ya, yb, p, wa, wb, d):
    s = ya.shape[0]
    tm = _tile(s, 512)

    def body(ya_ref, yb_ref, ga_ref, gb_ref, wa_ref, wb_ref, m_ref, a_ref, b_ref):
        pa = _dot(ya_ref[...], wa_ref[...])
        pb = _dot(yb_ref[...], wb_ref[...])
        m_ref[...] = (jax.nn.sigmoid(ga_ref[...]) * pa + jax.nn.sigmoid(gb_ref[...]) * pb).astype(BF16)
        a_ref[...] = pa.astype(BF16)
        b_ref[...] = pb.astype(BF16)

    row = pl.BlockSpec((tm, d), lambda i: (i, 0))
    wsp = pl.BlockSpec((d, d), lambda i: (0, 0))
    shp = jax.ShapeDtypeStruct((s, d), BF16)
    return pl.pallas_call(
        body, name="branch", grid=(s // tm,),
        in_specs=[row, row, pl.BlockSpec((tm, d), lambda i: (i, 6)), pl.BlockSpec((tm, d), lambda i: (i, 7)), wsp, wsp],
        out_specs=[row, row, row], out_shape=[shp, shp, shp], compiler_params=_params("parallel"),
    )(ya, yb, p, p, wa, wb)


def _out_proj(merged, wout, x0, g1):
    s, d = x0.shape
    tm = _tile(s, 512)

    def body(m_ref, w_ref, x_ref, g_ref, x1_ref, mo_ref):
        mo = _dot(m_ref[...], w_ref[...])
        mo_ref[...] = mo
        x1_ref[...] = x_ref[...] + g_ref[...] * mo

    row = pl.BlockSpec((tm, d), lambda i: (i, 0))
    shp = jax.ShapeDtypeStruct((s, d), F32)
    return pl.pallas_call(
        body, name="out_proj", grid=(s // tm,),
        in_specs=[row, pl.BlockSpec((d, d), lambda i: (0, 0)), row, pl.BlockSpec((1, d), lambda i: (0, 0))],
        out_specs=[row, row], out_shape=[shp, shp], compiler_params=_params("parallel"),
    )(merged, wout, x0, g1)


def _ffn_up(h, wg_g, wu_g):
    s, d = h.shape
    f4 = wg_g.shape[-1]
    tm = _tile(s, 512)

    def body(h_ref, wg_ref, wu_ref, gate_ref, up_ref, act_ref):
        hv = h_ref[...]
        gt = _dot(hv, wg_ref[...])
        up = _dot(hv, wu_ref[...])
        gate_ref[...] = gt.astype(BF16)
        up_ref[...] = up.astype(BF16)
        act_ref[...] = (gt * jax.nn.sigmoid(gt) * up).astype(BF16)

    wsp = pl.BlockSpec((None, d, f4), lambda j, i: (j, 0, 0))
    osp = pl.BlockSpec((None, tm, f4), lambda j, i: (j, i, 0))
    shp = jax.ShapeDtypeStruct((N_CHIP, s, f4), BF16)
    return pl.pallas_call(
        body, name="ffn_up", grid=(N_CHIP, s // tm),
        in_specs=[pl.BlockSpec((tm, d), lambda j, i: (i, 0)), wsp, wsp],
        out_specs=[osp, osp, osp], out_shape=[shp, shp, shp], compiler_params=_params("parallel", "parallel"),
    )(h, wg_g, wu_g)


def _ffn_down(act, wd_g, x1, g2):
    s, d = x1.shape
    f4 = act.shape[-1]
    tm = _tile(s, 1024)

    def body(a_ref, w_ref, x_ref, g_ref, x2_ref, f_ref, acc):
        j = pl.program_id(1)

        @pl.when(j == 0)
        def _():
            acc[...] = jnp.zeros_like(acc)

        acc[...] += _dot(a_ref[...], w_ref[...])

        @pl.when(j == N_CHIP - 1)
        def _():
            f = acc[...]
            f_ref[...] = f
            x2_ref[...] = x_ref[...] + g_ref[...] * f

    row = pl.BlockSpec((tm, d), lambda i, j: (i, 0))
    shp = jax.ShapeDtypeStruct((s, d), F32)
    return pl.pallas_call(
        body, name="ffn_down", grid=(s // tm, N_CHIP),
        in_specs=[pl.BlockSpec((None, tm, f4), lambda i, j: (j, i, 0)),
                  pl.BlockSpec((None, f4, d), lambda i, j: (j, 0, 0)),
                  row, pl.BlockSpec((1, d), lambda i, j: (0, 0))],
        out_specs=[row, row], out_shape=[shp, shp],
        scratch_shapes=[pltpu.VMEM((tm, d), F32)], compiler_params=_params("parallel", "arbitrary"),
    )(act, wd_g, x1, g2)


def _loss_head(y, target):
    s, d = y.shape
    tm = _tile(s, 512)
    n_steps = s // tm

    def body(y_ref, t_ref, dy_ref, l_ref, acc):
        i = pl.program_id(0)

        @pl.when(i == 0)
        def _():
            acc[...] = jnp.zeros_like(acc)

        err = y_ref[...] - t_ref[...]
        dy_ref[...] = err / d
        acc[...] += jnp.sum(err * err, axis=0, keepdims=True)

        @pl.when(i == n_steps - 1)
        def _():
            l_ref[...] = jnp.broadcast_to(jnp.sum(acc[...], axis=1, keepdims=True), (8, LANES))

    row = pl.BlockSpec((tm, d), lambda i: (i, 0))
    return pl.pallas_call(
        body, name="loss_head", grid=(n_steps,), in_specs=[row, row],
        out_specs=[row, pl.BlockSpec((8, LANES), lambda i: (0, 0))],
        out_shape=[jax.ShapeDtypeStruct((s, d), F32), jax.ShapeDtypeStruct((8, LANES), F32)],
        scratch_shapes=[pltpu.VMEM((1, d), F32)], compiler_params=_params("arbitrary"),
    )(y, target)


def _mm_tn(a, b, a_spec, b_spec, out_rc, name):
    r, c = out_rc
    s = a.shape[-2]
    tk = _tile(s, 1024)
    nk = s // tk

    def body(a_ref, b_ref, o_ref, acc):
        k = pl.program_id(1)

        @pl.when(k == 0)
        def _():
            acc[...] = jnp.zeros_like(acc)

        acc[...] += _dot_tn(a_ref[...], b_ref[...])

        @pl.when(k == nk - 1)
        def _():
            o_ref[...] = acc[...].astype(BF16)

    return pl.pallas_call(
        body, name=name, grid=(N_CHIP, nk),
        in_specs=[pl.BlockSpec(*a_spec(tk)), pl.BlockSpec(*b_spec(tk))],
        out_specs=pl.BlockSpec((None, r, c), lambda j, k: (j, 0, 0)),
        out_shape=jax.ShapeDtypeStruct((N_CHIP, r, c), BF16),
        scratch_shapes=[pltpu.VMEM((r, c), F32)], compiler_params=_params("parallel", "arbitrary"),
    )(a, b)


def _mm_tn_square(a, b, name):
    s, d = a.shape
    r4 = d // N_CHIP
    tk = _tile(s, 1024)
    nk = s // tk

    def body(a_ref, b_ref, o_ref, acc):
        k = pl.program_id(0)

        @pl.when(k == 0)
        def _():
            acc[...] = jnp.zeros_like(acc)

        acc[...] += _dot_tn(a_ref[...], b_ref[...])

        @pl.when(k == nk - 1)
        def _():
            for j in range(N_CHIP):
                o_ref[j] = acc[j * r4:(j + 1) * r4, :].astype(BF16)

    blk = pl.BlockSpec((tk, d), lambda k: (k, 0))
    return pl.pallas_call(
        body, name=name, grid=(nk,), in_specs=[blk, blk],
        out_specs=pl.BlockSpec((N_CHIP, r4, d), lambda k: (0, 0, 0)),
        out_shape=jax.ShapeDtypeStruct((N_CHIP, r4, d), BF16),
        scratch_shapes=[pltpu.VMEM((d, d), F32)], compiler_params=_params("arbitrary"),
    )(a, b)


def _ffn_bwd1(dx2, f, g2, wd_g, gate, up):
    s, d = dx2.shape
    f4 = gate.shape[-1]
    tm = _tile(s, 512)

    def body(dx_ref, f_ref, g_ref, w_ref, gate_ref, up_ref, dgate_ref, dup_ref, df_ref, dg_ref):
        i, j = pl.program_id(0), pl.program_id(1)

        @pl.when((i == 0) & (j == 0))
        def _():
            dg_ref[...] = jnp.zeros_like(dg_ref)

        dxv = dx_ref[...]
        df = (g_ref[...] * dxv).astype(BF16)

        @pl.when(j == 0)
        def _():
            df_ref[...] = df
            dg_ref[0:1, :] += jnp.sum(dxv * f_ref[...], axis=0, keepdims=True)

        da = _dot_nt(df, w_ref[...])
        gt = gate_ref[...].astype(F32)
        sg = jax.nn.sigmoid(gt)
        dup_ref[...] = (da * gt * sg).astype(BF16)
        dgate_ref[...] = (da * up_ref[...].astype(F32) * (sg * (1.0 + gt * (1.0 - sg)))).astype(BF16)

    row = pl.BlockSpec((tm, d), lambda i, j: (i, 0))
    hsp = pl.BlockSpec((None, tm, f4), lambda i, j: (j, i, 0))
    hshp = jax.ShapeDtypeStruct((N_CHIP, s, f4), BF16)
    return pl.pallas_call(
        body, name="ffn_bwd1", grid=(s // tm, N_CHIP),
        in_specs=[row, row, pl.BlockSpec((1, d), lambda i, j: (0, 0)),
                  pl.BlockSpec((None, f4, d), lambda i, j: (j, 0, 0)), hsp, hsp],
        out_specs=[hsp, hsp, row, pl.BlockSpec((8, d), lambda i, j: (0, 0))],
        out_shape=[hshp, hshp, jax.ShapeDtypeStruct((s, d), BF16), jax.ShapeDtypeStruct((8, d), F32)],
        compiler_params=_params("arbitrary", "arbitrary"),
    )(dx2, f, g2, wd_g, gate, up)


def _ffn_bwd2(dgate, dup, wg_g, wu_g):
    _, s, f4 = dgate.shape
    d = wg_g.shape[-2]
    tm = _tile(s, 1024)

    def body(dg_ref, du_ref, wg_ref, wu_ref, o_ref, acc):
        j = pl.program_id(1)

        @pl.when(j == 0)
        def _():
            acc[...] = jnp.zeros_like(acc)

        acc[...] += _dot_nt(dg_ref[...], wg_ref[...]) + _dot_nt(du_ref[...], wu_ref[...])

        @pl.when(j == N_CHIP - 1)
        def _():
            o_ref[...] = acc[...]

    hsp = pl.BlockSpec((None, tm, f4), lambda i, j: (j, i, 0))
    wsp = pl.BlockSpec((None, d, f4), lambda i, j: (j, 0, 0))
    return pl.pallas_call(
        body, name="ffn_bwd2", grid=(s // tm, N_CHIP), in_specs=[hsp, hsp, wsp, wsp],
        out_specs=pl.BlockSpec((tm, d), lambda i, j: (i, 0)), out_shape=jax.ShapeDtypeStruct((s, d), F32),
        scratch_shapes=[pltpu.VMEM((tm, d), F32)], compiler_params=_params("parallel", "arbitrary"),
    )(dgate, dup, wg_g, wu_g)


def _lnmod_bwd(x, g, sc, dh, dres):
    s, d = x.shape
    tm = _tile(s, 512)

    def body(x_ref, g_ref, sc_ref, dh_ref, dr_ref, dx_ref, sums_ref):
        @pl.when(pl.program_id(0) == 0)
        def _():
            sums_ref[...] = jnp.zeros_like(sums_ref)

        xv, dhv, gv = x_ref[...], dh_ref[...], g_ref[...]
        r = lax.rsqrt(jnp.mean(xv * xv, axis=-1, keepdims=True) + EPS)
        n = xv * r
        one_sc = 1.0 + sc_ref[...]
        dt = dhv * one_sc
        sums_ref[0:1, :] += jnp.sum(dhv, axis=0, keepdims=True)
        sums_ref[1:2, :] += jnp.sum(dhv * (n * gv), axis=0, keepdims=True)
        sums_ref[2:3, :] += jnp.sum(dt * n, axis=0, keepdims=True)
        dn = dt * gv
        dx_ref[...] = dr_ref[...] + r * (dn - n * jnp.mean(dn * n, axis=-1, keepdims=True))

    vec = pl.BlockSpec((1, d), lambda i: (0, 0))
    row = pl.BlockSpec((tm, d), lambda i: (i, 0))
    return pl.pallas_call(
        body, name="lnmod_bwd", grid=(s // tm,), in_specs=[row, vec, vec, row, row],
        out_specs=[row, pl.BlockSpec((8, d), lambda i: (0, 0))],
        out_shape=[jax.ShapeDtypeStruct((s, d), F32), jax.ShapeDtypeStruct((8, d), F32)],
        compiler_params=_params("arbitrary"),
    )(x, g, sc, dh, dres)


def _out_bwd(dx1, mo, g1, wout, pa, pb, p, wa, wb, d):
    s = dx1.shape[0]
    tm = _tile(s, 256)

    def body(dx_ref, mo_ref, g_ref, wo_ref, pa_ref, pb_ref, ga_ref, gb_ref, wa_ref, wb_ref,
             dmo_ref, da_ref, db_ref, dya_ref, dyb_ref, dp_ref, dg_ref):
        @pl.when(pl.program_id(0) == 0)
        def _():
            dg_ref[...] = jnp.zeros_like(dg_ref)

        dxv = dx_ref[...]
        dg_ref[0:1, :] += jnp.sum(dxv * mo_ref[...], axis=0, keepdims=True)
        dmo = (g_ref[...] * dxv).astype(BF16)
        dmo_ref[...] = dmo
        dm = _dot_nt(dmo, wo_ref[...])
        sa, sb = jax.nn.sigmoid(ga_ref[...]), jax.nn.sigmoid(gb_ref[...])
        da = (dm * sa).astype(BF16)
        db = (dm * sb).astype(BF16)
        da_ref[...] = da
        db_ref[...] = db
        dp_ref[:, :d] = (dm * pa_ref[...].astype(F32) * (sa * (1.0 - sa))).astype(BF16)
        dp_ref[:, d:] = (dm * pb_ref[...].astype(F32) * (sb * (1.0 - sb))).astype(BF16)
        dya_ref[...] = _dot_nt(da, wa_ref[...]).astype(BF16)
        dyb_ref[...] = _dot_nt(db, wb_ref[...]).astype(BF16)

    row = pl.BlockSpec((tm, d), lambda i: (i, 0))
    wsp = pl.BlockSpec((d, d), lambda i: (0, 0))
    shp = jax.ShapeDtypeStruct((s, d), BF16)
    return pl.pallas_call(
        body, name="out_bwd", grid=(s // tm,),
        in_specs=[row, row, pl.BlockSpec((1, d), lambda i: (0, 0)), wsp, row, row,
                  pl.BlockSpec((tm, d), lambda i: (i, 6)), pl.BlockSpec((tm, d), lambda i: (i, 7)), wsp, wsp],
        out_specs=[row] * 5 + [pl.BlockSpec((tm, 2 * d), lambda i: (i, 3)), pl.BlockSpec((8, d), lambda i: (0, 0))],
        out_shape=[shp] * 5 + [jax.ShapeDtypeStruct((s, 8 * d), BF16), jax.ShapeDtypeStruct((8, d), F32)],
        compiler_params=_params("arbitrary"),
    )(dx1, mo, g1, wout, pa, pb, p, p, wa, wb)


def _store_segments(outs, dp_out, sems, col_blocks):
    copies = [pltpu.make_async_copy(outs.at[k], dp_out.at[:, pl.ds(pl.multiple_of(cb * LANES, LANES), LANES)],
                                    sems.at[k]) for k, cb in enumerate(col_blocks)]
    for cp in copies:
        cp.start()
    for cp in copies:
        cp.wait()


def _conv_bwd(p, conv_w, dyb, dp, d):
    s = p.shape[0]
    nb = d // LANES
    rows_n = _conv_rows(s)

    def compute(cb_ref, cc_ref, cx_ref, w_ref, dy_ref, dcb_ref, dcc_ref, dcx_ref, dw_ref, us, ds):
        us[pl.ds(0, 8), :] = jnp.zeros((8, LANES), F32)
        ds[pl.ds(s, 8), :] = jnp.zeros((8, LANES), F32)

        def fill(r, _):
            rows = pl.ds(pl.multiple_of(r * rows_n, rows_n), rows_n)
            us[pl.ds(pl.multiple_of(r * rows_n + 8, 8), rows_n), :] = cc_ref[rows, :] * cx_ref[rows, :]
            ds[rows, :] = dy_ref[rows, :].astype(F32) * cb_ref[rows, :]
            return 0

        lax.fori_loop(0, s // rows_n, fill, 0)
        w = w_ref[...]

        def out(r, carry):
            dw0, dw1, dw2 = carry
            rows = pl.ds(pl.multiple_of(r * rows_n, rows_n), rows_n)
            ext = us[pl.ds(pl.multiple_of(r * rows_n, 8), rows_n + 8), :]
            u0, u1, u2 = ext[8:, :], pltpu.roll(ext, 1, 0)[8:, :], pltpu.roll(ext, 2, 0)[8:, :]
            cv = w[0:1, :] * u2 + w[1:2, :] * u1 + w[2:3, :] * u0
            dcb_ref[rows, :] = (dy_ref[rows, :].astype(F32) * cv).astype(BF16)
            nxt = ds[pl.ds(pl.multiple_of(r * rows_n, 8), rows_n + 8), :]
            e0 = nxt[:rows_n, :]
            e1 = pltpu.roll(nxt, rows_n + 7, 0)[:rows_n, :]
            e2 = pltpu.roll(nxt, rows_n + 6, 0)[:rows_n, :]
            du = w[2:3, :] * e0 + w[1:2, :] * e1 + w[0:1, :] * e2
            dcc_ref[rows, :] = (du * cx_ref[rows, :]).astype(BF16)
            dcx_ref[rows, :] = (du * cc_ref[rows, :]).astype(BF16)
            return (dw0 + jnp.sum(e0 * u2, axis=0, keepdims=True), dw1 + jnp.sum(e0 * u1, axis=0, keepdims=True),
                    dw2 + jnp.sum(e0 * u0, axis=0, keepdims=True))

        zero = jnp.zeros((1, LANES), F32)
        dw0, dw1, dw2 = lax.fori_loop(0, s // rows_n, out, (zero, zero, zero))
        dw_ref[...] = jnp.zeros_like(dw_ref)
        dw_ref[0:1, :] = dw0
        dw_ref[1:2, :] = dw1
        dw_ref[2:3, :] = dw2

    def body(cb_ref, cc_ref, cx_ref, w_ref, dy_ref, dp_in, dp_out, dw_ref, us, ds, outs, sems):
        del dp_in
        compute(cb_ref, cc_ref, cx_ref, w_ref, dy_ref, outs.at[0], outs.at[1], outs.at[2], dw_ref, us, ds)
        _store_segments(outs, dp_out, sems, [(3 + k) * nb + pl.program_id(0) for k in range(3)])

    def seg(k):
        return pl.BlockSpec((s, LANES), lambda b, k=k: (0, k * nb + b))

    return pl.pallas_call(
        body, name="conv_bwd", grid=(nb,),
        in_specs=[seg(3), seg(4), seg(5), pl.BlockSpec((3, LANES), lambda b: (0, b)),
                  pl.BlockSpec((s, LANES), lambda b: (0, b)), ANY_SPEC],
        out_specs=[ANY_SPEC, pl.BlockSpec((8, LANES), lambda b: (0, b))],
        out_shape=[jax.ShapeDtypeStruct(dp.shape, BF16), jax.ShapeDtypeStruct((8, d), F32)],
        input_output_aliases={5: 0},
        scratch_shapes=[pltpu.VMEM((s + 8, LANES), F32), pltpu.VMEM((s + 8, LANES), F32),
                        pltpu.VMEM((3, s, LANES), BF16), pltpu.SemaphoreType.DMA((3,))],
        compiler_params=_params("arbitrary"),
    )(p, p, p, conv_w, dyb, dp)


def _attn_bwd(p, qg2, kg2, dy, lt, dp, d):
    s = p.shape[0]
    n_pairs = d // LANES
    qsb = _tile(s, Q_SUPER_BWD)
    n_sub, n_sb, n_kb = qsb // Q_BLOCK, s // qsb, s // Q_BLOCK
    unroll = math.gcd(KEY_UNROLL, n_sub)
    chunk = _tile(s, 512)
    inv_sqrt = 1.0 / math.sqrt(HEAD_DIM)

    def compute(q_ref, k_ref, v_ref, qg_ref, kg_ref, dy_ref, lt_ref, dq_ref, dk_ref, dv_ref, dgain_ref,
                qs, k2, v2, dkt, dvt, qt, dyt, rem, gbef, dqa):
        low, causal, w4 = _attn_consts(True)

        def prep(r, _):
            rows = pl.ds(pl.multiple_of(r * chunk, chunk), chunk)
            qs[rows, :] = (_pair_norm(q_ref[rows, :], low)[0] * (qg_ref[...] * inv_sqrt)).astype(BF16)
            return 0

        lax.fori_loop(0, s // chunk, prep, 0)
        _fill_pair_blocks(k2, lambda rows: _pair_norm(k_ref[rows, :], low)[0] * kg_ref[...], low, n_kb)
        _fill_pair_blocks(v2, lambda rows: v_ref[rows, :], low, n_kb)

        def clear(b, _):
            dkt[b] = jnp.zeros((LANES, Q_BLOCK), F32)
            dvt[b] = jnp.zeros((LANES, Q_BLOCK), F32)
            return 0

        lax.fori_loop(0, n_kb, clear, 0)

        def step(sb, j, t0=0, diag_t=None):
            rows = pl.ds(pl.multiple_of(sb * qsb + t0 * Q_BLOCK, Q_BLOCK), (n_sub - t0) * Q_BLOCK)
            kj2, vj2 = k2[j], v2[j]
            z_both = _dot_nt(qs[rows, :], kj2)
            da_both = _dot_nt(dy_ref[rows, :], vj2)
            zls, cats = [], []
            for t in range(t0, n_sub):
                sub = slice((t - t0) * Q_BLOCK, (t - t0 + 1) * Q_BLOCK)
                for h in range(2):
                    z = z_both[sub, h * LANES:(h + 1) * LANES]
                    ln = _log_not(z)
                    if t == diag_t:
                        ln = jnp.where(causal, ln, 0.0)
                    zls.append(z + ln)
                    cats.append(_split_cat(ln))
            c2 = _dot(jnp.concatenate(cats, axis=0), w4)
            a_rows, gs, cats = [], [], []
            for t in range(t0, n_sub):
                sub = slice(t * Q_BLOCK, (t + 1) * Q_BLOCK)
                a_pair = []
                for h in range(2):
                    i = 2 * (t - t0) + h
                    tile = slice(i * Q_BLOCK, (i + 1) * Q_BLOCK)
                    left = rem[h, sub, :]
                    log_a = zls[i] + (left - c2[tile, :LANES])
                    if t == diag_t:
                        log_a = jnp.where(causal, log_a, -1e30)
                    a = jnp.exp(log_a)
                    rem[h, sub, :] = left - c2[tile, LANES:]
                    g = a * da_both[(t - t0) * Q_BLOCK:(t - t0 + 1) * Q_BLOCK, h * LANES:(h + 1) * LANES]
                    a_pair.append(a.astype(BF16))
                    gs.append(g)
                    cats.append(_split_cat(g))
                a_rows.append(jnp.concatenate(a_pair, axis=1))
            c2g = _dot(jnp.concatenate(cats, axis=0), w4)
            dz_rows = []
            for t in range(t0, n_sub):
                sub = slice(t * Q_BLOCK, (t + 1) * Q_BLOCK)
                dz_pair = []
                for h in range(2):
                    i = 2 * (t - t0) + h
                    tile = slice(i * Q_BLOCK, (i + 1) * Q_BLOCK)
                    before = gbef[h, sub, :]
                    dz = gs[i] - jnp.exp(zls[i]) * (before + c2g[tile, :LANES])
                    if t == diag_t:
                        dz = jnp.where(causal, dz, 0.0)
                    gbef[h, sub, :] = before + c2g[tile, LANES:]
                    dz_pair.append(dz.astype(BF16))
                dz_rows.append(jnp.concatenate(dz_pair, axis=1))
            a_both = jnp.concatenate(a_rows, axis=0)
            dz_both = jnp.concatenate(dz_rows, axis=0)
            used = slice(t0 * Q_BLOCK, qsb)
            dvt[j] += _dot(dyt[0, :, used], a_both[:, :LANES]) + _dot(dyt[1, :, used], a_both[:, LANES:])
            dkt[j] += _dot(qt[0, :, used], dz_both[:, :LANES]) + _dot(qt[1, :, used], dz_both[:, LANES:])
            dqa[used, :] += _dot(dz_both, kj2)

        def super_block(sb, dqg):
            rows_sb = pl.ds(pl.multiple_of(sb * qsb, qsb), qsb)
            total = lt_ref[rows_sb, :]
            other = pltpu.roll(total, HEAD_DIM, 1)
            rem[0] = jnp.where(low, total, other)
            rem[1] = jnp.where(low, other, total)
            gbef[...] = jnp.zeros_like(gbef)
            dqa[...] = jnp.zeros_like(dqa)
            qv = qs[rows_sb, :].astype(F32)
            dyv = dy_ref[rows_sb, :].astype(F32)
            qt[0] = jnp.where(low, qv, 0.0).T.astype(BF16)
            qt[1] = jnp.where(low, 0.0, qv).T.astype(BF16)
            dyt[0] = jnp.where(low, dyv, 0.0).T.astype(BF16)
            dyt[1] = jnp.where(low, 0.0, dyv).T.astype(BF16)

            def below(n, _):
                for u in range(unroll):
                    step(sb, unroll * n + u)
                return 0

            lax.fori_loop(0, sb * (n_sub // unroll), below, 0)
            for t in range(n_sub):
                step(sb, sb * n_sub + t, t0=t, diag_t=t)
            qhat, r = _pair_norm(q_ref[rows_sb, :], low)
            dqn = dqa[...]
            dqhat = dqn * (qg_ref[...] * inv_sqrt)
            dq_ref[rows_sb, :] = (r * (dqhat - qhat * _pair_mean(dqhat * qhat, low))).astype(BF16)
            return dqg + jnp.sum(dqn * qhat, axis=0, keepdims=True) * inv_sqrt

        dqg = lax.fori_loop(0, n_sb, super_block, jnp.zeros((1, LANES), F32))

        def finish(b, dkg):
            rows = pl.ds(pl.multiple_of(b * Q_BLOCK, Q_BLOCK), Q_BLOCK)
            khat, rk = _pair_norm(k_ref[rows, :], low)
            dkn = dkt[b].T
            dkhat = dkn * kg_ref[...]
            dk_ref[rows, :] = (rk * (dkhat - khat * _pair_mean(dkhat * khat, low))).astype(BF16)
            dv_ref[rows, :] = dvt[b].T.astype(BF16)
            return dkg + jnp.sum(dkn * khat, axis=0, keepdims=True)

        dkg = lax.fori_loop(0, n_kb, finish, jnp.zeros((1, LANES), F32))
        dgain_ref[...] = jnp.zeros_like(dgain_ref)
        dgain_ref[0:1, :] = dqg
        dgain_ref[1:2, :] = dkg

    def body(q_ref, k_ref, v_ref, qg_ref, kg_ref, dy_ref, lt_ref, dp_in, dp_out, dgain_ref, outs, sems, *scratch):
        del dp_in
        compute(q_ref, k_ref, v_ref, qg_ref, kg_ref, dy_ref, lt_ref, outs.at[0], outs.at[1], outs.at[2], dgain_ref,
                *scratch)
        _store_segments(outs, dp_out, sems, [k * n_pairs + pl.program_id(0) for k in range(3)])

    def seg(k):
        return pl.BlockSpec((s, LANES), lambda h, k=k: (0, k * n_pairs + h))

    vec = pl.BlockSpec((1, LANES), lambda h: (0, 0))
    col = pl.BlockSpec((s, LANES), lambda h: (0, h))
    return pl.pallas_call(
        body, name="attn_bwd", grid=(n_pairs,),
        in_specs=[seg(0), seg(1), seg(2), vec, vec, col, col, ANY_SPEC],
        out_specs=[ANY_SPEC, pl.BlockSpec((None, 8, LANES), lambda h: (h, 0, 0))],
        out_shape=[jax.ShapeDtypeStruct(dp.shape, BF16), jax.ShapeDtypeStruct((n_pairs, 8, LANES), F32)],
        input_output_aliases={7: 0},
        scratch_shapes=[pltpu.VMEM((3, s, LANES), BF16), pltpu.SemaphoreType.DMA((3,)), pltpu.VMEM((s, LANES), BF16)]
        + [pltpu.VMEM((n_kb, 2 * Q_BLOCK, LANES), BF16)] * 2
        + [pltpu.VMEM((n_kb, LANES, Q_BLOCK), F32)] * 2
        + [pltpu.VMEM((2, LANES, qsb), BF16)] * 2
        + [pltpu.VMEM((2, qsb, LANES), F32)] * 2 + [pltpu.VMEM((qsb, LANES), F32)],
        compiler_params=_params("arbitrary"),
    )(p, p, p, qg2, kg2, dy, lt, dp)


def _mm_in_bwd(dp, w_g):
    s = dp.shape[0]
    d, n4 = w_g.shape[-2:]
    tm = _tile(s, 1024)

    def body(a_ref, w_ref, o_ref, acc):
        j = pl.program_id(1)

        @pl.when(j == 0)
        def _():
            acc[...] = jnp.zeros_like(acc)

        acc[...] += _dot_nt(a_ref[...], w_ref[...])

        @pl.when(j == N_CHIP - 1)
        def _():
            o_ref[...] = acc[...]

    return pl.pallas_call(
        body, name="mm_in_bwd", grid=(s // tm, N_CHIP),
        in_specs=[pl.BlockSpec((tm, n4), lambda i, j: (i, j)),
                  pl.BlockSpec((None, d, n4), lambda i, j: (j, 0, 0))],
        out_specs=pl.BlockSpec((tm, d), lambda i, j: (i, 0)), out_shape=jax.ShapeDtypeStruct((s, d), F32),
        scratch_shapes=[pltpu.VMEM((tm, d), F32)], compiler_params=_params("parallel", "arbitrary"),
    )(dp, w_g)


def _sum_adam(parts, w, m, v, name):
    n_l, r, c = w.shape
    tr = next((t for t in (256, 176, 128, 64, 32, 16) if r % t == 0 and t * c <= 256 * 1024), r)
    n_blk = r // tr

    def body(*refs):
        p_refs = refs[:n_l]
        w_ref, m_ref, v_ref, g_ref, dl_ref, nm_ref, nv_ref = refs[n_l:]
        for l in range(n_l):
            @pl.when(pl.program_id(0) == l)
            def _(p_ref=p_refs[l]):
                g = p_ref[0].astype(F32)
                for dev in range(1, N_DEV):
                    g = g + p_ref[dev].astype(F32)
                g_ref[...] = g
                delta, nm, nv = _adamw(w_ref[...], g, m_ref[...], v_ref[...])
                dl_ref[...] = delta
                nm_ref[...] = nm
                nv_ref[...] = nv

    def part_spec(l):
        return pl.BlockSpec((N_DEV, tr, c), lambda ll, i, l=l: (0, jnp.where(ll == l, i, jnp.where(ll < l, 0, n_blk - 1)), 0))

    wsp = pl.BlockSpec((None, tr, c), lambda l, i: (l, i, 0))
    shp = jax.ShapeDtypeStruct(w.shape, F32)
    return pl.pallas_call(
        body, name=name, grid=(n_l, n_blk),
        in_specs=[part_spec(l) for l in range(n_l)] + [wsp, wsp, wsp],
        out_specs=[wsp] * 4, out_shape=[shp] * 4, compiler_params=_params("arbitrary", "arbitrary"),
    )(*parts, w, m, v)


def _small_adam(parts, w, m, v):
    def body(p_ref, w_ref, m_ref, v_ref, g_ref, dl_ref, nm_ref, nv_ref):
        g = p_ref[0]
        for dev in range(1, N_DEV):
            g = g + p_ref[dev]
        g_ref[...] = g
        delta, nm, nv = _adamw(w_ref[...], g, m_ref[...], v_ref[...])
        dl_ref[...] = delta
        nm_ref[...] = nm
        nv_ref[...] = nv

    shp = jax.ShapeDtypeStruct(w.shape, F32)
    return pl.pallas_call(body, name="small_adam", in_specs=[VMEM_SPEC] * 4, out_specs=[VMEM_SPEC] * 4,
                          out_shape=[shp] * 4,
                          compiler_params=pltpu.CompilerParams(vmem_limit_bytes=VMEM_LIMIT_BYTES))(parts, w, m, v)


def _pack(vecs, mult=8 * LANES):
    flat = jnp.concatenate([a.reshape(-1).astype(F32) for a in vecs])
    pad = (-flat.shape[0]) % mult
    if pad:
        flat = jnp.concatenate([flat, jnp.zeros((pad,), F32)])
    return flat.reshape(8, -1)


def _unpack(flat, shapes):
    flat = flat.reshape(-1)
    out, off = [], 0
    for shp in shapes:
        n = math.prod(shp)
        out.append(flat[off:off + n].reshape(shp))
        off += n
    return out


BIG = ("win", "wa", "wb", "wo", "wg", "wu", "wd")
GRAD_GROUPS = (("wd", "wg", "wu"), ("wo", "wa", "wb"), ("win",))


def _local_step(x, target, mods, ln1_g, ln2_g, qg, kg, conv_w, weights, send_grads):
    s, d = x.shape
    n_l = mods.shape[0]
    saved = []
    h_in = x
    for l in range(n_l):
        sh1, sc1, g1, sh2, sc2, g2 = [mods[l, k * d:(k + 1) * d].reshape(1, d) for k in range(6)]
        qg2, kg2 = jnp.tile(qg[l:l + 1], (1, 2)), jnp.tile(kg[l:l + 1], (1, 2))
        h1 = _lnmod(h_in, ln1_g[l:l + 1], sc1, sh1)
        (win,), tie = weights(l, ("win",), h1)
        p = _mm_in(h1, win)
        ya, lt = _attn_fwd(p, qg2 + tie, kg2, d)
        yb = _conv_fwd(p, conv_w[l], d)
        (wa, wb, wo, wg, wu, wd), tie = weights(l, ("wa", "wb", "wo", "wg", "wu", "wd"), ya)
        wa, wb, wo = wa.reshape(d, d), wb.reshape(d, d), wo.reshape(d, d)
        merged, pa, pb = _branch(ya, yb, p, wa, wb, d)
        x1, mo = _out_proj(merged, wo, h_in, g1 + tie)
        h2 = _lnmod(x1, ln2_g[l:l + 1], sc2, sh2)
        gate, up, act = _ffn_up(h2, wg, wu)
        x2, f = _ffn_down(act, wd, x1, g2)
        saved.append(dict(x0=h_in, h1=h1, p=p, ya=ya, lt=lt, yb=yb, merged=merged, pa=pa, pb=pb, x1=x1, mo=mo,
                          h2=h2, gate=gate, up=up, act=act, f=f, win=win, wa=wa, wb=wb, wo=wo, wg=wg, wu=wu, wd=wd,
                          mod=(sh1, sc1, g1, sh2, sc2, g2), qg2=qg2, kg2=kg2))
        h_in = x2

    dx, loss_tile = _loss_head(h_in, target)

    small = [None] * n_l
    for l in reversed(range(n_l)):
        sv = saved[l]
        sh1, sc1, g1, sh2, sc2, g2 = sv["mod"]
        f4, n4 = sv["wg"].shape[-1], sv["win"].shape[-1]
        hsp = lambda tk: ((tk, d), lambda j, k: (k, 0))
        fsp = lambda tk: ((None, tk, f4), lambda j, k: (j, k, 0))
        dgate, dup, df, dg2 = _ffn_bwd1(dx, sv["f"], g2, sv["wd"], sv["gate"], sv["up"])
        g_wd = _mm_tn(sv["act"], df, fsp, hsp, (f4, d), "grad_wd")
        g_wg = _mm_tn(dgate, sv["h2"], fsp, hsp, (f4, d), "grad_wg")
        g_wu = _mm_tn(dup, sv["h2"], fsp, hsp, (f4, d), "grad_wu")
        tie = send_grads(l, dict(wd=g_wd, wg=g_wg, wu=g_wu))
        dh2 = _ffn_bwd2(dgate, dup, sv["wg"], sv["wu"])
        dx1, sums2 = _lnmod_bwd(sv["x1"], ln2_g[l:l + 1], sc2 + tie, dh2, dx)
        dmo, da, db, dya, dyb, dp, dg1 = _out_bwd(dx1, sv["mo"], g1, sv["wo"], sv["pa"], sv["pb"], sv["p"],
                                                        sv["wa"], sv["wb"], d)
        g_wo = _mm_tn_square(sv["merged"], dmo, "grad_wo")
        g_wa = _mm_tn_square(sv["ya"], da, "grad_wa")
        g_wb = _mm_tn_square(sv["yb"], db, "grad_wb")
        tie = send_grads(l, dict(wo=g_wo, wa=g_wa, wb=g_wb))
        dp, dconv = _conv_bwd(sv["p"], conv_w[l] + tie, dyb, dp, d)
        dp, dgain = _attn_bwd(sv["p"], sv["qg2"], sv["kg2"], dya, sv["lt"], dp, d)
        g_win = _mm_tn(sv["h1"], dp, hsp, lambda tk: ((tk, n4), lambda j, k: (k, j)), (d, n4), "grad_win")
        tie = send_grads(l, dict(win=g_win))
        dh1 = _mm_in_bwd(dp, sv["win"])
        dx, sums1 = _lnmod_bwd(sv["x0"], ln1_g[l:l + 1], sc1 + tie, dh1, dx1)
        dgain = jnp.sum(dgain[:, 0:2, :], axis=0)
        dgain = dgain[:, :HEAD_DIM] + dgain[:, HEAD_DIM:]
        dmod = jnp.concatenate([sums1[0], sums1[1], dg1[0], sums2[0], sums2[1], dg2[0]])
        small[l] = dict(dmod=dmod, ln1=sums1[2], ln2=sums2[2], qg=dgain[0], kg=dgain[1], conv=dconv[0:3])
    return loss_tile, dx, small


def kernel(x, c, ada_w, ada_b, ln1_g, w_in, q_norm_g, k_norm_g, conv_w, w_branch_a, w_branch_b, w_out, ln2_g, w_ffn_gate, w_ffn_up, w_ffn_down, loss_target, m_ada_w, m_ada_b, m_ln1_g, m_w_in, m_q_norm_g, m_k_norm_g, m_conv_w, m_w_branch_a, m_w_branch_b, m_w_out, m_ln2_g, m_w_ffn_gate, m_w_ffn_up, m_w_ffn_down, v_ada_w, v_ada_b, v_ln1_g, v_w_in, v_q_norm_g, v_k_norm_g, v_conv_w, v_w_branch_a, v_w_branch_b, v_w_out, v_ln2_g, v_w_ffn_gate, v_w_ffn_up, v_w_ffn_down):
    n_l, d, a4 = ada_w.shape
    cw4 = conv_w.shape[-1]
    ix, iy, ic = lax.axis_index("x"), lax.axis_index("y"), lax.axis_index("c")
    chip = 2 * ix + iy
    me = 2 * chip + ic

    big_w = dict(win=w_in, wa=w_branch_a, wb=w_branch_b, wo=w_out, wg=w_ffn_gate, wu=w_ffn_up, wd=w_ffn_down)
    big_m = dict(win=m_w_in, wa=m_w_branch_a, wb=m_w_branch_b, wo=m_w_out, wg=m_w_ffn_gate, wu=m_w_ffn_up,
                 wd=m_w_ffn_down)
    big_v = dict(win=v_w_in, wa=v_w_branch_a, wb=v_w_branch_b, wo=v_w_out, wg=v_w_ffn_gate, wu=v_w_ffn_up,
                 wd=v_w_ffn_down)

    def adam_view(a, k):
        return jnp.swapaxes(a, 1, 2) if k in ("wg", "wu") else a

    got = _gather8(_pack([c, conv_w])).reshape(N_DEV, -1)

    weight_groups = [(l, names) for l in range(n_l) for names in (("win",), ("wa", "wb", "wo", "wg", "wu", "wd"))]
    group_srcs = [[big_w[k][l].astype(BF16) for k in names] for l, names in weight_groups]
    started_w = {}

    def start_weights(gi):
        l, names = weight_groups[gi]
        copies = _weight_half_copies if gi == 0 else _weight_copies
        st = _split_start("weights_start_%d" % gi, copies, group_srcs[gi],
                          [(N_CHIP,) + sh.shape for sh in group_srcs[gi]], 3)
        for k in names:
            started_w[(l, k)] = [gi, names, st, None, copies]
        return st[4]

    got, group_srcs[0] = lax.optimization_barrier((got, group_srcs[0]))
    tie = start_weights(0)[0, 0]
    c_all = got[:, :d]
    conv_all = got[:, d:d + n_l * 3 * cw4].reshape(N_CHIP, 2, n_l, 3, cw4)[:, 0]
    conv_full = jnp.transpose(conv_all, (1, 2, 0, 3)).reshape(n_l, 3, N_CHIP * cw4)
    b_cols = lax.dynamic_slice_in_dim(ada_b, chip * a4, a4, axis=1).reshape(n_l, 1, a4)
    b_cols, group_srcs[1:] = lax.optimization_barrier((b_cols + tie, group_srcs[1:]))
    mod_cols = _ada_mod(c_all, ada_w, b_cols)
    mod_all = _gather8(_pack([mod_cols])).reshape(N_DEV, -1)[:, :n_l * N_DEV * a4]
    mod_all = mod_all.reshape(N_CHIP, 2, n_l, N_DEV, a4)[:, 0]
    mods = lax.dynamic_index_in_dim(mod_all, me, axis=2, keepdims=False)
    mods = jnp.transpose(mods, (1, 0, 2)).reshape(n_l, N_CHIP * a4)

    def weights(l, names, after):
        entry, tie = started_w[(l, names[0])], jnp.zeros((), F32)
        if entry[3] is None:
            lands = _split_wait("weights_wait_%d" % entry[0], entry[4], entry[2], after)
            if entry[4] is _weight_half_copies:
                passed = _split_start_in_place("weights_pass_start_%d" % entry[0], _weight_half_pass, lands, 3)
                lands = _split_wait_in_place("weights_pass_wait_%d" % entry[0], _weight_half_pass, passed, passed[3])
            nxt = entry[0] + 1
            if nxt < len(weight_groups):
                lands, group_srcs[nxt] = lax.optimization_barrier((lands, group_srcs[nxt]))
                tie = start_weights(nxt)[0, 0]
            lands = [lax.dynamic_update_index_in_dim(land, own, chip, 0) for land, own in zip(lands, entry[2][2])]
            for k in entry[1]:
                started_w[(l, k)][3] = dict(zip(entry[1], lands))
        return [started_w[(l, k)][3][k] for k in names], tie

    started_g, held_back = [], []

    def start_grads(l, grads, copies=_grad_copies, sems_per=7):
        names = tuple(grads)
        st = _split_start("grads_start_%d" % len(started_g), copies, [grads[k] for k in names],
                          [(N_DEV,) + grads[k].shape[1:] for k in names], sems_per)
        started_g.append((l, names, st, copies))
        return st[4][0, 0]

    def send_grads(l, grads):
        if l == 0 and tuple(grads) == GRAD_GROUPS[-1]:
            held_back.append(grads)
            return jnp.zeros((), F32)
        return start_grads(l, grads)

    loss_tile, grad_x, small = _local_step(
        x[0], loss_target[0], mods, ln1_g, ln2_g, q_norm_g, k_norm_g, conv_full, weights, send_grads)

    sm_shapes = [(n_l, 6 * d), (n_l, d), (n_l, d), (n_l, HEAD_DIM), (n_l, HEAD_DIM), (n_l, 3, d), (1,)]
    vec = _pack([jnp.stack([small[l][k] for l in range(n_l)]) for k in ("dmod", "ln1", "ln2", "qg", "kg", "conv")]
                + [loss_tile[0, 0:1]])
    n_vec = vec.shape[1] * 8
    all_vec = _gather8(vec).reshape(N_DEV, n_vec)
    all_vec, held_back = lax.optimization_barrier((all_vec, held_back))
    tie = sum([start_grads(0, grads, _grad_copies_same_core, 4) for grads in held_back], jnp.zeros((), F32))
    per_dev = [_unpack(all_vec[dev], sm_shapes) for dev in range(N_DEV)]
    dmod_all = jnp.stack([pd[0] for pd in per_dev])
    dmod_cols = jnp.transpose(lax.dynamic_slice_in_dim(dmod_all, chip * a4, a4, axis=2), (1, 0, 2))
    ada_out = _ada_grad_adam(jnp.transpose(c_all) + tie, dmod_cols, ada_w, m_ada_w, v_ada_w)

    dev_parts = jnp.stack([
        _pack([pd[0], pd[1], pd[2], pd[3], pd[4], lax.dynamic_slice_in_dim(pd[5], chip * cw4, cw4, axis=2), pd[6]])
        for pd in per_dev])
    zero1 = jnp.zeros((1,), F32)
    sw = _pack([ada_b, ln1_g, ln2_g, q_norm_g, k_norm_g, conv_w, zero1])
    sm = _pack([m_ada_b, m_ln1_g, m_ln2_g, m_q_norm_g, m_k_norm_g, m_conv_w, zero1])
    sv = _pack([v_ada_b, v_ln1_g, v_ln2_g, v_q_norm_g, v_k_norm_g, v_conv_w, zero1 + 1.0])
    out_shapes = [(n_l, 6 * d), (n_l, d), (n_l, d), (n_l, HEAD_DIM), (n_l, HEAD_DIM), (n_l, 3, cw4), (1,)]
    sm_out = [_unpack(o, out_shapes) for o in _small_adam(dev_parts, sw, sm, sv)]
    loss = 0.5 * sm_out[0][6][0] / d

    after = jnp.full((8, LANES), tie + sm_out[0][0][0, 0] + ada_out[0][0, 0, 0])
    big_out = {}
    for names in GRAD_GROUPS:
        got_parts = {}
        for gi, (l, sent, st, copies) in enumerate(started_g):
            if sent == names:
                parts = _split_wait("grads_wait_%d" % gi, copies, st, after)
                if copies is _grad_copies_same_core:
                    passed = _split_start_in_place("grads_pass_start_%d" % gi, _grad_pass_copies, parts, 3)
                    parts = _split_wait_in_place("grads_pass_wait_%d" % gi, _grad_pass_copies, passed, passed[3])
                for k, part, grad in zip(sent, parts, st[2]):
                    own = lax.dynamic_index_in_dim(grad, chip, 0, keepdims=False)
                    got_parts[(l, k)] = lax.dynamic_update_index_in_dim(part, own, me, 0)
        for k in names:
            res = _sum_adam([got_parts[(l, k)] for l in range(n_l)], adam_view(big_w[k], k), adam_view(big_m[k], k),
                            adam_view(big_v[k], k), "sum_adam_" + k)
            after = res[0]
            big_out[k] = [adam_view(r, k) for r in res]

    outs = [loss, grad_x[None]]
    for kind in range(4):
        sm_k = sm_out[kind]
        outs += [ada_out[kind], sm_k[0], sm_k[1], big_out["win"][kind], sm_k[3], sm_k[4], sm_k[5],
                 big_out["wa"][kind], big_out["wb"][kind], big_out["wo"][kind], sm_k[2],
                 big_out["wg"][kind], big_out["wu"][kind], big_out["wd"][kind]]
    return tuple(outs)
```

```python
import math

import jax
import jax.numpy as jnp
from jax import lax
from jax.experimental import pallas as pl
from jax.experimental.pallas import tpu as pltpu

F32 = jnp.float32
BF16 = jnp.bfloat16
MESH_ID = pl.DeviceIdType.MESH

EPS = 1e-6
HEAD_DIM = 64
Q_BLOCK = 128
Q_SUPER = 1024
Q_SUPER_BWD = 1024
KEY_UNROLL = 4
LANES = 128
N_DEV = 8
N_CHIP = 4
VMEM_LIMIT_BYTES = 56 * 1024 * 1024

ADAM_LR = 0.001
ADAM_B1 = 0.9
ADAM_B2 = 0.999
ADAM_EPS = 1e-08
ADAM_WD = 0.01
ADAM_STEP = 10

HBM_SPEC = pl.BlockSpec(memory_space=pltpu.HBM)
ANY_SPEC = pl.BlockSpec(memory_space=pl.ANY)
SEM_SPEC = pl.BlockSpec(memory_space=pltpu.SEMAPHORE)
VMEM_SPEC = pl.BlockSpec(memory_space=pltpu.VMEM)
SIDE_EFFECT = pltpu.SideEffectType.DATAFLOW_SIDE_EFFECTING


def _params(*sem):
    return pltpu.CompilerParams(dimension_semantics=tuple(sem), vmem_limit_bytes=VMEM_LIMIT_BYTES)


def _tile(n, pref):
    return pref if n % pref == 0 else n


def _dot(a, b):
    return jnp.dot(a, b, preferred_element_type=F32)


def _dot_nt(a, b):
    return lax.dot_general(a, b, (((1,), (1,)), ((), ())), preferred_element_type=F32)


def _dot_tn(a, b):
    return lax.dot_general(a, b, (((0,), (0,)), ((), ())), preferred_element_type=F32)


def _adamw(w, g, m, v):
    m = ADAM_B1 * m + (1.0 - ADAM_B1) * g
    v = ADAM_B2 * v + (1.0 - ADAM_B2) * (g * g)
    m_hat = m / (1.0 - ADAM_B1 ** ADAM_STEP)
    v_hat = v / (1.0 - ADAM_B2 ** ADAM_STEP)
    delta = -ADAM_LR * (m_hat / (jnp.sqrt(v_hat) + ADAM_EPS) + ADAM_WD * w)
    return delta, m, v


def _hbm(a):
    return pltpu.with_memory_space_constraint(a, pltpu.HBM)


def _peer(x, y, c, k):
    return (1 - x if k & 4 else x, 1 - y if k & 2 else y, 1 - c if k & 1 else c)


def _gather8(v):
    rows_per, m = v.shape

    def body(v_ref, out_ref, send_sems, recv_sems, local_sem):
        x, y, c = lax.axis_index("x"), lax.axis_index("y"), lax.axis_index("c")

        def rows(p):
            return out_ref.at[pl.ds((4 * p[0] + 2 * p[1] + p[2]) * rows_per, rows_per), :]

        me = (x, y, c)
        mine = pltpu.make_async_copy(v_ref, rows(me), local_sem)
        mine.start()
        sends = []
        for k in range(1, N_DEV):
            cp = pltpu.make_async_remote_copy(
                src_ref=v_ref, dst_ref=rows(me), send_sem=send_sems.at[k - 1], recv_sem=recv_sems.at[k - 1],
                device_id=_peer(x, y, c, k), device_id_type=MESH_ID)
            cp.start()
            sends.append(cp)
        for k in range(1, N_DEV):
            pltpu.make_async_remote_copy(
                src_ref=v_ref, dst_ref=rows(_peer(x, y, c, k)), send_sem=send_sems.at[k - 1],
                recv_sem=recv_sems.at[k - 1], device_id=_peer(x, y, c, k), device_id_type=MESH_ID).wait_recv()
        for cp in sends:
            cp.wait_send()
        mine.wait()

    return pl.pallas_call(
        body, name="gather8",
        out_shape=jax.ShapeDtypeStruct((N_DEV * rows_per, m), v.dtype),
        in_specs=[VMEM_SPEC], out_specs=VMEM_SPEC,
        scratch_shapes=[pltpu.SemaphoreType.DMA((N_DEV - 1,)), pltpu.SemaphoreType.DMA((N_DEV - 1,)),
                        pltpu.SemaphoreType.DMA],
    )(v)


def _weight_copies(srcs, lands, send_sems, recv_sems):
    x, y, c = lax.axis_index("x"), lax.axis_index("y"), lax.axis_index("c")
    chips = [(1 - x, y), (x, 1 - y), (1 - x, 1 - y)]
    sends, recvs = [], []
    for a, (src, land) in enumerate(zip(srcs, lands)):
        for j, (px, py) in enumerate(chips):
            def copy(dst_block, a=a, j=j, px=px, py=py, src=src, land=land):
                return pltpu.make_async_remote_copy(
                    src_ref=src, dst_ref=land.at[dst_block], send_sem=send_sems.at[3 * a + j],
                    recv_sem=recv_sems.at[3 * a + j], device_id=(px, py, c), device_id_type=MESH_ID)
            sends.append(copy(2 * x + y))
            recvs.append(copy(2 * px + py))
    return sends, recvs


def _weight_half_copies(srcs, lands, send_sems, recv_sems):
    x, y, c = lax.axis_index("x"), lax.axis_index("y"), lax.axis_index("c")
    chips = [(1 - x, y), (x, 1 - y), (1 - x, 1 - y)]
    sends, recvs = [], []
    for a, (src, land) in enumerate(zip(srcs, lands)):
        half = src.shape[0] // 2
        rows = pl.ds(c * half, half)
        for j, (px, py) in enumerate(chips):
            def copy(dst_block, a=a, j=j, px=px, py=py, src=src, land=land, rows=rows):
                return pltpu.make_async_remote_copy(
                    src_ref=src.at[rows], dst_ref=land.at[dst_block, rows], send_sem=send_sems.at[3 * a + j],
                    recv_sem=recv_sems.at[3 * a + j], device_id=(px, py, c), device_id_type=MESH_ID)
            sends.append(copy(2 * x + y))
            recvs.append(copy(2 * px + py))
    return sends, recvs


def _weight_half_pass(lands, same_lands, send_sems, recv_sems):
    del same_lands
    x, y, c = lax.axis_index("x"), lax.axis_index("y"), lax.axis_index("c")
    chips = [(1 - x, y), (x, 1 - y), (1 - x, 1 - y)]
    sends, recvs = [], []
    for a, land in enumerate(lands):
        half = land.shape[1] // 2
        for j, (px, py) in enumerate(chips):
            def copy(pc, a=a, j=j, px=px, py=py, land=land, half=half):
                part = land.at[2 * px + py, pl.ds(pc * half, half)]
                return pltpu.make_async_remote_copy(
                    src_ref=part, dst_ref=part, send_sem=send_sems.at[3 * a + j], recv_sem=recv_sems.at[3 * a + j],
                    device_id=(x, y, 1 - c), device_id_type=MESH_ID)
            sends.append(copy(c))
            recvs.append(copy(1 - c))
    return sends, recvs


def _split_start(name, copies, srcs, land_shapes, sems_per_src):
    n = len(srcs)

    def body(*refs):
        sends, _ = copies(refs[:n], refs[n + 2:2 * n + 2], refs[n], refs[n + 1])
        for cp in sends:
            cp.start()
        token = refs[-1]
        token[...] = jnp.zeros_like(token)

    n_sems = sems_per_src * n
    outs = pl.pallas_call(
        body, name=name,
        out_shape=(pltpu.SemaphoreType.DMA((n_sems,)), pltpu.SemaphoreType.DMA((n_sems,)),
                   *[pltpu.HBM(shape, a.dtype) for a, shape in zip(srcs, land_shapes)],
                   jax.ShapeDtypeStruct((8, LANES), F32)),
        in_specs=[HBM_SPEC] * n, out_specs=(SEM_SPEC, SEM_SPEC, *[HBM_SPEC] * n, VMEM_SPEC),
        compiler_params=pltpu.CompilerParams(has_side_effects=SIDE_EFFECT),
    )(*[_hbm(a) for a in srcs])
    return outs[0], outs[1], list(srcs), list(outs[2:2 + n]), outs[-1]


def _split_wait(name, copies, started, after):
    send_sems, recv_sems, srcs, lands, _ = started
    n = len(srcs)

    def body(*refs):
        sends, recvs = copies(refs[:n], refs[n:2 * n], refs[2 * n], refs[2 * n + 1])
        for cp in sends:
            cp.wait_send()
        for cp in recvs:
            cp.wait_recv()

    return pl.pallas_call(
        body, name=name,
        out_shape=tuple(pltpu.HBM(a.shape, a.dtype) for a in lands),
        in_specs=[HBM_SPEC] * (2 * n) + [SEM_SPEC, SEM_SPEC, ANY_SPEC], out_specs=tuple([HBM_SPEC] * n),
        input_output_aliases={n + i: i for i in range(n)},
        compiler_params=pltpu.CompilerParams(has_side_effects=SIDE_EFFECT),
    )(*srcs, *lands, send_sems, recv_sems, after)


def _split_start_in_place(name, copies, bufs, sems_per_buf):
    n = len(bufs)

    def body(*refs):
        sends, _ = copies(refs[:n], refs[:n], refs[n], refs[n + 1])
        for cp in sends:
            cp.start()
        token = refs[-1]
        token[...] = jnp.zeros_like(token)

    n_sems = sems_per_buf * n
    outs = pl.pallas_call(
        body, name=name,
        out_shape=(pltpu.SemaphoreType.DMA((n_sems,)), pltpu.SemaphoreType.DMA((n_sems,)),
                   *[pltpu.HBM(a.shape, a.dtype) for a in bufs], jax.ShapeDtypeStruct((8, LANES), F32)),
        in_specs=[HBM_SPEC] * n, out_specs=(SEM_SPEC, SEM_SPEC, *[HBM_SPEC] * n, VMEM_SPEC),
        input_output_aliases={i: 2 + i for i in range(n)},
        compiler_params=pltpu.CompilerParams(has_side_effects=SIDE_EFFECT),
    )(*[_hbm(a) for a in bufs])
    return outs[0], outs[1], list(outs[2:2 + n]), outs[-1]


def _split_wait_in_place(name, copies, started, after):
    send_sems, recv_sems, bufs, _ = started
    n = len(bufs)

    def body(*refs):
        sends, recvs = copies(refs[:n], refs[:n], refs[n], refs[n + 1])
        for cp in sends:
            cp.wait_send()
        for cp in recvs:
            cp.wait_recv()

    return pl.pallas_call(
        body, name=name,
        out_shape=tuple(pltpu.HBM(a.shape, a.dtype) for a in bufs),
        in_specs=[HBM_SPEC] * n + [SEM_SPEC, SEM_SPEC, ANY_SPEC], out_specs=tuple([HBM_SPEC] * n),
        input_output_aliases={i: i for i in range(n)},
        compiler_params=pltpu.CompilerParams(has_side_effects=SIDE_EFFECT),
    )(*bufs, send_sems, recv_sems, after)


def _grad_copies(grads, parts, send_sems, recv_sems):
    x, y, c = lax.axis_index("x"), lax.axis_index("y"), lax.axis_index("c")
    chips = [(1 - x, y), (x, 1 - y), (1 - x, 1 - y)]
    my_slot = 4 * x + 2 * y + c
    sends, recvs = [], []
    for a, (grad, part) in enumerate(zip(grads, parts)):
        def copy(k, block, slot, to, a=a, grad=grad, part=part):
            return pltpu.make_async_remote_copy(
                src_ref=grad.at[block], dst_ref=part.at[slot], send_sem=send_sems.at[7 * a + k],
                recv_sem=recv_sems.at[7 * a + k], device_id=to, device_id_type=MESH_ID)
        sends.append(copy(0, 2 * x + y, my_slot, (x, y, 1 - c)))
        recvs.append(copy(0, 2 * x + y, 4 * x + 2 * y + (1 - c), (x, y, 1 - c)))
        for j, (px, py) in enumerate(chips):
            for other, pc in enumerate((c, 1 - c)):
                sends.append(copy(1 + 2 * j + other, 2 * px + py, my_slot, (px, py, pc)))
                recvs.append(copy(1 + 2 * j + other, 2 * x + y, 4 * px + 2 * py + pc, (px, py, pc)))
    return sends, recvs


def _grad_copies_same_core(grads, parts, send_sems, recv_sems):
    x, y, c = lax.axis_index("x"), lax.axis_index("y"), lax.axis_index("c")
    chips = [(1 - x, y), (x, 1 - y), (1 - x, 1 - y)]
    my_slot = 4 * x + 2 * y + c
    sends, recvs = [], []
    for a, (grad, part) in enumerate(zip(grads, parts)):
        def copy(k, block, slot, to, a=a, grad=grad, part=part):
            return pltpu.make_async_remote_copy(
                src_ref=grad.at[block], dst_ref=part.at[slot], send_sem=send_sems.at[4 * a + k],
                recv_sem=recv_sems.at[4 * a + k], device_id=to, device_id_type=MESH_ID)
        sends.append(copy(0, 2 * x + y, my_slot, (x, y, 1 - c)))
        recvs.append(copy(0, 2 * x + y, 4 * x + 2 * y + (1 - c), (x, y, 1 - c)))
        for j, (px, py) in enumerate(chips):
            sends.append(copy(1 + j, 2 * px + py, my_slot, (px, py, c)))
            recvs.append(copy(1 + j, 2 * x + y, 4 * px + 2 * py + c, (px, py, c)))
    return sends, recvs


def _grad_pass_copies(parts, same_parts, send_sems, recv_sems):
    del same_parts
    x, y, c = lax.axis_index("x"), lax.axis_index("y"), lax.axis_index("c")
    chips = [(1 - x, y), (x, 1 - y), (1 - x, 1 - y)]
    sends, recvs = [], []
    for a, part in enumerate(parts):
        for j, (px, py) in enumerate(chips):
            def copy(pc, a=a, j=j, px=px, py=py, part=part):
                slot = part.at[4 * px + 2 * py + pc]
                return pltpu.make_async_remote_copy(
                    src_ref=slot, dst_ref=slot, send_sem=send_sems.at[3 * a + j], recv_sem=recv_sems.at[3 * a + j],
                    device_id=(x, y, 1 - c), device_id_type=MESH_ID)
            sends.append(copy(c))
            recvs.append(copy(1 - c))
    return sends, recvs


def _ada_mod(c_all, ada_w, ada_b_cols):
    n_l, d, a4 = ada_w.shape
    tn = _tile(a4, 512)

    def body(c_ref, w_ref, b_ref, o_ref):
        cv = c_ref[...]
        ca = (cv * jax.nn.sigmoid(cv)).astype(BF16)
        o_ref[...] = _dot(ca, w_ref[...].astype(BF16)) + b_ref[...]

    return pl.pallas_call(
        body, name="ada_mod", grid=(n_l, a4 // tn),
        in_specs=[pl.BlockSpec((N_DEV, d), lambda l, j: (0, 0)),
                  pl.BlockSpec((None, d, tn), lambda l, j: (l, 0, j)),
                  pl.BlockSpec((None, 1, tn), lambda l, j: (l, 0, j))],
        out_specs=pl.BlockSpec((None, N_DEV, tn), lambda l, j: (l, 0, j)),
        out_shape=jax.ShapeDtypeStruct((n_l, N_DEV, a4), F32),
        compiler_params=_params("parallel", "parallel"),
    )(c_all, ada_w, ada_b_cols)


def _ada_grad_adam(c_all_t, dmod_cols, w, m, v):
    n_l, d, a4 = w.shape
    tn = _tile(a4, 512)

    def body(ct_ref, dm_ref, w_ref, m_ref, v_ref, g_ref, dl_ref, nm_ref, nv_ref):
        ct = ct_ref[...]
        ca = ct * jax.nn.sigmoid(ct)
        dm = dm_ref[...]
        g = ca[:, 0:1] * dm[0:1, :]
        for dev in range(1, N_DEV):
            g = g + ca[:, dev:dev + 1] * dm[dev:dev + 1, :]
        g_ref[...] = g
        delta, nm, nv = _adamw(w_ref[...], g, m_ref[...], v_ref[...])
        dl_ref[...] = delta
        nm_ref[...] = nm
        nv_ref[...] = nv

    wspec = pl.BlockSpec((None, d, tn), lambda l, j: (l, 0, j))
    shp = jax.ShapeDtypeStruct(w.shape, F32)
    return pl.pallas_call(
        body, name="ada_grad_adam", grid=(n_l, a4 // tn),
        in_specs=[pl.BlockSpec((d, N_DEV), lambda l, j: (0, 0)),
                  pl.BlockSpec((None, N_DEV, tn), lambda l, j: (l, 0, j)), wspec, wspec, wspec],
        out_specs=[wspec] * 4, out_shape=[shp] * 4,
        compiler_params=_params("parallel", "parallel"),
    )(c_all_t, dmod_cols, w, m, v)


def _lnmod(x, g, sc, sh):
    s, d = x.shape
    tm = _tile(s, 1024)

    def body(x_ref, g_ref, sc_ref, sh_ref, h_ref):
        xv = x_ref[...]
        r = lax.rsqrt(jnp.mean(xv * xv, axis=-1, keepdims=True) + EPS)
        h_ref[...] = ((xv * r * g_ref[...]) * (1.0 + sc_ref[...]) + sh_ref[...]).astype(BF16)

    vec = pl.BlockSpec((1, d), lambda i: (0, 0))
    row = pl.BlockSpec((tm, d), lambda i: (i, 0))
    return pl.pallas_call(
        body, name="lnmod", grid=(s // tm,), in_specs=[row, vec, vec, vec], out_specs=row,
        out_shape=jax.ShapeDtypeStruct((s, d), BF16), compiler_params=_params("parallel"),
    )(x, g, sc, sh)


def _mm_in(h, w_g):
    s, d = h.shape
    n4 = w_g.shape[-1]
    tm = _tile(s, 1024)

    def body(a_ref, b_ref, o_ref):
        o_ref[...] = _dot(a_ref[...], b_ref[...])

    return pl.pallas_call(
        body, name="mm_in", grid=(N_CHIP, s // tm),
        in_specs=[pl.BlockSpec((tm, d), lambda j, i: (i, 0)),
                  pl.BlockSpec((None, d, n4), lambda j, i: (j, 0, 0))],
        out_specs=pl.BlockSpec((tm, n4), lambda j, i: (i, j)),
        out_shape=jax.ShapeDtypeStruct((s, N_CHIP * n4), F32),
        compiler_params=_params("parallel", "parallel"),
    )(h, w_g)


def _pair_mean(x, low):
    lo = jnp.sum(jnp.where(low, x, 0.0), axis=-1, keepdims=True)
    hi = jnp.sum(jnp.where(low, 0.0, x), axis=-1, keepdims=True)
    return jnp.where(low, lo, hi) * (1.0 / HEAD_DIM)


def _pair_norm(x, low):
    r = lax.rsqrt(_pair_mean(x * x, low) + EPS)
    return x * r, r


def _log_not(z):
    nz = -z
    return jnp.minimum(nz, 0.0) - jnp.log(1.0 + jnp.exp(jnp.minimum(z, nz)))


def _attn_consts(inclusive):
    low = lax.broadcasted_iota(jnp.int32, (1, LANES), 1) < HEAD_DIM
    row = lax.broadcasted_iota(jnp.int32, (Q_BLOCK, Q_BLOCK), 0)
    col = lax.broadcasted_iota(jnp.int32, (Q_BLOCK, Q_BLOCK), 1)
    tri = (row <= col) if inclusive else (row > col)
    w2 = jnp.concatenate([tri.astype(BF16), jnp.ones((Q_BLOCK, Q_BLOCK), BF16)], axis=1)
    return low, col < row, jnp.concatenate([w2, w2], axis=0)


def _split_cat(v):
    hi = v.astype(BF16)
    return jnp.concatenate([hi, (v - hi.astype(F32)).astype(BF16)], axis=1)


def _fill_pair_blocks(dst, src_fn, low, n_kb):
    def fill(b, _):
        v = src_fn(pl.ds(pl.multiple_of(b * Q_BLOCK, Q_BLOCK), Q_BLOCK))
        dst[b, 0:Q_BLOCK, :] = jnp.where(low, v, 0.0).astype(BF16)
        dst[b, Q_BLOCK:2 * Q_BLOCK, :] = jnp.where(low, 0.0, v).astype(BF16)
        return 0

    lax.fori_loop(0, n_kb, fill, 0)


def _attn_fwd(p, qg2, kg2, d):
    s = p.shape[0]
    n_pairs = d // LANES
    qsb = _tile(s, Q_SUPER)
    n_sub, n_sb, n_kb = qsb // Q_BLOCK, s // qsb, s // Q_BLOCK
    unroll = math.gcd(KEY_UNROLL, n_sub)
    chunk = _tile(s, 512)
    inv_sqrt = 1.0 / math.sqrt(HEAD_DIM)

    def body(q_ref, k_ref, v_ref, qg_ref, kg_ref, o_ref, lt_ref, qs, k2, v2, run, acc):
        low, causal, w4 = _attn_consts(False)

        def prep(r, _):
            rows = pl.ds(pl.multiple_of(r * chunk, chunk), chunk)
            qs[rows, :] = (_pair_norm(q_ref[rows, :], low)[0] * (qg_ref[...] * inv_sqrt)).astype(BF16)
            return 0

        lax.fori_loop(0, s // chunk, prep, 0)
        _fill_pair_blocks(k2, lambda rows: _pair_norm(k_ref[rows, :], low)[0] * kg_ref[...], low, n_kb)
        _fill_pair_blocks(v2, lambda rows: v_ref[rows, :], low, n_kb)

        def step(sb, j, t0=0, diag_t=None):
            rows = pl.ds(pl.multiple_of(sb * qsb + t0 * Q_BLOCK, Q_BLOCK), (n_sub - t0) * Q_BLOCK)
            z_both = _dot_nt(qs[rows, :], k2[j])
            zls, cats = [], []
            for t in range(t0, n_sub):
                sub = slice((t - t0) * Q_BLOCK, (t - t0 + 1) * Q_BLOCK)
                for h in range(2):
                    z = z_both[sub, h * LANES:(h + 1) * LANES]
                    ln = _log_not(z)
                    if t == diag_t:
                        ln = jnp.where(causal, ln, 0.0)
                    zls.append(z + ln)
                    cats.append(_split_cat(ln))
            c2 = _dot(jnp.concatenate(cats, axis=0), w4)
            a_rows = []
            for t in range(t0, n_sub):
                sub = slice(t * Q_BLOCK, (t + 1) * Q_BLOCK)
                a_pair = []
                for h in range(2):
                    i = 2 * (t - t0) + h
                    tile = slice(i * Q_BLOCK, (i + 1) * Q_BLOCK)
                    later = run[h, sub, :]
                    log_a = zls[i] + c2[tile, :LANES] + later
                    if t == diag_t:
                        log_a = jnp.where(causal, log_a, -1e30)
                    a_pair.append(jnp.exp(log_a).astype(BF16))
                    run[h, sub, :] = later + c2[tile, LANES:]
                a_rows.append(jnp.concatenate(a_pair, axis=1))
            acc[t0 * Q_BLOCK:, :] += _dot(jnp.concatenate(a_rows, axis=0), v2[j])

        def super_block(sb, _):
            run[...] = jnp.zeros_like(run)
            acc[...] = jnp.zeros_like(acc)
            for t in reversed(range(n_sub)):
                step(sb, sb * n_sub + t, t0=t, diag_t=t)

            def below(n, _):
                for u in range(unroll):
                    step(sb, sb * n_sub - 1 - (unroll * n + u))
                return 0

            lax.fori_loop(0, sb * (n_sub // unroll), below, 0)
            rows_sb = pl.ds(pl.multiple_of(sb * qsb, qsb), qsb)
            o_ref[rows_sb, :] = acc[...].astype(BF16)
            lt_ref[rows_sb, :] = jnp.where(low, run[0], run[1])
            return 0

        lax.fori_loop(0, n_sb, super_block, 0)

    def seg(k):
        return pl.BlockSpec((s, LANES), lambda h, k=k: (0, k * n_pairs + h))

    vec = pl.BlockSpec((1, LANES), lambda h: (0, 0))
    out = pl.BlockSpec((s, LANES), lambda h: (0, h))
    return pl.pallas_call(
        body, name="attn_fwd", grid=(n_pairs,),
        in_specs=[seg(0), seg(1), seg(2), vec, vec], out_specs=[out, out],
        out_shape=[jax.ShapeDtypeStruct((s, d), BF16), jax.ShapeDtypeStruct((s, d), F32)],
        scratch_shapes=[pltpu.VMEM((s, LANES), BF16)] + [pltpu.VMEM((n_kb, 2 * Q_BLOCK, LANES), BF16)] * 2
        + [pltpu.VMEM((2, qsb, LANES), F32), pltpu.VMEM((qsb, LANES), F32)],
        compiler_params=_params("parallel"),
    )(p, p, p, qg2, kg2)


def _conv_rows(s):
    return _tile(s, 512)


def _conv_fwd(p, conv_w, d):
    s = p.shape[0]
    nb = d // LANES
    rows_n = _conv_rows(s)

    def body(cb_ref, cc_ref, cx_ref, w_ref, y_ref, us):
        us[pl.ds(0, 8), :] = jnp.zeros((8, LANES), F32)

        def fill(r, _):
            rows = pl.ds(pl.multiple_of(r * rows_n, rows_n), rows_n)
            us[pl.ds(pl.multiple_of(r * rows_n + 8, 8), rows_n), :] = cc_ref[rows, :] * cx_ref[rows, :]
            return 0

        lax.fori_loop(0, s // rows_n, fill, 0)
        w = w_ref[...]

        def out(r, _):
            rows = pl.ds(pl.multiple_of(r * rows_n, rows_n), rows_n)
            ext = us[pl.ds(pl.multiple_of(r * rows_n, 8), rows_n + 8), :]
            cv = (w[0:1, :] * pltpu.roll(ext, 2, 0)[8:, :] + w[1:2, :] * pltpu.roll(ext, 1, 0)[8:, :]
                  + w[2:3, :] * ext[8:, :])
            y_ref[rows, :] = (cb_ref[rows, :] * cv).astype(BF16)
            return 0

        lax.fori_loop(0, s // rows_n, out, 0)

    def seg(k):
        return pl.BlockSpec((s, LANES), lambda b, k=k: (0, k * nb + b))

    return pl.pallas_call(
        body, name="conv_fwd", grid=(nb,),
        in_specs=[seg(3), seg(4), seg(5), pl.BlockSpec((3, LANES), lambda b: (0, b))],
        out_specs=pl.BlockSpec((s, LANES), lambda b: (0, b)),
        out_shape=jax.ShapeDtypeStruct((s, d), BF16),
        scratch_shapes=[pltpu.VMEM((s + 8, LANES), F32)],
        compiler_params=_params("parallel"),
    )(p, p, p, conv_w)


def _branch(ya, yb, p, wa, wb, d):
    s = ya.shape[0]
    tm = _tile(s, 512)

    def body(ya_ref, yb_ref, ga_ref, gb_ref, wa_ref, wb_ref, m_ref, a_ref, b_ref):
        pa = _dot(ya_ref[...], wa_ref[...])
        pb = _dot(yb_ref[...], wb_ref[...])
        m_ref[...] = (jax.nn.sigmoid(ga_ref[...]) * pa + jax.nn.sigmoid(gb_ref[...]) * pb).astype(BF16)
        a_ref[...] = pa.astype(BF16)
        b_ref[...] = pb.astype(BF16)

    row = pl.BlockSpec((tm, d), lambda i: (i, 0))
    wsp = pl.BlockSpec((d, d), lambda i: (0, 0))
    shp = jax.ShapeDtypeStruct((s, d), BF16)
    return pl.pallas_call(
        body, name="branch", grid=(s // tm,),
        in_specs=[row, row, pl.BlockSpec((tm, d), lambda i: (i, 6)), pl.BlockSpec((tm, d), lambda i: (i, 7)), wsp, wsp],
        out_specs=[row, row, row], out_shape=[shp, shp, shp], compiler_params=_params("parallel"),
    )(ya, yb, p, p, wa, wb)


def _out_proj(merged, wout, x0, g1):
    s, d = x0.shape
    tm = _tile(s, 1024)

    def body(m_ref, w_ref, x_ref, g_ref, x1_ref, mo_ref):
        mo = _dot(m_ref[...], w_ref[...])
        mo_ref[...] = mo
        x1_ref[...] = x_ref[...] + g_ref[...] * mo

    row = pl.BlockSpec((tm, d), lambda i: (i, 0))
    shp = jax.ShapeDtypeStruct((s, d), F32)
    return pl.pallas_call(
        body, name="out_proj", grid=(s // tm,),
        in_specs=[row, pl.BlockSpec((d, d), lambda i: (0, 0)), row, pl.BlockSpec((1, d), lambda i: (0, 0))],
        out_specs=[row, row], out_shape=[shp, shp], compiler_params=_params("parallel"),
    )(merged, wout, x0, g1)


def _ffn_up(h, wg_g, wu_g):
    s, d = h.shape
    f4 = wg_g.shape[-1]
    tm = _tile(s, 1024)

    def body(h_ref, wg_ref, wu_ref, gate_ref, up_ref, act_ref):
        hv = h_ref[...]
        gt = _dot(hv, wg_ref[...])
        up = _dot(hv, wu_ref[...])
        gate_ref[...] = gt.astype(BF16)
        up_ref[...] = up.astype(BF16)
        act_ref[...] = (gt * jax.nn.sigmoid(gt) * up).astype(BF16)

    wsp = pl.BlockSpec((None, d, f4), lambda j, i: (j, 0, 0))
    osp = pl.BlockSpec((None, tm, f4), lambda j, i: (j, i, 0))
    shp = jax.ShapeDtypeStruct((N_CHIP, s, f4), BF16)
    return pl.pallas_call(
        body, name="ffn_up", grid=(N_CHIP, s // tm),
        in_specs=[pl.BlockSpec((tm, d), lambda j, i: (i, 0)), wsp, wsp],
        out_specs=[osp, osp, osp], out_shape=[shp, shp, shp], compiler_params=_params("parallel", "parallel"),
    )(h, wg_g, wu_g)


def _ffn_down(act, wd_g, x1, g2):
    s, d = x1.shape
    f4 = act.shape[-1]
    tm = _tile(s, 1024)

    def body(a_ref, w_ref, x_ref, g_ref, x2_ref, f_ref, acc):
        j = pl.program_id(1)

        @pl.when(j == 0)
        def _():
            acc[...] = jnp.zeros_like(acc)

        acc[...] += _dot(a_ref[...], w_ref[...])

        @pl.when(j == N_CHIP - 1)
        def _():
            f = acc[...]
            f_ref[...] = f
            x2_ref[...] = x_ref[...] + g_ref[...] * f

    row = pl.BlockSpec((tm, d), lambda i, j: (i, 0))
    shp = jax.ShapeDtypeStruct((s, d), F32)
    return pl.pallas_call(
        body, name="ffn_down", grid=(s // tm, N_CHIP),
        in_specs=[pl.BlockSpec((None, tm, f4), lambda i, j: (j, i, 0)),
                  pl.BlockSpec((None, f4, d), lambda i, j: (j, 0, 0)),
                  row, pl.BlockSpec((1, d), lambda i, j: (0, 0))],
        out_specs=[row, row], out_shape=[shp, shp],
        scratch_shapes=[pltpu.VMEM((tm, d), F32)], compiler_params=_params("parallel", "arbitrary"),
    )(act, wd_g, x1, g2)


def _loss_head(y, target):
    s, d = y.shape
    tm = _tile(s, 1024)
    n_steps = s // tm

    def body(y_ref, t_ref, dy_ref, l_ref, acc):
        i = pl.program_id(0)

        @pl.when(i == 0)
        def _():
            acc[...] = jnp.zeros_like(acc)

        err = y_ref[...] - t_ref[...]
        dy_ref[...] = err / d
        acc[...] += jnp.sum(err * err, axis=0, keepdims=True)

        @pl.when(i == n_steps - 1)
        def _():
            l_ref[...] = jnp.broadcast_to(jnp.sum(acc[...], axis=1, keepdims=True), (8, LANES))

    row = pl.BlockSpec((tm, d), lambda i: (i, 0))
    return pl.pallas_call(
        body, name="loss_head", grid=(n_steps,), in_specs=[row, row],
        out_specs=[row, pl.BlockSpec((8, LANES), lambda i: (0, 0))],
        out_shape=[jax.ShapeDtypeStruct((s, d), F32), jax.ShapeDtypeStruct((8, LANES), F32)],
        scratch_shapes=[pltpu.VMEM((1, d), F32)], compiler_params=_params("arbitrary"),
    )(y, target)


def _mm_tn(a, b, a_spec, b_spec, out_rc, name):
    r, c = out_rc
    s = a.shape[-2]
    tk = _tile(s, 1024)
    nk = s // tk

    def body(a_ref, b_ref, o_ref, acc):
        k = pl.program_id(1)

        @pl.when(k == 0)
        def _():
            acc[...] = jnp.zeros_like(acc)

        acc[...] += _dot_tn(a_ref[...], b_ref[...])

        @pl.when(k == nk - 1)
        def _():
            o_ref[...] = acc[...].astype(BF16)

    return pl.pallas_call(
        body, name=name, grid=(N_CHIP, nk),
        in_specs=[pl.BlockSpec(*a_spec(tk)), pl.BlockSpec(*b_spec(tk))],
        out_specs=pl.BlockSpec((None, r, c), lambda j, k: (j, 0, 0)),
        out_shape=jax.ShapeDtypeStruct((N_CHIP, r, c), BF16),
        scratch_shapes=[pltpu.VMEM((r, c), F32)], compiler_params=_params("parallel", "arbitrary"),
    )(a, b)


def _mm_tn_square(a, b, name):
    s, d = a.shape
    r4 = d // N_CHIP
    tk = _tile(s, 1024)
    nk = s // tk

    def body(a_ref, b_ref, o_ref, acc):
        k = pl.program_id(0)

        @pl.when(k == 0)
        def _():
            acc[...] = jnp.zeros_like(acc)

        acc[...] += _dot_tn(a_ref[...], b_ref[...])

        @pl.when(k == nk - 1)
        def _():
            for j in range(N_CHIP):
                o_ref[j] = acc[j * r4:(j + 1) * r4, :].astype(BF16)

    blk = pl.BlockSpec((tk, d), lambda k: (k, 0))
    return pl.pallas_call(
        body, name=name, grid=(nk,), in_specs=[blk, blk],
        out_specs=pl.BlockSpec((N_CHIP, r4, d), lambda k: (0, 0, 0)),
        out_shape=jax.ShapeDtypeStruct((N_CHIP, r4, d), BF16),
        scratch_shapes=[pltpu.VMEM((d, d), F32)], compiler_params=_params("arbitrary"),
    )(a, b)


def _ffn_bwd1(dx2, f, g2, wd_g, gate, up):
    s, d = dx2.shape
    f4 = gate.shape[-1]
    tm = _tile(s, 1024)

    def body(dx_ref, f_ref, g_ref, w_ref, gate_ref, up_ref, dgate_ref, dup_ref, df_ref, dg_ref):
        i, j = pl.program_id(0), pl.program_id(1)

        @pl.when((i == 0) & (j == 0))
        def _():
            dg_ref[...] = jnp.zeros_like(dg_ref)

        dxv = dx_ref[...]
        df = (g_ref[...] * dxv).astype(BF16)

        @pl.when(j == 0)
        def _():
            df_ref[...] = df
            dg_ref[0:1, :] += jnp.sum(dxv * f_ref[...], axis=0, keepdims=True)

        da = _dot_nt(df, w_ref[...])
        gt = gate_ref[...].astype(F32)
        sg = jax.nn.sigmoid(gt)
        dup_ref[...] = (da * gt * sg).astype(BF16)
        dgate_ref[...] = (da * up_ref[...].astype(F32) * (sg * (1.0 + gt * (1.0 - sg)))).astype(BF16)

    row = pl.BlockSpec((tm, d), lambda i, j: (i, 0))
    hsp = pl.BlockSpec((None, tm, f4), lambda i, j: (j, i, 0))
    hshp = jax.ShapeDtypeStruct((N_CHIP, s, f4), BF16)
    return pl.pallas_call(
        body, name="ffn_bwd1", grid=(s // tm, N_CHIP),
        in_specs=[row, row, pl.BlockSpec((1, d), lambda i, j: (0, 0)),
                  pl.BlockSpec((None, f4, d), lambda i, j: (j, 0, 0)), hsp, hsp],
        out_specs=[hsp, hsp, row, pl.BlockSpec((8, d), lambda i, j: (0, 0))],
        out_shape=[hshp, hshp, jax.ShapeDtypeStruct((s, d), BF16), jax.ShapeDtypeStruct((8, d), F32)],
        compiler_params=_params("arbitrary", "arbitrary"),
    )(dx2, f, g2, wd_g, gate, up)


def _ffn_bwd2(dgate, dup, wg_g, wu_g):
    _, s, f4 = dgate.shape
    d = wg_g.shape[-2]
    tm = _tile(s, 1024)

    def body(dg_ref, du_ref, wg_ref, wu_ref, o_ref, acc):
        j = pl.program_id(1)

        @pl.when(j == 0)
        def _():
            acc[...] = jnp.zeros_like(acc)

        acc[...] += _dot_nt(dg_ref[...], wg_ref[...]) + _dot_nt(du_ref[...], wu_ref[...])

        @pl.when(j == N_CHIP - 1)
        def _():
            o_ref[...] = acc[...]

    hsp = pl.BlockSpec((None, tm, f4), lambda i, j: (j, i, 0))
    wsp = pl.BlockSpec((None, d, f4), lambda i, j: (j, 0, 0))
    return pl.pallas_call(
        body, name="ffn_bwd2", grid=(s // tm, N_CHIP), in_specs=[hsp, hsp, wsp, wsp],
        out_specs=pl.BlockSpec((tm, d), lambda i, j: (i, 0)), out_shape=jax.ShapeDtypeStruct((s, d), F32),
        scratch_shapes=[pltpu.VMEM((tm, d), F32)], compiler_params=_params("parallel", "arbitrary"),
    )(dgate, dup, wg_g, wu_g)


def _lnmod_bwd(x, g, sc, dh, dres):
    s, d = x.shape
    tm = _tile(s, 1024)

    def body(x_ref, g_ref, sc_ref, dh_ref, dr_ref, dx_ref, sums_ref):
        @pl.when(pl.program_id(0) == 0)
        def _():
            sums_ref[...] = jnp.zeros_like(sums_ref)

        xv, dhv, gv = x_ref[...], dh_ref[...], g_ref[...]
        r = lax.rsqrt(jnp.mean(xv * xv, axis=-1, keepdims=True) + EPS)
        n = xv * r
        one_sc = 1.0 + sc_ref[...]
        dt = dhv * one_sc
        sums_ref[0:1, :] += jnp.sum(dhv, axis=0, keepdims=True)
        sums_ref[1:2, :] += jnp.sum(dhv * (n * gv), axis=0, keepdims=True)
        sums_ref[2:3, :] += jnp.sum(dt * n, axis=0, keepdims=True)
        dn = dt * gv
        dx_ref[...] = dr_ref[...] + r * (dn - n * jnp.mean(dn * n, axis=-1, keepdims=True))

    vec = pl.BlockSpec((1, d), lambda i: (0, 0))
    row = pl.BlockSpec((tm, d), lambda i: (i, 0))
    return pl.pallas_call(
        body, name="lnmod_bwd", grid=(s // tm,), in_specs=[row, vec, vec, row, row],
        out_specs=[row, pl.BlockSpec((8, d), lambda i: (0, 0))],
        out_shape=[jax.ShapeDtypeStruct((s, d), F32), jax.ShapeDtypeStruct((8, d), F32)],
        compiler_params=_params("arbitrary"),
    )(x, g, sc, dh, dres)


def _out_bwd(dx1, mo, g1, wout, pa, pb, p, wa, wb, d):
    s = dx1.shape[0]
    tm = _tile(s, 256)

    def body(dx_ref, mo_ref, g_ref, wo_ref, pa_ref, pb_ref, ga_ref, gb_ref, wa_ref, wb_ref,
             dmo_ref, da_ref, db_ref, dya_ref, dyb_ref, dp_ref, dg_ref):
        @pl.when(pl.program_id(0) == 0)
        def _():
            dg_ref[...] = jnp.zeros_like(dg_ref)

        dxv = dx_ref[...]
        dg_ref[0:1, :] += jnp.sum(dxv * mo_ref[...], axis=0, keepdims=True)
        dmo = (g_ref[...] * dxv).astype(BF16)
        dmo_ref[...] = dmo
        dm = _dot_nt(dmo, wo_ref[...])
        sa, sb = jax.nn.sigmoid(ga_ref[...]), jax.nn.sigmoid(gb_ref[...])
        da = (dm * sa).astype(BF16)
        db = (dm * sb).astype(BF16)
        da_ref[...] = da
        db_ref[...] = db
        dp_ref[:, :d] = (dm * pa_ref[...].astype(F32) * (sa * (1.0 - sa))).astype(BF16)
        dp_ref[:, d:] = (dm * pb_ref[...].astype(F32) * (sb * (1.0 - sb))).astype(BF16)
        dya_ref[...] = _dot_nt(da, wa_ref[...]).astype(BF16)
        dyb_ref[...] = _dot_nt(db, wb_ref[...]).astype(BF16)

    row = pl.BlockSpec((tm, d), lambda i: (i, 0))
    wsp = pl.BlockSpec((d, d), lambda i: (0, 0))
    shp = jax.ShapeDtypeStruct((s, d), BF16)
    return pl.pallas_call(
        body, name="out_bwd", grid=(s // tm,),
        in_specs=[row, row, pl.BlockSpec((1, d), lambda i: (0, 0)), wsp, row, row,
                  pl.BlockSpec((tm, d), lambda i: (i, 6)), pl.BlockSpec((tm, d), lambda i: (i, 7)), wsp, wsp],
        out_specs=[row] * 5 + [pl.BlockSpec((tm, 2 * d), lambda i: (i, 3)), pl.BlockSpec((8, d), lambda i: (0, 0))],
        out_shape=[shp] * 5 + [jax.ShapeDtypeStruct((s, 8 * d), BF16), jax.ShapeDtypeStruct((8, d), F32)],
        compiler_params=_params("arbitrary"),
    )(dx1, mo, g1, wout, pa, pb, p, p, wa, wb)


def _store_segments(outs, dp_out, sems, col_blocks):
    copies = [pltpu.make_async_copy(outs.at[k], dp_out.at[:, pl.ds(pl.multiple_of(cb * LANES, LANES), LANES)],
                                    sems.at[k]) for k, cb in enumerate(col_blocks)]
    for cp in copies:
        cp.start()
    for cp in copies:
        cp.wait()


def _conv_bwd(p, conv_w, dyb, dp, d):
    s = p.shape[0]
    nb = d // LANES
    rows_n = _conv_rows(s)

    def compute(cb_ref, cc_ref, cx_ref, w_ref, dy_ref, dcb_ref, dcc_ref, dcx_ref, dw_ref, us, ds):
        us[pl.ds(0, 8), :] = jnp.zeros((8, LANES), F32)
        ds[pl.ds(s, 8), :] = jnp.zeros((8, LANES), F32)

        def fill(r, _):
            rows = pl.ds(pl.multiple_of(r * rows_n, rows_n), rows_n)
            us[pl.ds(pl.multiple_of(r * rows_n + 8, 8), rows_n), :] = cc_ref[rows, :] * cx_ref[rows, :]
            ds[rows, :] = dy_ref[rows, :].astype(F32) * cb_ref[rows, :]
            return 0

        lax.fori_loop(0, s // rows_n, fill, 0)
        w = w_ref[...]

        def out(r, carry):
            dw0, dw1, dw2 = carry
            rows = pl.ds(pl.multiple_of(r * rows_n, rows_n), rows_n)
            ext = us[pl.ds(pl.multiple_of(r * rows_n, 8), rows_n + 8), :]
            u0, u1, u2 = ext[8:, :], pltpu.roll(ext, 1, 0)[8:, :], pltpu.roll(ext, 2, 0)[8:, :]
            cv = w[0:1, :] * u2 + w[1:2, :] * u1 + w[2:3, :] * u0
            dcb_ref[rows, :] = (dy_ref[rows, :].astype(F32) * cv).astype(BF16)
            nxt = ds[pl.ds(pl.multiple_of(r * rows_n, 8), rows_n + 8), :]
            e0 = nxt[:rows_n, :]
            e1 = pltpu.roll(nxt, rows_n + 7, 0)[:rows_n, :]
            e2 = pltpu.roll(nxt, rows_n + 6, 0)[:rows_n, :]
            du = w[2:3, :] * e0 + w[1:2, :] * e1 + w[0:1, :] * e2
            dcc_ref[rows, :] = (du * cx_ref[rows, :]).astype(BF16)
            dcx_ref[rows, :] = (du * cc_ref[rows, :]).astype(BF16)
            return (dw0 + jnp.sum(e0 * u2, axis=0, keepdims=True), dw1 + jnp.sum(e0 * u1, axis=0, keepdims=True),
                    dw2 + jnp.sum(e0 * u0, axis=0, keepdims=True))

        zero = jnp.zeros((1, LANES), F32)
        dw0, dw1, dw2 = lax.fori_loop(0, s // rows_n, out, (zero, zero, zero))
        dw_ref[...] = jnp.zeros_like(dw_ref)
        dw_ref[0:1, :] = dw0
        dw_ref[1:2, :] = dw1
        dw_ref[2:3, :] = dw2

    def body(cb_ref, cc_ref, cx_ref, w_ref, dy_ref, dp_in, dp_out, dw_ref, us, ds, outs, sems):
        del dp_in
        compute(cb_ref, cc_ref, cx_ref, w_ref, dy_ref, outs.at[0], outs.at[1], outs.at[2], dw_ref, us, ds)
        _store_segments(outs, dp_out, sems, [(3 + k) * nb + pl.program_id(0) for k in range(3)])

    def seg(k):
        return pl.BlockSpec((s, LANES), lambda b, k=k: (0, k * nb + b))

    return pl.pallas_call(
        body, name="conv_bwd", grid=(nb,),
        in_specs=[seg(3), seg(4), seg(5), pl.BlockSpec((3, LANES), lambda b: (0, b)),
                  pl.BlockSpec((s, LANES), lambda b: (0, b)), ANY_SPEC],
        out_specs=[ANY_SPEC, pl.BlockSpec((8, LANES), lambda b: (0, b))],
        out_shape=[jax.ShapeDtypeStruct(dp.shape, BF16), jax.ShapeDtypeStruct((8, d), F32)],
        input_output_aliases={5: 0},
        scratch_shapes=[pltpu.VMEM((s + 8, LANES), F32), pltpu.VMEM((s + 8, LANES), F32),
                        pltpu.VMEM((3, s, LANES), BF16), pltpu.SemaphoreType.DMA((3,))],
        compiler_params=_params("arbitrary"),
    )(p, p, p, conv_w, dyb, dp)


def _attn_bwd(p, qg2, kg2, dy, lt, dp, d):
    s = p.shape[0]
    n_pairs = d // LANES
    qsb = _tile(s, Q_SUPER_BWD)
    n_sub, n_sb, n_kb = qsb // Q_BLOCK, s // qsb, s // Q_BLOCK
    unroll = math.gcd(KEY_UNROLL, n_sub)
    chunk = _tile(s, 512)
    inv_sqrt = 1.0 / math.sqrt(HEAD_DIM)

    def compute(q_ref, k_ref, v_ref, qg_ref, kg_ref, dy_ref, lt_ref, dq_ref, dk_ref, dv_ref, dgain_ref,
                qs, k2, v2, dkt, dvt, qt, dyt, rem, gbef, dqa):
        low, causal, w4 = _attn_consts(True)

        def prep(r, _):
            rows = pl.ds(pl.multiple_of(r * chunk, chunk), chunk)
            qs[rows, :] = (_pair_norm(q_ref[rows, :], low)[0] * (qg_ref[...] * inv_sqrt)).astype(BF16)
            return 0

        lax.fori_loop(0, s // chunk, prep, 0)
        _fill_pair_blocks(k2, lambda rows: _pair_norm(k_ref[rows, :], low)[0] * kg_ref[...], low, n_kb)
        _fill_pair_blocks(v2, lambda rows: v_ref[rows, :], low, n_kb)

        def clear(b, _):
            dkt[b] = jnp.zeros((LANES, Q_BLOCK), F32)
            dvt[b] = jnp.zeros((LANES, Q_BLOCK), F32)
            return 0

        lax.fori_loop(0, n_kb, clear, 0)

        def step(sb, j, t0=0, diag_t=None):
            rows = pl.ds(pl.multiple_of(sb * qsb + t0 * Q_BLOCK, Q_BLOCK), (n_sub - t0) * Q_BLOCK)
            kj2, vj2 = k2[j], v2[j]
            z_both = _dot_nt(qs[rows, :], kj2)
            da_both = _dot_nt(dy_ref[rows, :], vj2)
            zls, cats = [], []
            for t in range(t0, n_sub):
                sub = slice((t - t0) * Q_BLOCK, (t - t0 + 1) * Q_BLOCK)
                for h in range(2):
                    z = z_both[sub, h * LANES:(h + 1) * LANES]
                    ln = _log_not(z)
                    if t == diag_t:
                        ln = jnp.where(causal, ln, 0.0)
                    zls.append(z + ln)
                    cats.append(_split_cat(ln))
            c2 = _dot(jnp.concatenate(cats, axis=0), w4)
            a_rows, gs, cats = [], [], []
            for t in range(t0, n_sub):
                sub = slice(t * Q_BLOCK, (t + 1) * Q_BLOCK)
                a_pair = []
                for h in range(2):
                    i = 2 * (t - t0) + h
                    tile = slice(i * Q_BLOCK, (i + 1) * Q_BLOCK)
                    left = rem[h, sub, :]
                    log_a = zls[i] + (left - c2[tile, :LANES])
                    if t == diag_t:
                        log_a = jnp.where(causal, log_a, -1e30)
                    a = jnp.exp(log_a)
                    rem[h, sub, :] = left - c2[tile, LANES:]
                    g = a * da_both[(t - t0) * Q_BLOCK:(t - t0 + 1) * Q_BLOCK, h * LANES:(h + 1) * LANES]
                    a_pair.append(a.astype(BF16))
                    gs.append(g)
                    cats.append(_split_cat(g))
                a_rows.append(jnp.concatenate(a_pair, axis=1))
            c2g = _dot(jnp.concatenate(cats, axis=0), w4)
            dz_rows = []
            for t in range(t0, n_sub):
                sub = slice(t * Q_BLOCK, (t + 1) * Q_BLOCK)
                dz_pair = []
                for h in range(2):
                    i = 2 * (t - t0) + h
                    tile = slice(i * Q_BLOCK, (i + 1) * Q_BLOCK)
                    before = gbef[h, sub, :]
                    dz = gs[i] - jnp.exp(zls[i]) * (before + c2g[tile, :LANES])
                    if t == diag_t:
                        dz = jnp.where(causal, dz, 0.0)
                    gbef[h, sub, :] = before + c2g[tile, LANES:]
                    dz_pair.append(dz.astype(BF16))
                dz_rows.append(jnp.concatenate(dz_pair, axis=1))
            a_both = jnp.concatenate(a_rows, axis=0)
            dz_both = jnp.concatenate(dz_rows, axis=0)
            used = slice(t0 * Q_BLOCK, qsb)
            dvt[j] += _dot(dyt[0, :, used], a_both[:, :LANES]) + _dot(dyt[1, :, used], a_both[:, LANES:])
            dkt[j] += _dot(qt[0, :, used], dz_both[:, :LANES]) + _dot(qt[1, :, used], dz_both[:, LANES:])
            dqa[used, :] += _dot(dz_both, kj2)

        def super_block(sb, dqg):
            rows_sb = pl.ds(pl.multiple_of(sb * qsb, qsb), qsb)
            total = lt_ref[rows_sb, :]
            other = pltpu.roll(total, HEAD_DIM, 1)
            rem[0] = jnp.where(low, total, other)
            rem[1] = jnp.where(low, other, total)
            gbef[...] = jnp.zeros_like(gbef)
            dqa[...] = jnp.zeros_like(dqa)
            qv = qs[rows_sb, :].astype(F32)
            dyv = dy_ref[rows_sb, :].astype(F32)
            qt[0] = jnp.where(low, qv, 0.0).T.astype(BF16)
            qt[1] = jnp.where(low, 0.0, qv).T.astype(BF16)
            dyt[0] = jnp.where(low, dyv, 0.0).T.astype(BF16)
            dyt[1] = jnp.where(low, 0.0, dyv).T.astype(BF16)

            def below(n, _):
                for u in range(unroll):
                    step(sb, unroll * n + u)
                return 0

            lax.fori_loop(0, sb * (n_sub // unroll), below, 0)
            for t in range(n_sub):
                step(sb, sb * n_sub + t, t0=t, diag_t=t)
            qhat, r = _pair_norm(q_ref[rows_sb, :], low)
            dqn = dqa[...]
            dqhat = dqn * (qg_ref[...] * inv_sqrt)
            dq_ref[rows_sb, :] = (r * (dqhat - qhat * _pair_mean(dqhat * qhat, low))).astype(BF16)
            return dqg + jnp.sum(dqn * qhat, axis=0, keepdims=True) * inv_sqrt

        dqg = lax.fori_loop(0, n_sb, super_block, jnp.zeros((1, LANES), F32))

        def finish(b, dkg):
            rows = pl.ds(pl.multiple_of(b * Q_BLOCK, Q_BLOCK), Q_BLOCK)
            khat, rk = _pair_norm(k_ref[rows, :], low)
            dkn = dkt[b].T
            dkhat = dkn * kg_ref[...]
            dk_ref[rows, :] = (rk * (dkhat - khat * _pair_mean(dkhat * khat, low))).astype(BF16)
            dv_ref[rows, :] = dvt[b].T.astype(BF16)
            return dkg + jnp.sum(dkn * khat, axis=0, keepdims=True)

        dkg = lax.fori_loop(0, n_kb, finish, jnp.zeros((1, LANES), F32))
        dgain_ref[...] = jnp.zeros_like(dgain_ref)
        dgain_ref[0:1, :] = dqg
        dgain_ref[1:2, :] = dkg

    def body(q_ref, k_ref, v_ref, qg_ref, kg_ref, dy_ref, lt_ref, dp_in, dp_out, dgain_ref, outs, sems, *scratch):
        del dp_in
        compute(q_ref, k_ref, v_ref, qg_ref, kg_ref, dy_ref, lt_ref, outs.at[0], outs.at[1], outs.at[2], dgain_ref,
                *scratch)
        _store_segments(outs, dp_out, sems, [k * n_pairs + pl.program_id(0) for k in range(3)])

    def seg(k):
        return pl.BlockSpec((s, LANES), lambda h, k=k: (0, k * n_pairs + h))

    vec = pl.BlockSpec((1, LANES), lambda h: (0, 0))
    col = pl.BlockSpec((s, LANES), lambda h: (0, h))
    return pl.pallas_call(
        body, name="attn_bwd", grid=(n_pairs,),
        in_specs=[seg(0), seg(1), seg(2), vec, vec, col, col, ANY_SPEC],
        out_specs=[ANY_SPEC, pl.BlockSpec((None, 8, LANES), lambda h: (h, 0, 0))],
        out_shape=[jax.ShapeDtypeStruct(dp.shape, BF16), jax.ShapeDtypeStruct((n_pairs, 8, LANES), F32)],
        input_output_aliases={7: 0},
        scratch_shapes=[pltpu.VMEM((3, s, LANES), BF16), pltpu.SemaphoreType.DMA((3,)), pltpu.VMEM((s, LANES), BF16)]
        + [pltpu.VMEM((n_kb, 2 * Q_BLOCK, LANES), BF16)] * 2
        + [pltpu.VMEM((n_kb, LANES, Q_BLOCK), F32)] * 2
        + [pltpu.VMEM((2, LANES, qsb), BF16)] * 2
        + [pltpu.VMEM((2, qsb, LANES), F32)] * 2 + [pltpu.VMEM((qsb, LANES), F32)],
        compiler_params=_params("arbitrary"),
    )(p, p, p, qg2, kg2, dy, lt, dp)


def _mm_in_bwd(dp, w_g):
    s = dp.shape[0]
    d, n4 = w_g.shape[-2:]
    tm = _tile(s, 1024)

    def body(a_ref, w_ref, o_ref, acc):
        j = pl.program_id(1)

        @pl.when(j == 0)
        def _():
            acc[...] = jnp.zeros_like(acc)

        acc[...] += _dot_nt(a_ref[...], w_ref[...])

        @pl.when(j == N_CHIP - 1)
        def _():
            o_ref[...] = acc[...]

    return pl.pallas_call(
        body, name="mm_in_bwd", grid=(s // tm, N_CHIP),
        in_specs=[pl.BlockSpec((tm, n4), lambda i, j: (i, j)),
                  pl.BlockSpec((None, d, n4), lambda i, j: (j, 0, 0))],
        out_specs=pl.BlockSpec((tm, d), lambda i, j: (i, 0)), out_shape=jax.ShapeDtypeStruct((s, d), F32),
        scratch_shapes=[pltpu.VMEM((tm, d), F32)], compiler_params=_params("parallel", "arbitrary"),
    )(dp, w_g)


def _sum_adam(parts, w, m, v, name):
    n_l, r, c = w.shape
    tr = next((t for t in (256, 176, 128, 64, 32, 16) if r % t == 0 and t * c <= 256 * 1024), r)
    n_blk = r // tr

    def body(*refs):
        p_refs = refs[:n_l]
        w_ref, m_ref, v_ref, g_ref, dl_ref, nm_ref, nv_ref = refs[n_l:]
        for l in range(n_l):
            @pl.when(pl.program_id(0) == l)
            def _(p_ref=p_refs[l]):
                g = p_ref[0].astype(F32)
                for dev in range(1, N_DEV):
                    g = g + p_ref[dev].astype(F32)
                g_ref[...] = g
                delta, nm, nv = _adamw(w_ref[...], g, m_ref[...], v_ref[...])
                dl_ref[...] = delta
                nm_ref[...] = nm
                nv_ref[...] = nv

    def part_spec(l):
        return pl.BlockSpec((N_DEV, tr, c), lambda ll, i, l=l: (0, jnp.where(ll == l, i, jnp.where(ll < l, 0, n_blk - 1)), 0))

    wsp = pl.BlockSpec((None, tr, c), lambda l, i: (l, i, 0))
    shp = jax.ShapeDtypeStruct(w.shape, F32)
    return pl.pallas_call(
        body, name=name, grid=(n_l, n_blk),
        in_specs=[part_spec(l) for l in range(n_l)] + [wsp, wsp, wsp],
        out_specs=[wsp] * 4, out_shape=[shp] * 4, compiler_params=_params("arbitrary", "arbitrary"),
    )(*parts, w, m, v)


def _small_adam(parts, w, m, v):
    def body(p_ref, w_ref, m_ref, v_ref, g_ref, dl_ref, nm_ref, nv_ref):
        g = p_ref[0]
        for dev in range(1, N_DEV):
            g = g + p_ref[dev]
        g_ref[...] = g
        delta, nm, nv = _adamw(w_ref[...], g, m_ref[...], v_ref[...])
        dl_ref[...] = delta
        nm_ref[...] = nm
        nv_ref[...] = nv

    shp = jax.ShapeDtypeStruct(w.shape, F32)
    return pl.pallas_call(body, name="small_adam", in_specs=[VMEM_SPEC] * 4, out_specs=[VMEM_SPEC] * 4,
                          out_shape=[shp] * 4,
                          compiler_params=pltpu.CompilerParams(vmem_limit_bytes=VMEM_LIMIT_BYTES))(parts, w, m, v)


def _pack(vecs, mult=8 * LANES):
    flat = jnp.concatenate([a.reshape(-1).astype(F32) for a in vecs])
    pad = (-flat.shape[0]) % mult
    if pad:
        flat = jnp.concatenate([flat, jnp.zeros((pad,), F32)])
    return flat.reshape(8, -1)


def _unpack(flat, shapes):
    flat = flat.reshape(-1)
    out, off = [], 0
    for shp in shapes:
        n = math.prod(shp)
        out.append(flat[off:off + n].reshape(shp))
        off += n
    return out


BIG = ("win", "wa", "wb", "wo", "wg", "wu", "wd")
GRAD_GROUPS = (("wd", "wg", "wu"), ("wo", "wa", "wb"), ("win",))


def _local_step(x, target, mods, ln1_g, ln2_g, qg, kg, conv_w, weights, send_grads):
    s, d = x.shape
    n_l = mods.shape[0]
    saved = []
    h_in = x
    for l in range(n_l):
        sh1, sc1, g1, sh2, sc2, g2 = [mods[l, k * d:(k + 1) * d].reshape(1, d) for k in range(6)]
        qg2, kg2 = jnp.tile(qg[l:l + 1], (1, 2)), jnp.tile(kg[l:l + 1], (1, 2))
        h1 = _lnmod(h_in, ln1_g[l:l + 1], sc1, sh1)
        (win,), tie = weights(l, ("win",), h1)
        p = _mm_in(h1, win)
        ya, lt = _attn_fwd(p, qg2 + tie, kg2, d)
        yb = _conv_fwd(p, conv_w[l], d)
        (wa, wb, wo, wg, wu, wd), tie = weights(l, ("wa", "wb", "wo", "wg", "wu", "wd"), ya)
        wa, wb, wo = wa.reshape(d, d), wb.reshape(d, d), wo.reshape(d, d)
        merged, pa, pb = _branch(ya, yb, p, wa, wb, d)
        x1, mo = _out_proj(merged, wo, h_in, g1 + tie)
        h2 = _lnmod(x1, ln2_g[l:l + 1], sc2, sh2)
        gate, up, act = _ffn_up(h2, wg, wu)
        x2, f = _ffn_down(act, wd, x1, g2)
        saved.append(dict(x0=h_in, h1=h1, p=p, ya=ya, lt=lt, yb=yb, merged=merged, pa=pa, pb=pb, x1=x1, mo=mo,
                          h2=h2, gate=gate, up=up, act=act, f=f, win=win, wa=wa, wb=wb, wo=wo, wg=wg, wu=wu, wd=wd,
                          mod=(sh1, sc1, g1, sh2, sc2, g2), qg2=qg2, kg2=kg2))
        h_in = x2

    dx, loss_tile = _loss_head(h_in, target)

    small = [None] * n_l
    for l in reversed(range(n_l)):
        sv = saved[l]
        sh1, sc1, g1, sh2, sc2, g2 = sv["mod"]
        f4, n4 = sv["wg"].shape[-1], sv["win"].shape[-1]
        hsp = lambda tk: ((tk, d), lambda j, k: (k, 0))
        fsp = lambda tk: ((None, tk, f4), lambda j, k: (j, k, 0))
        dgate, dup, df, dg2 = _ffn_bwd1(dx, sv["f"], g2, sv["wd"], sv["gate"], sv["up"])
        g_wd = _mm_tn(sv["act"], df, fsp, hsp, (f4, d), "grad_wd")
        g_wg = _mm_tn(dgate, sv["h2"], fsp, hsp, (f4, d), "grad_wg")
        g_wu = _mm_tn(dup, sv["h2"], fsp, hsp, (f4, d), "grad_wu")
        tie = send_grads(l, dict(wd=g_wd, wg=g_wg, wu=g_wu))
        dh2 = _ffn_bwd2(dgate, dup, sv["wg"], sv["wu"])
        dx1, sums2 = _lnmod_bwd(sv["x1"], ln2_g[l:l + 1], sc2 + tie, dh2, dx)
        dmo, da, db, dya, dyb, dp, dg1 = _out_bwd(dx1, sv["mo"], g1, sv["wo"], sv["pa"], sv["pb"], sv["p"],
                                                        sv["wa"], sv["wb"], d)
        g_wo = _mm_tn_square(sv["merged"], dmo, "grad_wo")
        g_wa = _mm_tn_square(sv["ya"], da, "grad_wa")
        g_wb = _mm_tn_square(sv["yb"], db, "grad_wb")
        tie = send_grads(l, dict(wo=g_wo, wa=g_wa, wb=g_wb))
        dp, dconv = _conv_bwd(sv["p"], conv_w[l] + tie, dyb, dp, d)
        dp, dgain = _attn_bwd(sv["p"], sv["qg2"], sv["kg2"], dya, sv["lt"], dp, d)
        g_win = _mm_tn(sv["h1"], dp, hsp, lambda tk: ((tk, n4), lambda j, k: (k, j)), (d, n4), "grad_win")
        tie = send_grads(l, dict(win=g_win))
        dh1 = _mm_in_bwd(dp, sv["win"])
        dx, sums1 = _lnmod_bwd(sv["x0"], ln1_g[l:l + 1], sc1 + tie, dh1, dx1)
        dgain = jnp.sum(dgain[:, 0:2, :], axis=0)
        dgain = dgain[:, :HEAD_DIM] + dgain[:, HEAD_DIM:]
        dmod = jnp.concatenate([sums1[0], sums1[1], dg1[0], sums2[0], sums2[1], dg2[0]])
        small[l] = dict(dmod=dmod, ln1=sums1[2], ln2=sums2[2], qg=dgain[0], kg=dgain[1], conv=dconv[0:3])
    return loss_tile, dx, small


def kernel(x, c, ada_w, ada_b, ln1_g, w_in, q_norm_g, k_norm_g, conv_w, w_branch_a, w_branch_b, w_out, ln2_g, w_ffn_gate, w_ffn_up, w_ffn_down, loss_target, m_ada_w, m_ada_b, m_ln1_g, m_w_in, m_q_norm_g, m_k_norm_g, m_conv_w, m_w_branch_a, m_w_branch_b, m_w_out, m_ln2_g, m_w_ffn_gate, m_w_ffn_up, m_w_ffn_down, v_ada_w, v_ada_b, v_ln1_g, v_w_in, v_q_norm_g, v_k_norm_g, v_conv_w, v_w_branch_a, v_w_branch_b, v_w_out, v_ln2_g, v_w_ffn_gate, v_w_ffn_up, v_w_ffn_down):
    n_l, d, a4 = ada_w.shape
    cw4 = conv_w.shape[-1]
    ix, iy, ic = lax.axis_index("x"), lax.axis_index("y"), lax.axis_index("c")
    chip = 2 * ix + iy
    me = 2 * chip + ic

    big_w = dict(win=w_in, wa=w_branch_a, wb=w_branch_b, wo=w_out, wg=w_ffn_gate, wu=w_ffn_up, wd=w_ffn_down)
    big_m = dict(win=m_w_in, wa=m_w_branch_a, wb=m_w_branch_b, wo=m_w_out, wg=m_w_ffn_gate, wu=m_w_ffn_up,
                 wd=m_w_ffn_down)
    big_v = dict(win=v_w_in, wa=v_w_branch_a, wb=v_w_branch_b, wo=v_w_out, wg=v_w_ffn_gate, wu=v_w_ffn_up,
                 wd=v_w_ffn_down)

    def adam_view(a, k):
        return jnp.swapaxes(a, 1, 2) if k in ("wg", "wu") else a

    got = _gather8(_pack([c, conv_w])).reshape(N_DEV, -1)

    weight_groups = [(l, names) for l in range(n_l) for names in (("win",), ("wa", "wb", "wo", "wg", "wu", "wd"))]
    group_srcs = [[big_w[k][l].astype(BF16) for k in names] for l, names in weight_groups]
    started_w = {}

    def start_weights(gi):
        l, names = weight_groups[gi]
        copies = _weight_half_copies if gi == 0 else _weight_copies
        st = _split_start("weights_start_%d" % gi, copies, group_srcs[gi],
                          [(N_CHIP,) + sh.shape for sh in group_srcs[gi]], 3)
        for k in names:
            started_w[(l, k)] = [gi, names, st, None, copies]
        return st[4]

    got, group_srcs[0] = lax.optimization_barrier((got, group_srcs[0]))
    tie = start_weights(0)[0, 0]
    c_all = got[:, :d]
    conv_all = got[:, d:d + n_l * 3 * cw4].reshape(N_CHIP, 2, n_l, 3, cw4)[:, 0]
    conv_full = jnp.transpose(conv_all, (1, 2, 0, 3)).reshape(n_l, 3, N_CHIP * cw4)
    b_cols = lax.dynamic_slice_in_dim(ada_b, chip * a4, a4, axis=1).reshape(n_l, 1, a4)
    b_cols, group_srcs[1:] = lax.optimization_barrier((b_cols + tie, group_srcs[1:]))
    mod_cols = _ada_mod(c_all, ada_w, b_cols)
    mod_all = _gather8(_pack([mod_cols])).reshape(N_DEV, -1)[:, :n_l * N_DEV * a4]
    mod_all = mod_all.reshape(N_CHIP, 2, n_l, N_DEV, a4)[:, 0]
    mods = lax.dynamic_index_in_dim(mod_all, me, axis=2, keepdims=False)
    mods = jnp.transpose(mods, (1, 0, 2)).reshape(n_l, N_CHIP * a4)

    def weights(l, names, after):
        entry, tie = started_w[(l, names[0])], jnp.zeros((), F32)
        if entry[3] is None:
            lands = _split_wait("weights_wait_%d" % entry[0], entry[4], entry[2], after)
            if entry[4] is _weight_half_copies:
                passed = _split_start_in_place("weights_pass_start_%d" % entry[0], _weight_half_pass, lands, 3)
                lands = _split_wait_in_place("weights_pass_wait_%d" % entry[0], _weight_half_pass, passed, passed[3])
            nxt = entry[0] + 1
            if nxt < len(weight_groups):
                lands, group_srcs[nxt] = lax.optimization_barrier((lands, group_srcs[nxt]))
                tie = start_weights(nxt)[0, 0]
            lands = [lax.dynamic_update_index_in_dim(land, own, chip, 0) for land, own in zip(lands, entry[2][2])]
            for k in entry[1]:
                started_w[(l, k)][3] = dict(zip(entry[1], lands))
        return [started_w[(l, k)][3][k] for k in names], tie

    started_g, held_back = [], []

    def start_grads(l, grads, copies=_grad_copies, sems_per=7):
        names = tuple(grads)
        st = _split_start("grads_start_%d" % len(started_g), copies, [grads[k] for k in names],
                          [(N_DEV,) + grads[k].shape[1:] for k in names], sems_per)
        started_g.append((l, names, st, copies))
        return st[4][0, 0]

    def send_grads(l, grads):
        if l == 0 and tuple(grads) == GRAD_GROUPS[-1]:
            held_back.append(grads)
            return jnp.zeros((), F32)
        return start_grads(l, grads)

    loss_tile, grad_x, small = _local_step(
        x[0], loss_target[0], mods, ln1_g, ln2_g, q_norm_g, k_norm_g, conv_full, weights, send_grads)

    sm_shapes = [(n_l, 6 * d), (n_l, d), (n_l, d), (n_l, HEAD_DIM), (n_l, HEAD_DIM), (n_l, 3, d), (1,)]
    vec = _pack([jnp.stack([small[l][k] for l in range(n_l)]) for k in ("dmod", "ln1", "ln2", "qg", "kg", "conv")]
                + [loss_tile[0, 0:1]])
    n_vec = vec.shape[1] * 8
    all_vec = _gather8(vec).reshape(N_DEV, n_vec)
    all_vec, held_back = lax.optimization_barrier((all_vec, held_back))
    tie = sum([start_grads(0, grads, _grad_copies_same_core, 4) for grads in held_back], jnp.zeros((), F32))
    per_dev = [_unpack(all_vec[dev], sm_shapes) for dev in range(N_DEV)]
    dmod_all = jnp.stack([pd[0] for pd in per_dev])
    dmod_cols = jnp.transpose(lax.dynamic_slice_in_dim(dmod_all, chip * a4, a4, axis=2), (1, 0, 2))
    ada_out = _ada_grad_adam(jnp.transpose(c_all) + tie, dmod_cols, ada_w, m_ada_w, v_ada_w)

    dev_parts = jnp.stack([
        _pack([pd[0], pd[1], pd[2], pd[3], pd[4], lax.dynamic_slice_in_dim(pd[5], chip * cw4, cw4, axis=2), pd[6]])
        for pd in per_dev])
    zero1 = jnp.zeros((1,), F32)
    sw = _pack([ada_b, ln1_g, ln2_g, q_norm_g, k_norm_g, conv_w, zero1])
    sm = _pack([m_ada_b, m_ln1_g, m_ln2_g, m_q_norm_g, m_k_norm_g, m_conv_w, zero1])
    sv = _pack([v_ada_b, v_ln1_g, v_ln2_g, v_q_norm_g, v_k_norm_g, v_conv_w, zero1 + 1.0])
    out_shapes = [(n_l, 6 * d), (n_l, d), (n_l, d), (n_l, HEAD_DIM), (n_l, HEAD_DIM), (n_l, 3, cw4), (1,)]
    sm_out = [_unpack(o, out_shapes) for o in _small_adam(dev_parts, sw, sm, sv)]
    loss = 0.5 * sm_out[0][6][0] / d

    after = jnp.full((8, LANES), tie + sm_out[0][0][0, 0] + ada_out[0][0, 0, 0])
    big_out = {}
    for names in GRAD_GROUPS:
        got_parts = {}
        for gi, (l, sent, st, copies) in enumerate(started_g):
            if sent == names:
                parts = _split_wait("grads_wait_%d" % gi, copies, st, after)
                if copies is _grad_copies_same_core:
                    passed = _split_start_in_place("grads_pass_start_%d" % gi, _grad_pass_copies, parts, 3)
                    parts = _split_wait_in_place("grads_pass_wait_%d" % gi, _grad_pass_copies, passed, passed[3])
                for k, part, grad in zip(sent, parts, st[2]):
                    own = lax.dynamic_index_in_dim(grad, chip, 0, keepdims=False)
                    got_parts[(l, k)] = lax.dynamic_update_index_in_dim(part, own, me, 0)
        for k in names:
            res = _sum_adam([got_parts[(l, k)] for l in range(n_l)], adam_view(big_w[k], k), adam_view(big_m[k], k),
                            adam_view(big_v[k], k), "sum_adam_" + k)
            after = res[0]
            big_out[k] = [adam_view(r, k) for r in res]

    outs = [loss, grad_x[None]]
    for kind in range(4):
        sm_k = sm_out[kind]
        outs += [ada_out[kind], sm_k[0], sm_k[1], big_out["win"][kind], sm_k[3], sm_k[4], sm_k[5],
                 big_out["wa"][kind], big_out["wb"][kind], big_out["wo"][kind], sm_k[2],
                 big_out["wg"][kind], big_out["wu"][kind], big_out["wd"][kind]]
    return tuple(outs)
```

```python
import math

import jax
import jax.numpy as jnp
from jax import lax
from jax.experimental import pallas as pl
from jax.experimental.pallas import tpu as pltpu

F32 = jnp.float32
BF16 = jnp.bfloat16
MESH_ID = pl.DeviceIdType.MESH

EPS = 1e-6
HEAD_DIM = 64
Q_BLOCK = 128
Q_SUPER = 1024
Q_SUPER_BWD = 1024
KEY_UNROLL = 4
LANES = 128
N_DEV = 8
N_CHIP = 4
VMEM_LIMIT_BYTES = 56 * 1024 * 1024

ADAM_LR = 0.001
ADAM_B1 = 0.9
ADAM_B2 = 0.999
ADAM_EPS = 1e-08
ADAM_WD = 0.01
ADAM_STEP = 10

HBM_SPEC = pl.BlockSpec(memory_space=pltpu.HBM)
ANY_SPEC = pl.BlockSpec(memory_space=pl.ANY)
SEM_SPEC = pl.BlockSpec(memory_space=pltpu.SEMAPHORE)
VMEM_SPEC = pl.BlockSpec(memory_space=pltpu.VMEM)
SIDE_EFFECT = pltpu.SideEffectType.DATAFLOW_SIDE_EFFECTING


def _params(*sem):
    return pltpu.CompilerParams(dimension_semantics=tuple(sem), vmem_limit_bytes=VMEM_LIMIT_BYTES)


def _tile(n, pref):
    return pref if n % pref == 0 else n


def _dot(a, b):
    return jnp.dot(a, b, preferred_element_type=F32)


def _dot_nt(a, b):
    return lax.dot_general(a, b, (((1,), (1,)), ((), ())), preferred_element_type=F32)


def _dot_tn(a, b):
    return lax.dot_general(a, b, (((0,), (0,)), ((), ())), preferred_element_type=F32)


def _adamw(w, g, m, v):
    m = ADAM_B1 * m + (1.0 - ADAM_B1) * g
    v = ADAM_B2 * v + (1.0 - ADAM_B2) * (g * g)
    m_hat = m / (1.0 - ADAM_B1 ** ADAM_STEP)
    v_hat = v / (1.0 - ADAM_B2 ** ADAM_STEP)
    delta = -ADAM_LR * (m_hat / (jnp.sqrt(v_hat) + ADAM_EPS) + ADAM_WD * w)
    return delta, m, v


def _hbm(a):
    return pltpu.with_memory_space_constraint(a, pltpu.HBM)


def _peer(x, y, c, k):
    return (1 - x if k & 4 else x, 1 - y if k & 2 else y, 1 - c if k & 1 else c)


def _gather8(v):
    rows_per, m = v.shape

    def body(v_ref, out_ref, send_sems, recv_sems, local_sem):
        x, y, c = lax.axis_index("x"), lax.axis_index("y"), lax.axis_index("c")

        def rows(p):
            return out_ref.at[pl.ds((4 * p[0] + 2 * p[1] + p[2]) * rows_per, rows_per), :]

        me = (x, y, c)
        mine = pltpu.make_async_copy(v_ref, rows(me), local_sem)
        mine.start()
        sends = []
        for k in range(1, N_DEV):
            cp = pltpu.make_async_remote_copy(
                src_ref=v_ref, dst_ref=rows(me), send_sem=send_sems.at[k - 1], recv_sem=recv_sems.at[k - 1],
                device_id=_peer(x, y, c, k), device_id_type=MESH_ID)
            cp.start()
            sends.append(cp)
        for k in range(1, N_DEV):
            pltpu.make_async_remote_copy(
                src_ref=v_ref, dst_ref=rows(_peer(x, y, c, k)), send_sem=send_sems.at[k - 1],
                recv_sem=recv_sems.at[k - 1], device_id=_peer(x, y, c, k), device_id_type=MESH_ID).wait_recv()
        for cp in sends:
            cp.wait_send()
        mine.wait()

    return pl.pallas_call(
        body, name="gather8",
        out_shape=jax.ShapeDtypeStruct((N_DEV * rows_per, m), v.dtype),
        in_specs=[VMEM_SPEC], out_specs=VMEM_SPEC,
        scratch_shapes=[pltpu.SemaphoreType.DMA((N_DEV - 1,)), pltpu.SemaphoreType.DMA((N_DEV - 1,)),
                        pltpu.SemaphoreType.DMA],
    )(v)


def _weight_copies(srcs, lands, send_sems, recv_sems):
    x, y, c = lax.axis_index("x"), lax.axis_index("y"), lax.axis_index("c")
    chips = [(1 - x, y), (x, 1 - y), (1 - x, 1 - y)]
    sends, recvs = [], []
    for a, (src, land) in enumerate(zip(srcs, lands)):
        for j, (px, py) in enumerate(chips):
            def copy(dst_block, a=a, j=j, px=px, py=py, src=src, land=land):
                return pltpu.make_async_remote_copy(
                    src_ref=src, dst_ref=land.at[dst_block], send_sem=send_sems.at[3 * a + j],
                    recv_sem=recv_sems.at[3 * a + j], device_id=(px, py, c), device_id_type=MESH_ID)
            sends.append(copy(2 * x + y))
            recvs.append(copy(2 * px + py))
    return sends, recvs


def _weight_half_copies(srcs, lands, send_sems, recv_sems):
    x, y, c = lax.axis_index("x"), lax.axis_index("y"), lax.axis_index("c")
    chips = [(1 - x, y), (x, 1 - y), (1 - x, 1 - y)]
    sends, recvs = [], []
    for a, (src, land) in enumerate(zip(srcs, lands)):
        half = src.shape[0] // 2
        rows = pl.ds(c * half, half)
        for j, (px, py) in enumerate(chips):
            def copy(dst_block, a=a, j=j, px=px, py=py, src=src, land=land, rows=rows):
                return pltpu.make_async_remote_copy(
                    src_ref=src.at[rows], dst_ref=land.at[dst_block, rows], send_sem=send_sems.at[3 * a + j],
                    recv_sem=recv_sems.at[3 * a + j], device_id=(px, py, c), device_id_type=MESH_ID)
            sends.append(copy(2 * x + y))
            recvs.append(copy(2 * px + py))
    return sends, recvs


def _weight_half_pass(lands, same_lands, send_sems, recv_sems):
    del same_lands
    x, y, c = lax.axis_index("x"), lax.axis_index("y"), lax.axis_index("c")
    chips = [(1 - x, y), (x, 1 - y), (1 - x, 1 - y)]
    sends, recvs = [], []
    for a, land in enumerate(lands):
        half = land.shape[1] // 2
        for j, (px, py) in enumerate(chips):
            def copy(pc, a=a, j=j, px=px, py=py, land=land, half=half):
                part = land.at[2 * px + py, pl.ds(pc * half, half)]
                return pltpu.make_async_remote_copy(
                    src_ref=part, dst_ref=part, send_sem=send_sems.at[3 * a + j], recv_sem=recv_sems.at[3 * a + j],
                    device_id=(x, y, 1 - c), device_id_type=MESH_ID)
            sends.append(copy(c))
            recvs.append(copy(1 - c))
    return sends, recvs


def _split_start(name, copies, srcs, land_shapes, sems_per_src):
    n = len(srcs)

    def body(*refs):
        sends, _ = copies(refs[:n], refs[n + 2:2 * n + 2], refs[n], refs[n + 1])
        for cp in sends:
            cp.start()
        token = refs[-1]
        token[...] = jnp.zeros_like(token)

    n_sems = sems_per_src * n
    outs = pl.pallas_call(
        body, name=name,
        out_shape=(pltpu.SemaphoreType.DMA((n_sems,)), pltpu.SemaphoreType.DMA((n_sems,)),
                   *[pltpu.HBM(shape, a.dtype) for a, shape in zip(srcs, land_shapes)],
                   jax.ShapeDtypeStruct((8, LANES), F32)),
        in_specs=[HBM_SPEC] * n, out_specs=(SEM_SPEC, SEM_SPEC, *[HBM_SPEC] * n, VMEM_SPEC),
        compiler_params=pltpu.CompilerParams(has_side_effects=SIDE_EFFECT),
    )(*[_hbm(a) for a in srcs])
    return outs[0], outs[1], list(srcs), list(outs[2:2 + n]), outs[-1]


def _split_wait(name, copies, started, after):
    send_sems, recv_sems, srcs, lands, _ = started
    n = len(srcs)

    def body(*refs):
        sends, recvs = copies(refs[:n], refs[n:2 * n], refs[2 * n], refs[2 * n + 1])
        for cp in sends:
            cp.wait_send()
        for cp in recvs:
            cp.wait_recv()

    return pl.pallas_call(
        body, name=name,
        out_shape=tuple(pltpu.HBM(a.shape, a.dtype) for a in lands),
        in_specs=[HBM_SPEC] * (2 * n) + [SEM_SPEC, SEM_SPEC, ANY_SPEC], out_specs=tuple([HBM_SPEC] * n),
        input_output_aliases={n + i: i for i in range(n)},
        compiler_params=pltpu.CompilerParams(has_side_effects=SIDE_EFFECT),
    )(*srcs, *lands, send_sems, recv_sems, after)


def _split_start_in_place(name, copies, bufs, sems_per_buf):
    n = len(bufs)

    def body(*refs):
        sends, _ = copies(refs[:n], refs[:n], refs[n], refs[n + 1])
        for cp in sends:
            cp.start()
        token = refs[-1]
        token[...] = jnp.zeros_like(token)

    n_sems = sems_per_buf * n
    outs = pl.pallas_call(
        body, name=name,
        out_shape=(pltpu.SemaphoreType.DMA((n_sems,)), pltpu.SemaphoreType.DMA((n_sems,)),
                   *[pltpu.HBM(a.shape, a.dtype) for a in bufs], jax.ShapeDtypeStruct((8, LANES), F32)),
        in_specs=[HBM_SPEC] * n, out_specs=(SEM_SPEC, SEM_SPEC, *[HBM_SPEC] * n, VMEM_SPEC),
        input_output_aliases={i: 2 + i for i in range(n)},
        compiler_params=pltpu.CompilerParams(has_side_effects=SIDE_EFFECT),
    )(*[_hbm(a) for a in bufs])
    return outs[0], outs[1], list(outs[2:2 + n]), outs[-1]


def _split_wait_in_place(name, copies, started, after):
    send_sems, recv_sems, bufs, _ = started
    n = len(bufs)

    def body(*refs):
        sends, recvs = copies(refs[:n], refs[:n], refs[n], refs[n + 1])
        for cp in sends:
            cp.wait_send()
        for cp in recvs:
            cp.wait_recv()

    return pl.pallas_call(
        body, name=name,
        out_shape=tuple(pltpu.HBM(a.shape, a.dtype) for a in bufs),
        in_specs=[HBM_SPEC] * n + [SEM_SPEC, SEM_SPEC, ANY_SPEC], out_specs=tuple([HBM_SPEC] * n),
        input_output_aliases={i: i for i in range(n)},
        compiler_params=pltpu.CompilerParams(has_side_effects=SIDE_EFFECT),
    )(*bufs, send_sems, recv_sems, after)


def _grad_copies(grads, parts, send_sems, recv_sems):
    x, y, c = lax.axis_index("x"), lax.axis_index("y"), lax.axis_index("c")
    chips = [(1 - x, y), (x, 1 - y), (1 - x, 1 - y)]
    my_slot = 4 * x + 2 * y + c
    sends, recvs = [], []
    for a, (grad, part) in enumerate(zip(grads, parts)):
        def copy(k, block, slot, to, a=a, grad=grad, part=part):
            return pltpu.make_async_remote_copy(
                src_ref=grad.at[block], dst_ref=part.at[slot], send_sem=send_sems.at[7 * a + k],
                recv_sem=recv_sems.at[7 * a + k], device_id=to, device_id_type=MESH_ID)
        sends.append(copy(0, 2 * x + y, my_slot, (x, y, 1 - c)))
        recvs.append(copy(0, 2 * x + y, 4 * x + 2 * y + (1 - c), (x, y, 1 - c)))
        for j, (px, py) in enumerate(chips):
            for other, pc in enumerate((c, 1 - c)):
                sends.append(copy(1 + 2 * j + other, 2 * px + py, my_slot, (px, py, pc)))
                recvs.append(copy(1 + 2 * j + other, 2 * x + y, 4 * px + 2 * py + pc, (px, py, pc)))
    return sends, recvs


def _grad_copies_same_core(grads, parts, send_sems, recv_sems):
    x, y, c = lax.axis_index("x"), lax.axis_index("y"), lax.axis_index("c")
    chips = [(1 - x, y), (x, 1 - y), (1 - x, 1 - y)]
    my_slot = 4 * x + 2 * y + c
    sends, recvs = [], []
    for a, (grad, part) in enumerate(zip(grads, parts)):
        def copy(k, block, slot, to, a=a, grad=grad, part=part):
            return pltpu.make_async_remote_copy(
                src_ref=grad.at[block], dst_ref=part.at[slot], send_sem=send_sems.at[4 * a + k],
                recv_sem=recv_sems.at[4 * a + k], device_id=to, device_id_type=MESH_ID)
        sends.append(copy(0, 2 * x + y, my_slot, (x, y, 1 - c)))
        recvs.append(copy(0, 2 * x + y, 4 * x + 2 * y + (1 - c), (x, y, 1 - c)))
        for j, (px, py) in enumerate(chips):
            sends.append(copy(1 + j, 2 * px + py, my_slot, (px, py, c)))
            recvs.append(copy(1 + j, 2 * x + y, 4 * px + 2 * py + c, (px, py, c)))
    return sends, recvs


def _grad_pass_copies(parts, same_parts, send_sems, recv_sems):
    del same_parts
    x, y, c = lax.axis_index("x"), lax.axis_index("y"), lax.axis_index("c")
    chips = [(1 - x, y), (x, 1 - y), (1 - x, 1 - y)]
    sends, recvs = [], []
    for a, part in enumerate(parts):
        for j, (px, py) in enumerate(chips):
            def copy(pc, a=a, j=j, px=px, py=py, part=part):
                slot = part.at[4 * px + 2 * py + pc]
                return pltpu.make_async_remote_copy(
                    src_ref=slot, dst_ref=slot, send_sem=send_sems.at[3 * a + j], recv_sem=recv_sems.at[3 * a + j],
                    device_id=(x, y, 1 - c), device_id_type=MESH_ID)
            sends.append(copy(c))
            recvs.append(copy(1 - c))
    return sends, recvs


def _ada_mod(c_all, ada_w, ada_b_cols):
    n_l, d, a4 = ada_w.shape
    tn = _tile(a4, 512)

    def body(c_ref, w_ref, b_ref, o_ref):
        cv = c_ref[...]
        ca = (cv * jax.nn.sigmoid(cv)).astype(BF16)
        o_ref[...] = _dot(ca, w_ref[...].astype(BF16)) + b_ref[...]

    return pl.pallas_call(
        body, name="ada_mod", grid=(n_l, a4 // tn),
        in_specs=[pl.BlockSpec((N_DEV, d), lambda l, j: (0, 0)),
                  pl.BlockSpec((None, d, tn), lambda l, j: (l, 0, j)),
                  pl.BlockSpec((None, 1, tn), lambda l, j: (l, 0, j))],
        out_specs=pl.BlockSpec((None, N_DEV, tn), lambda l, j: (l, 0, j)),
        out_shape=jax.ShapeDtypeStruct((n_l, N_DEV, a4), F32),
        compiler_params=_params("parallel", "parallel"),
    )(c_all, ada_w, ada_b_cols)


def _ada_grad_adam(c_all_t, dmod_cols, w, m, v):
    n_l, d, a4 = w.shape
    tn = _tile(a4, 512)

    def body(ct_ref, dm_ref, w_ref, m_ref, v_ref, g_ref, dl_ref, nm_ref, nv_ref):
        ct = ct_ref[...]
        ca = ct * jax.nn.sigmoid(ct)
        dm = dm_ref[...]
        g = ca[:, 0:1] * dm[0:1, :]
        for dev in range(1, N_DEV):
            g = g + ca[:, dev:dev + 1] * dm[dev:dev + 1, :]
        g_ref[...] = g
        delta, nm, nv = _adamw(w_ref[...], g, m_ref[...], v_ref[...])
        dl_ref[...] = delta
        nm_ref[...] = nm
        nv_ref[...] = nv

    wspec = pl.BlockSpec((None, d, tn), lambda l, j: (l, 0, j))
    shp = jax.ShapeDtypeStruct(w.shape, F32)
    return pl.pallas_call(
        body, name="ada_grad_adam", grid=(n_l, a4 // tn),
        in_specs=[pl.BlockSpec((d, N_DEV), lambda l, j: (0, 0)),
                  pl.BlockSpec((None, N_DEV, tn), lambda l, j: (l, 0, j)), wspec, wspec, wspec],
        out_specs=[wspec] * 4, out_shape=[shp] * 4,
        compiler_params=_params("parallel", "parallel"),
    )(c_all_t, dmod_cols, w, m, v)


def _lnmod(x, g, sc, sh):
    s, d = x.shape
    tm = _tile(s, 1024)

    def body(x_ref, g_ref, sc_ref, sh_ref, h_ref):
        xv = x_ref[...]
        r = lax.rsqrt(jnp.mean(xv * xv, axis=-1, keepdims=True) + EPS)
        h_ref[...] = ((xv * r * g_ref[...]) * (1.0 + sc_ref[...]) + sh_ref[...]).astype(BF16)

    vec = pl.BlockSpec((1, d), lambda i: (0, 0))
    row = pl.BlockSpec((tm, d), lambda i: (i, 0))
    return pl.pallas_call(
        body, name="lnmod", grid=(s // tm,), in_specs=[row, vec, vec, vec], out_specs=row,
        out_shape=jax.ShapeDtypeStruct((s, d), BF16), compiler_params=_params("parallel"),
    )(x, g, sc, sh)


def _mm_in(h, w_g):
    s, d = h.shape
    n4 = w_g.shape[-1]
    tm = _tile(s, 1024)

    def body(a_ref, b_ref, o_ref):
        o_ref[...] = _dot(a_ref[...], b_ref[...]).astype(BF16)

    return pl.pallas_call(
        body, name="mm_in", grid=(N_CHIP, s // tm),
        in_specs=[pl.BlockSpec((tm, d), lambda j, i: (i, 0)),
                  pl.BlockSpec((None, d, n4), lambda j, i: (j, 0, 0))],
        out_specs=pl.BlockSpec((tm, n4), lambda j, i: (i, j)),
        out_shape=jax.ShapeDtypeStruct((s, N_CHIP * n4), BF16),
        compiler_params=_params("parallel", "parallel"),
    )(h, w_g)


def _pair_mean(x, low):
    lo = jnp.sum(jnp.where(low, x, 0.0), axis=-1, keepdims=True)
    hi = jnp.sum(jnp.where(low, 0.0, x), axis=-1, keepdims=True)
    return jnp.where(low, lo, hi) * (1.0 / HEAD_DIM)


def _pair_norm(x, low):
    r = lax.rsqrt(_pair_mean(x * x, low) + EPS)
    return x * r, r


def _log_not(z):
    nz = -z
    return jnp.minimum(nz, 0.0) - jnp.log(1.0 + jnp.exp(jnp.minimum(z, nz)))


def _attn_consts(inclusive):
    low = lax.broadcasted_iota(jnp.int32, (1, LANES), 1) < HEAD_DIM
    row = lax.broadcasted_iota(jnp.int32, (Q_BLOCK, Q_BLOCK), 0)
    col = lax.broadcasted_iota(jnp.int32, (Q_BLOCK, Q_BLOCK), 1)
    tri = (row <= col) if inclusive else (row > col)
    w2 = jnp.concatenate([tri.astype(BF16), jnp.ones((Q_BLOCK, Q_BLOCK), BF16)], axis=1)
    return low, col < row, jnp.concatenate([w2, w2], axis=0)


def _split_cat(v):
    hi = v.astype(BF16)
    return jnp.concatenate([hi, (v - hi.astype(F32)).astype(BF16)], axis=1)


def _fill_pair_blocks(dst, src_fn, low, n_kb):
    def fill(b, _):
        v = src_fn(pl.ds(pl.multiple_of(b * Q_BLOCK, Q_BLOCK), Q_BLOCK))
        dst[b, 0:Q_BLOCK, :] = jnp.where(low, v, 0.0).astype(BF16)
        dst[b, Q_BLOCK:2 * Q_BLOCK, :] = jnp.where(low, 0.0, v).astype(BF16)
        return 0

    lax.fori_loop(0, n_kb, fill, 0)


def _attn_fwd(p, qg2, kg2, d):
    s = p.shape[0]
    n_pairs = d // LANES
    qsb = _tile(s, Q_SUPER)
    n_sub, n_sb, n_kb = qsb // Q_BLOCK, s // qsb, s // Q_BLOCK
    unroll = math.gcd(KEY_UNROLL, n_sub)
    chunk = _tile(s, 512)
    inv_sqrt = 1.0 / math.sqrt(HEAD_DIM)

    def body(q_ref, k_ref, v_ref, qg_ref, kg_ref, o_ref, lt_ref, qs, k2, v2, run, acc):
        low, causal, w4 = _attn_consts(False)

        def prep(r, _):
            rows = pl.ds(pl.multiple_of(r * chunk, chunk), chunk)
            qs[rows, :] = (_pair_norm(q_ref[rows, :].astype(F32), low)[0] * (qg_ref[...] * inv_sqrt)).astype(BF16)
            return 0

        lax.fori_loop(0, s // chunk, prep, 0)
        _fill_pair_blocks(k2, lambda rows: _pair_norm(k_ref[rows, :].astype(F32), low)[0] * kg_ref[...], low, n_kb)
        _fill_pair_blocks(v2, lambda rows: v_ref[rows, :].astype(F32), low, n_kb)

        def step(sb, j, t0=0, diag_t=None):
            rows = pl.ds(pl.multiple_of(sb * qsb + t0 * Q_BLOCK, Q_BLOCK), (n_sub - t0) * Q_BLOCK)
            z_both = _dot_nt(qs[rows, :], k2[j])
            zls, cats = [], []
            for t in range(t0, n_sub):
                sub = slice((t - t0) * Q_BLOCK, (t - t0 + 1) * Q_BLOCK)
                for h in range(2):
                    z = z_both[sub, h * LANES:(h + 1) * LANES]
                    ln = _log_not(z)
                    if t == diag_t:
                        ln = jnp.where(causal, ln, 0.0)
                    zls.append(z + ln)
                    cats.append(_split_cat(ln))
            c2 = _dot(jnp.concatenate(cats, axis=0), w4)
            a_rows = []
            for t in range(t0, n_sub):
                sub = slice(t * Q_BLOCK, (t + 1) * Q_BLOCK)
                a_pair = []
                for h in range(2):
                    i = 2 * (t - t0) + h
                    tile = slice(i * Q_BLOCK, (i + 1) * Q_BLOCK)
                    later = run[h, sub, :]
                    log_a = zls[i] + c2[tile, :LANES] + later
                    if t == diag_t:
                        log_a = jnp.where(causal, log_a, -1e30)
                    a_pair.append(jnp.exp(log_a).astype(BF16))
                    run[h, sub, :] = later + c2[tile, LANES:]
                a_rows.append(jnp.concatenate(a_pair, axis=1))
            acc[t0 * Q_BLOCK:, :] += _dot(jnp.concatenate(a_rows, axis=0), v2[j])

        def super_block(sb, _):
            run[...] = jnp.zeros_like(run)
            acc[...] = jnp.zeros_like(acc)
            for t in reversed(range(n_sub)):
                step(sb, sb * n_sub + t, t0=t, diag_t=t)

            def below(n, _):
                for u in range(unroll):
                    step(sb, sb * n_sub - 1 - (unroll * n + u))
                return 0

            lax.fori_loop(0, sb * (n_sub // unroll), below, 0)
            rows_sb = pl.ds(pl.multiple_of(sb * qsb, qsb), qsb)
            o_ref[rows_sb, :] = acc[...].astype(BF16)
            lt_ref[rows_sb, :] = jnp.where(low, run[0], run[1])
            return 0

        lax.fori_loop(0, n_sb, super_block, 0)

    def seg(k):
        return pl.BlockSpec((s, LANES), lambda h, k=k: (0, k * n_pairs + h))

    vec = pl.BlockSpec((1, LANES), lambda h: (0, 0))
    out = pl.BlockSpec((s, LANES), lambda h: (0, h))
    return pl.pallas_call(
        body, name="attn_fwd", grid=(n_pairs,),
        in_specs=[seg(0), seg(1), seg(2), vec, vec], out_specs=[out, out],
        out_shape=[jax.ShapeDtypeStruct((s, d), BF16), jax.ShapeDtypeStruct((s, d), F32)],
        scratch_shapes=[pltpu.VMEM((s, LANES), BF16)] + [pltpu.VMEM((n_kb, 2 * Q_BLOCK, LANES), BF16)] * 2
        + [pltpu.VMEM((2, qsb, LANES), F32), pltpu.VMEM((qsb, LANES), F32)],
        compiler_params=_params("parallel"),
    )(p, p, p, qg2, kg2)


def _conv_rows(s):
    return _tile(s, 512)


def _conv_fwd(p, conv_w, d):
    s = p.shape[0]
    nb = d // LANES
    rows_n = _conv_rows(s)

    def body(cb_ref, cc_ref, cx_ref, w_ref, y_ref, us):
        us[pl.ds(0, 8), :] = jnp.zeros((8, LANES), F32)

        def fill(r, _):
            rows = pl.ds(pl.multiple_of(r * rows_n, rows_n), rows_n)
            us[pl.ds(pl.multiple_of(r * rows_n + 8, 8), rows_n), :] = cc_ref[rows, :].astype(F32) * cx_ref[rows, :].astype(F32)
            return 0

        lax.fori_loop(0, s // rows_n, fill, 0)
        w = w_ref[...]

        def out(r, _):
            rows = pl.ds(pl.multiple_of(r * rows_n, rows_n), rows_n)
            ext = us[pl.ds(pl.multiple_of(r * rows_n, 8), rows_n + 8), :]
            cv = (w[0:1, :] * pltpu.roll(ext, 2, 0)[8:, :] + w[1:2, :] * pltpu.roll(ext, 1, 0)[8:, :]
                  + w[2:3, :] * ext[8:, :])
            y_ref[rows, :] = (cb_ref[rows, :].astype(F32) * cv).astype(BF16)
            return 0

        lax.fori_loop(0, s // rows_n, out, 0)

    def seg(k):
        return pl.BlockSpec((s, LANES), lambda b, k=k: (0, k * nb + b))

    return pl.pallas_call(
        body, name="conv_fwd", grid=(nb,),
        in_specs=[seg(3), seg(4), seg(5), pl.BlockSpec((3, LANES), lambda b: (0, b))],
        out_specs=pl.BlockSpec((s, LANES), lambda b: (0, b)),
        out_shape=jax.ShapeDtypeStruct((s, d), BF16),
        scratch_shapes=[pltpu.VMEM((s + 8, LANES), F32)],
        compiler_params=_params("parallel"),
    )(p, p, p, conv_w)


def _branch(ya, yb, p, wa, wb, d):
    s = ya.shape[0]
    tm = _tile(s, 512)

    def body(ya_ref, yb_ref, ga_ref, gb_ref, wa_ref, wb_ref, m_ref, a_ref, b_ref):
        pa = _dot(ya_ref[...], wa_ref[...])
        pb = _dot(yb_ref[...], wb_ref[...])
        ga, gb = ga_ref[...].astype(F32), gb_ref[...].astype(F32)
        m_ref[...] = (jax.nn.sigmoid(ga) * pa + jax.nn.sigmoid(gb) * pb).astype(BF16)
        a_ref[...] = pa.astype(BF16)
        b_ref[...] = pb.astype(BF16)

    row = pl.BlockSpec((tm, d), lambda i: (i, 0))
    wsp = pl.BlockSpec((d, d), lambda i: (0, 0))
    shp = jax.ShapeDtypeStruct((s, d), BF16)
    return pl.pallas_call(
        body, name="branch", grid=(s // tm,),
        in_specs=[row, row, pl.BlockSpec((tm, d), lambda i: (i, 6)), pl.BlockSpec((tm, d), lambda i: (i, 7)), wsp, wsp],
        out_specs=[row, row, row], out_shape=[shp, shp, shp], compiler_params=_params("parallel"),
    )(ya, yb, p, p, wa, wb)


def _out_proj(merged, wout, x0, g1):
    s, d = x0.shape
    tm = _tile(s, 1024)

    def body(m_ref, w_ref, x_ref, g_ref, x1_ref, mo_ref):
        mo = _dot(m_ref[...], w_ref[...])
        mo_ref[...] = mo
        x1_ref[...] = x_ref[...] + g_ref[...] * mo

    row = pl.BlockSpec((tm, d), lambda i: (i, 0))
    shp = jax.ShapeDtypeStruct((s, d), F32)
    return pl.pallas_call(
        body, name="out_proj", grid=(s // tm,),
        in_specs=[row, pl.BlockSpec((d, d), lambda i: (0, 0)), row, pl.BlockSpec((1, d), lambda i: (0, 0))],
        out_specs=[row, row], out_shape=[shp, shp], compiler_params=_params("parallel"),
    )(merged, wout, x0, g1)


def _ffn_up(h, wg_g, wu_g):
    s, d = h.shape
    f4 = wg_g.shape[-1]
    tm = _tile(s, 1024)

    def body(h_ref, wg_ref, wu_ref, gate_ref, up_ref, act_ref):
        hv = h_ref[...]
        gt = _dot(hv, wg_ref[...])
        up = _dot(hv, wu_ref[...])
        gate_ref[...] = gt.astype(BF16)
        up_ref[...] = up.astype(BF16)
        act_ref[...] = (gt * jax.nn.sigmoid(gt) * up).astype(BF16)

    wsp = pl.BlockSpec((None, d, f4), lambda j, i: (j, 0, 0))
    osp = pl.BlockSpec((None, tm, f4), lambda j, i: (j, i, 0))
    shp = jax.ShapeDtypeStruct((N_CHIP, s, f4), BF16)
    return pl.pallas_call(
        body, name="ffn_up", grid=(N_CHIP, s // tm),
        in_specs=[pl.BlockSpec((tm, d), lambda j, i: (i, 0)), wsp, wsp],
        out_specs=[osp, osp, osp], out_shape=[shp, shp, shp], compiler_params=_params("parallel", "parallel"),
    )(h, wg_g, wu_g)


def _ffn_down(act, wd_g, x1, g2):
    s, d = x1.shape
    f4 = act.shape[-1]
    tm = _tile(s, 1024)

    def body(a_ref, w_ref, x_ref, g_ref, x2_ref, f_ref, acc):
        j = pl.program_id(1)

        @pl.when(j == 0)
        def _():
            acc[...] = jnp.zeros_like(acc)

        acc[...] += _dot(a_ref[...], w_ref[...])

        @pl.when(j == N_CHIP - 1)
        def _():
            f = acc[...]
            f_ref[...] = f
            x2_ref[...] = x_ref[...] + g_ref[...] * f

    row = pl.BlockSpec((tm, d), lambda i, j: (i, 0))
    shp = jax.ShapeDtypeStruct((s, d), F32)
    return pl.pallas_call(
        body, name="ffn_down", grid=(s // tm, N_CHIP),
        in_specs=[pl.BlockSpec((None, tm, f4), lambda i, j: (j, i, 0)),
                  pl.BlockSpec((None, f4, d), lambda i, j: (j, 0, 0)),
                  row, pl.BlockSpec((1, d), lambda i, j: (0, 0))],
        out_specs=[row, row], out_shape=[shp, shp],
        scratch_shapes=[pltpu.VMEM((tm, d), F32)], compiler_params=_params("parallel", "arbitrary"),
    )(act, wd_g, x1, g2)


def _loss_head(y, target):
    s, d = y.shape
    tm = _tile(s, 1024)
    n_steps = s // tm

    def body(y_ref, t_ref, dy_ref, l_ref, acc):
        i = pl.program_id(0)

        @pl.when(i == 0)
        def _():
            acc[...] = jnp.zeros_like(acc)

        err = y_ref[...] - t_ref[...]
        dy_ref[...] = err / d
        acc[...] += jnp.sum(err * err, axis=0, keepdims=True)

        @pl.when(i == n_steps - 1)
        def _():
            l_ref[...] = jnp.broadcast_to(jnp.sum(acc[...], axis=1, keepdims=True), (8, LANES))

    row = pl.BlockSpec((tm, d), lambda i: (i, 0))
    return pl.pallas_call(
        body, name="loss_head", grid=(n_steps,), in_specs=[row, row],
        out_specs=[row, pl.BlockSpec((8, LANES), lambda i: (0, 0))],
        out_shape=[jax.ShapeDtypeStruct((s, d), F32), jax.ShapeDtypeStruct((8, LANES), F32)],
        scratch_shapes=[pltpu.VMEM((1, d), F32)], compiler_params=_params("arbitrary"),
    )(y, target)


def _mm_tn(a, b, a_spec, b_spec, out_rc, name):
    r, c = out_rc
    s = a.shape[-2]
    tk = _tile(s, 1024)
    nk = s // tk

    def body(a_ref, b_ref, o_ref, acc):
        k = pl.program_id(1)

        @pl.when(k == 0)
        def _():
            acc[...] = jnp.zeros_like(acc)

        acc[...] += _dot_tn(a_ref[...], b_ref[...])

        @pl.when(k == nk - 1)
        def _():
            o_ref[...] = acc[...].astype(BF16)

    return pl.pallas_call(
        body, name=name, grid=(N_CHIP, nk),
        in_specs=[pl.BlockSpec(*a_spec(tk)), pl.BlockSpec(*b_spec(tk))],
        out_specs=pl.BlockSpec((None, r, c), lambda j, k: (j, 0, 0)),
        out_shape=jax.ShapeDtypeStruct((N_CHIP, r, c), BF16),
        scratch_shapes=[pltpu.VMEM((r, c), F32)], compiler_params=_params("parallel", "arbitrary"),
    )(a, b)


def _mm_tn_square(a, b, name):
    s, d = a.shape
    r4 = d // N_CHIP
    tk = _tile(s, 1024)
    nk = s // tk

    def body(a_ref, b_ref, o_ref, acc):
        k = pl.program_id(0)

        @pl.when(k == 0)
        def _():
            acc[...] = jnp.zeros_like(acc)

        acc[...] += _dot_tn(a_ref[...], b_ref[...])

        @pl.when(k == nk - 1)
        def _():
            for j in range(N_CHIP):
                o_ref[j] = acc[j * r4:(j + 1) * r4, :].astype(BF16)

    blk = pl.BlockSpec((tk, d), lambda k: (k, 0))
    return pl.pallas_call(
        body, name=name, grid=(nk,), in_specs=[blk, blk],
        out_specs=pl.BlockSpec((N_CHIP, r4, d), lambda k: (0, 0, 0)),
        out_shape=jax.ShapeDtypeStruct((N_CHIP, r4, d), BF16),
        scratch_shapes=[pltpu.VMEM((d, d), F32)], compiler_params=_params("arbitrary"),
    )(a, b)


def _ffn_bwd1(dx2, f, g2, wd_g, gate, up):
    s, d = dx2.shape
    f4 = gate.shape[-1]
    tm = _tile(s, 1024)

    def body(dx_ref, f_ref, g_ref, w_ref, gate_ref, up_ref, dgate_ref, dup_ref, df_ref, dg_ref):
        i, j = pl.program_id(0), pl.program_id(1)

        @pl.when((i == 0) & (j == 0))
        def _():
            dg_ref[...] = jnp.zeros_like(dg_ref)

        dxv = dx_ref[...]
        df = (g_ref[...] * dxv).astype(BF16)

        @pl.when(j == 0)
        def _():
            df_ref[...] = df
            dg_ref[0:1, :] += jnp.sum(dxv * f_ref[...], axis=0, keepdims=True)

        da = _dot_nt(df, w_ref[...])
        gt = gate_ref[...].astype(F32)
        sg = jax.nn.sigmoid(gt)
        dup_ref[...] = (da * gt * sg).astype(BF16)
        dgate_ref[...] = (da * up_ref[...].astype(F32) * (sg * (1.0 + gt * (1.0 - sg)))).astype(BF16)

    row = pl.BlockSpec((tm, d), lambda i, j: (i, 0))
    hsp = pl.BlockSpec((None, tm, f4), lambda i, j: (j, i, 0))
    hshp = jax.ShapeDtypeStruct((N_CHIP, s, f4), BF16)
    return pl.pallas_call(
        body, name="ffn_bwd1", grid=(s // tm, N_CHIP),
        in_specs=[row, row, pl.BlockSpec((1, d), lambda i, j: (0, 0)),
                  pl.BlockSpec((None, f4, d), lambda i, j: (j, 0, 0)), hsp, hsp],
        out_specs=[hsp, hsp, row, pl.BlockSpec((8, d), lambda i, j: (0, 0))],
        out_shape=[hshp, hshp, jax.ShapeDtypeStruct((s, d), BF16), jax.ShapeDtypeStruct((8, d), F32)],
        compiler_params=_params("arbitrary", "arbitrary"),
    )(dx2, f, g2, wd_g, gate, up)


def _ffn_bwd2(dgate, dup, wg_g, wu_g):
    _, s, f4 = dgate.shape
    d = wg_g.shape[-2]
    tm = _tile(s, 1024)

    def body(dg_ref, du_ref, wg_ref, wu_ref, o_ref, acc):
        j = pl.program_id(1)

        @pl.when(j == 0)
        def _():
            acc[...] = jnp.zeros_like(acc)

        acc[...] += _dot_nt(dg_ref[...], wg_ref[...]) + _dot_nt(du_ref[...], wu_ref[...])

        @pl.when(j == N_CHIP - 1)
        def _():
            o_ref[...] = acc[...]

    hsp = pl.BlockSpec((None, tm, f4), lambda i, j: (j, i, 0))
    wsp = pl.BlockSpec((None, d, f4), lambda i, j: (j, 0, 0))
    return pl.pallas_call(
        body, name="ffn_bwd2", grid=(s // tm, N_CHIP), in_specs=[hsp, hsp, wsp, wsp],
        out_specs=pl.BlockSpec((tm, d), lambda i, j: (i, 0)), out_shape=jax.ShapeDtypeStruct((s, d), F32),
        scratch_shapes=[pltpu.VMEM((tm, d), F32)], compiler_params=_params("parallel", "arbitrary"),
    )(dgate, dup, wg_g, wu_g)


def _lnmod_bwd(x, g, sc, dh, dres):
    s, d = x.shape
    tm = _tile(s, 1024)

    def body(x_ref, g_ref, sc_ref, dh_ref, dr_ref, dx_ref, sums_ref):
        @pl.when(pl.program_id(0) == 0)
        def _():
            sums_ref[...] = jnp.zeros_like(sums_ref)

        xv, dhv, gv = x_ref[...], dh_ref[...], g_ref[...]
        r = lax.rsqrt(jnp.mean(xv * xv, axis=-1, keepdims=True) + EPS)
        n = xv * r
        one_sc = 1.0 + sc_ref[...]
        dt = dhv * one_sc
        sums_ref[0:1, :] += jnp.sum(dhv, axis=0, keepdims=True)
        sums_ref[1:2, :] += jnp.sum(dhv * (n * gv), axis=0, keepdims=True)
        sums_ref[2:3, :] += jnp.sum(dt * n, axis=0, keepdims=True)
        dn = dt * gv
        dx_ref[...] = dr_ref[...] + r * (dn - n * jnp.mean(dn * n, axis=-1, keepdims=True))

    vec = pl.BlockSpec((1, d), lambda i: (0, 0))
    row = pl.BlockSpec((tm, d), lambda i: (i, 0))
    return pl.pallas_call(
        body, name="lnmod_bwd", grid=(s // tm,), in_specs=[row, vec, vec, row, row],
        out_specs=[row, pl.BlockSpec((8, d), lambda i: (0, 0))],
        out_shape=[jax.ShapeDtypeStruct((s, d), F32), jax.ShapeDtypeStruct((8, d), F32)],
        compiler_params=_params("arbitrary"),
    )(x, g, sc, dh, dres)


def _out_bwd(dx1, mo, g1, wout, pa, pb, p, wa, wb, d):
    s = dx1.shape[0]
    tm = _tile(s, 256)

    def body(dx_ref, mo_ref, g_ref, wo_ref, pa_ref, pb_ref, ga_ref, gb_ref, wa_ref, wb_ref,
             dmo_ref, da_ref, db_ref, dya_ref, dyb_ref, dp_ref, dg_ref):
        @pl.when(pl.program_id(0) == 0)
        def _():
            dg_ref[...] = jnp.zeros_like(dg_ref)

        dxv = dx_ref[...]
        dg_ref[0:1, :] += jnp.sum(dxv * mo_ref[...], axis=0, keepdims=True)
        dmo = (g_ref[...] * dxv).astype(BF16)
        dmo_ref[...] = dmo
        dm = _dot_nt(dmo, wo_ref[...])
        sa, sb = jax.nn.sigmoid(ga_ref[...].astype(F32)), jax.nn.sigmoid(gb_ref[...].astype(F32))
        da = (dm * sa).astype(BF16)
        db = (dm * sb).astype(BF16)
        da_ref[...] = da
        db_ref[...] = db
        dp_ref[:, :d] = (dm * pa_ref[...].astype(F32) * (sa * (1.0 - sa))).astype(BF16)
        dp_ref[:, d:] = (dm * pb_ref[...].astype(F32) * (sb * (1.0 - sb))).astype(BF16)
        dya_ref[...] = _dot_nt(da, wa_ref[...]).astype(BF16)
        dyb_ref[...] = _dot_nt(db, wb_ref[...]).astype(BF16)

    row = pl.BlockSpec((tm, d), lambda i: (i, 0))
    wsp = pl.BlockSpec((d, d), lambda i: (0, 0))
    shp = jax.ShapeDtypeStruct((s, d), BF16)
    return pl.pallas_call(
        body, name="out_bwd", grid=(s // tm,),
        in_specs=[row, row, pl.BlockSpec((1, d), lambda i: (0, 0)), wsp, row, row,
                  pl.BlockSpec((tm, d), lambda i: (i, 6)), pl.BlockSpec((tm, d), lambda i: (i, 7)), wsp, wsp],
        out_specs=[row] * 5 + [pl.BlockSpec((tm, 2 * d), lambda i: (i, 3)), pl.BlockSpec((8, d), lambda i: (0, 0))],
        out_shape=[shp] * 5 + [jax.ShapeDtypeStruct((s, 8 * d), BF16), jax.ShapeDtypeStruct((8, d), F32)],
        compiler_params=_params("arbitrary"),
    )(dx1, mo, g1, wout, pa, pb, p, p, wa, wb)


def _store_segments(outs, dp_out, sems, col_blocks):
    copies = [pltpu.make_async_copy(outs.at[k], dp_out.at[:, pl.ds(pl.multiple_of(cb * LANES, LANES), LANES)],
                                    sems.at[k]) for k, cb in enumerate(col_blocks)]
    for cp in copies:
        cp.start()
    for cp in copies:
        cp.wait()


def _conv_bwd(p, conv_w, dyb, dp, d):
    s = p.shape[0]
    nb = d // LANES
    rows_n = _conv_rows(s)

    def compute(cb_ref, cc_ref, cx_ref, w_ref, dy_ref, dcb_ref, dcc_ref, dcx_ref, dw_ref, us, ds):
        us[pl.ds(0, 8), :] = jnp.zeros((8, LANES), F32)
        ds[pl.ds(s, 8), :] = jnp.zeros((8, LANES), F32)

        def fill(r, _):
            rows = pl.ds(pl.multiple_of(r * rows_n, rows_n), rows_n)
            us[pl.ds(pl.multiple_of(r * rows_n + 8, 8), rows_n), :] = cc_ref[rows, :].astype(F32) * cx_ref[rows, :].astype(F32)
            ds[rows, :] = dy_ref[rows, :].astype(F32) * cb_ref[rows, :].astype(F32)
            return 0

        lax.fori_loop(0, s // rows_n, fill, 0)
        w = w_ref[...]

        def out(r, carry):
            dw0, dw1, dw2 = carry
            rows = pl.ds(pl.multiple_of(r * rows_n, rows_n), rows_n)
            ext = us[pl.ds(pl.multiple_of(r * rows_n, 8), rows_n + 8), :]
            u0, u1, u2 = ext[8:, :], pltpu.roll(ext, 1, 0)[8:, :], pltpu.roll(ext, 2, 0)[8:, :]
            cv = w[0:1, :] * u2 + w[1:2, :] * u1 + w[2:3, :] * u0
            dcb_ref[rows, :] = (dy_ref[rows, :].astype(F32) * cv).astype(BF16)
            nxt = ds[pl.ds(pl.multiple_of(r * rows_n, 8), rows_n + 8), :]
            e0 = nxt[:rows_n, :]
            e1 = pltpu.roll(nxt, rows_n + 7, 0)[:rows_n, :]
            e2 = pltpu.roll(nxt, rows_n + 6, 0)[:rows_n, :]
            du = w[2:3, :] * e0 + w[1:2, :] * e1 + w[0:1, :] * e2
            dcc_ref[rows, :] = (du * cx_ref[rows, :].astype(F32)).astype(BF16)
            dcx_ref[rows, :] = (du * cc_ref[rows, :].astype(F32)).astype(BF16)
            return (dw0 + jnp.sum(e0 * u2, axis=0, keepdims=True), dw1 + jnp.sum(e0 * u1, axis=0, keepdims=True),
                    dw2 + jnp.sum(e0 * u0, axis=0, keepdims=True))

        zero = jnp.zeros((1, LANES), F32)
        dw0, dw1, dw2 = lax.fori_loop(0, s // rows_n, out, (zero, zero, zero))
        dw_ref[...] = jnp.zeros_like(dw_ref)
        dw_ref[0:1, :] = dw0
        dw_ref[1:2, :] = dw1
        dw_ref[2:3, :] = dw2

    def body(cb_ref, cc_ref, cx_ref, w_ref, dy_ref, dp_in, dp_out, dw_ref, us, ds, outs, sems):
        del dp_in
        compute(cb_ref, cc_ref, cx_ref, w_ref, dy_ref, outs.at[0], outs.at[1], outs.at[2], dw_ref, us, ds)
        _store_segments(outs, dp_out, sems, [(3 + k) * nb + pl.program_id(0) for k in range(3)])

    def seg(k):
        return pl.BlockSpec((s, LANES), lambda b, k=k: (0, k * nb + b))

    return pl.pallas_call(
        body, name="conv_bwd", grid=(nb,),
        in_specs=[seg(3), seg(4), seg(5), pl.BlockSpec((3, LANES), lambda b: (0, b)),
                  pl.BlockSpec((s, LANES), lambda b: (0, b)), ANY_SPEC],
        out_specs=[ANY_SPEC, pl.BlockSpec((8, LANES), lambda b: (0, b))],
        out_shape=[jax.ShapeDtypeStruct(dp.shape, BF16), jax.ShapeDtypeStruct((8, d), F32)],
        input_output_aliases={5: 0},
        scratch_shapes=[pltpu.VMEM((s + 8, LANES), F32), pltpu.VMEM((s + 8, LANES), F32),
                        pltpu.VMEM((3, s, LANES), BF16), pltpu.SemaphoreType.DMA((3,))],
        compiler_params=_params("arbitrary"),
    )(p, p, p, conv_w, dyb, dp)


def _attn_bwd(p, qg2, kg2, dy, lt, dp, d):
    s = p.shape[0]
    n_pairs = d // LANES
    qsb = _tile(s, Q_SUPER_BWD)
    n_sub, n_sb, n_kb = qsb // Q_BLOCK, s // qsb, s // Q_BLOCK
    unroll = math.gcd(KEY_UNROLL, n_sub)
    chunk = _tile(s, 512)
    inv_sqrt = 1.0 / math.sqrt(HEAD_DIM)

    def compute(q_ref, k_ref, v_ref, qg_ref, kg_ref, dy_ref, lt_ref, dq_ref, dk_ref, dv_ref, dgain_ref,
                qs, k2, v2, dkt, dvt, qt, dyt, rem, gbef, dqa):
        low, causal, w4 = _attn_consts(True)

        def prep(r, _):
            rows = pl.ds(pl.multiple_of(r * chunk, chunk), chunk)
            qs[rows, :] = (_pair_norm(q_ref[rows, :].astype(F32), low)[0] * (qg_ref[...] * inv_sqrt)).astype(BF16)
            return 0

        lax.fori_loop(0, s // chunk, prep, 0)
        _fill_pair_blocks(k2, lambda rows: _pair_norm(k_ref[rows, :].astype(F32), low)[0] * kg_ref[...], low, n_kb)
        _fill_pair_blocks(v2, lambda rows: v_ref[rows, :].astype(F32), low, n_kb)

        def clear(b, _):
            dkt[b] = jnp.zeros((LANES, Q_BLOCK), F32)
            dvt[b] = jnp.zeros((LANES, Q_BLOCK), F32)
            return 0

        lax.fori_loop(0, n_kb, clear, 0)

        def step(sb, j, t0=0, diag_t=None):
            rows = pl.ds(pl.multiple_of(sb * qsb + t0 * Q_BLOCK, Q_BLOCK), (n_sub - t0) * Q_BLOCK)
            kj2, vj2 = k2[j], v2[j]
            z_both = _dot_nt(qs[rows, :], kj2)
            da_both = _dot_nt(dy_ref[rows, :], vj2)
            zls, cats = [], []
            for t in range(t0, n_sub):
                sub = slice((t - t0) * Q_BLOCK, (t - t0 + 1) * Q_BLOCK)
                for h in range(2):
                    z = z_both[sub, h * LANES:(h + 1) * LANES]
                    ln = _log_not(z)
                    if t == diag_t:
                        ln = jnp.where(causal, ln, 0.0)
                    zls.append(z + ln)
                    cats.append(_split_cat(ln))
            c2 = _dot(jnp.concatenate(cats, axis=0), w4)
            a_rows, gs, cats = [], [], []
            for t in range(t0, n_sub):
                sub = slice(t * Q_BLOCK, (t + 1) * Q_BLOCK)
                a_pair = []
                for h in range(2):
                    i = 2 * (t - t0) + h
                    tile = slice(i * Q_BLOCK, (i + 1) * Q_BLOCK)
                    left = rem[h, sub, :]
                    log_a = zls[i] + (left - c2[tile, :LANES])
                    if t == diag_t:
                        log_a = jnp.where(causal, log_a, -1e30)
                    a = jnp.exp(log_a)
                    rem[h, sub, :] = left - c2[tile, LANES:]
                    g = a * da_both[(t - t0) * Q_BLOCK:(t - t0 + 1) * Q_BLOCK, h * LANES:(h + 1) * LANES]
                    a_pair.append(a.astype(BF16))
                    gs.append(g)
                    cats.append(_split_cat(g))
                a_rows.append(jnp.concatenate(a_pair, axis=1))
            c2g = _dot(jnp.concatenate(cats, axis=0), w4)
            dz_rows = []
            for t in range(t0, n_sub):
                sub = slice(t * Q_BLOCK, (t + 1) * Q_BLOCK)
                dz_pair = []
                for h in range(2):
                    i = 2 * (t - t0) + h
                    tile = slice(i * Q_BLOCK, (i + 1) * Q_BLOCK)
                    before = gbef[h, sub, :]
                    dz = gs[i] - jnp.exp(zls[i]) * (before + c2g[tile, :LANES])
                    if t == diag_t:
                        dz = jnp.where(causal, dz, 0.0)
                    gbef[h, sub, :] = before + c2g[tile, LANES:]
                    dz_pair.append(dz.astype(BF16))
                dz_rows.append(jnp.concatenate(dz_pair, axis=1))
            a_both = jnp.concatenate(a_rows, axis=0)
            dz_both = jnp.concatenate(dz_rows, axis=0)
            used = slice(t0 * Q_BLOCK, qsb)
            dvt[j] += _dot(dyt[0, :, used], a_both[:, :LANES]) + _dot(dyt[1, :, used], a_both[:, LANES:])
            dkt[j] += _dot(qt[0, :, used], dz_both[:, :LANES]) + _dot(qt[1, :, used], dz_both[:, LANES:])
            dqa[used, :] += _dot(dz_both, kj2)

        def super_block(sb, dqg):
            rows_sb = pl.ds(pl.multiple_of(sb * qsb, qsb), qsb)
            total = lt_ref[rows_sb, :]
            other = pltpu.roll(total, HEAD_DIM, 1)
            rem[0] = jnp.where(low, total, other)
            rem[1] = jnp.where(low, other, total)
            gbef[...] = jnp.zeros_like(gbef)
            dqa[...] = jnp.zeros_like(dqa)
            qv = qs[rows_sb, :].astype(F32)
            dyv = dy_ref[rows_sb, :].astype(F32)
            qt[0] = jnp.where(low, qv, 0.0).T.astype(BF16)
            qt[1] = jnp.where(low, 0.0, qv).T.astype(BF16)
            dyt[0] = jnp.where(low, dyv, 0.0).T.astype(BF16)
            dyt[1] = jnp.where(low, 0.0, dyv).T.astype(BF16)

            def below(n, _):
                for u in range(unroll):
                    step(sb, unroll * n + u)
                return 0

            lax.fori_loop(0, sb * (n_sub // unroll), below, 0)
            for t in range(n_sub):
                step(sb, sb * n_sub + t, t0=t, diag_t=t)
            qhat, r = _pair_norm(q_ref[rows_sb, :].astype(F32), low)
            dqn = dqa[...]
            dqhat = dqn * (qg_ref[...] * inv_sqrt)
            dq_ref[rows_sb, :] = (r * (dqhat - qhat * _pair_mean(dqhat * qhat, low))).astype(BF16)
            return dqg + jnp.sum(dqn * qhat, axis=0, keepdims=True) * inv_sqrt

        dqg = lax.fori_loop(0, n_sb, super_block, jnp.zeros((1, LANES), F32))

        def finish(b, dkg):
            rows = pl.ds(pl.multiple_of(b * Q_BLOCK, Q_BLOCK), Q_BLOCK)
            khat, rk = _pair_norm(k_ref[rows, :].astype(F32), low)
            dkn = dkt[b].T
            dkhat = dkn * kg_ref[...]
            dk_ref[rows, :] = (rk * (dkhat - khat * _pair_mean(dkhat * khat, low))).astype(BF16)
            dv_ref[rows, :] = dvt[b].T.astype(BF16)
            return dkg + jnp.sum(dkn * khat, axis=0, keepdims=True)

        dkg = lax.fori_loop(0, n_kb, finish, jnp.zeros((1, LANES), F32))
        dgain_ref[...] = jnp.zeros_like(dgain_ref)
        dgain_ref[0:1, :] = dqg
        dgain_ref[1:2, :] = dkg

    def body(q_ref, k_ref, v_ref, qg_ref, kg_ref, dy_ref, lt_ref, dp_in, dp_out, dgain_ref, outs, sems, *scratch):
        del dp_in
        compute(q_ref, k_ref, v_ref, qg_ref, kg_ref, dy_ref, lt_ref, outs.at[0], outs.at[1], outs.at[2], dgain_ref,
                *scratch)
        _store_segments(outs, dp_out, sems, [k * n_pairs + pl.program_id(0) for k in range(3)])

    def seg(k):
        return pl.BlockSpec((s, LANES), lambda h, k=k: (0, k * n_pairs + h))

    vec = pl.BlockSpec((1, LANES), lambda h: (0, 0))
    col = pl.BlockSpec((s, LANES), lambda h: (0, h))
    return pl.pallas_call(
        body, name="attn_bwd", grid=(n_pairs,),
        in_specs=[seg(0), seg(1), seg(2), vec, vec, col, col, ANY_SPEC],
        out_specs=[ANY_SPEC, pl.BlockSpec((None, 8, LANES), lambda h: (h, 0, 0))],
        out_shape=[jax.ShapeDtypeStruct(dp.shape, BF16), jax.ShapeDtypeStruct((n_pairs, 8, LANES), F32)],
        input_output_aliases={7: 0},
        scratch_shapes=[pltpu.VMEM((3, s, LANES), BF16), pltpu.SemaphoreType.DMA((3,)), pltpu.VMEM((s, LANES), BF16)]
        + [pltpu.VMEM((n_kb, 2 * Q_BLOCK, LANES), BF16)] * 2
        + [pltpu.VMEM((n_kb, LANES, Q_BLOCK), F32)] * 2
        + [pltpu.VMEM((2, LANES, qsb), BF16)] * 2
        + [pltpu.VMEM((2, qsb, LANES), F32)] * 2 + [pltpu.VMEM((qsb, LANES), F32)],
        compiler_params=_params("arbitrary"),
    )(p, p, p, qg2, kg2, dy, lt, dp)


def _mm_in_bwd(dp, w_g):
    s = dp.shape[0]
    d, n4 = w_g.shape[-2:]
    tm = _tile(s, 1024)

    def body(a_ref, w_ref, o_ref, acc):
        j = pl.program_id(1)

        @pl.when(j == 0)
        def _():
            acc[...] = jnp.zeros_like(acc)

        acc[...] += _dot_nt(a_ref[...], w_ref[...])

        @pl.when(j == N_CHIP - 1)
        def _():
            o_ref[...] = acc[...]

    return pl.pallas_call(
        body, name="mm_in_bwd", grid=(s // tm, N_CHIP),
        in_specs=[pl.BlockSpec((tm, n4), lambda i, j: (i, j)),
                  pl.BlockSpec((None, d, n4), lambda i, j: (j, 0, 0))],
        out_specs=pl.BlockSpec((tm, d), lambda i, j: (i, 0)), out_shape=jax.ShapeDtypeStruct((s, d), F32),
        scratch_shapes=[pltpu.VMEM((tm, d), F32)], compiler_params=_params("parallel", "arbitrary"),
    )(dp, w_g)


def _sum_adam(parts, w, m, v, name):
    n_l, r, c = w.shape
    tr = next((t for t in (256, 176, 128, 64, 32, 16) if r % t == 0 and t * c <= 256 * 1024), r)
    n_blk = r // tr

    def body(*refs):
        p_refs = refs[:n_l]
        w_ref, m_ref, v_ref, g_ref, dl_ref, nm_ref, nv_ref = refs[n_l:]
        for l in range(n_l):
            @pl.when(pl.program_id(0) == l)
            def _(p_ref=p_refs[l]):
                g = p_ref[0].astype(F32)
                for dev in range(1, N_DEV):
                    g = g + p_ref[dev].astype(F32)
                g_ref[...] = g
                delta, nm, nv = _adamw(w_ref[...], g, m_ref[...], v_ref[...])
                dl_ref[...] = delta
                nm_ref[...] = nm
                nv_ref[...] = nv

    def part_spec(l):
        return pl.BlockSpec((N_DEV, tr, c), lambda ll, i, l=l: (0, jnp.where(ll == l, i, jnp.where(ll < l, 0, n_blk - 1)), 0))

    wsp = pl.BlockSpec((None, tr, c), lambda l, i: (l, i, 0))
    shp = jax.ShapeDtypeStruct(w.shape, F32)
    return pl.pallas_call(
        body, name=name, grid=(n_l, n_blk),
        in_specs=[part_spec(l) for l in range(n_l)] + [wsp, wsp, wsp],
        out_specs=[wsp] * 4, out_shape=[shp] * 4, compiler_params=_params("arbitrary", "arbitrary"),
    )(*parts, w, m, v)


def _small_adam(parts, w, m, v):
    def body(p_ref, w_ref, m_ref, v_ref, g_ref, dl_ref, nm_ref, nv_ref):
        g = p_ref[0]
        for dev in range(1, N_DEV):
            g = g + p_ref[dev]
        g_ref[...] = g
        delta, nm, nv = _adamw(w_ref[...], g, m_ref[...], v_ref[...])
        dl_ref[...] = delta
        nm_ref[...] = nm
        nv_ref[...] = nv

    shp = jax.ShapeDtypeStruct(w.shape, F32)
    return pl.pallas_call(body, name="small_adam", in_specs=[VMEM_SPEC] * 4, out_specs=[VMEM_SPEC] * 4,
                          out_shape=[shp] * 4,
                          compiler_params=pltpu.CompilerParams(vmem_limit_bytes=VMEM_LIMIT_BYTES))(parts, w, m, v)


def _pack(vecs, mult=8 * LANES):
    flat = jnp.concatenate([a.reshape(-1).astype(F32) for a in vecs])
    pad = (-flat.shape[0]) % mult
    if pad:
        flat = jnp.concatenate([flat, jnp.zeros((pad,), F32)])
    return flat.reshape(8, -1)


def _unpack(flat, shapes):
    flat = flat.reshape(-1)
    out, off = [], 0
    for shp in shapes:
        n = math.prod(shp)
        out.append(flat[off:off + n].reshape(shp))
        off += n
    return out


BIG = ("win", "wa", "wb", "wo", "wg", "wu", "wd")
GRAD_GROUPS = (("wd", "wg", "wu"), ("wo", "wa", "wb"), ("win",))


def _local_step(x, target, mods, ln1_g, ln2_g, qg, kg, conv_w, weights, send_grads):
    s, d = x.shape
    n_l = mods.shape[0]
    saved = []
    h_in = x
    for l in range(n_l):
        sh1, sc1, g1, sh2, sc2, g2 = [mods[l, k * d:(k + 1) * d].reshape(1, d) for k in range(6)]
        qg2, kg2 = jnp.tile(qg[l:l + 1], (1, 2)), jnp.tile(kg[l:l + 1], (1, 2))
        h1 = _lnmod(h_in, ln1_g[l:l + 1], sc1, sh1)
        (win,), tie = weights(l, ("win",), h1)
        p = _mm_in(h1, win)
        ya, lt = _attn_fwd(p, qg2 + tie, kg2, d)
        yb = _conv_fwd(p, conv_w[l], d)
        (wa, wb, wo, wg, wu, wd), tie = weights(l, ("wa", "wb", "wo", "wg", "wu", "wd"), ya)
        wa, wb, wo = wa.reshape(d, d), wb.reshape(d, d), wo.reshape(d, d)
        merged, pa, pb = _branch(ya, yb, p, wa, wb, d)
        x1, mo = _out_proj(merged, wo, h_in, g1 + tie)
        h2 = _lnmod(x1, ln2_g[l:l + 1], sc2, sh2)
        gate, up, act = _ffn_up(h2, wg, wu)
        x2, f = _ffn_down(act, wd, x1, g2)
        saved.append(dict(x0=h_in, h1=h1, p=p, ya=ya, lt=lt, yb=yb, merged=merged, pa=pa, pb=pb, x1=x1, mo=mo,
                          h2=h2, gate=gate, up=up, act=act, f=f, win=win, wa=wa, wb=wb, wo=wo, wg=wg, wu=wu, wd=wd,
                          mod=(sh1, sc1, g1, sh2, sc2, g2), qg2=qg2, kg2=kg2))
        h_in = x2

    dx, loss_tile = _loss_head(h_in, target)

    small = [None] * n_l
    for l in reversed(range(n_l)):
        sv = saved[l]
        sh1, sc1, g1, sh2, sc2, g2 = sv["mod"]
        f4, n4 = sv["wg"].shape[-1], sv["win"].shape[-1]
        hsp = lambda tk: ((tk, d), lambda j, k: (k, 0))
        fsp = lambda tk: ((None, tk, f4), lambda j, k: (j, k, 0))
        dgate, dup, df, dg2 = _ffn_bwd1(dx, sv["f"], g2, sv["wd"], sv["gate"], sv["up"])
        g_wd = _mm_tn(sv["act"], df, fsp, hsp, (f4, d), "grad_wd")
        g_wg = _mm_tn(dgate, sv["h2"], fsp, hsp, (f4, d), "grad_wg")
        g_wu = _mm_tn(dup, sv["h2"], fsp, hsp, (f4, d), "grad_wu")
        tie = send_grads(l, dict(wd=g_wd, wg=g_wg, wu=g_wu))
        dh2 = _ffn_bwd2(dgate, dup, sv["wg"], sv["wu"])
        dx1, sums2 = _lnmod_bwd(sv["x1"], ln2_g[l:l + 1], sc2 + tie, dh2, dx)
        dmo, da, db, dya, dyb, dp, dg1 = _out_bwd(dx1, sv["mo"], g1, sv["wo"], sv["pa"], sv["pb"], sv["p"],
                                                        sv["wa"], sv["wb"], d)
        g_wo = _mm_tn_square(sv["merged"], dmo, "grad_wo")
        g_wa = _mm_tn_square(sv["ya"], da, "grad_wa")
        g_wb = _mm_tn_square(sv["yb"], db, "grad_wb")
        tie = send_grads(l, dict(wo=g_wo, wa=g_wa, wb=g_wb))
        dp, dconv = _conv_bwd(sv["p"], conv_w[l] + tie, dyb, dp, d)
        dp, dgain = _attn_bwd(sv["p"], sv["qg2"], sv["kg2"], dya, sv["lt"], dp, d)
        g_win = _mm_tn(sv["h1"], dp, hsp, lambda tk: ((tk, n4), lambda j, k: (k, j)), (d, n4), "grad_win")
        tie = send_grads(l, dict(win=g_win))
        dh1 = _mm_in_bwd(dp, sv["win"])
        dx, sums1 = _lnmod_bwd(sv["x0"], ln1_g[l:l + 1], sc1 + tie, dh1, dx1)
        dgain = jnp.sum(dgain[:, 0:2, :], axis=0)
        dgain = dgain[:, :HEAD_DIM] + dgain[:, HEAD_DIM:]
        dmod = jnp.concatenate([sums1[0], sums1[1], dg1[0], sums2[0], sums2[1], dg2[0]])
        small[l] = dict(dmod=dmod, ln1=sums1[2], ln2=sums2[2], qg=dgain[0], kg=dgain[1], conv=dconv[0:3])
    return loss_tile, dx, small


def kernel(x, c, ada_w, ada_b, ln1_g, w_in, q_norm_g, k_norm_g, conv_w, w_branch_a, w_branch_b, w_out, ln2_g, w_ffn_gate, w_ffn_up, w_ffn_down, loss_target, m_ada_w, m_ada_b, m_ln1_g, m_w_in, m_q_norm_g, m_k_norm_g, m_conv_w, m_w_branch_a, m_w_branch_b, m_w_out, m_ln2_g, m_w_ffn_gate, m_w_ffn_up, m_w_ffn_down, v_ada_w, v_ada_b, v_ln1_g, v_w_in, v_q_norm_g, v_k_norm_g, v_conv_w, v_w_branch_a, v_w_branch_b, v_w_out, v_ln2_g, v_w_ffn_gate, v_w_ffn_up, v_w_ffn_down):
    n_l, d, a4 = ada_w.shape
    cw4 = conv_w.shape[-1]
    ix, iy, ic = lax.axis_index("x"), lax.axis_index("y"), lax.axis_index("c")
    chip = 2 * ix + iy
    me = 2 * chip + ic

    big_w = dict(win=w_in, wa=w_branch_a, wb=w_branch_b, wo=w_out, wg=w_ffn_gate, wu=w_ffn_up, wd=w_ffn_down)
    big_m = dict(win=m_w_in, wa=m_w_branch_a, wb=m_w_branch_b, wo=m_w_out, wg=m_w_ffn_gate, wu=m_w_ffn_up,
                 wd=m_w_ffn_down)
    big_v = dict(win=v_w_in, wa=v_w_branch_a, wb=v_w_branch_b, wo=v_w_out, wg=v_w_ffn_gate, wu=v_w_ffn_up,
                 wd=v_w_ffn_down)

    def adam_view(a, k):
        return jnp.swapaxes(a, 1, 2) if k in ("wg", "wu") else a

    got = _gather8(_pack([c, conv_w])).reshape(N_DEV, -1)

    weight_groups = [(l, names) for l in range(n_l) for names in (("win",), ("wa", "wb", "wo", "wg", "wu", "wd"))]
    group_srcs = [[big_w[k][l].astype(BF16) for k in names] for l, names in weight_groups]
    started_w = {}

    def start_weights(gi):
        l, names = weight_groups[gi]
        copies = _weight_half_copies if gi == 0 else _weight_copies
        st = _split_start("weights_start_%d" % gi, copies, group_srcs[gi],
                          [(N_CHIP,) + sh.shape for sh in group_srcs[gi]], 3)
        for k in names:
            started_w[(l, k)] = [gi, names, st, None, copies]
        return st[4]

    got, group_srcs[0] = lax.optimization_barrier((got, group_srcs[0]))
    tie = start_weights(0)[0, 0]
    c_all = got[:, :d]
    conv_all = got[:, d:d + n_l * 3 * cw4].reshape(N_CHIP, 2, n_l, 3, cw4)[:, 0]
    conv_full = jnp.transpose(conv_all, (1, 2, 0, 3)).reshape(n_l, 3, N_CHIP * cw4)
    b_cols = lax.dynamic_slice_in_dim(ada_b, chip * a4, a4, axis=1).reshape(n_l, 1, a4)
    b_cols, group_srcs[1:] = lax.optimization_barrier((b_cols + tie, group_srcs[1:]))
    mod_cols = _ada_mod(c_all, ada_w, b_cols)
    mod_all = _gather8(_pack([mod_cols])).reshape(N_DEV, -1)[:, :n_l * N_DEV * a4]
    mod_all = mod_all.reshape(N_CHIP, 2, n_l, N_DEV, a4)[:, 0]
    mods = lax.dynamic_index_in_dim(mod_all, me, axis=2, keepdims=False)
    mods = jnp.transpose(mods, (1, 0, 2)).reshape(n_l, N_CHIP * a4)

    def weights(l, names, after):
        entry, tie = started_w[(l, names[0])], jnp.zeros((), F32)
        if entry[3] is None:
            lands = _split_wait("weights_wait_%d" % entry[0], entry[4], entry[2], after)
            if entry[4] is _weight_half_copies:
                passed = _split_start_in_place("weights_pass_start_%d" % entry[0], _weight_half_pass, lands, 3)
                lands = _split_wait_in_place("weights_pass_wait_%d" % entry[0], _weight_half_pass, passed, passed[3])
            nxt = entry[0] + 1
            if nxt < len(weight_groups):
                lands, group_srcs[nxt] = lax.optimization_barrier((lands, group_srcs[nxt]))
                tie = start_weights(nxt)[0, 0]
            lands = [lax.dynamic_update_index_in_dim(land, own, chip, 0) for land, own in zip(lands, entry[2][2])]
            for k in entry[1]:
                started_w[(l, k)][3] = dict(zip(entry[1], lands))
        return [started_w[(l, k)][3][k] for k in names], tie

    started_g, held_back = [], []

    def start_grads(l, grads, copies=_grad_copies, sems_per=7):
        names = tuple(grads)
        st = _split_start("grads_start_%d" % len(started_g), copies, [grads[k] for k in names],
                          [(N_DEV,) + grads[k].shape[1:] for k in names], sems_per)
        started_g.append((l, names, st, copies))
        return st[4][0, 0]

    def send_grads(l, grads):
        if l == 0 and tuple(grads) == GRAD_GROUPS[-1]:
            held_back.append(grads)
            return jnp.zeros((), F32)
        return start_grads(l, grads)

    loss_tile, grad_x, small = _local_step(
        x[0], loss_target[0], mods, ln1_g, ln2_g, q_norm_g, k_norm_g, conv_full, weights, send_grads)

    sm_shapes = [(n_l, 6 * d), (n_l, d), (n_l, d), (n_l, HEAD_DIM), (n_l, HEAD_DIM), (n_l, 3, d), (1,)]
    vec = _pack([jnp.stack([small[l][k] for l in range(n_l)]) for k in ("dmod", "ln1", "ln2", "qg", "kg", "conv")]
                + [loss_tile[0, 0:1]])
    n_vec = vec.shape[1] * 8
    all_vec = _gather8(vec).reshape(N_DEV, n_vec)
    all_vec, held_back = lax.optimization_barrier((all_vec, held_back))
    tie = sum([start_grads(0, grads, _grad_copies_same_core, 4) for grads in held_back], jnp.zeros((), F32))
    per_dev = [_unpack(all_vec[dev], sm_shapes) for dev in range(N_DEV)]
    dmod_all = jnp.stack([pd[0] for pd in per_dev])
    dmod_cols = jnp.transpose(lax.dynamic_slice_in_dim(dmod_all, chip * a4, a4, axis=2), (1, 0, 2))
    ada_out = _ada_grad_adam(jnp.transpose(c_all) + tie, dmod_cols, ada_w, m_ada_w, v_ada_w)

    dev_parts = jnp.stack([
        _pack([pd[0], pd[1], pd[2], pd[3], pd[4], lax.dynamic_slice_in_dim(pd[5], chip * cw4, cw4, axis=2), pd[6]])
        for pd in per_dev])
    zero1 = jnp.zeros((1,), F32)
    sw = _pack([ada_b, ln1_g, ln2_g, q_norm_g, k_norm_g, conv_w, zero1])
    sm = _pack([m_ada_b, m_ln1_g, m_ln2_g, m_q_norm_g, m_k_norm_g, m_conv_w, zero1])
    sv = _pack([v_ada_b, v_ln1_g, v_ln2_g, v_q_norm_g, v_k_norm_g, v_conv_w, zero1 + 1.0])
    out_shapes = [(n_l, 6 * d), (n_l, d), (n_l, d), (n_l, HEAD_DIM), (n_l, HEAD_DIM), (n_l, 3, cw4), (1,)]
    sm_out = [_unpack(o, out_shapes) for o in _small_adam(dev_parts, sw, sm, sv)]
    loss = 0.5 * sm_out[0][6][0] / d

    after = jnp.full((8, LANES), tie + sm_out[0][0][0, 0] + ada_out[0][0, 0, 0])
    big_out = {}
    for names in GRAD_GROUPS:
        got_parts = {}
        for gi, (l, sent, st, copies) in enumerate(started_g):
            if sent == names:
                parts = _split_wait("grads_wait_%d" % gi, copies, st, after)
                if copies is _grad_copies_same_core:
                    passed = _split_start_in_place("grads_pass_start_%d" % gi, _grad_pass_copies, parts, 3)
                    parts = _split_wait_in_place("grads_pass_wait_%d" % gi, _grad_pass_copies, passed, passed[3])
                for k, part, grad in zip(sent, parts, st[2]):
                    own = lax.dynamic_index_in_dim(grad, chip, 0, keepdims=False)
                    got_parts[(l, k)] = lax.dynamic_update_index_in_dim(part, own, me, 0)
        for k in names:
            res = _sum_adam([got_parts[(l, k)] for l in range(n_l)], adam_view(big_w[k], k), adam_view(big_m[k], k),
                            adam_view(big_v[k], k), "sum_adam_" + k)
            after = res[0]
            big_out[k] = [adam_view(r, k) for r in res]

    outs = [loss, grad_x[None]]
    for kind in range(4):
        sm_k = sm_out[kind]
        outs += [ada_out[kind], sm_k[0], sm_k[1], big_out["win"][kind], sm_k[3], sm_k[4], sm_k[5],
                 big_out["wa"][kind], big_out["wb"][kind], big_out["wo"][kind], sm_k[2],
                 big_out["wg"][kind], big_out["wu"][kind], big_out["wd"][kind]]
    return tuple(outs)
```

```python
import math

import jax
import jax.numpy as jnp
from jax import lax
from jax.experimental import pallas as pl
from jax.experimental.pallas import tpu as pltpu

F32 = jnp.float32
BF16 = jnp.bfloat16
MESH_ID = pl.DeviceIdType.MESH

EPS = 1e-6
HEAD_DIM = 64
Q_BLOCK = 128
Q_SUPER = 1024
Q_SUPER_BWD = 1024
KEY_UNROLL = 4
LANES = 128
N_DEV = 8
N_CHIP = 4
VMEM_LIMIT_BYTES = 56 * 1024 * 1024

ADAM_LR = 0.001
ADAM_B1 = 0.9
ADAM_B2 = 0.999
ADAM_EPS = 1e-08
ADAM_WD = 0.01
ADAM_STEP = 10

HBM_SPEC = pl.BlockSpec(memory_space=pltpu.HBM)
ANY_SPEC = pl.BlockSpec(memory_space=pl.ANY)
SEM_SPEC = pl.BlockSpec(memory_space=pltpu.SEMAPHORE)
VMEM_SPEC = pl.BlockSpec(memory_space=pltpu.VMEM)
SIDE_EFFECT = pltpu.SideEffectType.DATAFLOW_SIDE_EFFECTING


def _params(*sem):
    return pltpu.CompilerParams(dimension_semantics=tuple(sem), vmem_limit_bytes=VMEM_LIMIT_BYTES)


def _tile(n, pref):
    return pref if n % pref == 0 else n


def _dot(a, b):
    return jnp.dot(a, b, preferred_element_type=F32)


def _dot_nt(a, b):
    return lax.dot_general(a, b, (((1,), (1,)), ((), ())), preferred_element_type=F32)


def _dot_tn(a, b):
    return lax.dot_general(a, b, (((0,), (0,)), ((), ())), preferred_element_type=F32)


def _adamw(w, g, m, v):
    m = ADAM_B1 * m + (1.0 - ADAM_B1) * g
    v = ADAM_B2 * v + (1.0 - ADAM_B2) * (g * g)
    m_hat = m / (1.0 - ADAM_B1 ** ADAM_STEP)
    v_hat = v / (1.0 - ADAM_B2 ** ADAM_STEP)
    delta = -ADAM_LR * (m_hat / (jnp.sqrt(v_hat) + ADAM_EPS) + ADAM_WD * w)
    return delta, m, v


def _hbm(a):
    return pltpu.with_memory_space_constraint(a, pltpu.HBM)


def _peer(x, y, c, k):
    return (1 - x if k & 4 else x, 1 - y if k & 2 else y, 1 - c if k & 1 else c)


def _gather8(v):
    rows_per, m = v.shape

    def body(v_ref, out_ref, send_sems, recv_sems, local_sem):
        x, y, c = lax.axis_index("x"), lax.axis_index("y"), lax.axis_index("c")

        def rows(p):
            return out_ref.at[pl.ds((4 * p[0] + 2 * p[1] + p[2]) * rows_per, rows_per), :]

        me = (x, y, c)
        mine = pltpu.make_async_copy(v_ref, rows(me), local_sem)
        mine.start()
        sends = []
        for k in range(1, N_DEV):
            cp = pltpu.make_async_remote_copy(
                src_ref=v_ref, dst_ref=rows(me), send_sem=send_sems.at[k - 1], recv_sem=recv_sems.at[k - 1],
                device_id=_peer(x, y, c, k), device_id_type=MESH_ID)
            cp.start()
            sends.append(cp)
        for k in range(1, N_DEV):
            pltpu.make_async_remote_copy(
                src_ref=v_ref, dst_ref=rows(_peer(x, y, c, k)), send_sem=send_sems.at[k - 1],
                recv_sem=recv_sems.at[k - 1], device_id=_peer(x, y, c, k), device_id_type=MESH_ID).wait_recv()
        for cp in sends:
            cp.wait_send()
        mine.wait()

    return pl.pallas_call(
        body, name="gather8",
        out_shape=jax.ShapeDtypeStruct((N_DEV * rows_per, m), v.dtype),
        in_specs=[VMEM_SPEC], out_specs=VMEM_SPEC,
        scratch_shapes=[pltpu.SemaphoreType.DMA((N_DEV - 1,)), pltpu.SemaphoreType.DMA((N_DEV - 1,)),
                        pltpu.SemaphoreType.DMA],
    )(v)


def _weight_copies(srcs, lands, send_sems, recv_sems):
    x, y, c = lax.axis_index("x"), lax.axis_index("y"), lax.axis_index("c")
    chips = [(1 - x, y), (x, 1 - y), (1 - x, 1 - y)]
    sends, recvs = [], []
    for a, (src, land) in enumerate(zip(srcs, lands)):
        for j, (px, py) in enumerate(chips):
            def copy(dst_block, a=a, j=j, px=px, py=py, src=src, land=land):
                return pltpu.make_async_remote_copy(
                    src_ref=src, dst_ref=land.at[dst_block], send_sem=send_sems.at[3 * a + j],
                    recv_sem=recv_sems.at[3 * a + j], device_id=(px, py, c), device_id_type=MESH_ID)
            sends.append(copy(2 * x + y))
            recvs.append(copy(2 * px + py))
    return sends, recvs


def _weight_half_copies(srcs, lands, send_sems, recv_sems):
    x, y, c = lax.axis_index("x"), lax.axis_index("y"), lax.axis_index("c")
    chips = [(1 - x, y), (x, 1 - y), (1 - x, 1 - y)]
    sends, recvs = [], []
    for a, (src, land) in enumerate(zip(srcs, lands)):
        half = src.shape[0] // 2
        rows = pl.ds(c * half, half)
        for j, (px, py) in enumerate(chips):
            def copy(dst_block, a=a, j=j, px=px, py=py, src=src, land=land, rows=rows):
                return pltpu.make_async_remote_copy(
                    src_ref=src.at[rows], dst_ref=land.at[dst_block, rows], send_sem=send_sems.at[3 * a + j],
                    recv_sem=recv_sems.at[3 * a + j], device_id=(px, py, c), device_id_type=MESH_ID)
            sends.append(copy(2 * x + y))
            recvs.append(copy(2 * px + py))
    return sends, recvs


def _weight_half_pass(lands, same_lands, send_sems, recv_sems):
    del same_lands
    x, y, c = lax.axis_index("x"), lax.axis_index("y"), lax.axis_index("c")
    chips = [(1 - x, y), (x, 1 - y), (1 - x, 1 - y)]
    sends, recvs = [], []
    for a, land in enumerate(lands):
        half = land.shape[1] // 2
        for j, (px, py) in enumerate(chips):
            def copy(pc, a=a, j=j, px=px, py=py, land=land, half=half):
                part = land.at[2 * px + py, pl.ds(pc * half, half)]
                return pltpu.make_async_remote_copy(
                    src_ref=part, dst_ref=part, send_sem=send_sems.at[3 * a + j], recv_sem=recv_sems.at[3 * a + j],
                    device_id=(x, y, 1 - c), device_id_type=MESH_ID)
            sends.append(copy(c))
            recvs.append(copy(1 - c))
    return sends, recvs


def _split_start(name, copies, srcs, land_shapes, sems_per_src):
    n = len(srcs)

    def body(*refs):
        sends, _ = copies(refs[:n], refs[n + 2:2 * n + 2], refs[n], refs[n + 1])
        for cp in sends:
            cp.start()
        token = refs[-1]
        token[...] = jnp.zeros_like(token)

    n_sems = sems_per_src * n
    outs = pl.pallas_call(
        body, name=name,
        out_shape=(pltpu.SemaphoreType.DMA((n_sems,)), pltpu.SemaphoreType.DMA((n_sems,)),
                   *[pltpu.HBM(shape, a.dtype) for a, shape in zip(srcs, land_shapes)],
                   jax.ShapeDtypeStruct((8, LANES), F32)),
        in_specs=[HBM_SPEC] * n, out_specs=(SEM_SPEC, SEM_SPEC, *[HBM_SPEC] * n, VMEM_SPEC),
        compiler_params=pltpu.CompilerParams(has_side_effects=SIDE_EFFECT),
    )(*[_hbm(a) for a in srcs])
    return outs[0], outs[1], list(srcs), list(outs[2:2 + n]), outs[-1]


def _split_wait(name, copies, started, after):
    send_sems, recv_sems, srcs, lands, _ = started
    n = len(srcs)

    def body(*refs):
        sends, recvs = copies(refs[:n], refs[n:2 * n], refs[2 * n], refs[2 * n + 1])
        for cp in sends:
            cp.wait_send()
        for cp in recvs:
            cp.wait_recv()

    return pl.pallas_call(
        body, name=name,
        out_shape=tuple(pltpu.HBM(a.shape, a.dtype) for a in lands),
        in_specs=[HBM_SPEC] * (2 * n) + [SEM_SPEC, SEM_SPEC, ANY_SPEC], out_specs=tuple([HBM_SPEC] * n),
        input_output_aliases={n + i: i for i in range(n)},
        compiler_params=pltpu.CompilerParams(has_side_effects=SIDE_EFFECT),
    )(*srcs, *lands, send_sems, recv_sems, after)


def _split_start_in_place(name, copies, bufs, sems_per_buf):
    n = len(bufs)

    def body(*refs):
        sends, _ = copies(refs[:n], refs[:n], refs[n], refs[n + 1])
        for cp in sends:
            cp.start()
        token = refs[-1]
        token[...] = jnp.zeros_like(token)

    n_sems = sems_per_buf * n
    outs = pl.pallas_call(
        body, name=name,
        out_shape=(pltpu.SemaphoreType.DMA((n_sems,)), pltpu.SemaphoreType.DMA((n_sems,)),
                   *[pltpu.HBM(a.shape, a.dtype) for a in bufs], jax.ShapeDtypeStruct((8, LANES), F32)),
        in_specs=[HBM_SPEC] * n, out_specs=(SEM_SPEC, SEM_SPEC, *[HBM_SPEC] * n, VMEM_SPEC),
        input_output_aliases={i: 2 + i for i in range(n)},
        compiler_params=pltpu.CompilerParams(has_side_effects=SIDE_EFFECT),
    )(*[_hbm(a) for a in bufs])
    return outs[0], outs[1], list(outs[2:2 + n]), outs[-1]


def _split_wait_in_place(name, copies, started, after):
    send_sems, recv_sems, bufs, _ = started
    n = len(bufs)

    def body(*refs):
        sends, recvs = copies(refs[:n], refs[:n], refs[n], refs[n + 1])
        for cp in sends:
            cp.wait_send()
        for cp in recvs:
            cp.wait_recv()

    return pl.pallas_call(
        body, name=name,
        out_shape=tuple(pltpu.HBM(a.shape, a.dtype) for a in bufs),
        in_specs=[HBM_SPEC] * n + [SEM_SPEC, SEM_SPEC, ANY_SPEC], out_specs=tuple([HBM_SPEC] * n),
        input_output_aliases={i: i for i in range(n)},
        compiler_params=pltpu.CompilerParams(has_side_effects=SIDE_EFFECT),
    )(*bufs, send_sems, recv_sems, after)


def _grad_copies(grads, parts, send_sems, recv_sems):
    x, y, c = lax.axis_index("x"), lax.axis_index("y"), lax.axis_index("c")
    chips = [(1 - x, y), (x, 1 - y), (1 - x, 1 - y)]
    my_slot = 4 * x + 2 * y + c
    sends, recvs = [], []
    for a, (grad, part) in enumerate(zip(grads, parts)):
        def copy(k, block, slot, to, a=a, grad=grad, part=part):
            return pltpu.make_async_remote_copy(
                src_ref=grad.at[block], dst_ref=part.at[slot], send_sem=send_sems.at[7 * a + k],
                recv_sem=recv_sems.at[7 * a + k], device_id=to, device_id_type=MESH_ID)
        sends.append(copy(0, 2 * x + y, my_slot, (x, y, 1 - c)))
        recvs.append(copy(0, 2 * x + y, 4 * x + 2 * y + (1 - c), (x, y, 1 - c)))
        for j, (px, py) in enumerate(chips):
            for other, pc in enumerate((c, 1 - c)):
                sends.append(copy(1 + 2 * j + other, 2 * px + py, my_slot, (px, py, pc)))
                recvs.append(copy(1 + 2 * j + other, 2 * x + y, 4 * px + 2 * py + pc, (px, py, pc)))
    return sends, recvs


def _grad_copies_same_core(grads, parts, send_sems, recv_sems):
    x, y, c = lax.axis_index("x"), lax.axis_index("y"), lax.axis_index("c")
    chips = [(1 - x, y), (x, 1 - y), (1 - x, 1 - y)]
    my_slot = 4 * x + 2 * y + c
    sends, recvs = [], []
    for a, (grad, part) in enumerate(zip(grads, parts)):
        def copy(k, block, slot, to, a=a, grad=grad, part=part):
            return pltpu.make_async_remote_copy(
                src_ref=grad.at[block], dst_ref=part.at[slot], send_sem=send_sems.at[4 * a + k],
                recv_sem=recv_sems.at[4 * a + k], device_id=to, device_id_type=MESH_ID)
        sends.append(copy(0, 2 * x + y, my_slot, (x, y, 1 - c)))
        recvs.append(copy(0, 2 * x + y, 4 * x + 2 * y + (1 - c), (x, y, 1 - c)))
        for j, (px, py) in enumerate(chips):
            sends.append(copy(1 + j, 2 * px + py, my_slot, (px, py, c)))
            recvs.append(copy(1 + j, 2 * x + y, 4 * px + 2 * py + c, (px, py, c)))
    return sends, recvs


def _grad_pass_copies(parts, same_parts, send_sems, recv_sems):
    del same_parts
    x, y, c = lax.axis_index("x"), lax.axis_index("y"), lax.axis_index("c")
    chips = [(1 - x, y), (x, 1 - y), (1 - x, 1 - y)]
    sends, recvs = [], []
    for a, part in enumerate(parts):
        for j, (px, py) in enumerate(chips):
            def copy(pc, a=a, j=j, px=px, py=py, part=part):
                slot = part.at[4 * px + 2 * py + pc]
                return pltpu.make_async_remote_copy(
                    src_ref=slot, dst_ref=slot, send_sem=send_sems.at[3 * a + j], recv_sem=recv_sems.at[3 * a + j],
                    device_id=(x, y, 1 - c), device_id_type=MESH_ID)
            sends.append(copy(c))
            recvs.append(copy(1 - c))
    return sends, recvs


def _ada_mod(c_all, ada_w, ada_b_cols):
    n_l, d, a4 = ada_w.shape
    tn = _tile(a4, 512)

    def body(c_ref, w_ref, b_ref, o_ref):
        cv = c_ref[...]
        ca = (cv * jax.nn.sigmoid(cv)).astype(BF16)
        o_ref[...] = _dot(ca, w_ref[...].astype(BF16)) + b_ref[...]

    return pl.pallas_call(
        body, name="ada_mod", grid=(n_l, a4 // tn),
        in_specs=[pl.BlockSpec((N_DEV, d), lambda l, j: (0, 0)),
                  pl.BlockSpec((None, d, tn), lambda l, j: (l, 0, j)),
                  pl.BlockSpec((None, 1, tn), lambda l, j: (l, 0, j))],
        out_specs=pl.BlockSpec((None, N_DEV, tn), lambda l, j: (l, 0, j)),
        out_shape=jax.ShapeDtypeStruct((n_l, N_DEV, a4), F32),
        compiler_params=_params("parallel", "parallel"),
    )(c_all, ada_w, ada_b_cols)


def _ada_grad_adam(c_all_t, dmod_cols, w, m, v):
    n_l, d, a4 = w.shape
    tn = _tile(a4, 512)

    def body(ct_ref, dm_ref, w_ref, m_ref, v_ref, g_ref, dl_ref, nm_ref, nv_ref):
        ct = ct_ref[...]
        ca = ct * jax.nn.sigmoid(ct)
        dm = dm_ref[...]
        g = ca[:, 0:1] * dm[0:1, :]
        for dev in range(1, N_DEV):
            g = g + ca[:, dev:dev + 1] * dm[dev:dev + 1, :]
        g_ref[...] = g
        delta, nm, nv = _adamw(w_ref[...], g, m_ref[...], v_ref[...])
        dl_ref[...] = delta
        nm_ref[...] = nm
        nv_ref[...] = nv

    wspec = pl.BlockSpec((None, d, tn), lambda l, j: (l, 0, j))
    shp = jax.ShapeDtypeStruct(w.shape, F32)
    return pl.pallas_call(
        body, name="ada_grad_adam", grid=(n_l, a4 // tn),
        in_specs=[pl.BlockSpec((d, N_DEV), lambda l, j: (0, 0)),
                  pl.BlockSpec((None, N_DEV, tn), lambda l, j: (l, 0, j)), wspec, wspec, wspec],
        out_specs=[wspec] * 4, out_shape=[shp] * 4,
        compiler_params=_params("parallel", "parallel"),
    )(c_all_t, dmod_cols, w, m, v)


def _lnmod(x, g, sc, sh):
    s, d = x.shape
    tm = _tile(s, 1024)

    def body(x_ref, g_ref, sc_ref, sh_ref, h_ref):
        xv = x_ref[...]
        r = lax.rsqrt(jnp.mean(xv * xv, axis=-1, keepdims=True) + EPS)
        h_ref[...] = ((xv * r * g_ref[...]) * (1.0 + sc_ref[...]) + sh_ref[...]).astype(BF16)

    vec = pl.BlockSpec((1, d), lambda i: (0, 0))
    row = pl.BlockSpec((tm, d), lambda i: (i, 0))
    return pl.pallas_call(
        body, name="lnmod", grid=(s // tm,), in_specs=[row, vec, vec, vec], out_specs=row,
        out_shape=jax.ShapeDtypeStruct((s, d), BF16), compiler_params=_params("parallel"),
    )(x, g, sc, sh)


def _mm_in(h, w_g):
    s, d = h.shape
    n4 = w_g.shape[-1]
    tm = _tile(s, 1024)

    def body(a_ref, b_ref, o_ref):
        o_ref[...] = _dot(a_ref[...], b_ref[...]).astype(BF16)

    return pl.pallas_call(
        body, name="mm_in", grid=(N_CHIP, s // tm),
        in_specs=[pl.BlockSpec((tm, d), lambda j, i: (i, 0)),
                  pl.BlockSpec((None, d, n4), lambda j, i: (j, 0, 0))],
        out_specs=pl.BlockSpec((tm, n4), lambda j, i: (i, j)),
        out_shape=jax.ShapeDtypeStruct((s, N_CHIP * n4), BF16),
        compiler_params=_params("parallel", "parallel"),
    )(h, w_g)


def _pair_mean(x, low):
    lo = jnp.sum(jnp.where(low, x, 0.0), axis=-1, keepdims=True)
    hi = jnp.sum(jnp.where(low, 0.0, x), axis=-1, keepdims=True)
    return jnp.where(low, lo, hi) * (1.0 / HEAD_DIM)


def _pair_norm(x, low):
    r = lax.rsqrt(_pair_mean(x * x, low) + EPS)
    return x * r, r


def _log_not(z):
    nz = -z
    return jnp.minimum(nz, 0.0) - jnp.log(1.0 + jnp.exp(jnp.minimum(z, nz)))


def _attn_consts(inclusive):
    low = lax.broadcasted_iota(jnp.int32, (1, LANES), 1) < HEAD_DIM
    row = lax.broadcasted_iota(jnp.int32, (Q_BLOCK, Q_BLOCK), 0)
    col = lax.broadcasted_iota(jnp.int32, (Q_BLOCK, Q_BLOCK), 1)
    tri = (row <= col) if inclusive else (row > col)
    w2 = jnp.concatenate([tri.astype(BF16), jnp.ones((Q_BLOCK, Q_BLOCK), BF16)], axis=1)
    return low, col < row, jnp.concatenate([w2, w2], axis=0)


def _split_cat(v):
    hi = v.astype(BF16)
    return jnp.concatenate([hi, (v - hi.astype(F32)).astype(BF16)], axis=1)


def _fill_pair_blocks(dst, src_fn, low, n_kb):
    def fill(b, _):
        v = src_fn(pl.ds(pl.multiple_of(b * Q_BLOCK, Q_BLOCK), Q_BLOCK))
        dst[b, 0:Q_BLOCK, :] = jnp.where(low, v, 0.0).astype(BF16)
        dst[b, Q_BLOCK:2 * Q_BLOCK, :] = jnp.where(low, 0.0, v).astype(BF16)
        return 0

    lax.fori_loop(0, n_kb, fill, 0)


def _attn_fwd(p, qg2, kg2, d):
    s = p.shape[0]
    n_pairs = d // LANES
    qsb = _tile(s, Q_SUPER)
    n_sub, n_sb, n_kb = qsb // Q_BLOCK, s // qsb, s // Q_BLOCK
    unroll = math.gcd(KEY_UNROLL, n_sub)
    chunk = _tile(s, 512)
    inv_sqrt = 1.0 / math.sqrt(HEAD_DIM)

    def body(q_ref, k_ref, v_ref, qg_ref, kg_ref, o_ref, lt_ref, qs, k2, v2, run, acc):
        low, causal, w4 = _attn_consts(False)

        def prep(r, _):
            rows = pl.ds(pl.multiple_of(r * chunk, chunk), chunk)
            qs[rows, :] = (_pair_norm(q_ref[rows, :].astype(F32), low)[0] * (qg_ref[...] * inv_sqrt)).astype(BF16)
            return 0

        lax.fori_loop(0, s // chunk, prep, 0)
        _fill_pair_blocks(k2, lambda rows: _pair_norm(k_ref[rows, :].astype(F32), low)[0] * kg_ref[...], low, n_kb)
        _fill_pair_blocks(v2, lambda rows: v_ref[rows, :].astype(F32), low, n_kb)

        def step(sb, j, t0=0, diag_t=None):
            rows = pl.ds(pl.multiple_of(sb * qsb + t0 * Q_BLOCK, Q_BLOCK), (n_sub - t0) * Q_BLOCK)
            z_both = _dot_nt(qs[rows, :], k2[j])
            zls, cats = [], []
            for t in range(t0, n_sub):
                sub = slice((t - t0) * Q_BLOCK, (t - t0 + 1) * Q_BLOCK)
                for h in range(2):
                    z = z_both[sub, h * LANES:(h + 1) * LANES]
                    ln = _log_not(z)
                    if t == diag_t:
                        ln = jnp.where(causal, ln, 0.0)
                    zls.append(z + ln)
                    cats.append(_split_cat(ln))
            c2 = _dot(jnp.concatenate(cats, axis=0), w4)
            a_rows = []
            for t in range(t0, n_sub):
                sub = slice(t * Q_BLOCK, (t + 1) * Q_BLOCK)
                a_pair = []
                for h in range(2):
                    i = 2 * (t - t0) + h
                    tile = slice(i * Q_BLOCK, (i + 1) * Q_BLOCK)
                    later = run[h, sub, :]
                    log_a = zls[i] + c2[tile, :LANES] + later
                    if t == diag_t:
                        log_a = jnp.where(causal, log_a, -1e30)
                    a_pair.append(jnp.exp(log_a).astype(BF16))
                    run[h, sub, :] = later + c2[tile, LANES:]
                a_rows.append(jnp.concatenate(a_pair, axis=1))
            acc[t0 * Q_BLOCK:, :] += _dot(jnp.concatenate(a_rows, axis=0), v2[j])

        def super_block(sb, _):
            run[...] = jnp.zeros_like(run)
            acc[...] = jnp.zeros_like(acc)
            for t in reversed(range(n_sub)):
                step(sb, sb * n_sub + t, t0=t, diag_t=t)

            def below(n, _):
                for u in range(unroll):
                    step(sb, sb * n_sub - 1 - (unroll * n + u))
                return 0

            lax.fori_loop(0, sb * (n_sub // unroll), below, 0)
            rows_sb = pl.ds(pl.multiple_of(sb * qsb, qsb), qsb)
            o_ref[rows_sb, :] = acc[...].astype(BF16)
            lt_ref[rows_sb, :] = jnp.where(low, run[0], run[1])
            return 0

        lax.fori_loop(0, n_sb, super_block, 0)

    def seg(k):
        return pl.BlockSpec((s, LANES), lambda h, k=k: (0, k * n_pairs + h))

    vec = pl.BlockSpec((1, LANES), lambda h: (0, 0))
    out = pl.BlockSpec((s, LANES), lambda h: (0, h))
    return pl.pallas_call(
        body, name="attn_fwd", grid=(n_pairs,),
        in_specs=[seg(0), seg(1), seg(2), vec, vec], out_specs=[out, out],
        out_shape=[jax.ShapeDtypeStruct((s, d), BF16), jax.ShapeDtypeStruct((s, d), F32)],
        scratch_shapes=[pltpu.VMEM((s, LANES), BF16)] + [pltpu.VMEM((n_kb, 2 * Q_BLOCK, LANES), BF16)] * 2
        + [pltpu.VMEM((2, qsb, LANES), F32), pltpu.VMEM((qsb, LANES), F32)],
        compiler_params=_params("parallel"),
    )(p, p, p, qg2, kg2)


def _conv_rows(s):
    return _tile(s, 512)


def _conv_fwd(p, conv_w, d):
    s = p.shape[0]
    nb = d // LANES
    rows_n = _conv_rows(s)

    def body(cb_ref, cc_ref, cx_ref, w_ref, y_ref, us):
        us[pl.ds(0, 8), :] = jnp.zeros((8, LANES), F32)

        def fill(r, _):
            rows = pl.ds(pl.multiple_of(r * rows_n, rows_n), rows_n)
            us[pl.ds(pl.multiple_of(r * rows_n + 8, 8), rows_n), :] = cc_ref[rows, :].astype(F32) * cx_ref[rows, :].astype(F32)
            return 0

        lax.fori_loop(0, s // rows_n, fill, 0)
        w = w_ref[...]

        def out(r, _):
            rows = pl.ds(pl.multiple_of(r * rows_n, rows_n), rows_n)
            ext = us[pl.ds(pl.multiple_of(r * rows_n, 8), rows_n + 8), :]
            cv = (w[0:1, :] * pltpu.roll(ext, 2, 0)[8:, :] + w[1:2, :] * pltpu.roll(ext, 1, 0)[8:, :]
                  + w[2:3, :] * ext[8:, :])
            y_ref[rows, :] = (cb_ref[rows, :].astype(F32) * cv).astype(BF16)
            return 0

        lax.fori_loop(0, s // rows_n, out, 0)

    def seg(k):
        return pl.BlockSpec((s, LANES), lambda b, k=k: (0, k * nb + b))

    return pl.pallas_call(
        body, name="conv_fwd", grid=(nb,),
        in_specs=[seg(3), seg(4), seg(5), pl.BlockSpec((3, LANES), lambda b: (0, b))],
        out_specs=pl.BlockSpec((s, LANES), lambda b: (0, b)),
        out_shape=jax.ShapeDtypeStruct((s, d), BF16),
        scratch_shapes=[pltpu.VMEM((s + 8, LANES), F32)],
        compiler_params=_params("parallel"),
    )(p, p, p, conv_w)


def _branch(ya, yb, p, wa, wb, d):
    s = ya.shape[0]
    tm = _tile(s, 512)

    def body(ya_ref, yb_ref, ga_ref, gb_ref, wa_ref, wb_ref, m_ref, a_ref, b_ref):
        pa = _dot(ya_ref[...], wa_ref[...])
        pb = _dot(yb_ref[...], wb_ref[...])
        ga, gb = ga_ref[...].astype(F32), gb_ref[...].astype(F32)
        m_ref[...] = (jax.nn.sigmoid(ga) * pa + jax.nn.sigmoid(gb) * pb).astype(BF16)
        a_ref[...] = pa.astype(BF16)
        b_ref[...] = pb.astype(BF16)

    row = pl.BlockSpec((tm, d), lambda i: (i, 0))
    wsp = pl.BlockSpec((d, d), lambda i: (0, 0))
    shp = jax.ShapeDtypeStruct((s, d), BF16)
    return pl.pallas_call(
        body, name="branch", grid=(s // tm,),
        in_specs=[row, row, pl.BlockSpec((tm, d), lambda i: (i, 6)), pl.BlockSpec((tm, d), lambda i: (i, 7)), wsp, wsp],
        out_specs=[row, row, row], out_shape=[shp, shp, shp], compiler_params=_params("parallel"),
    )(ya, yb, p, p, wa, wb)


def _out_proj(merged, wout, x0, g1, ln_g, sc, sh):
    s, d = x0.shape
    tm = _tile(s, 1024)

    def body(m_ref, w_ref, x_ref, g_ref, lg_ref, sc_ref, sh_ref, x1_ref, mo_ref, h_ref):
        mo = _dot(m_ref[...], w_ref[...])
        mo_ref[...] = mo
        x1 = x_ref[...] + g_ref[...] * mo
        x1_ref[...] = x1
        r = lax.rsqrt(jnp.mean(x1 * x1, axis=-1, keepdims=True) + EPS)
        h_ref[...] = ((x1 * r * lg_ref[...]) * (1.0 + sc_ref[...]) + sh_ref[...]).astype(BF16)

    row = pl.BlockSpec((tm, d), lambda i: (i, 0))
    vec = pl.BlockSpec((1, d), lambda i: (0, 0))
    shp = jax.ShapeDtypeStruct((s, d), F32)
    return pl.pallas_call(
        body, name="out_proj", grid=(s // tm,),
        in_specs=[row, pl.BlockSpec((d, d), lambda i: (0, 0)), row, vec, vec, vec, vec],
        out_specs=[row, row, row], out_shape=[shp, shp, jax.ShapeDtypeStruct((s, d), BF16)],
        compiler_params=_params("parallel"),
    )(merged, wout, x0, g1, ln_g, sc, sh)


def _ffn_up(h, wg_g, wu_g):
    s, d = h.shape
    f4 = wg_g.shape[-1]
    tm = _tile(s, 1024)

    def body(h_ref, wg_ref, wu_ref, gate_ref, up_ref, act_ref):
        hv = h_ref[...]
        gt = _dot(hv, wg_ref[...])
        up = _dot(hv, wu_ref[...])
        gate_ref[...] = gt.astype(BF16)
        up_ref[...] = up.astype(BF16)
        act_ref[...] = (gt * jax.nn.sigmoid(gt) * up).astype(BF16)

    wsp = pl.BlockSpec((None, d, f4), lambda j, i: (j, 0, 0))
    osp = pl.BlockSpec((None, tm, f4), lambda j, i: (j, i, 0))
    shp = jax.ShapeDtypeStruct((N_CHIP, s, f4), BF16)
    return pl.pallas_call(
        body, name="ffn_up", grid=(N_CHIP, s // tm),
        in_specs=[pl.BlockSpec((tm, d), lambda j, i: (i, 0)), wsp, wsp],
        out_specs=[osp, osp, osp], out_shape=[shp, shp, shp], compiler_params=_params("parallel", "parallel"),
    )(h, wg_g, wu_g)


def _ffn_down(act, wd_g, x1, g2):
    s, d = x1.shape
    f4 = act.shape[-1]
    tm = _tile(s, 1024)

    def body(a_ref, w_ref, x_ref, g_ref, x2_ref, f_ref, acc):
        j = pl.program_id(1)

        @pl.when(j == 0)
        def _():
            acc[...] = jnp.zeros_like(acc)

        acc[...] += _dot(a_ref[...], w_ref[...])

        @pl.when(j == N_CHIP - 1)
        def _():
            f = acc[...]
            f_ref[...] = f
            x2_ref[...] = x_ref[...] + g_ref[...] * f

    row = pl.BlockSpec((tm, d), lambda i, j: (i, 0))
    shp = jax.ShapeDtypeStruct((s, d), F32)
    return pl.pallas_call(
        body, name="ffn_down", grid=(s // tm, N_CHIP),
        in_specs=[pl.BlockSpec((None, tm, f4), lambda i, j: (j, i, 0)),
                  pl.BlockSpec((None, f4, d), lambda i, j: (j, 0, 0)),
                  row, pl.BlockSpec((1, d), lambda i, j: (0, 0))],
        out_specs=[row, row], out_shape=[shp, shp],
        scratch_shapes=[pltpu.VMEM((tm, d), F32)], compiler_params=_params("parallel", "arbitrary"),
    )(act, wd_g, x1, g2)


def _loss_head(y, target):
    s, d = y.shape
    tm = _tile(s, 1024)
    n_steps = s // tm

    def body(y_ref, t_ref, dy_ref, l_ref, acc):
        i = pl.program_id(0)

        @pl.when(i == 0)
        def _():
            acc[...] = jnp.zeros_like(acc)

        err = y_ref[...] - t_ref[...]
        dy_ref[...] = err / d
        acc[...] += jnp.sum(err * err, axis=0, keepdims=True)

        @pl.when(i == n_steps - 1)
        def _():
            l_ref[...] = jnp.broadcast_to(jnp.sum(acc[...], axis=1, keepdims=True), (8, LANES))

    row = pl.BlockSpec((tm, d), lambda i: (i, 0))
    return pl.pallas_call(
        body, name="loss_head", grid=(n_steps,), in_specs=[row, row],
        out_specs=[row, pl.BlockSpec((8, LANES), lambda i: (0, 0))],
        out_shape=[jax.ShapeDtypeStruct((s, d), F32), jax.ShapeDtypeStruct((8, LANES), F32)],
        scratch_shapes=[pltpu.VMEM((1, d), F32)], compiler_params=_params("arbitrary"),
    )(y, target)


def _mm_tn(a, b, a_spec, b_spec, out_rc, name):
    r, c = out_rc
    s = a.shape[-2]
    tk = _tile(s, 1024)
    nk = s // tk

    def body(a_ref, b_ref, o_ref, acc):
        k = pl.program_id(1)

        @pl.when(k == 0)
        def _():
            acc[...] = jnp.zeros_like(acc)

        acc[...] += _dot_tn(a_ref[...], b_ref[...])

        @pl.when(k == nk - 1)
        def _():
            o_ref[...] = acc[...].astype(BF16)

    return pl.pallas_call(
        body, name=name, grid=(N_CHIP, nk),
        in_specs=[pl.BlockSpec(*a_spec(tk)), pl.BlockSpec(*b_spec(tk))],
        out_specs=pl.BlockSpec((None, r, c), lambda j, k: (j, 0, 0)),
        out_shape=jax.ShapeDtypeStruct((N_CHIP, r, c), BF16),
        scratch_shapes=[pltpu.VMEM((r, c), F32)], compiler_params=_params("parallel", "arbitrary"),
    )(a, b)


def _mm_tn_square(a, b, name):
    s, d = a.shape
    r4 = d // N_CHIP
    tk = _tile(s, 1024)
    nk = s // tk

    def body(a_ref, b_ref, o_ref, acc):
        k = pl.program_id(0)

        @pl.when(k == 0)
        def _():
            acc[...] = jnp.zeros_like(acc)

        acc[...] += _dot_tn(a_ref[...], b_ref[...])

        @pl.when(k == nk - 1)
        def _():
            for j in range(N_CHIP):
                o_ref[j] = acc[j * r4:(j + 1) * r4, :].astype(BF16)

    blk = pl.BlockSpec((tk, d), lambda k: (k, 0))
    return pl.pallas_call(
        body, name=name, grid=(nk,), in_specs=[blk, blk],
        out_specs=pl.BlockSpec((N_CHIP, r4, d), lambda k: (0, 0, 0)),
        out_shape=jax.ShapeDtypeStruct((N_CHIP, r4, d), BF16),
        scratch_shapes=[pltpu.VMEM((d, d), F32)], compiler_params=_params("arbitrary"),
    )(a, b)


def _ffn_bwd1(dx2, f, g2, wd_g, gate, up):
    s, d = dx2.shape
    f4 = gate.shape[-1]
    tm = _tile(s, 1024)

    def body(dx_ref, f_ref, g_ref, w_ref, gate_ref, up_ref, dgate_ref, dup_ref, df_ref, dg_ref):
        i, j = pl.program_id(0), pl.program_id(1)

        @pl.when((i == 0) & (j == 0))
        def _():
            dg_ref[...] = jnp.zeros_like(dg_ref)

        dxv = dx_ref[...]
        df = (g_ref[...] * dxv).astype(BF16)

        @pl.when(j == 0)
        def _():
            df_ref[...] = df
            dg_ref[0:1, :] += jnp.sum(dxv * f_ref[...], axis=0, keepdims=True)

        da = _dot_nt(df, w_ref[...])
        gt = gate_ref[...].astype(F32)
        sg = jax.nn.sigmoid(gt)
        dup_ref[...] = (da * gt * sg).astype(BF16)
        dgate_ref[...] = (da * up_ref[...].astype(F32) * (sg * (1.0 + gt * (1.0 - sg)))).astype(BF16)

    row = pl.BlockSpec((tm, d), lambda i, j: (i, 0))
    hsp = pl.BlockSpec((None, tm, f4), lambda i, j: (j, i, 0))
    hshp = jax.ShapeDtypeStruct((N_CHIP, s, f4), BF16)
    return pl.pallas_call(
        body, name="ffn_bwd1", grid=(s // tm, N_CHIP),
        in_specs=[row, row, pl.BlockSpec((1, d), lambda i, j: (0, 0)),
                  pl.BlockSpec((None, f4, d), lambda i, j: (j, 0, 0)), hsp, hsp],
        out_specs=[hsp, hsp, row, pl.BlockSpec((8, d), lambda i, j: (0, 0))],
        out_shape=[hshp, hshp, jax.ShapeDtypeStruct((s, d), BF16), jax.ShapeDtypeStruct((8, d), F32)],
        compiler_params=_params("arbitrary", "arbitrary"),
    )(dx2, f, g2, wd_g, gate, up)


def _ffn_bwd2(dgate, dup, wg_g, wu_g):
    _, s, f4 = dgate.shape
    d = wg_g.shape[-2]
    tm = _tile(s, 1024)

    def body(dg_ref, du_ref, wg_ref, wu_ref, o_ref, acc):
        j = pl.program_id(1)

        @pl.when(j == 0)
        def _():
            acc[...] = jnp.zeros_like(acc)

        acc[...] += _dot_nt(dg_ref[...], wg_ref[...]) + _dot_nt(du_ref[...], wu_ref[...])

        @pl.when(j == N_CHIP - 1)
        def _():
            o_ref[...] = acc[...]

    hsp = pl.BlockSpec((None, tm, f4), lambda i, j: (j, i, 0))
    wsp = pl.BlockSpec((None, d, f4), lambda i, j: (j, 0, 0))
    return pl.pallas_call(
        body, name="ffn_bwd2", grid=(s // tm, N_CHIP), in_specs=[hsp, hsp, wsp, wsp],
        out_specs=pl.BlockSpec((tm, d), lambda i, j: (i, 0)), out_shape=jax.ShapeDtypeStruct((s, d), F32),
        scratch_shapes=[pltpu.VMEM((tm, d), F32)], compiler_params=_params("parallel", "arbitrary"),
    )(dgate, dup, wg_g, wu_g)


def _lnmod_bwd(x, g, sc, dh, dres):
    s, d = x.shape
    tm = _tile(s, 1024)

    def body(x_ref, g_ref, sc_ref, dh_ref, dr_ref, dx_ref, sums_ref):
        @pl.when(pl.program_id(0) == 0)
        def _():
            sums_ref[...] = jnp.zeros_like(sums_ref)

        xv, dhv, gv = x_ref[...], dh_ref[...], g_ref[...]
        r = lax.rsqrt(jnp.mean(xv * xv, axis=-1, keepdims=True) + EPS)
        n = xv * r
        one_sc = 1.0 + sc_ref[...]
        dt = dhv * one_sc
        sums_ref[0:1, :] += jnp.sum(dhv, axis=0, keepdims=True)
        sums_ref[1:2, :] += jnp.sum(dhv * (n * gv), axis=0, keepdims=True)
        sums_ref[2:3, :] += jnp.sum(dt * n, axis=0, keepdims=True)
        dn = dt * gv
        dx_ref[...] = dr_ref[...] + r * (dn - n * jnp.mean(dn * n, axis=-1, keepdims=True))

    vec = pl.BlockSpec((1, d), lambda i: (0, 0))
    row = pl.BlockSpec((tm, d), lambda i: (i, 0))
    return pl.pallas_call(
        body, name="lnmod_bwd", grid=(s // tm,), in_specs=[row, vec, vec, row, row],
        out_specs=[row, pl.BlockSpec((8, d), lambda i: (0, 0))],
        out_shape=[jax.ShapeDtypeStruct((s, d), F32), jax.ShapeDtypeStruct((8, d), F32)],
        compiler_params=_params("arbitrary"),
    )(x, g, sc, dh, dres)


def _out_bwd(dx1, mo, g1, wout, pa, pb, p, wa, wb, d):
    s = dx1.shape[0]
    tm = _tile(s, 256)

    def body(dx_ref, mo_ref, g_ref, wo_ref, pa_ref, pb_ref, ga_ref, gb_ref, wa_ref, wb_ref,
             dmo_ref, da_ref, db_ref, dya_ref, dyb_ref, dp_ref, dg_ref):
        @pl.when(pl.program_id(0) == 0)
        def _():
            dg_ref[...] = jnp.zeros_like(dg_ref)

        dxv = dx_ref[...]
        dg_ref[0:1, :] += jnp.sum(dxv * mo_ref[...], axis=0, keepdims=True)
        dmo = (g_ref[...] * dxv).astype(BF16)
        dmo_ref[...] = dmo
        dm = _dot_nt(dmo, wo_ref[...])
        sa, sb = jax.nn.sigmoid(ga_ref[...].astype(F32)), jax.nn.sigmoid(gb_ref[...].astype(F32))
        da = (dm * sa).astype(BF16)
        db = (dm * sb).astype(BF16)
        da_ref[...] = da
        db_ref[...] = db
        dp_ref[:, :d] = (dm * pa_ref[...].astype(F32) * (sa * (1.0 - sa))).astype(BF16)
        dp_ref[:, d:] = (dm * pb_ref[...].astype(F32) * (sb * (1.0 - sb))).astype(BF16)
        dya_ref[...] = _dot_nt(da, wa_ref[...]).astype(BF16)
        dyb_ref[...] = _dot_nt(db, wb_ref[...]).astype(BF16)

    row = pl.BlockSpec((tm, d), lambda i: (i, 0))
    wsp = pl.BlockSpec((d, d), lambda i: (0, 0))
    shp = jax.ShapeDtypeStruct((s, d), BF16)
    return pl.pallas_call(
        body, name="out_bwd", grid=(s // tm,),
        in_specs=[row, row, pl.BlockSpec((1, d), lambda i: (0, 0)), wsp, row, row,
                  pl.BlockSpec((tm, d), lambda i: (i, 6)), pl.BlockSpec((tm, d), lambda i: (i, 7)), wsp, wsp],
        out_specs=[row] * 5 + [pl.BlockSpec((tm, 2 * d), lambda i: (i, 3)), pl.BlockSpec((8, d), lambda i: (0, 0))],
        out_shape=[shp] * 5 + [jax.ShapeDtypeStruct((s, 8 * d), BF16), jax.ShapeDtypeStruct((8, d), F32)],
        compiler_params=_params("arbitrary"),
    )(dx1, mo, g1, wout, pa, pb, p, p, wa, wb)


def _store_segments(outs, dp_out, sems, col_blocks):
    copies = [pltpu.make_async_copy(outs.at[k], dp_out.at[:, pl.ds(pl.multiple_of(cb * LANES, LANES), LANES)],
                                    sems.at[k]) for k, cb in enumerate(col_blocks)]
    for cp in copies:
        cp.start()
    for cp in copies:
        cp.wait()


def _conv_bwd(p, conv_w, dyb, dp, d):
    s = p.shape[0]
    nb = d // LANES
    rows_n = _conv_rows(s)

    def compute(cb_ref, cc_ref, cx_ref, w_ref, dy_ref, dcb_ref, dcc_ref, dcx_ref, dw_ref, us, ds):
        us[pl.ds(0, 8), :] = jnp.zeros((8, LANES), F32)
        ds[pl.ds(s, 8), :] = jnp.zeros((8, LANES), F32)

        def fill(r, _):
            rows = pl.ds(pl.multiple_of(r * rows_n, rows_n), rows_n)
            us[pl.ds(pl.multiple_of(r * rows_n + 8, 8), rows_n), :] = cc_ref[rows, :].astype(F32) * cx_ref[rows, :].astype(F32)
            ds[rows, :] = dy_ref[rows, :].astype(F32) * cb_ref[rows, :].astype(F32)
            return 0

        lax.fori_loop(0, s // rows_n, fill, 0)
        w = w_ref[...]

        def out(r, carry):
            dw0, dw1, dw2 = carry
            rows = pl.ds(pl.multiple_of(r * rows_n, rows_n), rows_n)
            ext = us[pl.ds(pl.multiple_of(r * rows_n, 8), rows_n + 8), :]
            u0, u1, u2 = ext[8:, :], pltpu.roll(ext, 1, 0)[8:, :], pltpu.roll(ext, 2, 0)[8:, :]
            cv = w[0:1, :] * u2 + w[1:2, :] * u1 + w[2:3, :] * u0
            dcb_ref[rows, :] = (dy_ref[rows, :].astype(F32) * cv).astype(BF16)
            nxt = ds[pl.ds(pl.multiple_of(r * rows_n, 8), rows_n + 8), :]
            e0 = nxt[:rows_n, :]
            e1 = pltpu.roll(nxt, rows_n + 7, 0)[:rows_n, :]
            e2 = pltpu.roll(nxt, rows_n + 6, 0)[:rows_n, :]
            du = w[2:3, :] * e0 + w[1:2, :] * e1 + w[0:1, :] * e2
            dcc_ref[rows, :] = (du * cx_ref[rows, :].astype(F32)).astype(BF16)
            dcx_ref[rows, :] = (du * cc_ref[rows, :].astype(F32)).astype(BF16)
            return (dw0 + jnp.sum(e0 * u2, axis=0, keepdims=True), dw1 + jnp.sum(e0 * u1, axis=0, keepdims=True),
                    dw2 + jnp.sum(e0 * u0, axis=0, keepdims=True))

        zero = jnp.zeros((1, LANES), F32)
        dw0, dw1, dw2 = lax.fori_loop(0, s // rows_n, out, (zero, zero, zero))
        dw_ref[...] = jnp.zeros_like(dw_ref)
        dw_ref[0:1, :] = dw0
        dw_ref[1:2, :] = dw1
        dw_ref[2:3, :] = dw2

    def body(cb_ref, cc_ref, cx_ref, w_ref, dy_ref, dp_in, dp_out, dw_ref, us, ds, outs, sems):
        del dp_in
        compute(cb_ref, cc_ref, cx_ref, w_ref, dy_ref, outs.at[0], outs.at[1], outs.at[2], dw_ref, us, ds)
        _store_segments(outs, dp_out, sems, [(3 + k) * nb + pl.program_id(0) for k in range(3)])

    def seg(k):
        return pl.BlockSpec((s, LANES), lambda b, k=k: (0, k * nb + b))

    return pl.pallas_call(
        body, name="conv_bwd", grid=(nb,),
        in_specs=[seg(3), seg(4), seg(5), pl.BlockSpec((3, LANES), lambda b: (0, b)),
                  pl.BlockSpec((s, LANES), lambda b: (0, b)), ANY_SPEC],
        out_specs=[ANY_SPEC, pl.BlockSpec((8, LANES), lambda b: (0, b))],
        out_shape=[jax.ShapeDtypeStruct(dp.shape, BF16), jax.ShapeDtypeStruct((8, d), F32)],
        input_output_aliases={5: 0},
        scratch_shapes=[pltpu.VMEM((s + 8, LANES), F32), pltpu.VMEM((s + 8, LANES), F32),
                        pltpu.VMEM((3, s, LANES), BF16), pltpu.SemaphoreType.DMA((3,))],
        compiler_params=_params("arbitrary"),
    )(p, p, p, conv_w, dyb, dp)


def _attn_bwd(p, qg2, kg2, dy, lt, dp, d):
    s = p.shape[0]
    n_pairs = d // LANES
    qsb = _tile(s, Q_SUPER_BWD)
    n_sub, n_sb, n_kb = qsb // Q_BLOCK, s // qsb, s // Q_BLOCK
    unroll = math.gcd(KEY_UNROLL, n_sub)
    chunk = _tile(s, 512)
    inv_sqrt = 1.0 / math.sqrt(HEAD_DIM)

    def compute(q_ref, k_ref, v_ref, qg_ref, kg_ref, dy_ref, lt_ref, dq_ref, dk_ref, dv_ref, dgain_ref,
                qs, k2, v2, dkt, dvt, qt, dyt, rem, gbef, dqa):
        low, causal, w4 = _attn_consts(True)

        def prep(r, _):
            rows = pl.ds(pl.multiple_of(r * chunk, chunk), chunk)
            qs[rows, :] = (_pair_norm(q_ref[rows, :].astype(F32), low)[0] * (qg_ref[...] * inv_sqrt)).astype(BF16)
            return 0

        lax.fori_loop(0, s // chunk, prep, 0)
        _fill_pair_blocks(k2, lambda rows: _pair_norm(k_ref[rows, :].astype(F32), low)[0] * kg_ref[...], low, n_kb)
        _fill_pair_blocks(v2, lambda rows: v_ref[rows, :].astype(F32), low, n_kb)

        def clear(b, _):
            dkt[b] = jnp.zeros((LANES, Q_BLOCK), F32)
            dvt[b] = jnp.zeros((LANES, Q_BLOCK), F32)
            return 0

        lax.fori_loop(0, n_kb, clear, 0)

        def step(sb, j, t0=0, diag_t=None):
            rows = pl.ds(pl.multiple_of(sb * qsb + t0 * Q_BLOCK, Q_BLOCK), (n_sub - t0) * Q_BLOCK)
            kj2, vj2 = k2[j], v2[j]
            z_both = _dot_nt(qs[rows, :], kj2)
            da_both = _dot_nt(dy_ref[rows, :], vj2)
            zls, cats = [], []
            for t in range(t0, n_sub):
                sub = slice((t - t0) * Q_BLOCK, (t - t0 + 1) * Q_BLOCK)
                for h in range(2):
                    z = z_both[sub, h * LANES:(h + 1) * LANES]
                    ln = _log_not(z)
                    if t == diag_t:
                        ln = jnp.where(causal, ln, 0.0)
                    zls.append(z + ln)
                    cats.append(_split_cat(ln))
            c2 = _dot(jnp.concatenate(cats, axis=0), w4)
            a_rows, gs, cats = [], [], []
            for t in range(t0, n_sub):
                sub = slice(t * Q_BLOCK, (t + 1) * Q_BLOCK)
                a_pair = []
                for h in range(2):
                    i = 2 * (t - t0) + h
                    tile = slice(i * Q_BLOCK, (i + 1) * Q_BLOCK)
                    left = rem[h, sub, :]
                    log_a = zls[i] + (left - c2[tile, :LANES])
                    if t == diag_t:
                        log_a = jnp.where(causal, log_a, -1e30)
                    a = jnp.exp(log_a)
                    rem[h, sub, :] = left - c2[tile, LANES:]
                    g = a * da_both[(t - t0) * Q_BLOCK:(t - t0 + 1) * Q_BLOCK, h * LANES:(h + 1) * LANES]
                    a_pair.append(a.astype(BF16))
                    gs.append(g)
                    cats.append(g.astype(BF16))
                a_rows.append(jnp.concatenate(a_pair, axis=1))
            c2g = _dot(jnp.concatenate(cats, axis=0), w4[:Q_BLOCK, :])
            dz_rows = []
            for t in range(t0, n_sub):
                sub = slice(t * Q_BLOCK, (t + 1) * Q_BLOCK)
                dz_pair = []
                for h in range(2):
                    i = 2 * (t - t0) + h
                    tile = slice(i * Q_BLOCK, (i + 1) * Q_BLOCK)
                    before = gbef[h, sub, :]
                    dz = gs[i] - jnp.exp(zls[i]) * (before + c2g[tile, :LANES])
                    if t == diag_t:
                        dz = jnp.where(causal, dz, 0.0)
                    gbef[h, sub, :] = before + c2g[tile, LANES:]
                    dz_pair.append(dz.astype(BF16))
                dz_rows.append(jnp.concatenate(dz_pair, axis=1))
            a_both = jnp.concatenate(a_rows, axis=0)
            dz_both = jnp.concatenate(dz_rows, axis=0)
            used = slice(t0 * Q_BLOCK, qsb)
            dvt[j] += _dot(dyt[0, :, used], a_both[:, :LANES]) + _dot(dyt[1, :, used], a_both[:, LANES:])
            dkt[j] += _dot(qt[0, :, used], dz_both[:, :LANES]) + _dot(qt[1, :, used], dz_both[:, LANES:])
            dqa[used, :] += _dot(dz_both, kj2)

        def super_block(sb, dqg):
            rows_sb = pl.ds(pl.multiple_of(sb * qsb, qsb), qsb)
            total = lt_ref[rows_sb, :]
            other = pltpu.roll(total, HEAD_DIM, 1)
            rem[0] = jnp.where(low, total, other)
            rem[1] = jnp.where(low, other, total)
            gbef[...] = jnp.zeros_like(gbef)
            dqa[...] = jnp.zeros_like(dqa)
            qv = qs[rows_sb, :].astype(F32)
            dyv = dy_ref[rows_sb, :].astype(F32)
            qt[0] = jnp.where(low, qv, 0.0).T.astype(BF16)
            qt[1] = jnp.where(low, 0.0, qv).T.astype(BF16)
            dyt[0] = jnp.where(low, dyv, 0.0).T.astype(BF16)
            dyt[1] = jnp.where(low, 0.0, dyv).T.astype(BF16)

            def below(n, _):
                for u in range(unroll):
                    step(sb, unroll * n + u)
                return 0

            lax.fori_loop(0, sb * (n_sub // unroll), below, 0)
            for t in range(n_sub):
                step(sb, sb * n_sub + t, t0=t, diag_t=t)
            qhat, r = _pair_norm(q_ref[rows_sb, :].astype(F32), low)
            dqn = dqa[...]
            dqhat = dqn * (qg_ref[...] * inv_sqrt)
            dq_ref[rows_sb, :] = (r * (dqhat - qhat * _pair_mean(dqhat * qhat, low))).astype(BF16)
            return dqg + jnp.sum(dqn * qhat, axis=0, keepdims=True) * inv_sqrt

        dqg = lax.fori_loop(0, n_sb, super_block, jnp.zeros((1, LANES), F32))

        def finish(b, dkg):
            rows = pl.ds(pl.multiple_of(b * Q_BLOCK, Q_BLOCK), Q_BLOCK)
            khat, rk = _pair_norm(k_ref[rows, :].astype(F32), low)
            dkn = dkt[b].T
            dkhat = dkn * kg_ref[...]
            dk_ref[rows, :] = (rk * (dkhat - khat * _pair_mean(dkhat * khat, low))).astype(BF16)
            dv_ref[rows, :] = dvt[b].T.astype(BF16)
            return dkg + jnp.sum(dkn * khat, axis=0, keepdims=True)

        dkg = lax.fori_loop(0, n_kb, finish, jnp.zeros((1, LANES), F32))
        dgain_ref[...] = jnp.zeros_like(dgain_ref)
        dgain_ref[0:1, :] = dqg
        dgain_ref[1:2, :] = dkg

    def body(q_ref, k_ref, v_ref, qg_ref, kg_ref, dy_ref, lt_ref, dp_in, dp_out, dgain_ref, outs, sems, *scratch):
        del dp_in
        compute(q_ref, k_ref, v_ref, qg_ref, kg_ref, dy_ref, lt_ref, outs.at[0], outs.at[1], outs.at[2], dgain_ref,
                *scratch)
        _store_segments(outs, dp_out, sems, [k * n_pairs + pl.program_id(0) for k in range(3)])

    def seg(k):
        return pl.BlockSpec((s, LANES), lambda h, k=k: (0, k * n_pairs + h))

    vec = pl.BlockSpec((1, LANES), lambda h: (0, 0))
    col = pl.BlockSpec((s, LANES), lambda h: (0, h))
    return pl.pallas_call(
        body, name="attn_bwd", grid=(n_pairs,),
        in_specs=[seg(0), seg(1), seg(2), vec, vec, col, col, ANY_SPEC],
        out_specs=[ANY_SPEC, pl.BlockSpec((None, 8, LANES), lambda h: (h, 0, 0))],
        out_shape=[jax.ShapeDtypeStruct(dp.shape, BF16), jax.ShapeDtypeStruct((n_pairs, 8, LANES), F32)],
        input_output_aliases={7: 0},
        scratch_shapes=[pltpu.VMEM((3, s, LANES), BF16), pltpu.SemaphoreType.DMA((3,)), pltpu.VMEM((s, LANES), BF16)]
        + [pltpu.VMEM((n_kb, 2 * Q_BLOCK, LANES), BF16)] * 2
        + [pltpu.VMEM((n_kb, LANES, Q_BLOCK), F32)] * 2
        + [pltpu.VMEM((2, LANES, qsb), BF16)] * 2
        + [pltpu.VMEM((2, qsb, LANES), F32)] * 2 + [pltpu.VMEM((qsb, LANES), F32)],
        compiler_params=_params("arbitrary"),
    )(p, p, p, qg2, kg2, dy, lt, dp)


def _mm_in_bwd(dp, w_g):
    s = dp.shape[0]
    d, n4 = w_g.shape[-2:]
    tm = _tile(s, 1024)

    def body(a_ref, w_ref, o_ref, acc):
        j = pl.program_id(1)

        @pl.when(j == 0)
        def _():
            acc[...] = jnp.zeros_like(acc)

        acc[...] += _dot_nt(a_ref[...], w_ref[...])

        @pl.when(j == N_CHIP - 1)
        def _():
            o_ref[...] = acc[...]

    return pl.pallas_call(
        body, name="mm_in_bwd", grid=(s // tm, N_CHIP),
        in_specs=[pl.BlockSpec((tm, n4), lambda i, j: (i, j)),
                  pl.BlockSpec((None, d, n4), lambda i, j: (j, 0, 0))],
        out_specs=pl.BlockSpec((tm, d), lambda i, j: (i, 0)), out_shape=jax.ShapeDtypeStruct((s, d), F32),
        scratch_shapes=[pltpu.VMEM((tm, d), F32)], compiler_params=_params("parallel", "arbitrary"),
    )(dp, w_g)


def _sum_adam(parts, w, m, v, name):
    n_l, r, c = w.shape
    tr = next((t for t in (256, 176, 128, 64, 32, 16) if r % t == 0 and t * c <= 256 * 1024), r)
    n_blk = r // tr

    def body(*refs):
        p_refs = refs[:n_l]
        w_ref, m_ref, v_ref, g_ref, dl_ref, nm_ref, nv_ref = refs[n_l:]
        for l in range(n_l):
            @pl.when(pl.program_id(0) == l)
            def _(p_ref=p_refs[l]):
                g = p_ref[0].astype(F32)
                for dev in range(1, N_DEV):
                    g = g + p_ref[dev].astype(F32)
                g_ref[...] = g
                delta, nm, nv = _adamw(w_ref[...], g, m_ref[...], v_ref[...])
                dl_ref[...] = delta
                nm_ref[...] = nm
                nv_ref[...] = nv

    def part_spec(l):
        return pl.BlockSpec((N_DEV, tr, c), lambda ll, i, l=l: (0, jnp.where(ll == l, i, jnp.where(ll < l, 0, n_blk - 1)), 0))

    wsp = pl.BlockSpec((None, tr, c), lambda l, i: (l, i, 0))
    shp = jax.ShapeDtypeStruct(w.shape, F32)
    return pl.pallas_call(
        body, name=name, grid=(n_l, n_blk),
        in_specs=[part_spec(l) for l in range(n_l)] + [wsp, wsp, wsp],
        out_specs=[wsp] * 4, out_shape=[shp] * 4, compiler_params=_params("arbitrary", "arbitrary"),
    )(*parts, w, m, v)


def _small_adam(parts, w, m, v):
    def body(p_ref, w_ref, m_ref, v_ref, g_ref, dl_ref, nm_ref, nv_ref):
        g = p_ref[0]
        for dev in range(1, N_DEV):
            g = g + p_ref[dev]
        g_ref[...] = g
        delta, nm, nv = _adamw(w_ref[...], g, m_ref[...], v_ref[...])
        dl_ref[...] = delta
        nm_ref[...] = nm
        nv_ref[...] = nv

    shp = jax.ShapeDtypeStruct(w.shape, F32)
    return pl.pallas_call(body, name="small_adam", in_specs=[VMEM_SPEC] * 4, out_specs=[VMEM_SPEC] * 4,
                          out_shape=[shp] * 4,
                          compiler_params=pltpu.CompilerParams(vmem_limit_bytes=VMEM_LIMIT_BYTES))(parts, w, m, v)


def _pack(vecs, mult=8 * LANES):
    flat = jnp.concatenate([a.reshape(-1).astype(F32) for a in vecs])
    pad = (-flat.shape[0]) % mult
    if pad:
        flat = jnp.concatenate([flat, jnp.zeros((pad,), F32)])
    return flat.reshape(8, -1)


def _unpack(flat, shapes):
    flat = flat.reshape(-1)
    out, off = [], 0
    for shp in shapes:
        n = math.prod(shp)
        out.append(flat[off:off + n].reshape(shp))
        off += n
    return out


BIG = ("win", "wa", "wb", "wo", "wg", "wu", "wd")
GRAD_GROUPS = (("wd", "wg", "wu"), ("wo", "wa", "wb"), ("win",))


def _local_step(x, target, mods, ln1_g, ln2_g, qg, kg, conv_w, weights, send_grads):
    s, d = x.shape
    n_l = mods.shape[0]
    saved = []
    h_in = x
    for l in range(n_l):
        sh1, sc1, g1, sh2, sc2, g2 = [mods[l, k * d:(k + 1) * d].reshape(1, d) for k in range(6)]
        qg2, kg2 = jnp.tile(qg[l:l + 1], (1, 2)), jnp.tile(kg[l:l + 1], (1, 2))
        h1 = _lnmod(h_in, ln1_g[l:l + 1], sc1, sh1)
        (win,), tie = weights(l, ("win",), h1)
        p = _mm_in(h1, win)
        ya, lt = _attn_fwd(p, qg2 + tie, kg2, d)
        yb = _conv_fwd(p, conv_w[l], d)
        (wa, wb, wo, wg, wu, wd), tie = weights(l, ("wa", "wb", "wo", "wg", "wu", "wd"), ya)
        wa, wb, wo = wa.reshape(d, d), wb.reshape(d, d), wo.reshape(d, d)
        merged, pa, pb = _branch(ya, yb, p, wa, wb, d)
        x1, mo, h2 = _out_proj(merged, wo, h_in, g1 + tie, ln2_g[l:l + 1], sc2, sh2)
        gate, up, act = _ffn_up(h2, wg, wu)
        x2, f = _ffn_down(act, wd, x1, g2)
        saved.append(dict(x0=h_in, h1=h1, p=p, ya=ya, lt=lt, yb=yb, merged=merged, pa=pa, pb=pb, x1=x1, mo=mo,
                          h2=h2, gate=gate, up=up, act=act, f=f, win=win, wa=wa, wb=wb, wo=wo, wg=wg, wu=wu, wd=wd,
                          mod=(sh1, sc1, g1, sh2, sc2, g2), qg2=qg2, kg2=kg2))
        h_in = x2

    dx, loss_tile = _loss_head(h_in, target)

    small = [None] * n_l
    for l in reversed(range(n_l)):
        sv = saved[l]
        sh1, sc1, g1, sh2, sc2, g2 = sv["mod"]
        f4, n4 = sv["wg"].shape[-1], sv["win"].shape[-1]
        hsp = lambda tk: ((tk, d), lambda j, k: (k, 0))
        fsp = lambda tk: ((None, tk, f4), lambda j, k: (j, k, 0))
        dgate, dup, df, dg2 = _ffn_bwd1(dx, sv["f"], g2, sv["wd"], sv["gate"], sv["up"])
        g_wd = _mm_tn(sv["act"], df, fsp, hsp, (f4, d), "grad_wd")
        g_wg = _mm_tn(dgate, sv["h2"], fsp, hsp, (f4, d), "grad_wg")
        g_wu = _mm_tn(dup, sv["h2"], fsp, hsp, (f4, d), "grad_wu")
        tie = send_grads(l, dict(wd=g_wd, wg=g_wg, wu=g_wu))
        dh2 = _ffn_bwd2(dgate, dup, sv["wg"], sv["wu"])
        dx1, sums2 = _lnmod_bwd(sv["x1"], ln2_g[l:l + 1], sc2 + tie, dh2, dx)
        dmo, da, db, dya, dyb, dp, dg1 = _out_bwd(dx1, sv["mo"], g1, sv["wo"], sv["pa"], sv["pb"], sv["p"],
                                                        sv["wa"], sv["wb"], d)
        g_wo = _mm_tn_square(sv["merged"], dmo, "grad_wo")
        g_wa = _mm_tn_square(sv["ya"], da, "grad_wa")
        g_wb = _mm_tn_square(sv["yb"], db, "grad_wb")
        tie = send_grads(l, dict(wo=g_wo, wa=g_wa, wb=g_wb))
        dp, dconv = _conv_bwd(sv["p"], conv_w[l] + tie, dyb, dp, d)
        dp, dgain = _attn_bwd(sv["p"], sv["qg2"], sv["kg2"], dya, sv["lt"], dp, d)
        g_win = _mm_tn(sv["h1"], dp, hsp, lambda tk: ((tk, n4), lambda j, k: (k, j)), (d, n4), "grad_win")
        tie = send_grads(l, dict(win=g_win))
        dh1 = _mm_in_bwd(dp, sv["win"])
        dx, sums1 = _lnmod_bwd(sv["x0"], ln1_g[l:l + 1], sc1 + tie, dh1, dx1)
        dgain = jnp.sum(dgain[:, 0:2, :], axis=0)
        dgain = dgain[:, :HEAD_DIM] + dgain[:, HEAD_DIM:]
        dmod = jnp.concatenate([sums1[0], sums1[1], dg1[0], sums2[0], sums2[1], dg2[0]])
        small[l] = dict(dmod=dmod, ln1=sums1[2], ln2=sums2[2], qg=dgain[0], kg=dgain[1], conv=dconv[0:3])
    return loss_tile, dx, small


def kernel(x, c, ada_w, ada_b, ln1_g, w_in, q_norm_g, k_norm_g, conv_w, w_branch_a, w_branch_b, w_out, ln2_g, w_ffn_gate, w_ffn_up, w_ffn_down, loss_target, m_ada_w, m_ada_b, m_ln1_g, m_w_in, m_q_norm_g, m_k_norm_g, m_conv_w, m_w_branch_a, m_w_branch_b, m_w_out, m_ln2_g, m_w_ffn_gate, m_w_ffn_up, m_w_ffn_down, v_ada_w, v_ada_b, v_ln1_g, v_w_in, v_q_norm_g, v_k_norm_g, v_conv_w, v_w_branch_a, v_w_branch_b, v_w_out, v_ln2_g, v_w_ffn_gate, v_w_ffn_up, v_w_ffn_down):
    n_l, d, a4 = ada_w.shape
    cw4 = conv_w.shape[-1]
    ix, iy, ic = lax.axis_index("x"), lax.axis_index("y"), lax.axis_index("c")
    chip = 2 * ix + iy
    me = 2 * chip + ic

    big_w = dict(win=w_in, wa=w_branch_a, wb=w_branch_b, wo=w_out, wg=w_ffn_gate, wu=w_ffn_up, wd=w_ffn_down)
    big_m = dict(win=m_w_in, wa=m_w_branch_a, wb=m_w_branch_b, wo=m_w_out, wg=m_w_ffn_gate, wu=m_w_ffn_up,
                 wd=m_w_ffn_down)
    big_v = dict(win=v_w_in, wa=v_w_branch_a, wb=v_w_branch_b, wo=v_w_out, wg=v_w_ffn_gate, wu=v_w_ffn_up,
                 wd=v_w_ffn_down)

    def adam_view(a, k):
        return jnp.swapaxes(a, 1, 2) if k in ("wg", "wu") else a

    got = _gather8(_pack([c, conv_w])).reshape(N_DEV, -1)

    weight_groups = [(l, names) for l in range(n_l) for names in (("win",), ("wa", "wb", "wo", "wg", "wu", "wd"))]
    group_srcs = [[big_w[k][l].astype(BF16) for k in names] for l, names in weight_groups]
    started_w = {}

    def start_weights(gi):
        l, names = weight_groups[gi]
        copies = _weight_half_copies if gi == 0 else _weight_copies
        st = _split_start("weights_start_%d" % gi, copies, group_srcs[gi],
                          [(N_CHIP,) + sh.shape for sh in group_srcs[gi]], 3)
        for k in names:
            started_w[(l, k)] = [gi, names, st, None, copies]
        return st[4]

    got, group_srcs[0] = lax.optimization_barrier((got, group_srcs[0]))
    tie = start_weights(0)[0, 0]
    c_all = got[:, :d]
    conv_all = got[:, d:d + n_l * 3 * cw4].reshape(N_CHIP, 2, n_l, 3, cw4)[:, 0]
    conv_full = jnp.transpose(conv_all, (1, 2, 0, 3)).reshape(n_l, 3, N_CHIP * cw4)
    b_cols = lax.dynamic_slice_in_dim(ada_b, chip * a4, a4, axis=1).reshape(n_l, 1, a4)
    b_cols, group_srcs[1:] = lax.optimization_barrier((b_cols + tie, group_srcs[1:]))
    mod_cols = _ada_mod(c_all, ada_w, b_cols)
    mod_all = _gather8(_pack([mod_cols])).reshape(N_DEV, -1)[:, :n_l * N_DEV * a4]
    mod_all = mod_all.reshape(N_CHIP, 2, n_l, N_DEV, a4)[:, 0]
    mods = lax.dynamic_index_in_dim(mod_all, me, axis=2, keepdims=False)
    mods = jnp.transpose(mods, (1, 0, 2)).reshape(n_l, N_CHIP * a4)

    def weights(l, names, after):
        entry, tie = started_w[(l, names[0])], jnp.zeros((), F32)
        if entry[3] is None:
            lands = _split_wait("weights_wait_%d" % entry[0], entry[4], entry[2], after)
            if entry[4] is _weight_half_copies:
                passed = _split_start_in_place("weights_pass_start_%d" % entry[0], _weight_half_pass, lands, 3)
                lands = _split_wait_in_place("weights_pass_wait_%d" % entry[0], _weight_half_pass, passed, passed[3])
            nxt = entry[0] + 1
            if nxt < len(weight_groups):
                lands, group_srcs[nxt] = lax.optimization_barrier((lands, group_srcs[nxt]))
                tie = start_weights(nxt)[0, 0]
            lands = [lax.dynamic_update_index_in_dim(land, own, chip, 0) for land, own in zip(lands, entry[2][2])]
            for k in entry[1]:
                started_w[(l, k)][3] = dict(zip(entry[1], lands))
        return [started_w[(l, k)][3][k] for k in names], tie

    started_g, held_back = [], []

    def start_grads(l, grads, copies=_grad_copies, sems_per=7):
        names = tuple(grads)
        st = _split_start("grads_start_%d" % len(started_g), copies, [grads[k] for k in names],
                          [(N_DEV,) + grads[k].shape[1:] for k in names], sems_per)
        started_g.append((l, names, st, copies))
        return st[4][0, 0]

    def send_grads(l, grads):
        if l == 0 and tuple(grads) == GRAD_GROUPS[-1]:
            held_back.append(grads)
            return jnp.zeros((), F32)
        return start_grads(l, grads)

    loss_tile, grad_x, small = _local_step(
        x[0], loss_target[0], mods, ln1_g, ln2_g, q_norm_g, k_norm_g, conv_full, weights, send_grads)

    sm_shapes = [(n_l, 6 * d), (n_l, d), (n_l, d), (n_l, HEAD_DIM), (n_l, HEAD_DIM), (n_l, 3, d), (1,)]
    vec = _pack([jnp.stack([small[l][k] for l in range(n_l)]) for k in ("dmod", "ln1", "ln2", "qg", "kg", "conv")]
                + [loss_tile[0, 0:1]])
    n_vec = vec.shape[1] * 8
    all_vec = _gather8(vec).reshape(N_DEV, n_vec)
    all_vec, held_back = lax.optimization_barrier((all_vec, held_back))
    tie = sum([start_grads(0, grads, _grad_copies_same_core, 4) for grads in held_back], jnp.zeros((), F32))
    per_dev = [_unpack(all_vec[dev], sm_shapes) for dev in range(N_DEV)]
    dmod_all = jnp.stack([pd[0] for pd in per_dev])
    dmod_cols = jnp.transpose(lax.dynamic_slice_in_dim(dmod_all, chip * a4, a4, axis=2), (1, 0, 2))
    ada_out = _ada_grad_adam(jnp.transpose(c_all) + tie, dmod_cols, ada_w, m_ada_w, v_ada_w)

    dev_parts = jnp.stack([
        _pack([pd[0], pd[1], pd[2], pd[3], pd[4], lax.dynamic_slice_in_dim(pd[5], chip * cw4, cw4, axis=2), pd[6]])
        for pd in per_dev])
    zero1 = jnp.zeros((1,), F32)
    sw = _pack([ada_b, ln1_g, ln2_g, q_norm_g, k_norm_g, conv_w, zero1])
    sm = _pack([m_ada_b, m_ln1_g, m_ln2_g, m_q_norm_g, m_k_norm_g, m_conv_w, zero1])
    sv = _pack([v_ada_b, v_ln1_g, v_ln2_g, v_q_norm_g, v_k_norm_g, v_conv_w, zero1 + 1.0])
    out_shapes = [(n_l, 6 * d), (n_l, d), (n_l, d), (n_l, HEAD_DIM), (n_l, HEAD_DIM), (n_l, 3, cw4), (1,)]
    sm_out = [_unpack(o, out_shapes) for o in _small_adam(dev_parts, sw, sm, sv)]
    loss = 0.5 * sm_out[0][6][0] / d

    after = jnp.full((8, LANES), tie + sm_out[0][0][0, 0] + ada_out[0][0, 0, 0])
    big_out = {}
    for names in GRAD_GROUPS:
        got_parts = {}
        for gi, (l, sent, st, copies) in enumerate(started_g):
            if sent == names:
                parts = _split_wait("grads_wait_%d" % gi, copies, st, after)
                if copies is _grad_copies_same_core:
                    passed = _split_start_in_place("grads_pass_start_%d" % gi, _grad_pass_copies, parts, 3)
                    parts = _split_wait_in_place("grads_pass_wait_%d" % gi, _grad_pass_copies, passed, passed[3])
                for k, part, grad in zip(sent, parts, st[2]):
                    own = lax.dynamic_index_in_dim(grad, chip, 0, keepdims=False)
                    got_parts[(l, k)] = lax.dynamic_update_index_in_dim(part, own, me, 0)
        for k in names:
            res = _sum_adam([got_parts[(l, k)] for l in range(n_l)], adam_view(big_w[k], k), adam_view(big_m[k], k),
                            adam_view(big_v[k], k), "sum_adam_" + k)
            after = res[0]
            big_out[k] = [adam_view(r, k) for r in res]

    outs = [loss, grad_x[None]]
    for kind in range(4):
        sm_k = sm_out[kind]
        outs += [ada_out[kind], sm_k[0], sm_k[1], big_out["win"][kind], sm_k[3], sm_k[4], sm_k[5],
                 big_out["wa"][kind], big_out["wb"][kind], big_out["wo"][kind], sm_k[2],
                 big_out["wg"][kind], big_out["wu"][kind], big_out["wd"][kind]]
    return tuple(outs)
```

```python
import math

import jax
import jax.numpy as jnp
from jax import lax
from jax.experimental import pallas as pl
from jax.experimental.pallas import tpu as pltpu

F32 = jnp.float32
BF16 = jnp.bfloat16
MESH_ID = pl.DeviceIdType.MESH

EPS = 1e-6
HEAD_DIM = 64
Q_BLOCK = 128
Q_SUPER = 1024
Q_SUPER_BWD = 1024
KEY_UNROLL = 4
LANES = 128
N_DEV = 8
N_CHIP = 4
VMEM_LIMIT_BYTES = 56 * 1024 * 1024

ADAM_LR = 0.001
ADAM_B1 = 0.9
ADAM_B2 = 0.999
ADAM_EPS = 1e-08
ADAM_WD = 0.01
ADAM_STEP = 10

HBM_SPEC = pl.BlockSpec(memory_space=pltpu.HBM)
ANY_SPEC = pl.BlockSpec(memory_space=pl.ANY)
SEM_SPEC = pl.BlockSpec(memory_space=pltpu.SEMAPHORE)
VMEM_SPEC = pl.BlockSpec(memory_space=pltpu.VMEM)
SIDE_EFFECT = pltpu.SideEffectType.DATAFLOW_SIDE_EFFECTING


def _params(*sem):
    return pltpu.CompilerParams(dimension_semantics=tuple(sem), vmem_limit_bytes=VMEM_LIMIT_BYTES)


def _tile(n, pref):
    return pref if n % pref == 0 else n


def _dot(a, b):
    return jnp.dot(a, b, preferred_element_type=F32)


def _dot_nt(a, b):
    return lax.dot_general(a, b, (((1,), (1,)), ((), ())), preferred_element_type=F32)


def _dot_tn(a, b):
    return lax.dot_general(a, b, (((0,), (0,)), ((), ())), preferred_element_type=F32)


def _adamw(w, g, m, v):
    m = ADAM_B1 * m + (1.0 - ADAM_B1) * g
    v = ADAM_B2 * v + (1.0 - ADAM_B2) * (g * g)
    m_hat = m / (1.0 - ADAM_B1 ** ADAM_STEP)
    v_hat = v / (1.0 - ADAM_B2 ** ADAM_STEP)
    delta = -ADAM_LR * (m_hat / (jnp.sqrt(v_hat) + ADAM_EPS) + ADAM_WD * w)
    return delta, m, v


def _hbm(a):
    return pltpu.with_memory_space_constraint(a, pltpu.HBM)


def _peer(x, y, c, k):
    return (1 - x if k & 4 else x, 1 - y if k & 2 else y, 1 - c if k & 1 else c)


def _gather8(v):
    rows_per, m = v.shape

    def body(v_ref, out_ref, send_sems, recv_sems, local_sem):
        x, y, c = lax.axis_index("x"), lax.axis_index("y"), lax.axis_index("c")

        def rows(p):
            return out_ref.at[pl.ds((4 * p[0] + 2 * p[1] + p[2]) * rows_per, rows_per), :]

        me = (x, y, c)
        mine = pltpu.make_async_copy(v_ref, rows(me), local_sem)
        mine.start()
        sends = []
        for k in range(1, N_DEV):
            cp = pltpu.make_async_remote_copy(
                src_ref=v_ref, dst_ref=rows(me), send_sem=send_sems.at[k - 1], recv_sem=recv_sems.at[k - 1],
                device_id=_peer(x, y, c, k), device_id_type=MESH_ID)
            cp.start()
            sends.append(cp)
        for k in range(1, N_DEV):
            pltpu.make_async_remote_copy(
                src_ref=v_ref, dst_ref=rows(_peer(x, y, c, k)), send_sem=send_sems.at[k - 1],
                recv_sem=recv_sems.at[k - 1], device_id=_peer(x, y, c, k), device_id_type=MESH_ID).wait_recv()
        for cp in sends:
            cp.wait_send()
        mine.wait()

    return pl.pallas_call(
        body, name="gather8",
        out_shape=jax.ShapeDtypeStruct((N_DEV * rows_per, m), v.dtype),
        in_specs=[VMEM_SPEC], out_specs=VMEM_SPEC,
        scratch_shapes=[pltpu.SemaphoreType.DMA((N_DEV - 1,)), pltpu.SemaphoreType.DMA((N_DEV - 1,)),
                        pltpu.SemaphoreType.DMA],
    )(v)


def _weight_copies(srcs, lands, send_sems, recv_sems):
    x, y, c = lax.axis_index("x"), lax.axis_index("y"), lax.axis_index("c")
    chips = [(1 - x, y), (x, 1 - y), (1 - x, 1 - y)]
    sends, recvs = [], []
    for a, (src, land) in enumerate(zip(srcs, lands)):
        for j, (px, py) in enumerate(chips):
            def copy(dst_block, a=a, j=j, px=px, py=py, src=src, land=land):
                return pltpu.make_async_remote_copy(
                    src_ref=src, dst_ref=land.at[dst_block], send_sem=send_sems.at[3 * a + j],
                    recv_sem=recv_sems.at[3 * a + j], device_id=(px, py, c), device_id_type=MESH_ID)
            sends.append(copy(2 * x + y))
            recvs.append(copy(2 * px + py))
    return sends, recvs


def _weight_half_copies(srcs, lands, send_sems, recv_sems):
    x, y, c = lax.axis_index("x"), lax.axis_index("y"), lax.axis_index("c")
    chips = [(1 - x, y), (x, 1 - y), (1 - x, 1 - y)]
    sends, recvs = [], []
    for a, (src, land) in enumerate(zip(srcs, lands)):
        half = src.shape[0] // 2
        rows = pl.ds(c * half, half)
        for j, (px, py) in enumerate(chips):
            def copy(dst_block, a=a, j=j, px=px, py=py, src=src, land=land, rows=rows):
                return pltpu.make_async_remote_copy(
                    src_ref=src.at[rows], dst_ref=land.at[dst_block, rows], send_sem=send_sems.at[3 * a + j],
                    recv_sem=recv_sems.at[3 * a + j], device_id=(px, py, c), device_id_type=MESH_ID)
            sends.append(copy(2 * x + y))
            recvs.append(copy(2 * px + py))
    return sends, recvs


def _weight_half_pass(lands, same_lands, send_sems, recv_sems):
    del same_lands
    x, y, c = lax.axis_index("x"), lax.axis_index("y"), lax.axis_index("c")
    chips = [(1 - x, y), (x, 1 - y), (1 - x, 1 - y)]
    sends, recvs = [], []
    for a, land in enumerate(lands):
        half = land.shape[1] // 2
        for j, (px, py) in enumerate(chips):
            def copy(pc, a=a, j=j, px=px, py=py, land=land, half=half):
                part = land.at[2 * px + py, pl.ds(pc * half, half)]
                return pltpu.make_async_remote_copy(
                    src_ref=part, dst_ref=part, send_sem=send_sems.at[3 * a + j], recv_sem=recv_sems.at[3 * a + j],
                    device_id=(x, y, 1 - c), device_id_type=MESH_ID)
            sends.append(copy(c))
            recvs.append(copy(1 - c))
    return sends, recvs


def _split_start(name, copies, srcs, land_shapes, sems_per_src):
    n = len(srcs)

    def body(*refs):
        sends, _ = copies(refs[:n], refs[n + 2:2 * n + 2], refs[n], refs[n + 1])
        for cp in sends:
            cp.start()
        token = refs[-1]
        token[...] = jnp.zeros_like(token)

    n_sems = sems_per_src * n
    outs = pl.pallas_call(
        body, name=name,
        out_shape=(pltpu.SemaphoreType.DMA((n_sems,)), pltpu.SemaphoreType.DMA((n_sems,)),
                   *[pltpu.HBM(shape, a.dtype) for a, shape in zip(srcs, land_shapes)],
                   jax.ShapeDtypeStruct((8, LANES), F32)),
        in_specs=[HBM_SPEC] * n, out_specs=(SEM_SPEC, SEM_SPEC, *[HBM_SPEC] * n, VMEM_SPEC),
        compiler_params=pltpu.CompilerParams(has_side_effects=SIDE_EFFECT),
    )(*[_hbm(a) for a in srcs])
    return outs[0], outs[1], list(srcs), list(outs[2:2 + n]), outs[-1]


def _split_wait(name, copies, started, after):
    send_sems, recv_sems, srcs, lands, _ = started
    n = len(srcs)

    def body(*refs):
        sends, recvs = copies(refs[:n], refs[n:2 * n], refs[2 * n], refs[2 * n + 1])
        for cp in sends:
            cp.wait_send()
        for cp in recvs:
            cp.wait_recv()

    return pl.pallas_call(
        body, name=name,
        out_shape=tuple(pltpu.HBM(a.shape, a.dtype) for a in lands),
        in_specs=[HBM_SPEC] * (2 * n) + [SEM_SPEC, SEM_SPEC, ANY_SPEC], out_specs=tuple([HBM_SPEC] * n),
        input_output_aliases={n + i: i for i in range(n)},
        compiler_params=pltpu.CompilerParams(has_side_effects=SIDE_EFFECT),
    )(*srcs, *lands, send_sems, recv_sems, after)


def _split_start_in_place(name, copies, bufs, sems_per_buf):
    n = len(bufs)

    def body(*refs):
        sends, _ = copies(refs[:n], refs[:n], refs[n], refs[n + 1])
        for cp in sends:
            cp.start()
        token = refs[-1]
        token[...] = jnp.zeros_like(token)

    n_sems = sems_per_buf * n
    outs = pl.pallas_call(
        body, name=name,
        out_shape=(pltpu.SemaphoreType.DMA((n_sems,)), pltpu.SemaphoreType.DMA((n_sems,)),
                   *[pltpu.HBM(a.shape, a.dtype) for a in bufs], jax.ShapeDtypeStruct((8, LANES), F32)),
        in_specs=[HBM_SPEC] * n, out_specs=(SEM_SPEC, SEM_SPEC, *[HBM_SPEC] * n, VMEM_SPEC),
        input_output_aliases={i: 2 + i for i in range(n)},
        compiler_params=pltpu.CompilerParams(has_side_effects=SIDE_EFFECT),
    )(*[_hbm(a) for a in bufs])
    return outs[0], outs[1], list(outs[2:2 + n]), outs[-1]


def _split_wait_in_place(name, copies, started, after):
    send_sems, recv_sems, bufs, _ = started
    n = len(bufs)

    def body(*refs):
        sends, recvs = copies(refs[:n], refs[:n], refs[n], refs[n + 1])
        for cp in sends:
            cp.wait_send()
        for cp in recvs:
            cp.wait_recv()

    return pl.pallas_call(
        body, name=name,
        out_shape=tuple(pltpu.HBM(a.shape, a.dtype) for a in bufs),
        in_specs=[HBM_SPEC] * n + [SEM_SPEC, SEM_SPEC, ANY_SPEC], out_specs=tuple([HBM_SPEC] * n),
        input_output_aliases={i: i for i in range(n)},
        compiler_params=pltpu.CompilerParams(has_side_effects=SIDE_EFFECT),
    )(*bufs, send_sems, recv_sems, after)


def _grad_copies(grads, parts, send_sems, recv_sems):
    x, y, c = lax.axis_index("x"), lax.axis_index("y"), lax.axis_index("c")
    chips = [(1 - x, y), (x, 1 - y), (1 - x, 1 - y)]
    my_slot = 4 * x + 2 * y + c
    sends, recvs = [], []
    for a, (grad, part) in enumerate(zip(grads, parts)):
        def copy(k, block, slot, to, a=a, grad=grad, part=part):
            return pltpu.make_async_remote_copy(
                src_ref=grad.at[block], dst_ref=part.at[slot], send_sem=send_sems.at[7 * a + k],
                recv_sem=recv_sems.at[7 * a + k], device_id=to, device_id_type=MESH_ID)
        sends.append(copy(0, 2 * x + y, my_slot, (x, y, 1 - c)))
        recvs.append(copy(0, 2 * x + y, 4 * x + 2 * y + (1 - c), (x, y, 1 - c)))
        for j, (px, py) in enumerate(chips):
            for other, pc in enumerate((c, 1 - c)):
                sends.append(copy(1 + 2 * j + other, 2 * px + py, my_slot, (px, py, pc)))
                recvs.append(copy(1 + 2 * j + other, 2 * x + y, 4 * px + 2 * py + pc, (px, py, pc)))
    return sends, recvs


def _grad_copies_same_core(grads, parts, send_sems, recv_sems):
    x, y, c = lax.axis_index("x"), lax.axis_index("y"), lax.axis_index("c")
    chips = [(1 - x, y), (x, 1 - y), (1 - x, 1 - y)]
    my_slot = 4 * x + 2 * y + c
    sends, recvs = [], []
    for a, (grad, part) in enumerate(zip(grads, parts)):
        def copy(k, block, slot, to, a=a, grad=grad, part=part):
            return pltpu.make_async_remote_copy(
                src_ref=grad.at[block], dst_ref=part.at[slot], send_sem=send_sems.at[4 * a + k],
                recv_sem=recv_sems.at[4 * a + k], device_id=to, device_id_type=MESH_ID)
        sends.append(copy(0, 2 * x + y, my_slot, (x, y, 1 - c)))
        recvs.append(copy(0, 2 * x + y, 4 * x + 2 * y + (1 - c), (x, y, 1 - c)))
        for j, (px, py) in enumerate(chips):
            sends.append(copy(1 + j, 2 * px + py, my_slot, (px, py, c)))
            recvs.append(copy(1 + j, 2 * x + y, 4 * px + 2 * py + c, (px, py, c)))
    return sends, recvs


def _grad_pass_copies(parts, same_parts, send_sems, recv_sems):
    del same_parts
    x, y, c = lax.axis_index("x"), lax.axis_index("y"), lax.axis_index("c")
    chips = [(1 - x, y), (x, 1 - y), (1 - x, 1 - y)]
    sends, recvs = [], []
    for a, part in enumerate(parts):
        for j, (px, py) in enumerate(chips):
            def copy(pc, a=a, j=j, px=px, py=py, part=part):
                slot = part.at[4 * px + 2 * py + pc]
                return pltpu.make_async_remote_copy(
                    src_ref=slot, dst_ref=slot, send_sem=send_sems.at[3 * a + j], recv_sem=recv_sems.at[3 * a + j],
                    device_id=(x, y, 1 - c), device_id_type=MESH_ID)
            sends.append(copy(c))
            recvs.append(copy(1 - c))
    return sends, recvs


def _ada_mod(c_all, ada_w, ada_b_cols):
    n_l, d, a4 = ada_w.shape
    tn = _tile(a4, 512)

    def body(c_ref, w_ref, b_ref, o_ref):
        cv = c_ref[...]
        ca = (cv * jax.nn.sigmoid(cv)).astype(BF16)
        o_ref[...] = _dot(ca, w_ref[...].astype(BF16)) + b_ref[...]

    return pl.pallas_call(
        body, name="ada_mod", grid=(n_l, a4 // tn),
        in_specs=[pl.BlockSpec((N_DEV, d), lambda l, j: (0, 0)),
                  pl.BlockSpec((None, d, tn), lambda l, j: (l, 0, j)),
                  pl.BlockSpec((None, 1, tn), lambda l, j: (l, 0, j))],
        out_specs=pl.BlockSpec((None, N_DEV, tn), lambda l, j: (l, 0, j)),
        out_shape=jax.ShapeDtypeStruct((n_l, N_DEV, a4), F32),
        compiler_params=_params("parallel", "parallel"),
    )(c_all, ada_w, ada_b_cols)


def _ada_grad_adam(c_all_t, dmod_cols, w, m, v):
    n_l, d, a4 = w.shape
    tn = _tile(a4, 512)

    def body(ct_ref, dm_ref, w_ref, m_ref, v_ref, g_ref, dl_ref, nm_ref, nv_ref):
        ct = ct_ref[...]
        ca = ct * jax.nn.sigmoid(ct)
        dm = dm_ref[...]
        g = ca[:, 0:1] * dm[0:1, :]
        for dev in range(1, N_DEV):
            g = g + ca[:, dev:dev + 1] * dm[dev:dev + 1, :]
        g_ref[...] = g
        delta, nm, nv = _adamw(w_ref[...], g, m_ref[...], v_ref[...])
        dl_ref[...] = delta
        nm_ref[...] = nm
        nv_ref[...] = nv

    wspec = pl.BlockSpec((None, d, tn), lambda l, j: (l, 0, j))
    shp = jax.ShapeDtypeStruct(w.shape, F32)
    return pl.pallas_call(
        body, name="ada_grad_adam", grid=(n_l, a4 // tn),
        in_specs=[pl.BlockSpec((d, N_DEV), lambda l, j: (0, 0)),
                  pl.BlockSpec((None, N_DEV, tn), lambda l, j: (l, 0, j)), wspec, wspec, wspec],
        out_specs=[wspec] * 4, out_shape=[shp] * 4,
        compiler_params=_params("parallel", "parallel"),
    )(c_all_t, dmod_cols, w, m, v)


def _lnmod(x, g, sc, sh):
    s, d = x.shape
    tm = _tile(s, 1024)

    def body(x_ref, g_ref, sc_ref, sh_ref, h_ref):
        xv = x_ref[...]
        r = lax.rsqrt(jnp.mean(xv * xv, axis=-1, keepdims=True) + EPS)
        h_ref[...] = ((xv * r * g_ref[...]) * (1.0 + sc_ref[...]) + sh_ref[...]).astype(BF16)

    vec = pl.BlockSpec((1, d), lambda i: (0, 0))
    row = pl.BlockSpec((tm, d), lambda i: (i, 0))
    return pl.pallas_call(
        body, name="lnmod", grid=(s // tm,), in_specs=[row, vec, vec, vec], out_specs=row,
        out_shape=jax.ShapeDtypeStruct((s, d), BF16), compiler_params=_params("parallel"),
    )(x, g, sc, sh)


def _mm_in(h, w_g):
    s, d = h.shape
    n4 = w_g.shape[-1]
    tm = _tile(s, 1024)

    def body(a_ref, b_ref, o_ref):
        o_ref[...] = _dot(a_ref[...], b_ref[...]).astype(BF16)

    return pl.pallas_call(
        body, name="mm_in", grid=(N_CHIP, s // tm),
        in_specs=[pl.BlockSpec((tm, d), lambda j, i: (i, 0)),
                  pl.BlockSpec((None, d, n4), lambda j, i: (j, 0, 0))],
        out_specs=pl.BlockSpec((tm, n4), lambda j, i: (i, j)),
        out_shape=jax.ShapeDtypeStruct((s, N_CHIP * n4), BF16),
        compiler_params=_params("parallel", "parallel"),
    )(h, w_g)


def _pair_mean(x, low):
    lo = jnp.sum(jnp.where(low, x, 0.0), axis=-1, keepdims=True)
    hi = jnp.sum(jnp.where(low, 0.0, x), axis=-1, keepdims=True)
    return jnp.where(low, lo, hi) * (1.0 / HEAD_DIM)


def _pair_norm(x, low):
    r = lax.rsqrt(_pair_mean(x * x, low) + EPS)
    return x * r, r


def _log_not(z):
    nz = -z
    return jnp.minimum(nz, 0.0) - jnp.log(1.0 + jnp.exp(jnp.minimum(z, nz)))


def _attn_consts(inclusive):
    low = lax.broadcasted_iota(jnp.int32, (1, LANES), 1) < HEAD_DIM
    row = lax.broadcasted_iota(jnp.int32, (Q_BLOCK, Q_BLOCK), 0)
    col = lax.broadcasted_iota(jnp.int32, (Q_BLOCK, Q_BLOCK), 1)
    tri = (row <= col) if inclusive else (row > col)
    w2 = jnp.concatenate([tri.astype(BF16), jnp.ones((Q_BLOCK, Q_BLOCK), BF16)], axis=1)
    return low, col < row, jnp.concatenate([w2, w2], axis=0)


def _split_cat(v):
    hi = v.astype(BF16)
    return jnp.concatenate([hi, (v - hi.astype(F32)).astype(BF16)], axis=1)


def _fill_pair_blocks(dst, src_fn, low, n_kb):
    def fill(b, _):
        v = src_fn(pl.ds(pl.multiple_of(b * Q_BLOCK, Q_BLOCK), Q_BLOCK))
        dst[b, 0:Q_BLOCK, :] = jnp.where(low, v, 0.0).astype(BF16)
        dst[b, Q_BLOCK:2 * Q_BLOCK, :] = jnp.where(low, 0.0, v).astype(BF16)
        return 0

    lax.fori_loop(0, n_kb, fill, 0)


def _attn_fwd(p, qg2, kg2, d):
    s = p.shape[0]
    n_pairs = d // LANES
    qsb = _tile(s, Q_SUPER)
    n_sub, n_sb, n_kb = qsb // Q_BLOCK, s // qsb, s // Q_BLOCK
    unroll = math.gcd(KEY_UNROLL, n_sub)
    chunk = _tile(s, 512)
    inv_sqrt = 1.0 / math.sqrt(HEAD_DIM)

    def body(q_ref, k_ref, v_ref, qg_ref, kg_ref, o_ref, lt_ref, qs, k2, v2, run, acc):
        low, causal, w4 = _attn_consts(False)

        def prep(r, _):
            rows = pl.ds(pl.multiple_of(r * chunk, chunk), chunk)
            qs[rows, :] = (_pair_norm(q_ref[rows, :].astype(F32), low)[0] * (qg_ref[...] * inv_sqrt)).astype(BF16)
            return 0

        lax.fori_loop(0, s // chunk, prep, 0)
        _fill_pair_blocks(k2, lambda rows: _pair_norm(k_ref[rows, :].astype(F32), low)[0] * kg_ref[...], low, n_kb)
        _fill_pair_blocks(v2, lambda rows: v_ref[rows, :].astype(F32), low, n_kb)

        def step(sb, j, t0=0, diag_t=None):
            rows = pl.ds(pl.multiple_of(sb * qsb + t0 * Q_BLOCK, Q_BLOCK), (n_sub - t0) * Q_BLOCK)
            z_both = _dot_nt(qs[rows, :], k2[j])
            zls, cats = [], []
            for t in range(t0, n_sub):
                sub = slice((t - t0) * Q_BLOCK, (t - t0 + 1) * Q_BLOCK)
                for h in range(2):
                    z = z_both[sub, h * LANES:(h + 1) * LANES]
                    ln = _log_not(z)
                    if t == diag_t:
                        ln = jnp.where(causal, ln, 0.0)
                    zls.append(z + ln)
                    cats.append(_split_cat(ln))
            c2 = _dot(jnp.concatenate(cats, axis=0), w4)
            a_rows = []
            for t in range(t0, n_sub):
                sub = slice(t * Q_BLOCK, (t + 1) * Q_BLOCK)
                a_pair = []
                for h in range(2):
                    i = 2 * (t - t0) + h
                    tile = slice(i * Q_BLOCK, (i + 1) * Q_BLOCK)
                    later = run[h, sub, :]
                    log_a = zls[i] + c2[tile, :LANES] + later
                    if t == diag_t:
                        log_a = jnp.where(causal, log_a, -1e30)
                    a_pair.append(jnp.exp(log_a).astype(BF16))
                    run[h, sub, :] = later + c2[tile, LANES:]
                a_rows.append(jnp.concatenate(a_pair, axis=1))
            acc[t0 * Q_BLOCK:, :] += _dot(jnp.concatenate(a_rows, axis=0), v2[j])

        def super_block(sb, _):
            run[...] = jnp.zeros_like(run)
            acc[...] = jnp.zeros_like(acc)
            for t in reversed(range(n_sub)):
                step(sb, sb * n_sub + t, t0=t, diag_t=t)

            def below(n, _):
                for u in range(unroll):
                    step(sb, sb * n_sub - 1 - (unroll * n + u))
                return 0

            lax.fori_loop(0, sb * (n_sub // unroll), below, 0)
            rows_sb = pl.ds(pl.multiple_of(sb * qsb, qsb), qsb)
            o_ref[rows_sb, :] = acc[...].astype(BF16)
            lt_ref[rows_sb, :] = jnp.where(low, run[0], run[1])
            return 0

        lax.fori_loop(0, n_sb, super_block, 0)

    def seg(k):
        return pl.BlockSpec((s, LANES), lambda h, k=k: (0, k * n_pairs + h))

    vec = pl.BlockSpec((1, LANES), lambda h: (0, 0))
    out = pl.BlockSpec((s, LANES), lambda h: (0, h))
    return pl.pallas_call(
        body, name="attn_fwd", grid=(n_pairs,),
        in_specs=[seg(0), seg(1), seg(2), vec, vec], out_specs=[out, out],
        out_shape=[jax.ShapeDtypeStruct((s, d), BF16), jax.ShapeDtypeStruct((s, d), F32)],
        scratch_shapes=[pltpu.VMEM((s, LANES), BF16)] + [pltpu.VMEM((n_kb, 2 * Q_BLOCK, LANES), BF16)] * 2
        + [pltpu.VMEM((2, qsb, LANES), F32), pltpu.VMEM((qsb, LANES), F32)],
        compiler_params=_params("parallel"),
    )(p, p, p, qg2, kg2)


def _conv_rows(s):
    return _tile(s, 512)


def _conv_fwd(p, conv_w, d):
    s = p.shape[0]
    nb = d // LANES
    rows_n = _conv_rows(s)

    def body(cb_ref, cc_ref, cx_ref, w_ref, y_ref, us):
        us[pl.ds(0, 8), :] = jnp.zeros((8, LANES), F32)

        def fill(r, _):
            rows = pl.ds(pl.multiple_of(r * rows_n, rows_n), rows_n)
            us[pl.ds(pl.multiple_of(r * rows_n + 8, 8), rows_n), :] = cc_ref[rows, :].astype(F32) * cx_ref[rows, :].astype(F32)
            return 0

        lax.fori_loop(0, s // rows_n, fill, 0)
        w = w_ref[...]

        def out(r, _):
            rows = pl.ds(pl.multiple_of(r * rows_n, rows_n), rows_n)
            ext = us[pl.ds(pl.multiple_of(r * rows_n, 8), rows_n + 8), :]
            cv = (w[0:1, :] * pltpu.roll(ext, 2, 0)[8:, :] + w[1:2, :] * pltpu.roll(ext, 1, 0)[8:, :]
                  + w[2:3, :] * ext[8:, :])
            y_ref[rows, :] = (cb_ref[rows, :].astype(F32) * cv).astype(BF16)
            return 0

        lax.fori_loop(0, s // rows_n, out, 0)

    def seg(k):
        return pl.BlockSpec((s, LANES), lambda b, k=k: (0, k * nb + b))

    return pl.pallas_call(
        body, name="conv_fwd", grid=(nb,),
        in_specs=[seg(3), seg(4), seg(5), pl.BlockSpec((3, LANES), lambda b: (0, b))],
        out_specs=pl.BlockSpec((s, LANES), lambda b: (0, b)),
        out_shape=jax.ShapeDtypeStruct((s, d), BF16),
        scratch_shapes=[pltpu.VMEM((s + 8, LANES), F32)],
        compiler_params=_params("parallel"),
    )(p, p, p, conv_w)


def _branch(ya, yb, p, wa, wb, d):
    s = ya.shape[0]
    tm = _tile(s, 512)

    def body(ya_ref, yb_ref, ga_ref, gb_ref, wa_ref, wb_ref, m_ref, a_ref, b_ref):
        pa = _dot(ya_ref[...], wa_ref[...])
        pb = _dot(yb_ref[...], wb_ref[...])
        ga, gb = ga_ref[...].astype(F32), gb_ref[...].astype(F32)
        m_ref[...] = (jax.nn.sigmoid(ga) * pa + jax.nn.sigmoid(gb) * pb).astype(BF16)
        a_ref[...] = pa.astype(BF16)
        b_ref[...] = pb.astype(BF16)

    row = pl.BlockSpec((tm, d), lambda i: (i, 0))
    wsp = pl.BlockSpec((d, d), lambda i: (0, 0))
    shp = jax.ShapeDtypeStruct((s, d), BF16)
    return pl.pallas_call(
        body, name="branch", grid=(s // tm,),
        in_specs=[row, row, pl.BlockSpec((tm, d), lambda i: (i, 6)), pl.BlockSpec((tm, d), lambda i: (i, 7)), wsp, wsp],
        out_specs=[row, row, row], out_shape=[shp, shp, shp], compiler_params=_params("parallel"),
    )(ya, yb, p, p, wa, wb)


def _out_proj(merged, wout, x0, g1, ln_g, sc, sh):
    s, d = x0.shape
    tm = _tile(s, 1024)

    def body(m_ref, w_ref, x_ref, g_ref, lg_ref, sc_ref, sh_ref, x1_ref, mo_ref, h_ref):
        mo = _dot(m_ref[...], w_ref[...])
        mo_ref[...] = mo
        x1 = x_ref[...] + g_ref[...] * mo
        x1_ref[...] = x1
        r = lax.rsqrt(jnp.mean(x1 * x1, axis=-1, keepdims=True) + EPS)
        h_ref[...] = ((x1 * r * lg_ref[...]) * (1.0 + sc_ref[...]) + sh_ref[...]).astype(BF16)

    row = pl.BlockSpec((tm, d), lambda i: (i, 0))
    vec = pl.BlockSpec((1, d), lambda i: (0, 0))
    shp = jax.ShapeDtypeStruct((s, d), F32)
    return pl.pallas_call(
        body, name="out_proj", grid=(s // tm,),
        in_specs=[row, pl.BlockSpec((d, d), lambda i: (0, 0)), row, vec, vec, vec, vec],
        out_specs=[row, row, row], out_shape=[shp, shp, jax.ShapeDtypeStruct((s, d), BF16)],
        compiler_params=_params("parallel"),
    )(merged, wout, x0, g1, ln_g, sc, sh)


def _ffn_up(h, wg_g, wu_g):
    s, d = h.shape
    f4 = wg_g.shape[-1]
    tm = _tile(s, 1024)

    def body(h_ref, wg_ref, wu_ref, gate_ref, up_ref, act_ref):
        hv = h_ref[...]
        gt = _dot(hv, wg_ref[...])
        up = _dot(hv, wu_ref[...])
        gate_ref[...] = gt.astype(BF16)
        up_ref[...] = up.astype(BF16)
        act_ref[...] = (gt * jax.nn.sigmoid(gt) * up).astype(BF16)

    wsp = pl.BlockSpec((None, d, f4), lambda j, i: (j, 0, 0))
    osp = pl.BlockSpec((None, tm, f4), lambda j, i: (j, i, 0))
    shp = jax.ShapeDtypeStruct((N_CHIP, s, f4), BF16)
    return pl.pallas_call(
        body, name="ffn_up", grid=(N_CHIP, s // tm),
        in_specs=[pl.BlockSpec((tm, d), lambda j, i: (i, 0)), wsp, wsp],
        out_specs=[osp, osp, osp], out_shape=[shp, shp, shp], compiler_params=_params("parallel", "parallel"),
    )(h, wg_g, wu_g)


def _ffn_down(act, wd_g, x1, g2):
    s, d = x1.shape
    f4 = act.shape[-1]
    tm = _tile(s, 1024)

    def body(a_ref, w_ref, x_ref, g_ref, x2_ref, f_ref, acc):
        j = pl.program_id(1)

        @pl.when(j == 0)
        def _():
            acc[...] = jnp.zeros_like(acc)

        acc[...] += _dot(a_ref[...], w_ref[...])

        @pl.when(j == N_CHIP - 1)
        def _():
            f = acc[...]
            f_ref[...] = f
            x2_ref[...] = x_ref[...] + g_ref[...] * f

    row = pl.BlockSpec((tm, d), lambda i, j: (i, 0))
    shp = jax.ShapeDtypeStruct((s, d), F32)
    return pl.pallas_call(
        body, name="ffn_down", grid=(s // tm, N_CHIP),
        in_specs=[pl.BlockSpec((None, tm, f4), lambda i, j: (j, i, 0)),
                  pl.BlockSpec((None, f4, d), lambda i, j: (j, 0, 0)),
                  row, pl.BlockSpec((1, d), lambda i, j: (0, 0))],
        out_specs=[row, row], out_shape=[shp, shp],
        scratch_shapes=[pltpu.VMEM((tm, d), F32)], compiler_params=_params("parallel", "arbitrary"),
    )(act, wd_g, x1, g2)


def _loss_head(y, target):
    s, d = y.shape
    tm = _tile(s, 1024)
    n_steps = s // tm

    def body(y_ref, t_ref, dy_ref, l_ref, acc):
        i = pl.program_id(0)

        @pl.when(i == 0)
        def _():
            acc[...] = jnp.zeros_like(acc)

        err = y_ref[...] - t_ref[...]
        dy_ref[...] = err / d
        acc[...] += jnp.sum(err * err, axis=0, keepdims=True)

        @pl.when(i == n_steps - 1)
        def _():
            l_ref[...] = jnp.broadcast_to(jnp.sum(acc[...], axis=1, keepdims=True), (8, LANES))

    row = pl.BlockSpec((tm, d), lambda i: (i, 0))
    return pl.pallas_call(
        body, name="loss_head", grid=(n_steps,), in_specs=[row, row],
        out_specs=[row, pl.BlockSpec((8, LANES), lambda i: (0, 0))],
        out_shape=[jax.ShapeDtypeStruct((s, d), F32), jax.ShapeDtypeStruct((8, LANES), F32)],
        scratch_shapes=[pltpu.VMEM((1, d), F32)], compiler_params=_params("arbitrary"),
    )(y, target)


def _mm_tn(a, b, a_spec, b_spec, out_rc, name):
    r, c = out_rc
    s = a.shape[-2]
    tk = _tile(s, 1024)
    nk = s // tk

    def body(a_ref, b_ref, o_ref, acc):
        k = pl.program_id(1)

        @pl.when(k == 0)
        def _():
            acc[...] = jnp.zeros_like(acc)

        acc[...] += _dot_tn(a_ref[...], b_ref[...])

        @pl.when(k == nk - 1)
        def _():
            o_ref[...] = acc[...].astype(BF16)

    return pl.pallas_call(
        body, name=name, grid=(N_CHIP, nk),
        in_specs=[pl.BlockSpec(*a_spec(tk)), pl.BlockSpec(*b_spec(tk))],
        out_specs=pl.BlockSpec((None, r, c), lambda j, k: (j, 0, 0)),
        out_shape=jax.ShapeDtypeStruct((N_CHIP, r, c), BF16),
        scratch_shapes=[pltpu.VMEM((r, c), F32)], compiler_params=_params("parallel", "arbitrary"),
    )(a, b)


def _mm_tn_square(a, b, name):
    s, d = a.shape
    r4 = d // N_CHIP
    tk = _tile(s, 1024)
    nk = s // tk

    def body(a_ref, b_ref, o_ref, acc):
        k = pl.program_id(0)

        @pl.when(k == 0)
        def _():
            acc[...] = jnp.zeros_like(acc)

        acc[...] += _dot_tn(a_ref[...], b_ref[...])

        @pl.when(k == nk - 1)
        def _():
            for j in range(N_CHIP):
                o_ref[j] = acc[j * r4:(j + 1) * r4, :].astype(BF16)

    blk = pl.BlockSpec((tk, d), lambda k: (k, 0))
    return pl.pallas_call(
        body, name=name, grid=(nk,), in_specs=[blk, blk],
        out_specs=pl.BlockSpec((N_CHIP, r4, d), lambda k: (0, 0, 0)),
        out_shape=jax.ShapeDtypeStruct((N_CHIP, r4, d), BF16),
        scratch_shapes=[pltpu.VMEM((d, d), F32)], compiler_params=_params("arbitrary"),
    )(a, b)


def _ffn_bwd1(dx2, f, g2, wd_g, gate, up):
    s, d = dx2.shape
    f4 = gate.shape[-1]
    tm = _tile(s, 1024)

    def body(dx_ref, f_ref, g_ref, w_ref, gate_ref, up_ref, dgate_ref, dup_ref, df_ref, dg_ref):
        i, j = pl.program_id(0), pl.program_id(1)

        @pl.when((i == 0) & (j == 0))
        def _():
            dg_ref[...] = jnp.zeros_like(dg_ref)

        dxv = dx_ref[...]
        df = (g_ref[...] * dxv).astype(BF16)

        @pl.when(j == 0)
        def _():
            df_ref[...] = df
            dg_ref[0:1, :] += jnp.sum(dxv * f_ref[...], axis=0, keepdims=True)

        da = _dot_nt(df, w_ref[...])
        gt = gate_ref[...].astype(F32)
        sg = jax.nn.sigmoid(gt)
        dup_ref[...] = (da * gt * sg).astype(BF16)
        dgate_ref[...] = (da * up_ref[...].astype(F32) * (sg * (1.0 + gt * (1.0 - sg)))).astype(BF16)

    row = pl.BlockSpec((tm, d), lambda i, j: (i, 0))
    hsp = pl.BlockSpec((None, tm, f4), lambda i, j: (j, i, 0))
    hshp = jax.ShapeDtypeStruct((N_CHIP, s, f4), BF16)
    return pl.pallas_call(
        body, name="ffn_bwd1", grid=(s // tm, N_CHIP),
        in_specs=[row, row, pl.BlockSpec((1, d), lambda i, j: (0, 0)),
                  pl.BlockSpec((None, f4, d), lambda i, j: (j, 0, 0)), hsp, hsp],
        out_specs=[hsp, hsp, row, pl.BlockSpec((8, d), lambda i, j: (0, 0))],
        out_shape=[hshp, hshp, jax.ShapeDtypeStruct((s, d), BF16), jax.ShapeDtypeStruct((8, d), F32)],
        compiler_params=_params("arbitrary", "arbitrary"),
    )(dx2, f, g2, wd_g, gate, up)


def _lnmod_bwd_rows(xv, gv, scv, dhv, drv, sums_ref):
    r = lax.rsqrt(jnp.mean(xv * xv, axis=-1, keepdims=True) + EPS)
    n = xv * r
    dt = dhv * (1.0 + scv)
    sums_ref[0:1, :] += jnp.sum(dhv, axis=0, keepdims=True)
    sums_ref[1:2, :] += jnp.sum(dhv * (n * gv), axis=0, keepdims=True)
    sums_ref[2:3, :] += jnp.sum(dt * n, axis=0, keepdims=True)
    dn = dt * gv
    return drv + r * (dn - n * jnp.mean(dn * n, axis=-1, keepdims=True))


def _ffn_bwd2(dgate, dup, wg_g, wu_g, x, g, sc, dres):
    _, s, f4 = dgate.shape
    d = wg_g.shape[-2]
    tm = _tile(s, 1024)

    def body(dg_ref, du_ref, wg_ref, wu_ref, x_ref, g_ref, sc_ref, dr_ref, dx_ref, sums_ref, acc):
        i, j = pl.program_id(0), pl.program_id(1)

        @pl.when((i == 0) & (j == 0))
        def _():
            sums_ref[...] = jnp.zeros_like(sums_ref)

        @pl.when(j == 0)
        def _():
            acc[...] = jnp.zeros_like(acc)

        acc[...] += _dot_nt(dg_ref[...], wg_ref[...]) + _dot_nt(du_ref[...], wu_ref[...])

        @pl.when(j == N_CHIP - 1)
        def _():
            dx_ref[...] = _lnmod_bwd_rows(x_ref[...], g_ref[...], sc_ref[...], acc[...], dr_ref[...], sums_ref)

    hsp = pl.BlockSpec((None, tm, f4), lambda i, j: (j, i, 0))
    wsp = pl.BlockSpec((None, d, f4), lambda i, j: (j, 0, 0))
    row = pl.BlockSpec((tm, d), lambda i, j: (i, 0))
    vec = pl.BlockSpec((1, d), lambda i, j: (0, 0))
    return pl.pallas_call(
        body, name="ffn_bwd2", grid=(s // tm, N_CHIP), in_specs=[hsp, hsp, wsp, wsp, row, vec, vec, row],
        out_specs=[row, pl.BlockSpec((8, d), lambda i, j: (0, 0))],
        out_shape=[jax.ShapeDtypeStruct((s, d), F32), jax.ShapeDtypeStruct((8, d), F32)],
        scratch_shapes=[pltpu.VMEM((tm, d), F32)], compiler_params=_params("arbitrary", "arbitrary"),
    )(dgate, dup, wg_g, wu_g, x, g, sc, dres)


def _lnmod_bwd(x, g, sc, dh, dres):
    s, d = x.shape
    tm = _tile(s, 1024)

    def body(x_ref, g_ref, sc_ref, dh_ref, dr_ref, dx_ref, sums_ref):
        @pl.when(pl.program_id(0) == 0)
        def _():
            sums_ref[...] = jnp.zeros_like(sums_ref)

        dx_ref[...] = _lnmod_bwd_rows(x_ref[...], g_ref[...], sc_ref[...], dh_ref[...], dr_ref[...], sums_ref)

    vec = pl.BlockSpec((1, d), lambda i: (0, 0))
    row = pl.BlockSpec((tm, d), lambda i: (i, 0))
    return pl.pallas_call(
        body, name="lnmod_bwd", grid=(s // tm,), in_specs=[row, vec, vec, row, row],
        out_specs=[row, pl.BlockSpec((8, d), lambda i: (0, 0))],
        out_shape=[jax.ShapeDtypeStruct((s, d), F32), jax.ShapeDtypeStruct((8, d), F32)],
        compiler_params=_params("arbitrary"),
    )(x, g, sc, dh, dres)


def _out_bwd(dx1, mo, g1, wout, pa, pb, p, wa, wb, d):
    s = dx1.shape[0]
    tm = _tile(s, 256)

    def body(dx_ref, mo_ref, g_ref, wo_ref, pa_ref, pb_ref, ga_ref, gb_ref, wa_ref, wb_ref,
             dmo_ref, da_ref, db_ref, dya_ref, dyb_ref, dp_ref, dg_ref):
        @pl.when(pl.program_id(0) == 0)
        def _():
            dg_ref[...] = jnp.zeros_like(dg_ref)

        dxv = dx_ref[...]
        dg_ref[0:1, :] += jnp.sum(dxv * mo_ref[...], axis=0, keepdims=True)
        dmo = (g_ref[...] * dxv).astype(BF16)
        dmo_ref[...] = dmo
        dm = _dot_nt(dmo, wo_ref[...])
        sa, sb = jax.nn.sigmoid(ga_ref[...].astype(F32)), jax.nn.sigmoid(gb_ref[...].astype(F32))
        da = (dm * sa).astype(BF16)
        db = (dm * sb).astype(BF16)
        da_ref[...] = da
        db_ref[...] = db
        dp_ref[:, :d] = (dm * pa_ref[...].astype(F32) * (sa * (1.0 - sa))).astype(BF16)
        dp_ref[:, d:] = (dm * pb_ref[...].astype(F32) * (sb * (1.0 - sb))).astype(BF16)
        dya_ref[...] = _dot_nt(da, wa_ref[...]).astype(BF16)
        dyb_ref[...] = _dot_nt(db, wb_ref[...]).astype(BF16)

    row = pl.BlockSpec((tm, d), lambda i: (i, 0))
    wsp = pl.BlockSpec((d, d), lambda i: (0, 0))
    shp = jax.ShapeDtypeStruct((s, d), BF16)
    return pl.pallas_call(
        body, name="out_bwd", grid=(s // tm,),
        in_specs=[row, row, pl.BlockSpec((1, d), lambda i: (0, 0)), wsp, row, row,
                  pl.BlockSpec((tm, d), lambda i: (i, 6)), pl.BlockSpec((tm, d), lambda i: (i, 7)), wsp, wsp],
        out_specs=[row] * 5 + [pl.BlockSpec((tm, 2 * d), lambda i: (i, 3)), pl.BlockSpec((8, d), lambda i: (0, 0))],
        out_shape=[shp] * 5 + [jax.ShapeDtypeStruct((s, 8 * d), BF16), jax.ShapeDtypeStruct((8, d), F32)],
        compiler_params=_params("arbitrary"),
    )(dx1, mo, g1, wout, pa, pb, p, p, wa, wb)


def _store_segments(outs, dp_out, sems, col_blocks):
    copies = [pltpu.make_async_copy(outs.at[k], dp_out.at[:, pl.ds(pl.multiple_of(cb * LANES, LANES), LANES)],
                                    sems.at[k]) for k, cb in enumerate(col_blocks)]
    for cp in copies:
        cp.start()
    for cp in copies:
        cp.wait()


def _conv_bwd(p, conv_w, dyb, dp, d):
    s = p.shape[0]
    nb = d // LANES
    rows_n = _conv_rows(s)

    def compute(cb_ref, cc_ref, cx_ref, w_ref, dy_ref, dcb_ref, dcc_ref, dcx_ref, dw_ref, us, ds):
        us[pl.ds(0, 8), :] = jnp.zeros((8, LANES), F32)
        ds[pl.ds(s, 8), :] = jnp.zeros((8, LANES), F32)

        def fill(r, _):
            rows = pl.ds(pl.multiple_of(r * rows_n, rows_n), rows_n)
            us[pl.ds(pl.multiple_of(r * rows_n + 8, 8), rows_n), :] = cc_ref[rows, :].astype(F32) * cx_ref[rows, :].astype(F32)
            ds[rows, :] = dy_ref[rows, :].astype(F32) * cb_ref[rows, :].astype(F32)
            return 0

        lax.fori_loop(0, s // rows_n, fill, 0)
        w = w_ref[...]

        def out(r, carry):
            dw0, dw1, dw2 = carry
            rows = pl.ds(pl.multiple_of(r * rows_n, rows_n), rows_n)
            ext = us[pl.ds(pl.multiple_of(r * rows_n, 8), rows_n + 8), :]
            u0, u1, u2 = ext[8:, :], pltpu.roll(ext, 1, 0)[8:, :], pltpu.roll(ext, 2, 0)[8:, :]
            cv = w[0:1, :] * u2 + w[1:2, :] * u1 + w[2:3, :] * u0
            dcb_ref[rows, :] = (dy_ref[rows, :].astype(F32) * cv).astype(BF16)
            nxt = ds[pl.ds(pl.multiple_of(r * rows_n, 8), rows_n + 8), :]
            e0 = nxt[:rows_n, :]
            e1 = pltpu.roll(nxt, rows_n + 7, 0)[:rows_n, :]
            e2 = pltpu.roll(nxt, rows_n + 6, 0)[:rows_n, :]
            du = w[2:3, :] * e0 + w[1:2, :] * e1 + w[0:1, :] * e2
            dcc_ref[rows, :] = (du * cx_ref[rows, :].astype(F32)).astype(BF16)
            dcx_ref[rows, :] = (du * cc_ref[rows, :].astype(F32)).astype(BF16)
            return (dw0 + jnp.sum(e0 * u2, axis=0, keepdims=True), dw1 + jnp.sum(e0 * u1, axis=0, keepdims=True),
                    dw2 + jnp.sum(e0 * u0, axis=0, keepdims=True))

        zero = jnp.zeros((1, LANES), F32)
        dw0, dw1, dw2 = lax.fori_loop(0, s // rows_n, out, (zero, zero, zero))
        dw_ref[...] = jnp.zeros_like(dw_ref)
        dw_ref[0:1, :] = dw0
        dw_ref[1:2, :] = dw1
        dw_ref[2:3, :] = dw2

    def body(cb_ref, cc_ref, cx_ref, w_ref, dy_ref, dp_in, dp_out, dw_ref, us, ds, outs, sems):
        del dp_in
        compute(cb_ref, cc_ref, cx_ref, w_ref, dy_ref, outs.at[0], outs.at[1], outs.at[2], dw_ref, us, ds)
        _store_segments(outs, dp_out, sems, [(3 + k) * nb + pl.program_id(0) for k in range(3)])

    def seg(k):
        return pl.BlockSpec((s, LANES), lambda b, k=k: (0, k * nb + b))

    return pl.pallas_call(
        body, name="conv_bwd", grid=(nb,),
        in_specs=[seg(3), seg(4), seg(5), pl.BlockSpec((3, LANES), lambda b: (0, b)),
                  pl.BlockSpec((s, LANES), lambda b: (0, b)), ANY_SPEC],
        out_specs=[ANY_SPEC, pl.BlockSpec((8, LANES), lambda b: (0, b))],
        out_shape=[jax.ShapeDtypeStruct(dp.shape, BF16), jax.ShapeDtypeStruct((8, d), F32)],
        input_output_aliases={5: 0},
        scratch_shapes=[pltpu.VMEM((s + 8, LANES), F32), pltpu.VMEM((s + 8, LANES), F32),
                        pltpu.VMEM((3, s, LANES), BF16), pltpu.SemaphoreType.DMA((3,))],
        compiler_params=_params("arbitrary"),
    )(p, p, p, conv_w, dyb, dp)


def _attn_bwd(p, qg2, kg2, dy, lt, dp, d):
    s = p.shape[0]
    n_pairs = d // LANES
    qsb = _tile(s, Q_SUPER_BWD)
    n_sub, n_sb, n_kb = qsb // Q_BLOCK, s // qsb, s // Q_BLOCK
    unroll = math.gcd(KEY_UNROLL, n_sub)
    chunk = _tile(s, 512)
    inv_sqrt = 1.0 / math.sqrt(HEAD_DIM)

    def compute(q_ref, k_ref, v_ref, qg_ref, kg_ref, dy_ref, lt_ref, dq_ref, dk_ref, dv_ref, dgain_ref,
                qs, k2, v2, dkt, dvt, qt, dyt, rem, gbef, dqa):
        low, causal, w4 = _attn_consts(True)

        def prep(r, _):
            rows = pl.ds(pl.multiple_of(r * chunk, chunk), chunk)
            qs[rows, :] = (_pair_norm(q_ref[rows, :].astype(F32), low)[0] * (qg_ref[...] * inv_sqrt)).astype(BF16)
            return 0

        lax.fori_loop(0, s // chunk, prep, 0)
        _fill_pair_blocks(k2, lambda rows: _pair_norm(k_ref[rows, :].astype(F32), low)[0] * kg_ref[...], low, n_kb)
        _fill_pair_blocks(v2, lambda rows: v_ref[rows, :].astype(F32), low, n_kb)

        def clear(b, _):
            dkt[b] = jnp.zeros((LANES, Q_BLOCK), F32)
            dvt[b] = jnp.zeros((LANES, Q_BLOCK), F32)
            return 0

        lax.fori_loop(0, n_kb, clear, 0)

        def step(sb, j, t0=0, diag_t=None):
            rows = pl.ds(pl.multiple_of(sb * qsb + t0 * Q_BLOCK, Q_BLOCK), (n_sub - t0) * Q_BLOCK)
            kj2, vj2 = k2[j], v2[j]
            z_both = _dot_nt(qs[rows, :], kj2)
            da_both = _dot_nt(dy_ref[rows, :], vj2)
            zls, cats = [], []
            for t in range(t0, n_sub):
                sub = slice((t - t0) * Q_BLOCK, (t - t0 + 1) * Q_BLOCK)
                for h in range(2):
                    z = z_both[sub, h * LANES:(h + 1) * LANES]
                    ln = _log_not(z)
                    if t == diag_t:
                        ln = jnp.where(causal, ln, 0.0)
                    zls.append(z + ln)
                    cats.append(_split_cat(ln))
            c2 = _dot(jnp.concatenate(cats, axis=0), w4)
            a_rows, gs, cats = [], [], []
            for t in range(t0, n_sub):
                sub = slice(t * Q_BLOCK, (t + 1) * Q_BLOCK)
                a_pair = []
                for h in range(2):
                    i = 2 * (t - t0) + h
                    tile = slice(i * Q_BLOCK, (i + 1) * Q_BLOCK)
                    left = rem[h, sub, :]
                    log_a = zls[i] + (left - c2[tile, :LANES])
                    if t == diag_t:
                        log_a = jnp.where(causal, log_a, -1e30)
                    a = jnp.exp(log_a)
                    rem[h, sub, :] = left - c2[tile, LANES:]
                    g = a * da_both[(t - t0) * Q_BLOCK:(t - t0 + 1) * Q_BLOCK, h * LANES:(h + 1) * LANES]
                    a_pair.append(a.astype(BF16))
                    gs.append(g)
                    cats.append(g.astype(BF16))
                a_rows.append(jnp.concatenate(a_pair, axis=1))
            c2g = _dot(jnp.concatenate(cats, axis=0), w4[:Q_BLOCK, :])
            dz_rows = []
            for t in range(t0, n_sub):
                sub = slice(t * Q_BLOCK, (t + 1) * Q_BLOCK)
                dz_pair = []
                for h in range(2):
                    i = 2 * (t - t0) + h
                    tile = slice(i * Q_BLOCK, (i + 1) * Q_BLOCK)
                    before = gbef[h, sub, :]
                    dz = gs[i] - jnp.exp(zls[i]) * (before + c2g[tile, :LANES])
                    if t == diag_t:
                        dz = jnp.where(causal, dz, 0.0)
                    gbef[h, sub, :] = before + c2g[tile, LANES:]
                    dz_pair.append(dz.astype(BF16))
                dz_rows.append(jnp.concatenate(dz_pair, axis=1))
            a_both = jnp.concatenate(a_rows, axis=0)
            dz_both = jnp.concatenate(dz_rows, axis=0)
            used = slice(t0 * Q_BLOCK, qsb)
            dvt[j] += _dot(dyt[0, :, used], a_both[:, :LANES]) + _dot(dyt[1, :, used], a_both[:, LANES:])
            dkt[j] += _dot(qt[0, :, used], dz_both[:, :LANES]) + _dot(qt[1, :, used], dz_both[:, LANES:])
            dqa[used, :] += _dot(dz_both, kj2)

        def super_block(sb, dqg):
            rows_sb = pl.ds(pl.multiple_of(sb * qsb, qsb), qsb)
            total = lt_ref[rows_sb, :]
            other = pltpu.roll(total, HEAD_DIM, 1)
            rem[0] = jnp.where(low, total, other)
            rem[1] = jnp.where(low, other, total)
            gbef[...] = jnp.zeros_like(gbef)
            dqa[...] = jnp.zeros_like(dqa)
            qv = qs[rows_sb, :].astype(F32)
            dyv = dy_ref[rows_sb, :].astype(F32)
            qt[0] = jnp.where(low, qv, 0.0).T.astype(BF16)
            qt[1] = jnp.where(low, 0.0, qv).T.astype(BF16)
            dyt[0] = jnp.where(low, dyv, 0.0).T.astype(BF16)
            dyt[1] = jnp.where(low, 0.0, dyv).T.astype(BF16)

            def below(n, _):
                for u in range(unroll):
                    step(sb, unroll * n + u)
                return 0

            lax.fori_loop(0, sb * (n_sub // unroll), below, 0)
            for t in range(n_sub):
                step(sb, sb * n_sub + t, t0=t, diag_t=t)
            qhat, r = _pair_norm(q_ref[rows_sb, :].astype(F32), low)
            dqn = dqa[...]
            dqhat = dqn * (qg_ref[...] * inv_sqrt)
            dq_ref[rows_sb, :] = (r * (dqhat - qhat * _pair_mean(dqhat * qhat, low))).astype(BF16)
            return dqg + jnp.sum(dqn * qhat, axis=0, keepdims=True) * inv_sqrt

        dqg = lax.fori_loop(0, n_sb, super_block, jnp.zeros((1, LANES), F32))

        def finish(b, dkg):
            rows = pl.ds(pl.multiple_of(b * Q_BLOCK, Q_BLOCK), Q_BLOCK)
            khat, rk = _pair_norm(k_ref[rows, :].astype(F32), low)
            dkn = dkt[b].T
            dkhat = dkn * kg_ref[...]
            dk_ref[rows, :] = (rk * (dkhat - khat * _pair_mean(dkhat * khat, low))).astype(BF16)
            dv_ref[rows, :] = dvt[b].T.astype(BF16)
            return dkg + jnp.sum(dkn * khat, axis=0, keepdims=True)

        dkg = lax.fori_loop(0, n_kb, finish, jnp.zeros((1, LANES), F32))
        dgain_ref[...] = jnp.zeros_like(dgain_ref)
        dgain_ref[0:1, :] = dqg
        dgain_ref[1:2, :] = dkg

    def body(q_ref, k_ref, v_ref, qg_ref, kg_ref, dy_ref, lt_ref, dp_in, dp_out, dgain_ref, outs, sems, *scratch):
        del dp_in
        compute(q_ref, k_ref, v_ref, qg_ref, kg_ref, dy_ref, lt_ref, outs.at[0], outs.at[1], outs.at[2], dgain_ref,
                *scratch)
        _store_segments(outs, dp_out, sems, [k * n_pairs + pl.program_id(0) for k in range(3)])

    def seg(k):
        return pl.BlockSpec((s, LANES), lambda h, k=k: (0, k * n_pairs + h))

    vec = pl.BlockSpec((1, LANES), lambda h: (0, 0))
    col = pl.BlockSpec((s, LANES), lambda h: (0, h))
    return pl.pallas_call(
        body, name="attn_bwd", grid=(n_pairs,),
        in_specs=[seg(0), seg(1), seg(2), vec, vec, col, col, ANY_SPEC],
        out_specs=[ANY_SPEC, pl.BlockSpec((None, 8, LANES), lambda h: (h, 0, 0))],
        out_shape=[jax.ShapeDtypeStruct(dp.shape, BF16), jax.ShapeDtypeStruct((n_pairs, 8, LANES), F32)],
        input_output_aliases={7: 0},
        scratch_shapes=[pltpu.VMEM((3, s, LANES), BF16), pltpu.SemaphoreType.DMA((3,)), pltpu.VMEM((s, LANES), BF16)]
        + [pltpu.VMEM((n_kb, 2 * Q_BLOCK, LANES), BF16)] * 2
        + [pltpu.VMEM((n_kb, LANES, Q_BLOCK), F32)] * 2
        + [pltpu.VMEM((2, LANES, qsb), BF16)] * 2
        + [pltpu.VMEM((2, qsb, LANES), F32)] * 2 + [pltpu.VMEM((qsb, LANES), F32)],
        compiler_params=_params("arbitrary"),
    )(p, p, p, qg2, kg2, dy, lt, dp)


def _mm_in_bwd(dp, w_g):
    s = dp.shape[0]
    d, n4 = w_g.shape[-2:]
    tm = _tile(s, 1024)

    def body(a_ref, w_ref, o_ref, acc):
        j = pl.program_id(1)

        @pl.when(j == 0)
        def _():
            acc[...] = jnp.zeros_like(acc)

        acc[...] += _dot_nt(a_ref[...], w_ref[...])

        @pl.when(j == N_CHIP - 1)
        def _():
            o_ref[...] = acc[...]

    return pl.pallas_call(
        body, name="mm_in_bwd", grid=(s // tm, N_CHIP),
        in_specs=[pl.BlockSpec((tm, n4), lambda i, j: (i, j)),
                  pl.BlockSpec((None, d, n4), lambda i, j: (j, 0, 0))],
        out_specs=pl.BlockSpec((tm, d), lambda i, j: (i, 0)), out_shape=jax.ShapeDtypeStruct((s, d), F32),
        scratch_shapes=[pltpu.VMEM((tm, d), F32)], compiler_params=_params("parallel", "arbitrary"),
    )(dp, w_g)


def _sum_adam(parts, w, m, v, name):
    n_l, r, c = w.shape
    tr = next((t for t in (256, 176, 128, 64, 32, 16) if r % t == 0 and t * c <= 256 * 1024), r)
    n_blk = r // tr

    def body(*refs):
        p_refs = refs[:n_l]
        w_ref, m_ref, v_ref, g_ref, dl_ref, nm_ref, nv_ref = refs[n_l:]
        for l in range(n_l):
            @pl.when(pl.program_id(0) == l)
            def _(p_ref=p_refs[l]):
                g = p_ref[0].astype(F32)
                for dev in range(1, N_DEV):
                    g = g + p_ref[dev].astype(F32)
                g_ref[...] = g
                delta, nm, nv = _adamw(w_ref[...], g, m_ref[...], v_ref[...])
                dl_ref[...] = delta
                nm_ref[...] = nm
                nv_ref[...] = nv

    def part_spec(l):
        return pl.BlockSpec((N_DEV, tr, c), lambda ll, i, l=l: (0, jnp.where(ll == l, i, jnp.where(ll < l, 0, n_blk - 1)), 0))

    wsp = pl.BlockSpec((None, tr, c), lambda l, i: (l, i, 0))
    shp = jax.ShapeDtypeStruct(w.shape, F32)
    return pl.pallas_call(
        body, name=name, grid=(n_l, n_blk),
        in_specs=[part_spec(l) for l in range(n_l)] + [wsp, wsp, wsp],
        out_specs=[wsp] * 4, out_shape=[shp] * 4, compiler_params=_params("arbitrary", "arbitrary"),
    )(*parts, w, m, v)


def _small_adam(parts, w, m, v):
    def body(p_ref, w_ref, m_ref, v_ref, g_ref, dl_ref, nm_ref, nv_ref):
        g = p_ref[0]
        for dev in range(1, N_DEV):
            g = g + p_ref[dev]
        g_ref[...] = g
        delta, nm, nv = _adamw(w_ref[...], g, m_ref[...], v_ref[...])
        dl_ref[...] = delta
        nm_ref[...] = nm
        nv_ref[...] = nv

    shp = jax.ShapeDtypeStruct(w.shape, F32)
    return pl.pallas_call(body, name="small_adam", in_specs=[VMEM_SPEC] * 4, out_specs=[VMEM_SPEC] * 4,
                          out_shape=[shp] * 4,
                          compiler_params=pltpu.CompilerParams(vmem_limit_bytes=VMEM_LIMIT_BYTES))(parts, w, m, v)


def _pack(vecs, mult=8 * LANES):
    flat = jnp.concatenate([a.reshape(-1).astype(F32) for a in vecs])
    pad = (-flat.shape[0]) % mult
    if pad:
        flat = jnp.concatenate([flat, jnp.zeros((pad,), F32)])
    return flat.reshape(8, -1)


def _unpack(flat, shapes):
    flat = flat.reshape(-1)
    out, off = [], 0
    for shp in shapes:
        n = math.prod(shp)
        out.append(flat[off:off + n].reshape(shp))
        off += n
    return out


BIG = ("win", "wa", "wb", "wo", "wg", "wu", "wd")
GRAD_GROUPS = (("wd", "wg", "wu"), ("wo", "wa", "wb"), ("win",))


def _local_step(x, target, mods, ln1_g, ln2_g, qg, kg, conv_w, weights, send_grads):
    s, d = x.shape
    n_l = mods.shape[0]
    saved = []
    h_in = x
    for l in range(n_l):
        sh1, sc1, g1, sh2, sc2, g2 = [mods[l, k * d:(k + 1) * d].reshape(1, d) for k in range(6)]
        qg2, kg2 = jnp.tile(qg[l:l + 1], (1, 2)), jnp.tile(kg[l:l + 1], (1, 2))
        h1 = _lnmod(h_in, ln1_g[l:l + 1], sc1, sh1)
        (win,), tie = weights(l, ("win",), h1)
        p = _mm_in(h1, win)
        ya, lt = _attn_fwd(p, qg2 + tie, kg2, d)
        yb = _conv_fwd(p, conv_w[l], d)
        (wa, wb, wo, wg, wu, wd), tie = weights(l, ("wa", "wb", "wo", "wg", "wu", "wd"), ya)
        wa, wb, wo = wa.reshape(d, d), wb.reshape(d, d), wo.reshape(d, d)
        merged, pa, pb = _branch(ya, yb, p, wa, wb, d)
        x1, mo, h2 = _out_proj(merged, wo, h_in, g1 + tie, ln2_g[l:l + 1], sc2, sh2)
        gate, up, act = _ffn_up(h2, wg, wu)
        x2, f = _ffn_down(act, wd, x1, g2)
        saved.append(dict(x0=h_in, h1=h1, p=p, ya=ya, lt=lt, yb=yb, merged=merged, pa=pa, pb=pb, x1=x1, mo=mo,
                          h2=h2, gate=gate, up=up, act=act, f=f, win=win, wa=wa, wb=wb, wo=wo, wg=wg, wu=wu, wd=wd,
                          mod=(sh1, sc1, g1, sh2, sc2, g2), qg2=qg2, kg2=kg2))
        h_in = x2

    dx, loss_tile = _loss_head(h_in, target)

    small = [None] * n_l
    for l in reversed(range(n_l)):
        sv = saved[l]
        sh1, sc1, g1, sh2, sc2, g2 = sv["mod"]
        f4, n4 = sv["wg"].shape[-1], sv["win"].shape[-1]
        hsp = lambda tk: ((tk, d), lambda j, k: (k, 0))
        fsp = lambda tk: ((None, tk, f4), lambda j, k: (j, k, 0))
        dgate, dup, df, dg2 = _ffn_bwd1(dx, sv["f"], g2, sv["wd"], sv["gate"], sv["up"])
        g_wd = _mm_tn(sv["act"], df, fsp, hsp, (f4, d), "grad_wd")
        g_wg = _mm_tn(dgate, sv["h2"], fsp, hsp, (f4, d), "grad_wg")
        g_wu = _mm_tn(dup, sv["h2"], fsp, hsp, (f4, d), "grad_wu")
        tie = send_grads(l, dict(wd=g_wd, wg=g_wg, wu=g_wu))
        dx1, sums2 = _ffn_bwd2(dgate, dup, sv["wg"], sv["wu"], sv["x1"], ln2_g[l:l + 1], sc2 + tie, dx)
        dmo, da, db, dya, dyb, dp, dg1 = _out_bwd(dx1, sv["mo"], g1, sv["wo"], sv["pa"], sv["pb"], sv["p"],
                                                        sv["wa"], sv["wb"], d)
        g_wo = _mm_tn_square(sv["merged"], dmo, "grad_wo")
        g_wa = _mm_tn_square(sv["ya"], da, "grad_wa")
        g_wb = _mm_tn_square(sv["yb"], db, "grad_wb")
        tie = send_grads(l, dict(wo=g_wo, wa=g_wa, wb=g_wb))
        dp, dconv = _conv_bwd(sv["p"], conv_w[l] + tie, dyb, dp, d)
        dp, dgain = _attn_bwd(sv["p"], sv["qg2"], sv["kg2"], dya, sv["lt"], dp, d)
        g_win = _mm_tn(sv["h1"], dp, hsp, lambda tk: ((tk, n4), lambda j, k: (k, j)), (d, n4), "grad_win")
        tie = send_grads(l, dict(win=g_win))
        dh1 = _mm_in_bwd(dp, sv["win"])
        dx, sums1 = _lnmod_bwd(sv["x0"], ln1_g[l:l + 1], sc1 + tie, dh1, dx1)
        dgain = jnp.sum(dgain[:, 0:2, :], axis=0)
        dgain = dgain[:, :HEAD_DIM] + dgain[:, HEAD_DIM:]
        dmod = jnp.concatenate([sums1[0], sums1[1], dg1[0], sums2[0], sums2[1], dg2[0]])
        small[l] = dict(dmod=dmod, ln1=sums1[2], ln2=sums2[2], qg=dgain[0], kg=dgain[1], conv=dconv[0:3])
    return loss_tile, dx, small


def kernel(x, c, ada_w, ada_b, ln1_g, w_in, q_norm_g, k_norm_g, conv_w, w_branch_a, w_branch_b, w_out, ln2_g, w_ffn_gate, w_ffn_up, w_ffn_down, loss_target, m_ada_w, m_ada_b, m_ln1_g, m_w_in, m_q_norm_g, m_k_norm_g, m_conv_w, m_w_branch_a, m_w_branch_b, m_w_out, m_ln2_g, m_w_ffn_gate, m_w_ffn_up, m_w_ffn_down, v_ada_w, v_ada_b, v_ln1_g, v_w_in, v_q_norm_g, v_k_norm_g, v_conv_w, v_w_branch_a, v_w_branch_b, v_w_out, v_ln2_g, v_w_ffn_gate, v_w_ffn_up, v_w_ffn_down):
    n_l, d, a4 = ada_w.shape
    cw4 = conv_w.shape[-1]
    ix, iy, ic = lax.axis_index("x"), lax.axis_index("y"), lax.axis_index("c")
    chip = 2 * ix + iy
    me = 2 * chip + ic

    big_w = dict(win=w_in, wa=w_branch_a, wb=w_branch_b, wo=w_out, wg=w_ffn_gate, wu=w_ffn_up, wd=w_ffn_down)
    big_m = dict(win=m_w_in, wa=m_w_branch_a, wb=m_w_branch_b, wo=m_w_out, wg=m_w_ffn_gate, wu=m_w_ffn_up,
                 wd=m_w_ffn_down)
    big_v = dict(win=v_w_in, wa=v_w_branch_a, wb=v_w_branch_b, wo=v_w_out, wg=v_w_ffn_gate, wu=v_w_ffn_up,
                 wd=v_w_ffn_down)

    def adam_view(a, k):
        return jnp.swapaxes(a, 1, 2) if k in ("wg", "wu") else a

    got = _gather8(_pack([c, conv_w])).reshape(N_DEV, -1)

    weight_groups = [(l, names) for l in range(n_l) for names in (("win",), ("wa", "wb", "wo", "wg", "wu", "wd"))]
    group_srcs = [[big_w[k][l].astype(BF16) for k in names] for l, names in weight_groups]
    started_w = {}

    def start_weights(gi):
        l, names = weight_groups[gi]
        copies = _weight_half_copies if gi == 0 else _weight_copies
        st = _split_start("weights_start_%d" % gi, copies, group_srcs[gi],
                          [(N_CHIP,) + sh.shape for sh in group_srcs[gi]], 3)
        for k in names:
            started_w[(l, k)] = [gi, names, st, None, copies]
        return st[4]

    got, group_srcs[0] = lax.optimization_barrier((got, group_srcs[0]))
    tie = start_weights(0)[0, 0]
    c_all = got[:, :d]
    conv_all = got[:, d:d + n_l * 3 * cw4].reshape(N_CHIP, 2, n_l, 3, cw4)[:, 0]
    conv_full = jnp.transpose(conv_all, (1, 2, 0, 3)).reshape(n_l, 3, N_CHIP * cw4)
    b_cols = lax.dynamic_slice_in_dim(ada_b, chip * a4, a4, axis=1).reshape(n_l, 1, a4)
    b_cols, group_srcs[1:] = lax.optimization_barrier((b_cols + tie, group_srcs[1:]))
    mod_cols = _ada_mod(c_all, ada_w, b_cols)
    mod_all = _gather8(_pack([mod_cols])).reshape(N_DEV, -1)[:, :n_l * N_DEV * a4]
    mod_all = mod_all.reshape(N_CHIP, 2, n_l, N_DEV, a4)[:, 0]
    mods = lax.dynamic_index_in_dim(mod_all, me, axis=2, keepdims=False)
    mods = jnp.transpose(mods, (1, 0, 2)).reshape(n_l, N_CHIP * a4)

    def weights(l, names, after):
        entry, tie = started_w[(l, names[0])], jnp.zeros((), F32)
        if entry[3] is None:
            lands = _split_wait("weights_wait_%d" % entry[0], entry[4], entry[2], after)
            if entry[4] is _weight_half_copies:
                passed = _split_start_in_place("weights_pass_start_%d" % entry[0], _weight_half_pass, lands, 3)
                lands = _split_wait_in_place("weights_pass_wait_%d" % entry[0], _weight_half_pass, passed, passed[3])
            nxt = entry[0] + 1
            if nxt < len(weight_groups):
                lands, group_srcs[nxt] = lax.optimization_barrier((lands, group_srcs[nxt]))
                tie = start_weights(nxt)[0, 0]
            lands = [lax.dynamic_update_index_in_dim(land, own, chip, 0) for land, own in zip(lands, entry[2][2])]
            for k in entry[1]:
                started_w[(l, k)][3] = dict(zip(entry[1], lands))
        return [started_w[(l, k)][3][k] for k in names], tie

    started_g, held_back = [], []

    def start_grads(l, grads, copies=_grad_copies, sems_per=7):
        names = tuple(grads)
        st = _split_start("grads_start_%d" % len(started_g), copies, [grads[k] for k in names],
                          [(N_DEV,) + grads[k].shape[1:] for k in names], sems_per)
        started_g.append((l, names, st, copies))
        return st[4][0, 0]

    def send_grads(l, grads):
        if l == 0 and tuple(grads) == GRAD_GROUPS[-1]:
            held_back.append(grads)
            return jnp.zeros((), F32)
        return start_grads(l, grads)

    loss_tile, grad_x, small = _local_step(
        x[0], loss_target[0], mods, ln1_g, ln2_g, q_norm_g, k_norm_g, conv_full, weights, send_grads)

    sm_shapes = [(n_l, 6 * d), (n_l, d), (n_l, d), (n_l, HEAD_DIM), (n_l, HEAD_DIM), (n_l, 3, d), (1,)]
    vec = _pack([jnp.stack([small[l][k] for l in range(n_l)]) for k in ("dmod", "ln1", "ln2", "qg", "kg", "conv")]
                + [loss_tile[0, 0:1]])
    n_vec = vec.shape[1] * 8
    all_vec = _gather8(vec).reshape(N_DEV, n_vec)
    all_vec, held_back = lax.optimization_barrier((all_vec, held_back))
    tie = sum([start_grads(0, grads, _grad_copies_same_core, 4) for grads in held_back], jnp.zeros((), F32))
    per_dev = [_unpack(all_vec[dev], sm_shapes) for dev in range(N_DEV)]
    dmod_all = jnp.stack([pd[0] for pd in per_dev])
    dmod_cols = jnp.transpose(lax.dynamic_slice_in_dim(dmod_all, chip * a4, a4, axis=2), (1, 0, 2))
    ada_out = _ada_grad_adam(jnp.transpose(c_all) + tie, dmod_cols, ada_w, m_ada_w, v_ada_w)

    dev_parts = jnp.stack([
        _pack([pd[0], pd[1], pd[2], pd[3], pd[4], lax.dynamic_slice_in_dim(pd[5], chip * cw4, cw4, axis=2), pd[6]])
        for pd in per_dev])
    zero1 = jnp.zeros((1,), F32)
    sw = _pack([ada_b, ln1_g, ln2_g, q_norm_g, k_norm_g, conv_w, zero1])
    sm = _pack([m_ada_b, m_ln1_g, m_ln2_g, m_q_norm_g, m_k_norm_g, m_conv_w, zero1])
    sv = _pack([v_ada_b, v_ln1_g, v_ln2_g, v_q_norm_g, v_k_norm_g, v_conv_w, zero1 + 1.0])
    out_shapes = [(n_l, 6 * d), (n_l, d), (n_l, d), (n_l, HEAD_DIM), (n_l, HEAD_DIM), (n_l, 3, cw4), (1,)]
    sm_out = [_unpack(o, out_shapes) for o in _small_adam(dev_parts, sw, sm, sv)]
    loss = 0.5 * sm_out[0][6][0] / d

    after = jnp.full((8, LANES), tie + sm_out[0][0][0, 0] + ada_out[0][0, 0, 0])
    big_out = {}
    for names in GRAD_GROUPS:
        got_parts = {}
        for gi, (l, sent, st, copies) in enumerate(started_g):
            if sent == names:
                parts = _split_wait("grads_wait_%d" % gi, copies, st, after)
                if copies is _grad_copies_same_core:
                    passed = _split_start_in_place("grads_pass_start_%d" % gi, _grad_pass_copies, parts, 3)
                    parts = _split_wait_in_place("grads_pass_wait_%d" % gi, _grad_pass_copies, passed, passed[3])
                for k, part, grad in zip(sent, parts, st[2]):
                    own = lax.dynamic_index_in_dim(grad, chip, 0, keepdims=False)
                    got_parts[(l, k)] = lax.dynamic_update_index_in_dim(part, own, me, 0)
        for k in names:
            res = _sum_adam([got_parts[(l, k)] for l in range(n_l)], adam_view(big_w[k], k), adam_view(big_m[k], k),
                            adam_view(big_v[k], k), "sum_adam_" + k)
            after = res[0]
            big_out[k] = [adam_view(r, k) for r in res]

    outs = [loss, grad_x[None]]
    for kind in range(4):
        sm_k = sm_out[kind]
        outs += [ada_out[kind], sm_k[0], sm_k[1], big_out["win"][kind], sm_k[3], sm_k[4], sm_k[5],
                 big_out["wa"][kind], big_out["wb"][kind], big_out["wo"][kind], sm_k[2],
                 big_out["wg"][kind], big_out["wu"][kind], big_out["wd"][kind]]
    return tuple(outs)
```

```python
import math

import jax
import jax.numpy as jnp
from jax import lax
from jax.experimental import pallas as pl
from jax.experimental.pallas import tpu as pltpu

F32 = jnp.float32
BF16 = jnp.bfloat16
MESH_ID = pl.DeviceIdType.MESH

EPS = 1e-6
HEAD_DIM = 64
Q_BLOCK = 128
Q_SUPER = 1024
Q_SUPER_BWD = 1024
KEY_UNROLL = 8
LANES = 128
N_DEV = 8
N_CHIP = 4
VMEM_LIMIT_BYTES = 56 * 1024 * 1024

ADAM_LR = 0.001
ADAM_B1 = 0.9
ADAM_B2 = 0.999
ADAM_EPS = 1e-08
ADAM_WD = 0.01
ADAM_STEP = 10

HBM_SPEC = pl.BlockSpec(memory_space=pltpu.HBM)
ANY_SPEC = pl.BlockSpec(memory_space=pl.ANY)
SEM_SPEC = pl.BlockSpec(memory_space=pltpu.SEMAPHORE)
VMEM_SPEC = pl.BlockSpec(memory_space=pltpu.VMEM)
SIDE_EFFECT = pltpu.SideEffectType.DATAFLOW_SIDE_EFFECTING


def _params(*sem):
    return pltpu.CompilerParams(dimension_semantics=tuple(sem), vmem_limit_bytes=VMEM_LIMIT_BYTES)


def _tile(n, pref):
    return pref if n % pref == 0 else n


def _dot(a, b):
    return jnp.dot(a, b, preferred_element_type=F32)


def _dot_nt(a, b):
    return lax.dot_general(a, b, (((1,), (1,)), ((), ())), preferred_element_type=F32)


def _dot_tn(a, b):
    return lax.dot_general(a, b, (((0,), (0,)), ((), ())), preferred_element_type=F32)


def _adamw(w, g, m, v):
    m = ADAM_B1 * m + (1.0 - ADAM_B1) * g
    v = ADAM_B2 * v + (1.0 - ADAM_B2) * (g * g)
    m_hat = m / (1.0 - ADAM_B1 ** ADAM_STEP)
    v_hat = v / (1.0 - ADAM_B2 ** ADAM_STEP)
    delta = -ADAM_LR * (m_hat / (jnp.sqrt(v_hat) + ADAM_EPS) + ADAM_WD * w)
    return delta, m, v


def _hbm(a):
    return pltpu.with_memory_space_constraint(a, pltpu.HBM)


def _peer(x, y, c, k):
    return (1 - x if k & 4 else x, 1 - y if k & 2 else y, 1 - c if k & 1 else c)


def _gather8(v):
    rows_per, m = v.shape

    def body(v_ref, out_ref, send_sems, recv_sems, local_sem):
        x, y, c = lax.axis_index("x"), lax.axis_index("y"), lax.axis_index("c")

        def rows(p):
            return out_ref.at[pl.ds((4 * p[0] + 2 * p[1] + p[2]) * rows_per, rows_per), :]

        me = (x, y, c)
        mine = pltpu.make_async_copy(v_ref, rows(me), local_sem)
        mine.start()
        sends = []
        for k in range(1, N_DEV):
            cp = pltpu.make_async_remote_copy(
                src_ref=v_ref, dst_ref=rows(me), send_sem=send_sems.at[k - 1], recv_sem=recv_sems.at[k - 1],
                device_id=_peer(x, y, c, k), device_id_type=MESH_ID)
            cp.start()
            sends.append(cp)
        for k in range(1, N_DEV):
            pltpu.make_async_remote_copy(
                src_ref=v_ref, dst_ref=rows(_peer(x, y, c, k)), send_sem=send_sems.at[k - 1],
                recv_sem=recv_sems.at[k - 1], device_id=_peer(x, y, c, k), device_id_type=MESH_ID).wait_recv()
        for cp in sends:
            cp.wait_send()
        mine.wait()

    return pl.pallas_call(
        body, name="gather8",
        out_shape=jax.ShapeDtypeStruct((N_DEV * rows_per, m), v.dtype),
        in_specs=[VMEM_SPEC], out_specs=VMEM_SPEC,
        scratch_shapes=[pltpu.SemaphoreType.DMA((N_DEV - 1,)), pltpu.SemaphoreType.DMA((N_DEV - 1,)),
                        pltpu.SemaphoreType.DMA],
    )(v)


def _weight_copies(srcs, lands, send_sems, recv_sems):
    x, y, c = lax.axis_index("x"), lax.axis_index("y"), lax.axis_index("c")
    chips = [(1 - x, y), (x, 1 - y), (1 - x, 1 - y)]
    sends, recvs = [], []
    for a, (src, land) in enumerate(zip(srcs, lands)):
        for j, (px, py) in enumerate(chips):
            def copy(dst_block, a=a, j=j, px=px, py=py, src=src, land=land):
                return pltpu.make_async_remote_copy(
                    src_ref=src, dst_ref=land.at[dst_block], send_sem=send_sems.at[3 * a + j],
                    recv_sem=recv_sems.at[3 * a + j], device_id=(px, py, c), device_id_type=MESH_ID)
            sends.append(copy(2 * x + y))
            recvs.append(copy(2 * px + py))
    return sends, recvs


def _weight_half_copies(srcs, lands, send_sems, recv_sems):
    x, y, c = lax.axis_index("x"), lax.axis_index("y"), lax.axis_index("c")
    chips = [(1 - x, y), (x, 1 - y), (1 - x, 1 - y)]
    sends, recvs = [], []
    for a, (src, land) in enumerate(zip(srcs, lands)):
        half = src.shape[0] // 2
        rows = pl.ds(c * half, half)
        for j, (px, py) in enumerate(chips):
            def copy(dst_block, a=a, j=j, px=px, py=py, src=src, land=land, rows=rows):
                return pltpu.make_async_remote_copy(
                    src_ref=src.at[rows], dst_ref=land.at[dst_block, rows], send_sem=send_sems.at[3 * a + j],
                    recv_sem=recv_sems.at[3 * a + j], device_id=(px, py, c), device_id_type=MESH_ID)
            sends.append(copy(2 * x + y))
            recvs.append(copy(2 * px + py))
    return sends, recvs


def _weight_half_pass(lands, same_lands, send_sems, recv_sems):
    del same_lands
    x, y, c = lax.axis_index("x"), lax.axis_index("y"), lax.axis_index("c")
    chips = [(1 - x, y), (x, 1 - y), (1 - x, 1 - y)]
    sends, recvs = [], []
    for a, land in enumerate(lands):
        half = land.shape[1] // 2
        for j, (px, py) in enumerate(chips):
            def copy(pc, a=a, j=j, px=px, py=py, land=land, half=half):
                part = land.at[2 * px + py, pl.ds(pc * half, half)]
                return pltpu.make_async_remote_copy(
                    src_ref=part, dst_ref=part, send_sem=send_sems.at[3 * a + j], recv_sem=recv_sems.at[3 * a + j],
                    device_id=(x, y, 1 - c), device_id_type=MESH_ID)
            sends.append(copy(c))
            recvs.append(copy(1 - c))
    return sends, recvs


def _split_start(name, copies, srcs, land_shapes, sems_per_src):
    n = len(srcs)

    def body(*refs):
        sends, _ = copies(refs[:n], refs[n + 2:2 * n + 2], refs[n], refs[n + 1])
        for cp in sends:
            cp.start()
        token = refs[-1]
        token[...] = jnp.zeros_like(token)

    n_sems = sems_per_src * n
    outs = pl.pallas_call(
        body, name=name,
        out_shape=(pltpu.SemaphoreType.DMA((n_sems,)), pltpu.SemaphoreType.DMA((n_sems,)),
                   *[pltpu.HBM(shape, a.dtype) for a, shape in zip(srcs, land_shapes)],
                   jax.ShapeDtypeStruct((8, LANES), F32)),
        in_specs=[HBM_SPEC] * n, out_specs=(SEM_SPEC, SEM_SPEC, *[HBM_SPEC] * n, VMEM_SPEC),
        compiler_params=pltpu.CompilerParams(has_side_effects=SIDE_EFFECT),
    )(*[_hbm(a) for a in srcs])
    return outs[0], outs[1], list(srcs), list(outs[2:2 + n]), outs[-1]


def _split_wait(name, copies, started, after):
    send_sems, recv_sems, srcs, lands, _ = started
    n = len(srcs)

    def body(*refs):
        sends, recvs = copies(refs[:n], refs[n:2 * n], refs[2 * n], refs[2 * n + 1])
        for cp in sends:
            cp.wait_send()
        for cp in recvs:
            cp.wait_recv()

    return pl.pallas_call(
        body, name=name,
        out_shape=tuple(pltpu.HBM(a.shape, a.dtype) for a in lands),
        in_specs=[HBM_SPEC] * (2 * n) + [SEM_SPEC, SEM_SPEC, ANY_SPEC], out_specs=tuple([HBM_SPEC] * n),
        input_output_aliases={n + i: i for i in range(n)},
        compiler_params=pltpu.CompilerParams(has_side_effects=SIDE_EFFECT),
    )(*srcs, *lands, send_sems, recv_sems, after)


def _split_start_in_place(name, copies, bufs, sems_per_buf):
    n = len(bufs)

    def body(*refs):
        sends, _ = copies(refs[:n], refs[:n], refs[n], refs[n + 1])
        for cp in sends:
            cp.start()
        token = refs[-1]
        token[...] = jnp.zeros_like(token)

    n_sems = sems_per_buf * n
    outs = pl.pallas_call(
        body, name=name,
        out_shape=(pltpu.SemaphoreType.DMA((n_sems,)), pltpu.SemaphoreType.DMA((n_sems,)),
                   *[pltpu.HBM(a.shape, a.dtype) for a in bufs], jax.ShapeDtypeStruct((8, LANES), F32)),
        in_specs=[HBM_SPEC] * n, out_specs=(SEM_SPEC, SEM_SPEC, *[HBM_SPEC] * n, VMEM_SPEC),
        input_output_aliases={i: 2 + i for i in range(n)},
        compiler_params=pltpu.CompilerParams(has_side_effects=SIDE_EFFECT),
    )(*[_hbm(a) for a in bufs])
    return outs[0], outs[1], list(outs[2:2 + n]), outs[-1]


def _split_wait_in_place(name, copies, started, after):
    send_sems, recv_sems, bufs, _ = started
    n = len(bufs)

    def body(*refs):
        sends, recvs = copies(refs[:n], refs[:n], refs[n], refs[n + 1])
        for cp in sends:
            cp.wait_send()
        for cp in recvs:
            cp.wait_recv()

    return pl.pallas_call(
        body, name=name,
        out_shape=tuple(pltpu.HBM(a.shape, a.dtype) for a in bufs),
        in_specs=[HBM_SPEC] * n + [SEM_SPEC, SEM_SPEC, ANY_SPEC], out_specs=tuple([HBM_SPEC] * n),
        input_output_aliases={i: i for i in range(n)},
        compiler_params=pltpu.CompilerParams(has_side_effects=SIDE_EFFECT),
    )(*bufs, send_sems, recv_sems, after)


def _grad_copies(grads, parts, send_sems, recv_sems):
    x, y, c = lax.axis_index("x"), lax.axis_index("y"), lax.axis_index("c")
    chips = [(1 - x, y), (x, 1 - y), (1 - x, 1 - y)]
    my_slot = 4 * x + 2 * y + c
    sends, recvs = [], []
    for a, (grad, part) in enumerate(zip(grads, parts)):
        def copy(k, block, slot, to, a=a, grad=grad, part=part):
            return pltpu.make_async_remote_copy(
                src_ref=grad.at[block], dst_ref=part.at[slot], send_sem=send_sems.at[7 * a + k],
                recv_sem=recv_sems.at[7 * a + k], device_id=to, device_id_type=MESH_ID)
        sends.append(copy(0, 2 * x + y, my_slot, (x, y, 1 - c)))
        recvs.append(copy(0, 2 * x + y, 4 * x + 2 * y + (1 - c), (x, y, 1 - c)))
        for j, (px, py) in enumerate(chips):
            for other, pc in enumerate((c, 1 - c)):
                sends.append(copy(1 + 2 * j + other, 2 * px + py, my_slot, (px, py, pc)))
                recvs.append(copy(1 + 2 * j + other, 2 * x + y, 4 * px + 2 * py + pc, (px, py, pc)))
    return sends, recvs


def _grad_copies_same_core(grads, parts, send_sems, recv_sems):
    x, y, c = lax.axis_index("x"), lax.axis_index("y"), lax.axis_index("c")
    chips = [(1 - x, y), (x, 1 - y), (1 - x, 1 - y)]
    my_slot = 4 * x + 2 * y + c
    sends, recvs = [], []
    for a, (grad, part) in enumerate(zip(grads, parts)):
        def copy(k, block, slot, to, a=a, grad=grad, part=part):
            return pltpu.make_async_remote_copy(
                src_ref=grad.at[block], dst_ref=part.at[slot], send_sem=send_sems.at[4 * a + k],
                recv_sem=recv_sems.at[4 * a + k], device_id=to, device_id_type=MESH_ID)
        sends.append(copy(0, 2 * x + y, my_slot, (x, y, 1 - c)))
        recvs.append(copy(0, 2 * x + y, 4 * x + 2 * y + (1 - c), (x, y, 1 - c)))
        for j, (px, py) in enumerate(chips):
            sends.append(copy(1 + j, 2 * px + py, my_slot, (px, py, c)))
            recvs.append(copy(1 + j, 2 * x + y, 4 * px + 2 * py + c, (px, py, c)))
    return sends, recvs


def _grad_pass_copies(parts, same_parts, send_sems, recv_sems):
    del same_parts
    x, y, c = lax.axis_index("x"), lax.axis_index("y"), lax.axis_index("c")
    chips = [(1 - x, y), (x, 1 - y), (1 - x, 1 - y)]
    sends, recvs = [], []
    for a, part in enumerate(parts):
        for j, (px, py) in enumerate(chips):
            def copy(pc, a=a, j=j, px=px, py=py, part=part):
                slot = part.at[4 * px + 2 * py + pc]
                return pltpu.make_async_remote_copy(
                    src_ref=slot, dst_ref=slot, send_sem=send_sems.at[3 * a + j], recv_sem=recv_sems.at[3 * a + j],
                    device_id=(x, y, 1 - c), device_id_type=MESH_ID)
            sends.append(copy(c))
            recvs.append(copy(1 - c))
    return sends, recvs


def _ada_mod(c_all, ada_w, ada_b_cols):
    n_l, d, a4 = ada_w.shape
    tn = _tile(a4, 512)

    def body(c_ref, w_ref, b_ref, o_ref):
        cv = c_ref[...]
        ca = (cv * jax.nn.sigmoid(cv)).astype(BF16)
        o_ref[...] = _dot(ca, w_ref[...].astype(BF16)) + b_ref[...]

    return pl.pallas_call(
        body, name="ada_mod", grid=(n_l, a4 // tn),
        in_specs=[pl.BlockSpec((N_DEV, d), lambda l, j: (0, 0)),
                  pl.BlockSpec((None, d, tn), lambda l, j: (l, 0, j)),
                  pl.BlockSpec((None, 1, tn), lambda l, j: (l, 0, j))],
        out_specs=pl.BlockSpec((None, N_DEV, tn), lambda l, j: (l, 0, j)),
        out_shape=jax.ShapeDtypeStruct((n_l, N_DEV, a4), F32),
        compiler_params=_params("parallel", "parallel"),
    )(c_all, ada_w, ada_b_cols)


def _ada_grad_adam(c_all_t, dmod_cols, w, m, v):
    n_l, d, a4 = w.shape
    tn = _tile(a4, 512)

    def body(ct_ref, dm_ref, w_ref, m_ref, v_ref, g_ref, dl_ref, nm_ref, nv_ref):
        ct = ct_ref[...]
        ca = ct * jax.nn.sigmoid(ct)
        dm = dm_ref[...]
        g = ca[:, 0:1] * dm[0:1, :]
        for dev in range(1, N_DEV):
            g = g + ca[:, dev:dev + 1] * dm[dev:dev + 1, :]
        g_ref[...] = g
        delta, nm, nv = _adamw(w_ref[...], g, m_ref[...], v_ref[...])
        dl_ref[...] = delta
        nm_ref[...] = nm
        nv_ref[...] = nv

    wspec = pl.BlockSpec((None, d, tn), lambda l, j: (l, 0, j))
    shp = jax.ShapeDtypeStruct(w.shape, F32)
    return pl.pallas_call(
        body, name="ada_grad_adam", grid=(n_l, a4 // tn),
        in_specs=[pl.BlockSpec((d, N_DEV), lambda l, j: (0, 0)),
                  pl.BlockSpec((None, N_DEV, tn), lambda l, j: (l, 0, j)), wspec, wspec, wspec],
        out_specs=[wspec] * 4, out_shape=[shp] * 4,
        compiler_params=_params("parallel", "parallel"),
    )(c_all_t, dmod_cols, w, m, v)


def _lnmod(x, g, sc, sh):
    s, d = x.shape
    tm = _tile(s, 1024)

    def body(x_ref, g_ref, sc_ref, sh_ref, h_ref):
        xv = x_ref[...]
        r = lax.rsqrt(jnp.mean(xv * xv, axis=-1, keepdims=True) + EPS)
        h_ref[...] = ((xv * r * g_ref[...]) * (1.0 + sc_ref[...]) + sh_ref[...]).astype(BF16)

    vec = pl.BlockSpec((1, d), lambda i: (0, 0))
    row = pl.BlockSpec((tm, d), lambda i: (i, 0))
    return pl.pallas_call(
        body, name="lnmod", grid=(s // tm,), in_specs=[row, vec, vec, vec], out_specs=row,
        out_shape=jax.ShapeDtypeStruct((s, d), BF16), compiler_params=_params("parallel"),
    )(x, g, sc, sh)


def _mm_in(h, w_g):
    s, d = h.shape
    n4 = w_g.shape[-1]
    tm = _tile(s, 1024)

    def body(a_ref, b_ref, o_ref):
        o_ref[...] = _dot(a_ref[...], b_ref[...]).astype(BF16)

    return pl.pallas_call(
        body, name="mm_in", grid=(N_CHIP, s // tm),
        in_specs=[pl.BlockSpec((tm, d), lambda j, i: (i, 0)),
                  pl.BlockSpec((None, d, n4), lambda j, i: (j, 0, 0))],
        out_specs=pl.BlockSpec((tm, n4), lambda j, i: (i, j)),
        out_shape=jax.ShapeDtypeStruct((s, N_CHIP * n4), BF16),
        compiler_params=_params("parallel", "parallel"),
    )(h, w_g)


def _pair_mean(x, low):
    lo = jnp.sum(jnp.where(low, x, 0.0), axis=-1, keepdims=True)
    hi = jnp.sum(jnp.where(low, 0.0, x), axis=-1, keepdims=True)
    return jnp.where(low, lo, hi) * (1.0 / HEAD_DIM)


def _pair_norm(x, low):
    r = lax.rsqrt(_pair_mean(x * x, low) + EPS)
    return x * r, r


def _log_not(z):
    nz = -z
    return jnp.minimum(nz, 0.0) - jnp.log(1.0 + jnp.exp(jnp.minimum(z, nz)))


def _attn_consts(inclusive):
    low = lax.broadcasted_iota(jnp.int32, (1, LANES), 1) < HEAD_DIM
    row = lax.broadcasted_iota(jnp.int32, (Q_BLOCK, Q_BLOCK), 0)
    col = lax.broadcasted_iota(jnp.int32, (Q_BLOCK, Q_BLOCK), 1)
    tri = (row <= col) if inclusive else (row > col)
    w2 = jnp.concatenate([tri.astype(BF16), jnp.ones((Q_BLOCK, Q_BLOCK), BF16)], axis=1)
    return low, col < row, jnp.concatenate([w2, w2], axis=0)


def _split_cat(v):
    hi = v.astype(BF16)
    return jnp.concatenate([hi, (v - hi.astype(F32)).astype(BF16)], axis=1)


def _fill_pair_blocks(dst, src_fn, low, n_kb):
    def fill(b, _):
        v = src_fn(pl.ds(pl.multiple_of(b * Q_BLOCK, Q_BLOCK), Q_BLOCK))
        dst[b, 0:Q_BLOCK, :] = jnp.where(low, v, 0.0).astype(BF16)
        dst[b, Q_BLOCK:2 * Q_BLOCK, :] = jnp.where(low, 0.0, v).astype(BF16)
        return 0

    lax.fori_loop(0, n_kb, fill, 0)


def _attn_fwd(p, qg2, kg2, d):
    s = p.shape[0]
    n_pairs = d // LANES
    qsb = _tile(s, Q_SUPER)
    n_sub, n_sb, n_kb = qsb // Q_BLOCK, s // qsb, s // Q_BLOCK
    unroll = math.gcd(KEY_UNROLL, n_sub)
    chunk = _tile(s, 512)
    inv_sqrt = 1.0 / math.sqrt(HEAD_DIM)

    def body(q_ref, k_ref, v_ref, qg_ref, kg_ref, o_ref, lt_ref, qs, k2, v2, run, acc):
        low, causal, w4 = _attn_consts(False)

        def prep(r, _):
            rows = pl.ds(pl.multiple_of(r * chunk, chunk), chunk)
            qs[rows, :] = (_pair_norm(q_ref[rows, :].astype(F32), low)[0] * (qg_ref[...] * inv_sqrt)).astype(BF16)
            return 0

        lax.fori_loop(0, s // chunk, prep, 0)
        _fill_pair_blocks(k2, lambda rows: _pair_norm(k_ref[rows, :].astype(F32), low)[0] * kg_ref[...], low, n_kb)
        _fill_pair_blocks(v2, lambda rows: v_ref[rows, :].astype(F32), low, n_kb)

        def step(sb, j, t0=0, diag_t=None):
            rows = pl.ds(pl.multiple_of(sb * qsb + t0 * Q_BLOCK, Q_BLOCK), (n_sub - t0) * Q_BLOCK)
            z_both = _dot_nt(qs[rows, :], k2[j])
            zls, cats = [], []
            for t in range(t0, n_sub):
                sub = slice((t - t0) * Q_BLOCK, (t - t0 + 1) * Q_BLOCK)
                for h in range(2):
                    z = z_both[sub, h * LANES:(h + 1) * LANES]
                    ln = _log_not(z)
                    if t == diag_t:
                        ln = jnp.where(causal, ln, 0.0)
                    zls.append(z + ln)
                    cats.append(_split_cat(ln))
            c2 = _dot(jnp.concatenate(cats, axis=0), w4)
            a_rows = []
            for t in range(t0, n_sub):
                sub = slice(t * Q_BLOCK, (t + 1) * Q_BLOCK)
                a_pair = []
                for h in range(2):
                    i = 2 * (t - t0) + h
                    tile = slice(i * Q_BLOCK, (i + 1) * Q_BLOCK)
                    later = run[h, sub, :]
                    log_a = zls[i] + c2[tile, :LANES] + later
                    if t == diag_t:
                        log_a = jnp.where(causal, log_a, -1e30)
                    a_pair.append(jnp.exp(log_a).astype(BF16))
                    run[h, sub, :] = later + c2[tile, LANES:]
                a_rows.append(jnp.concatenate(a_pair, axis=1))
            acc[t0 * Q_BLOCK:, :] += _dot(jnp.concatenate(a_rows, axis=0), v2[j])

        def super_block(sb, _):
            run[...] = jnp.zeros_like(run)
            acc[...] = jnp.zeros_like(acc)
            for t in reversed(range(n_sub)):
                step(sb, sb * n_sub + t, t0=t, diag_t=t)

            def below(n, _):
                for u in range(unroll):
                    step(sb, sb * n_sub - 1 - (unroll * n + u))
                return 0

            lax.fori_loop(0, sb * (n_sub // unroll), below, 0)
            rows_sb = pl.ds(pl.multiple_of(sb * qsb, qsb), qsb)
            o_ref[rows_sb, :] = acc[...].astype(BF16)
            lt_ref[rows_sb, :] = jnp.where(low, run[0], run[1])
            return 0

        lax.fori_loop(0, n_sb, super_block, 0)

    def seg(k):
        return pl.BlockSpec((s, LANES), lambda h, k=k: (0, k * n_pairs + h))

    vec = pl.BlockSpec((1, LANES), lambda h: (0, 0))
    out = pl.BlockSpec((s, LANES), lambda h: (0, h))
    return pl.pallas_call(
        body, name="attn_fwd", grid=(n_pairs,),
        in_specs=[seg(0), seg(1), seg(2), vec, vec], out_specs=[out, out],
        out_shape=[jax.ShapeDtypeStruct((s, d), BF16), jax.ShapeDtypeStruct((s, d), F32)],
        scratch_shapes=[pltpu.VMEM((s, LANES), BF16)] + [pltpu.VMEM((n_kb, 2 * Q_BLOCK, LANES), BF16)] * 2
        + [pltpu.VMEM((2, qsb, LANES), F32), pltpu.VMEM((qsb, LANES), F32)],
        compiler_params=_params("parallel"),
    )(p, p, p, qg2, kg2)


def _conv_rows(s):
    return _tile(s, 512)


def _conv_fwd(p, conv_w, d):
    s = p.shape[0]
    nb = d // LANES
    rows_n = _conv_rows(s)

    def body(cb_ref, cc_ref, cx_ref, w_ref, y_ref, us):
        us[pl.ds(0, 8), :] = jnp.zeros((8, LANES), F32)

        def fill(r, _):
            rows = pl.ds(pl.multiple_of(r * rows_n, rows_n), rows_n)
            us[pl.ds(pl.multiple_of(r * rows_n + 8, 8), rows_n), :] = cc_ref[rows, :].astype(F32) * cx_ref[rows, :].astype(F32)
            return 0

        lax.fori_loop(0, s // rows_n, fill, 0)
        w = w_ref[...]

        def out(r, _):
            rows = pl.ds(pl.multiple_of(r * rows_n, rows_n), rows_n)
            ext = us[pl.ds(pl.multiple_of(r * rows_n, 8), rows_n + 8), :]
            cv = (w[0:1, :] * pltpu.roll(ext, 2, 0)[8:, :] + w[1:2, :] * pltpu.roll(ext, 1, 0)[8:, :]
                  + w[2:3, :] * ext[8:, :])
            y_ref[rows, :] = (cb_ref[rows, :].astype(F32) * cv).astype(BF16)
            return 0

        lax.fori_loop(0, s // rows_n, out, 0)

    def seg(k):
        return pl.BlockSpec((s, LANES), lambda b, k=k: (0, k * nb + b))

    return pl.pallas_call(
        body, name="conv_fwd", grid=(nb,),
        in_specs=[seg(3), seg(4), seg(5), pl.BlockSpec((3, LANES), lambda b: (0, b))],
        out_specs=pl.BlockSpec((s, LANES), lambda b: (0, b)),
        out_shape=jax.ShapeDtypeStruct((s, d), BF16),
        scratch_shapes=[pltpu.VMEM((s + 8, LANES), F32)],
        compiler_params=_params("parallel"),
    )(p, p, p, conv_w)


def _branch(ya, yb, p, wa, wb, d):
    s = ya.shape[0]
    tm = _tile(s, 512)

    def body(ya_ref, yb_ref, ga_ref, gb_ref, wa_ref, wb_ref, m_ref, a_ref, b_ref):
        pa = _dot(ya_ref[...], wa_ref[...])
        pb = _dot(yb_ref[...], wb_ref[...])
        ga, gb = ga_ref[...].astype(F32), gb_ref[...].astype(F32)
        m_ref[...] = (jax.nn.sigmoid(ga) * pa + jax.nn.sigmoid(gb) * pb).astype(BF16)
        a_ref[...] = pa.astype(BF16)
        b_ref[...] = pb.astype(BF16)

    row = pl.BlockSpec((tm, d), lambda i: (i, 0))
    wsp = pl.BlockSpec((d, d), lambda i: (0, 0))
    shp = jax.ShapeDtypeStruct((s, d), BF16)
    return pl.pallas_call(
        body, name="branch", grid=(s // tm,),
        in_specs=[row, row, pl.BlockSpec((tm, d), lambda i: (i, 6)), pl.BlockSpec((tm, d), lambda i: (i, 7)), wsp, wsp],
        out_specs=[row, row, row], out_shape=[shp, shp, shp], compiler_params=_params("parallel"),
    )(ya, yb, p, p, wa, wb)


def _out_proj(merged, wout, x0, g1, ln_g, sc, sh):
    s, d = x0.shape
    tm = _tile(s, 1024)

    def body(m_ref, w_ref, x_ref, g_ref, lg_ref, sc_ref, sh_ref, x1_ref, mo_ref, h_ref):
        mo = _dot(m_ref[...], w_ref[...])
        mo_ref[...] = mo
        x1 = x_ref[...] + g_ref[...] * mo
        x1_ref[...] = x1
        r = lax.rsqrt(jnp.mean(x1 * x1, axis=-1, keepdims=True) + EPS)
        h_ref[...] = ((x1 * r * lg_ref[...]) * (1.0 + sc_ref[...]) + sh_ref[...]).astype(BF16)

    row = pl.BlockSpec((tm, d), lambda i: (i, 0))
    vec = pl.BlockSpec((1, d), lambda i: (0, 0))
    shp = jax.ShapeDtypeStruct((s, d), F32)
    return pl.pallas_call(
        body, name="out_proj", grid=(s // tm,),
        in_specs=[row, pl.BlockSpec((d, d), lambda i: (0, 0)), row, vec, vec, vec, vec],
        out_specs=[row, row, row], out_shape=[shp, shp, jax.ShapeDtypeStruct((s, d), BF16)],
        compiler_params=_params("parallel"),
    )(merged, wout, x0, g1, ln_g, sc, sh)


def _ffn_up(h, wg_g, wu_g):
    s, d = h.shape
    f4 = wg_g.shape[-1]
    tm = _tile(s, 1024)

    def body(h_ref, wg_ref, wu_ref, gate_ref, up_ref, act_ref):
        hv = h_ref[...]
        gt = _dot(hv, wg_ref[...])
        up = _dot(hv, wu_ref[...])
        gate_ref[...] = gt.astype(BF16)
        up_ref[...] = up.astype(BF16)
        act_ref[...] = (gt * jax.nn.sigmoid(gt) * up).astype(BF16)

    wsp = pl.BlockSpec((None, d, f4), lambda j, i: (j, 0, 0))
    osp = pl.BlockSpec((None, tm, f4), lambda j, i: (j, i, 0))
    shp = jax.ShapeDtypeStruct((N_CHIP, s, f4), BF16)
    return pl.pallas_call(
        body, name="ffn_up", grid=(N_CHIP, s // tm),
        in_specs=[pl.BlockSpec((tm, d), lambda j, i: (i, 0)), wsp, wsp],
        out_specs=[osp, osp, osp], out_shape=[shp, shp, shp], compiler_params=_params("parallel", "parallel"),
    )(h, wg_g, wu_g)


def _ffn_down(act, wd_g, x1, g2):
    s, d = x1.shape
    f4 = act.shape[-1]
    tm = _tile(s, 1024)

    def body(a_ref, w_ref, x_ref, g_ref, x2_ref, f_ref, acc):
        j = pl.program_id(1)

        @pl.when(j == 0)
        def _():
            acc[...] = jnp.zeros_like(acc)

        acc[...] += _dot(a_ref[...], w_ref[...])

        @pl.when(j == N_CHIP - 1)
        def _():
            f = acc[...]
            f_ref[...] = f
            x2_ref[...] = x_ref[...] + g_ref[...] * f

    row = pl.BlockSpec((tm, d), lambda i, j: (i, 0))
    shp = jax.ShapeDtypeStruct((s, d), F32)
    return pl.pallas_call(
        body, name="ffn_down", grid=(s // tm, N_CHIP),
        in_specs=[pl.BlockSpec((None, tm, f4), lambda i, j: (j, i, 0)),
                  pl.BlockSpec((None, f4, d), lambda i, j: (j, 0, 0)),
                  row, pl.BlockSpec((1, d), lambda i, j: (0, 0))],
        out_specs=[row, row], out_shape=[shp, shp],
        scratch_shapes=[pltpu.VMEM((tm, d), F32)], compiler_params=_params("parallel", "arbitrary"),
    )(act, wd_g, x1, g2)


def _loss_head(y, target):
    s, d = y.shape
    tm = _tile(s, 1024)
    n_steps = s // tm

    def body(y_ref, t_ref, dy_ref, l_ref, acc):
        i = pl.program_id(0)

        @pl.when(i == 0)
        def _():
            acc[...] = jnp.zeros_like(acc)

        err = y_ref[...] - t_ref[...]
        dy_ref[...] = err / d
        acc[...] += jnp.sum(err * err, axis=0, keepdims=True)

        @pl.when(i == n_steps - 1)
        def _():
            l_ref[...] = jnp.broadcast_to(jnp.sum(acc[...], axis=1, keepdims=True), (8, LANES))

    row = pl.BlockSpec((tm, d), lambda i: (i, 0))
    return pl.pallas_call(
        body, name="loss_head", grid=(n_steps,), in_specs=[row, row],
        out_specs=[row, pl.BlockSpec((8, LANES), lambda i: (0, 0))],
        out_shape=[jax.ShapeDtypeStruct((s, d), F32), jax.ShapeDtypeStruct((8, LANES), F32)],
        scratch_shapes=[pltpu.VMEM((1, d), F32)], compiler_params=_params("arbitrary"),
    )(y, target)


def _mm_tn(a, b, a_spec, b_spec, out_rc, name):
    r, c = out_rc
    s = a.shape[-2]
    tk = _tile(s, 1024)
    nk = s // tk

    def body(a_ref, b_ref, o_ref, acc):
        k = pl.program_id(1)

        @pl.when(k == 0)
        def _():
            acc[...] = jnp.zeros_like(acc)

        acc[...] += _dot_tn(a_ref[...], b_ref[...])

        @pl.when(k == nk - 1)
        def _():
            o_ref[...] = acc[...].astype(BF16)

    return pl.pallas_call(
        body, name=name, grid=(N_CHIP, nk),
        in_specs=[pl.BlockSpec(*a_spec(tk)), pl.BlockSpec(*b_spec(tk))],
        out_specs=pl.BlockSpec((None, r, c), lambda j, k: (j, 0, 0)),
        out_shape=jax.ShapeDtypeStruct((N_CHIP, r, c), BF16),
        scratch_shapes=[pltpu.VMEM((r, c), F32)], compiler_params=_params("parallel", "arbitrary"),
    )(a, b)


def _mm_tn_square(a, b, name):
    s, d = a.shape
    r4 = d // N_CHIP
    tk = _tile(s, 1024)
    nk = s // tk

    def body(a_ref, b_ref, o_ref, acc):
        k = pl.program_id(0)

        @pl.when(k == 0)
        def _():
            acc[...] = jnp.zeros_like(acc)

        acc[...] += _dot_tn(a_ref[...], b_ref[...])

        @pl.when(k == nk - 1)
        def _():
            for j in range(N_CHIP):
                o_ref[j] = acc[j * r4:(j + 1) * r4, :].astype(BF16)

    blk = pl.BlockSpec((tk, d), lambda k: (k, 0))
    return pl.pallas_call(
        body, name=name, grid=(nk,), in_specs=[blk, blk],
        out_specs=pl.BlockSpec((N_CHIP, r4, d), lambda k: (0, 0, 0)),
        out_shape=jax.ShapeDtypeStruct((N_CHIP, r4, d), BF16),
        scratch_shapes=[pltpu.VMEM((d, d), F32)], compiler_params=_params("arbitrary"),
    )(a, b)


def _ffn_bwd1(dx2, f, g2, wd_g, gate, up):
    s, d = dx2.shape
    f4 = gate.shape[-1]
    tm = _tile(s, 1024)

    def body(dx_ref, f_ref, g_ref, w_ref, gate_ref, up_ref, dgate_ref, dup_ref, df_ref, dg_ref):
        i, j = pl.program_id(0), pl.program_id(1)

        @pl.when((i == 0) & (j == 0))
        def _():
            dg_ref[...] = jnp.zeros_like(dg_ref)

        dxv = dx_ref[...]
        df = (g_ref[...] * dxv).astype(BF16)

        @pl.when(j == 0)
        def _():
            df_ref[...] = df
            dg_ref[0:1, :] += jnp.sum(dxv * f_ref[...], axis=0, keepdims=True)

        da = _dot_nt(df, w_ref[...])
        gt = gate_ref[...].astype(F32)
        sg = jax.nn.sigmoid(gt)
        dup_ref[...] = (da * gt * sg).astype(BF16)
        dgate_ref[...] = (da * up_ref[...].astype(F32) * (sg * (1.0 + gt * (1.0 - sg)))).astype(BF16)

    row = pl.BlockSpec((tm, d), lambda i, j: (i, 0))
    hsp = pl.BlockSpec((None, tm, f4), lambda i, j: (j, i, 0))
    hshp = jax.ShapeDtypeStruct((N_CHIP, s, f4), BF16)
    return pl.pallas_call(
        body, name="ffn_bwd1", grid=(s // tm, N_CHIP),
        in_specs=[row, row, pl.BlockSpec((1, d), lambda i, j: (0, 0)),
                  pl.BlockSpec((None, f4, d), lambda i, j: (j, 0, 0)), hsp, hsp],
        out_specs=[hsp, hsp, row, pl.BlockSpec((8, d), lambda i, j: (0, 0))],
        out_shape=[hshp, hshp, jax.ShapeDtypeStruct((s, d), BF16), jax.ShapeDtypeStruct((8, d), F32)],
        compiler_params=_params("arbitrary", "arbitrary"),
    )(dx2, f, g2, wd_g, gate, up)


def _lnmod_bwd_rows(xv, gv, scv, dhv, drv, sums_ref):
    r = lax.rsqrt(jnp.mean(xv * xv, axis=-1, keepdims=True) + EPS)
    n = xv * r
    dt = dhv * (1.0 + scv)
    sums_ref[0:1, :] += jnp.sum(dhv, axis=0, keepdims=True)
    sums_ref[1:2, :] += jnp.sum(dhv * (n * gv), axis=0, keepdims=True)
    sums_ref[2:3, :] += jnp.sum(dt * n, axis=0, keepdims=True)
    dn = dt * gv
    return drv + r * (dn - n * jnp.mean(dn * n, axis=-1, keepdims=True))


def _ffn_bwd2(dgate, dup, wg_g, wu_g, x, g, sc, dres):
    _, s, f4 = dgate.shape
    d = wg_g.shape[-2]
    tm = _tile(s, 1024)

    def body(dg_ref, du_ref, wg_ref, wu_ref, x_ref, g_ref, sc_ref, dr_ref, dx_ref, sums_ref, acc):
        i, j = pl.program_id(0), pl.program_id(1)

        @pl.when((i == 0) & (j == 0))
        def _():
            sums_ref[...] = jnp.zeros_like(sums_ref)

        @pl.when(j == 0)
        def _():
            acc[...] = jnp.zeros_like(acc)

        acc[...] += _dot_nt(dg_ref[...], wg_ref[...]) + _dot_nt(du_ref[...], wu_ref[...])

        @pl.when(j == N_CHIP - 1)
        def _():
            dx_ref[...] = _lnmod_bwd_rows(x_ref[...], g_ref[...], sc_ref[...], acc[...], dr_ref[...], sums_ref)

    hsp = pl.BlockSpec((None, tm, f4), lambda i, j: (j, i, 0))
    wsp = pl.BlockSpec((None, d, f4), lambda i, j: (j, 0, 0))
    row = pl.BlockSpec((tm, d), lambda i, j: (i, 0))
    vec = pl.BlockSpec((1, d), lambda i, j: (0, 0))
    return pl.pallas_call(
        body, name="ffn_bwd2", grid=(s // tm, N_CHIP), in_specs=[hsp, hsp, wsp, wsp, row, vec, vec, row],
        out_specs=[row, pl.BlockSpec((8, d), lambda i, j: (0, 0))],
        out_shape=[jax.ShapeDtypeStruct((s, d), F32), jax.ShapeDtypeStruct((8, d), F32)],
        scratch_shapes=[pltpu.VMEM((tm, d), F32)], compiler_params=_params("arbitrary", "arbitrary"),
    )(dgate, dup, wg_g, wu_g, x, g, sc, dres)


def _lnmod_bwd(x, g, sc, dh, dres):
    s, d = x.shape
    tm = _tile(s, 1024)

    def body(x_ref, g_ref, sc_ref, dh_ref, dr_ref, dx_ref, sums_ref):
        @pl.when(pl.program_id(0) == 0)
        def _():
            sums_ref[...] = jnp.zeros_like(sums_ref)

        dx_ref[...] = _lnmod_bwd_rows(x_ref[...], g_ref[...], sc_ref[...], dh_ref[...], dr_ref[...], sums_ref)

    vec = pl.BlockSpec((1, d), lambda i: (0, 0))
    row = pl.BlockSpec((tm, d), lambda i: (i, 0))
    return pl.pallas_call(
        body, name="lnmod_bwd", grid=(s // tm,), in_specs=[row, vec, vec, row, row],
        out_specs=[row, pl.BlockSpec((8, d), lambda i: (0, 0))],
        out_shape=[jax.ShapeDtypeStruct((s, d), F32), jax.ShapeDtypeStruct((8, d), F32)],
        compiler_params=_params("arbitrary"),
    )(x, g, sc, dh, dres)


def _out_bwd(dx1, mo, g1, wout, pa, pb, p, wa, wb, d):
    s = dx1.shape[0]
    tm = _tile(s, 256)

    def body(dx_ref, mo_ref, g_ref, wo_ref, pa_ref, pb_ref, ga_ref, gb_ref, wa_ref, wb_ref,
             dmo_ref, da_ref, db_ref, dya_ref, dyb_ref, dp_ref, dg_ref):
        @pl.when(pl.program_id(0) == 0)
        def _():
            dg_ref[...] = jnp.zeros_like(dg_ref)

        dxv = dx_ref[...]
        dg_ref[0:1, :] += jnp.sum(dxv * mo_ref[...], axis=0, keepdims=True)
        dmo = (g_ref[...] * dxv).astype(BF16)
        dmo_ref[...] = dmo
        dm = _dot_nt(dmo, wo_ref[...])
        sa, sb = jax.nn.sigmoid(ga_ref[...].astype(F32)), jax.nn.sigmoid(gb_ref[...].astype(F32))
        da = (dm * sa).astype(BF16)
        db = (dm * sb).astype(BF16)
        da_ref[...] = da
        db_ref[...] = db
        dp_ref[:, :d] = (dm * pa_ref[...].astype(F32) * (sa * (1.0 - sa))).astype(BF16)
        dp_ref[:, d:] = (dm * pb_ref[...].astype(F32) * (sb * (1.0 - sb))).astype(BF16)
        dya_ref[...] = _dot_nt(da, wa_ref[...]).astype(BF16)
        dyb_ref[...] = _dot_nt(db, wb_ref[...]).astype(BF16)

    row = pl.BlockSpec((tm, d), lambda i: (i, 0))
    wsp = pl.BlockSpec((d, d), lambda i: (0, 0))
    shp = jax.ShapeDtypeStruct((s, d), BF16)
    return pl.pallas_call(
        body, name="out_bwd", grid=(s // tm,),
        in_specs=[row, row, pl.BlockSpec((1, d), lambda i: (0, 0)), wsp, row, row,
                  pl.BlockSpec((tm, d), lambda i: (i, 6)), pl.BlockSpec((tm, d), lambda i: (i, 7)), wsp, wsp],
        out_specs=[row] * 5 + [pl.BlockSpec((tm, 2 * d), lambda i: (i, 3)), pl.BlockSpec((8, d), lambda i: (0, 0))],
        out_shape=[shp] * 5 + [jax.ShapeDtypeStruct((s, 8 * d), BF16), jax.ShapeDtypeStruct((8, d), F32)],
        compiler_params=_params("arbitrary"),
    )(dx1, mo, g1, wout, pa, pb, p, p, wa, wb)


def _store_segments(outs, dp_out, sems, col_blocks):
    copies = [pltpu.make_async_copy(outs.at[k], dp_out.at[:, pl.ds(pl.multiple_of(cb * LANES, LANES), LANES)],
                                    sems.at[k]) for k, cb in enumerate(col_blocks)]
    for cp in copies:
        cp.start()
    for cp in copies:
        cp.wait()


def _conv_bwd(p, conv_w, dyb, dp, d):
    s = p.shape[0]
    nb = d // LANES
    rows_n = _conv_rows(s)

    def compute(cb_ref, cc_ref, cx_ref, w_ref, dy_ref, dcb_ref, dcc_ref, dcx_ref, dw_ref, us, ds):
        us[pl.ds(0, 8), :] = jnp.zeros((8, LANES), F32)
        ds[pl.ds(s, 8), :] = jnp.zeros((8, LANES), F32)

        def fill(r, _):
            rows = pl.ds(pl.multiple_of(r * rows_n, rows_n), rows_n)
            us[pl.ds(pl.multiple_of(r * rows_n + 8, 8), rows_n), :] = cc_ref[rows, :].astype(F32) * cx_ref[rows, :].astype(F32)
            ds[rows, :] = dy_ref[rows, :].astype(F32) * cb_ref[rows, :].astype(F32)
            return 0

        lax.fori_loop(0, s // rows_n, fill, 0)
        w = w_ref[...]

        def out(r, carry):
            dw0, dw1, dw2 = carry
            rows = pl.ds(pl.multiple_of(r * rows_n, rows_n), rows_n)
            ext = us[pl.ds(pl.multiple_of(r * rows_n, 8), rows_n + 8), :]
            u0, u1, u2 = ext[8:, :], pltpu.roll(ext, 1, 0)[8:, :], pltpu.roll(ext, 2, 0)[8:, :]
            cv = w[0:1, :] * u2 + w[1:2, :] * u1 + w[2:3, :] * u0
            dcb_ref[rows, :] = (dy_ref[rows, :].astype(F32) * cv).astype(BF16)
            nxt = ds[pl.ds(pl.multiple_of(r * rows_n, 8), rows_n + 8), :]
            e0 = nxt[:rows_n, :]
            e1 = pltpu.roll(nxt, rows_n + 7, 0)[:rows_n, :]
            e2 = pltpu.roll(nxt, rows_n + 6, 0)[:rows_n, :]
            du = w[2:3, :] * e0 + w[1:2, :] * e1 + w[0:1, :] * e2
            dcc_ref[rows, :] = (du * cx_ref[rows, :].astype(F32)).astype(BF16)
            dcx_ref[rows, :] = (du * cc_ref[rows, :].astype(F32)).astype(BF16)
            return (dw0 + jnp.sum(e0 * u2, axis=0, keepdims=True), dw1 + jnp.sum(e0 * u1, axis=0, keepdims=True),
                    dw2 + jnp.sum(e0 * u0, axis=0, keepdims=True))

        zero = jnp.zeros((1, LANES), F32)
        dw0, dw1, dw2 = lax.fori_loop(0, s // rows_n, out, (zero, zero, zero))
        dw_ref[...] = jnp.zeros_like(dw_ref)
        dw_ref[0:1, :] = dw0
        dw_ref[1:2, :] = dw1
        dw_ref[2:3, :] = dw2

    def body(cb_ref, cc_ref, cx_ref, w_ref, dy_ref, dp_in, dp_out, dw_ref, us, ds, outs, sems):
        del dp_in
        compute(cb_ref, cc_ref, cx_ref, w_ref, dy_ref, outs.at[0], outs.at[1], outs.at[2], dw_ref, us, ds)
        _store_segments(outs, dp_out, sems, [(3 + k) * nb + pl.program_id(0) for k in range(3)])

    def seg(k):
        return pl.BlockSpec((s, LANES), lambda b, k=k: (0, k * nb + b))

    return pl.pallas_call(
        body, name="conv_bwd", grid=(nb,),
        in_specs=[seg(3), seg(4), seg(5), pl.BlockSpec((3, LANES), lambda b: (0, b)),
                  pl.BlockSpec((s, LANES), lambda b: (0, b)), ANY_SPEC],
        out_specs=[ANY_SPEC, pl.BlockSpec((8, LANES), lambda b: (0, b))],
        out_shape=[jax.ShapeDtypeStruct(dp.shape, BF16), jax.ShapeDtypeStruct((8, d), F32)],
        input_output_aliases={5: 0},
        scratch_shapes=[pltpu.VMEM((s + 8, LANES), F32), pltpu.VMEM((s + 8, LANES), F32),
                        pltpu.VMEM((3, s, LANES), BF16), pltpu.SemaphoreType.DMA((3,))],
        compiler_params=_params("arbitrary"),
    )(p, p, p, conv_w, dyb, dp)


def _attn_bwd(p, qg2, kg2, dy, lt, dp, d):
    s = p.shape[0]
    n_pairs = d // LANES
    qsb = _tile(s, Q_SUPER_BWD)
    n_sub, n_sb, n_kb = qsb // Q_BLOCK, s // qsb, s // Q_BLOCK
    unroll = math.gcd(KEY_UNROLL, n_sub)
    chunk = _tile(s, 512)
    inv_sqrt = 1.0 / math.sqrt(HEAD_DIM)

    def compute(q_ref, k_ref, v_ref, qg_ref, kg_ref, dy_ref, lt_ref, dq_ref, dk_ref, dv_ref, dgain_ref,
                qs, k2, v2, dkt, dvt, qt, dyt, rem, gbef, dqa):
        low, causal, w4 = _attn_consts(True)

        def prep(r, _):
            rows = pl.ds(pl.multiple_of(r * chunk, chunk), chunk)
            qs[rows, :] = (_pair_norm(q_ref[rows, :].astype(F32), low)[0] * (qg_ref[...] * inv_sqrt)).astype(BF16)
            return 0

        lax.fori_loop(0, s // chunk, prep, 0)
        _fill_pair_blocks(k2, lambda rows: _pair_norm(k_ref[rows, :].astype(F32), low)[0] * kg_ref[...], low, n_kb)
        _fill_pair_blocks(v2, lambda rows: v_ref[rows, :].astype(F32), low, n_kb)

        def clear(b, _):
            dkt[b] = jnp.zeros((LANES, Q_BLOCK), F32)
            dvt[b] = jnp.zeros((LANES, Q_BLOCK), F32)
            return 0

        lax.fori_loop(0, n_kb, clear, 0)

        def step(sb, j, t0=0, diag_t=None):
            rows = pl.ds(pl.multiple_of(sb * qsb + t0 * Q_BLOCK, Q_BLOCK), (n_sub - t0) * Q_BLOCK)
            kj2, vj2 = k2[j], v2[j]
            z_both = _dot_nt(qs[rows, :], kj2)
            da_both = _dot_nt(dy_ref[rows, :], vj2)
            zls, cats = [], []
            for t in range(t0, n_sub):
                sub = slice((t - t0) * Q_BLOCK, (t - t0 + 1) * Q_BLOCK)
                for h in range(2):
                    z = z_both[sub, h * LANES:(h + 1) * LANES]
                    ln = _log_not(z)
                    if t == diag_t:
                        ln = jnp.where(causal, ln, 0.0)
                    zls.append(z + ln)
                    cats.append(_split_cat(ln))
            c2 = _dot(jnp.concatenate(cats, axis=0), w4)
            a_rows, gs, cats = [], [], []
            for t in range(t0, n_sub):
                sub = slice(t * Q_BLOCK, (t + 1) * Q_BLOCK)
                a_pair = []
                for h in range(2):
                    i = 2 * (t - t0) + h
                    tile = slice(i * Q_BLOCK, (i + 1) * Q_BLOCK)
                    left = rem[h, sub, :]
                    log_a = zls[i] + (left - c2[tile, :LANES])
                    if t == diag_t:
                        log_a = jnp.where(causal, log_a, -1e30)
                    a = jnp.exp(log_a)
                    rem[h, sub, :] = left - c2[tile, LANES:]
                    g = a * da_both[(t - t0) * Q_BLOCK:(t - t0 + 1) * Q_BLOCK, h * LANES:(h + 1) * LANES]
                    a_pair.append(a.astype(BF16))
                    gs.append(g)
                    cats.append(g.astype(BF16))
                a_rows.append(jnp.concatenate(a_pair, axis=1))
            c2g = _dot(jnp.concatenate(cats, axis=0), w4[:Q_BLOCK, :])
            dz_rows = []
            for t in range(t0, n_sub):
                sub = slice(t * Q_BLOCK, (t + 1) * Q_BLOCK)
                dz_pair = []
                for h in range(2):
                    i = 2 * (t - t0) + h
                    tile = slice(i * Q_BLOCK, (i + 1) * Q_BLOCK)
                    before = gbef[h, sub, :]
                    dz = gs[i] - jnp.exp(zls[i]) * (before + c2g[tile, :LANES])
                    if t == diag_t:
                        dz = jnp.where(causal, dz, 0.0)
                    gbef[h, sub, :] = before + c2g[tile, LANES:]
                    dz_pair.append(dz.astype(BF16))
                dz_rows.append(jnp.concatenate(dz_pair, axis=1))
            a_both = jnp.concatenate(a_rows, axis=0)
            dz_both = jnp.concatenate(dz_rows, axis=0)
            used = slice(t0 * Q_BLOCK, qsb)
            dvt[j] += _dot(dyt[0, :, used], a_both[:, :LANES]) + _dot(dyt[1, :, used], a_both[:, LANES:])
            dkt[j] += _dot(qt[0, :, used], dz_both[:, :LANES]) + _dot(qt[1, :, used], dz_both[:, LANES:])
            dqa[used, :] += _dot(dz_both, kj2)

        def super_block(sb, dqg):
            rows_sb = pl.ds(pl.multiple_of(sb * qsb, qsb), qsb)
            total = lt_ref[rows_sb, :]
            other = pltpu.roll(total, HEAD_DIM, 1)
            rem[0] = jnp.where(low, total, other)
            rem[1] = jnp.where(low, other, total)
            gbef[...] = jnp.zeros_like(gbef)
            dqa[...] = jnp.zeros_like(dqa)
            qv = qs[rows_sb, :].astype(F32)
            dyv = dy_ref[rows_sb, :].astype(F32)
            qt[0] = jnp.where(low, qv, 0.0).T.astype(BF16)
            qt[1] = jnp.where(low, 0.0, qv).T.astype(BF16)
            dyt[0] = jnp.where(low, dyv, 0.0).T.astype(BF16)
            dyt[1] = jnp.where(low, 0.0, dyv).T.astype(BF16)

            def below(n, _):
                for u in range(unroll):
                    step(sb, unroll * n + u)
                return 0

            lax.fori_loop(0, sb * (n_sub // unroll), below, 0)
            for t in range(n_sub):
                step(sb, sb * n_sub + t, t0=t, diag_t=t)
            qhat, r = _pair_norm(q_ref[rows_sb, :].astype(F32), low)
            dqn = dqa[...]
            dqhat = dqn * (qg_ref[...] * inv_sqrt)
            dq_ref[rows_sb, :] = (r * (dqhat - qhat * _pair_mean(dqhat * qhat, low))).astype(BF16)
            return dqg + jnp.sum(dqn * qhat, axis=0, keepdims=True) * inv_sqrt

        dqg = lax.fori_loop(0, n_sb, super_block, jnp.zeros((1, LANES), F32))

        def finish(b, dkg):
            rows = pl.ds(pl.multiple_of(b * Q_BLOCK, Q_BLOCK), Q_BLOCK)
            khat, rk = _pair_norm(k_ref[rows, :].astype(F32), low)
            dkn = dkt[b].T
            dkhat = dkn * kg_ref[...]
            dk_ref[rows, :] = (rk * (dkhat - khat * _pair_mean(dkhat * khat, low))).astype(BF16)
            dv_ref[rows, :] = dvt[b].T.astype(BF16)
            return dkg + jnp.sum(dkn * khat, axis=0, keepdims=True)

        dkg = lax.fori_loop(0, n_kb, finish, jnp.zeros((1, LANES), F32))
        dgain_ref[...] = jnp.zeros_like(dgain_ref)
        dgain_ref[0:1, :] = dqg
        dgain_ref[1:2, :] = dkg

    def body(q_ref, k_ref, v_ref, qg_ref, kg_ref, dy_ref, lt_ref, dp_in, dp_out, dgain_ref, outs, sems, *scratch):
        del dp_in
        compute(q_ref, k_ref, v_ref, qg_ref, kg_ref, dy_ref, lt_ref, outs.at[0], outs.at[1], outs.at[2], dgain_ref,
                *scratch)
        _store_segments(outs, dp_out, sems, [k * n_pairs + pl.program_id(0) for k in range(3)])

    def seg(k):
        return pl.BlockSpec((s, LANES), lambda h, k=k: (0, k * n_pairs + h))

    vec = pl.BlockSpec((1, LANES), lambda h: (0, 0))
    col = pl.BlockSpec((s, LANES), lambda h: (0, h))
    return pl.pallas_call(
        body, name="attn_bwd", grid=(n_pairs,),
        in_specs=[seg(0), seg(1), seg(2), vec, vec, col, col, ANY_SPEC],
        out_specs=[ANY_SPEC, pl.BlockSpec((None, 8, LANES), lambda h: (h, 0, 0))],
        out_shape=[jax.ShapeDtypeStruct(dp.shape, BF16), jax.ShapeDtypeStruct((n_pairs, 8, LANES), F32)],
        input_output_aliases={7: 0},
        scratch_shapes=[pltpu.VMEM((3, s, LANES), BF16), pltpu.SemaphoreType.DMA((3,)), pltpu.VMEM((s, LANES), BF16)]
        + [pltpu.VMEM((n_kb, 2 * Q_BLOCK, LANES), BF16)] * 2
        + [pltpu.VMEM((n_kb, LANES, Q_BLOCK), F32)] * 2
        + [pltpu.VMEM((2, LANES, qsb), BF16)] * 2
        + [pltpu.VMEM((2, qsb, LANES), F32)] * 2 + [pltpu.VMEM((qsb, LANES), F32)],
        compiler_params=_params("arbitrary"),
    )(p, p, p, qg2, kg2, dy, lt, dp)


def _mm_in_bwd(dp, w_g):
    s = dp.shape[0]
    d, n4 = w_g.shape[-2:]
    tm = _tile(s, 1024)

    def body(a_ref, w_ref, o_ref, acc):
        j = pl.program_id(1)

        @pl.when(j == 0)
        def _():
            acc[...] = jnp.zeros_like(acc)

        acc[...] += _dot_nt(a_ref[...], w_ref[...])

        @pl.when(j == N_CHIP - 1)
        def _():
            o_ref[...] = acc[...]

    return pl.pallas_call(
        body, name="mm_in_bwd", grid=(s // tm, N_CHIP),
        in_specs=[pl.BlockSpec((tm, n4), lambda i, j: (i, j)),
                  pl.BlockSpec((None, d, n4), lambda i, j: (j, 0, 0))],
        out_specs=pl.BlockSpec((tm, d), lambda i, j: (i, 0)), out_shape=jax.ShapeDtypeStruct((s, d), F32),
        scratch_shapes=[pltpu.VMEM((tm, d), F32)], compiler_params=_params("parallel", "arbitrary"),
    )(dp, w_g)


def _sum_adam(parts, w, m, v, name):
    n_l, r, c = w.shape
    tr = next((t for t in (256, 176, 128, 64, 32, 16) if r % t == 0 and t * c <= 256 * 1024), r)
    n_blk = r // tr

    def body(*refs):
        p_refs = refs[:n_l]
        w_ref, m_ref, v_ref, g_ref, dl_ref, nm_ref, nv_ref = refs[n_l:]
        for l in range(n_l):
            @pl.when(pl.program_id(0) == l)
            def _(p_ref=p_refs[l]):
                g = p_ref[0].astype(F32)
                for dev in range(1, N_DEV):
                    g = g + p_ref[dev].astype(F32)
                g_ref[...] = g
                delta, nm, nv = _adamw(w_ref[...], g, m_ref[...], v_ref[...])
                dl_ref[...] = delta
                nm_ref[...] = nm
                nv_ref[...] = nv

    def part_spec(l):
        return pl.BlockSpec((N_DEV, tr, c), lambda ll, i, l=l: (0, jnp.where(ll == l, i, jnp.where(ll < l, 0, n_blk - 1)), 0))

    wsp = pl.BlockSpec((None, tr, c), lambda l, i: (l, i, 0))
    shp = jax.ShapeDtypeStruct(w.shape, F32)
    return pl.pallas_call(
        body, name=name, grid=(n_l, n_blk),
        in_specs=[part_spec(l) for l in range(n_l)] + [wsp, wsp, wsp],
        out_specs=[wsp] * 4, out_shape=[shp] * 4, compiler_params=_params("arbitrary", "arbitrary"),
    )(*parts, w, m, v)


def _small_adam(parts, w, m, v):
    def body(p_ref, w_ref, m_ref, v_ref, g_ref, dl_ref, nm_ref, nv_ref):
        g = p_ref[0]
        for dev in range(1, N_DEV):
            g = g + p_ref[dev]
        g_ref[...] = g
        delta, nm, nv = _adamw(w_ref[...], g, m_ref[...], v_ref[...])
        dl_ref[...] = delta
        nm_ref[...] = nm
        nv_ref[...] = nv

    shp = jax.ShapeDtypeStruct(w.shape, F32)
    return pl.pallas_call(body, name="small_adam", in_specs=[VMEM_SPEC] * 4, out_specs=[VMEM_SPEC] * 4,
                          out_shape=[shp] * 4,
                          compiler_params=pltpu.CompilerParams(vmem_limit_bytes=VMEM_LIMIT_BYTES))(parts, w, m, v)


def _pack(vecs, mult=8 * LANES):
    flat = jnp.concatenate([a.reshape(-1).astype(F32) for a in vecs])
    pad = (-flat.shape[0]) % mult
    if pad:
        flat = jnp.concatenate([flat, jnp.zeros((pad,), F32)])
    return flat.reshape(8, -1)


def _unpack(flat, shapes):
    flat = flat.reshape(-1)
    out, off = [], 0
    for shp in shapes:
        n = math.prod(shp)
        out.append(flat[off:off + n].reshape(shp))
        off += n
    return out


BIG = ("win", "wa", "wb", "wo", "wg", "wu", "wd")
GRAD_GROUPS = (("wd", "wg", "wu"), ("wo", "wa", "wb"), ("win",))


def _local_step(x, target, mods, ln1_g, ln2_g, qg, kg, conv_w, weights, send_grads):
    s, d = x.shape
    n_l = mods.shape[0]
    saved = []
    h_in = x
    for l in range(n_l):
        sh1, sc1, g1, sh2, sc2, g2 = [mods[l, k * d:(k + 1) * d].reshape(1, d) for k in range(6)]
        qg2, kg2 = jnp.tile(qg[l:l + 1], (1, 2)), jnp.tile(kg[l:l + 1], (1, 2))
        h1 = _lnmod(h_in, ln1_g[l:l + 1], sc1, sh1)
        (win,), tie = weights(l, ("win",), h1)
        p = _mm_in(h1, win)
        ya, lt = _attn_fwd(p, qg2 + tie, kg2, d)
        yb = _conv_fwd(p, conv_w[l], d)
        (wa, wb, wo, wg, wu, wd), tie = weights(l, ("wa", "wb", "wo", "wg", "wu", "wd"), ya)
        wa, wb, wo = wa.reshape(d, d), wb.reshape(d, d), wo.reshape(d, d)
        merged, pa, pb = _branch(ya, yb, p, wa, wb, d)
        x1, mo, h2 = _out_proj(merged, wo, h_in, g1 + tie, ln2_g[l:l + 1], sc2, sh2)
        gate, up, act = _ffn_up(h2, wg, wu)
        x2, f = _ffn_down(act, wd, x1, g2)
        saved.append(dict(x0=h_in, h1=h1, p=p, ya=ya, lt=lt, yb=yb, merged=merged, pa=pa, pb=pb, x1=x1, mo=mo,
                          h2=h2, gate=gate, up=up, act=act, f=f, win=win, wa=wa, wb=wb, wo=wo, wg=wg, wu=wu, wd=wd,
                          mod=(sh1, sc1, g1, sh2, sc2, g2), qg2=qg2, kg2=kg2))
        h_in = x2

    dx, loss_tile = _loss_head(h_in, target)

    small = [None] * n_l
    for l in reversed(range(n_l)):
        sv = saved[l]
        sh1, sc1, g1, sh2, sc2, g2 = sv["mod"]
        f4, n4 = sv["wg"].shape[-1], sv["win"].shape[-1]
        hsp = lambda tk: ((tk, d), lambda j, k: (k, 0))
        fsp = lambda tk: ((None, tk, f4), lambda j, k: (j, k, 0))
        dgate, dup, df, dg2 = _ffn_bwd1(dx, sv["f"], g2, sv["wd"], sv["gate"], sv["up"])
        g_wd = _mm_tn(sv["act"], df, fsp, hsp, (f4, d), "grad_wd")
        g_wg = _mm_tn(dgate, sv["h2"], fsp, hsp, (f4, d), "grad_wg")
        g_wu = _mm_tn(dup, sv["h2"], fsp, hsp, (f4, d), "grad_wu")
        tie = send_grads(l, dict(wd=g_wd, wg=g_wg, wu=g_wu))
        dx1, sums2 = _ffn_bwd2(dgate, dup, sv["wg"], sv["wu"], sv["x1"], ln2_g[l:l + 1], sc2 + tie, dx)
        dmo, da, db, dya, dyb, dp, dg1 = _out_bwd(dx1, sv["mo"], g1, sv["wo"], sv["pa"], sv["pb"], sv["p"],
                                                        sv["wa"], sv["wb"], d)
        g_wo = _mm_tn_square(sv["merged"], dmo, "grad_wo")
        g_wa = _mm_tn_square(sv["ya"], da, "grad_wa")
        g_wb = _mm_tn_square(sv["yb"], db, "grad_wb")
        tie = send_grads(l, dict(wo=g_wo, wa=g_wa, wb=g_wb))
        dp, dconv = _conv_bwd(sv["p"], conv_w[l] + tie, dyb, dp, d)
        dp, dgain = _attn_bwd(sv["p"], sv["qg2"], sv["kg2"], dya, sv["lt"], dp, d)
        g_win = _mm_tn(sv["h1"], dp, hsp, lambda tk: ((tk, n4), lambda j, k: (k, j)), (d, n4), "grad_win")
        tie = send_grads(l, dict(win=g_win))
        dh1 = _mm_in_bwd(dp, sv["win"])
        dx, sums1 = _lnmod_bwd(sv["x0"], ln1_g[l:l + 1], sc1 + tie, dh1, dx1)
        dgain = jnp.sum(dgain[:, 0:2, :], axis=0)
        dgain = dgain[:, :HEAD_DIM] + dgain[:, HEAD_DIM:]
        dmod = jnp.concatenate([sums1[0], sums1[1], dg1[0], sums2[0], sums2[1], dg2[0]])
        small[l] = dict(dmod=dmod, ln1=sums1[2], ln2=sums2[2], qg=dgain[0], kg=dgain[1], conv=dconv[0:3])
    return loss_tile, dx, small


def kernel(x, c, ada_w, ada_b, ln1_g, w_in, q_norm_g, k_norm_g, conv_w, w_branch_a, w_branch_b, w_out, ln2_g, w_ffn_gate, w_ffn_up, w_ffn_down, loss_target, m_ada_w, m_ada_b, m_ln1_g, m_w_in, m_q_norm_g, m_k_norm_g, m_conv_w, m_w_branch_a, m_w_branch_b, m_w_out, m_ln2_g, m_w_ffn_gate, m_w_ffn_up, m_w_ffn_down, v_ada_w, v_ada_b, v_ln1_g, v_w_in, v_q_norm_g, v_k_norm_g, v_conv_w, v_w_branch_a, v_w_branch_b, v_w_out, v_ln2_g, v_w_ffn_gate, v_w_ffn_up, v_w_ffn_down):
    n_l, d, a4 = ada_w.shape
    cw4 = conv_w.shape[-1]
    ix, iy, ic = lax.axis_index("x"), lax.axis_index("y"), lax.axis_index("c")
    chip = 2 * ix + iy
    me = 2 * chip + ic

    big_w = dict(win=w_in, wa=w_branch_a, wb=w_branch_b, wo=w_out, wg=w_ffn_gate, wu=w_ffn_up, wd=w_ffn_down)
    big_m = dict(win=m_w_in, wa=m_w_branch_a, wb=m_w_branch_b, wo=m_w_out, wg=m_w_ffn_gate, wu=m_w_ffn_up,
                 wd=m_w_ffn_down)
    big_v = dict(win=v_w_in, wa=v_w_branch_a, wb=v_w_branch_b, wo=v_w_out, wg=v_w_ffn_gate, wu=v_w_ffn_up,
                 wd=v_w_ffn_down)

    def adam_view(a, k):
        return jnp.swapaxes(a, 1, 2) if k in ("wg", "wu") else a

    got = _gather8(_pack([c, conv_w])).reshape(N_DEV, -1)

    weight_groups = [(l, names) for l in range(n_l) for names in (("win",), ("wa", "wb", "wo", "wg", "wu", "wd"))]
    group_srcs = [[big_w[k][l].astype(BF16) for k in names] for l, names in weight_groups]
    started_w = {}

    def start_weights(gi):
        l, names = weight_groups[gi]
        copies = _weight_half_copies if gi == 0 else _weight_copies
        st = _split_start("weights_start_%d" % gi, copies, group_srcs[gi],
                          [(N_CHIP,) + sh.shape for sh in group_srcs[gi]], 3)
        for k in names:
            started_w[(l, k)] = [gi, names, st, None, copies]
        return st[4]

    got, group_srcs[0] = lax.optimization_barrier((got, group_srcs[0]))
    tie = start_weights(0)[0, 0]
    c_all = got[:, :d]
    conv_all = got[:, d:d + n_l * 3 * cw4].reshape(N_CHIP, 2, n_l, 3, cw4)[:, 0]
    conv_full = jnp.transpose(conv_all, (1, 2, 0, 3)).reshape(n_l, 3, N_CHIP * cw4)
    b_cols = lax.dynamic_slice_in_dim(ada_b, chip * a4, a4, axis=1).reshape(n_l, 1, a4)
    b_cols, group_srcs[1:] = lax.optimization_barrier((b_cols + tie, group_srcs[1:]))
    mod_cols = _ada_mod(c_all, ada_w, b_cols)
    mod_all = _gather8(_pack([mod_cols])).reshape(N_DEV, -1)[:, :n_l * N_DEV * a4]
    mod_all = mod_all.reshape(N_CHIP, 2, n_l, N_DEV, a4)[:, 0]
    mods = lax.dynamic_index_in_dim(mod_all, me, axis=2, keepdims=False)
    mods = jnp.transpose(mods, (1, 0, 2)).reshape(n_l, N_CHIP * a4)

    def weights(l, names, after):
        entry, tie = started_w[(l, names[0])], jnp.zeros((), F32)
        if entry[3] is None:
            lands = _split_wait("weights_wait_%d" % entry[0], entry[4], entry[2], after)
            if entry[4] is _weight_half_copies:
                passed = _split_start_in_place("weights_pass_start_%d" % entry[0], _weight_half_pass, lands, 3)
                lands = _split_wait_in_place("weights_pass_wait_%d" % entry[0], _weight_half_pass, passed, passed[3])
            nxt = entry[0] + 1
            if nxt < len(weight_groups):
                lands, group_srcs[nxt] = lax.optimization_barrier((lands, group_srcs[nxt]))
                tie = start_weights(nxt)[0, 0]
            lands = [lax.dynamic_update_index_in_dim(land, own, chip, 0) for land, own in zip(lands, entry[2][2])]
            for k in entry[1]:
                started_w[(l, k)][3] = dict(zip(entry[1], lands))
        return [started_w[(l, k)][3][k] for k in names], tie

    started_g, held_back = [], []

    def start_grads(l, grads, copies=_grad_copies, sems_per=7):
        names = tuple(grads)
        st = _split_start("grads_start_%d" % len(started_g), copies, [grads[k] for k in names],
                          [(N_DEV,) + grads[k].shape[1:] for k in names], sems_per)
        started_g.append((l, names, st, copies))
        return st[4][0, 0]

    def send_grads(l, grads):
        if l == 0 and tuple(grads) == GRAD_GROUPS[-1]:
            held_back.append(grads)
            return jnp.zeros((), F32)
        return start_grads(l, grads)

    loss_tile, grad_x, small = _local_step(
        x[0], loss_target[0], mods, ln1_g, ln2_g, q_norm_g, k_norm_g, conv_full, weights, send_grads)

    sm_shapes = [(n_l, 6 * d), (n_l, d), (n_l, d), (n_l, HEAD_DIM), (n_l, HEAD_DIM), (n_l, 3, d), (1,)]
    vec = _pack([jnp.stack([small[l][k] for l in range(n_l)]) for k in ("dmod", "ln1", "ln2", "qg", "kg", "conv")]
                + [loss_tile[0, 0:1]])
    n_vec = vec.shape[1] * 8
    all_vec = _gather8(vec).reshape(N_DEV, n_vec)
    all_vec, held_back = lax.optimization_barrier((all_vec, held_back))
    tie = sum([start_grads(0, grads, _grad_copies_same_core, 4) for grads in held_back], jnp.zeros((), F32))
    per_dev = [_unpack(all_vec[dev], sm_shapes) for dev in range(N_DEV)]
    dmod_all = jnp.stack([pd[0] for pd in per_dev])
    dmod_cols = jnp.transpose(lax.dynamic_slice_in_dim(dmod_all, chip * a4, a4, axis=2), (1, 0, 2))
    ada_out = _ada_grad_adam(jnp.transpose(c_all) + tie, dmod_cols, ada_w, m_ada_w, v_ada_w)

    dev_parts = jnp.stack([
        _pack([pd[0], pd[1], pd[2], pd[3], pd[4], lax.dynamic_slice_in_dim(pd[5], chip * cw4, cw4, axis=2), pd[6]])
        for pd in per_dev])
    zero1 = jnp.zeros((1,), F32)
    sw = _pack([ada_b, ln1_g, ln2_g, q_norm_g, k_norm_g, conv_w, zero1])
    sm = _pack([m_ada_b, m_ln1_g, m_ln2_g, m_q_norm_g, m_k_norm_g, m_conv_w, zero1])
    sv = _pack([v_ada_b, v_ln1_g, v_ln2_g, v_q_norm_g, v_k_norm_g, v_conv_w, zero1 + 1.0])
    out_shapes = [(n_l, 6 * d), (n_l, d), (n_l, d), (n_l, HEAD_DIM), (n_l, HEAD_DIM), (n_l, 3, cw4), (1,)]
    sm_out = [_unpack(o, out_shapes) for o in _small_adam(dev_parts, sw, sm, sv)]
    loss = 0.5 * sm_out[0][6][0] / d

    after = jnp.full((8, LANES), tie + sm_out[0][0][0, 0] + ada_out[0][0, 0, 0])
    big_out = {}
    for names in GRAD_GROUPS:
        got_parts = {}
        for gi, (l, sent, st, copies) in enumerate(started_g):
            if sent == names:
                parts = _split_wait("grads_wait_%d" % gi, copies, st, after)
                if copies is _grad_copies_same_core:
                    passed = _split_start_in_place("grads_pass_start_%d" % gi, _grad_pass_copies, parts, 3)
                    parts = _split_wait_in_place("grads_pass_wait_%d" % gi, _grad_pass_copies, passed, passed[3])
                for k, part, grad in zip(sent, parts, st[2]):
                    own = lax.dynamic_index_in_dim(grad, chip, 0, keepdims=False)
                    got_parts[(l, k)] = lax.dynamic_update_index_in_dim(part, own, me, 0)
        for k in names:
            res = _sum_adam([got_parts[(l, k)] for l in range(n_l)], adam_view(big_w[k], k), adam_view(big_m[k], k),
                            adam_view(big_v[k], k), "sum_adam_" + k)
            after = res[0]
            big_out[k] = [adam_view(r, k) for r in res]

    outs = [loss, grad_x[None]]
    for kind in range(4):
        sm_k = sm_out[kind]
        outs += [ada_out[kind], sm_k[0], sm_k[1], big_out["win"][kind], sm_k[3], sm_k[4], sm_k[5],
                 big_out["wa"][kind], big_out["wb"][kind], big_out["wo"][kind], sm_k[2],
                 big_out["wg"][kind], big_out["wu"][kind], big_out["wd"][kind]]
    return tuple(outs)
```

```python
import math

import jax
import jax.numpy as jnp
from jax import lax
from jax.experimental import pallas as pl
from jax.experimental.pallas import tpu as pltpu

F32 = jnp.float32
BF16 = jnp.bfloat16
MESH_ID = pl.DeviceIdType.MESH

EPS = 1e-6
HEAD_DIM = 64
Q_BLOCK = 128
Q_SUPER = 1024
Q_SUPER_BWD = 1024
KEY_UNROLL = 8
LANES = 128
N_DEV = 8
N_CHIP = 4
VMEM_LIMIT_BYTES = 56 * 1024 * 1024

ADAM_LR = 0.001
ADAM_B1 = 0.9
ADAM_B2 = 0.999
ADAM_EPS = 1e-08
ADAM_WD = 0.01
ADAM_STEP = 10

HBM_SPEC = pl.BlockSpec(memory_space=pltpu.HBM)
ANY_SPEC = pl.BlockSpec(memory_space=pl.ANY)
SEM_SPEC = pl.BlockSpec(memory_space=pltpu.SEMAPHORE)
VMEM_SPEC = pl.BlockSpec(memory_space=pltpu.VMEM)
SIDE_EFFECT = pltpu.SideEffectType.DATAFLOW_SIDE_EFFECTING


def _params(*sem):
    return pltpu.CompilerParams(dimension_semantics=tuple(sem), vmem_limit_bytes=VMEM_LIMIT_BYTES)


def _tile(n, pref):
    return pref if n % pref == 0 else n


def _dot(a, b):
    return jnp.dot(a, b, preferred_element_type=F32)


def _dot_nt(a, b):
    return lax.dot_general(a, b, (((1,), (1,)), ((), ())), preferred_element_type=F32)


def _dot_tn(a, b):
    return lax.dot_general(a, b, (((0,), (0,)), ((), ())), preferred_element_type=F32)


def _adamw(w, g, m, v):
    m = ADAM_B1 * m + (1.0 - ADAM_B1) * g
    v = ADAM_B2 * v + (1.0 - ADAM_B2) * (g * g)
    m_hat = m / (1.0 - ADAM_B1 ** ADAM_STEP)
    v_hat = v / (1.0 - ADAM_B2 ** ADAM_STEP)
    delta = -ADAM_LR * (m_hat / (jnp.sqrt(v_hat) + ADAM_EPS) + ADAM_WD * w)
    return delta, m, v


def _hbm(a):
    return pltpu.with_memory_space_constraint(a, pltpu.HBM)


def _peer(x, y, c, k):
    return (1 - x if k & 4 else x, 1 - y if k & 2 else y, 1 - c if k & 1 else c)


def _gather8(v):
    rows_per, m = v.shape

    def body(v_ref, out_ref, send_sems, recv_sems, local_sem):
        x, y, c = lax.axis_index("x"), lax.axis_index("y"), lax.axis_index("c")

        def rows(p):
            return out_ref.at[pl.ds((4 * p[0] + 2 * p[1] + p[2]) * rows_per, rows_per), :]

        me = (x, y, c)
        mine = pltpu.make_async_copy(v_ref, rows(me), local_sem)
        mine.start()
        sends = []
        for k in range(1, N_DEV):
            cp = pltpu.make_async_remote_copy(
                src_ref=v_ref, dst_ref=rows(me), send_sem=send_sems.at[k - 1], recv_sem=recv_sems.at[k - 1],
                device_id=_peer(x, y, c, k), device_id_type=MESH_ID)
            cp.start()
            sends.append(cp)
        for k in range(1, N_DEV):
            pltpu.make_async_remote_copy(
                src_ref=v_ref, dst_ref=rows(_peer(x, y, c, k)), send_sem=send_sems.at[k - 1],
                recv_sem=recv_sems.at[k - 1], device_id=_peer(x, y, c, k), device_id_type=MESH_ID).wait_recv()
        for cp in sends:
            cp.wait_send()
        mine.wait()

    return pl.pallas_call(
        body, name="gather8",
        out_shape=jax.ShapeDtypeStruct((N_DEV * rows_per, m), v.dtype),
        in_specs=[VMEM_SPEC], out_specs=VMEM_SPEC,
        scratch_shapes=[pltpu.SemaphoreType.DMA((N_DEV - 1,)), pltpu.SemaphoreType.DMA((N_DEV - 1,)),
                        pltpu.SemaphoreType.DMA],
    )(v)


def _weight_copies(srcs, lands, send_sems, recv_sems):
    x, y, c = lax.axis_index("x"), lax.axis_index("y"), lax.axis_index("c")
    chips = [(1 - x, y), (x, 1 - y), (1 - x, 1 - y)]
    sends, recvs = [], []
    for a, (src, land) in enumerate(zip(srcs, lands)):
        for j, (px, py) in enumerate(chips):
            def copy(dst_block, a=a, j=j, px=px, py=py, src=src, land=land):
                return pltpu.make_async_remote_copy(
                    src_ref=src, dst_ref=land.at[dst_block], send_sem=send_sems.at[3 * a + j],
                    recv_sem=recv_sems.at[3 * a + j], device_id=(px, py, c), device_id_type=MESH_ID)
            sends.append(copy(2 * x + y))
            recvs.append(copy(2 * px + py))
    return sends, recvs


def _weight_half_copies(srcs, lands, send_sems, recv_sems):
    x, y, c = lax.axis_index("x"), lax.axis_index("y"), lax.axis_index("c")
    chips = [(1 - x, y), (x, 1 - y), (1 - x, 1 - y)]
    sends, recvs = [], []
    for a, (src, land) in enumerate(zip(srcs, lands)):
        half = src.shape[0] // 2
        rows = pl.ds(c * half, half)
        for j, (px, py) in enumerate(chips):
            def copy(dst_block, a=a, j=j, px=px, py=py, src=src, land=land, rows=rows):
                return pltpu.make_async_remote_copy(
                    src_ref=src.at[rows], dst_ref=land.at[dst_block, rows], send_sem=send_sems.at[3 * a + j],
                    recv_sem=recv_sems.at[3 * a + j], device_id=(px, py, c), device_id_type=MESH_ID)
            sends.append(copy(2 * x + y))
            recvs.append(copy(2 * px + py))
    return sends, recvs


def _weight_half_pass(lands, same_lands, send_sems, recv_sems):
    del same_lands
    x, y, c = lax.axis_index("x"), lax.axis_index("y"), lax.axis_index("c")
    chips = [(1 - x, y), (x, 1 - y), (1 - x, 1 - y)]
    sends, recvs = [], []
    for a, land in enumerate(lands):
        half = land.shape[1] // 2
        for j, (px, py) in enumerate(chips):
            def copy(pc, a=a, j=j, px=px, py=py, land=land, half=half):
                part = land.at[2 * px + py, pl.ds(pc * half, half)]
                return pltpu.make_async_remote_copy(
                    src_ref=part, dst_ref=part, send_sem=send_sems.at[3 * a + j], recv_sem=recv_sems.at[3 * a + j],
                    device_id=(x, y, 1 - c), device_id_type=MESH_ID)
            sends.append(copy(c))
            recvs.append(copy(1 - c))
    return sends, recvs


def _split_start(name, copies, srcs, land_shapes, sems_per_src):
    n = len(srcs)

    def body(*refs):
        sends, _ = copies(refs[:n], refs[n + 2:2 * n + 2], refs[n], refs[n + 1])
        for cp in sends:
            cp.start()
        token = refs[-1]
        token[...] = jnp.zeros_like(token)

    n_sems = sems_per_src * n
    outs = pl.pallas_call(
        body, name=name,
        out_shape=(pltpu.SemaphoreType.DMA((n_sems,)), pltpu.SemaphoreType.DMA((n_sems,)),
                   *[pltpu.HBM(shape, a.dtype) for a, shape in zip(srcs, land_shapes)],
                   jax.ShapeDtypeStruct((8, LANES), F32)),
        in_specs=[HBM_SPEC] * n, out_specs=(SEM_SPEC, SEM_SPEC, *[HBM_SPEC] * n, VMEM_SPEC),
        compiler_params=pltpu.CompilerParams(has_side_effects=SIDE_EFFECT),
    )(*[_hbm(a) for a in srcs])
    return outs[0], outs[1], list(srcs), list(outs[2:2 + n]), outs[-1]


def _split_wait(name, copies, started, after):
    send_sems, recv_sems, srcs, lands, _ = started
    n = len(srcs)

    def body(*refs):
        sends, recvs = copies(refs[:n], refs[n:2 * n], refs[2 * n], refs[2 * n + 1])
        for cp in sends:
            cp.wait_send()
        for cp in recvs:
            cp.wait_recv()

    return pl.pallas_call(
        body, name=name,
        out_shape=tuple(pltpu.HBM(a.shape, a.dtype) for a in lands),
        in_specs=[HBM_SPEC] * (2 * n) + [SEM_SPEC, SEM_SPEC, ANY_SPEC], out_specs=tuple([HBM_SPEC] * n),
        input_output_aliases={n + i: i for i in range(n)},
        compiler_params=pltpu.CompilerParams(has_side_effects=SIDE_EFFECT),
    )(*srcs, *lands, send_sems, recv_sems, after)


def _split_start_in_place(name, copies, bufs, sems_per_buf):
    n = len(bufs)

    def body(*refs):
        sends, _ = copies(refs[:n], refs[:n], refs[n], refs[n + 1])
        for cp in sends:
            cp.start()
        token = refs[-1]
        token[...] = jnp.zeros_like(token)

    n_sems = sems_per_buf * n
    outs = pl.pallas_call(
        body, name=name,
        out_shape=(pltpu.SemaphoreType.DMA((n_sems,)), pltpu.SemaphoreType.DMA((n_sems,)),
                   *[pltpu.HBM(a.shape, a.dtype) for a in bufs], jax.ShapeDtypeStruct((8, LANES), F32)),
        in_specs=[HBM_SPEC] * n, out_specs=(SEM_SPEC, SEM_SPEC, *[HBM_SPEC] * n, VMEM_SPEC),
        input_output_aliases={i: 2 + i for i in range(n)},
        compiler_params=pltpu.CompilerParams(has_side_effects=SIDE_EFFECT),
    )(*[_hbm(a) for a in bufs])
    return outs[0], outs[1], list(outs[2:2 + n]), outs[-1]


def _split_wait_in_place(name, copies, started, after):
    send_sems, recv_sems, bufs, _ = started
    n = len(bufs)

    def body(*refs):
        sends, recvs = copies(refs[:n], refs[:n], refs[n], refs[n + 1])
        for cp in sends:
            cp.wait_send()
        for cp in recvs:
            cp.wait_recv()

    return pl.pallas_call(
        body, name=name,
        out_shape=tuple(pltpu.HBM(a.shape, a.dtype) for a in bufs),
        in_specs=[HBM_SPEC] * n + [SEM_SPEC, SEM_SPEC, ANY_SPEC], out_specs=tuple([HBM_SPEC] * n),
        input_output_aliases={i: i for i in range(n)},
        compiler_params=pltpu.CompilerParams(has_side_effects=SIDE_EFFECT),
    )(*bufs, send_sems, recv_sems, after)


def _grad_copies(grads, parts, send_sems, recv_sems):
    x, y, c = lax.axis_index("x"), lax.axis_index("y"), lax.axis_index("c")
    chips = [(1 - x, y), (x, 1 - y), (1 - x, 1 - y)]
    my_slot = 4 * x + 2 * y + c
    sends, recvs = [], []
    for a, (grad, part) in enumerate(zip(grads, parts)):
        def copy(k, block, slot, to, a=a, grad=grad, part=part):
            return pltpu.make_async_remote_copy(
                src_ref=grad.at[block], dst_ref=part.at[slot], send_sem=send_sems.at[7 * a + k],
                recv_sem=recv_sems.at[7 * a + k], device_id=to, device_id_type=MESH_ID)
        sends.append(copy(0, 2 * x + y, my_slot, (x, y, 1 - c)))
        recvs.append(copy(0, 2 * x + y, 4 * x + 2 * y + (1 - c), (x, y, 1 - c)))
        for j, (px, py) in enumerate(chips):
            for other, pc in enumerate((c, 1 - c)):
                sends.append(copy(1 + 2 * j + other, 2 * px + py, my_slot, (px, py, pc)))
                recvs.append(copy(1 + 2 * j + other, 2 * x + y, 4 * px + 2 * py + pc, (px, py, pc)))
    return sends, recvs


def _grad_copies_same_core(grads, parts, send_sems, recv_sems):
    x, y, c = lax.axis_index("x"), lax.axis_index("y"), lax.axis_index("c")
    chips = [(1 - x, y), (x, 1 - y), (1 - x, 1 - y)]
    my_slot = 4 * x + 2 * y + c
    sends, recvs = [], []
    for a, (grad, part) in enumerate(zip(grads, parts)):
        def copy(k, block, slot, to, a=a, grad=grad, part=part):
            return pltpu.make_async_remote_copy(
                src_ref=grad.at[block], dst_ref=part.at[slot], send_sem=send_sems.at[4 * a + k],
                recv_sem=recv_sems.at[4 * a + k], device_id=to, device_id_type=MESH_ID)
        sends.append(copy(0, 2 * x + y, my_slot, (x, y, 1 - c)))
        recvs.append(copy(0, 2 * x + y, 4 * x + 2 * y + (1 - c), (x, y, 1 - c)))
        for j, (px, py) in enumerate(chips):
            sends.append(copy(1 + j, 2 * px + py, my_slot, (px, py, c)))
            recvs.append(copy(1 + j, 2 * x + y, 4 * px + 2 * py + c, (px, py, c)))
    return sends, recvs


def _grad_pass_copies(parts, same_parts, send_sems, recv_sems):
    del same_parts
    x, y, c = lax.axis_index("x"), lax.axis_index("y"), lax.axis_index("c")
    chips = [(1 - x, y), (x, 1 - y), (1 - x, 1 - y)]
    sends, recvs = [], []
    for a, part in enumerate(parts):
        for j, (px, py) in enumerate(chips):
            def copy(pc, a=a, j=j, px=px, py=py, part=part):
                slot = part.at[4 * px + 2 * py + pc]
                return pltpu.make_async_remote_copy(
                    src_ref=slot, dst_ref=slot, send_sem=send_sems.at[3 * a + j], recv_sem=recv_sems.at[3 * a + j],
                    device_id=(x, y, 1 - c), device_id_type=MESH_ID)
            sends.append(copy(c))
            recvs.append(copy(1 - c))
    return sends, recvs


def _ada_mod(c_all, ada_w, ada_b_cols):
    n_l, d, a4 = ada_w.shape
    tn = _tile(a4, 512)

    def body(c_ref, w_ref, b_ref, o_ref):
        cv = c_ref[...]
        ca = (cv * jax.nn.sigmoid(cv)).astype(BF16)
        o_ref[...] = _dot(ca, w_ref[...].astype(BF16)) + b_ref[...]

    return pl.pallas_call(
        body, name="ada_mod", grid=(n_l, a4 // tn),
        in_specs=[pl.BlockSpec((N_DEV, d), lambda l, j: (0, 0)),
                  pl.BlockSpec((None, d, tn), lambda l, j: (l, 0, j)),
                  pl.BlockSpec((None, 1, tn), lambda l, j: (l, 0, j))],
        out_specs=pl.BlockSpec((None, N_DEV, tn), lambda l, j: (l, 0, j)),
        out_shape=jax.ShapeDtypeStruct((n_l, N_DEV, a4), F32),
        compiler_params=_params("parallel", "parallel"),
    )(c_all, ada_w, ada_b_cols)


def _ada_grad_adam(c_all_t, dmod_cols, w, m, v):
    n_l, d, a4 = w.shape
    tn = _tile(a4, 512)

    def body(ct_ref, dm_ref, w_ref, m_ref, v_ref, g_ref, dl_ref, nm_ref, nv_ref):
        ct = ct_ref[...]
        ca = ct * jax.nn.sigmoid(ct)
        dm = dm_ref[...]
        g = ca[:, 0:1] * dm[0:1, :]
        for dev in range(1, N_DEV):
            g = g + ca[:, dev:dev + 1] * dm[dev:dev + 1, :]
        g_ref[...] = g
        delta, nm, nv = _adamw(w_ref[...], g, m_ref[...], v_ref[...])
        dl_ref[...] = delta
        nm_ref[...] = nm
        nv_ref[...] = nv

    wspec = pl.BlockSpec((None, d, tn), lambda l, j: (l, 0, j))
    shp = jax.ShapeDtypeStruct(w.shape, F32)
    return pl.pallas_call(
        body, name="ada_grad_adam", grid=(n_l, a4 // tn),
        in_specs=[pl.BlockSpec((d, N_DEV), lambda l, j: (0, 0)),
                  pl.BlockSpec((None, N_DEV, tn), lambda l, j: (l, 0, j)), wspec, wspec, wspec],
        out_specs=[wspec] * 4, out_shape=[shp] * 4,
        compiler_params=_params("parallel", "parallel"),
    )(c_all_t, dmod_cols, w, m, v)


def _lnmod(x, g, sc, sh):
    s, d = x.shape
    tm = _tile(s, 1024)

    def body(x_ref, g_ref, sc_ref, sh_ref, h_ref):
        xv = x_ref[...]
        r = lax.rsqrt(jnp.mean(xv * xv, axis=-1, keepdims=True) + EPS)
        h_ref[...] = ((xv * r * g_ref[...]) * (1.0 + sc_ref[...]) + sh_ref[...]).astype(BF16)

    vec = pl.BlockSpec((1, d), lambda i: (0, 0))
    row = pl.BlockSpec((tm, d), lambda i: (i, 0))
    return pl.pallas_call(
        body, name="lnmod", grid=(s // tm,), in_specs=[row, vec, vec, vec], out_specs=row,
        out_shape=jax.ShapeDtypeStruct((s, d), BF16), compiler_params=_params("parallel"),
    )(x, g, sc, sh)


def _mm_in(h, w_g):
    s, d = h.shape
    n4 = w_g.shape[-1]
    tm = _tile(s, 1024)

    def body(a_ref, b_ref, o_ref):
        o_ref[...] = _dot(a_ref[...], b_ref[...]).astype(BF16)

    return pl.pallas_call(
        body, name="mm_in", grid=(N_CHIP, s // tm),
        in_specs=[pl.BlockSpec((tm, d), lambda j, i: (i, 0)),
                  pl.BlockSpec((None, d, n4), lambda j, i: (j, 0, 0))],
        out_specs=pl.BlockSpec((tm, n4), lambda j, i: (i, j)),
        out_shape=jax.ShapeDtypeStruct((s, N_CHIP * n4), BF16),
        compiler_params=_params("parallel", "parallel"),
    )(h, w_g)


def _pair_mean(x, low):
    lo = jnp.sum(jnp.where(low, x, 0.0), axis=-1, keepdims=True)
    hi = jnp.sum(jnp.where(low, 0.0, x), axis=-1, keepdims=True)
    return jnp.where(low, lo, hi) * (1.0 / HEAD_DIM)


def _pair_norm(x, low):
    r = lax.rsqrt(_pair_mean(x * x, low) + EPS)
    return x * r, r


def _log_not(z):
    nz = -z
    return jnp.minimum(nz, 0.0) - jnp.log(1.0 + jnp.exp(jnp.minimum(z, nz)))


def _attn_consts(inclusive):
    low = lax.broadcasted_iota(jnp.int32, (1, LANES), 1) < HEAD_DIM
    row = lax.broadcasted_iota(jnp.int32, (Q_BLOCK, Q_BLOCK), 0)
    col = lax.broadcasted_iota(jnp.int32, (Q_BLOCK, Q_BLOCK), 1)
    tri = (row <= col) if inclusive else (row > col)
    w2 = jnp.concatenate([tri.astype(BF16), jnp.ones((Q_BLOCK, Q_BLOCK), BF16)], axis=1)
    return low, col < row, jnp.concatenate([w2, w2], axis=0)


def _split_cat(v):
    hi = v.astype(BF16)
    return jnp.concatenate([hi, (v - hi.astype(F32)).astype(BF16)], axis=1)


def _fill_pair_blocks(dst, src_fn, low, n_kb):
    def fill(b, _):
        v = src_fn(pl.ds(pl.multiple_of(b * Q_BLOCK, Q_BLOCK), Q_BLOCK))
        dst[b, 0:Q_BLOCK, :] = jnp.where(low, v, 0.0).astype(BF16)
        dst[b, Q_BLOCK:2 * Q_BLOCK, :] = jnp.where(low, 0.0, v).astype(BF16)
        return 0

    lax.fori_loop(0, n_kb, fill, 0)


def _attn_fwd(p, qg2, kg2, d):
    s = p.shape[0]
    n_pairs = d // LANES
    qsb = _tile(s, Q_SUPER)
    n_sub, n_sb, n_kb = qsb // Q_BLOCK, s // qsb, s // Q_BLOCK
    unroll = math.gcd(KEY_UNROLL, n_sub)
    chunk = _tile(s, 512)
    inv_sqrt = 1.0 / math.sqrt(HEAD_DIM)

    def body(q_ref, k_ref, v_ref, qg_ref, kg_ref, o_ref, lt_ref, qs, k2, v2, run, acc):
        low, causal, w4 = _attn_consts(False)

        def prep(r, _):
            rows = pl.ds(pl.multiple_of(r * chunk, chunk), chunk)
            qs[rows, :] = (_pair_norm(q_ref[rows, :].astype(F32), low)[0] * (qg_ref[...] * inv_sqrt)).astype(BF16)
            return 0

        lax.fori_loop(0, s // chunk, prep, 0)
        _fill_pair_blocks(k2, lambda rows: _pair_norm(k_ref[rows, :].astype(F32), low)[0] * kg_ref[...], low, n_kb)
        _fill_pair_blocks(v2, lambda rows: v_ref[rows, :].astype(F32), low, n_kb)

        def step(sb, j, t0=0, diag_t=None):
            rows = pl.ds(pl.multiple_of(sb * qsb + t0 * Q_BLOCK, Q_BLOCK), (n_sub - t0) * Q_BLOCK)
            z_both = _dot_nt(qs[rows, :], k2[j])
            zls, cats = [], []
            for t in range(t0, n_sub):
                sub = slice((t - t0) * Q_BLOCK, (t - t0 + 1) * Q_BLOCK)
                for h in range(2):
                    z = z_both[sub, h * LANES:(h + 1) * LANES]
                    ln = _log_not(z)
                    if t == diag_t:
                        ln = jnp.where(causal, ln, 0.0)
                    zls.append(z + ln)
                    cats.append(_split_cat(ln))
            c2 = _dot(jnp.concatenate(cats, axis=0), w4)
            a_rows = []
            for t in range(t0, n_sub):
                sub = slice(t * Q_BLOCK, (t + 1) * Q_BLOCK)
                a_pair = []
                for h in range(2):
                    i = 2 * (t - t0) + h
                    tile = slice(i * Q_BLOCK, (i + 1) * Q_BLOCK)
                    later = run[h, sub, :]
                    log_a = zls[i] + c2[tile, :LANES] + later
                    if t == diag_t:
                        log_a = jnp.where(causal, log_a, -1e30)
                    a_pair.append(jnp.exp(log_a).astype(BF16))
                    run[h, sub, :] = later + c2[tile, LANES:]
                a_rows.append(jnp.concatenate(a_pair, axis=1))
            acc[t0 * Q_BLOCK:, :] += _dot(jnp.concatenate(a_rows, axis=0), v2[j])

        def super_block(sb, _):
            run[...] = jnp.zeros_like(run)
            acc[...] = jnp.zeros_like(acc)
            for t in reversed(range(n_sub)):
                step(sb, sb * n_sub + t, t0=t, diag_t=t)

            def below(n, _):
                for u in range(unroll):
                    step(sb, sb * n_sub - 1 - (unroll * n + u))
                return 0

            lax.fori_loop(0, sb * (n_sub // unroll), below, 0)
            rows_sb = pl.ds(pl.multiple_of(sb * qsb, qsb), qsb)
            o_ref[rows_sb, :] = acc[...].astype(BF16)
            lt_ref[rows_sb, :] = jnp.where(low, run[0], run[1])
            return 0

        lax.fori_loop(0, n_sb, super_block, 0)

    def seg(k):
        return pl.BlockSpec((s, LANES), lambda h, k=k: (0, k * n_pairs + h))

    vec = pl.BlockSpec((1, LANES), lambda h: (0, 0))
    out = pl.BlockSpec((s, LANES), lambda h: (0, h))
    return pl.pallas_call(
        body, name="attn_fwd", grid=(n_pairs,),
        in_specs=[seg(0), seg(1), seg(2), vec, vec], out_specs=[out, out],
        out_shape=[jax.ShapeDtypeStruct((s, d), BF16), jax.ShapeDtypeStruct((s, d), F32)],
        scratch_shapes=[pltpu.VMEM((s, LANES), BF16)] + [pltpu.VMEM((n_kb, 2 * Q_BLOCK, LANES), BF16)] * 2
        + [pltpu.VMEM((2, qsb, LANES), F32), pltpu.VMEM((qsb, LANES), F32)],
        compiler_params=_params("parallel"),
    )(p, p, p, qg2, kg2)


def _conv_rows(s):
    return _tile(s, 512)


def _conv_fwd(p, conv_w, d):
    s = p.shape[0]
    nb = d // LANES
    rows_n = _conv_rows(s)

    def body(cb_ref, cc_ref, cx_ref, w_ref, y_ref, us):
        us[pl.ds(0, 8), :] = jnp.zeros((8, LANES), F32)

        def fill(r, _):
            rows = pl.ds(pl.multiple_of(r * rows_n, rows_n), rows_n)
            us[pl.ds(pl.multiple_of(r * rows_n + 8, 8), rows_n), :] = cc_ref[rows, :].astype(F32) * cx_ref[rows, :].astype(F32)
            return 0

        lax.fori_loop(0, s // rows_n, fill, 0)
        w = w_ref[...]

        def out(r, _):
            rows = pl.ds(pl.multiple_of(r * rows_n, rows_n), rows_n)
            ext = us[pl.ds(pl.multiple_of(r * rows_n, 8), rows_n + 8), :]
            cv = (w[0:1, :] * pltpu.roll(ext, 2, 0)[8:, :] + w[1:2, :] * pltpu.roll(ext, 1, 0)[8:, :]
                  + w[2:3, :] * ext[8:, :])
            y_ref[rows, :] = (cb_ref[rows, :].astype(F32) * cv).astype(BF16)
            return 0

        lax.fori_loop(0, s // rows_n, out, 0)

    def seg(k):
        return pl.BlockSpec((s, LANES), lambda b, k=k: (0, k * nb + b))

    return pl.pallas_call(
        body, name="conv_fwd", grid=(nb,),
        in_specs=[seg(3), seg(4), seg(5), pl.BlockSpec((3, LANES), lambda b: (0, b))],
        out_specs=pl.BlockSpec((s, LANES), lambda b: (0, b)),
        out_shape=jax.ShapeDtypeStruct((s, d), BF16),
        scratch_shapes=[pltpu.VMEM((s + 8, LANES), F32)],
        compiler_params=_params("parallel"),
    )(p, p, p, conv_w)


def _branch(ya, yb, p, wa, wb, d):
    s = ya.shape[0]
    tm = _tile(s, 1024)

    def body(ya_ref, yb_ref, ga_ref, gb_ref, wa_ref, wb_ref, m_ref, a_ref, b_ref):
        pa = _dot(ya_ref[...], wa_ref[...])
        pb = _dot(yb_ref[...], wb_ref[...])
        ga, gb = ga_ref[...].astype(F32), gb_ref[...].astype(F32)
        m_ref[...] = (jax.nn.sigmoid(ga) * pa + jax.nn.sigmoid(gb) * pb).astype(BF16)
        a_ref[...] = pa.astype(BF16)
        b_ref[...] = pb.astype(BF16)

    row = pl.BlockSpec((tm, d), lambda i: (i, 0))
    wsp = pl.BlockSpec((d, d), lambda i: (0, 0), pipeline_mode=pl.Buffered(1))
    shp = jax.ShapeDtypeStruct((s, d), BF16)
    return pl.pallas_call(
        body, name="branch", grid=(s // tm,),
        in_specs=[row, row, pl.BlockSpec((tm, d), lambda i: (i, 6)), pl.BlockSpec((tm, d), lambda i: (i, 7)), wsp, wsp],
        out_specs=[row, row, row], out_shape=[shp, shp, shp], compiler_params=_params("parallel"),
    )(ya, yb, p, p, wa, wb)


def _out_proj(merged, wout, x0, g1, ln_g, sc, sh):
    s, d = x0.shape
    tm = _tile(s, 1024)

    def body(m_ref, w_ref, x_ref, g_ref, lg_ref, sc_ref, sh_ref, x1_ref, mo_ref, h_ref):
        mo = _dot(m_ref[...], w_ref[...])
        mo_ref[...] = mo
        x1 = x_ref[...] + g_ref[...] * mo
        x1_ref[...] = x1
        r = lax.rsqrt(jnp.mean(x1 * x1, axis=-1, keepdims=True) + EPS)
        h_ref[...] = ((x1 * r * lg_ref[...]) * (1.0 + sc_ref[...]) + sh_ref[...]).astype(BF16)

    row = pl.BlockSpec((tm, d), lambda i: (i, 0))
    vec = pl.BlockSpec((1, d), lambda i: (0, 0))
    shp = jax.ShapeDtypeStruct((s, d), F32)
    return pl.pallas_call(
        body, name="out_proj", grid=(s // tm,),
        in_specs=[row, pl.BlockSpec((d, d), lambda i: (0, 0)), row, vec, vec, vec, vec],
        out_specs=[row, row, row], out_shape=[shp, shp, jax.ShapeDtypeStruct((s, d), BF16)],
        compiler_params=_params("parallel"),
    )(merged, wout, x0, g1, ln_g, sc, sh)


def _ffn_up(h, wg_g, wu_g):
    s, d = h.shape
    f4 = wg_g.shape[-1]
    tm = _tile(s, 1024)

    def body(h_ref, wg_ref, wu_ref, gate_ref, up_ref, act_ref):
        hv = h_ref[...]
        gt = _dot(hv, wg_ref[...])
        up = _dot(hv, wu_ref[...])
        gate_ref[...] = gt.astype(BF16)
        up_ref[...] = up.astype(BF16)
        act_ref[...] = (gt * jax.nn.sigmoid(gt) * up).astype(BF16)

    wsp = pl.BlockSpec((None, d, f4), lambda j, i: (j, 0, 0))
    osp = pl.BlockSpec((None, tm, f4), lambda j, i: (j, i, 0))
    shp = jax.ShapeDtypeStruct((N_CHIP, s, f4), BF16)
    return pl.pallas_call(
        body, name="ffn_up", grid=(N_CHIP, s // tm),
        in_specs=[pl.BlockSpec((tm, d), lambda j, i: (i, 0)), wsp, wsp],
        out_specs=[osp, osp, osp], out_shape=[shp, shp, shp], compiler_params=_params("parallel", "parallel"),
    )(h, wg_g, wu_g)


def _ffn_down(act, wd_g, x1, g2):
    s, d = x1.shape
    f4 = act.shape[-1]
    tm = _tile(s, 1024)

    def body(a_ref, w_ref, x_ref, g_ref, x2_ref, f_ref, acc):
        j = pl.program_id(1)

        @pl.when(j == 0)
        def _():
            acc[...] = jnp.zeros_like(acc)

        acc[...] += _dot(a_ref[...], w_ref[...])

        @pl.when(j == N_CHIP - 1)
        def _():
            f = acc[...]
            f_ref[...] = f
            x2_ref[...] = x_ref[...] + g_ref[...] * f

    row = pl.BlockSpec((tm, d), lambda i, j: (i, 0))
    shp = jax.ShapeDtypeStruct((s, d), F32)
    return pl.pallas_call(
        body, name="ffn_down", grid=(s // tm, N_CHIP),
        in_specs=[pl.BlockSpec((None, tm, f4), lambda i, j: (j, i, 0)),
                  pl.BlockSpec((None, f4, d), lambda i, j: (j, 0, 0)),
                  row, pl.BlockSpec((1, d), lambda i, j: (0, 0))],
        out_specs=[row, row], out_shape=[shp, shp],
        scratch_shapes=[pltpu.VMEM((tm, d), F32)], compiler_params=_params("parallel", "arbitrary"),
    )(act, wd_g, x1, g2)


def _loss_head(y, target):
    s, d = y.shape
    tm = _tile(s, 1024)
    n_steps = s // tm

    def body(y_ref, t_ref, dy_ref, l_ref, acc):
        i = pl.program_id(0)

        @pl.when(i == 0)
        def _():
            acc[...] = jnp.zeros_like(acc)

        err = y_ref[...] - t_ref[...]
        dy_ref[...] = err / d
        acc[...] += jnp.sum(err * err, axis=0, keepdims=True)

        @pl.when(i == n_steps - 1)
        def _():
            l_ref[...] = jnp.broadcast_to(jnp.sum(acc[...], axis=1, keepdims=True), (8, LANES))

    row = pl.BlockSpec((tm, d), lambda i: (i, 0))
    return pl.pallas_call(
        body, name="loss_head", grid=(n_steps,), in_specs=[row, row],
        out_specs=[row, pl.BlockSpec((8, LANES), lambda i: (0, 0))],
        out_shape=[jax.ShapeDtypeStruct((s, d), F32), jax.ShapeDtypeStruct((8, LANES), F32)],
        scratch_shapes=[pltpu.VMEM((1, d), F32)], compiler_params=_params("arbitrary"),
    )(y, target)


def _mm_tn(a, b, a_spec, b_spec, out_rc, name):
    r, c = out_rc
    s = a.shape[-2]
    tk = _tile(s, 1024)
    nk = s // tk

    def body(a_ref, b_ref, o_ref, acc):
        k = pl.program_id(1)

        @pl.when(k == 0)
        def _():
            acc[...] = jnp.zeros_like(acc)

        acc[...] += _dot_tn(a_ref[...], b_ref[...])

        @pl.when(k == nk - 1)
        def _():
            o_ref[...] = acc[...].astype(BF16)

    return pl.pallas_call(
        body, name=name, grid=(N_CHIP, nk),
        in_specs=[pl.BlockSpec(*a_spec(tk)), pl.BlockSpec(*b_spec(tk))],
        out_specs=pl.BlockSpec((None, r, c), lambda j, k: (j, 0, 0)),
        out_shape=jax.ShapeDtypeStruct((N_CHIP, r, c), BF16),
        scratch_shapes=[pltpu.VMEM((r, c), F32)], compiler_params=_params("parallel", "arbitrary"),
    )(a, b)


def _mm_tn_square(a, b, name):
    s, d = a.shape
    r4 = d // N_CHIP
    tk = _tile(s, 1024)
    nk = s // tk

    def body(a_ref, b_ref, o_ref, acc):
        k = pl.program_id(0)

        @pl.when(k == 0)
        def _():
            acc[...] = jnp.zeros_like(acc)

        acc[...] += _dot_tn(a_ref[...], b_ref[...])

        @pl.when(k == nk - 1)
        def _():
            for j in range(N_CHIP):
                o_ref[j] = acc[j * r4:(j + 1) * r4, :].astype(BF16)

    blk = pl.BlockSpec((tk, d), lambda k: (k, 0))
    return pl.pallas_call(
        body, name=name, grid=(nk,), in_specs=[blk, blk],
        out_specs=pl.BlockSpec((N_CHIP, r4, d), lambda k: (0, 0, 0)),
        out_shape=jax.ShapeDtypeStruct((N_CHIP, r4, d), BF16),
        scratch_shapes=[pltpu.VMEM((d, d), F32)], compiler_params=_params("arbitrary"),
    )(a, b)


def _ffn_bwd1(dx2, f, g2, wd_g, gate, up):
    s, d = dx2.shape
    f4 = gate.shape[-1]
    tm = _tile(s, 1024)

    def body(dx_ref, f_ref, g_ref, w_ref, gate_ref, up_ref, dgate_ref, dup_ref, df_ref, dg_ref):
        i, j = pl.program_id(0), pl.program_id(1)

        @pl.when((i == 0) & (j == 0))
        def _():
            dg_ref[...] = jnp.zeros_like(dg_ref)

        dxv = dx_ref[...]
        df = (g_ref[...] * dxv).astype(BF16)

        @pl.when(j == 0)
        def _():
            df_ref[...] = df
            dg_ref[0:1, :] += jnp.sum(dxv * f_ref[...], axis=0, keepdims=True)

        da = _dot_nt(df, w_ref[...])
        gt = gate_ref[...].astype(F32)
        sg = jax.nn.sigmoid(gt)
        dup_ref[...] = (da * gt * sg).astype(BF16)
        dgate_ref[...] = (da * up_ref[...].astype(F32) * (sg * (1.0 + gt * (1.0 - sg)))).astype(BF16)

    row = pl.BlockSpec((tm, d), lambda i, j: (i, 0))
    hsp = pl.BlockSpec((None, tm, f4), lambda i, j: (j, i, 0))
    hshp = jax.ShapeDtypeStruct((N_CHIP, s, f4), BF16)
    return pl.pallas_call(
        body, name="ffn_bwd1", grid=(s // tm, N_CHIP),
        in_specs=[row, row, pl.BlockSpec((1, d), lambda i, j: (0, 0)),
                  pl.BlockSpec((None, f4, d), lambda i, j: (j, 0, 0)), hsp, hsp],
        out_specs=[hsp, hsp, row, pl.BlockSpec((8, d), lambda i, j: (0, 0))],
        out_shape=[hshp, hshp, jax.ShapeDtypeStruct((s, d), BF16), jax.ShapeDtypeStruct((8, d), F32)],
        compiler_params=_params("arbitrary", "arbitrary"),
    )(dx2, f, g2, wd_g, gate, up)


def _lnmod_bwd_rows(xv, gv, scv, dhv, drv, sums_ref):
    r = lax.rsqrt(jnp.mean(xv * xv, axis=-1, keepdims=True) + EPS)
    n = xv * r
    dt = dhv * (1.0 + scv)
    sums_ref[0:1, :] += jnp.sum(dhv, axis=0, keepdims=True)
    sums_ref[1:2, :] += jnp.sum(dhv * (n * gv), axis=0, keepdims=True)
    sums_ref[2:3, :] += jnp.sum(dt * n, axis=0, keepdims=True)
    dn = dt * gv
    return drv + r * (dn - n * jnp.mean(dn * n, axis=-1, keepdims=True))


def _ffn_bwd2(dgate, dup, wg_g, wu_g, x, g, sc, dres):
    _, s, f4 = dgate.shape
    d = wg_g.shape[-2]
    tm = _tile(s, 1024)

    def body(dg_ref, du_ref, wg_ref, wu_ref, x_ref, g_ref, sc_ref, dr_ref, dx_ref, sums_ref, acc):
        i, j = pl.program_id(0), pl.program_id(1)

        @pl.when((i == 0) & (j == 0))
        def _():
            sums_ref[...] = jnp.zeros_like(sums_ref)

        @pl.when(j == 0)
        def _():
            acc[...] = jnp.zeros_like(acc)

        acc[...] += _dot_nt(dg_ref[...], wg_ref[...]) + _dot_nt(du_ref[...], wu_ref[...])

        @pl.when(j == N_CHIP - 1)
        def _():
            dx_ref[...] = _lnmod_bwd_rows(x_ref[...], g_ref[...], sc_ref[...], acc[...], dr_ref[...], sums_ref)

    hsp = pl.BlockSpec((None, tm, f4), lambda i, j: (j, i, 0))
    wsp = pl.BlockSpec((None, d, f4), lambda i, j: (j, 0, 0))
    row = pl.BlockSpec((tm, d), lambda i, j: (i, 0))
    vec = pl.BlockSpec((1, d), lambda i, j: (0, 0))
    return pl.pallas_call(
        body, name="ffn_bwd2", grid=(s // tm, N_CHIP), in_specs=[hsp, hsp, wsp, wsp, row, vec, vec, row],
        out_specs=[row, pl.BlockSpec((8, d), lambda i, j: (0, 0))],
        out_shape=[jax.ShapeDtypeStruct((s, d), F32), jax.ShapeDtypeStruct((8, d), F32)],
        scratch_shapes=[pltpu.VMEM((tm, d), F32)], compiler_params=_params("arbitrary", "arbitrary"),
    )(dgate, dup, wg_g, wu_g, x, g, sc, dres)


def _lnmod_bwd(x, g, sc, dh, dres):
    s, d = x.shape
    tm = _tile(s, 1024)

    def body(x_ref, g_ref, sc_ref, dh_ref, dr_ref, dx_ref, sums_ref):
        @pl.when(pl.program_id(0) == 0)
        def _():
            sums_ref[...] = jnp.zeros_like(sums_ref)

        dx_ref[...] = _lnmod_bwd_rows(x_ref[...], g_ref[...], sc_ref[...], dh_ref[...], dr_ref[...], sums_ref)

    vec = pl.BlockSpec((1, d), lambda i: (0, 0))
    row = pl.BlockSpec((tm, d), lambda i: (i, 0))
    return pl.pallas_call(
        body, name="lnmod_bwd", grid=(s // tm,), in_specs=[row, vec, vec, row, row],
        out_specs=[row, pl.BlockSpec((8, d), lambda i: (0, 0))],
        out_shape=[jax.ShapeDtypeStruct((s, d), F32), jax.ShapeDtypeStruct((8, d), F32)],
        compiler_params=_params("arbitrary"),
    )(x, g, sc, dh, dres)


def _out_bwd(dx1, mo, g1, wout, pa, pb, p, wa, wb, d):
    s = dx1.shape[0]
    tm = _tile(s, 512)

    def body(dx_ref, mo_ref, g_ref, wo_ref, pa_ref, pb_ref, ga_ref, gb_ref, wa_ref, wb_ref,
             dmo_ref, da_ref, db_ref, dya_ref, dyb_ref, dp_ref, dg_ref):
        @pl.when(pl.program_id(0) == 0)
        def _():
            dg_ref[...] = jnp.zeros_like(dg_ref)

        dxv = dx_ref[...]
        dg_ref[0:1, :] += jnp.sum(dxv * mo_ref[...], axis=0, keepdims=True)
        dmo = (g_ref[...] * dxv).astype(BF16)
        dmo_ref[...] = dmo
        dm = _dot_nt(dmo, wo_ref[...])
        sa, sb = jax.nn.sigmoid(ga_ref[...].astype(F32)), jax.nn.sigmoid(gb_ref[...].astype(F32))
        da = (dm * sa).astype(BF16)
        db = (dm * sb).astype(BF16)
        da_ref[...] = da
        db_ref[...] = db
        dp_ref[:, :d] = (dm * pa_ref[...].astype(F32) * (sa * (1.0 - sa))).astype(BF16)
        dp_ref[:, d:] = (dm * pb_ref[...].astype(F32) * (sb * (1.0 - sb))).astype(BF16)
        dya_ref[...] = _dot_nt(da, wa_ref[...]).astype(BF16)
        dyb_ref[...] = _dot_nt(db, wb_ref[...]).astype(BF16)

    row = pl.BlockSpec((tm, d), lambda i: (i, 0))
    wsp = pl.BlockSpec((d, d), lambda i: (0, 0), pipeline_mode=pl.Buffered(1))
    shp = jax.ShapeDtypeStruct((s, d), BF16)
    return pl.pallas_call(
        body, name="out_bwd", grid=(s // tm,),
        in_specs=[row, row, pl.BlockSpec((1, d), lambda i: (0, 0)), wsp, row, row,
                  pl.BlockSpec((tm, d), lambda i: (i, 6)), pl.BlockSpec((tm, d), lambda i: (i, 7)), wsp, wsp],
        out_specs=[row] * 5 + [pl.BlockSpec((tm, 2 * d), lambda i: (i, 3)), pl.BlockSpec((8, d), lambda i: (0, 0))],
        out_shape=[shp] * 5 + [jax.ShapeDtypeStruct((s, 8 * d), BF16), jax.ShapeDtypeStruct((8, d), F32)],
        compiler_params=_params("arbitrary"),
    )(dx1, mo, g1, wout, pa, pb, p, p, wa, wb)


def _store_segments(outs, dp_out, sems, col_blocks):
    copies = [pltpu.make_async_copy(outs.at[k], dp_out.at[:, pl.ds(pl.multiple_of(cb * LANES, LANES), LANES)],
                                    sems.at[k]) for k, cb in enumerate(col_blocks)]
    for cp in copies:
        cp.start()
    for cp in copies:
        cp.wait()


def _conv_bwd(p, conv_w, dyb, dp, d):
    s = p.shape[0]
    nb = d // LANES
    rows_n = _conv_rows(s)

    def compute(cb_ref, cc_ref, cx_ref, w_ref, dy_ref, dcb_ref, dcc_ref, dcx_ref, dw_ref, us, ds):
        us[pl.ds(0, 8), :] = jnp.zeros((8, LANES), F32)
        ds[pl.ds(s, 8), :] = jnp.zeros((8, LANES), F32)

        def fill(r, _):
            rows = pl.ds(pl.multiple_of(r * rows_n, rows_n), rows_n)
            us[pl.ds(pl.multiple_of(r * rows_n + 8, 8), rows_n), :] = cc_ref[rows, :].astype(F32) * cx_ref[rows, :].astype(F32)
            ds[rows, :] = dy_ref[rows, :].astype(F32) * cb_ref[rows, :].astype(F32)
            return 0

        lax.fori_loop(0, s // rows_n, fill, 0)
        w = w_ref[...]

        def out(r, carry):
            dw0, dw1, dw2 = carry
            rows = pl.ds(pl.multiple_of(r * rows_n, rows_n), rows_n)
            ext = us[pl.ds(pl.multiple_of(r * rows_n, 8), rows_n + 8), :]
            u0, u1, u2 = ext[8:, :], pltpu.roll(ext, 1, 0)[8:, :], pltpu.roll(ext, 2, 0)[8:, :]
            cv = w[0:1, :] * u2 + w[1:2, :] * u1 + w[2:3, :] * u0
            dcb_ref[rows, :] = (dy_ref[rows, :].astype(F32) * cv).astype(BF16)
            nxt = ds[pl.ds(pl.multiple_of(r * rows_n, 8), rows_n + 8), :]
            e0 = nxt[:rows_n, :]
            e1 = pltpu.roll(nxt, rows_n + 7, 0)[:rows_n, :]
            e2 = pltpu.roll(nxt, rows_n + 6, 0)[:rows_n, :]
            du = w[2:3, :] * e0 + w[1:2, :] * e1 + w[0:1, :] * e2
            dcc_ref[rows, :] = (du * cx_ref[rows, :].astype(F32)).astype(BF16)
            dcx_ref[rows, :] = (du * cc_ref[rows, :].astype(F32)).astype(BF16)
            return (dw0 + jnp.sum(e0 * u2, axis=0, keepdims=True), dw1 + jnp.sum(e0 * u1, axis=0, keepdims=True),
                    dw2 + jnp.sum(e0 * u0, axis=0, keepdims=True))

        zero = jnp.zeros((1, LANES), F32)
        dw0, dw1, dw2 = lax.fori_loop(0, s // rows_n, out, (zero, zero, zero))
        dw_ref[...] = jnp.zeros_like(dw_ref)
        dw_ref[0:1, :] = dw0
        dw_ref[1:2, :] = dw1
        dw_ref[2:3, :] = dw2

    def body(cb_ref, cc_ref, cx_ref, w_ref, dy_ref, dp_in, dp_out, dw_ref, us, ds, outs, sems):
        del dp_in
        compute(cb_ref, cc_ref, cx_ref, w_ref, dy_ref, outs.at[0], outs.at[1], outs.at[2], dw_ref, us, ds)
        _store_segments(outs, dp_out, sems, [(3 + k) * nb + pl.program_id(0) for k in range(3)])

    def seg(k):
        return pl.BlockSpec((s, LANES), lambda b, k=k: (0, k * nb + b))

    return pl.pallas_call(
        body, name="conv_bwd", grid=(nb,),
        in_specs=[seg(3), seg(4), seg(5), pl.BlockSpec((3, LANES), lambda b: (0, b)),
                  pl.BlockSpec((s, LANES), lambda b: (0, b)), ANY_SPEC],
        out_specs=[ANY_SPEC, pl.BlockSpec((8, LANES), lambda b: (0, b))],
        out_shape=[jax.ShapeDtypeStruct(dp.shape, BF16), jax.ShapeDtypeStruct((8, d), F32)],
        input_output_aliases={5: 0},
        scratch_shapes=[pltpu.VMEM((s + 8, LANES), F32), pltpu.VMEM((s + 8, LANES), F32),
                        pltpu.VMEM((3, s, LANES), BF16), pltpu.SemaphoreType.DMA((3,))],
        compiler_params=_params("arbitrary"),
    )(p, p, p, conv_w, dyb, dp)


def _attn_bwd(p, qg2, kg2, dy, lt, dp, d):
    s = p.shape[0]
    n_pairs = d // LANES
    qsb = _tile(s, Q_SUPER_BWD)
    n_sub, n_sb, n_kb = qsb // Q_BLOCK, s // qsb, s // Q_BLOCK
    unroll = math.gcd(KEY_UNROLL, n_sub)
    chunk = _tile(s, 512)
    inv_sqrt = 1.0 / math.sqrt(HEAD_DIM)

    def compute(q_ref, k_ref, v_ref, qg_ref, kg_ref, dy_ref, lt_ref, dq_ref, dk_ref, dv_ref, dgain_ref,
                qs, k2, v2, dkt, dvt, qt, dyt, rem, gbef, dqa):
        low, causal, w4 = _attn_consts(True)

        def prep(r, _):
            rows = pl.ds(pl.multiple_of(r * chunk, chunk), chunk)
            qs[rows, :] = (_pair_norm(q_ref[rows, :].astype(F32), low)[0] * (qg_ref[...] * inv_sqrt)).astype(BF16)
            return 0

        lax.fori_loop(0, s // chunk, prep, 0)
        _fill_pair_blocks(k2, lambda rows: _pair_norm(k_ref[rows, :].astype(F32), low)[0] * kg_ref[...], low, n_kb)
        _fill_pair_blocks(v2, lambda rows: v_ref[rows, :].astype(F32), low, n_kb)

        def clear(b, _):
            dkt[b] = jnp.zeros((LANES, Q_BLOCK), F32)
            dvt[b] = jnp.zeros((LANES, Q_BLOCK), F32)
            return 0

        lax.fori_loop(0, n_kb, clear, 0)

        def step(sb, j, t0=0, diag_t=None):
            rows = pl.ds(pl.multiple_of(sb * qsb + t0 * Q_BLOCK, Q_BLOCK), (n_sub - t0) * Q_BLOCK)
            kj2, vj2 = k2[j], v2[j]
            z_both = _dot_nt(qs[rows, :], kj2)
            da_both = _dot_nt(dy_ref[rows, :], vj2)
            zls, cats = [], []
            for t in range(t0, n_sub):
                sub = slice((t - t0) * Q_BLOCK, (t - t0 + 1) * Q_BLOCK)
                for h in range(2):
                    z = z_both[sub, h * LANES:(h + 1) * LANES]
                    ln = _log_not(z)
                    if t == diag_t:
                        ln = jnp.where(causal, ln, 0.0)
                    zls.append(z + ln)
                    cats.append(_split_cat(ln))
            c2 = _dot(jnp.concatenate(cats, axis=0), w4)
            a_rows, gs, cats = [], [], []
            for t in range(t0, n_sub):
                sub = slice(t * Q_BLOCK, (t + 1) * Q_BLOCK)
                a_pair = []
                for h in range(2):
                    i = 2 * (t - t0) + h
                    tile = slice(i * Q_BLOCK, (i + 1) * Q_BLOCK)
                    left = rem[h, sub, :]
                    log_a = zls[i] + (left - c2[tile, :LANES])
                    if t == diag_t:
                        log_a = jnp.where(causal, log_a, -1e30)
                    a = jnp.exp(log_a)
                    rem[h, sub, :] = left - c2[tile, LANES:]
                    g = a * da_both[(t - t0) * Q_BLOCK:(t - t0 + 1) * Q_BLOCK, h * LANES:(h + 1) * LANES]
                    a_pair.append(a.astype(BF16))
                    gs.append(g)
                    cats.append(g.astype(BF16))
                a_rows.append(jnp.concatenate(a_pair, axis=1))
            c2g = _dot(jnp.concatenate(cats, axis=0), w4[:Q_BLOCK, :])
            dz_rows = []
            for t in range(t0, n_sub):
                sub = slice(t * Q_BLOCK, (t + 1) * Q_BLOCK)
                dz_pair = []
                for h in range(2):
                    i = 2 * (t - t0) + h
                    tile = slice(i * Q_BLOCK, (i + 1) * Q_BLOCK)
                    before = gbef[h, sub, :]
                    dz = gs[i] - jnp.exp(zls[i]) * (before + c2g[tile, :LANES])
                    if t == diag_t:
                        dz = jnp.where(causal, dz, 0.0)
                    gbef[h, sub, :] = before + c2g[tile, LANES:]
                    dz_pair.append(dz.astype(BF16))
                dz_rows.append(jnp.concatenate(dz_pair, axis=1))
            a_both = jnp.concatenate(a_rows, axis=0)
            dz_both = jnp.concatenate(dz_rows, axis=0)
            used = slice(t0 * Q_BLOCK, qsb)
            dvt[j] += _dot(dyt[0, :, used], a_both[:, :LANES]) + _dot(dyt[1, :, used], a_both[:, LANES:])
            dkt[j] += _dot(qt[0, :, used], dz_both[:, :LANES]) + _dot(qt[1, :, used], dz_both[:, LANES:])
            dqa[used, :] += _dot(dz_both, kj2)

        def super_block(sb, dqg):
            rows_sb = pl.ds(pl.multiple_of(sb * qsb, qsb), qsb)
            total = lt_ref[rows_sb, :]
            other = pltpu.roll(total, HEAD_DIM, 1)
            rem[0] = jnp.where(low, total, other)
            rem[1] = jnp.where(low, other, total)
            gbef[...] = jnp.zeros_like(gbef)
            dqa[...] = jnp.zeros_like(dqa)
            qv = qs[rows_sb, :].astype(F32)
            dyv = dy_ref[rows_sb, :].astype(F32)
            qt[0] = jnp.where(low, qv, 0.0).T.astype(BF16)
            qt[1] = jnp.where(low, 0.0, qv).T.astype(BF16)
            dyt[0] = jnp.where(low, dyv, 0.0).T.astype(BF16)
            dyt[1] = jnp.where(low, 0.0, dyv).T.astype(BF16)

            def below(n, _):
                for u in range(unroll):
                    step(sb, unroll * n + u)
                return 0

            lax.fori_loop(0, sb * (n_sub // unroll), below, 0)
            for t in range(n_sub):
                step(sb, sb * n_sub + t, t0=t, diag_t=t)
            qhat, r = _pair_norm(q_ref[rows_sb, :].astype(F32), low)
            dqn = dqa[...]
            dqhat = dqn * (qg_ref[...] * inv_sqrt)
            dq_ref[rows_sb, :] = (r * (dqhat - qhat * _pair_mean(dqhat * qhat, low))).astype(BF16)
            return dqg + jnp.sum(dqn * qhat, axis=0, keepdims=True) * inv_sqrt

        dqg = lax.fori_loop(0, n_sb, super_block, jnp.zeros((1, LANES), F32))

        def finish(b, dkg):
            rows = pl.ds(pl.multiple_of(b * Q_BLOCK, Q_BLOCK), Q_BLOCK)
            khat, rk = _pair_norm(k_ref[rows, :].astype(F32), low)
            dkn = dkt[b].T
            dkhat = dkn * kg_ref[...]
            dk_ref[rows, :] = (rk * (dkhat - khat * _pair_mean(dkhat * khat, low))).astype(BF16)
            dv_ref[rows, :] = dvt[b].T.astype(BF16)
            return dkg + jnp.sum(dkn * khat, axis=0, keepdims=True)

        dkg = lax.fori_loop(0, n_kb, finish, jnp.zeros((1, LANES), F32))
        dgain_ref[...] = jnp.zeros_like(dgain_ref)
        dgain_ref[0:1, :] = dqg
        dgain_ref[1:2, :] = dkg

    def body(q_ref, k_ref, v_ref, qg_ref, kg_ref, dy_ref, lt_ref, dp_in, dp_out, dgain_ref, outs, sems, *scratch):
        del dp_in
        compute(q_ref, k_ref, v_ref, qg_ref, kg_ref, dy_ref, lt_ref, outs.at[0], outs.at[1], outs.at[2], dgain_ref,
                *scratch)
        _store_segments(outs, dp_out, sems, [k * n_pairs + pl.program_id(0) for k in range(3)])

    def seg(k):
        return pl.BlockSpec((s, LANES), lambda h, k=k: (0, k * n_pairs + h))

    vec = pl.BlockSpec((1, LANES), lambda h: (0, 0))
    col = pl.BlockSpec((s, LANES), lambda h: (0, h))
    return pl.pallas_call(
        body, name="attn_bwd", grid=(n_pairs,),
        in_specs=[seg(0), seg(1), seg(2), vec, vec, col, col, ANY_SPEC],
        out_specs=[ANY_SPEC, pl.BlockSpec((None, 8, LANES), lambda h: (h, 0, 0))],
        out_shape=[jax.ShapeDtypeStruct(dp.shape, BF16), jax.ShapeDtypeStruct((n_pairs, 8, LANES), F32)],
        input_output_aliases={7: 0},
        scratch_shapes=[pltpu.VMEM((3, s, LANES), BF16), pltpu.SemaphoreType.DMA((3,)), pltpu.VMEM((s, LANES), BF16)]
        + [pltpu.VMEM((n_kb, 2 * Q_BLOCK, LANES), BF16)] * 2
        + [pltpu.VMEM((n_kb, LANES, Q_BLOCK), F32)] * 2
        + [pltpu.VMEM((2, LANES, qsb), BF16)] * 2
        + [pltpu.VMEM((2, qsb, LANES), F32)] * 2 + [pltpu.VMEM((qsb, LANES), F32)],
        compiler_params=_params("arbitrary"),
    )(p, p, p, qg2, kg2, dy, lt, dp)


def _mm_in_bwd(dp, w_g):
    s = dp.shape[0]
    d, n4 = w_g.shape[-2:]
    tm = _tile(s, 1024)

    def body(a_ref, w_ref, o_ref, acc):
        j = pl.program_id(1)

        @pl.when(j == 0)
        def _():
            acc[...] = jnp.zeros_like(acc)

        acc[...] += _dot_nt(a_ref[...], w_ref[...])

        @pl.when(j == N_CHIP - 1)
        def _():
            o_ref[...] = acc[...]

    return pl.pallas_call(
        body, name="mm_in_bwd", grid=(s // tm, N_CHIP),
        in_specs=[pl.BlockSpec((tm, n4), lambda i, j: (i, j)),
                  pl.BlockSpec((None, d, n4), lambda i, j: (j, 0, 0))],
        out_specs=pl.BlockSpec((tm, d), lambda i, j: (i, 0)), out_shape=jax.ShapeDtypeStruct((s, d), F32),
        scratch_shapes=[pltpu.VMEM((tm, d), F32)], compiler_params=_params("parallel", "arbitrary"),
    )(dp, w_g)


def _sum_adam(parts, w, m, v, name):
    n_l, r, c = w.shape
    tr = next((t for t in (256, 176, 128, 64, 32, 16) if r % t == 0 and t * c <= 256 * 1024), r)
    n_blk = r // tr

    def body(*refs):
        p_refs = refs[:n_l]
        w_ref, m_ref, v_ref, g_ref, dl_ref, nm_ref, nv_ref = refs[n_l:]
        for l in range(n_l):
            @pl.when(pl.program_id(0) == l)
            def _(p_ref=p_refs[l]):
                g = p_ref[0].astype(F32)
                for dev in range(1, N_DEV):
                    g = g + p_ref[dev].astype(F32)
                g_ref[...] = g
                delta, nm, nv = _adamw(w_ref[...], g, m_ref[...], v_ref[...])
                dl_ref[...] = delta
                nm_ref[...] = nm
                nv_ref[...] = nv

    def part_spec(l):
        return pl.BlockSpec((N_DEV, tr, c), lambda ll, i, l=l: (0, jnp.where(ll == l, i, jnp.where(ll < l, 0, n_blk - 1)), 0))

    wsp = pl.BlockSpec((None, tr, c), lambda l, i: (l, i, 0))
    shp = jax.ShapeDtypeStruct(w.shape, F32)
    return pl.pallas_call(
        body, name=name, grid=(n_l, n_blk),
        in_specs=[part_spec(l) for l in range(n_l)] + [wsp, wsp, wsp],
        out_specs=[wsp] * 4, out_shape=[shp] * 4, compiler_params=_params("arbitrary", "arbitrary"),
    )(*parts, w, m, v)


def _small_adam(parts, w, m, v):
    def body(p_ref, w_ref, m_ref, v_ref, g_ref, dl_ref, nm_ref, nv_ref):
        g = p_ref[0]
        for dev in range(1, N_DEV):
            g = g + p_ref[dev]
        g_ref[...] = g
        delta, nm, nv = _adamw(w_ref[...], g, m_ref[...], v_ref[...])
        dl_ref[...] = delta
        nm_ref[...] = nm
        nv_ref[...] = nv

    shp = jax.ShapeDtypeStruct(w.shape, F32)
    return pl.pallas_call(body, name="small_adam", in_specs=[VMEM_SPEC] * 4, out_specs=[VMEM_SPEC] * 4,
                          out_shape=[shp] * 4,
                          compiler_params=pltpu.CompilerParams(vmem_limit_bytes=VMEM_LIMIT_BYTES))(parts, w, m, v)


def _pack(vecs, mult=8 * LANES):
    flat = jnp.concatenate([a.reshape(-1).astype(F32) for a in vecs])
    pad = (-flat.shape[0]) % mult
    if pad:
        flat = jnp.concatenate([flat, jnp.zeros((pad,), F32)])
    return flat.reshape(8, -1)


def _unpack(flat, shapes):
    flat = flat.reshape(-1)
    out, off = [], 0
    for shp in shapes:
        n = math.prod(shp)
        out.append(flat[off:off + n].reshape(shp))
        off += n
    return out


BIG = ("win", "wa", "wb", "wo", "wg", "wu", "wd")
GRAD_GROUPS = (("wd", "wg", "wu"), ("wo", "wa", "wb"), ("win",))


def _local_step(x, target, mods, ln1_g, ln2_g, qg, kg, conv_w, weights, send_grads):
    s, d = x.shape
    n_l = mods.shape[0]
    saved = []
    h_in = x
    for l in range(n_l):
        sh1, sc1, g1, sh2, sc2, g2 = [mods[l, k * d:(k + 1) * d].reshape(1, d) for k in range(6)]
        qg2, kg2 = jnp.tile(qg[l:l + 1], (1, 2)), jnp.tile(kg[l:l + 1], (1, 2))
        h1 = _lnmod(h_in, ln1_g[l:l + 1], sc1, sh1)
        (win,), tie = weights(l, ("win",), h1)
        p = _mm_in(h1, win)
        ya, lt = _attn_fwd(p, qg2 + tie, kg2, d)
        yb = _conv_fwd(p, conv_w[l], d)
        (wa, wb, wo, wg, wu, wd), tie = weights(l, ("wa", "wb", "wo", "wg", "wu", "wd"), ya)
        wa, wb, wo = wa.reshape(d, d), wb.reshape(d, d), wo.reshape(d, d)
        merged, pa, pb = _branch(ya, yb, p, wa, wb, d)
        x1, mo, h2 = _out_proj(merged, wo, h_in, g1 + tie, ln2_g[l:l + 1], sc2, sh2)
        gate, up, act = _ffn_up(h2, wg, wu)
        x2, f = _ffn_down(act, wd, x1, g2)
        saved.append(dict(x0=h_in, h1=h1, p=p, ya=ya, lt=lt, yb=yb, merged=merged, pa=pa, pb=pb, x1=x1, mo=mo,
                          h2=h2, gate=gate, up=up, act=act, f=f, win=win, wa=wa, wb=wb, wo=wo, wg=wg, wu=wu, wd=wd,
                          mod=(sh1, sc1, g1, sh2, sc2, g2), qg2=qg2, kg2=kg2))
        h_in = x2

    dx, loss_tile = _loss_head(h_in, target)

    small = [None] * n_l
    for l in reversed(range(n_l)):
        sv = saved[l]
        sh1, sc1, g1, sh2, sc2, g2 = sv["mod"]
        f4, n4 = sv["wg"].shape[-1], sv["win"].shape[-1]
        hsp = lambda tk: ((tk, d), lambda j, k: (k, 0))
        fsp = lambda tk: ((None, tk, f4), lambda j, k: (j, k, 0))
        dgate, dup, df, dg2 = _ffn_bwd1(dx, sv["f"], g2, sv["wd"], sv["gate"], sv["up"])
        g_wd = _mm_tn(sv["act"], df, fsp, hsp, (f4, d), "grad_wd")
        g_wg = _mm_tn(dgate, sv["h2"], fsp, hsp, (f4, d), "grad_wg")
        g_wu = _mm_tn(dup, sv["h2"], fsp, hsp, (f4, d), "grad_wu")
        tie = send_grads(l, dict(wd=g_wd, wg=g_wg, wu=g_wu))
        dx1, sums2 = _ffn_bwd2(dgate, dup, sv["wg"], sv["wu"], sv["x1"], ln2_g[l:l + 1], sc2 + tie, dx)
        dmo, da, db, dya, dyb, dp, dg1 = _out_bwd(dx1, sv["mo"], g1, sv["wo"], sv["pa"], sv["pb"], sv["p"],
                                                        sv["wa"], sv["wb"], d)
        g_wo = _mm_tn_square(sv["merged"], dmo, "grad_wo")
        g_wa = _mm_tn_square(sv["ya"], da, "grad_wa")
        g_wb = _mm_tn_square(sv["yb"], db, "grad_wb")
        tie = send_grads(l, dict(wo=g_wo, wa=g_wa, wb=g_wb))
        dp, dconv = _conv_bwd(sv["p"], conv_w[l] + tie, dyb, dp, d)
        dp, dgain = _attn_bwd(sv["p"], sv["qg2"], sv["kg2"], dya, sv["lt"], dp, d)
        g_win = _mm_tn(sv["h1"], dp, hsp, lambda tk: ((tk, n4), lambda j, k: (k, j)), (d, n4), "grad_win")
        tie = send_grads(l, dict(win=g_win))
        dh1 = _mm_in_bwd(dp, sv["win"])
        dx, sums1 = _lnmod_bwd(sv["x0"], ln1_g[l:l + 1], sc1 + tie, dh1, dx1)
        dgain = jnp.sum(dgain[:, 0:2, :], axis=0)
        dgain = dgain[:, :HEAD_DIM] + dgain[:, HEAD_DIM:]
        dmod = jnp.concatenate([sums1[0], sums1[1], dg1[0], sums2[0], sums2[1], dg2[0]])
        small[l] = dict(dmod=dmod, ln1=sums1[2], ln2=sums2[2], qg=dgain[0], kg=dgain[1], conv=dconv[0:3])
    return loss_tile, dx, small


def kernel(x, c, ada_w, ada_b, ln1_g, w_in, q_norm_g, k_norm_g, conv_w, w_branch_a, w_branch_b, w_out, ln2_g, w_ffn_gate, w_ffn_up, w_ffn_down, loss_target, m_ada_w, m_ada_b, m_ln1_g, m_w_in, m_q_norm_g, m_k_norm_g, m_conv_w, m_w_branch_a, m_w_branch_b, m_w_out, m_ln2_g, m_w_ffn_gate, m_w_ffn_up, m_w_ffn_down, v_ada_w, v_ada_b, v_ln1_g, v_w_in, v_q_norm_g, v_k_norm_g, v_conv_w, v_w_branch_a, v_w_branch_b, v_w_out, v_ln2_g, v_w_ffn_gate, v_w_ffn_up, v_w_ffn_down):
    n_l, d, a4 = ada_w.shape
    cw4 = conv_w.shape[-1]
    ix, iy, ic = lax.axis_index("x"), lax.axis_index("y"), lax.axis_index("c")
    chip = 2 * ix + iy
    me = 2 * chip + ic

    big_w = dict(win=w_in, wa=w_branch_a, wb=w_branch_b, wo=w_out, wg=w_ffn_gate, wu=w_ffn_up, wd=w_ffn_down)
    big_m = dict(win=m_w_in, wa=m_w_branch_a, wb=m_w_branch_b, wo=m_w_out, wg=m_w_ffn_gate, wu=m_w_ffn_up,
                 wd=m_w_ffn_down)
    big_v = dict(win=v_w_in, wa=v_w_branch_a, wb=v_w_branch_b, wo=v_w_out, wg=v_w_ffn_gate, wu=v_w_ffn_up,
                 wd=v_w_ffn_down)

    def adam_view(a, k):
        return jnp.swapaxes(a, 1, 2) if k in ("wg", "wu") else a

    got = _gather8(_pack([c, conv_w])).reshape(N_DEV, -1)

    weight_groups = [(l, names) for l in range(n_l) for names in (("win",), ("wa", "wb", "wo", "wg", "wu", "wd"))]
    group_srcs = [[big_w[k][l].astype(BF16) for k in names] for l, names in weight_groups]
    started_w = {}

    def start_weights(gi):
        l, names = weight_groups[gi]
        copies = _weight_half_copies if gi == 0 else _weight_copies
        st = _split_start("weights_start_%d" % gi, copies, group_srcs[gi],
                          [(N_CHIP,) + sh.shape for sh in group_srcs[gi]], 3)
        for k in names:
            started_w[(l, k)] = [gi, names, st, None, copies]
        return st[4]

    got, group_srcs[0] = lax.optimization_barrier((got, group_srcs[0]))
    tie = start_weights(0)[0, 0]
    c_all = got[:, :d]
    conv_all = got[:, d:d + n_l * 3 * cw4].reshape(N_CHIP, 2, n_l, 3, cw4)[:, 0]
    conv_full = jnp.transpose(conv_all, (1, 2, 0, 3)).reshape(n_l, 3, N_CHIP * cw4)
    b_cols = lax.dynamic_slice_in_dim(ada_b, chip * a4, a4, axis=1).reshape(n_l, 1, a4)
    b_cols, group_srcs[1:] = lax.optimization_barrier((b_cols + tie, group_srcs[1:]))
    mod_cols = _ada_mod(c_all, ada_w, b_cols)
    mod_all = _gather8(_pack([mod_cols])).reshape(N_DEV, -1)[:, :n_l * N_DEV * a4]
    mod_all = mod_all.reshape(N_CHIP, 2, n_l, N_DEV, a4)[:, 0]
    mods = lax.dynamic_index_in_dim(mod_all, me, axis=2, keepdims=False)
    mods = jnp.transpose(mods, (1, 0, 2)).reshape(n_l, N_CHIP * a4)

    def weights(l, names, after):
        entry, tie = started_w[(l, names[0])], jnp.zeros((), F32)
        if entry[3] is None:
            lands = _split_wait("weights_wait_%d" % entry[0], entry[4], entry[2], after)
            if entry[4] is _weight_half_copies:
                passed = _split_start_in_place("weights_pass_start_%d" % entry[0], _weight_half_pass, lands, 3)
                lands = _split_wait_in_place("weights_pass_wait_%d" % entry[0], _weight_half_pass, passed, passed[3])
            nxt = entry[0] + 1
            if nxt < len(weight_groups):
                lands, group_srcs[nxt] = lax.optimization_barrier((lands, group_srcs[nxt]))
                tie = start_weights(nxt)[0, 0]
            lands = [lax.dynamic_update_index_in_dim(land, own, chip, 0) for land, own in zip(lands, entry[2][2])]
            for k in entry[1]:
                started_w[(l, k)][3] = dict(zip(entry[1], lands))
        return [started_w[(l, k)][3][k] for k in names], tie

    started_g, held_back = [], []

    def start_grads(l, grads, copies=_grad_copies, sems_per=7):
        names = tuple(grads)
        st = _split_start("grads_start_%d" % len(started_g), copies, [grads[k] for k in names],
                          [(N_DEV,) + grads[k].shape[1:] for k in names], sems_per)
        started_g.append((l, names, st, copies))
        return st[4][0, 0]

    def send_grads(l, grads):
        if l == 0 and tuple(grads) == GRAD_GROUPS[-1]:
            held_back.append(grads)
            return jnp.zeros((), F32)
        return start_grads(l, grads)

    loss_tile, grad_x, small = _local_step(
        x[0], loss_target[0], mods, ln1_g, ln2_g, q_norm_g, k_norm_g, conv_full, weights, send_grads)

    sm_shapes = [(n_l, 6 * d), (n_l, d), (n_l, d), (n_l, HEAD_DIM), (n_l, HEAD_DIM), (n_l, 3, d), (1,)]
    vec = _pack([jnp.stack([small[l][k] for l in range(n_l)]) for k in ("dmod", "ln1", "ln2", "qg", "kg", "conv")]
                + [loss_tile[0, 0:1]])
    n_vec = vec.shape[1] * 8
    all_vec = _gather8(vec).reshape(N_DEV, n_vec)
    all_vec, held_back = lax.optimization_barrier((all_vec, held_back))
    tie = sum([start_grads(0, grads, _grad_copies_same_core, 4) for grads in held_back], jnp.zeros((), F32))
    per_dev = [_unpack(all_vec[dev], sm_shapes) for dev in range(N_DEV)]
    dmod_all = jnp.stack([pd[0] for pd in per_dev])
    dmod_cols = jnp.transpose(lax.dynamic_slice_in_dim(dmod_all, chip * a4, a4, axis=2), (1, 0, 2))
    ada_out = _ada_grad_adam(jnp.transpose(c_all) + tie, dmod_cols, ada_w, m_ada_w, v_ada_w)

    dev_parts = jnp.stack([
        _pack([pd[0], pd[1], pd[2], pd[3], pd[4], lax.dynamic_slice_in_dim(pd[5], chip * cw4, cw4, axis=2), pd[6]])
        for pd in per_dev])
    zero1 = jnp.zeros((1,), F32)
    sw = _pack([ada_b, ln1_g, ln2_g, q_norm_g, k_norm_g, conv_w, zero1])
    sm = _pack([m_ada_b, m_ln1_g, m_ln2_g, m_q_norm_g, m_k_norm_g, m_conv_w, zero1])
    sv = _pack([v_ada_b, v_ln1_g, v_ln2_g, v_q_norm_g, v_k_norm_g, v_conv_w, zero1 + 1.0])
    out_shapes = [(n_l, 6 * d), (n_l, d), (n_l, d), (n_l, HEAD_DIM), (n_l, HEAD_DIM), (n_l, 3, cw4), (1,)]
    sm_out = [_unpack(o, out_shapes) for o in _small_adam(dev_parts, sw, sm, sv)]
    loss = 0.5 * sm_out[0][6][0] / d

    after = jnp.full((8, LANES), tie + sm_out[0][0][0, 0] + ada_out[0][0, 0, 0])
    big_out = {}
    for names in GRAD_GROUPS:
        got_parts = {}
        for gi, (l, sent, st, copies) in enumerate(started_g):
            if sent == names:
                parts = _split_wait("grads_wait_%d" % gi, copies, st, after)
                if copies is _grad_copies_same_core:
                    passed = _split_start_in_place("grads_pass_start_%d" % gi, _grad_pass_copies, parts, 3)
                    parts = _split_wait_in_place("grads_pass_wait_%d" % gi, _grad_pass_copies, passed, passed[3])
                for k, part, grad in zip(sent, parts, st[2]):
                    own = lax.dynamic_index_in_dim(grad, chip, 0, keepdims=False)
                    got_parts[(l, k)] = lax.dynamic_update_index_in_dim(part, own, me, 0)
        for k in names:
            res = _sum_adam([got_parts[(l, k)] for l in range(n_l)], adam_view(big_w[k], k), adam_view(big_m[k], k),
                            adam_view(big_v[k], k), "sum_adam_" + k)
            after = res[0]
            big_out[k] = [adam_view(r, k) for r in res]

    outs = [loss, grad_x[None]]
    for kind in range(4):
        sm_k = sm_out[kind]
        outs += [ada_out[kind], sm_k[0], sm_k[1], big_out["win"][kind], sm_k[3], sm_k[4], sm_k[5],
                 big_out["wa"][kind], big_out["wb"][kind], big_out["wo"][kind], sm_k[2],
                 big_out["wg"][kind], big_out["wu"][kind], big_out["wd"][kind]]
    return tuple(outs)
```
